```python
import math
import jax, jax.numpy as jnp
from jax import lax
import numpy as np

D_MODEL = 1024
BATCH = 8
SEQ = 2048
DEPTH = 1

CTX_LEN = 256
GRID_W = 64
CONV_WIDTH = D_MODEL // 2
CONV_GROUPS = 8
SSD_WIDTH = D_MODEL - CONV_WIDTH
SSD_HEADDIM = 64
SSD_HEADS = SSD_WIDTH // SSD_HEADDIM
SSD_GROUPS = 2
SSD_STATE = 128
SSD_CHUNK = 128
N_DIRS = 2
D_FF = 4 * D_MODEL
LN_EPS = 1e-5
RMS_EPS = 1e-5

SSD_GN = SSD_GROUPS * SSD_STATE
XBC_DIM = SSD_WIDTH + 2 * SSD_GN
SSD_TAIL = XBC_DIM + N_DIRS * SSD_HEADS
Z_OFF = 3 * CONV_WIDTH
XBC_OFF = Z_OFF + SSD_WIDTH
IN_DIM = XBC_OFF + SSD_TAIL

kernel_name = "hybrid_conv_ssd_dit_block"


def layer_norm(x, g, b):
    xf = x.astype(jnp.float32)
    mu = jnp.mean(xf, -1, keepdims=True)
    var = jnp.mean(jnp.square(xf - mu), -1, keepdims=True)
    return ((xf - mu) * lax.rsqrt(var + LN_EPS) * g + b).astype(x.dtype)


def dwconv3(x, w, axis):
    n = x.shape[axis]
    pad = [(0, 0)] * x.ndim
    pad[axis] = (1, 1)
    xp = jnp.pad(x, pad)
    sl = lambda s: lax.slice_in_dim(xp, s, s + n, axis=axis)
    return sl(0) * w[0] + sl(1) * w[1] + sl(2) * w[2]


def token_conv(x, w, is_grid):
    if is_grid:
        bsz, l, ch = x.shape
        rows = l // GRID_W
        return dwconv3(x.reshape(bsz, rows, GRID_W, ch), w, 2).reshape(bsz, l, ch)
    return dwconv3(x, w, 1)


def segsum(a):
    t = a.shape[-1]
    cs = jnp.cumsum(a, -1)
    diff = cs[..., :, None] - cs[..., None, :]
    mask = jnp.tril(jnp.ones((t, t), dtype=bool))
    return jnp.where(mask, diff, -jnp.inf)


def ssd_chunked(x, dt, a, bmat, cmat, h0):
    bsz, l, h, p = x.shape
    nc = l // SSD_CHUNK
    xdt = (x * dt[..., None]).reshape(bsz, nc, SSD_CHUNK, h, p)
    bc = bmat.reshape(bsz, nc, SSD_CHUNK, h, SSD_STATE)
    cc = cmat.reshape(bsz, nc, SSD_CHUNK, h, SSD_STATE)
    adt = jnp.moveaxis((dt * a).reshape(bsz, nc, SSD_CHUNK, h), -1, 1)
    acs = jnp.cumsum(adt, -1)
    lmat = jnp.exp(segsum(adt))
    scores = jnp.einsum('bclhn,bcshn->bhcls', cc, bc) * lmat
    y_diag = jnp.einsum('bhcls,bcshp->bclhp', scores, xdt)
    decay_states = jnp.exp(acs[..., -1:] - acs)
    local = jnp.einsum('bclhn,bhcl,bclhp->bchpn', bc, decay_states, xdt)
    states_in = jnp.concatenate([h0[:, None], local[:, :-1]], axis=1)
    a_last = jnp.pad(acs[..., -1][..., :-1], ((0, 0), (0, 0), (1, 0)))
    decay_chunk = jnp.exp(segsum(a_last))
    prev = jnp.einsum('bhzc,bchpn->bzhpn', decay_chunk, states_in)
    y_off = jnp.einsum('bclhn,bchpn,bhcl->bclhp', cc, prev, jnp.exp(acs))
    return (y_diag + y_off).reshape(bsz, l, h, p)


def ssd_final_state(x, dt, a, bmat):
    acs = jnp.cumsum(dt * a, axis=1)
    w = jnp.exp(acs[:, -1:] - acs) * dt
    return jnp.einsum('blh,blhn,blhp->bhpn', w, bmat, x)


def ssd_inputs(p, conv_w, conv_b, dt_bias, is_grid):
    bsz, l, _ = p.shape
    xbc = jax.nn.silu(token_conv(p[..., :XBC_DIM], conv_w, is_grid) + conv_b).astype(jnp.float32)
    xs = xbc[..., :SSD_WIDTH].reshape(bsz, l, SSD_HEADS, SSD_HEADDIM)
    bm = xbc[..., SSD_WIDTH:SSD_WIDTH + SSD_GN].reshape(bsz, l, SSD_GROUPS, SSD_STATE)
    cm = xbc[..., SSD_WIDTH + SSD_GN:].reshape(bsz, l, SSD_GROUPS, SSD_STATE)
    rep = SSD_HEADS // SSD_GROUPS
    bm = jnp.repeat(bm, rep, axis=2)
    cm = jnp.repeat(cm, rep, axis=2)
    dt_raw = p[..., XBC_DIM:].astype(jnp.float32).reshape(bsz, l, N_DIRS, SSD_HEADS)
    dt = jax.nn.softplus(dt_raw + dt_bias.astype(jnp.float32))
    return xs, bm, cm, dt


def mixer_out(proj, h0_f, h0_b, is_grid, conv_w, ssd_conv_w, ssd_conv_b, dt_bias, a,
              ssd_d, ssd_norm_w, w_out):
    gb = proj[..., :CONV_WIDTH]
    gc = proj[..., CONV_WIDTH:2 * CONV_WIDTH]
    gh = proj[..., 2 * CONV_WIDTH:3 * CONV_WIDTH]
    y_conv = gb * token_conv(gc * gh, conv_w, is_grid)
    z = proj[..., Z_OFF:XBC_OFF]
    xs, bm, cm, dt = ssd_inputs(proj[..., XBC_OFF:], ssd_conv_w, ssd_conv_b, dt_bias, is_grid)
    fl = lambda t: jnp.flip(t, 1)
    y_f = ssd_chunked(xs, dt[:, :, 0], a[0], bm, cm, h0_f)
    y_b = ssd_chunked(fl(xs), fl(dt[:, :, 1]), a[1], fl(bm), fl(cm), h0_b)
    y = y_f + fl(y_b) + xs * ssd_d.astype(jnp.float32)[:, None]
    bsz, l = y.shape[:2]
    yg = y.reshape(bsz, l, SSD_WIDTH) * jax.nn.silu(z.astype(jnp.float32))
    y_ssd = yg * lax.rsqrt(jnp.mean(jnp.square(yg), -1, keepdims=True) + RMS_EPS) * ssd_norm_w
    merged = jnp.concatenate([y_conv, y_ssd.astype(proj.dtype)], axis=-1)
    return merged @ w_out


def sq_relu_mlp(u, w1, w2):
    return jnp.square(jax.nn.relu(u @ w1)) @ w2


def setup_inputs(seed: int = 0) -> dict:
    key = jax.random.key(seed)
    ks = jax.random.split(key, 24)
    beta = (8.0 * DEPTH) ** -0.25
    nrm = lambda k, shape, s: jax.random.normal(k, shape, jnp.float32) * s
    x = nrm(ks[0], (BATCH, SEQ, D_MODEL), 1.0)
    c = nrm(ks[1], (BATCH, D_MODEL), 1.0)
    ctx = nrm(ks[2], (BATCH, CTX_LEN, D_MODEL), 1.0)
    c_ctx = nrm(ks[3], (D_MODEL,), 1.0)
    ln_in_g = 1.0 + nrm(ks[4], (D_MODEL,), 0.02)
    ln_in_b = nrm(ks[5], (D_MODEL,), 0.02)
    w_mod = nrm(ks[6], (DEPTH, D_MODEL, 6 * D_MODEL), D_MODEL ** -0.5)
    b_mod = nrm(ks[7], (DEPTH, 6 * D_MODEL), 0.02)
    w_in = nrm(ks[8], (DEPTH, D_MODEL, IN_DIM), D_MODEL ** -0.5)
    conv_w = nrm(ks[9], (DEPTH, 3, CONV_WIDTH), 3.0 ** -0.5)
    ssd_conv_w = nrm(ks[10], (DEPTH, 3, XBC_DIM), 3.0 ** -0.5)
    ssd_conv_b = nrm(ks[11], (DEPTH, XBC_DIM), 0.02)
    dt0 = jnp.exp(jax.random.uniform(ks[12], (DEPTH, N_DIRS, SSD_HEADS), jnp.float32,
                                     minval=math.log(1e-3), maxval=math.log(1e-1)))
    dt_bias = dt0 + jnp.log(-jnp.expm1(-dt0))
    a_log = jnp.log(jax.random.uniform(ks[13], (DEPTH, N_DIRS, SSD_HEADS), jnp.float32,
                                       minval=1.0, maxval=16.0))
    ssd_d = 1.0 + nrm(ks[14], (DEPTH, SSD_HEADS), 0.1)
    ssd_norm_w = 1.0 + nrm(ks[15], (DEPTH, SSD_WIDTH), 0.02)
    w_out = nrm(ks[16], (DEPTH, D_MODEL, D_MODEL), beta * D_MODEL ** -0.5)
    ln1_g = 1.0 + nrm(ks[17], (DEPTH, D_MODEL), 0.02)
    ln1_b = nrm(ks[18], (DEPTH, D_MODEL), 0.02)
    w_ff1 = nrm(ks[19], (DEPTH, D_MODEL, D_FF), D_MODEL ** -0.5)
    w_ff2 = nrm(ks[20], (DEPTH, D_FF, D_MODEL), beta * D_FF ** -0.5)
    ln2_g = 1.0 + nrm(ks[21], (DEPTH, D_MODEL), 0.02)
    ln2_b = nrm(ks[22], (DEPTH, D_MODEL), 0.02)
    return {"x": x, "c": c, "ctx": ctx, "c_ctx": c_ctx,
            "ln_in_g": ln_in_g, "ln_in_b": ln_in_b, "w_mod": w_mod, "b_mod": b_mod,
            "w_in": w_in, "conv_w": conv_w, "ssd_conv_w": ssd_conv_w, "ssd_conv_b": ssd_conv_b,
            "dt_bias": dt_bias, "a_log": a_log, "ssd_d": ssd_d, "ssd_norm_w": ssd_norm_w,
            "w_out": w_out, "ln1_g": ln1_g, "ln1_b": ln1_b, "w_ff1": w_ff1, "w_ff2": w_ff2,
            "ln2_g": ln2_g, "ln2_b": ln2_b}


def reference(x, c, ctx, c_ctx, ln_in_g, ln_in_b, w_mod, b_mod, w_in, conv_w, ssd_conv_w,
              ssd_conv_b, dt_bias, a_log, ssd_d, ssd_norm_w, w_out, ln1_g, ln1_b, w_ff1, w_ff2,
              ln2_g, ln2_b):
    alpha = (2.0 * DEPTH) ** 0.25
    h = layer_norm(x, ln_in_g, ln_in_b)
    hc = layer_norm(ctx, ln_in_g, ln_in_b)
    c_lat = jax.nn.silu(c)[:, None, :]
    c_con = jax.nn.silu(c_ctx)[None, None, :]
    for i in range(DEPTH):
        last = i == DEPTH - 1
        sh1, sc1, g1, sh2, sc2, g2 = jnp.split(c_lat @ w_mod[i] + b_mod[i], 6, axis=-1)
        mod_c = jnp.split(c_con @ w_mod[i] + b_mod[i], 6, axis=-1)
        a = -jnp.exp(a_log[i].astype(jnp.float32))
        mix_args = (conv_w[i], ssd_conv_w[i], ssd_conv_b[i], dt_bias[i], a, ssd_d[i],
                    ssd_norm_w[i], w_out[i])
        uc = hc * (1.0 + mod_c[1]) + mod_c[0]
        proj_c = uc @ (w_in[i][:, XBC_OFF:] if last else w_in[i])
        xs_c, bm_c, _, dt_c = ssd_inputs(proj_c[..., -SSD_TAIL:], ssd_conv_w[i], ssd_conv_b[i],
                                         dt_bias[i], False)
        fl = lambda t: jnp.flip(t, 1)
        h0_f = ssd_final_state(xs_c, dt_c[:, :, 0], a[0], bm_c)
        h0_b = ssd_final_state(fl(xs_c), fl(dt_c[:, :, 1]), a[1], fl(bm_c))
        u = h * (1.0 + sc1) + sh1
        mix = mixer_out(u @ w_in[i], h0_f, h0_b, True, *mix_args)
        h = layer_norm(alpha * h + g1 * mix, ln1_g[i], ln1_b[i])
        u2 = h * (1.0 + sc2) + sh2
        h = layer_norm(alpha * h + g2 * sq_relu_mlp(u2, w_ff1[i], w_ff2[i]), ln2_g[i], ln2_b[i])
        if not last:
            zeros = jnp.zeros_like(h0_f)
            mix_c = mixer_out(proj_c, zeros, zeros, False, *mix_args)
            hc = layer_norm(alpha * hc + mod_c[2] * mix_c, ln1_g[i], ln1_b[i])
            uc2 = hc * (1.0 + mod_c[4]) + mod_c[3]
            hc = layer_norm(alpha * hc + mod_c[5] * sq_relu_mlp(uc2, w_ff1[i], w_ff2[i]),
                            ln2_g[i], ln2_b[i])
    return h
```

```python
import functools

import jax
import jax.numpy as jnp
import numpy as np
from jax import lax
from jax.experimental import pallas as pl
from jax.experimental.pallas import tpu as pltpu

F32 = jnp.float32
BF16 = jnp.bfloat16

D_MODEL = 1024
BATCH = 8
SEQ = 2048
CTX_LEN = 256
GRID_W = 64
CONV_WIDTH = 512
SSD_WIDTH = 512
SSD_HEADDIM = 64
SSD_HEADS = 8
SSD_GROUPS = 2
SSD_STATE = 128
SSD_CHUNK = 128
N_DIRS = 2
D_FF = 4 * D_MODEL
LN_EPS = 1e-5
RMS_EPS = 1e-5
SSD_GN = SSD_GROUPS * SSD_STATE
XBC_DIM = SSD_WIDTH + 2 * SSD_GN
Z_OFF = 3 * CONV_WIDTH
XBC_OFF = Z_OFF + SSD_WIDTH
DT_OFF = XBC_OFF + XBC_DIM
N_DH = N_DIRS * SSD_HEADS
LANES = 128
IN_PAD = DT_OFF + LANES
CTX_PAD = XBC_DIM + LANES
GROUP_COLS = (SSD_HEADS // SSD_GROUPS) * SSD_HEADDIM
ALPHA = 2.0 ** 0.25

TM = 512
NCH = TM // SSD_CHUNK
NT = SEQ // TM
NCHUNK = SEQ // SSD_CHUNK
FF_BLK = 1024
VMEM_LIMIT = 58 * 1024 * 1024

COL_CS = 0
COL_E1 = 48
COL_W = 80


def _expansion(col0, pieces, width):
    m = np.zeros((LANES, N_DH * width), np.float32)
    for t in range(pieces):
        for j in range(N_DH):
            m[col0 + 16 * t + j, j * width:(j + 1) * width] = 1.0
    return m


_EB = _expansion(COL_CS, 3, LANES)
_EXE = _expansion(COL_E1, 2, SSD_HEADDIM)
_EXW = _expansion(COL_W, 2, SSD_HEADDIM)


def _ln_hat(x):
    mu = jnp.mean(x, axis=-1, keepdims=True)
    xc = x - mu
    var = jnp.mean(xc * xc, axis=-1, keepdims=True)
    return xc * lax.rsqrt(var + LN_EPS)


def _silu(x):
    return x / (1.0 + jnp.exp(-x))


def _softplus(x):
    return jnp.maximum(x, 0.0) + jnp.log1p(jnp.exp(-jnp.abs(x)))


def _edge_masks(rows, period):
    pos = lax.broadcasted_iota(jnp.int32, (rows, LANES), 0) % period
    return (pos != 0).astype(F32), (pos != period - 1).astype(F32)


def _conv3(t, w, mprev, mnext):
    rows = t.shape[0]
    prev = pltpu.roll(t, 1, 0) * mprev
    nxt = pltpu.roll(t, rows - 1, 0) * mnext
    return prev * w[0:1, :] + t * w[1:2, :] + nxt * w[2:3, :]


def _split(v, pieces):
    out = []
    for _ in range(pieces - 1):
        p = v.astype(BF16).astype(F32)
        out.append(p)
        v = v - p
    out.append(v.astype(BF16).astype(F32))
    return out


def _scan_lanes(v, length, is_fwd):
    lane = lax.broadcasted_iota(jnp.int32, v.shape, 1)
    pre, suf = v, v
    k = 1
    while k < length:
        pre = pre + jnp.where(lane >= k, pltpu.roll(pre, k, 1), 0.0)
        suf = suf + jnp.where(lane < length - k, pltpu.roll(suf, length - k, 1), 0.0)
        k *= 2
    return jnp.where(is_fwd, pre, suf)


def _dt_rows(raw, dtb, a_log):
    r = raw.T[0:N_DH, :] + dtb
    dt = _softplus(r)
    return dt, dt * (-jnp.exp(a_log))


def _fwd_rows():
    return lax.broadcasted_iota(jnp.int32, (N_DH, 1), 0) < SSD_HEADS


def _mod_kernel(c_ref, w_ref, b_ref, o_ref):
    s = _silu(c_ref[...]).astype(BF16)
    o_ref[...] = jnp.dot(s, w_ref[...].astype(BF16), preferred_element_type=F32) + b_ref[...]


def _mod_call(cvec, w_mod, b_mod):
    tn = 1536
    return pl.pallas_call(
        _mod_kernel,
        grid=(6 * D_MODEL // tn,),
        in_specs=[pl.BlockSpec((16, D_MODEL), lambda j: (0, 0)),
                  pl.BlockSpec((D_MODEL, tn), lambda j: (0, j)),
                  pl.BlockSpec((1, tn), lambda j: (0, j))],
        out_specs=pl.BlockSpec((16, tn), lambda j: (0, j)),
        out_shape=jax.ShapeDtypeStruct((16, 6 * D_MODEL), F32),
        compiler_params=pltpu.CompilerParams(dimension_semantics=("arbitrary",),
                                             vmem_limit_bytes=VMEM_LIMIT),
        name="mod",
    )(cvec, w_mod, b_mod)


def _ctx_kernel(x_ref, mod_ref, lng_ref, lnb_ref, w_ref, scw_ref, scb_ref, dtb_ref, alog_ref,
                exw_ref, h0_ref):
    m = mod_ref[0]
    sc = 1.0 + m[1:2]
    u = _ln_hat(x_ref[0]) * (lng_ref[...] * sc) + (lnb_ref[...] * sc + m[0:1])
    proj = jnp.dot(u.astype(BF16), w_ref[...], preferred_element_type=F32)
    mprev, mnext = _edge_masks(CTX_LEN, CTX_LEN)
    slabs = []
    for j in range((SSD_WIDTH + SSD_GN) // LANES):
        sl = slice(j * LANES, (j + 1) * LANES)
        slabs.append(_silu(_conv3(proj[:, sl], scw_ref[:, sl], mprev, mnext) + scb_ref[:, sl]))
    xs = jnp.concatenate(slabs[:4], axis=1)
    dt, adt = _dt_rows(proj[:, XBC_DIM:], dtb_ref[...], alog_ref[...])
    incl = _scan_lanes(adt, CTX_LEN, jnp.logical_not(_fwd_rows()))
    w = jnp.exp(incl - adt) * dt
    zero = jnp.zeros((N_DH, CTX_LEN), F32)
    table = jnp.concatenate([zero] * (COL_W // 16) + _split(w, 2) + [zero], axis=0)
    cols = table.T.astype(BF16)
    wx = jnp.dot(cols, exw_ref[...], preferred_element_type=F32)
    for d in range(N_DIRS):
        xw = (xs * wx[:, d * SSD_WIDTH:(d + 1) * SSD_WIDTH]).astype(BF16)
        for g in range(SSD_GROUPS):
            bt = slabs[4 + g].T.astype(BF16)
            h0_ref[0, d, g] = jnp.dot(bt, xw[:, g * GROUP_COLS:(g + 1) * GROUP_COLS],
                                      preferred_element_type=F32)


def _const_spec(shape):
    nd = len(shape)
    return pl.BlockSpec(shape, lambda *_: (0,) * nd, pipeline_mode=pl.Buffered(1))


def _ctx_call(ctx, mod_ctx, lng, lnb, w_ctx, scw, scb, dtb, alog, exw):
    return pl.pallas_call(
        _ctx_kernel,
        grid=(BATCH,),
        in_specs=[pl.BlockSpec((1, CTX_LEN, D_MODEL), lambda b: (b, 0, 0)),
                  _const_spec((1, 6, D_MODEL)), _const_spec((1, D_MODEL)), _const_spec((1, D_MODEL)),
                  _const_spec((D_MODEL, CTX_PAD)), _const_spec((3, XBC_DIM)), _const_spec((1, XBC_DIM)),
                  _const_spec((N_DH, 1)), _const_spec((N_DH, 1)), _const_spec((LANES, 2 * SSD_WIDTH))],
        out_specs=pl.BlockSpec((1, N_DIRS, SSD_GROUPS, SSD_STATE, GROUP_COLS),
                               lambda b: (b, 0, 0, 0, 0)),
        out_shape=jax.ShapeDtypeStruct((BATCH, N_DIRS, SSD_GROUPS, SSD_STATE, GROUP_COLS), F32),
        compiler_params=pltpu.CompilerParams(dimension_semantics=("arbitrary",),
                                             vmem_limit_bytes=VMEM_LIMIT),
        name="ctx",
    )(ctx, mod_ctx, lng, lnb, w_ctx, scw, scb, dtb, alog, exw)


def _proj_kernel(x_ref, mod_ref, lng_ref, lnb_ref, w_ref, cw_ref, scw_ref, scb_ref, dtb_ref,
                 alog_ref, exe_ref, exw_ref, h0_ref,
                 ycv_ref, zg_ref, xs_ref, bt_ref, cm_ref, rows_ref, cols_ref, sb_ref, st_ref):
    @pl.when(pl.program_id(1) == 0)
    def _():
        st_ref[...] = h0_ref[0, 0]

    m = mod_ref[0]
    sc = 1.0 + m[1:2]
    u = _ln_hat(x_ref[0]) * (lng_ref[...] * sc) + (lnb_ref[...] * sc + m[0:1])
    proj = jnp.dot(u.astype(BF16), w_ref[...], preferred_element_type=F32)
    mprev, mnext = _edge_masks(TM, GRID_W)

    for j in range(CONV_WIDTH // LANES):
        sl = slice(j * LANES, (j + 1) * LANES)
        gb = proj[:, j * LANES:(j + 1) * LANES]
        gc = proj[:, CONV_WIDTH + j * LANES:CONV_WIDTH + (j + 1) * LANES]
        gh = proj[:, 2 * CONV_WIDTH + j * LANES:2 * CONV_WIDTH + (j + 1) * LANES]
        ycv_ref[0, :, sl] = (gb * _conv3(gc * gh, cw_ref[:, sl], mprev, mnext)).astype(BF16)

    zg_ref[0] = _silu(proj[:, Z_OFF:XBC_OFF]).astype(BF16)

    slabs = []
    for j in range(XBC_DIM // LANES):
        sl = slice(j * LANES, (j + 1) * LANES)
        pj = proj[:, XBC_OFF + j * LANES:XBC_OFF + (j + 1) * LANES]
        slabs.append(_silu(_conv3(pj, scw_ref[:, sl], mprev, mnext) + scb_ref[:, sl]))
    xs = jnp.concatenate(slabs[:4], axis=1)
    xs_ref[0] = xs.astype(BF16)
    cm_ref[0] = jnp.concatenate(slabs[6:8], axis=1).astype(BF16)

    raw = proj[:, DT_OFF:]
    is_fwd = _fwd_rows()
    zero = jnp.zeros((N_DH, SSD_CHUNK), F32)
    bts, colss = [], []
    for c in range(NCH):
        tok = slice(c * SSD_CHUNK, (c + 1) * SSD_CHUNK)
        dt, adt = _dt_rows(raw[tok], dtb_ref[...], alog_ref[...])
        cs = _scan_lanes(adt, SSD_CHUNK, is_fwd)
        tot = jnp.sum(adt, axis=1, keepdims=True)
        e1 = jnp.exp(cs)
        w = jnp.exp(tot - cs) * dt
        rows_ref[0, c] = jnp.concatenate([cs, dt], axis=0)
        table = jnp.concatenate(_split(cs, 3) + _split(e1, 2) + _split(w, 2) + [zero], axis=0)
        colss.append(table.T.astype(BF16))
        cols_ref[0, c] = colss[c]
        bt = [slabs[4 + g][tok].T.astype(BF16) for g in range(SSD_GROUPS)]
        for g in range(SSD_GROUPS):
            bt_ref[0, c, g] = bt[g]
        bts.append(bt)

    state = [st_ref[g] for g in range(SSD_GROUPS)]
    for c in reversed(range(NCH)):
        tok = slice(c * SSD_CHUNK, (c + 1) * SSD_CHUNK)
        cols = colss[c]
        wxb = jnp.dot(cols, exw_ref[:, SSD_WIDTH:], preferred_element_type=F32)
        dec = jnp.dot(cols[0:16], exe_ref[:, SSD_WIDTH:], preferred_element_type=F32)[0:1]
        xw = (xs[tok] * wxb).astype(BF16)
        for g in range(SSD_GROUPS):
            gs = slice(g * GROUP_COLS, (g + 1) * GROUP_COLS)
            sb_ref[0, c, g] = state[g].astype(BF16)
            local = jnp.dot(bts[c][g], xw[:, gs], preferred_element_type=F32)
            state[g] = state[g] * dec[:, gs] + local
    for g in range(SSD_GROUPS):
        st_ref[g] = state[g]


def _proj_call(x, mod_lat, lng, lnb, w_in, cw, scw, scb, dtb, alog, exe, exw, h0):
    rev = lambda b, t: (b, NT - 1 - t, 0)
    rev4 = lambda b, t: (b, NT - 1 - t, 0, 0)
    rev5 = lambda b, t: (b, NT - 1 - t, 0, 0, 0)
    tok = lambda width: pl.BlockSpec((1, TM, width), rev)
    out_shape = [
        jax.ShapeDtypeStruct((BATCH, SEQ, CONV_WIDTH), BF16),
        jax.ShapeDtypeStruct((BATCH, SEQ, SSD_WIDTH), BF16),
        jax.ShapeDtypeStruct((BATCH, SEQ, SSD_WIDTH), BF16),
        jax.ShapeDtypeStruct((BATCH, NCHUNK, SSD_GROUPS, SSD_STATE, SSD_CHUNK), BF16),
        jax.ShapeDtypeStruct((BATCH, SEQ, SSD_GN), BF16),
        jax.ShapeDtypeStruct((BATCH, NCHUNK, 2 * N_DH, SSD_CHUNK), F32),
        jax.ShapeDtypeStruct((BATCH, NCHUNK, SSD_CHUNK, LANES), BF16),
        jax.ShapeDtypeStruct((BATCH, NCHUNK, SSD_GROUPS, SSD_STATE, GROUP_COLS), BF16),
    ]
    out_specs = [
        tok(CONV_WIDTH), tok(SSD_WIDTH), tok(SSD_WIDTH),
        pl.BlockSpec((1, NCH, SSD_GROUPS, SSD_STATE, SSD_CHUNK), rev5),
        tok(SSD_GN),
        pl.BlockSpec((1, NCH, 2 * N_DH, SSD_CHUNK), rev4),
        pl.BlockSpec((1, NCH, SSD_CHUNK, LANES), rev4),
        pl.BlockSpec((1, NCH, SSD_GROUPS, SSD_STATE, GROUP_COLS), rev5),
    ]
    in_specs = [
        pl.BlockSpec((1, TM, D_MODEL), rev),
        pl.BlockSpec((1, 6, D_MODEL), lambda b, t: (b, 0, 0)),
        _const_spec((1, D_MODEL)), _const_spec((1, D_MODEL)),
        _const_spec((D_MODEL, IN_PAD)),
        _const_spec((3, CONV_WIDTH)), _const_spec((3, XBC_DIM)), _const_spec((1, XBC_DIM)),
        _const_spec((N_DH, 1)), _const_spec((N_DH, 1)),
        _const_spec((LANES, 2 * SSD_WIDTH)), _const_spec((LANES, 2 * SSD_WIDTH)),
        pl.BlockSpec((1, 1, SSD_GROUPS, SSD_STATE, GROUP_COLS), lambda b, t: (b, 1, 0, 0, 0)),
    ]
    return pl.pallas_call(
        _proj_kernel,
        grid=(BATCH, NT),
        in_specs=in_specs,
        out_specs=out_specs,
        out_shape=out_shape,
        scratch_shapes=[pltpu.VMEM((SSD_GROUPS, SSD_STATE, GROUP_COLS), F32)],
        compiler_params=pltpu.CompilerParams(dimension_semantics=("arbitrary", "arbitrary"),
                                             vmem_limit_bytes=VMEM_LIMIT),
        name="proj",
    )(x, mod_lat, lng, lnb, w_in, cw, scw, scb, dtb, alog, exe, exw, h0)


def _out_kernel(x_ref, mod_ref, lng_ref, lnb_ref, ycv_ref, zg_ref, xs_ref, bt_ref, cm_ref, rows_ref,
                cols_ref, sb_ref, h0_ref, eb_ref, exe_ref, exw_ref, dx_ref, nw_ref, wout_ref,
                ln1g_ref, ln1b_ref, wff1_ref, wff2_ref, ln2g_ref, ln2b_ref,
                out_ref, st_ref, yn_ref):
    @pl.when(pl.program_id(1) == 0)
    def _():
        st_ref[...] = h0_ref[0, 0]

    li = lax.broadcasted_iota(jnp.int32, (SSD_CHUNK, SSD_CHUNK), 0)
    si = lax.broadcasted_iota(jnp.int32, (SSD_CHUNK, SSD_CHUNK), 1)
    low = li >= si
    diag = li == si
    lo_half = si < SSD_HEADDIM

    state = [st_ref[g] for g in range(SSD_GROUPS)]
    for c in range(NCH):
        tok = slice(c * SSD_CHUNK, (c + 1) * SSD_CHUNK)
        rows = rows_ref[0, c]
        cols = cols_ref[0, c]
        xs = xs_ref[0, tok, :]
        cm = cm_ref[0, tok, :]
        bc = jnp.dot(cols, eb_ref[...], preferred_element_type=F32)
        e1x = jnp.dot(cols, exe_ref[...], preferred_element_type=F32)
        wxf = jnp.dot(cols, exw_ref[:, :SSD_WIDTH], preferred_element_type=F32)
        dec = jnp.dot(cols[SSD_CHUNK - 16:], exe_ref[:, :SSD_WIDTH],
                      preferred_element_type=F32)[15:16]
        gmat = [jnp.dot(cm[:, g * SSD_STATE:(g + 1) * SSD_STATE], bt_ref[0, c, g],
                        preferred_element_type=F32) for g in range(SSD_GROUPS)]

        ys = []
        for k in range(SSD_HEADS // 2):
            ms = []
            for h in (2 * k, 2 * k + 1):
                g = h // (SSD_HEADS // SSD_GROUPS)
                hb = SSD_HEADS + h
                arg = jnp.where(low,
                                bc[:, h * LANES:(h + 1) * LANES] - rows[h:h + 1, :],
                                bc[:, hb * LANES:(hb + 1) * LANES] - rows[hb:hb + 1, :])
                dtf = rows[N_DH + h:N_DH + h + 1, :]
                dtb = rows[N_DH + hb:N_DH + hb + 1, :]
                fac = jnp.where(low, dtf, dtb) + jnp.where(diag, dtb, 0.0)
                ms.append((gmat[g] * jnp.exp(arg) * fac).astype(BF16))
            xp = xs[:, k * LANES:(k + 1) * LANES]
            rhs = jnp.concatenate([jnp.where(lo_half, xp, jnp.zeros_like(xp)),
                                   jnp.where(lo_half, jnp.zeros_like(xp), xp)], axis=0)
            ys.append(jnp.dot(jnp.concatenate(ms, axis=1), rhs, preferred_element_type=F32))
        y = jnp.concatenate(ys, axis=1)

        yf, yb = [], []
        for g in range(SSD_GROUPS):
            cg = cm[:, g * SSD_STATE:(g + 1) * SSD_STATE]
            yf.append(jnp.dot(cg, state[g].astype(BF16), preferred_element_type=F32))
            yb.append(jnp.dot(cg, sb_ref[0, c, g], preferred_element_type=F32))
        y = (y + jnp.concatenate(yf, axis=1) * e1x[:, :SSD_WIDTH]
             + jnp.concatenate(yb, axis=1) * e1x[:, SSD_WIDTH:]
             + xs.astype(F32) * dx_ref[...])

        yg = y * zg_ref[0, tok, :].astype(F32)
        ms_ = jnp.mean(yg * yg, axis=-1, keepdims=True)
        yn_ref[tok, :] = (yg * lax.rsqrt(ms_ + RMS_EPS) * nw_ref[...]).astype(BF16)

        xw = (xs.astype(F32) * wxf).astype(BF16)
        for g in range(SSD_GROUPS):
            gs = slice(g * GROUP_COLS, (g + 1) * GROUP_COLS)
            local = jnp.dot(bt_ref[0, c, g], xw[:, gs], preferred_element_type=F32)
            state[g] = state[g] * dec[:, gs] + local
    for g in range(SSD_GROUPS):
        st_ref[g] = state[g]

    mix = (jnp.dot(ycv_ref[0], wout_ref[:CONV_WIDTH, :], preferred_element_type=F32)
           + jnp.dot(yn_ref[...], wout_ref[CONV_WIDTH:, :], preferred_element_type=F32))
    m = mod_ref[0]
    h = _ln_hat(x_ref[0]) * lng_ref[...] + lnb_ref[...]
    h1 = _ln_hat(ALPHA * h + m[2:3] * mix) * ln1g_ref[...] + ln1b_ref[...]
    u2 = (h1 * (1.0 + m[4:5]) + m[3:4]).astype(BF16)
    acc = jnp.zeros((TM, D_MODEL), F32)
    for j in range(D_FF // FF_BLK):
        hid = jnp.maximum(jnp.dot(u2, wff1_ref[:, j * FF_BLK:(j + 1) * FF_BLK],
                                  preferred_element_type=F32), 0.0)
        acc = acc + jnp.dot((hid * hid).astype(BF16), wff2_ref[j * FF_BLK:(j + 1) * FF_BLK, :],
                            preferred_element_type=F32)
    out_ref[0] = _ln_hat(ALPHA * h1 + m[5:6] * acc) * ln2g_ref[...] + ln2b_ref[...]


def _out_call(x, mod_lat, lng, lnb, ycv, zg, xs, bt, cm, rows, cols, sb, h0, eb, exe, exw, dx, nw,
              wout, ln1g, ln1b, wff1, wff2, ln2g, ln2b):
    fwd = lambda b, t: (b, t, 0)
    fwd4 = lambda b, t: (b, t, 0, 0)
    fwd5 = lambda b, t: (b, t, 0, 0, 0)
    tok = lambda width: pl.BlockSpec((1, TM, width), fwd)
    in_specs = [
        tok(D_MODEL),
        pl.BlockSpec((1, 6, D_MODEL), lambda b, t: (b, 0, 0)),
        _const_spec((1, D_MODEL)), _const_spec((1, D_MODEL)),
        tok(CONV_WIDTH), tok(SSD_WIDTH), tok(SSD_WIDTH),
        pl.BlockSpec((1, NCH, SSD_GROUPS, SSD_STATE, SSD_CHUNK), fwd5),
        tok(SSD_GN),
        pl.BlockSpec((1, NCH, 2 * N_DH, SSD_CHUNK), fwd4),
        pl.BlockSpec((1, NCH, SSD_CHUNK, LANES), fwd4),
        pl.BlockSpec((1, NCH, SSD_GROUPS, SSD_STATE, GROUP_COLS), fwd5),
        pl.BlockSpec((1, 1, SSD_GROUPS, SSD_STATE, GROUP_COLS), lambda b, t: (b, 0, 0, 0, 0)),
        _const_spec((LANES, N_DH * LANES)),
        _const_spec((LANES, 2 * SSD_WIDTH)), _const_spec((LANES, 2 * SSD_WIDTH)),
        _const_spec((1, SSD_WIDTH)), _const_spec((1, SSD_WIDTH)),
        _const_spec((D_MODEL, D_MODEL)),
        _const_spec((1, D_MODEL)), _const_spec((1, D_MODEL)),
        _const_spec((D_MODEL, D_FF)), _const_spec((D_FF, D_MODEL)),
        _const_spec((1, D_MODEL)), _const_spec((1, D_MODEL)),
    ]
    return pl.pallas_call(
        _out_kernel,
        grid=(BATCH, NT),
        in_specs=in_specs,
        out_specs=tok(D_MODEL),
        out_shape=jax.ShapeDtypeStruct((BATCH, SEQ, D_MODEL), F32),
        scratch_shapes=[pltpu.VMEM((SSD_GROUPS, SSD_STATE, GROUP_COLS), F32),
                        pltpu.VMEM((TM, SSD_WIDTH), BF16)],
        compiler_params=pltpu.CompilerParams(dimension_semantics=("arbitrary", "arbitrary"),
                                             vmem_limit_bytes=VMEM_LIMIT),
        name="out",
    )(x, mod_lat, lng, lnb, ycv, zg, xs, bt, cm, rows, cols, sb, h0, eb, exe, exw, dx, nw,
      wout, ln1g, ln1b, wff1, wff2, ln2g, ln2b)


def kernel(x, c, ctx, c_ctx, ln_in_g, ln_in_b, w_mod, b_mod, w_in, conv_w, ssd_conv_w, ssd_conv_b,
           dt_bias, a_log, ssd_d, ssd_norm_w, w_out, ln1_g, ln1_b, w_ff1, w_ff2, ln2_g, ln2_b):
    row = lambda v: v.reshape(1, -1).astype(F32)
    cvec = jnp.concatenate([c, c_ctx[None, :], jnp.zeros((16 - BATCH - 1, D_MODEL), F32)], axis=0)
    mod = _mod_call(cvec, w_mod[0], row(b_mod[0]))
    mod_lat = mod[:BATCH].reshape(BATCH, 6, D_MODEL)
    mod_ctx = mod[BATCH:BATCH + 1].reshape(1, 6, D_MODEL)

    w_in_p = jnp.pad(w_in[0], ((0, 0), (0, IN_PAD - w_in.shape[-1]))).astype(BF16)
    w_ctx = w_in_p[:, XBC_OFF:]
    lng, lnb = row(ln_in_g), row(ln_in_b)
    scw, scb = ssd_conv_w[0], row(ssd_conv_b[0])
    dtb = dt_bias[0].reshape(N_DH, 1)
    alog = a_log[0].reshape(N_DH, 1)
    eb = jnp.asarray(_EB, BF16)
    exe = jnp.asarray(_EXE, BF16)
    exw = jnp.asarray(_EXW, BF16)

    h0 = _ctx_call(ctx, mod_ctx, lng, lnb, w_ctx, scw, scb, dtb, alog, exw)
    ycv, zg, xs, bt, cm, rows, cols, sb = _proj_call(
        x, mod_lat, lng, lnb, w_in_p, conv_w[0], scw, scb, dtb, alog, exe, exw, h0)
    dx = jnp.repeat(ssd_d[0], SSD_HEADDIM).reshape(1, SSD_WIDTH)
    return _out_call(x, mod_lat, lng, lnb, ycv, zg, xs, bt, cm, rows, cols, sb, h0, eb, exe, exw,
                     dx, row(ssd_norm_w[0]), w_out[0].astype(BF16), row(ln1_g[0]), row(ln1_b[0]),
                     w_ff1[0].astype(BF16), w_ff2[0].astype(BF16), row(ln2_g[0]), row(ln2_b[0]))
```

```python
import functools

import jax
import jax.numpy as jnp
import numpy as np
from jax import lax
from jax.experimental import pallas as pl
from jax.experimental.pallas import tpu as pltpu

F32 = jnp.float32
BF16 = jnp.bfloat16

D_MODEL = 1024
BATCH = 8
SEQ = 2048
CTX_LEN = 256
GRID_W = 64
CONV_WIDTH = 512
SSD_WIDTH = 512
SSD_HEADDIM = 64
SSD_HEADS = 8
SSD_GROUPS = 2
SSD_STATE = 128
SSD_CHUNK = 128
N_DIRS = 2
D_FF = 4 * D_MODEL
LN_EPS = 1e-5
RMS_EPS = 1e-5
SSD_GN = SSD_GROUPS * SSD_STATE
XBC_DIM = SSD_WIDTH + 2 * SSD_GN
Z_OFF = 3 * CONV_WIDTH
XBC_OFF = Z_OFF + SSD_WIDTH
DT_OFF = XBC_OFF + XBC_DIM
N_DH = N_DIRS * SSD_HEADS
LANES = 128
P_DT = 0
P_XBC = LANES
P_Z = P_XBC + XBC_DIM
P_CONV = P_Z + SSD_WIDTH
IN_PAD = P_CONV + 3 * CONV_WIDTH
CTX_PAD = P_Z
GROUP_COLS = (SSD_HEADS // SSD_GROUPS) * SSD_HEADDIM
ALPHA = 2.0 ** 0.25

TM = 512
NCH = TM // SSD_CHUNK
NT = SEQ // TM
NCHUNK = SEQ // SSD_CHUNK
FF_BLK = 1024
VMEM_LIMIT = 58 * 1024 * 1024

COL_CS = 0
COL_E1 = 48
COL_W = 80


def _expansion(col0, pieces, width):
    m = np.zeros((LANES, N_DH * width), np.float32)
    for t in range(pieces):
        for j in range(N_DH):
            m[col0 + 16 * t + j, j * width:(j + 1) * width] = 1.0
    return m


_EB = _expansion(COL_CS, 3, LANES)
_EXE = _expansion(COL_E1, 2, SSD_HEADDIM)
_EXW = _expansion(COL_W, 2, SSD_HEADDIM)


def _ln_hat(x):
    mu = jnp.mean(x, axis=-1, keepdims=True)
    xc = x - mu
    var = jnp.mean(xc * xc, axis=-1, keepdims=True)
    return xc * lax.rsqrt(var + LN_EPS)


def _silu(x):
    return x / (1.0 + jnp.exp(-x))


def _softplus(x):
    return jnp.maximum(x, 0.0) + jnp.log1p(jnp.exp(-jnp.abs(x)))


def _edge_masks(rows, period):
    pos = lax.broadcasted_iota(jnp.int32, (rows, LANES), 0) % period
    return (pos != 0).astype(F32), (pos != period - 1).astype(F32)


def _conv3(t, w, mprev, mnext):
    rows = t.shape[0]
    prev = pltpu.roll(t, 1, 0) * mprev
    nxt = pltpu.roll(t, rows - 1, 0) * mnext
    return prev * w[0:1, :] + t * w[1:2, :] + nxt * w[2:3, :]


def _split(v, pieces):
    out = []
    for _ in range(pieces - 1):
        p = v.astype(BF16).astype(F32)
        out.append(p)
        v = v - p
    out.append(v.astype(BF16).astype(F32))
    return out


def _tri(length, op):
    i = np.arange(length)
    return op(i[:, None], i[None, :]).astype(np.float32)


_U_CHUNK = np.concatenate([_tri(SSD_CHUNK, np.less_equal), _tri(SSD_CHUNK, np.greater_equal),
                           np.ones((SSD_CHUNK, SSD_CHUNK), np.float32)], axis=1)
_U_CTX = np.concatenate([_tri(CTX_LEN, np.greater), _tri(CTX_LEN, np.less)], axis=1)


def _scan_mm(v, u_ref):
    pieces = jnp.concatenate(_split(v, 3), axis=0).astype(BF16)
    o = jnp.dot(pieces, u_ref[...], preferred_element_type=F32)
    return o[0:N_DH] + o[N_DH:2 * N_DH] + o[2 * N_DH:3 * N_DH]


def _dt_rows(raw, dtb, a_log):
    r = raw.T[0:N_DH, :] + dtb
    dt = _softplus(r)
    return dt, dt * (-jnp.exp(a_log))


def _fwd_rows():
    return lax.broadcasted_iota(jnp.int32, (N_DH, 1), 0) < SSD_HEADS


def _mod_kernel(c_ref, w_ref, b_ref, o_ref):
    s = _silu(c_ref[...]).astype(BF16)
    o_ref[...] = jnp.dot(s, w_ref[...].astype(BF16), preferred_element_type=F32) + b_ref[...]


def _mod_call(cvec, w_mod, b_mod):
    tn = 1536
    return pl.pallas_call(
        _mod_kernel,
        grid=(6 * D_MODEL // tn,),
        in_specs=[pl.BlockSpec((16, D_MODEL), lambda j: (0, 0)),
                  pl.BlockSpec((D_MODEL, tn), lambda j: (0, j)),
                  pl.BlockSpec((1, tn), lambda j: (0, j))],
        out_specs=pl.BlockSpec((16, tn), lambda j: (0, j)),
        out_shape=jax.ShapeDtypeStruct((16, 6 * D_MODEL), F32),
        compiler_params=pltpu.CompilerParams(dimension_semantics=("arbitrary",),
                                             vmem_limit_bytes=VMEM_LIMIT),
        name="mod",
    )(cvec, w_mod, b_mod)


def _ctx_kernel(x_ref, mod_ref, lng_ref, lnb_ref, w_ref, scw_ref, scb_ref, dtb_ref, alog_ref,
                exw_ref, u_ref, h0_ref):
    m = mod_ref[0]
    sc = 1.0 + m[1:2]
    u = _ln_hat(x_ref[0]) * (lng_ref[...] * sc) + (lnb_ref[...] * sc + m[0:1])
    proj = jnp.dot(u.astype(BF16), w_ref[...], preferred_element_type=F32)
    mprev, mnext = _edge_masks(CTX_LEN, CTX_LEN)
    slabs = []
    for j in range((SSD_WIDTH + SSD_GN) // LANES):
        sl = slice(j * LANES, (j + 1) * LANES)
        pj = proj[:, P_XBC + j * LANES:P_XBC + (j + 1) * LANES]
        slabs.append(_silu(_conv3(pj, scw_ref[:, sl], mprev, mnext) + scb_ref[:, sl]))
    xs = jnp.concatenate(slabs[:4], axis=1)
    dt, adt = _dt_rows(proj[:, P_DT:P_DT + LANES], dtb_ref[...], alog_ref[...])
    sc2 = _scan_mm(adt, u_ref)
    excl = jnp.where(_fwd_rows(), sc2[:, :CTX_LEN], sc2[:, CTX_LEN:])
    w = jnp.exp(excl) * dt
    zero = jnp.zeros((N_DH, CTX_LEN), F32)
    table = jnp.concatenate([zero] * (COL_W // 16) + _split(w, 2) + [zero], axis=0)
    cols = table.T.astype(BF16)
    wx = jnp.dot(cols, exw_ref[...], preferred_element_type=F32)
    for d in range(N_DIRS):
        xw = (xs * wx[:, d * SSD_WIDTH:(d + 1) * SSD_WIDTH]).astype(BF16)
        for g in range(SSD_GROUPS):
            bt = slabs[4 + g].T.astype(BF16)
            h0_ref[0, d, g] = jnp.dot(bt, xw[:, g * GROUP_COLS:(g + 1) * GROUP_COLS],
                                      preferred_element_type=F32)


def _const_spec(shape):
    nd = len(shape)
    return pl.BlockSpec(shape, lambda *_: (0,) * nd, pipeline_mode=pl.Buffered(1))


def _ctx_call(ctx, mod_ctx, lng, lnb, w_in, scw, scb, dtb, alog, exw, u_ctx):
    return pl.pallas_call(
        _ctx_kernel,
        grid=(BATCH,),
        in_specs=[pl.BlockSpec((1, CTX_LEN, D_MODEL), lambda b: (b, 0, 0)),
                  _const_spec((1, 6, D_MODEL)), _const_spec((1, D_MODEL)), _const_spec((1, D_MODEL)),
                  _const_spec((D_MODEL, CTX_PAD)), _const_spec((3, XBC_DIM)), _const_spec((1, XBC_DIM)),
                  _const_spec((N_DH, 1)), _const_spec((N_DH, 1)), _const_spec((LANES, 2 * SSD_WIDTH)),
                  _const_spec((CTX_LEN, 2 * CTX_LEN))],
        out_specs=pl.BlockSpec((1, N_DIRS, SSD_GROUPS, SSD_STATE, GROUP_COLS),
                               lambda b: (b, 0, 0, 0, 0)),
        out_shape=jax.ShapeDtypeStruct((BATCH, N_DIRS, SSD_GROUPS, SSD_STATE, GROUP_COLS), F32),
        compiler_params=pltpu.CompilerParams(dimension_semantics=("arbitrary",),
                                             vmem_limit_bytes=VMEM_LIMIT),
        name="ctx",
    )(ctx, mod_ctx, lng, lnb, w_in, scw, scb, dtb, alog, exw, u_ctx)


def _proj_kernel(x_ref, mod_ref, lng_ref, lnb_ref, w_ref, cw_ref, scw_ref, scb_ref, dtb_ref,
                 alog_ref, exe_ref, exw_ref, u_ref, h0_ref,
                 ycv_ref, zg_ref, xs_ref, bt_ref, cm_ref, rows_ref, cols_ref, sb_ref, st_ref):
    @pl.when(pl.program_id(1) == 0)
    def _():
        st_ref[...] = h0_ref[0, 0]

    m = mod_ref[0]
    sc = 1.0 + m[1:2]
    u = _ln_hat(x_ref[0]) * (lng_ref[...] * sc) + (lnb_ref[...] * sc + m[0:1])
    proj = jnp.dot(u.astype(BF16), w_ref[...], preferred_element_type=F32)
    mprev, mnext = _edge_masks(TM, GRID_W)

    for j in range(CONV_WIDTH // LANES):
        sl = slice(j * LANES, (j + 1) * LANES)
        gb = proj[:, P_CONV + j * LANES:P_CONV + (j + 1) * LANES]
        gc = proj[:, P_CONV + CONV_WIDTH + j * LANES:P_CONV + CONV_WIDTH + (j + 1) * LANES]
        gh = proj[:, P_CONV + 2 * CONV_WIDTH + j * LANES:P_CONV + 2 * CONV_WIDTH + (j + 1) * LANES]
        ycv_ref[0, :, sl] = (gb * _conv3(gc * gh, cw_ref[:, sl], mprev, mnext)).astype(BF16)

    zg_ref[0] = _silu(proj[:, P_Z:P_CONV]).astype(BF16)

    slabs = []
    for j in range(XBC_DIM // LANES):
        sl = slice(j * LANES, (j + 1) * LANES)
        pj = proj[:, P_XBC + j * LANES:P_XBC + (j + 1) * LANES]
        slabs.append(_silu(_conv3(pj, scw_ref[:, sl], mprev, mnext) + scb_ref[:, sl]))
    xs = jnp.concatenate(slabs[:4], axis=1)
    xs_ref[0] = xs.astype(BF16)
    cm_ref[0] = jnp.concatenate(slabs[6:8], axis=1).astype(BF16)

    raw = proj[:, P_DT:P_DT + LANES]
    is_fwd = _fwd_rows()
    zero = jnp.zeros((N_DH, SSD_CHUNK), F32)
    bts, colss = [], []
    for c in range(NCH):
        tok = slice(c * SSD_CHUNK, (c + 1) * SSD_CHUNK)
        dt, adt = _dt_rows(raw[tok], dtb_ref[...], alog_ref[...])
        sc3 = _scan_mm(adt, u_ref)
        cs = jnp.where(is_fwd, sc3[:, :SSD_CHUNK], sc3[:, SSD_CHUNK:2 * SSD_CHUNK])
        tot = sc3[:, 2 * SSD_CHUNK:]
        e1 = jnp.exp(cs)
        w = jnp.exp(tot - cs) * dt
        rows_ref[0, c] = jnp.concatenate([cs, dt], axis=0)
        table = jnp.concatenate(_split(cs, 3) + _split(e1, 2) + _split(w, 2) + [zero], axis=0)
        colss.append(table.T.astype(BF16))
        cols_ref[0, c] = colss[c]
        bt = [slabs[4 + g][tok].T.astype(BF16) for g in range(SSD_GROUPS)]
        for g in range(SSD_GROUPS):
            bt_ref[0, c, g] = bt[g]
        bts.append(bt)

    state = [st_ref[g] for g in range(SSD_GROUPS)]
    for c in reversed(range(NCH)):
        tok = slice(c * SSD_CHUNK, (c + 1) * SSD_CHUNK)
        cols = colss[c]
        wxb = jnp.dot(cols, exw_ref[:, SSD_WIDTH:], preferred_element_type=F32)
        dec = jnp.dot(cols[0:16], exe_ref[:, SSD_WIDTH:], preferred_element_type=F32)[0:1]
        xw = (xs[tok] * wxb).astype(BF16)
        for g in range(SSD_GROUPS):
            gs = slice(g * GROUP_COLS, (g + 1) * GROUP_COLS)
            sb_ref[0, c, g] = state[g].astype(BF16)
            local = jnp.dot(bts[c][g], xw[:, gs], preferred_element_type=F32)
            state[g] = state[g] * dec[:, gs] + local
    for g in range(SSD_GROUPS):
        st_ref[g] = state[g]


def _proj_call(x, mod_lat, lng, lnb, w_in, cw, scw, scb, dtb, alog, exe, exw, u_chunk, h0):
    rev = lambda b, t: (b, NT - 1 - t, 0)
    rev4 = lambda b, t: (b, NT - 1 - t, 0, 0)
    rev5 = lambda b, t: (b, NT - 1 - t, 0, 0, 0)
    tok = lambda width: pl.BlockSpec((1, TM, width), rev)
    out_shape = [
        jax.ShapeDtypeStruct((BATCH, SEQ, CONV_WIDTH), BF16),
        jax.ShapeDtypeStruct((BATCH, SEQ, SSD_WIDTH), BF16),
        jax.ShapeDtypeStruct((BATCH, SEQ, SSD_WIDTH), BF16),
        jax.ShapeDtypeStruct((BATCH, NCHUNK, SSD_GROUPS, SSD_STATE, SSD_CHUNK), BF16),
        jax.ShapeDtypeStruct((BATCH, SEQ, SSD_GN), BF16),
        jax.ShapeDtypeStruct((BATCH, NCHUNK, 2 * N_DH, SSD_CHUNK), F32),
        jax.ShapeDtypeStruct((BATCH, NCHUNK, SSD_CHUNK, LANES), BF16),
        jax.ShapeDtypeStruct((BATCH, NCHUNK, SSD_GROUPS, SSD_STATE, GROUP_COLS), BF16),
    ]
    out_specs = [
        tok(CONV_WIDTH), tok(SSD_WIDTH), tok(SSD_WIDTH),
        pl.BlockSpec((1, NCH, SSD_GROUPS, SSD_STATE, SSD_CHUNK), rev5),
        tok(SSD_GN),
        pl.BlockSpec((1, NCH, 2 * N_DH, SSD_CHUNK), rev4),
        pl.BlockSpec((1, NCH, SSD_CHUNK, LANES), rev4),
        pl.BlockSpec((1, NCH, SSD_GROUPS, SSD_STATE, GROUP_COLS), rev5),
    ]
    in_specs = [
        pl.BlockSpec((1, TM, D_MODEL), rev),
        pl.BlockSpec((1, 6, D_MODEL), lambda b, t: (b, 0, 0)),
        _const_spec((1, D_MODEL)), _const_spec((1, D_MODEL)),
        _const_spec((D_MODEL, IN_PAD)),
        _const_spec((3, CONV_WIDTH)), _const_spec((3, XBC_DIM)), _const_spec((1, XBC_DIM)),
        _const_spec((N_DH, 1)), _const_spec((N_DH, 1)),
        _const_spec((LANES, 2 * SSD_WIDTH)), _const_spec((LANES, 2 * SSD_WIDTH)),
        _const_spec((SSD_CHUNK, 3 * SSD_CHUNK)),
        pl.BlockSpec((1, 1, SSD_GROUPS, SSD_STATE, GROUP_COLS), lambda b, t: (b, 1, 0, 0, 0)),
    ]
    return pl.pallas_call(
        _proj_kernel,
        grid=(BATCH, NT),
        in_specs=in_specs,
        out_specs=out_specs,
        out_shape=out_shape,
        scratch_shapes=[pltpu.VMEM((SSD_GROUPS, SSD_STATE, GROUP_COLS), F32)],
        compiler_params=pltpu.CompilerParams(dimension_semantics=("arbitrary", "arbitrary"),
                                             vmem_limit_bytes=VMEM_LIMIT),
        name="proj",
    )(x, mod_lat, lng, lnb, w_in, cw, scw, scb, dtb, alog, exe, exw, u_chunk, h0)


def _out_kernel(x_ref, mod_ref, lng_ref, lnb_ref, ycv_ref, zg_ref, xs_ref, bt_ref, cm_ref, rows_ref,
                cols_ref, sb_ref, h0_ref, eb_ref, exe_ref, exw_ref, dx_ref, nw_ref, wout_ref,
                ln1g_ref, ln1b_ref, wff1_ref, wff2_ref, ln2g_ref, ln2b_ref,
                out_ref, st_ref, yn_ref):
    @pl.when(pl.program_id(1) == 0)
    def _():
        st_ref[...] = h0_ref[0, 0]

    li = lax.broadcasted_iota(jnp.int32, (SSD_CHUNK, SSD_CHUNK), 0)
    si = lax.broadcasted_iota(jnp.int32, (SSD_CHUNK, SSD_CHUNK), 1)
    low = li >= si
    diag = li == si
    lo_half = si < SSD_HEADDIM

    state = [st_ref[g] for g in range(SSD_GROUPS)]
    for c in range(NCH):
        tok = slice(c * SSD_CHUNK, (c + 1) * SSD_CHUNK)
        rows = rows_ref[0, c]
        cols = cols_ref[0, c]
        xs = xs_ref[0, tok, :]
        cm = cm_ref[0, tok, :]
        bc = jnp.dot(cols, eb_ref[...], preferred_element_type=F32)
        e1x = jnp.dot(cols, exe_ref[...], preferred_element_type=F32)
        wxf = jnp.dot(cols, exw_ref[:, :SSD_WIDTH], preferred_element_type=F32)
        dec = jnp.dot(cols[SSD_CHUNK - 16:], exe_ref[:, :SSD_WIDTH],
                      preferred_element_type=F32)[15:16]
        gmat = [jnp.dot(cm[:, g * SSD_STATE:(g + 1) * SSD_STATE], bt_ref[0, c, g],
                        preferred_element_type=F32) for g in range(SSD_GROUPS)]

        ys = []
        for k in range(SSD_HEADS // 2):
            ms = []
            for h in (2 * k, 2 * k + 1):
                g = h // (SSD_HEADS // SSD_GROUPS)
                hb = SSD_HEADS + h
                arg = jnp.where(low,
                                bc[:, h * LANES:(h + 1) * LANES] - rows[h:h + 1, :],
                                bc[:, hb * LANES:(hb + 1) * LANES] - rows[hb:hb + 1, :])
                dtf = rows[N_DH + h:N_DH + h + 1, :]
                dtb = rows[N_DH + hb:N_DH + hb + 1, :]
                fac = jnp.where(low, dtf, dtb) + jnp.where(diag, dtb, 0.0)
                ms.append((gmat[g] * jnp.exp(arg) * fac).astype(BF16))
            xp = xs[:, k * LANES:(k + 1) * LANES]
            rhs = jnp.concatenate([jnp.where(lo_half, xp, jnp.zeros_like(xp)),
                                   jnp.where(lo_half, jnp.zeros_like(xp), xp)], axis=0)
            ys.append(jnp.dot(jnp.concatenate(ms, axis=1), rhs, preferred_element_type=F32))
        y = jnp.concatenate(ys, axis=1)

        yf, yb = [], []
        for g in range(SSD_GROUPS):
            cg = cm[:, g * SSD_STATE:(g + 1) * SSD_STATE]
            yf.append(jnp.dot(cg, state[g].astype(BF16), preferred_element_type=F32))
            yb.append(jnp.dot(cg, sb_ref[0, c, g], preferred_element_type=F32))
        y = (y + jnp.concatenate(yf, axis=1) * e1x[:, :SSD_WIDTH]
             + jnp.concatenate(yb, axis=1) * e1x[:, SSD_WIDTH:]
             + xs.astype(F32) * dx_ref[...])

        yg = y * zg_ref[0, tok, :].astype(F32)
        ms_ = jnp.mean(yg * yg, axis=-1, keepdims=True)
        yn_ref[tok, :] = (yg * lax.rsqrt(ms_ + RMS_EPS) * nw_ref[...]).astype(BF16)

        xw = (xs.astype(F32) * wxf).astype(BF16)
        for g in range(SSD_GROUPS):
            gs = slice(g * GROUP_COLS, (g + 1) * GROUP_COLS)
            local = jnp.dot(bt_ref[0, c, g], xw[:, gs], preferred_element_type=F32)
            state[g] = state[g] * dec[:, gs] + local
    for g in range(SSD_GROUPS):
        st_ref[g] = state[g]

    mix = (jnp.dot(ycv_ref[0], wout_ref[:CONV_WIDTH, :], preferred_element_type=F32)
           + jnp.dot(yn_ref[...], wout_ref[CONV_WIDTH:, :], preferred_element_type=F32))
    m = mod_ref[0]
    h = _ln_hat(x_ref[0]) * lng_ref[...] + lnb_ref[...]
    h1 = _ln_hat(ALPHA * h + m[2:3] * mix) * ln1g_ref[...] + ln1b_ref[...]
    u2 = (h1 * (1.0 + m[4:5]) + m[3:4]).astype(BF16)
    acc = jnp.zeros((TM, D_MODEL), F32)
    for j in range(D_FF // FF_BLK):
        hid = jnp.maximum(jnp.dot(u2, wff1_ref[:, j * FF_BLK:(j + 1) * FF_BLK],
                                  preferred_element_type=F32), 0.0)
        acc = acc + jnp.dot((hid * hid).astype(BF16), wff2_ref[j * FF_BLK:(j + 1) * FF_BLK, :],
                            preferred_element_type=F32)
    out_ref[0] = _ln_hat(ALPHA * h1 + m[5:6] * acc) * ln2g_ref[...] + ln2b_ref[...]


def _out_call(x, mod_lat, lng, lnb, ycv, zg, xs, bt, cm, rows, cols, sb, h0, eb, exe, exw, dx, nw,
              wout, ln1g, ln1b, wff1, wff2, ln2g, ln2b):
    fwd = lambda b, t: (b, t, 0)
    fwd4 = lambda b, t: (b, t, 0, 0)
    fwd5 = lambda b, t: (b, t, 0, 0, 0)
    tok = lambda width: pl.BlockSpec((1, TM, width), fwd)
    in_specs = [
        tok(D_MODEL),
        pl.BlockSpec((1, 6, D_MODEL), lambda b, t: (b, 0, 0)),
        _const_spec((1, D_MODEL)), _const_spec((1, D_MODEL)),
        tok(CONV_WIDTH), tok(SSD_WIDTH), tok(SSD_WIDTH),
        pl.BlockSpec((1, NCH, SSD_GROUPS, SSD_STATE, SSD_CHUNK), fwd5),
        tok(SSD_GN),
        pl.BlockSpec((1, NCH, 2 * N_DH, SSD_CHUNK), fwd4),
        pl.BlockSpec((1, NCH, SSD_CHUNK, LANES), fwd4),
        pl.BlockSpec((1, NCH, SSD_GROUPS, SSD_STATE, GROUP_COLS), fwd5),
        pl.BlockSpec((1, 1, SSD_GROUPS, SSD_STATE, GROUP_COLS), lambda b, t: (b, 0, 0, 0, 0)),
        _const_spec((LANES, N_DH * LANES)),
        _const_spec((LANES, 2 * SSD_WIDTH)), _const_spec((LANES, 2 * SSD_WIDTH)),
        _const_spec((1, SSD_WIDTH)), _const_spec((1, SSD_WIDTH)),
        _const_spec((D_MODEL, D_MODEL)),
        _const_spec((1, D_MODEL)), _const_spec((1, D_MODEL)),
        _const_spec((D_MODEL, D_FF)), _const_spec((D_FF, D_MODEL)),
        _const_spec((1, D_MODEL)), _const_spec((1, D_MODEL)),
    ]
    return pl.pallas_call(
        _out_kernel,
        grid=(BATCH, NT),
        in_specs=in_specs,
        out_specs=tok(D_MODEL),
        out_shape=jax.ShapeDtypeStruct((BATCH, SEQ, D_MODEL), F32),
        scratch_shapes=[pltpu.VMEM((SSD_GROUPS, SSD_STATE, GROUP_COLS), F32),
                        pltpu.VMEM((TM, SSD_WIDTH), BF16)],
        compiler_params=pltpu.CompilerParams(dimension_semantics=("arbitrary", "arbitrary"),
                                             vmem_limit_bytes=VMEM_LIMIT),
        name="out",
    )(x, mod_lat, lng, lnb, ycv, zg, xs, bt, cm, rows, cols, sb, h0, eb, exe, exw, dx, nw,
      wout, ln1g, ln1b, wff1, wff2, ln2g, ln2b)


def kernel(x, c, ctx, c_ctx, ln_in_g, ln_in_b, w_mod, b_mod, w_in, conv_w, ssd_conv_w, ssd_conv_b,
           dt_bias, a_log, ssd_d, ssd_norm_w, w_out, ln1_g, ln1_b, w_ff1, w_ff2, ln2_g, ln2_b):
    row = lambda v: v.reshape(1, -1).astype(F32)
    cvec = jnp.concatenate([c, c_ctx[None, :], jnp.zeros((16 - BATCH - 1, D_MODEL), F32)], axis=0)
    mod = _mod_call(cvec, w_mod[0], row(b_mod[0]))
    mod_lat = mod[:BATCH].reshape(BATCH, 6, D_MODEL)
    mod_ctx = mod[BATCH:BATCH + 1].reshape(1, 6, D_MODEL)

    w0 = w_in[0]
    w_in_p = jnp.concatenate(
        [w0[:, DT_OFF:], jnp.zeros((D_MODEL, LANES - N_DH), F32), w0[:, XBC_OFF:DT_OFF],
         w0[:, Z_OFF:XBC_OFF], w0[:, :Z_OFF]], axis=1).astype(BF16)
    lng, lnb = row(ln_in_g), row(ln_in_b)
    scw, scb = ssd_conv_w[0], row(ssd_conv_b[0])
    dtb = dt_bias[0].reshape(N_DH, 1)
    alog = a_log[0].reshape(N_DH, 1)
    eb = jnp.asarray(_EB, BF16)
    exe = jnp.asarray(_EXE, BF16)
    exw = jnp.asarray(_EXW, BF16)

    h0 = _ctx_call(ctx, mod_ctx, lng, lnb, w_in_p, scw, scb, dtb, alog, exw, jnp.asarray(_U_CTX, BF16))
    ycv, zg, xs, bt, cm, rows, cols, sb = _proj_call(
        x, mod_lat, lng, lnb, w_in_p, conv_w[0], scw, scb, dtb, alog, exe, exw,
        jnp.asarray(_U_CHUNK, BF16), h0)
    dx = jnp.repeat(ssd_d[0], SSD_HEADDIM).reshape(1, SSD_WIDTH)
    return _out_call(x, mod_lat, lng, lnb, ycv, zg, xs, bt, cm, rows, cols, sb, h0, eb, exe, exw,
                     dx, row(ssd_norm_w[0]), w_out[0].astype(BF16), row(ln1_g[0]), row(ln1_b[0]),
                     w_ff1[0].astype(BF16), w_ff2[0].astype(BF16), row(ln2_g[0]), row(ln2_b[0]))
```

```python
import jax
import jax.numpy as jnp
import numpy as np
from jax import lax
from jax.experimental import pallas as pl
from jax.experimental.pallas import tpu as pltpu

F32 = jnp.float32
BF16 = jnp.bfloat16

D_MODEL = 1024
BATCH = 8
SEQ = 2048
CTX_LEN = 256
GRID_W = 64
CONV_WIDTH = 512
SSD_WIDTH = 512
SSD_HEADDIM = 64
SSD_HEADS = 8
SSD_GROUPS = 2
SSD_STATE = 128
SSD_CHUNK = 128
N_DIRS = 2
D_FF = 4 * D_MODEL
LN_EPS = 1e-5
RMS_EPS = 1e-5
SSD_GN = SSD_GROUPS * SSD_STATE
XBC_DIM = SSD_WIDTH + 2 * SSD_GN
Z_OFF = 3 * CONV_WIDTH
XBC_OFF = Z_OFF + SSD_WIDTH
DT_OFF = XBC_OFF + XBC_DIM
N_DH = N_DIRS * SSD_HEADS
LANES = 128
P_DT = 0
P_XBC = LANES
P_Z = P_XBC + XBC_DIM
P_CONV = P_Z + SSD_WIDTH
IN_PAD = P_CONV + 3 * CONV_WIDTH
CTX_PAD = P_Z
GROUP_COLS = (SSD_HEADS // SSD_GROUPS) * SSD_HEADDIM
ALPHA = 2.0 ** 0.25

TM = 512
NCH = TM // SSD_CHUNK
NT = SEQ // TM
NCHUNK = SEQ // SSD_CHUNK
FF_BLK = 1024
VMEM_LIMIT = 58 * 1024 * 1024

COL_CS = 0
COL_E1 = 48
COL_W = 80


def _expansion(col0, pieces, width):
    m = np.zeros((LANES, N_DH * width), np.float32)
    for t in range(pieces):
        for j in range(N_DH):
            m[col0 + 16 * t + j, j * width:(j + 1) * width] = 1.0
    return m


_EB = _expansion(COL_CS, 3, LANES)
_EXE = _expansion(COL_E1, 2, SSD_HEADDIM)
_EXW = _expansion(COL_W, 2, SSD_HEADDIM)


def _ln_hat(x):
    mu = jnp.mean(x, axis=-1, keepdims=True)
    xc = x - mu
    var = jnp.mean(xc * xc, axis=-1, keepdims=True)
    return xc * lax.rsqrt(var + LN_EPS)


def _silu(x):
    return x / (1.0 + jnp.exp(-x))


def _softplus(x):
    return jnp.maximum(x, 0.0) + jnp.log1p(jnp.exp(-jnp.abs(x)))


def _edge_masks(rows, period):
    pos = lax.broadcasted_iota(jnp.int32, (rows, LANES), 0) % period
    return (pos != 0).astype(F32), (pos != period - 1).astype(F32)


def _conv3(t, w, mprev, mnext):
    rows = t.shape[0]
    prev = pltpu.roll(t, 1, 0) * mprev
    nxt = pltpu.roll(t, rows - 1, 0) * mnext
    return prev * w[0:1, :] + t * w[1:2, :] + nxt * w[2:3, :]


def _split(v, pieces):
    out = []
    for _ in range(pieces - 1):
        p = v.astype(BF16).astype(F32)
        out.append(p)
        v = v - p
    out.append(v.astype(BF16).astype(F32))
    return out


def _tri(length, op):
    i = np.arange(length)
    return op(i[:, None], i[None, :]).astype(np.float32)


_U_CHUNK = np.concatenate([_tri(SSD_CHUNK, np.less_equal), _tri(SSD_CHUNK, np.greater_equal),
                           np.ones((SSD_CHUNK, SSD_CHUNK), np.float32)], axis=1)
_U_CTX = np.concatenate([_tri(CTX_LEN, np.greater), _tri(CTX_LEN, np.less)], axis=1)


def _scan_mm(v, u_ref):
    pieces = jnp.concatenate(_split(v, 3), axis=0).astype(BF16)
    o = jnp.dot(pieces, u_ref[...], preferred_element_type=F32)
    return o[0:N_DH] + o[N_DH:2 * N_DH] + o[2 * N_DH:3 * N_DH]


def _dt_rows(raw, dtb, a_log):
    r = raw.T[0:N_DH, :] + dtb
    dt = _softplus(r)
    return dt, dt * (-jnp.exp(a_log))


def _fwd_rows():
    return lax.broadcasted_iota(jnp.int32, (N_DH, 1), 0) < SSD_HEADS


def _const_spec(shape):
    nd = len(shape)
    return pl.BlockSpec(shape, lambda *_: (0,) * nd, pipeline_mode=pl.Buffered(1))


def _mod_kernel(c_ref, w_ref, b_ref, o_ref):
    s = _silu(c_ref[...]).astype(BF16)
    o_ref[...] = jnp.dot(s, w_ref[...].astype(BF16), preferred_element_type=F32) + b_ref[...]


def _mod_call(cvec, w_mod, b_mod):
    tn = 1536
    return pl.pallas_call(
        _mod_kernel,
        grid=(6 * D_MODEL // tn,),
        in_specs=[pl.BlockSpec((16, D_MODEL), lambda j: (0, 0)),
                  pl.BlockSpec((D_MODEL, tn), lambda j: (0, j)),
                  pl.BlockSpec((1, tn), lambda j: (0, j))],
        out_specs=pl.BlockSpec((16, tn), lambda j: (0, j)),
        out_shape=jax.ShapeDtypeStruct((16, 6 * D_MODEL), F32),
        compiler_params=pltpu.CompilerParams(dimension_semantics=("arbitrary",),
                                             vmem_limit_bytes=VMEM_LIMIT),
        name="mod",
    )(cvec, w_mod, b_mod)


def _wprep_kernel(win_ref, wout_ref, wff1_ref, wff2_ref, pin_ref, pout_ref, pff1_ref, pff2_ref):
    pin_ref[:, P_DT:P_DT + LANES] = jnp.zeros((pin_ref.shape[0], LANES), BF16)
    pin_ref[:, P_DT:P_DT + N_DH] = win_ref[0, :, DT_OFF:DT_OFF + N_DH].astype(BF16)
    pin_ref[:, P_XBC:P_Z] = win_ref[0, :, XBC_OFF:DT_OFF].astype(BF16)
    pin_ref[:, P_Z:P_CONV] = win_ref[0, :, Z_OFF:XBC_OFF].astype(BF16)
    pin_ref[:, P_CONV:] = win_ref[0, :, :Z_OFF].astype(BF16)
    pout_ref[...] = wout_ref[0].astype(BF16)
    pff1_ref[...] = wff1_ref[0].astype(BF16)
    pff2_ref[...] = wff2_ref[0].astype(BF16)


def _wprep_call(w_in, w_out, w_ff1, w_ff2):
    steps = 8
    r1, r4 = D_MODEL // steps, D_FF // steps
    return pl.pallas_call(
        _wprep_kernel,
        grid=(steps,),
        in_specs=[pl.BlockSpec((1, r1, w_in.shape[-1]), lambda i: (0, i, 0)),
                  pl.BlockSpec((1, r1, D_MODEL), lambda i: (0, i, 0)),
                  pl.BlockSpec((1, r1, D_FF), lambda i: (0, i, 0)),
                  pl.BlockSpec((1, r4, D_MODEL), lambda i: (0, i, 0))],
        out_specs=[pl.BlockSpec((r1, IN_PAD), lambda i: (i, 0)),
                   pl.BlockSpec((r1, D_MODEL), lambda i: (i, 0)),
                   pl.BlockSpec((r1, D_FF), lambda i: (i, 0)),
                   pl.BlockSpec((r4, D_MODEL), lambda i: (i, 0))],
        out_shape=[jax.ShapeDtypeStruct((D_MODEL, IN_PAD), BF16),
                   jax.ShapeDtypeStruct((D_MODEL, D_MODEL), BF16),
                   jax.ShapeDtypeStruct((D_MODEL, D_FF), BF16),
                   jax.ShapeDtypeStruct((D_FF, D_MODEL), BF16)],
        compiler_params=pltpu.CompilerParams(dimension_semantics=("arbitrary",),
                                             vmem_limit_bytes=VMEM_LIMIT),
        name="wprep",
    )(w_in, w_out, w_ff1, w_ff2)


def _ctx_kernel(x_ref, mod_ref, lng_ref, lnb_ref, w_ref, scw_ref, scb_ref, dtb_ref, alog_ref,
                exw_ref, u_ref, h0_ref):
    m = mod_ref[0]
    sc = 1.0 + m[1:2]
    u = _ln_hat(x_ref[0]) * (lng_ref[...] * sc) + (lnb_ref[...] * sc + m[0:1])
    proj = jnp.dot(u.astype(BF16), w_ref[...], preferred_element_type=F32)
    mprev, mnext = _edge_masks(CTX_LEN, CTX_LEN)
    slabs = []
    for j in range((SSD_WIDTH + SSD_GN) // LANES):
        sl = slice(j * LANES, (j + 1) * LANES)
        pj = proj[:, P_XBC + j * LANES:P_XBC + (j + 1) * LANES]
        slabs.append(_silu(_conv3(pj, scw_ref[:, sl], mprev, mnext) + scb_ref[:, sl]))
    xs = jnp.concatenate(slabs[:4], axis=1)
    dt, adt = _dt_rows(proj[:, P_DT:P_DT + LANES], dtb_ref[...], alog_ref[...])
    sc2 = _scan_mm(adt, u_ref)
    excl = jnp.where(_fwd_rows(), sc2[:, :CTX_LEN], sc2[:, CTX_LEN:])
    w = jnp.exp(excl) * dt
    zero = jnp.zeros((N_DH, CTX_LEN), F32)
    table = jnp.concatenate([zero] * (COL_W // 16) + _split(w, 2) + [zero], axis=0)
    cols = table.T.astype(BF16)
    wx = jnp.dot(cols, exw_ref[...], preferred_element_type=F32)
    for d in range(N_DIRS):
        xw = (xs * wx[:, d * SSD_WIDTH:(d + 1) * SSD_WIDTH]).astype(BF16)
        for g in range(SSD_GROUPS):
            bt = slabs[4 + g].T.astype(BF16)
            h0_ref[0, d, g] = jnp.dot(bt, xw[:, g * GROUP_COLS:(g + 1) * GROUP_COLS],
                                      preferred_element_type=F32)


def _ctx_call(ctx, mod_ctx, lng, lnb, w_in, scw, scb, dtb, alog, exw, u_ctx):
    return pl.pallas_call(
        _ctx_kernel,
        grid=(BATCH,),
        in_specs=[pl.BlockSpec((1, CTX_LEN, D_MODEL), lambda b: (b, 0, 0)),
                  _const_spec((1, 6, D_MODEL)), _const_spec((1, D_MODEL)), _const_spec((1, D_MODEL)),
                  _const_spec((D_MODEL, CTX_PAD)), _const_spec((3, XBC_DIM)), _const_spec((1, XBC_DIM)),
                  _const_spec((N_DH, 1)), _const_spec((N_DH, 1)), _const_spec((LANES, 2 * SSD_WIDTH)),
                  _const_spec((CTX_LEN, 2 * CTX_LEN))],
        out_specs=pl.BlockSpec((1, N_DIRS, SSD_GROUPS, SSD_STATE, GROUP_COLS),
                               lambda b: (b, 0, 0, 0, 0)),
        out_shape=jax.ShapeDtypeStruct((BATCH, N_DIRS, SSD_GROUPS, SSD_STATE, GROUP_COLS), F32),
        compiler_params=pltpu.CompilerParams(dimension_semantics=("arbitrary",),
                                             vmem_limit_bytes=VMEM_LIMIT),
        name="ctx",
    )(ctx, mod_ctx, lng, lnb, w_in, scw, scb, dtb, alog, exw, u_ctx)


def _proj_kernel(x_ref, mod_ref, lng_ref, lnb_ref, w_ref, cw_ref, scw_ref, scb_ref, dtb_ref,
                 alog_ref, exe_ref, exw_ref, u_ref, h0_ref,
                 ycv_ref, zg_ref, xs_ref, bt_ref, cm_ref, rows_ref, cols_ref, sb_ref, st_ref):
    @pl.when(pl.program_id(1) == 0)
    def _():
        st_ref[...] = h0_ref[0, 0]

    m = mod_ref[0]
    sc = 1.0 + m[1:2]
    u = _ln_hat(x_ref[0]) * (lng_ref[...] * sc) + (lnb_ref[...] * sc + m[0:1])
    proj = jnp.dot(u.astype(BF16), w_ref[...], preferred_element_type=F32)
    mprev, mnext = _edge_masks(TM, GRID_W)

    for j in range(CONV_WIDTH // LANES):
        sl = slice(j * LANES, (j + 1) * LANES)
        gb = proj[:, P_CONV + j * LANES:P_CONV + (j + 1) * LANES]
        gc = proj[:, P_CONV + CONV_WIDTH + j * LANES:P_CONV + CONV_WIDTH + (j + 1) * LANES]
        gh = proj[:, P_CONV + 2 * CONV_WIDTH + j * LANES:P_CONV + 2 * CONV_WIDTH + (j + 1) * LANES]
        ycv_ref[0, :, sl] = (gb * _conv3(gc * gh, cw_ref[:, sl], mprev, mnext)).astype(BF16)

    zg_ref[0] = _silu(proj[:, P_Z:P_CONV]).astype(BF16)

    slabs = []
    for j in range(XBC_DIM // LANES):
        sl = slice(j * LANES, (j + 1) * LANES)
        pj = proj[:, P_XBC + j * LANES:P_XBC + (j + 1) * LANES]
        slabs.append(_silu(_conv3(pj, scw_ref[:, sl], mprev, mnext) + scb_ref[:, sl]))
    xs = jnp.concatenate(slabs[:4], axis=1)
    xs_ref[0] = xs.astype(BF16)
    cm_ref[0] = jnp.concatenate(slabs[6:8], axis=1).astype(BF16)

    raw = proj[:, P_DT:P_DT + LANES]
    is_fwd = _fwd_rows()
    zero = jnp.zeros((N_DH, SSD_CHUNK), F32)
    bts, colss = [], []
    for c in range(NCH):
        tok = slice(c * SSD_CHUNK, (c + 1) * SSD_CHUNK)
        dt, adt = _dt_rows(raw[tok], dtb_ref[...], alog_ref[...])
        sc3 = _scan_mm(adt, u_ref)
        cs = jnp.where(is_fwd, sc3[:, :SSD_CHUNK], sc3[:, SSD_CHUNK:2 * SSD_CHUNK])
        tot = sc3[:, 2 * SSD_CHUNK:]
        e1 = jnp.exp(cs)
        w = jnp.exp(tot - cs) * dt
        rows_ref[0, c] = jnp.concatenate([cs, dt], axis=0)
        table = jnp.concatenate(_split(cs, 3) + _split(e1, 2) + _split(w, 2) + [zero], axis=0)
        colss.append(table.T.astype(BF16))
        cols_ref[0, c] = colss[c]
        bt = [slabs[4 + g][tok].T.astype(BF16) for g in range(SSD_GROUPS)]
        for g in range(SSD_GROUPS):
            bt_ref[0, c, g] = bt[g]
        bts.append(bt)

    state = [st_ref[g] for g in range(SSD_GROUPS)]
    for c in reversed(range(NCH)):
        tok = slice(c * SSD_CHUNK, (c + 1) * SSD_CHUNK)
        cols = colss[c]
        wxb = jnp.dot(cols, exw_ref[:, SSD_WIDTH:], preferred_element_type=F32)
        dec = jnp.dot(cols[0:16], exe_ref[:, SSD_WIDTH:], preferred_element_type=F32)[0:1]
        xw = (xs[tok] * wxb).astype(BF16)
        for g in range(SSD_GROUPS):
            gs = slice(g * GROUP_COLS, (g + 1) * GROUP_COLS)
            sb_ref[0, c, g] = state[g].astype(BF16)
            local = jnp.dot(bts[c][g], xw[:, gs], preferred_element_type=F32)
            state[g] = state[g] * dec[:, gs] + local
    for g in range(SSD_GROUPS):
        st_ref[g] = state[g]


def _proj_call(x, mod_lat, lng, lnb, w_in, cw, scw, scb, dtb, alog, exe, exw, u_chunk, h0):
    rev = lambda b, t: (b, NT - 1 - t, 0)
    rev4 = lambda b, t: (b, NT - 1 - t, 0, 0)
    rev5 = lambda b, t: (b, NT - 1 - t, 0, 0, 0)
    tok = lambda width: pl.BlockSpec((1, TM, width), rev)
    out_shape = [
        jax.ShapeDtypeStruct((BATCH, SEQ, CONV_WIDTH), BF16),
        jax.ShapeDtypeStruct((BATCH, SEQ, SSD_WIDTH), BF16),
        jax.ShapeDtypeStruct((BATCH, SEQ, SSD_WIDTH), BF16),
        jax.ShapeDtypeStruct((BATCH, NCHUNK, SSD_GROUPS, SSD_STATE, SSD_CHUNK), BF16),
        jax.ShapeDtypeStruct((BATCH, SEQ, SSD_GN), BF16),
        jax.ShapeDtypeStruct((BATCH, NCHUNK, 2 * N_DH, SSD_CHUNK), F32),
        jax.ShapeDtypeStruct((BATCH, NCHUNK, SSD_CHUNK, LANES), BF16),
        jax.ShapeDtypeStruct((BATCH, NCHUNK, SSD_GROUPS, SSD_STATE, GROUP_COLS), BF16),
    ]
    out_specs = [
        tok(CONV_WIDTH), tok(SSD_WIDTH), tok(SSD_WIDTH),
        pl.BlockSpec((1, NCH, SSD_GROUPS, SSD_STATE, SSD_CHUNK), rev5),
        tok(SSD_GN),
        pl.BlockSpec((1, NCH, 2 * N_DH, SSD_CHUNK), rev4),
        pl.BlockSpec((1, NCH, SSD_CHUNK, LANES), rev4),
        pl.BlockSpec((1, NCH, SSD_GROUPS, SSD_STATE, GROUP_COLS), rev5),
    ]
    in_specs = [
        pl.BlockSpec((1, TM, D_MODEL), rev),
        pl.BlockSpec((1, 6, D_MODEL), lambda b, t: (b, 0, 0)),
        _const_spec((1, D_MODEL)), _const_spec((1, D_MODEL)),
        _const_spec((D_MODEL, IN_PAD)),
        _const_spec((3, CONV_WIDTH)), _const_spec((3, XBC_DIM)), _const_spec((1, XBC_DIM)),
        _const_spec((N_DH, 1)), _const_spec((N_DH, 1)),
        _const_spec((LANES, 2 * SSD_WIDTH)), _const_spec((LANES, 2 * SSD_WIDTH)),
        _const_spec((SSD_CHUNK, 3 * SSD_CHUNK)),
        pl.BlockSpec((1, 1, SSD_GROUPS, SSD_STATE, GROUP_COLS), lambda b, t: (b, 1, 0, 0, 0)),
    ]
    return pl.pallas_call(
        _proj_kernel,
        grid=(BATCH, NT),
        in_specs=in_specs,
        out_specs=out_specs,
        out_shape=out_shape,
        scratch_shapes=[pltpu.VMEM((SSD_GROUPS, SSD_STATE, GROUP_COLS), F32)],
        compiler_params=pltpu.CompilerParams(dimension_semantics=("arbitrary", "arbitrary"),
                                             vmem_limit_bytes=VMEM_LIMIT),
        name="proj",
    )(x, mod_lat, lng, lnb, w_in, cw, scw, scb, dtb, alog, exe, exw, u_chunk, h0)


def _out_kernel(x_ref, mod_ref, lng_ref, lnb_ref, ycv_ref, zg_ref, xs_ref, bt_ref, cm_ref, rows_ref,
                cols_ref, sb_ref, h0_ref, eb_ref, exe_ref, exw_ref, dx_ref, nw_ref, wout_ref,
                ln1g_ref, ln1b_ref, wff1_ref, wff2_ref, ln2g_ref, ln2b_ref,
                out_ref, st_ref, yn_ref):
    @pl.when(pl.program_id(1) == 0)
    def _():
        st_ref[...] = h0_ref[0, 0]

    li = lax.broadcasted_iota(jnp.int32, (SSD_CHUNK, SSD_CHUNK), 0)
    si = lax.broadcasted_iota(jnp.int32, (SSD_CHUNK, SSD_CHUNK), 1)
    low = li >= si
    diag = li == si
    lo_half = si < SSD_HEADDIM

    state = [st_ref[g] for g in range(SSD_GROUPS)]
    for c in range(NCH):
        tok = slice(c * SSD_CHUNK, (c + 1) * SSD_CHUNK)
        rows = rows_ref[0, c]
        cols = cols_ref[0, c]
        xs = xs_ref[0, tok, :]
        cm = cm_ref[0, tok, :]
        bc = jnp.dot(cols, eb_ref[...], preferred_element_type=F32)
        e1x = jnp.dot(cols, exe_ref[...], preferred_element_type=F32)
        wxf = jnp.dot(cols, exw_ref[:, :SSD_WIDTH], preferred_element_type=F32)
        dec = jnp.dot(cols[SSD_CHUNK - 16:], exe_ref[:, :SSD_WIDTH],
                      preferred_element_type=F32)[15:16]
        gmat = [jnp.dot(cm[:, g * SSD_STATE:(g + 1) * SSD_STATE], bt_ref[0, c, g],
                        preferred_element_type=F32) for g in range(SSD_GROUPS)]

        ys = []
        for k in range(SSD_HEADS // 2):
            ms = []
            for h in (2 * k, 2 * k + 1):
                g = h // (SSD_HEADS // SSD_GROUPS)
                hb = SSD_HEADS + h
                arg = jnp.where(low,
                                bc[:, h * LANES:(h + 1) * LANES] - rows[h:h + 1, :],
                                bc[:, hb * LANES:(hb + 1) * LANES] - rows[hb:hb + 1, :])
                dtf = rows[N_DH + h:N_DH + h + 1, :]
                dtb = rows[N_DH + hb:N_DH + hb + 1, :]
                fac = jnp.where(low, dtf, dtb) + jnp.where(diag, dtb, 0.0)
                ms.append((gmat[g] * jnp.exp(arg) * fac).astype(BF16))
            xp = xs[:, k * LANES:(k + 1) * LANES]
            rhs = jnp.concatenate([jnp.where(lo_half, xp, jnp.zeros_like(xp)),
                                   jnp.where(lo_half, jnp.zeros_like(xp), xp)], axis=0)
            ys.append(jnp.dot(jnp.concatenate(ms, axis=1), rhs, preferred_element_type=F32))
        y = jnp.concatenate(ys, axis=1)

        yf, yb = [], []
        for g in range(SSD_GROUPS):
            cg = cm[:, g * SSD_STATE:(g + 1) * SSD_STATE]
            yf.append(jnp.dot(cg, state[g].astype(BF16), preferred_element_type=F32))
            yb.append(jnp.dot(cg, sb_ref[0, c, g], preferred_element_type=F32))
        y = (y + jnp.concatenate(yf, axis=1) * e1x[:, :SSD_WIDTH]
             + jnp.concatenate(yb, axis=1) * e1x[:, SSD_WIDTH:]
             + xs.astype(F32) * dx_ref[...])

        yg = y * zg_ref[0, tok, :].astype(F32)
        ms_ = jnp.mean(yg * yg, axis=-1, keepdims=True)
        yn_ref[tok, :] = (yg * lax.rsqrt(ms_ + RMS_EPS) * nw_ref[...]).astype(BF16)

        xw = (xs.astype(F32) * wxf).astype(BF16)
        for g in range(SSD_GROUPS):
            gs = slice(g * GROUP_COLS, (g + 1) * GROUP_COLS)
            local = jnp.dot(bt_ref[0, c, g], xw[:, gs], preferred_element_type=F32)
            state[g] = state[g] * dec[:, gs] + local

    for g in range(SSD_GROUPS):
        st_ref[g] = state[g]

    mix = (jnp.dot(ycv_ref[0], wout_ref[:CONV_WIDTH, :], preferred_element_type=F32)
           + jnp.dot(yn_ref[...], wout_ref[CONV_WIDTH:, :], preferred_element_type=F32))
    m = mod_ref[0]
    h = _ln_hat(x_ref[0]) * lng_ref[...] + lnb_ref[...]
    h1 = _ln_hat(ALPHA * h + m[2:3] * mix) * ln1g_ref[...] + ln1b_ref[...]
    u2 = (h1 * (1.0 + m[4:5]) + m[3:4]).astype(BF16)
    acc = jnp.zeros((TM, D_MODEL), F32)
    for j in range(D_FF // FF_BLK):
        hid = jnp.maximum(jnp.dot(u2, wff1_ref[:, j * FF_BLK:(j + 1) * FF_BLK],
                                  preferred_element_type=F32), 0.0)
        acc = acc + jnp.dot((hid * hid).astype(BF16), wff2_ref[j * FF_BLK:(j + 1) * FF_BLK, :],
                            preferred_element_type=F32)
    out_ref[0] = _ln_hat(ALPHA * h1 + m[5:6] * acc) * ln2g_ref[...] + ln2b_ref[...]


def _out_call(x, mod_lat, lng, lnb, ycv, zg, xs, bt, cm, rows, cols, sb, h0, eb, exe, exw, dx, nw,
              wout, ln1g, ln1b, wff1, wff2, ln2g, ln2b):
    fwd = lambda b, t: (b, t, 0)
    fwd4 = lambda b, t: (b, t, 0, 0)
    fwd5 = lambda b, t: (b, t, 0, 0, 0)
    tok = lambda width: pl.BlockSpec((1, TM, width), fwd)
    in_specs = [
        tok(D_MODEL),
        pl.BlockSpec((1, 6, D_MODEL), lambda b, t: (b, 0, 0)),
        _const_spec((1, D_MODEL)), _const_spec((1, D_MODEL)),
        tok(CONV_WIDTH), tok(SSD_WIDTH), tok(SSD_WIDTH),
        pl.BlockSpec((1, NCH, SSD_GROUPS, SSD_STATE, SSD_CHUNK), fwd5),
        tok(SSD_GN),
        pl.BlockSpec((1, NCH, 2 * N_DH, SSD_CHUNK), fwd4),
        pl.BlockSpec((1, NCH, SSD_CHUNK, LANES), fwd4),
        pl.BlockSpec((1, NCH, SSD_GROUPS, SSD_STATE, GROUP_COLS), fwd5),
        pl.BlockSpec((1, 1, SSD_GROUPS, SSD_STATE, GROUP_COLS), lambda b, t: (b, 0, 0, 0, 0)),
        _const_spec((LANES, N_DH * LANES)),
        _const_spec((LANES, 2 * SSD_WIDTH)), _const_spec((LANES, 2 * SSD_WIDTH)),
        _const_spec((1, SSD_WIDTH)), _const_spec((1, SSD_WIDTH)),
        _const_spec((D_MODEL, D_MODEL)),
        _const_spec((1, D_MODEL)), _const_spec((1, D_MODEL)),
        _const_spec((D_MODEL, D_FF)), _const_spec((D_FF, D_MODEL)),
        _const_spec((1, D_MODEL)), _const_spec((1, D_MODEL)),
    ]
    return pl.pallas_call(
        _out_kernel,
        grid=(BATCH, NT),
        in_specs=in_specs,
        out_specs=tok(D_MODEL),
        out_shape=jax.ShapeDtypeStruct((BATCH, SEQ, D_MODEL), F32),
        scratch_shapes=[pltpu.VMEM((SSD_GROUPS, SSD_STATE, GROUP_COLS), F32),
                        pltpu.VMEM((TM, SSD_WIDTH), BF16)],
        compiler_params=pltpu.CompilerParams(dimension_semantics=("arbitrary", "arbitrary"),
                                             vmem_limit_bytes=VMEM_LIMIT),
        name="out",
    )(x, mod_lat, lng, lnb, ycv, zg, xs, bt, cm, rows, cols, sb, h0, eb, exe, exw, dx, nw,
      wout, ln1g, ln1b, wff1, wff2, ln2g, ln2b)


def kernel(x, c, ctx, c_ctx, ln_in_g, ln_in_b, w_mod, b_mod, w_in, conv_w, ssd_conv_w, ssd_conv_b,
           dt_bias, a_log, ssd_d, ssd_norm_w, w_out, ln1_g, ln1_b, w_ff1, w_ff2, ln2_g, ln2_b):
    row = lambda v: v.reshape(1, -1).astype(F32)
    cvec = jnp.concatenate([c, c_ctx[None, :], jnp.zeros((16 - BATCH - 1, D_MODEL), F32)], axis=0)
    mod = _mod_call(cvec, w_mod[0], row(b_mod[0]))
    mod_lat = mod[:BATCH].reshape(BATCH, 6, D_MODEL)
    mod_ctx = mod[BATCH:BATCH + 1].reshape(1, 6, D_MODEL)

    w_in_p, w_out_p, w_ff1_p, w_ff2_p = _wprep_call(w_in, w_out, w_ff1, w_ff2)
    lng, lnb = row(ln_in_g), row(ln_in_b)
    scw, scb = ssd_conv_w[0], row(ssd_conv_b[0])
    dtb = dt_bias[0].reshape(N_DH, 1)
    alog = a_log[0].reshape(N_DH, 1)
    eb = jnp.asarray(_EB, BF16)
    exe = jnp.asarray(_EXE, BF16)
    exw = jnp.asarray(_EXW, BF16)

    h0 = _ctx_call(ctx, mod_ctx, lng, lnb, w_in_p, scw, scb, dtb, alog, exw, jnp.asarray(_U_CTX, BF16))
    ycv, zg, xs, bt, cm, rows, cols, sb = _proj_call(
        x, mod_lat, lng, lnb, w_in_p, conv_w[0], scw, scb, dtb, alog, exe, exw,
        jnp.asarray(_U_CHUNK, BF16), h0)
    dx = jnp.repeat(ssd_d[0], SSD_HEADDIM).reshape(1, SSD_WIDTH)
    return _out_call(x, mod_lat, lng, lnb, ycv, zg, xs, bt, cm, rows, cols, sb, h0, eb, exe, exw,
                     dx, row(ssd_norm_w[0]), w_out_p, row(ln1_g[0]), row(ln1_b[0]),
                     w_ff1_p, w_ff2_p, row(ln2_g[0]), row(ln2_b[0]))
```

```python
import jax
import jax.numpy as jnp
import numpy as np
from jax import lax
from jax.experimental import pallas as pl
from jax.experimental.pallas import tpu as pltpu

F32 = jnp.float32
BF16 = jnp.bfloat16

D_MODEL = 1024
BATCH = 8
SEQ = 2048
CTX_LEN = 256
GRID_W = 64
CONV_WIDTH = 512
SSD_WIDTH = 512
SSD_HEADDIM = 64
SSD_HEADS = 8
SSD_GROUPS = 2
SSD_STATE = 128
SSD_CHUNK = 128
N_DIRS = 2
D_FF = 4 * D_MODEL
LN_EPS = 1e-5
RMS_EPS = 1e-5
SSD_GN = SSD_GROUPS * SSD_STATE
XBC_DIM = SSD_WIDTH + 2 * SSD_GN
Z_OFF = 3 * CONV_WIDTH
XBC_OFF = Z_OFF + SSD_WIDTH
DT_OFF = XBC_OFF + XBC_DIM
N_DH = N_DIRS * SSD_HEADS
LANES = 128
P_DT = 0
P_XBC = LANES
P_Z = P_XBC + XBC_DIM
P_CONV = P_Z + SSD_WIDTH
IN_PAD = P_CONV + 3 * CONV_WIDTH
CTX_PAD = P_Z
GROUP_COLS = (SSD_HEADS // SSD_GROUPS) * SSD_HEADDIM
ALPHA = 2.0 ** 0.25

TM = 512
NCH = TM // SSD_CHUNK
NT = SEQ // TM
NCHUNK = SEQ // SSD_CHUNK
FF_BLK = 1024
VMEM_LIMIT = 58 * 1024 * 1024

COL_CS = 0
COL_E1 = 48
COL_W = 80


def _expansion(col0, pieces, width):
    m = np.zeros((LANES, N_DH * width), np.float32)
    for t in range(pieces):
        for j in range(N_DH):
            m[col0 + 16 * t + j, j * width:(j + 1) * width] = 1.0
    return m


_EB = _expansion(COL_CS, 3, LANES)
_EXE = _expansion(COL_E1, 2, SSD_HEADDIM)
_EXW = _expansion(COL_W, 2, SSD_HEADDIM)


def _ln_hat(x):
    mu = jnp.mean(x, axis=-1, keepdims=True)
    xc = x - mu
    var = jnp.mean(xc * xc, axis=-1, keepdims=True)
    return xc * lax.rsqrt(var + LN_EPS)


def _silu(x):
    return x / (1.0 + jnp.exp(-x))


def _softplus(x):
    return jnp.maximum(x, 0.0) + jnp.log1p(jnp.exp(-jnp.abs(x)))


def _edge_masks(rows, period):
    pos = lax.broadcasted_iota(jnp.int32, (rows, LANES), 0) % period
    return (pos != 0).astype(F32), (pos != period - 1).astype(F32)


def _conv3(t, w, mprev, mnext):
    rows = t.shape[0]
    prev = pltpu.roll(t, 1, 0) * mprev
    nxt = pltpu.roll(t, rows - 1, 0) * mnext
    return prev * w[0:1, :] + t * w[1:2, :] + nxt * w[2:3, :]


def _split(v, pieces):
    out = []
    for _ in range(pieces - 1):
        p = v.astype(BF16).astype(F32)
        out.append(p)
        v = v - p
    out.append(v.astype(BF16).astype(F32))
    return out


def _tri(length, op):
    i = np.arange(length)
    return op(i[:, None], i[None, :]).astype(np.float32)


_U_CHUNK = np.concatenate([_tri(SSD_CHUNK, np.less_equal), _tri(SSD_CHUNK, np.greater_equal),
                           np.ones((SSD_CHUNK, SSD_CHUNK), np.float32)], axis=1)
_U_CTX = np.concatenate([_tri(CTX_LEN, np.greater), _tri(CTX_LEN, np.less)], axis=1)


def _scan_mm(v, u_ref):
    pieces = jnp.concatenate(_split(v, 3), axis=0).astype(BF16)
    o = jnp.dot(pieces, u_ref[...], preferred_element_type=F32)
    return o[0:N_DH] + o[N_DH:2 * N_DH] + o[2 * N_DH:3 * N_DH]


def _dt_rows(raw, dtb, a_log):
    r = raw.T[0:N_DH, :] + dtb
    dt = _softplus(r)
    return dt, dt * (-jnp.exp(a_log))


def _fwd_rows():
    return lax.broadcasted_iota(jnp.int32, (N_DH, 1), 0) < SSD_HEADS


def _const_spec(shape):
    nd = len(shape)
    return pl.BlockSpec(shape, lambda *_: (0,) * nd, pipeline_mode=pl.Buffered(1))


def _mod_kernel(c_ref, w_ref, b_ref, o_ref):
    s = _silu(c_ref[...]).astype(BF16)
    o_ref[...] = jnp.dot(s, w_ref[...].astype(BF16), preferred_element_type=F32) + b_ref[...]


def _mod_call(cvec, w_mod, b_mod):
    tn = 1536
    return pl.pallas_call(
        _mod_kernel,
        grid=(6 * D_MODEL // tn,),
        in_specs=[pl.BlockSpec((16, D_MODEL), lambda j: (0, 0)),
                  pl.BlockSpec((D_MODEL, tn), lambda j: (0, j)),
                  pl.BlockSpec((1, tn), lambda j: (0, j))],
        out_specs=pl.BlockSpec((16, tn), lambda j: (0, j)),
        out_shape=jax.ShapeDtypeStruct((16, 6 * D_MODEL), F32),
        compiler_params=pltpu.CompilerParams(dimension_semantics=("arbitrary",),
                                             vmem_limit_bytes=VMEM_LIMIT),
        name="mod",
    )(cvec, w_mod, b_mod)


def _wprep_kernel(wint_ref, wout_ref, wff1_ref, wff2_ref, pin_ref, pout_ref, pff1_ref, pff2_ref):
    dt_rows = jnp.concatenate([wint_ref[DT_OFF:DT_OFF + N_DH, :],
                               jnp.zeros((LANES - N_DH, wint_ref.shape[1]), F32)], axis=0)
    pin_ref[:, P_DT:P_DT + LANES] = dt_rows.T.astype(BF16)
    for dst, src, width in ((P_XBC, XBC_OFF, XBC_DIM), (P_Z, Z_OFF, SSD_WIDTH), (P_CONV, 0, Z_OFF)):
        for j in range(0, width, LANES):
            pin_ref[:, dst + j:dst + j + LANES] = wint_ref[src + j:src + j + LANES, :].T.astype(BF16)
    pout_ref[...] = wout_ref[0].astype(BF16)
    pff1_ref[...] = wff1_ref[0].astype(BF16)
    pff2_ref[...] = wff2_ref[0].astype(BF16)


def _wprep_call(w_in_t, w_out, w_ff1, w_ff2):
    steps = 8
    r1, r4 = D_MODEL // steps, D_FF // steps
    return pl.pallas_call(
        _wprep_kernel,
        grid=(steps,),
        in_specs=[pl.BlockSpec((w_in_t.shape[0], r1), lambda i: (0, i)),
                  pl.BlockSpec((1, r1, D_MODEL), lambda i: (0, i, 0)),
                  pl.BlockSpec((1, r1, D_FF), lambda i: (0, i, 0)),
                  pl.BlockSpec((1, r4, D_MODEL), lambda i: (0, i, 0))],
        out_specs=[pl.BlockSpec((r1, IN_PAD), lambda i: (i, 0)),
                   pl.BlockSpec((r1, D_MODEL), lambda i: (i, 0)),
                   pl.BlockSpec((r1, D_FF), lambda i: (i, 0)),
                   pl.BlockSpec((r4, D_MODEL), lambda i: (i, 0))],
        out_shape=[jax.ShapeDtypeStruct((D_MODEL, IN_PAD), BF16),
                   jax.ShapeDtypeStruct((D_MODEL, D_MODEL), BF16),
                   jax.ShapeDtypeStruct((D_MODEL, D_FF), BF16),
                   jax.ShapeDtypeStruct((D_FF, D_MODEL), BF16)],
        compiler_params=pltpu.CompilerParams(dimension_semantics=("arbitrary",),
                                             vmem_limit_bytes=VMEM_LIMIT),
        name="wprep",
    )(w_in_t, w_out, w_ff1, w_ff2)


def _ctx_kernel(x_ref, mod_ref, lng_ref, lnb_ref, w_ref, scw_ref, scb_ref, dtb_ref, alog_ref,
                exw_ref, u_ref, h0_ref):
    m = mod_ref[0]
    sc = 1.0 + m[1:2]
    u = _ln_hat(x_ref[0]) * (lng_ref[...] * sc) + (lnb_ref[...] * sc + m[0:1])
    proj = jnp.dot(u.astype(BF16), w_ref[...], preferred_element_type=F32)
    mprev, mnext = _edge_masks(CTX_LEN, CTX_LEN)
    slabs = []
    for j in range((SSD_WIDTH + SSD_GN) // LANES):
        sl = slice(j * LANES, (j + 1) * LANES)
        pj = proj[:, P_XBC + j * LANES:P_XBC + (j + 1) * LANES]
        slabs.append(_silu(_conv3(pj, scw_ref[:, sl], mprev, mnext) + scb_ref[:, sl]))
    xs = jnp.concatenate(slabs[:4], axis=1)
    dt, adt = _dt_rows(proj[:, P_DT:P_DT + LANES], dtb_ref[...], alog_ref[...])
    sc2 = _scan_mm(adt, u_ref)
    excl = jnp.where(_fwd_rows(), sc2[:, :CTX_LEN], sc2[:, CTX_LEN:])
    w = jnp.exp(excl) * dt
    zero = jnp.zeros((N_DH, CTX_LEN), F32)
    table = jnp.concatenate([zero] * (COL_W // 16) + _split(w, 2) + [zero], axis=0)
    cols = table.T.astype(BF16)
    wx = jnp.dot(cols, exw_ref[...], preferred_element_type=F32)
    for d in range(N_DIRS):
        xw = (xs * wx[:, d * SSD_WIDTH:(d + 1) * SSD_WIDTH]).astype(BF16)
        for g in range(SSD_GROUPS):
            bt = slabs[4 + g].T.astype(BF16)
            h0_ref[0, d, g] = jnp.dot(bt, xw[:, g * GROUP_COLS:(g + 1) * GROUP_COLS],
                                      preferred_element_type=F32)


def _ctx_call(ctx, mod_ctx, lng, lnb, w_in, scw, scb, dtb, alog, exw, u_ctx):
    return pl.pallas_call(
        _ctx_kernel,
        grid=(BATCH,),
        in_specs=[pl.BlockSpec((1, CTX_LEN, D_MODEL), lambda b: (b, 0, 0)),
                  _const_spec((1, 6, D_MODEL)), _const_spec((1, D_MODEL)), _const_spec((1, D_MODEL)),
                  _const_spec((D_MODEL, CTX_PAD)), _const_spec((3, XBC_DIM)), _const_spec((1, XBC_DIM)),
                  _const_spec((N_DH, 1)), _const_spec((N_DH, 1)), _const_spec((LANES, 2 * SSD_WIDTH)),
                  _const_spec((CTX_LEN, 2 * CTX_LEN))],
        out_specs=pl.BlockSpec((1, N_DIRS, SSD_GROUPS, SSD_STATE, GROUP_COLS),
                               lambda b: (b, 0, 0, 0, 0)),
        out_shape=jax.ShapeDtypeStruct((BATCH, N_DIRS, SSD_GROUPS, SSD_STATE, GROUP_COLS), F32),
        compiler_params=pltpu.CompilerParams(dimension_semantics=("arbitrary",),
                                             vmem_limit_bytes=VMEM_LIMIT),
        name="ctx",
    )(ctx, mod_ctx, lng, lnb, w_in, scw, scb, dtb, alog, exw, u_ctx)


def _proj_kernel(x_ref, mod_ref, lng_ref, lnb_ref, w_ref, cw_ref, scw_ref, scb_ref, dtb_ref,
                 alog_ref, exe_ref, exw_ref, u_ref, h0_ref,
                 ycv_ref, zg_ref, xs_ref, bt_ref, cm_ref, rows_ref, cols_ref, sb_ref, st_ref):
    @pl.when(pl.program_id(1) == 0)
    def _():
        st_ref[...] = h0_ref[0, 0]

    m = mod_ref[0]
    sc = 1.0 + m[1:2]
    u = _ln_hat(x_ref[0]) * (lng_ref[...] * sc) + (lnb_ref[...] * sc + m[0:1])
    proj = jnp.dot(u.astype(BF16), w_ref[...], preferred_element_type=F32)
    mprev, mnext = _edge_masks(TM, GRID_W)

    for j in range(CONV_WIDTH // LANES):
        sl = slice(j * LANES, (j + 1) * LANES)
        gb = proj[:, P_CONV + j * LANES:P_CONV + (j + 1) * LANES]
        gc = proj[:, P_CONV + CONV_WIDTH + j * LANES:P_CONV + CONV_WIDTH + (j + 1) * LANES]
        gh = proj[:, P_CONV + 2 * CONV_WIDTH + j * LANES:P_CONV + 2 * CONV_WIDTH + (j + 1) * LANES]
        ycv_ref[0, :, sl] = (gb * _conv3(gc * gh, cw_ref[:, sl], mprev, mnext)).astype(BF16)

    zg_ref[0] = _silu(proj[:, P_Z:P_CONV]).astype(BF16)

    slabs = []
    for j in range(XBC_DIM // LANES):
        sl = slice(j * LANES, (j + 1) * LANES)
        pj = proj[:, P_XBC + j * LANES:P_XBC + (j + 1) * LANES]
        slabs.append(_silu(_conv3(pj, scw_ref[:, sl], mprev, mnext) + scb_ref[:, sl]))
    xs = jnp.concatenate(slabs[:4], axis=1)
    xs_ref[0] = xs.astype(BF16)
    cm_ref[0] = jnp.concatenate(slabs[6:8], axis=1).astype(BF16)

    raw = proj[:, P_DT:P_DT + LANES]
    is_fwd = _fwd_rows()
    zero = jnp.zeros((N_DH, SSD_CHUNK), F32)
    bts, colss = [], []
    for c in range(NCH):
        tok = slice(c * SSD_CHUNK, (c + 1) * SSD_CHUNK)
        dt, adt = _dt_rows(raw[tok], dtb_ref[...], alog_ref[...])
        sc3 = _scan_mm(adt, u_ref)
        cs = jnp.where(is_fwd, sc3[:, :SSD_CHUNK], sc3[:, SSD_CHUNK:2 * SSD_CHUNK])
        tot = sc3[:, 2 * SSD_CHUNK:]
        e1 = jnp.exp(cs)
        w = jnp.exp(tot - cs) * dt
        rows_ref[0, c] = jnp.concatenate([cs, dt], axis=0)
        table = jnp.concatenate(_split(cs, 3) + _split(e1, 2) + _split(w, 2) + [zero], axis=0)
        colss.append(table.T.astype(BF16))
        cols_ref[0, c] = colss[c]
        bt = [slabs[4 + g][tok].T.astype(BF16) for g in range(SSD_GROUPS)]
        for g in range(SSD_GROUPS):
            bt_ref[0, c, g] = bt[g]
        bts.append(bt)

    state = [st_ref[g] for g in range(SSD_GROUPS)]
    for c in reversed(range(NCH)):
        tok = slice(c * SSD_CHUNK, (c + 1) * SSD_CHUNK)
        cols = colss[c]
        wxb = jnp.dot(cols, exw_ref[:, SSD_WIDTH:], preferred_element_type=F32)
        dec = jnp.dot(cols[0:16], exe_ref[:, SSD_WIDTH:], preferred_element_type=F32)[0:1]
        xw = (xs[tok] * wxb).astype(BF16)
        for g in range(SSD_GROUPS):
            gs = slice(g * GROUP_COLS, (g + 1) * GROUP_COLS)
            sb_ref[0, c, g] = state[g].astype(BF16)
            local = jnp.dot(bts[c][g], xw[:, gs], preferred_element_type=F32)
            state[g] = state[g] * dec[:, gs] + local
    for g in range(SSD_GROUPS):
        st_ref[g] = state[g]


def _proj_call(x, mod_lat, lng, lnb, w_in, cw, scw, scb, dtb, alog, exe, exw, u_chunk, h0):
    rev = lambda b, t: (b, NT - 1 - t, 0)
    rev4 = lambda b, t: (b, NT - 1 - t, 0, 0)
    rev5 = lambda b, t: (b, NT - 1 - t, 0, 0, 0)
    tok = lambda width: pl.BlockSpec((1, TM, width), rev)
    out_shape = [
        jax.ShapeDtypeStruct((BATCH, SEQ, CONV_WIDTH), BF16),
        jax.ShapeDtypeStruct((BATCH, SEQ, SSD_WIDTH), BF16),
        jax.ShapeDtypeStruct((BATCH, SEQ, SSD_WIDTH), BF16),
        jax.ShapeDtypeStruct((BATCH, NCHUNK, SSD_GROUPS, SSD_STATE, SSD_CHUNK), BF16),
        jax.ShapeDtypeStruct((BATCH, SEQ, SSD_GN), BF16),
        jax.ShapeDtypeStruct((BATCH, NCHUNK, 2 * N_DH, SSD_CHUNK), F32),
        jax.ShapeDtypeStruct((BATCH, NCHUNK, SSD_CHUNK, LANES), BF16),
        jax.ShapeDtypeStruct((BATCH, NCHUNK, SSD_GROUPS, SSD_STATE, GROUP_COLS), BF16),
    ]
    out_specs = [
        tok(CONV_WIDTH), tok(SSD_WIDTH), tok(SSD_WIDTH),
        pl.BlockSpec((1, NCH, SSD_GROUPS, SSD_STATE, SSD_CHUNK), rev5),
        tok(SSD_GN),
        pl.BlockSpec((1, NCH, 2 * N_DH, SSD_CHUNK), rev4),
        pl.BlockSpec((1, NCH, SSD_CHUNK, LANES), rev4),
        pl.BlockSpec((1, NCH, SSD_GROUPS, SSD_STATE, GROUP_COLS), rev5),
    ]
    in_specs = [
        pl.BlockSpec((1, TM, D_MODEL), rev),
        pl.BlockSpec((1, 6, D_MODEL), lambda b, t: (b, 0, 0)),
        _const_spec((1, D_MODEL)), _const_spec((1, D_MODEL)),
        _const_spec((D_MODEL, IN_PAD)),
        _const_spec((3, CONV_WIDTH)), _const_spec((3, XBC_DIM)), _const_spec((1, XBC_DIM)),
        _const_spec((N_DH, 1)), _const_spec((N_DH, 1)),
        _const_spec((LANES, 2 * SSD_WIDTH)), _const_spec((LANES, 2 * SSD_WIDTH)),
        _const_spec((SSD_CHUNK, 3 * SSD_CHUNK)),
        pl.BlockSpec((1, 1, SSD_GROUPS, SSD_STATE, GROUP_COLS), lambda b, t: (b, 1, 0, 0, 0)),
    ]
    return pl.pallas_call(
        _proj_kernel,
        grid=(BATCH, NT),
        in_specs=in_specs,
        out_specs=out_specs,
        out_shape=out_shape,
        scratch_shapes=[pltpu.VMEM((SSD_GROUPS, SSD_STATE, GROUP_COLS), F32)],
        compiler_params=pltpu.CompilerParams(dimension_semantics=("arbitrary", "arbitrary"),
                                             vmem_limit_bytes=VMEM_LIMIT),
        name="proj",
    )(x, mod_lat, lng, lnb, w_in, cw, scw, scb, dtb, alog, exe, exw, u_chunk, h0)


def _out_kernel(x_ref, mod_ref, lng_ref, lnb_ref, ycv_ref, zg_ref, xs_ref, bt_ref, cm_ref, rows_ref,
                cols_ref, sb_ref, h0_ref, eb_ref, exe_ref, exw_ref, dx_ref, nw_ref, wout_ref,
                ln1g_ref, ln1b_ref, wff1_ref, wff2_ref, ln2g_ref, ln2b_ref,
                out_ref, st_ref, yn_ref):
    @pl.when(pl.program_id(1) == 0)
    def _():
        st_ref[...] = h0_ref[0, 0]

    li = lax.broadcasted_iota(jnp.int32, (SSD_CHUNK, SSD_CHUNK), 0)
    si = lax.broadcasted_iota(jnp.int32, (SSD_CHUNK, SSD_CHUNK), 1)
    low = li >= si
    diag = li == si
    lo_half = si < SSD_HEADDIM

    state = [st_ref[g] for g in range(SSD_GROUPS)]
    for c in range(NCH):
        tok = slice(c * SSD_CHUNK, (c + 1) * SSD_CHUNK)
        rows = rows_ref[0, c]
        cols = cols_ref[0, c]
        xs = xs_ref[0, tok, :]
        cm = cm_ref[0, tok, :]
        bc = jnp.dot(cols, eb_ref[...], preferred_element_type=F32)
        e1x = jnp.dot(cols, exe_ref[...], preferred_element_type=F32)
        wxf = jnp.dot(cols, exw_ref[:, :SSD_WIDTH], preferred_element_type=F32)
        dec = jnp.dot(cols[SSD_CHUNK - 16:], exe_ref[:, :SSD_WIDTH],
                      preferred_element_type=F32)[15:16]
        gmat = [jnp.dot(cm[:, g * SSD_STATE:(g + 1) * SSD_STATE], bt_ref[0, c, g],
                        preferred_element_type=F32) for g in range(SSD_GROUPS)]

        ys = []
        for k in range(SSD_HEADS // 2):
            ms = []
            for h in (2 * k, 2 * k + 1):
                g = h // (SSD_HEADS // SSD_GROUPS)
                hb = SSD_HEADS + h
                arg = jnp.where(low,
                                bc[:, h * LANES:(h + 1) * LANES] - rows[h:h + 1, :],
                                bc[:, hb * LANES:(hb + 1) * LANES] - rows[hb:hb + 1, :])
                dtf = rows[N_DH + h:N_DH + h + 1, :]
                dtb = rows[N_DH + hb:N_DH + hb + 1, :]
                fac = jnp.where(low, dtf, dtb) + jnp.where(diag, dtb, 0.0)
                ms.append((gmat[g] * jnp.exp(arg) * fac).astype(BF16))
            xp = xs[:, k * LANES:(k + 1) * LANES]
            rhs = jnp.concatenate([jnp.where(lo_half, xp, jnp.zeros_like(xp)),
                                   jnp.where(lo_half, jnp.zeros_like(xp), xp)], axis=0)
            ys.append(jnp.dot(jnp.concatenate(ms, axis=1), rhs, preferred_element_type=F32))
        y = jnp.concatenate(ys, axis=1)

        yf, yb = [], []
        for g in range(SSD_GROUPS):
            cg = cm[:, g * SSD_STATE:(g + 1) * SSD_STATE]
            yf.append(jnp.dot(cg, state[g].astype(BF16), preferred_element_type=F32))
            yb.append(jnp.dot(cg, sb_ref[0, c, g], preferred_element_type=F32))
        y = (y + jnp.concatenate(yf, axis=1) * e1x[:, :SSD_WIDTH]
             + jnp.concatenate(yb, axis=1) * e1x[:, SSD_WIDTH:]
             + xs.astype(F32) * dx_ref[...])

        yg = y * zg_ref[0, tok, :].astype(F32)
        ms_ = jnp.mean(yg * yg, axis=-1, keepdims=True)
        yn_ref[tok, :] = (yg * lax.rsqrt(ms_ + RMS_EPS) * nw_ref[...]).astype(BF16)

        xw = (xs.astype(F32) * wxf).astype(BF16)
        for g in range(SSD_GROUPS):
            gs = slice(g * GROUP_COLS, (g + 1) * GROUP_COLS)
            local = jnp.dot(bt_ref[0, c, g], xw[:, gs], preferred_element_type=F32)
            state[g] = state[g] * dec[:, gs] + local

    for g in range(SSD_GROUPS):
        st_ref[g] = state[g]

    mix = (jnp.dot(ycv_ref[0], wout_ref[:CONV_WIDTH, :], preferred_element_type=F32)
           + jnp.dot(yn_ref[...], wout_ref[CONV_WIDTH:, :], preferred_element_type=F32))
    m = mod_ref[0]
    h = _ln_hat(x_ref[0]) * lng_ref[...] + lnb_ref[...]
    h1 = _ln_hat(ALPHA * h + m[2:3] * mix) * ln1g_ref[...] + ln1b_ref[...]
    u2 = (h1 * (1.0 + m[4:5]) + m[3:4]).astype(BF16)
    acc = jnp.zeros((TM, D_MODEL), F32)
    for j in range(D_FF // FF_BLK):
        hid = jnp.maximum(jnp.dot(u2, wff1_ref[:, j * FF_BLK:(j + 1) * FF_BLK],
                                  preferred_element_type=F32), 0.0)
        acc = acc + jnp.dot((hid * hid).astype(BF16), wff2_ref[j * FF_BLK:(j + 1) * FF_BLK, :],
                            preferred_element_type=F32)
    out_ref[0] = _ln_hat(ALPHA * h1 + m[5:6] * acc) * ln2g_ref[...] + ln2b_ref[...]


def _out_call(x, mod_lat, lng, lnb, ycv, zg, xs, bt, cm, rows, cols, sb, h0, eb, exe, exw, dx, nw,
              wout, ln1g, ln1b, wff1, wff2, ln2g, ln2b):
    fwd = lambda b, t: (b, t, 0)
    fwd4 = lambda b, t: (b, t, 0, 0)
    fwd5 = lambda b, t: (b, t, 0, 0, 0)
    tok = lambda width: pl.BlockSpec((1, TM, width), fwd)
    in_specs = [
        tok(D_MODEL),
        pl.BlockSpec((1, 6, D_MODEL), lambda b, t: (b, 0, 0)),
        _const_spec((1, D_MODEL)), _const_spec((1, D_MODEL)),
        tok(CONV_WIDTH), tok(SSD_WIDTH), tok(SSD_WIDTH),
        pl.BlockSpec((1, NCH, SSD_GROUPS, SSD_STATE, SSD_CHUNK), fwd5),
        tok(SSD_GN),
        pl.BlockSpec((1, NCH, 2 * N_DH, SSD_CHUNK), fwd4),
        pl.BlockSpec((1, NCH, SSD_CHUNK, LANES), fwd4),
        pl.BlockSpec((1, NCH, SSD_GROUPS, SSD_STATE, GROUP_COLS), fwd5),
        pl.BlockSpec((1, 1, SSD_GROUPS, SSD_STATE, GROUP_COLS), lambda b, t: (b, 0, 0, 0, 0)),
        _const_spec((LANES, N_DH * LANES)),
        _const_spec((LANES, 2 * SSD_WIDTH)), _const_spec((LANES, 2 * SSD_WIDTH)),
        _const_spec((1, SSD_WIDTH)), _const_spec((1, SSD_WIDTH)),
        _const_spec((D_MODEL, D_MODEL)),
        _const_spec((1, D_MODEL)), _const_spec((1, D_MODEL)),
        _const_spec((D_MODEL, D_FF)), _const_spec((D_FF, D_MODEL)),
        _const_spec((1, D_MODEL)), _const_spec((1, D_MODEL)),
    ]
    return pl.pallas_call(
        _out_kernel,
        grid=(BATCH, NT),
        in_specs=in_specs,
        out_specs=tok(D_MODEL),
        out_shape=jax.ShapeDtypeStruct((BATCH, SEQ, D_MODEL), F32),
        scratch_shapes=[pltpu.VMEM((SSD_GROUPS, SSD_STATE, GROUP_COLS), F32),
                        pltpu.VMEM((TM, SSD_WIDTH), BF16)],
        compiler_params=pltpu.CompilerParams(dimension_semantics=("arbitrary", "arbitrary"),
                                             vmem_limit_bytes=VMEM_LIMIT),
        name="out",
    )(x, mod_lat, lng, lnb, ycv, zg, xs, bt, cm, rows, cols, sb, h0, eb, exe, exw, dx, nw,
      wout, ln1g, ln1b, wff1, wff2, ln2g, ln2b)


def kernel(x, c, ctx, c_ctx, ln_in_g, ln_in_b, w_mod, b_mod, w_in, conv_w, ssd_conv_w, ssd_conv_b,
           dt_bias, a_log, ssd_d, ssd_norm_w, w_out, ln1_g, ln1_b, w_ff1, w_ff2, ln2_g, ln2_b):
    row = lambda v: v.reshape(1, -1).astype(F32)
    cvec = jnp.concatenate([c, c_ctx[None, :], jnp.zeros((16 - BATCH - 1, D_MODEL), F32)], axis=0)
    mod = _mod_call(cvec, w_mod[0], row(b_mod[0]))
    mod_lat = mod[:BATCH].reshape(BATCH, 6, D_MODEL)
    mod_ctx = mod[BATCH:BATCH + 1].reshape(1, 6, D_MODEL)

    w_in_p, w_out_p, w_ff1_p, w_ff2_p = _wprep_call(w_in[0].T, w_out, w_ff1, w_ff2)
    lng, lnb = row(ln_in_g), row(ln_in_b)
    scw, scb = ssd_conv_w[0], row(ssd_conv_b[0])
    dtb = dt_bias[0].reshape(N_DH, 1)
    alog = a_log[0].reshape(N_DH, 1)
    eb = jnp.asarray(_EB, BF16)
    exe = jnp.asarray(_EXE, BF16)
    exw = jnp.asarray(_EXW, BF16)

    h0 = _ctx_call(ctx, mod_ctx, lng, lnb, w_in_p, scw, scb, dtb, alog, exw, jnp.asarray(_U_CTX, BF16))
    ycv, zg, xs, bt, cm, rows, cols, sb = _proj_call(
        x, mod_lat, lng, lnb, w_in_p, conv_w[0], scw, scb, dtb, alog, exe, exw,
        jnp.asarray(_U_CHUNK, BF16), h0)
    dx = jnp.repeat(ssd_d[0], SSD_HEADDIM).reshape(1, SSD_WIDTH)
    return _out_call(x, mod_lat, lng, lnb, ycv, zg, xs, bt, cm, rows, cols, sb, h0, eb, exe, exw,
                     dx, row(ssd_norm_w[0]), w_out_p, row(ln1_g[0]), row(ln1_b[0]),
                     w_ff1_p, w_ff2_p, row(ln2_g[0]), row(ln2_b[0]))
```

```python
import jax
import jax.numpy as jnp
import numpy as np
from jax import lax
from jax.experimental import pallas as pl
from jax.experimental.pallas import tpu as pltpu

F32 = jnp.float32
BF16 = jnp.bfloat16

D_MODEL = 1024
BATCH = 8
SEQ = 2048
CTX_LEN = 256
GRID_W = 64
CONV_WIDTH = 512
SSD_WIDTH = 512
SSD_HEADDIM = 64
SSD_HEADS = 8
SSD_GROUPS = 2
SSD_STATE = 128
SSD_CHUNK = 128
N_DIRS = 2
D_FF = 4 * D_MODEL
LN_EPS = 1e-5
RMS_EPS = 1e-5
SSD_GN = SSD_GROUPS * SSD_STATE
XBC_DIM = SSD_WIDTH + 2 * SSD_GN
Z_OFF = 3 * CONV_WIDTH
XBC_OFF = Z_OFF + SSD_WIDTH
DT_OFF = XBC_OFF + XBC_DIM
N_DH = N_DIRS * SSD_HEADS
LANES = 128
P_DT = 0
P_XBC = LANES
P_Z = P_XBC + XBC_DIM
P_CONV = P_Z + SSD_WIDTH
IN_PAD = P_CONV + 3 * CONV_WIDTH
CTX_PAD = P_Z
GROUP_COLS = (SSD_HEADS // SSD_GROUPS) * SSD_HEADDIM
ALPHA = 2.0 ** 0.25

TM = 512
NCH = TM // SSD_CHUNK
NT = SEQ // TM
NCHUNK = SEQ // SSD_CHUNK
FF_BLK = 1024
VMEM_LIMIT = 58 * 1024 * 1024

COL_E1 = 48
COL_W = 80
COL_CSF = 112


def _expansion(col0, pieces, width):
    m = np.zeros((LANES, N_DH * width), np.float32)
    for t in range(pieces):
        for j in range(N_DH):
            m[col0 + 16 * t + j, j * width:(j + 1) * width] = 1.0
    return m


_EXE = _expansion(COL_E1, 2, SSD_HEADDIM)
_EXW = _expansion(COL_W, 2, SSD_HEADDIM)


def _ln_hat(x):
    mu = jnp.mean(x, axis=-1, keepdims=True)
    xc = x - mu
    var = jnp.mean(xc * xc, axis=-1, keepdims=True)
    return xc * lax.rsqrt(var + LN_EPS)


def _silu(x):
    return x / (1.0 + jnp.exp(-x))


def _softplus(x):
    return jnp.maximum(x, 0.0) + jnp.log1p(jnp.exp(-jnp.abs(x)))


def _edge_masks(rows, period):
    pos = lax.broadcasted_iota(jnp.int32, (rows, LANES), 0) % period
    return (pos != 0).astype(F32), (pos != period - 1).astype(F32)


def _conv3(t, w, mprev, mnext):
    rows = t.shape[0]
    prev = pltpu.roll(t, 1, 0) * mprev
    nxt = pltpu.roll(t, rows - 1, 0) * mnext
    return prev * w[0:1, :] + t * w[1:2, :] + nxt * w[2:3, :]


def _split(v, pieces):
    out = []
    for _ in range(pieces - 1):
        p = v.astype(BF16).astype(F32)
        out.append(p)
        v = v - p
    out.append(v.astype(BF16).astype(F32))
    return out


def _tri(length, op):
    i = np.arange(length)
    return op(i[:, None], i[None, :]).astype(np.float32)


_U_CHUNK = np.concatenate([_tri(SSD_CHUNK, np.less_equal), _tri(SSD_CHUNK, np.greater_equal),
                           np.ones((SSD_CHUNK, SSD_CHUNK), np.float32)], axis=1)
_U_CTX = np.concatenate([_tri(CTX_LEN, np.greater), _tri(CTX_LEN, np.less)], axis=1)


def _scan_mm(v, u_ref):
    pieces = jnp.concatenate(_split(v, 3), axis=0).astype(BF16)
    o = jnp.dot(pieces, u_ref[...], preferred_element_type=F32)
    return o[0:N_DH] + o[N_DH:2 * N_DH] + o[2 * N_DH:3 * N_DH]


def _dt_rows(raw, dtb, a_log):
    r = raw.T[0:N_DH, :] + dtb
    dt = _softplus(r)
    return dt, dt * (-jnp.exp(a_log))


def _fwd_rows():
    return lax.broadcasted_iota(jnp.int32, (N_DH, 1), 0) < SSD_HEADS


def _const_spec(shape):
    nd = len(shape)
    return pl.BlockSpec(shape, lambda *_: (0,) * nd, pipeline_mode=pl.Buffered(1))


def _mod_kernel(c_ref, w_ref, b_ref, o_ref):
    s = _silu(c_ref[...]).astype(BF16)
    o_ref[...] = jnp.dot(s, w_ref[...].astype(BF16), preferred_element_type=F32) + b_ref[...]


def _mod_call(cvec, w_mod, b_mod):
    tn = 1536
    return pl.pallas_call(
        _mod_kernel,
        grid=(6 * D_MODEL // tn,),
        in_specs=[pl.BlockSpec((16, D_MODEL), lambda j: (0, 0)),
                  pl.BlockSpec((D_MODEL, tn), lambda j: (0, j)),
                  pl.BlockSpec((1, tn), lambda j: (0, j))],
        out_specs=pl.BlockSpec((16, tn), lambda j: (0, j)),
        out_shape=jax.ShapeDtypeStruct((16, 6 * D_MODEL), F32),
        compiler_params=pltpu.CompilerParams(dimension_semantics=("arbitrary",),
                                             vmem_limit_bytes=VMEM_LIMIT),
        name="mod",
    )(cvec, w_mod, b_mod)


def _wprep_kernel(wint_ref, wout_ref, wff1_ref, wff2_ref, pin_ref, pout_ref, pff1_ref, pff2_ref):
    dt_rows = jnp.concatenate([wint_ref[DT_OFF:DT_OFF + N_DH, :],
                               jnp.zeros((LANES - N_DH, wint_ref.shape[1]), F32)], axis=0)
    pin_ref[:, P_DT:P_DT + LANES] = dt_rows.T.astype(BF16)
    for dst, src, width in ((P_XBC, XBC_OFF, XBC_DIM), (P_Z, Z_OFF, SSD_WIDTH), (P_CONV, 0, Z_OFF)):
        for j in range(0, width, LANES):
            pin_ref[:, dst + j:dst + j + LANES] = wint_ref[src + j:src + j + LANES, :].T.astype(BF16)
    pout_ref[...] = wout_ref[0].astype(BF16)
    pff1_ref[...] = wff1_ref[0].astype(BF16)
    pff2_ref[...] = wff2_ref[0].astype(BF16)


def _wprep_call(w_in_t, w_out, w_ff1, w_ff2):
    steps = 8
    r1, r4 = D_MODEL // steps, D_FF // steps
    return pl.pallas_call(
        _wprep_kernel,
        grid=(steps,),
        in_specs=[pl.BlockSpec((w_in_t.shape[0], r1), lambda i: (0, i)),
                  pl.BlockSpec((1, r1, D_MODEL), lambda i: (0, i, 0)),
                  pl.BlockSpec((1, r1, D_FF), lambda i: (0, i, 0)),
                  pl.BlockSpec((1, r4, D_MODEL), lambda i: (0, i, 0))],
        out_specs=[pl.BlockSpec((r1, IN_PAD), lambda i: (i, 0)),
                   pl.BlockSpec((r1, D_MODEL), lambda i: (i, 0)),
                   pl.BlockSpec((r1, D_FF), lambda i: (i, 0)),
                   pl.BlockSpec((r4, D_MODEL), lambda i: (i, 0))],
        out_shape=[jax.ShapeDtypeStruct((D_MODEL, IN_PAD), BF16),
                   jax.ShapeDtypeStruct((D_MODEL, D_MODEL), BF16),
                   jax.ShapeDtypeStruct((D_MODEL, D_FF), BF16),
                   jax.ShapeDtypeStruct((D_FF, D_MODEL), BF16)],
        compiler_params=pltpu.CompilerParams(dimension_semantics=("arbitrary",),
                                             vmem_limit_bytes=VMEM_LIMIT),
        name="wprep",
    )(w_in_t, w_out, w_ff1, w_ff2)


def _ctx_kernel(x_ref, mod_ref, lng_ref, lnb_ref, w_ref, scw_ref, scb_ref, dtb_ref, alog_ref,
                exw_ref, u_ref, h0_ref):
    m = mod_ref[0]
    sc = 1.0 + m[1:2]
    u = _ln_hat(x_ref[0]) * (lng_ref[...] * sc) + (lnb_ref[...] * sc + m[0:1])
    proj = jnp.dot(u.astype(BF16), w_ref[...], preferred_element_type=F32)
    mprev, mnext = _edge_masks(CTX_LEN, CTX_LEN)
    slabs = []
    for j in range((SSD_WIDTH + SSD_GN) // LANES):
        sl = slice(j * LANES, (j + 1) * LANES)
        pj = proj[:, P_XBC + j * LANES:P_XBC + (j + 1) * LANES]
        slabs.append(_silu(_conv3(pj, scw_ref[:, sl], mprev, mnext) + scb_ref[:, sl]))
    xs = jnp.concatenate(slabs[:4], axis=1)
    dt, adt = _dt_rows(proj[:, P_DT:P_DT + LANES], dtb_ref[...], alog_ref[...])
    sc2 = _scan_mm(adt, u_ref)
    excl = jnp.where(_fwd_rows(), sc2[:, :CTX_LEN], sc2[:, CTX_LEN:])
    w = jnp.exp(excl) * dt
    zero = jnp.zeros((N_DH, CTX_LEN), F32)
    table = jnp.concatenate([zero] * (COL_W // 16) + _split(w, 2) + [zero], axis=0)
    cols = table.T.astype(BF16)
    wx = jnp.dot(cols, exw_ref[...], preferred_element_type=F32)
    for d in range(N_DIRS):
        xw = (xs * wx[:, d * SSD_WIDTH:(d + 1) * SSD_WIDTH]).astype(BF16)
        for g in range(SSD_GROUPS):
            bt = slabs[4 + g].T.astype(BF16)
            h0_ref[0, d, g] = jnp.dot(bt, xw[:, g * GROUP_COLS:(g + 1) * GROUP_COLS],
                                      preferred_element_type=F32)


def _ctx_call(ctx, mod_ctx, lng, lnb, w_in, scw, scb, dtb, alog, exw, u_ctx):
    return pl.pallas_call(
        _ctx_kernel,
        grid=(BATCH,),
        in_specs=[pl.BlockSpec((1, CTX_LEN, D_MODEL), lambda b: (b, 0, 0)),
                  _const_spec((1, 6, D_MODEL)), _const_spec((1, D_MODEL)), _const_spec((1, D_MODEL)),
                  _const_spec((D_MODEL, CTX_PAD)), _const_spec((3, XBC_DIM)), _const_spec((1, XBC_DIM)),
                  _const_spec((N_DH, 1)), _const_spec((N_DH, 1)), _const_spec((LANES, 2 * SSD_WIDTH)),
                  _const_spec((CTX_LEN, 2 * CTX_LEN))],
        out_specs=pl.BlockSpec((1, N_DIRS, SSD_GROUPS, SSD_STATE, GROUP_COLS),
                               lambda b: (b, 0, 0, 0, 0)),
        out_shape=jax.ShapeDtypeStruct((BATCH, N_DIRS, SSD_GROUPS, SSD_STATE, GROUP_COLS), F32),
        compiler_params=pltpu.CompilerParams(dimension_semantics=("arbitrary",),
                                             vmem_limit_bytes=VMEM_LIMIT),
        name="ctx",
    )(ctx, mod_ctx, lng, lnb, w_in, scw, scb, dtb, alog, exw, u_ctx)


def _proj_kernel(x_ref, mod_ref, lng_ref, lnb_ref, w_ref, cw_ref, scw_ref, scb_ref, dtb_ref,
                 alog_ref, exe_ref, exw_ref, u_ref, h0_ref,
                 ycv_ref, zg_ref, xs_ref, bt_ref, cm_ref, rows_ref, cols_ref, colf_ref, sb_ref, st_ref):
    @pl.when(pl.program_id(1) == 0)
    def _():
        st_ref[...] = h0_ref[0, 0]

    m = mod_ref[0]
    sc = 1.0 + m[1:2]
    u = _ln_hat(x_ref[0]) * (lng_ref[...] * sc) + (lnb_ref[...] * sc + m[0:1])
    proj = jnp.dot(u.astype(BF16), w_ref[...], preferred_element_type=F32)
    mprev, mnext = _edge_masks(TM, GRID_W)

    for j in range(CONV_WIDTH // LANES):
        sl = slice(j * LANES, (j + 1) * LANES)
        gb = proj[:, P_CONV + j * LANES:P_CONV + (j + 1) * LANES]
        gc = proj[:, P_CONV + CONV_WIDTH + j * LANES:P_CONV + CONV_WIDTH + (j + 1) * LANES]
        gh = proj[:, P_CONV + 2 * CONV_WIDTH + j * LANES:P_CONV + 2 * CONV_WIDTH + (j + 1) * LANES]
        ycv_ref[0, :, sl] = (gb * _conv3(gc * gh, cw_ref[:, sl], mprev, mnext)).astype(BF16)

    zg_ref[0] = _silu(proj[:, P_Z:P_CONV]).astype(BF16)

    slabs = []
    for j in range(XBC_DIM // LANES):
        sl = slice(j * LANES, (j + 1) * LANES)
        pj = proj[:, P_XBC + j * LANES:P_XBC + (j + 1) * LANES]
        slabs.append(_silu(_conv3(pj, scw_ref[:, sl], mprev, mnext) + scb_ref[:, sl]))
    xs = jnp.concatenate(slabs[:4], axis=1)
    xs_ref[0] = xs.astype(BF16)
    cm_ref[0] = jnp.concatenate(slabs[6:8], axis=1).astype(BF16)

    raw = proj[:, P_DT:P_DT + LANES]
    is_fwd = _fwd_rows()
    zero = jnp.zeros((N_DH, SSD_CHUNK), F32)
    bts, colss = [], []
    for c in range(NCH):
        tok = slice(c * SSD_CHUNK, (c + 1) * SSD_CHUNK)
        dt, adt = _dt_rows(raw[tok], dtb_ref[...], alog_ref[...])
        sc3 = _scan_mm(adt, u_ref)
        cs = jnp.where(is_fwd, sc3[:, :SSD_CHUNK], sc3[:, SSD_CHUNK:2 * SSD_CHUNK])
        tot = sc3[:, 2 * SSD_CHUNK:]
        e1 = jnp.exp(cs)
        w = jnp.exp(tot - cs) * dt
        rows_ref[0, c] = jnp.concatenate([cs, dt], axis=0)
        table = jnp.concatenate([zero] * 3 + _split(e1, 2) + _split(w, 2) + [cs], axis=0).T
        colf_ref[0, c] = table
        colss.append(table.astype(BF16))
        cols_ref[0, c] = colss[c]
        bt = [slabs[4 + g][tok].T.astype(BF16) for g in range(SSD_GROUPS)]
        for g in range(SSD_GROUPS):
            bt_ref[0, c, g] = bt[g]
        bts.append(bt)

    state = [st_ref[g] for g in range(SSD_GROUPS)]
    for c in reversed(range(NCH)):
        tok = slice(c * SSD_CHUNK, (c + 1) * SSD_CHUNK)
        cols = colss[c]
        wxb = jnp.dot(cols, exw_ref[:, SSD_WIDTH:], preferred_element_type=F32)
        dec = jnp.dot(cols[0:16], exe_ref[:, SSD_WIDTH:], preferred_element_type=F32)[0:1]
        xw = (xs[tok] * wxb).astype(BF16)
        for g in range(SSD_GROUPS):
            gs = slice(g * GROUP_COLS, (g + 1) * GROUP_COLS)
            sb_ref[0, c, g] = state[g].astype(BF16)
            local = jnp.dot(bts[c][g], xw[:, gs], preferred_element_type=F32)
            state[g] = state[g] * dec[:, gs] + local
    for g in range(SSD_GROUPS):
        st_ref[g] = state[g]


def _proj_call(x, mod_lat, lng, lnb, w_in, cw, scw, scb, dtb, alog, exe, exw, u_chunk, h0):
    rev = lambda b, t: (b, NT - 1 - t, 0)
    rev4 = lambda b, t: (b, NT - 1 - t, 0, 0)
    rev5 = lambda b, t: (b, NT - 1 - t, 0, 0, 0)
    tok = lambda width: pl.BlockSpec((1, TM, width), rev)
    out_shape = [
        jax.ShapeDtypeStruct((BATCH, SEQ, CONV_WIDTH), BF16),
        jax.ShapeDtypeStruct((BATCH, SEQ, SSD_WIDTH), BF16),
        jax.ShapeDtypeStruct((BATCH, SEQ, SSD_WIDTH), BF16),
        jax.ShapeDtypeStruct((BATCH, NCHUNK, SSD_GROUPS, SSD_STATE, SSD_CHUNK), BF16),
        jax.ShapeDtypeStruct((BATCH, SEQ, SSD_GN), BF16),
        jax.ShapeDtypeStruct((BATCH, NCHUNK, 2 * N_DH, SSD_CHUNK), F32),
        jax.ShapeDtypeStruct((BATCH, NCHUNK, SSD_CHUNK, LANES), BF16),
        jax.ShapeDtypeStruct((BATCH, NCHUNK, SSD_CHUNK, LANES), F32),
        jax.ShapeDtypeStruct((BATCH, NCHUNK, SSD_GROUPS, SSD_STATE, GROUP_COLS), BF16),
    ]
    out_specs = [
        tok(CONV_WIDTH), tok(SSD_WIDTH), tok(SSD_WIDTH),
        pl.BlockSpec((1, NCH, SSD_GROUPS, SSD_STATE, SSD_CHUNK), rev5),
        tok(SSD_GN),
        pl.BlockSpec((1, NCH, 2 * N_DH, SSD_CHUNK), rev4),
        pl.BlockSpec((1, NCH, SSD_CHUNK, LANES), rev4),
        pl.BlockSpec((1, NCH, SSD_CHUNK, LANES), rev4),
        pl.BlockSpec((1, NCH, SSD_GROUPS, SSD_STATE, GROUP_COLS), rev5),
    ]
    in_specs = [
        pl.BlockSpec((1, TM, D_MODEL), rev),
        pl.BlockSpec((1, 6, D_MODEL), lambda b, t: (b, 0, 0)),
        _const_spec((1, D_MODEL)), _const_spec((1, D_MODEL)),
        _const_spec((D_MODEL, IN_PAD)),
        _const_spec((3, CONV_WIDTH)), _const_spec((3, XBC_DIM)), _const_spec((1, XBC_DIM)),
        _const_spec((N_DH, 1)), _const_spec((N_DH, 1)),
        _const_spec((LANES, 2 * SSD_WIDTH)), _const_spec((LANES, 2 * SSD_WIDTH)),
        _const_spec((SSD_CHUNK, 3 * SSD_CHUNK)),
        pl.BlockSpec((1, 1, SSD_GROUPS, SSD_STATE, GROUP_COLS), lambda b, t: (b, 1, 0, 0, 0)),
    ]
    return pl.pallas_call(
        _proj_kernel,
        grid=(BATCH, NT),
        in_specs=in_specs,
        out_specs=out_specs,
        out_shape=out_shape,
        scratch_shapes=[pltpu.VMEM((SSD_GROUPS, SSD_STATE, GROUP_COLS), F32)],
        compiler_params=pltpu.CompilerParams(dimension_semantics=("arbitrary", "arbitrary"),
                                             vmem_limit_bytes=VMEM_LIMIT),
        name="proj",
    )(x, mod_lat, lng, lnb, w_in, cw, scw, scb, dtb, alog, exe, exw, u_chunk, h0)


def _out_kernel(x_ref, mod_ref, lng_ref, lnb_ref, ycv_ref, zg_ref, xs_ref, bt_ref, cm_ref, rows_ref,
                cols_ref, colf_ref, sb_ref, h0_ref, exw_ref, dx_ref, nw_ref, wout_ref,
                ln1g_ref, ln1b_ref, wff1_ref, wff2_ref, ln2g_ref, ln2b_ref,
                out_ref, st_ref, yn_ref):
    @pl.when(pl.program_id(1) == 0)
    def _():
        st_ref[...] = h0_ref[0, 0]

    li = lax.broadcasted_iota(jnp.int32, (SSD_CHUNK, SSD_CHUNK), 0)
    si = lax.broadcasted_iota(jnp.int32, (SSD_CHUNK, SSD_CHUNK), 1)
    low = li >= si
    diag = li == si
    lo_half = si < SSD_HEADDIM

    state = [st_ref[g] for g in range(SSD_GROUPS)]
    for c in range(NCH):
        tok = slice(c * SSD_CHUNK, (c + 1) * SSD_CHUNK)
        rows = rows_ref[0, c]
        cols = cols_ref[0, c]
        xs = xs_ref[0, tok, :]
        cm = cm_ref[0, tok, :]
        colf = colf_ref[0, c]
        bc = [jnp.broadcast_to(colf[:, COL_CSF + j:COL_CSF + j + 1], (SSD_CHUNK, LANES))
              for j in range(N_DH)]
        e1x = [jnp.concatenate(
            [jnp.exp(jnp.where(lo_half, bc[d * SSD_HEADS + 2 * k], bc[d * SSD_HEADS + 2 * k + 1]))
             for k in range(SSD_HEADS // 2)], axis=1) for d in range(N_DIRS)]
        wxf = jnp.dot(cols, exw_ref[:, :SSD_WIDTH], preferred_element_type=F32)
        dec = e1x[0][SSD_CHUNK - 1:SSD_CHUNK, :]
        gmat = [jnp.dot(cm[:, g * SSD_STATE:(g + 1) * SSD_STATE], bt_ref[0, c, g],
                        preferred_element_type=F32) for g in range(SSD_GROUPS)]

        ys = []
        for k in range(SSD_HEADS // 2):
            ms = []
            for h in (2 * k, 2 * k + 1):
                g = h // (SSD_HEADS // SSD_GROUPS)
                hb = SSD_HEADS + h
                arg = jnp.where(low,
                                bc[h] - rows[h:h + 1, :],
                                bc[hb] - rows[hb:hb + 1, :])
                dtf = rows[N_DH + h:N_DH + h + 1, :]
                dtb = rows[N_DH + hb:N_DH + hb + 1, :]
                fac = jnp.where(low, dtf, dtb) + jnp.where(diag, dtb, 0.0)
                ms.append((gmat[g] * jnp.exp(arg) * fac).astype(BF16))
            xp = xs[:, k * LANES:(k + 1) * LANES]
            rhs = jnp.concatenate([jnp.where(lo_half, xp, jnp.zeros_like(xp)),
                                   jnp.where(lo_half, jnp.zeros_like(xp), xp)], axis=0)
            ys.append(jnp.dot(jnp.concatenate(ms, axis=1), rhs, preferred_element_type=F32))
        y = jnp.concatenate(ys, axis=1)

        yf, yb = [], []
        for g in range(SSD_GROUPS):
            cg = cm[:, g * SSD_STATE:(g + 1) * SSD_STATE]
            yf.append(jnp.dot(cg, state[g].astype(BF16), preferred_element_type=F32))
            yb.append(jnp.dot(cg, sb_ref[0, c, g], preferred_element_type=F32))
        y = (y + jnp.concatenate(yf, axis=1) * e1x[0]
             + jnp.concatenate(yb, axis=1) * e1x[1]
             + xs.astype(F32) * dx_ref[...])

        yg = y * zg_ref[0, tok, :].astype(F32)
        ms_ = jnp.mean(yg * yg, axis=-1, keepdims=True)
        yn_ref[tok, :] = (yg * lax.rsqrt(ms_ + RMS_EPS) * nw_ref[...]).astype(BF16)

        xw = (xs.astype(F32) * wxf).astype(BF16)
        for g in range(SSD_GROUPS):
            gs = slice(g * GROUP_COLS, (g + 1) * GROUP_COLS)
            local = jnp.dot(bt_ref[0, c, g], xw[:, gs], preferred_element_type=F32)
            state[g] = state[g] * dec[:, gs] + local

    for g in range(SSD_GROUPS):
        st_ref[g] = state[g]

    mix = (jnp.dot(ycv_ref[0], wout_ref[:CONV_WIDTH, :], preferred_element_type=F32)
           + jnp.dot(yn_ref[...], wout_ref[CONV_WIDTH:, :], preferred_element_type=F32))
    m = mod_ref[0]
    h = _ln_hat(x_ref[0]) * lng_ref[...] + lnb_ref[...]
    h1 = _ln_hat(ALPHA * h + m[2:3] * mix) * ln1g_ref[...] + ln1b_ref[...]
    u2 = (h1 * (1.0 + m[4:5]) + m[3:4]).astype(BF16)
    acc = jnp.zeros((TM, D_MODEL), F32)
    for j in range(D_FF // FF_BLK):
        hid = jnp.maximum(jnp.dot(u2, wff1_ref[:, j * FF_BLK:(j + 1) * FF_BLK],
                                  preferred_element_type=F32), 0.0)
        acc = acc + jnp.dot((hid * hid).astype(BF16), wff2_ref[j * FF_BLK:(j + 1) * FF_BLK, :],
                            preferred_element_type=F32)
    out_ref[0] = _ln_hat(ALPHA * h1 + m[5:6] * acc) * ln2g_ref[...] + ln2b_ref[...]


def _out_call(x, mod_lat, lng, lnb, ycv, zg, xs, bt, cm, rows, cols, colf, sb, h0, exw, dx, nw,
              wout, ln1g, ln1b, wff1, wff2, ln2g, ln2b):
    fwd = lambda b, t: (b, t, 0)
    fwd4 = lambda b, t: (b, t, 0, 0)
    fwd5 = lambda b, t: (b, t, 0, 0, 0)
    tok = lambda width: pl.BlockSpec((1, TM, width), fwd)
    in_specs = [
        tok(D_MODEL),
        pl.BlockSpec((1, 6, D_MODEL), lambda b, t: (b, 0, 0)),
        _const_spec((1, D_MODEL)), _const_spec((1, D_MODEL)),
        tok(CONV_WIDTH), tok(SSD_WIDTH), tok(SSD_WIDTH),
        pl.BlockSpec((1, NCH, SSD_GROUPS, SSD_STATE, SSD_CHUNK), fwd5),
        tok(SSD_GN),
        pl.BlockSpec((1, NCH, 2 * N_DH, SSD_CHUNK), fwd4),
        pl.BlockSpec((1, NCH, SSD_CHUNK, LANES), fwd4),
        pl.BlockSpec((1, NCH, SSD_CHUNK, LANES), fwd4),
        pl.BlockSpec((1, NCH, SSD_GROUPS, SSD_STATE, GROUP_COLS), fwd5),
        pl.BlockSpec((1, 1, SSD_GROUPS, SSD_STATE, GROUP_COLS), lambda b, t: (b, 0, 0, 0, 0)),
        _const_spec((LANES, 2 * SSD_WIDTH)),
        _const_spec((1, SSD_WIDTH)), _const_spec((1, SSD_WIDTH)),
        _const_spec((D_MODEL, D_MODEL)),
        _const_spec((1, D_MODEL)), _const_spec((1, D_MODEL)),
        _const_spec((D_MODEL, D_FF)), _const_spec((D_FF, D_MODEL)),
        _const_spec((1, D_MODEL)), _const_spec((1, D_MODEL)),
    ]
    return pl.pallas_call(
        _out_kernel,
        grid=(BATCH, NT),
        in_specs=in_specs,
        out_specs=tok(D_MODEL),
        out_shape=jax.ShapeDtypeStruct((BATCH, SEQ, D_MODEL), F32),
        scratch_shapes=[pltpu.VMEM((SSD_GROUPS, SSD_STATE, GROUP_COLS), F32),
                        pltpu.VMEM((TM, SSD_WIDTH), BF16)],
        compiler_params=pltpu.CompilerParams(dimension_semantics=("arbitrary", "arbitrary"),
                                             vmem_limit_bytes=VMEM_LIMIT),
        name="out",
    )(x, mod_lat, lng, lnb, ycv, zg, xs, bt, cm, rows, cols, colf, sb, h0, exw, dx, nw,
      wout, ln1g, ln1b, wff1, wff2, ln2g, ln2b)


def kernel(x, c, ctx, c_ctx, ln_in_g, ln_in_b, w_mod, b_mod, w_in, conv_w, ssd_conv_w, ssd_conv_b,
           dt_bias, a_log, ssd_d, ssd_norm_w, w_out, ln1_g, ln1_b, w_ff1, w_ff2, ln2_g, ln2_b):
    row = lambda v: v.reshape(1, -1).astype(F32)
    cvec = jnp.concatenate([c, c_ctx[None, :], jnp.zeros((16 - BATCH - 1, D_MODEL), F32)], axis=0)
    mod = _mod_call(cvec, w_mod[0], row(b_mod[0]))
    mod_lat = mod[:BATCH].reshape(BATCH, 6, D_MODEL)
    mod_ctx = mod[BATCH:BATCH + 1].reshape(1, 6, D_MODEL)

    w_in_p, w_out_p, w_ff1_p, w_ff2_p = _wprep_call(w_in[0].T, w_out, w_ff1, w_ff2)
    lng, lnb = row(ln_in_g), row(ln_in_b)
    scw, scb = ssd_conv_w[0], row(ssd_conv_b[0])
    dtb = dt_bias[0].reshape(N_DH, 1)
    alog = a_log[0].reshape(N_DH, 1)
    exe = jnp.asarray(_EXE, BF16)
    exw = jnp.asarray(_EXW, BF16)

    h0 = _ctx_call(ctx, mod_ctx, lng, lnb, w_in_p, scw, scb, dtb, alog, exw, jnp.asarray(_U_CTX, BF16))
    ycv, zg, xs, bt, cm, rows, cols, colf, sb = _proj_call(
        x, mod_lat, lng, lnb, w_in_p, conv_w[0], scw, scb, dtb, alog, exe, exw,
        jnp.asarray(_U_CHUNK, BF16), h0)
    dx = jnp.repeat(ssd_d[0], SSD_HEADDIM).reshape(1, SSD_WIDTH)
    return _out_call(x, mod_lat, lng, lnb, ycv, zg, xs, bt, cm, rows, cols, colf, sb, h0, exw,
                     dx, row(ssd_norm_w[0]), w_out_p, row(ln1_g[0]), row(ln1_b[0]),
                     w_ff1_p, w_ff2_p, row(ln2_g[0]), row(ln2_b[0]))
```

```python
import jax
import jax.numpy as jnp
import numpy as np
from jax import lax
from jax.experimental import pallas as pl
from jax.experimental.pallas import tpu as pltpu

F32 = jnp.float32
BF16 = jnp.bfloat16

D_MODEL = 1024
BATCH = 8
SEQ = 2048
CTX_LEN = 256
GRID_W = 64
CONV_WIDTH = 512
SSD_WIDTH = 512
SSD_HEADDIM = 64
SSD_HEADS = 8
SSD_GROUPS = 2
SSD_STATE = 128
SSD_CHUNK = 128
N_DIRS = 2
D_FF = 4 * D_MODEL
LN_EPS = 1e-5
RMS_EPS = 1e-5
SSD_GN = SSD_GROUPS * SSD_STATE
XBC_DIM = SSD_WIDTH + 2 * SSD_GN
Z_OFF = 3 * CONV_WIDTH
XBC_OFF = Z_OFF + SSD_WIDTH
DT_OFF = XBC_OFF + XBC_DIM
N_DH = N_DIRS * SSD_HEADS
LANES = 128
P_DT = 0
P_XBC = LANES
P_Z = P_XBC + XBC_DIM
P_CONV = P_Z + SSD_WIDTH
IN_PAD = P_CONV + 3 * CONV_WIDTH
CTX_PAD = P_Z
GROUP_COLS = (SSD_HEADS // SSD_GROUPS) * SSD_HEADDIM
ALPHA = 2.0 ** 0.25

TM = 512
NCH = TM // SSD_CHUNK
NT = SEQ // TM
NCHUNK = SEQ // SSD_CHUNK
N_TILES = BATCH * NT
FF_BLK = 1024
VMEM_LIMIT = 58 * 1024 * 1024

COL_E1 = 48
COL_W = 80
COL_CSF = 112


def _expansion(col0, pieces, width):
    m = np.zeros((LANES, N_DH * width), np.float32)
    for t in range(pieces):
        for j in range(N_DH):
            m[col0 + 16 * t + j, j * width:(j + 1) * width] = 1.0
    return m


_EXE = _expansion(COL_E1, 2, SSD_HEADDIM)
_EXW = _expansion(COL_W, 2, SSD_HEADDIM)


def _ln_hat(x):
    mu = jnp.mean(x, axis=-1, keepdims=True)
    xc = x - mu
    var = jnp.mean(xc * xc, axis=-1, keepdims=True)
    return xc * lax.rsqrt(var + LN_EPS)


def _silu(x):
    return x / (1.0 + jnp.exp(-x))


def _softplus(x):
    return jnp.maximum(x, 0.0) + jnp.log1p(jnp.exp(-jnp.abs(x)))


def _edge_masks(rows, period):
    pos = lax.broadcasted_iota(jnp.int32, (rows, LANES), 0) % period
    return (pos != 0).astype(F32), (pos != period - 1).astype(F32)


def _conv3(t, w, mprev, mnext):
    rows = t.shape[0]
    prev = pltpu.roll(t, 1, 0) * mprev
    nxt = pltpu.roll(t, rows - 1, 0) * mnext
    return prev * w[0:1, :] + t * w[1:2, :] + nxt * w[2:3, :]


def _split(v, pieces):
    out = []
    for _ in range(pieces - 1):
        p = v.astype(BF16).astype(F32)
        out.append(p)
        v = v - p
    out.append(v.astype(BF16).astype(F32))
    return out


def _tri(length, op):
    i = np.arange(length)
    return op(i[:, None], i[None, :]).astype(np.float32)


_U_CHUNK = np.concatenate([_tri(SSD_CHUNK, np.less_equal), _tri(SSD_CHUNK, np.greater_equal),
                           np.ones((SSD_CHUNK, SSD_CHUNK), np.float32)], axis=1)
_U_CTX = np.concatenate([_tri(CTX_LEN, np.greater), _tri(CTX_LEN, np.less)], axis=1)


def _scan_mm(v, u_ref):
    pieces = jnp.concatenate(_split(v, 3), axis=0).astype(BF16)
    o = jnp.dot(pieces, u_ref[...], preferred_element_type=F32)
    return o[0:N_DH] + o[N_DH:2 * N_DH] + o[2 * N_DH:3 * N_DH]


def _dt_rows(raw, dtb, a_log):
    r = raw.T[0:N_DH, :] + dtb
    dt = _softplus(r)
    return dt, dt * (-jnp.exp(a_log))


def _fwd_rows():
    return lax.broadcasted_iota(jnp.int32, (N_DH, 1), 0) < SSD_HEADS


def _const_spec(shape):
    nd = len(shape)
    return pl.BlockSpec(shape, lambda *_: (0,) * nd, pipeline_mode=pl.Buffered(1))


def _mod_kernel(c_ref, w_ref, b_ref, o_ref):
    s = _silu(c_ref[...]).astype(BF16)
    o_ref[...] = jnp.dot(s, w_ref[...].astype(BF16), preferred_element_type=F32) + b_ref[...]


def _mod_call(cvec, w_mod, b_mod):
    tn = 1536
    return pl.pallas_call(
        _mod_kernel,
        grid=(6 * D_MODEL // tn,),
        in_specs=[pl.BlockSpec((16, D_MODEL), lambda j: (0, 0)),
                  pl.BlockSpec((D_MODEL, tn), lambda j: (0, j)),
                  pl.BlockSpec((1, tn), lambda j: (0, j))],
        out_specs=pl.BlockSpec((16, tn), lambda j: (0, j)),
        out_shape=jax.ShapeDtypeStruct((16, 6 * D_MODEL), F32),
        compiler_params=pltpu.CompilerParams(dimension_semantics=("arbitrary",),
                                             vmem_limit_bytes=VMEM_LIMIT),
        name="mod",
    )(cvec, w_mod, b_mod)


def _wprep_kernel(wint_ref, wout_ref, wff1_ref, wff2_ref, pin_ref, pout_ref, pff1_ref, pff2_ref):
    dt_rows = jnp.concatenate([wint_ref[DT_OFF:DT_OFF + N_DH, :],
                               jnp.zeros((LANES - N_DH, wint_ref.shape[1]), F32)], axis=0)
    pin_ref[:, P_DT:P_DT + LANES] = dt_rows.T.astype(BF16)
    for dst, src, width in ((P_XBC, XBC_OFF, XBC_DIM), (P_Z, Z_OFF, SSD_WIDTH), (P_CONV, 0, Z_OFF)):
        for j in range(0, width, LANES):
            pin_ref[:, dst + j:dst + j + LANES] = wint_ref[src + j:src + j + LANES, :].T.astype(BF16)
    pout_ref[...] = wout_ref[0].astype(BF16)
    pff1_ref[...] = wff1_ref[0].astype(BF16)
    pff2_ref[...] = wff2_ref[0].astype(BF16)


def _wprep_call(w_in_t, w_out, w_ff1, w_ff2):
    steps = 8
    r1, r4 = D_MODEL // steps, D_FF // steps
    return pl.pallas_call(
        _wprep_kernel,
        grid=(steps,),
        in_specs=[pl.BlockSpec((w_in_t.shape[0], r1), lambda i: (0, i)),
                  pl.BlockSpec((1, r1, D_MODEL), lambda i: (0, i, 0)),
                  pl.BlockSpec((1, r1, D_FF), lambda i: (0, i, 0)),
                  pl.BlockSpec((1, r4, D_MODEL), lambda i: (0, i, 0))],
        out_specs=[pl.BlockSpec((r1, IN_PAD), lambda i: (i, 0)),
                   pl.BlockSpec((r1, D_MODEL), lambda i: (i, 0)),
                   pl.BlockSpec((r1, D_FF), lambda i: (i, 0)),
                   pl.BlockSpec((r4, D_MODEL), lambda i: (i, 0))],
        out_shape=[jax.ShapeDtypeStruct((D_MODEL, IN_PAD), BF16),
                   jax.ShapeDtypeStruct((D_MODEL, D_MODEL), BF16),
                   jax.ShapeDtypeStruct((D_MODEL, D_FF), BF16),
                   jax.ShapeDtypeStruct((D_FF, D_MODEL), BF16)],
        compiler_params=pltpu.CompilerParams(dimension_semantics=("arbitrary",),
                                             vmem_limit_bytes=VMEM_LIMIT),
        name="wprep",
    )(w_in_t, w_out, w_ff1, w_ff2)


def _ctx_kernel(x_ref, mod_ref, lng_ref, lnb_ref, w_ref, scw_ref, scb_ref, dtb_ref, alog_ref,
                exw_ref, u_ref, h0_ref):
    m = mod_ref[0]
    sc = 1.0 + m[1:2]
    u = _ln_hat(x_ref[0]) * (lng_ref[...] * sc) + (lnb_ref[...] * sc + m[0:1])
    proj = jnp.dot(u.astype(BF16), w_ref[...], preferred_element_type=F32)
    mprev, mnext = _edge_masks(CTX_LEN, CTX_LEN)
    slabs = []
    for j in range((SSD_WIDTH + SSD_GN) // LANES):
        sl = slice(j * LANES, (j + 1) * LANES)
        pj = proj[:, P_XBC + j * LANES:P_XBC + (j + 1) * LANES]
        slabs.append(_silu(_conv3(pj, scw_ref[:, sl], mprev, mnext) + scb_ref[:, sl]))
    xs = jnp.concatenate(slabs[:4], axis=1)
    dt, adt = _dt_rows(proj[:, P_DT:P_DT + LANES], dtb_ref[...], alog_ref[...])
    sc2 = _scan_mm(adt, u_ref)
    excl = jnp.where(_fwd_rows(), sc2[:, :CTX_LEN], sc2[:, CTX_LEN:])
    w = jnp.exp(excl) * dt
    zero = jnp.zeros((N_DH, CTX_LEN), F32)
    table = jnp.concatenate([zero] * (COL_W // 16) + _split(w, 2) + [zero], axis=0)
    cols = table.T.astype(BF16)
    wx = jnp.dot(cols, exw_ref[...], preferred_element_type=F32)
    for d in range(N_DIRS):
        xw = (xs * wx[:, d * SSD_WIDTH:(d + 1) * SSD_WIDTH]).astype(BF16)
        for g in range(SSD_GROUPS):
            bt = slabs[4 + g].T.astype(BF16)
            h0_ref[0, d, g] = jnp.dot(bt, xw[:, g * GROUP_COLS:(g + 1) * GROUP_COLS],
                                      preferred_element_type=F32)


def _ctx_call(ctx, mod_ctx, lng, lnb, w_in, scw, scb, dtb, alog, exw, u_ctx):
    return pl.pallas_call(
        _ctx_kernel,
        grid=(BATCH,),
        in_specs=[pl.BlockSpec((1, CTX_LEN, D_MODEL), lambda b: (b, 0, 0)),
                  _const_spec((1, 6, D_MODEL)), _const_spec((1, D_MODEL)), _const_spec((1, D_MODEL)),
                  _const_spec((D_MODEL, CTX_PAD)), _const_spec((3, XBC_DIM)), _const_spec((1, XBC_DIM)),
                  _const_spec((N_DH, 1)), _const_spec((N_DH, 1)), _const_spec((LANES, 2 * SSD_WIDTH)),
                  _const_spec((CTX_LEN, 2 * CTX_LEN))],
        out_specs=pl.BlockSpec((1, N_DIRS, SSD_GROUPS, SSD_STATE, GROUP_COLS),
                               lambda b: (b, 0, 0, 0, 0)),
        out_shape=jax.ShapeDtypeStruct((BATCH, N_DIRS, SSD_GROUPS, SSD_STATE, GROUP_COLS), F32),
        compiler_params=pltpu.CompilerParams(dimension_semantics=("arbitrary",),
                                             vmem_limit_bytes=VMEM_LIMIT),
        name="ctx",
    )(ctx, mod_ctx, lng, lnb, w_in, scw, scb, dtb, alog, exw, u_ctx)


def _proj_kernel(x_ref, mod_ref, lng_ref, lnb_ref, w_ref, cw_ref, scw_ref, scb_ref, dtb_ref,
                 alog_ref, exe_ref, exw_ref, u_ref, h0_ref,
                 ycv_ref, zg_ref, xs_ref, bt_ref, cm_ref, rows_ref, cols_ref, colf_ref, sb_ref, st_ref):
    @pl.when(pl.program_id(1) == 0)
    def _():
        st_ref[...] = h0_ref[0, 0]

    m = mod_ref[0]
    sc = 1.0 + m[1:2]
    u = _ln_hat(x_ref[0]) * (lng_ref[...] * sc) + (lnb_ref[...] * sc + m[0:1])
    proj = jnp.dot(u.astype(BF16), w_ref[...], preferred_element_type=F32)
    mprev, mnext = _edge_masks(TM, GRID_W)

    for j in range(CONV_WIDTH // LANES):
        sl = slice(j * LANES, (j + 1) * LANES)
        gb = proj[:, P_CONV + j * LANES:P_CONV + (j + 1) * LANES]
        gc = proj[:, P_CONV + CONV_WIDTH + j * LANES:P_CONV + CONV_WIDTH + (j + 1) * LANES]
        gh = proj[:, P_CONV + 2 * CONV_WIDTH + j * LANES:P_CONV + 2 * CONV_WIDTH + (j + 1) * LANES]
        ycv_ref[0, :, sl] = (gb * _conv3(gc * gh, cw_ref[:, sl], mprev, mnext)).astype(BF16)

    zg_ref[0] = _silu(proj[:, P_Z:P_CONV]).astype(BF16)

    slabs = []
    for j in range(XBC_DIM // LANES):
        sl = slice(j * LANES, (j + 1) * LANES)
        pj = proj[:, P_XBC + j * LANES:P_XBC + (j + 1) * LANES]
        slabs.append(_silu(_conv3(pj, scw_ref[:, sl], mprev, mnext) + scb_ref[:, sl]))
    xs = jnp.concatenate(slabs[:4], axis=1)
    xs_ref[0] = xs.astype(BF16)
    cm_ref[0] = jnp.concatenate(slabs[6:8], axis=1).astype(BF16)

    raw = proj[:, P_DT:P_DT + LANES]
    is_fwd = _fwd_rows()
    zero = jnp.zeros((N_DH, SSD_CHUNK), F32)
    bts, colss = [], []
    for c in range(NCH):
        tok = slice(c * SSD_CHUNK, (c + 1) * SSD_CHUNK)
        dt, adt = _dt_rows(raw[tok], dtb_ref[...], alog_ref[...])
        sc3 = _scan_mm(adt, u_ref)
        cs = jnp.where(is_fwd, sc3[:, :SSD_CHUNK], sc3[:, SSD_CHUNK:2 * SSD_CHUNK])
        tot = sc3[:, 2 * SSD_CHUNK:]
        e1 = jnp.exp(cs)
        w = jnp.exp(tot - cs) * dt
        rows_ref[0, c] = jnp.concatenate([cs, dt], axis=0)
        table = jnp.concatenate([zero] * 3 + _split(e1, 2) + _split(w, 2) + [cs], axis=0).T
        colf_ref[0, c] = table
        colss.append(table.astype(BF16))
        cols_ref[0, c] = colss[c]
        bt = [slabs[4 + g][tok].T.astype(BF16) for g in range(SSD_GROUPS)]
        for g in range(SSD_GROUPS):
            bt_ref[0, c, g] = bt[g]
        bts.append(bt)

    state = [st_ref[g] for g in range(SSD_GROUPS)]
    for c in reversed(range(NCH)):
        tok = slice(c * SSD_CHUNK, (c + 1) * SSD_CHUNK)
        cols = colss[c]
        wxb = jnp.dot(cols, exw_ref[:, SSD_WIDTH:], preferred_element_type=F32)
        dec = jnp.dot(cols[0:16], exe_ref[:, SSD_WIDTH:], preferred_element_type=F32)[0:1]
        xw = (xs[tok] * wxb).astype(BF16)
        for g in range(SSD_GROUPS):
            gs = slice(g * GROUP_COLS, (g + 1) * GROUP_COLS)
            sb_ref[0, c, g] = state[g].astype(BF16)
            local = jnp.dot(bts[c][g], xw[:, gs], preferred_element_type=F32)
            state[g] = state[g] * dec[:, gs] + local
    for g in range(SSD_GROUPS):
        st_ref[g] = state[g]


def _proj_call(x, mod_lat, lng, lnb, w_in, cw, scw, scb, dtb, alog, exe, exw, u_chunk, h0):
    rev = lambda b, t: (b, NT - 1 - t, 0)
    rev4 = lambda b, t: (b, NT - 1 - t, 0, 0)
    rev5 = lambda b, t: (b, NT - 1 - t, 0, 0, 0)
    tok = lambda width: pl.BlockSpec((1, TM, width), rev)
    out_shape = [
        jax.ShapeDtypeStruct((BATCH, SEQ, CONV_WIDTH), BF16),
        jax.ShapeDtypeStruct((BATCH, SEQ, SSD_WIDTH), BF16),
        jax.ShapeDtypeStruct((BATCH, SEQ, SSD_WIDTH), BF16),
        jax.ShapeDtypeStruct((BATCH, NCHUNK, SSD_GROUPS, SSD_STATE, SSD_CHUNK), BF16),
        jax.ShapeDtypeStruct((BATCH, SEQ, SSD_GN), BF16),
        jax.ShapeDtypeStruct((BATCH, NCHUNK, 2 * N_DH, SSD_CHUNK), F32),
        jax.ShapeDtypeStruct((BATCH, NCHUNK, SSD_CHUNK, LANES), BF16),
        jax.ShapeDtypeStruct((BATCH, NCHUNK, SSD_CHUNK, LANES), F32),
        jax.ShapeDtypeStruct((BATCH, NCHUNK, SSD_GROUPS, SSD_STATE, GROUP_COLS), BF16),
    ]
    out_specs = [
        tok(CONV_WIDTH), tok(SSD_WIDTH), tok(SSD_WIDTH),
        pl.BlockSpec((1, NCH, SSD_GROUPS, SSD_STATE, SSD_CHUNK), rev5),
        tok(SSD_GN),
        pl.BlockSpec((1, NCH, 2 * N_DH, SSD_CHUNK), rev4),
        pl.BlockSpec((1, NCH, SSD_CHUNK, LANES), rev4),
        pl.BlockSpec((1, NCH, SSD_CHUNK, LANES), rev4),
        pl.BlockSpec((1, NCH, SSD_GROUPS, SSD_STATE, GROUP_COLS), rev5),
    ]
    in_specs = [
        pl.BlockSpec((1, TM, D_MODEL), rev),
        pl.BlockSpec((1, 6, D_MODEL), lambda b, t: (b, 0, 0)),
        _const_spec((1, D_MODEL)), _const_spec((1, D_MODEL)),
        _const_spec((D_MODEL, IN_PAD)),
        _const_spec((3, CONV_WIDTH)), _const_spec((3, XBC_DIM)), _const_spec((1, XBC_DIM)),
        _const_spec((N_DH, 1)), _const_spec((N_DH, 1)),
        _const_spec((LANES, 2 * SSD_WIDTH)), _const_spec((LANES, 2 * SSD_WIDTH)),
        _const_spec((SSD_CHUNK, 3 * SSD_CHUNK)),
        pl.BlockSpec((1, 1, SSD_GROUPS, SSD_STATE, GROUP_COLS), lambda b, t: (b, 1, 0, 0, 0)),
    ]
    return pl.pallas_call(
        _proj_kernel,
        grid=(BATCH, NT),
        in_specs=in_specs,
        out_specs=out_specs,
        out_shape=out_shape,
        scratch_shapes=[pltpu.VMEM((SSD_GROUPS, SSD_STATE, GROUP_COLS), F32)],
        compiler_params=pltpu.CompilerParams(dimension_semantics=("arbitrary", "arbitrary"),
                                             vmem_limit_bytes=VMEM_LIMIT),
        name="proj",
    )(x, mod_lat, lng, lnb, w_in, cw, scw, scb, dtb, alog, exe, exw, u_chunk, h0)


def _out_kernel(x_ref, mod_ref, lng_ref, lnb_ref, ycv_ref, zg_ref, xs_ref, bt_ref, cm_ref, rows_ref,
                cols_ref, colf_ref, sb_ref, h0_ref, exw_ref, dx_ref, nw_ref, wout_ref,
                ln1g_ref, ln1b_ref, wff1_ref, wff2_ref, ln2g_ref, ln2b_ref,
                out_ref, st_ref, yn_ref, r2_ref):
    s = pl.program_id(0)

    def ln2_prev():
        out_ref[0] = _ln_hat(r2_ref[...]) * ln2g_ref[...] + ln2b_ref[...]

    @pl.when(s == 0)
    def _():
        r2_ref[...] = jnp.zeros((TM, D_MODEL), F32)

    @pl.when(s == N_TILES)
    def _():
        ln2_prev()

    @pl.when(s < N_TILES)
    def _():
        _out_tile(x_ref, mod_ref, lng_ref, lnb_ref, ycv_ref, zg_ref, xs_ref, bt_ref, cm_ref, rows_ref,
                  cols_ref, colf_ref, sb_ref, h0_ref, exw_ref, dx_ref, nw_ref, wout_ref,
                  ln1g_ref, ln1b_ref, wff1_ref, wff2_ref, st_ref, yn_ref, r2_ref, s % NT == 0, ln2_prev)


def _out_tile(x_ref, mod_ref, lng_ref, lnb_ref, ycv_ref, zg_ref, xs_ref, bt_ref, cm_ref, rows_ref,
              cols_ref, colf_ref, sb_ref, h0_ref, exw_ref, dx_ref, nw_ref, wout_ref,
              ln1g_ref, ln1b_ref, wff1_ref, wff2_ref, st_ref, yn_ref, r2_ref, first_tile_of_row,
              during_mlp):
    @pl.when(first_tile_of_row)
    def _():
        st_ref[...] = h0_ref[0, 0]

    li = lax.broadcasted_iota(jnp.int32, (SSD_CHUNK, SSD_CHUNK), 0)
    si = lax.broadcasted_iota(jnp.int32, (SSD_CHUNK, SSD_CHUNK), 1)
    low = li >= si
    diag = li == si
    lo_half = si < SSD_HEADDIM

    state = [st_ref[g] for g in range(SSD_GROUPS)]
    for c in range(NCH):
        tok = slice(c * SSD_CHUNK, (c + 1) * SSD_CHUNK)
        rows = rows_ref[0, c]
        cols = cols_ref[0, c]
        xs = xs_ref[0, tok, :]
        cm = cm_ref[0, tok, :]
        colf = colf_ref[0, c]
        bc = [jnp.broadcast_to(colf[:, COL_CSF + j:COL_CSF + j + 1], (SSD_CHUNK, LANES))
              for j in range(N_DH)]
        e1x = [jnp.concatenate(
            [jnp.exp(jnp.where(lo_half, bc[d * SSD_HEADS + 2 * k], bc[d * SSD_HEADS + 2 * k + 1]))
             for k in range(SSD_HEADS // 2)], axis=1) for d in range(N_DIRS)]
        wxf = jnp.dot(cols, exw_ref[:, :SSD_WIDTH], preferred_element_type=F32)
        dec = e1x[0][SSD_CHUNK - 1:SSD_CHUNK, :]
        gmat = [jnp.dot(cm[:, g * SSD_STATE:(g + 1) * SSD_STATE], bt_ref[0, c, g],
                        preferred_element_type=F32) for g in range(SSD_GROUPS)]

        ys = []
        for k in range(SSD_HEADS // 2):
            ms = []
            for h in (2 * k, 2 * k + 1):
                g = h // (SSD_HEADS // SSD_GROUPS)
                hb = SSD_HEADS + h
                arg = jnp.where(low,
                                bc[h] - rows[h:h + 1, :],
                                bc[hb] - rows[hb:hb + 1, :])
                dtf = rows[N_DH + h:N_DH + h + 1, :]
                dtb = rows[N_DH + hb:N_DH + hb + 1, :]
                fac = jnp.where(low, dtf, dtb) + jnp.where(diag, dtb, 0.0)
                ms.append((gmat[g] * jnp.exp(arg) * fac).astype(BF16))
            xp = xs[:, k * LANES:(k + 1) * LANES]
            rhs = jnp.concatenate([jnp.where(lo_half, xp, jnp.zeros_like(xp)),
                                   jnp.where(lo_half, jnp.zeros_like(xp), xp)], axis=0)
            ys.append(jnp.dot(jnp.concatenate(ms, axis=1), rhs, preferred_element_type=F32))
        y = jnp.concatenate(ys, axis=1)

        yf, yb = [], []
        for g in range(SSD_GROUPS):
            cg = cm[:, g * SSD_STATE:(g + 1) * SSD_STATE]
            yf.append(jnp.dot(cg, state[g].astype(BF16), preferred_element_type=F32))
            yb.append(jnp.dot(cg, sb_ref[0, c, g], preferred_element_type=F32))
        y = (y + jnp.concatenate(yf, axis=1) * e1x[0]
             + jnp.concatenate(yb, axis=1) * e1x[1]
             + xs.astype(F32) * dx_ref[...])

        yg = y * zg_ref[0, tok, :].astype(F32)
        ms_ = jnp.mean(yg * yg, axis=-1, keepdims=True)
        yn_ref[tok, :] = (yg * lax.rsqrt(ms_ + RMS_EPS) * nw_ref[...]).astype(BF16)

        xw = (xs.astype(F32) * wxf).astype(BF16)
        for g in range(SSD_GROUPS):
            gs = slice(g * GROUP_COLS, (g + 1) * GROUP_COLS)
            local = jnp.dot(bt_ref[0, c, g], xw[:, gs], preferred_element_type=F32)
            state[g] = state[g] * dec[:, gs] + local

    for g in range(SSD_GROUPS):
        st_ref[g] = state[g]

    mix = (jnp.dot(ycv_ref[0], wout_ref[:CONV_WIDTH, :], preferred_element_type=F32)
           + jnp.dot(yn_ref[...], wout_ref[CONV_WIDTH:, :], preferred_element_type=F32))
    m = mod_ref[0]
    h = _ln_hat(x_ref[0]) * lng_ref[...] + lnb_ref[...]
    h1 = _ln_hat(ALPHA * h + m[2:3] * mix) * ln1g_ref[...] + ln1b_ref[...]
    u2 = (h1 * (1.0 + m[4:5]) + m[3:4]).astype(BF16)
    acc = jnp.zeros((TM, D_MODEL), F32)
    for j in range(D_FF // FF_BLK):
        hid = jnp.maximum(jnp.dot(u2, wff1_ref[:, j * FF_BLK:(j + 1) * FF_BLK],
                                  preferred_element_type=F32), 0.0)
        acc = acc + jnp.dot((hid * hid).astype(BF16), wff2_ref[j * FF_BLK:(j + 1) * FF_BLK, :],
                            preferred_element_type=F32)
        if j == 0:
            during_mlp()
    r2_ref[...] = ALPHA * h1 + m[5:6] * acc


def _out_call(x, mod_lat, lng, lnb, ycv, zg, xs, bt, cm, rows, cols, colf, sb, h0, exw, dx, nw,
              wout, ln1g, ln1b, wff1, wff2, ln2g, ln2b):
    cur = lambda s: jnp.minimum(s, N_TILES - 1)
    fwd = lambda s: (cur(s) // NT, cur(s) % NT, 0)
    fwd4 = lambda s: (cur(s) // NT, cur(s) % NT, 0, 0)
    fwd5 = lambda s: (cur(s) // NT, cur(s) % NT, 0, 0, 0)
    prev = lambda s: (jnp.maximum(s - 1, 0) // NT, jnp.maximum(s - 1, 0) % NT, 0)
    tok = lambda width: pl.BlockSpec((1, TM, width), fwd)
    in_specs = [
        tok(D_MODEL),
        pl.BlockSpec((1, 6, D_MODEL), lambda s: (cur(s) // NT, 0, 0)),
        _const_spec((1, D_MODEL)), _const_spec((1, D_MODEL)),
        tok(CONV_WIDTH), tok(SSD_WIDTH), tok(SSD_WIDTH),
        pl.BlockSpec((1, NCH, SSD_GROUPS, SSD_STATE, SSD_CHUNK), fwd5),
        tok(SSD_GN),
        pl.BlockSpec((1, NCH, 2 * N_DH, SSD_CHUNK), fwd4),
        pl.BlockSpec((1, NCH, SSD_CHUNK, LANES), fwd4),
        pl.BlockSpec((1, NCH, SSD_CHUNK, LANES), fwd4),
        pl.BlockSpec((1, NCH, SSD_GROUPS, SSD_STATE, GROUP_COLS), fwd5),
        pl.BlockSpec((1, 1, SSD_GROUPS, SSD_STATE, GROUP_COLS), lambda s: (cur(s) // NT, 0, 0, 0, 0)),
        _const_spec((LANES, 2 * SSD_WIDTH)),
        _const_spec((1, SSD_WIDTH)), _const_spec((1, SSD_WIDTH)),
        _const_spec((D_MODEL, D_MODEL)),
        _const_spec((1, D_MODEL)), _const_spec((1, D_MODEL)),
        _const_spec((D_MODEL, D_FF)), _const_spec((D_FF, D_MODEL)),
        _const_spec((1, D_MODEL)), _const_spec((1, D_MODEL)),
    ]
    return pl.pallas_call(
        _out_kernel,
        grid=(N_TILES + 1,),
        in_specs=in_specs,
        out_specs=pl.BlockSpec((1, TM, D_MODEL), prev),
        out_shape=jax.ShapeDtypeStruct((BATCH, SEQ, D_MODEL), F32),
        scratch_shapes=[pltpu.VMEM((SSD_GROUPS, SSD_STATE, GROUP_COLS), F32),
                        pltpu.VMEM((TM, SSD_WIDTH), BF16),
                        pltpu.VMEM((TM, D_MODEL), F32)],
        compiler_params=pltpu.CompilerParams(dimension_semantics=("arbitrary",),
                                             vmem_limit_bytes=VMEM_LIMIT),
        name="out",
    )(x, mod_lat, lng, lnb, ycv, zg, xs, bt, cm, rows, cols, colf, sb, h0, exw, dx, nw,
      wout, ln1g, ln1b, wff1, wff2, ln2g, ln2b)


def kernel(x, c, ctx, c_ctx, ln_in_g, ln_in_b, w_mod, b_mod, w_in, conv_w, ssd_conv_w, ssd_conv_b,
           dt_bias, a_log, ssd_d, ssd_norm_w, w_out, ln1_g, ln1_b, w_ff1, w_ff2, ln2_g, ln2_b):
    row = lambda v: v.reshape(1, -1).astype(F32)
    cvec = jnp.concatenate([c, c_ctx[None, :], jnp.zeros((16 - BATCH - 1, D_MODEL), F32)], axis=0)
    mod = _mod_call(cvec, w_mod[0], row(b_mod[0]))
    mod_lat = mod[:BATCH].reshape(BATCH, 6, D_MODEL)
    mod_ctx = mod[BATCH:BATCH + 1].reshape(1, 6, D_MODEL)

    w_in_p, w_out_p, w_ff1_p, w_ff2_p = _wprep_call(w_in[0].T, w_out, w_ff1, w_ff2)
    lng, lnb = row(ln_in_g), row(ln_in_b)
    scw, scb = ssd_conv_w[0], row(ssd_conv_b[0])
    dtb = dt_bias[0].reshape(N_DH, 1)
    alog = a_log[0].reshape(N_DH, 1)
    exe = jnp.asarray(_EXE, BF16)
    exw = jnp.asarray(_EXW, BF16)

    h0 = _ctx_call(ctx, mod_ctx, lng, lnb, w_in_p, scw, scb, dtb, alog, exw, jnp.asarray(_U_CTX, BF16))
    ycv, zg, xs, bt, cm, rows, cols, colf, sb = _proj_call(
        x, mod_lat, lng, lnb, w_in_p, conv_w[0], scw, scb, dtb, alog, exe, exw,
        jnp.asarray(_U_CHUNK, BF16), h0)
    dx = jnp.repeat(ssd_d[0], SSD_HEADDIM).reshape(1, SSD_WIDTH)
    return _out_call(x, mod_lat, lng, lnb, ycv, zg, xs, bt, cm, rows, cols, colf, sb, h0, exw,
                     dx, row(ssd_norm_w[0]), w_out_p, row(ln1_g[0]), row(ln1_b[0]),
                     w_ff1_p, w_ff2_p, row(ln2_g[0]), row(ln2_b[0]))
```

```python
import jax
import jax.numpy as jnp
import numpy as np
from jax import lax
from jax.experimental import pallas as pl
from jax.experimental.pallas import tpu as pltpu

F32 = jnp.float32
BF16 = jnp.bfloat16

D_MODEL = 1024
BATCH = 8
SEQ = 2048
CTX_LEN = 256
GRID_W = 64
CONV_WIDTH = 512
SSD_WIDTH = 512
SSD_HEADDIM = 64
SSD_HEADS = 8
SSD_GROUPS = 2
SSD_STATE = 128
SSD_CHUNK = 128
N_DIRS = 2
D_FF = 4 * D_MODEL
LN_EPS = 1e-5
RMS_EPS = 1e-5
SSD_GN = SSD_GROUPS * SSD_STATE
XBC_DIM = SSD_WIDTH + 2 * SSD_GN
Z_OFF = 3 * CONV_WIDTH
XBC_OFF = Z_OFF + SSD_WIDTH
DT_OFF = XBC_OFF + XBC_DIM
N_DH = N_DIRS * SSD_HEADS
LANES = 128
P_DT = 0
P_XBC = LANES
P_Z = P_XBC + XBC_DIM
P_CONV = P_Z + SSD_WIDTH
IN_PAD = P_CONV + 3 * CONV_WIDTH
CTX_PAD = P_Z
GROUP_COLS = (SSD_HEADS // SSD_GROUPS) * SSD_HEADDIM
ALPHA = 2.0 ** 0.25

TM = 512
NCH = TM // SSD_CHUNK
NT = SEQ // TM
NCHUNK = SEQ // SSD_CHUNK
N_TILES = BATCH * NT
FF_BLK = 1024
FF_SUB = 256
VMEM_LIMIT = 58 * 1024 * 1024

COL_E1 = 48
COL_W = 80
COL_CSF = 112


def _expansion(col0, pieces, width):
    m = np.zeros((LANES, N_DH * width), np.float32)
    for t in range(pieces):
        for j in range(N_DH):
            m[col0 + 16 * t + j, j * width:(j + 1) * width] = 1.0
    return m


_EXE = _expansion(COL_E1, 2, SSD_HEADDIM)
_EXW = _expansion(COL_W, 2, SSD_HEADDIM)


def _ln_hat(x):
    mu = jnp.mean(x, axis=-1, keepdims=True)
    xc = x - mu
    var = jnp.mean(xc * xc, axis=-1, keepdims=True)
    return xc * lax.rsqrt(var + LN_EPS)


def _silu(x):
    return x / (1.0 + jnp.exp(-x))


def _softplus(x):
    return jnp.maximum(x, 0.0) + jnp.log1p(jnp.exp(-jnp.abs(x)))


def _edge_masks(rows, period):
    pos = lax.broadcasted_iota(jnp.int32, (rows, LANES), 0) % period
    return (pos != 0).astype(F32), (pos != period - 1).astype(F32)


def _conv3(t, w, mprev, mnext):
    rows = t.shape[0]
    prev = pltpu.roll(t, 1, 0) * mprev
    nxt = pltpu.roll(t, rows - 1, 0) * mnext
    return prev * w[0:1, :] + t * w[1:2, :] + nxt * w[2:3, :]


def _split(v, pieces):
    out = []
    for _ in range(pieces - 1):
        p = v.astype(BF16).astype(F32)
        out.append(p)
        v = v - p
    out.append(v.astype(BF16).astype(F32))
    return out


def _tri(length, op):
    i = np.arange(length)
    return op(i[:, None], i[None, :]).astype(np.float32)


_U_CHUNK = np.concatenate([_tri(SSD_CHUNK, np.less_equal), _tri(SSD_CHUNK, np.greater_equal),
                           np.ones((SSD_CHUNK, SSD_CHUNK), np.float32)], axis=1)
_U_CTX = np.concatenate([_tri(CTX_LEN, np.greater), _tri(CTX_LEN, np.less)], axis=1)


def _scan_mm(v, u_ref):
    pieces = jnp.concatenate(_split(v, 3), axis=0).astype(BF16)
    o = jnp.dot(pieces, u_ref[...], preferred_element_type=F32)
    return o[0:N_DH] + o[N_DH:2 * N_DH] + o[2 * N_DH:3 * N_DH]


def _dt_rows(raw, dtb, a_log):
    r = raw.T[0:N_DH, :] + dtb
    dt = _softplus(r)
    return dt, dt * (-jnp.exp(a_log))


def _fwd_rows():
    return lax.broadcasted_iota(jnp.int32, (N_DH, 1), 0) < SSD_HEADS


def _const_spec(shape):
    nd = len(shape)
    return pl.BlockSpec(shape, lambda *_: (0,) * nd, pipeline_mode=pl.Buffered(1))


def _mod_kernel(c_ref, w_ref, b_ref, o_ref):
    s = _silu(c_ref[...]).astype(BF16)
    o_ref[...] = jnp.dot(s, w_ref[...].astype(BF16), preferred_element_type=F32) + b_ref[...]


def _mod_call(cvec, w_mod, b_mod):
    tn = 1536
    return pl.pallas_call(
        _mod_kernel,
        grid=(6 * D_MODEL // tn,),
        in_specs=[pl.BlockSpec((16, D_MODEL), lambda j: (0, 0)),
                  pl.BlockSpec((D_MODEL, tn), lambda j: (0, j)),
                  pl.BlockSpec((1, tn), lambda j: (0, j))],
        out_specs=pl.BlockSpec((16, tn), lambda j: (0, j)),
        out_shape=jax.ShapeDtypeStruct((16, 6 * D_MODEL), F32),
        compiler_params=pltpu.CompilerParams(dimension_semantics=("arbitrary",),
                                             vmem_limit_bytes=VMEM_LIMIT),
        name="mod",
    )(cvec, w_mod, b_mod)


def _wprep_kernel(wint_ref, wout_ref, wff1_ref, wff2_ref, pin_ref, pout_ref, pff1_ref, pff2_ref):
    dt_rows = jnp.concatenate([wint_ref[DT_OFF:DT_OFF + N_DH, :],
                               jnp.zeros((LANES - N_DH, wint_ref.shape[1]), F32)], axis=0)
    pin_ref[:, P_DT:P_DT + LANES] = dt_rows.T.astype(BF16)
    for dst, src, width in ((P_XBC, XBC_OFF, XBC_DIM), (P_Z, Z_OFF, SSD_WIDTH), (P_CONV, 0, Z_OFF)):
        for j in range(0, width, LANES):
            pin_ref[:, dst + j:dst + j + LANES] = wint_ref[src + j:src + j + LANES, :].T.astype(BF16)
    pout_ref[...] = wout_ref[0].astype(BF16)
    pff1_ref[...] = wff1_ref[0].astype(BF16)
    pff2_ref[...] = wff2_ref[0].astype(BF16)


def _wprep_call(w_in_t, w_out, w_ff1, w_ff2):
    steps = 8
    r1, r4 = D_MODEL // steps, D_FF // steps
    return pl.pallas_call(
        _wprep_kernel,
        grid=(steps,),
        in_specs=[pl.BlockSpec((w_in_t.shape[0], r1), lambda i: (0, i)),
                  pl.BlockSpec((1, r1, D_MODEL), lambda i: (0, i, 0)),
                  pl.BlockSpec((1, r1, D_FF), lambda i: (0, i, 0)),
                  pl.BlockSpec((1, r4, D_MODEL), lambda i: (0, i, 0))],
        out_specs=[pl.BlockSpec((r1, IN_PAD), lambda i: (i, 0)),
                   pl.BlockSpec((r1, D_MODEL), lambda i: (i, 0)),
                   pl.BlockSpec((r1, D_FF), lambda i: (i, 0)),
                   pl.BlockSpec((r4, D_MODEL), lambda i: (i, 0))],
        out_shape=[jax.ShapeDtypeStruct((D_MODEL, IN_PAD), BF16),
                   jax.ShapeDtypeStruct((D_MODEL, D_MODEL), BF16),
                   jax.ShapeDtypeStruct((D_MODEL, D_FF), BF16),
                   jax.ShapeDtypeStruct((D_FF, D_MODEL), BF16)],
        compiler_params=pltpu.CompilerParams(dimension_semantics=("arbitrary",),
                                             vmem_limit_bytes=VMEM_LIMIT),
        name="wprep",
    )(w_in_t, w_out, w_ff1, w_ff2)


def _ctx_kernel(x_ref, mod_ref, lng_ref, lnb_ref, w_ref, scw_ref, scb_ref, dtb_ref, alog_ref,
                exw_ref, u_ref, h0_ref):
    m = mod_ref[0]
    sc = 1.0 + m[1:2]
    u = _ln_hat(x_ref[0]) * (lng_ref[...] * sc) + (lnb_ref[...] * sc + m[0:1])
    proj = jnp.dot(u.astype(BF16), w_ref[...], preferred_element_type=F32)
    mprev, mnext = _edge_masks(CTX_LEN, CTX_LEN)
    slabs = []
    for j in range((SSD_WIDTH + SSD_GN) // LANES):
        sl = slice(j * LANES, (j + 1) * LANES)
        pj = proj[:, P_XBC + j * LANES:P_XBC + (j + 1) * LANES]
        slabs.append(_silu(_conv3(pj, scw_ref[:, sl], mprev, mnext) + scb_ref[:, sl]))
    xs = jnp.concatenate(slabs[:4], axis=1)
    dt, adt = _dt_rows(proj[:, P_DT:P_DT + LANES], dtb_ref[...], alog_ref[...])
    sc2 = _scan_mm(adt, u_ref)
    excl = jnp.where(_fwd_rows(), sc2[:, :CTX_LEN], sc2[:, CTX_LEN:])
    w = jnp.exp(excl) * dt
    zero = jnp.zeros((N_DH, CTX_LEN), F32)
    table = jnp.concatenate([zero] * (COL_W // 16) + _split(w, 2) + [zero], axis=0)
    cols = table.T.astype(BF16)
    wx = jnp.dot(cols, exw_ref[...], preferred_element_type=F32)
    for d in range(N_DIRS):
        xw = (xs * wx[:, d * SSD_WIDTH:(d + 1) * SSD_WIDTH]).astype(BF16)
        for g in range(SSD_GROUPS):
            bt = slabs[4 + g].T.astype(BF16)
            h0_ref[0, d, g] = jnp.dot(bt, xw[:, g * GROUP_COLS:(g + 1) * GROUP_COLS],
                                      preferred_element_type=F32)


def _ctx_call(ctx, mod_ctx, lng, lnb, w_in, scw, scb, dtb, alog, exw, u_ctx):
    return pl.pallas_call(
        _ctx_kernel,
        grid=(BATCH,),
        in_specs=[pl.BlockSpec((1, CTX_LEN, D_MODEL), lambda b: (b, 0, 0)),
                  _const_spec((1, 6, D_MODEL)), _const_spec((1, D_MODEL)), _const_spec((1, D_MODEL)),
                  _const_spec((D_MODEL, CTX_PAD)), _const_spec((3, XBC_DIM)), _const_spec((1, XBC_DIM)),
                  _const_spec((N_DH, 1)), _const_spec((N_DH, 1)), _const_spec((LANES, 2 * SSD_WIDTH)),
                  _const_spec((CTX_LEN, 2 * CTX_LEN))],
        out_specs=pl.BlockSpec((1, N_DIRS, SSD_GROUPS, SSD_STATE, GROUP_COLS),
                               lambda b: (b, 0, 0, 0, 0)),
        out_shape=jax.ShapeDtypeStruct((BATCH, N_DIRS, SSD_GROUPS, SSD_STATE, GROUP_COLS), F32),
        compiler_params=pltpu.CompilerParams(dimension_semantics=("arbitrary",),
                                             vmem_limit_bytes=VMEM_LIMIT),
        name="ctx",
    )(ctx, mod_ctx, lng, lnb, w_in, scw, scb, dtb, alog, exw, u_ctx)


def _proj_kernel(x_ref, mod_ref, lng_ref, lnb_ref, w_ref, cw_ref, scw_ref, scb_ref, dtb_ref,
                 alog_ref, exe_ref, exw_ref, u_ref, h0_ref,
                 ycv_ref, zg_ref, xs_ref, bt_ref, cm_ref, rows_ref, cols_ref, colf_ref, sb_ref, st_ref):
    @pl.when(pl.program_id(1) == 0)
    def _():
        st_ref[...] = h0_ref[0, 0]

    m = mod_ref[0]
    sc = 1.0 + m[1:2]
    u = _ln_hat(x_ref[0]) * (lng_ref[...] * sc) + (lnb_ref[...] * sc + m[0:1])
    proj = jnp.dot(u.astype(BF16), w_ref[...], preferred_element_type=F32)
    mprev, mnext = _edge_masks(TM, GRID_W)

    for j in range(CONV_WIDTH // LANES):
        sl = slice(j * LANES, (j + 1) * LANES)
        gb = proj[:, P_CONV + j * LANES:P_CONV + (j + 1) * LANES]
        gc = proj[:, P_CONV + CONV_WIDTH + j * LANES:P_CONV + CONV_WIDTH + (j + 1) * LANES]
        gh = proj[:, P_CONV + 2 * CONV_WIDTH + j * LANES:P_CONV + 2 * CONV_WIDTH + (j + 1) * LANES]
        ycv_ref[0, :, sl] = (gb * _conv3(gc * gh, cw_ref[:, sl], mprev, mnext)).astype(BF16)

    zg_ref[0] = _silu(proj[:, P_Z:P_CONV]).astype(BF16)

    slabs = []
    for j in range(XBC_DIM // LANES):
        sl = slice(j * LANES, (j + 1) * LANES)
        pj = proj[:, P_XBC + j * LANES:P_XBC + (j + 1) * LANES]
        slabs.append(_silu(_conv3(pj, scw_ref[:, sl], mprev, mnext) + scb_ref[:, sl]))
    xs = jnp.concatenate(slabs[:4], axis=1)
    xs_ref[0] = xs.astype(BF16)
    cm_ref[0] = jnp.concatenate(slabs[6:8], axis=1).astype(BF16)

    raw = proj[:, P_DT:P_DT + LANES]
    is_fwd = _fwd_rows()
    zero = jnp.zeros((N_DH, SSD_CHUNK), F32)
    bts, colss = [], []
    for c in range(NCH):
        tok = slice(c * SSD_CHUNK, (c + 1) * SSD_CHUNK)
        dt, adt = _dt_rows(raw[tok], dtb_ref[...], alog_ref[...])
        sc3 = _scan_mm(adt, u_ref)
        cs = jnp.where(is_fwd, sc3[:, :SSD_CHUNK], sc3[:, SSD_CHUNK:2 * SSD_CHUNK])
        tot = sc3[:, 2 * SSD_CHUNK:]
        e1 = jnp.exp(cs)
        w = jnp.exp(tot - cs) * dt
        rows_ref[0, c] = jnp.concatenate([cs, dt], axis=0)
        table = jnp.concatenate([zero] * 3 + _split(e1, 2) + _split(w, 2) + [cs], axis=0).T
        colf_ref[0, c] = table
        colss.append(table.astype(BF16))
        cols_ref[0, c] = colss[c]
        bt = [slabs[4 + g][tok].T.astype(BF16) for g in range(SSD_GROUPS)]
        for g in range(SSD_GROUPS):
            bt_ref[0, c, g] = bt[g]
        bts.append(bt)

    state = [st_ref[g] for g in range(SSD_GROUPS)]
    for c in reversed(range(NCH)):
        tok = slice(c * SSD_CHUNK, (c + 1) * SSD_CHUNK)
        cols = colss[c]
        wxb = jnp.dot(cols, exw_ref[:, SSD_WIDTH:], preferred_element_type=F32)
        dec = jnp.dot(cols[0:16], exe_ref[:, SSD_WIDTH:], preferred_element_type=F32)[0:1]
        xw = (xs[tok] * wxb).astype(BF16)
        for g in range(SSD_GROUPS):
            gs = slice(g * GROUP_COLS, (g + 1) * GROUP_COLS)
            sb_ref[0, c, g] = state[g].astype(BF16)
            local = jnp.dot(bts[c][g], xw[:, gs], preferred_element_type=F32)
            state[g] = state[g] * dec[:, gs] + local
    for g in range(SSD_GROUPS):
        st_ref[g] = state[g]


def _proj_call(x, mod_lat, lng, lnb, w_in, cw, scw, scb, dtb, alog, exe, exw, u_chunk, h0):
    rev = lambda b, t: (b, NT - 1 - t, 0)
    rev4 = lambda b, t: (b, NT - 1 - t, 0, 0)
    rev5 = lambda b, t: (b, NT - 1 - t, 0, 0, 0)
    tok = lambda width: pl.BlockSpec((1, TM, width), rev)
    out_shape = [
        jax.ShapeDtypeStruct((BATCH, SEQ, CONV_WIDTH), BF16),
        jax.ShapeDtypeStruct((BATCH, SEQ, SSD_WIDTH), BF16),
        jax.ShapeDtypeStruct((BATCH, SEQ, SSD_WIDTH), BF16),
        jax.ShapeDtypeStruct((BATCH, NCHUNK, SSD_GROUPS, SSD_STATE, SSD_CHUNK), BF16),
        jax.ShapeDtypeStruct((BATCH, SEQ, SSD_GN), BF16),
        jax.ShapeDtypeStruct((BATCH, NCHUNK, 2 * N_DH, SSD_CHUNK), F32),
        jax.ShapeDtypeStruct((BATCH, NCHUNK, SSD_CHUNK, LANES), BF16),
        jax.ShapeDtypeStruct((BATCH, NCHUNK, SSD_CHUNK, LANES), F32),
        jax.ShapeDtypeStruct((BATCH, NCHUNK, SSD_GROUPS, SSD_STATE, GROUP_COLS), BF16),
    ]
    out_specs = [
        tok(CONV_WIDTH), tok(SSD_WIDTH), tok(SSD_WIDTH),
        pl.BlockSpec((1, NCH, SSD_GROUPS, SSD_STATE, SSD_CHUNK), rev5),
        tok(SSD_GN),
        pl.BlockSpec((1, NCH, 2 * N_DH, SSD_CHUNK), rev4),
        pl.BlockSpec((1, NCH, SSD_CHUNK, LANES), rev4),
        pl.BlockSpec((1, NCH, SSD_CHUNK, LANES), rev4),
        pl.BlockSpec((1, NCH, SSD_GROUPS, SSD_STATE, GROUP_COLS), rev5),
    ]
    in_specs = [
        pl.BlockSpec((1, TM, D_MODEL), rev),
        pl.BlockSpec((1, 6, D_MODEL), lambda b, t: (b, 0, 0)),
        _const_spec((1, D_MODEL)), _const_spec((1, D_MODEL)),
        _const_spec((D_MODEL, IN_PAD)),
        _const_spec((3, CONV_WIDTH)), _const_spec((3, XBC_DIM)), _const_spec((1, XBC_DIM)),
        _const_spec((N_DH, 1)), _const_spec((N_DH, 1)),
        _const_spec((LANES, 2 * SSD_WIDTH)), _const_spec((LANES, 2 * SSD_WIDTH)),
        _const_spec((SSD_CHUNK, 3 * SSD_CHUNK)),
        pl.BlockSpec((1, 1, SSD_GROUPS, SSD_STATE, GROUP_COLS), lambda b, t: (b, 1, 0, 0, 0)),
    ]
    return pl.pallas_call(
        _proj_kernel,
        grid=(BATCH, NT),
        in_specs=in_specs,
        out_specs=out_specs,
        out_shape=out_shape,
        scratch_shapes=[pltpu.VMEM((SSD_GROUPS, SSD_STATE, GROUP_COLS), F32)],
        compiler_params=pltpu.CompilerParams(dimension_semantics=("arbitrary", "arbitrary"),
                                             vmem_limit_bytes=VMEM_LIMIT),
        name="proj",
    )(x, mod_lat, lng, lnb, w_in, cw, scw, scb, dtb, alog, exe, exw, u_chunk, h0)


def _out_kernel(x_ref, mod_ref, lng_ref, lnb_ref, ycv_ref, zg_ref, xs_ref, bt_ref, cm_ref, rows_ref,
                cols_ref, colf_ref, sb_ref, h0_ref, exw_ref, dx_ref, nw_ref, wout_ref,
                ln1g_ref, ln1b_ref, wff1_ref, wff2_ref, ln2g_ref, ln2b_ref,
                out_ref, st_ref, yn_ref, h1_ref, u2_ref, g2_ref, r2_ref, hid_ref):
    s = pl.program_id(0)
    wr = s % 2
    rd = (s + 1) % 2
    li = lax.broadcasted_iota(jnp.int32, (SSD_CHUNK, SSD_CHUNK), 0)
    si = lax.broadcasted_iota(jnp.int32, (SSD_CHUNK, SSD_CHUNK), 1)
    low = li >= si
    diag = li == si
    lo_half = si < SSD_HEADDIM

    def ssd_chunk(c, state, fill):
        tok = slice(c * SSD_CHUNK, (c + 1) * SSD_CHUNK)
        rows = rows_ref[0, c]
        cols = cols_ref[0, c]
        xs = xs_ref[0, tok, :]
        cm = cm_ref[0, tok, :]
        colf = colf_ref[0, c]
        bc = [jnp.broadcast_to(colf[:, COL_CSF + j:COL_CSF + j + 1], (SSD_CHUNK, LANES))
              for j in range(N_DH)]
        e1x = [jnp.concatenate(
            [jnp.exp(jnp.where(lo_half, bc[d * SSD_HEADS + 2 * k], bc[d * SSD_HEADS + 2 * k + 1]))
             for k in range(SSD_HEADS // 2)], axis=1) for d in range(N_DIRS)]
        wxf = jnp.dot(cols, exw_ref[:, :SSD_WIDTH], preferred_element_type=F32)
        dec = e1x[0][SSD_CHUNK - 1:SSD_CHUNK, :]
        gmat = [jnp.dot(cm[:, g * SSD_STATE:(g + 1) * SSD_STATE], bt_ref[0, c, g],
                        preferred_element_type=F32) for g in range(SSD_GROUPS)]

        ys = []
        for k in range(SSD_HEADS // 2):
            ms = []
            for h in (2 * k, 2 * k + 1):
                g = h // (SSD_HEADS // SSD_GROUPS)
                hb = SSD_HEADS + h
                arg = jnp.where(low,
                                bc[h] - rows[h:h + 1, :],
                                bc[hb] - rows[hb:hb + 1, :])
                dtf = rows[N_DH + h:N_DH + h + 1, :]
                dtb = rows[N_DH + hb:N_DH + hb + 1, :]
                fac = jnp.where(low, dtf, dtb) + jnp.where(diag, dtb, 0.0)
                ms.append((gmat[g] * jnp.exp(arg) * fac).astype(BF16))
            xp = xs[:, k * LANES:(k + 1) * LANES]
            rhs = jnp.concatenate([jnp.where(lo_half, xp, jnp.zeros_like(xp)),
                                   jnp.where(lo_half, jnp.zeros_like(xp), xp)], axis=0)
            ys.append(jnp.dot(jnp.concatenate(ms, axis=1), rhs, preferred_element_type=F32))
            fill[k]()
        y = jnp.concatenate(ys, axis=1)

        yf, yb = [], []
        for g in range(SSD_GROUPS):
            cg = cm[:, g * SSD_STATE:(g + 1) * SSD_STATE]
            yf.append(jnp.dot(cg, state[g].astype(BF16), preferred_element_type=F32))
            yb.append(jnp.dot(cg, sb_ref[0, c, g], preferred_element_type=F32))
        y = (y + jnp.concatenate(yf, axis=1) * e1x[0]
             + jnp.concatenate(yb, axis=1) * e1x[1]
             + xs.astype(F32) * dx_ref[...])

        yg = y * zg_ref[0, tok, :].astype(F32)
        ms_ = jnp.mean(yg * yg, axis=-1, keepdims=True)
        yn_ref[tok, :] = (yg * lax.rsqrt(ms_ + RMS_EPS) * nw_ref[...]).astype(BF16)

        xw = (xs.astype(F32) * wxf).astype(BF16)
        new_state = []
        for g in range(SSD_GROUPS):
            gs = slice(g * GROUP_COLS, (g + 1) * GROUP_COLS)
            local = jnp.dot(bt_ref[0, c, g], xw[:, gs], preferred_element_type=F32)
            new_state.append(state[g] * dec[:, gs] + local)
        return new_state

    def mlp_up(j, k):
        blk = slice(j * FF_BLK + k * FF_SUB, j * FF_BLK + (k + 1) * FF_SUB)
        hid = jnp.maximum(jnp.dot(u2_ref[...], wff1_ref[:, blk], preferred_element_type=F32), 0.0)
        hid_ref[:, blk] = (hid * hid).astype(BF16)

    def mlp_down(j, acc):
        blk = slice(j * FF_BLK, (j + 1) * FF_BLK)
        part = jnp.dot(hid_ref[:, blk], wff2_ref[blk, :], preferred_element_type=F32)
        return part if acc is None else acc + part

    def ln2_rows(i):
        r = slice(i * SSD_CHUNK, (i + 1) * SSD_CHUNK)
        out_ref[0, r, :] = _ln_hat(r2_ref[r, :]) * ln2g_ref[...] + ln2b_ref[...]

    def ln_in_rows(i):
        r = slice(i * SSD_CHUNK, (i + 1) * SSD_CHUNK)
        h1_ref[wr, r, :] = _ln_hat(x_ref[0, r, :]) * lng_ref[...] + lnb_ref[...]

    def step(mixer, mlp, norm):
        nothing = lambda: None
        state, acc = None, None
        if mixer:
            @pl.when(jnp.minimum(s, N_TILES - 1) % NT == 0)
            def _():
                st_ref[...] = h0_ref[0, 0]
            state = [st_ref[g] for g in range(SSD_GROUPS)]
        for c in range(NCH):
            ups = [(lambda k=k: mlp_up(c, k)) if mlp else nothing for k in range(FF_BLK // FF_SUB)]
            if mixer:
                state = ssd_chunk(c, state, ups)
            else:
                for up in ups:
                    up()
            if mixer:
                ln_in_rows(c)
            if norm:
                ln2_rows(c)
        if mixer:
            for g in range(SSD_GROUPS):
                st_ref[g] = state[g]
            mix = (jnp.dot(ycv_ref[0], wout_ref[:CONV_WIDTH, :], preferred_element_type=F32)
                   + jnp.dot(yn_ref[...], wout_ref[CONV_WIDTH:, :], preferred_element_type=F32))
            m = mod_ref[0]
            g2_ref[wr] = m[5:6]
        for j in range(NCH):
            if mlp:
                acc = mlp_down(j, acc)
            if mixer:
                r = slice(j * SSD_CHUNK, (j + 1) * SSD_CHUNK)
                h1 = _ln_hat(ALPHA * h1_ref[wr, r, :] + m[2:3] * mix[r]) * ln1g_ref[...] + ln1b_ref[...]
                h1_ref[wr, r, :] = h1
                u2_ref[r, :] = (h1 * (1.0 + m[4:5]) + m[3:4]).astype(BF16)
        if mlp:
            r2_ref[...] = ALPHA * h1_ref[rd] + g2_ref[rd] * acc

    @pl.when(s == 0)
    def _():
        r2_ref[...] = jnp.zeros((TM, D_MODEL), F32)
        step(True, False, False)

    @pl.when(jnp.logical_and(s >= 1, s <= N_TILES))
    def _():
        step(True, True, True)

    @pl.when(s == N_TILES + 1)
    def _():
        step(False, False, True)


def _out_call(x, mod_lat, lng, lnb, ycv, zg, xs, bt, cm, rows, cols, colf, sb, h0, exw, dx, nw,
              wout, ln1g, ln1b, wff1, wff2, ln2g, ln2b):
    cur = lambda s: jnp.minimum(s, N_TILES - 1)
    fwd = lambda s: (cur(s) // NT, cur(s) % NT, 0)
    fwd4 = lambda s: (cur(s) // NT, cur(s) % NT, 0, 0)
    fwd5 = lambda s: (cur(s) // NT, cur(s) % NT, 0, 0, 0)
    done = lambda s: (jnp.maximum(s - 2, 0) // NT, jnp.maximum(s - 2, 0) % NT, 0)
    tok = lambda width: pl.BlockSpec((1, TM, width), fwd)
    in_specs = [
        tok(D_MODEL),
        pl.BlockSpec((1, 6, D_MODEL), lambda s: (cur(s) // NT, 0, 0)),
        _const_spec((1, D_MODEL)), _const_spec((1, D_MODEL)),
        tok(CONV_WIDTH), tok(SSD_WIDTH), tok(SSD_WIDTH),
        pl.BlockSpec((1, NCH, SSD_GROUPS, SSD_STATE, SSD_CHUNK), fwd5),
        tok(SSD_GN),
        pl.BlockSpec((1, NCH, 2 * N_DH, SSD_CHUNK), fwd4),
        pl.BlockSpec((1, NCH, SSD_CHUNK, LANES), fwd4),
        pl.BlockSpec((1, NCH, SSD_CHUNK, LANES), fwd4),
        pl.BlockSpec((1, NCH, SSD_GROUPS, SSD_STATE, GROUP_COLS), fwd5),
        pl.BlockSpec((1, 1, SSD_GROUPS, SSD_STATE, GROUP_COLS), lambda s: (cur(s) // NT, 0, 0, 0, 0)),
        _const_spec((LANES, 2 * SSD_WIDTH)),
        _const_spec((1, SSD_WIDTH)), _const_spec((1, SSD_WIDTH)),
        _const_spec((D_MODEL, D_MODEL)),
        _const_spec((1, D_MODEL)), _const_spec((1, D_MODEL)),
        _const_spec((D_MODEL, D_FF)), _const_spec((D_FF, D_MODEL)),
        _const_spec((1, D_MODEL)), _const_spec((1, D_MODEL)),
    ]
    return pl.pallas_call(
        _out_kernel,
        grid=(N_TILES + 2,),
        in_specs=in_specs,
        out_specs=pl.BlockSpec((1, TM, D_MODEL), done),
        out_shape=jax.ShapeDtypeStruct((BATCH, SEQ, D_MODEL), F32),
        scratch_shapes=[pltpu.VMEM((SSD_GROUPS, SSD_STATE, GROUP_COLS), F32),
                        pltpu.VMEM((TM, SSD_WIDTH), BF16),
                        pltpu.VMEM((2, TM, D_MODEL), F32),
                        pltpu.VMEM((TM, D_MODEL), BF16),
                        pltpu.VMEM((2, 1, D_MODEL), F32),
                        pltpu.VMEM((TM, D_MODEL), F32),
                        pltpu.VMEM((TM, D_FF), BF16)],
        compiler_params=pltpu.CompilerParams(dimension_semantics=("arbitrary",),
                                             vmem_limit_bytes=VMEM_LIMIT),
        name="out",
    )(x, mod_lat, lng, lnb, ycv, zg, xs, bt, cm, rows, cols, colf, sb, h0, exw, dx, nw,
      wout, ln1g, ln1b, wff1, wff2, ln2g, ln2b)


def kernel(x, c, ctx, c_ctx, ln_in_g, ln_in_b, w_mod, b_mod, w_in, conv_w, ssd_conv_w, ssd_conv_b,
           dt_bias, a_log, ssd_d, ssd_norm_w, w_out, ln1_g, ln1_b, w_ff1, w_ff2, ln2_g, ln2_b):
    row = lambda v: v.reshape(1, -1).astype(F32)
    cvec = jnp.concatenate([c, c_ctx[None, :], jnp.zeros((16 - BATCH - 1, D_MODEL), F32)], axis=0)
    mod = _mod_call(cvec, w_mod[0], row(b_mod[0]))
    mod_lat = mod[:BATCH].reshape(BATCH, 6, D_MODEL)
    mod_ctx = mod[BATCH:BATCH + 1].reshape(1, 6, D_MODEL)

    w_in_p, w_out_p, w_ff1_p, w_ff2_p = _wprep_call(w_in[0].T, w_out, w_ff1, w_ff2)
    lng, lnb = row(ln_in_g), row(ln_in_b)
    scw, scb = ssd_conv_w[0], row(ssd_conv_b[0])
    dtb = dt_bias[0].reshape(N_DH, 1)
    alog = a_log[0].reshape(N_DH, 1)
    exe = jnp.asarray(_EXE, BF16)
    exw = jnp.asarray(_EXW, BF16)

    h0 = _ctx_call(ctx, mod_ctx, lng, lnb, w_in_p, scw, scb, dtb, alog, exw, jnp.asarray(_U_CTX, BF16))
    ycv, zg, xs, bt, cm, rows, cols, colf, sb = _proj_call(
        x, mod_lat, lng, lnb, w_in_p, conv_w[0], scw, scb, dtb, alog, exe, exw,
        jnp.asarray(_U_CHUNK, BF16), h0)
    dx = jnp.repeat(ssd_d[0], SSD_HEADDIM).reshape(1, SSD_WIDTH)
    return _out_call(x, mod_lat, lng, lnb, ycv, zg, xs, bt, cm, rows, cols, colf, sb, h0, exw,
                     dx, row(ssd_norm_w[0]), w_out_p, row(ln1_g[0]), row(ln1_b[0]),
                     w_ff1_p, w_ff2_p, row(ln2_g[0]), row(ln2_b[0]))
```

```python
import jax
import jax.numpy as jnp
import numpy as np
from jax import lax
from jax.experimental import pallas as pl
from jax.experimental.pallas import tpu as pltpu

F32 = jnp.float32
BF16 = jnp.bfloat16

D_MODEL = 1024
BATCH = 8
SEQ = 2048
CTX_LEN = 256
GRID_W = 64
CONV_WIDTH = 512
SSD_WIDTH = 512
SSD_HEADDIM = 64
SSD_HEADS = 8
SSD_GROUPS = 2
SSD_STATE = 128
SSD_CHUNK = 128
N_DIRS = 2
D_FF = 4 * D_MODEL
LN_EPS = 1e-5
RMS_EPS = 1e-5
SSD_GN = SSD_GROUPS * SSD_STATE
XBC_DIM = SSD_WIDTH + 2 * SSD_GN
Z_OFF = 3 * CONV_WIDTH
XBC_OFF = Z_OFF + SSD_WIDTH
DT_OFF = XBC_OFF + XBC_DIM
N_DH = N_DIRS * SSD_HEADS
LANES = 128
P_DT = 0
P_XBC = LANES
P_Z = P_XBC + XBC_DIM
P_CONV = P_Z + SSD_WIDTH
IN_PAD = P_CONV + 3 * CONV_WIDTH
CTX_PAD = P_Z
GROUP_COLS = (SSD_HEADS // SSD_GROUPS) * SSD_HEADDIM
ALPHA = 2.0 ** 0.25

TM = 512
NCH = TM // SSD_CHUNK
NT = SEQ // TM
NCHUNK = SEQ // SSD_CHUNK
N_TILES = BATCH * NT
FF_BLK = 1024
FF_SUB = 256
VMEM_LIMIT = 58 * 1024 * 1024

COL_E1 = 48
COL_W = 80
COL_CSF = 112


def _expansion(col0, pieces, width):
    m = np.zeros((LANES, N_DH * width), np.float32)
    for t in range(pieces):
        for j in range(N_DH):
            m[col0 + 16 * t + j, j * width:(j + 1) * width] = 1.0
    return m


_EXE = _expansion(COL_E1, 2, SSD_HEADDIM)
_EXW = _expansion(COL_W, 2, SSD_HEADDIM)


def _ln_hat(x):
    mu = jnp.mean(x, axis=-1, keepdims=True)
    xc = x - mu
    var = jnp.mean(xc * xc, axis=-1, keepdims=True)
    return xc * lax.rsqrt(var + LN_EPS)


def _silu(x):
    return x / (1.0 + jnp.exp(-x))


def _softplus(x):
    return jnp.maximum(x, 0.0) + jnp.log1p(jnp.exp(-jnp.abs(x)))


def _edge_masks(rows, period):
    pos = lax.broadcasted_iota(jnp.int32, (rows, LANES), 0) % period
    return (pos != 0).astype(F32), (pos != period - 1).astype(F32)


def _conv3(t, w, mprev, mnext):
    rows = t.shape[0]
    prev = pltpu.roll(t, 1, 0) * mprev
    nxt = pltpu.roll(t, rows - 1, 0) * mnext
    return prev * w[0:1, :] + t * w[1:2, :] + nxt * w[2:3, :]


def _split(v, pieces):
    out = []
    for _ in range(pieces - 1):
        p = v.astype(BF16).astype(F32)
        out.append(p)
        v = v - p
    out.append(v.astype(BF16).astype(F32))
    return out


def _tri(length, op):
    i = np.arange(length)
    return op(i[:, None], i[None, :]).astype(np.float32)


_U_CHUNK = np.concatenate([_tri(SSD_CHUNK, np.less_equal), _tri(SSD_CHUNK, np.greater_equal),
                           np.ones((SSD_CHUNK, SSD_CHUNK), np.float32)], axis=1)
_U_CTX = np.concatenate([_tri(CTX_LEN, np.greater), _tri(CTX_LEN, np.less)], axis=1)


def _scan_mm(v, u_ref):
    pieces = jnp.concatenate(_split(v, 3), axis=0).astype(BF16)
    o = jnp.dot(pieces, u_ref[...], preferred_element_type=F32)
    return o[0:N_DH] + o[N_DH:2 * N_DH] + o[2 * N_DH:3 * N_DH]


def _dt_rows(raw, dtb, a_log):
    r = raw.T[0:N_DH, :] + dtb
    dt = _softplus(r)
    return dt, dt * (-jnp.exp(a_log))


def _fwd_rows():
    return lax.broadcasted_iota(jnp.int32, (N_DH, 1), 0) < SSD_HEADS


def _const_spec(shape):
    nd = len(shape)
    return pl.BlockSpec(shape, lambda *_: (0,) * nd, pipeline_mode=pl.Buffered(1))


def _mod_kernel(c_ref, w_ref, b_ref, o_ref):
    s = _silu(c_ref[...]).astype(BF16)
    o_ref[...] = jnp.dot(s, w_ref[...].astype(BF16), preferred_element_type=F32) + b_ref[...]


def _mod_call(cvec, w_mod, b_mod):
    tn = 1536
    return pl.pallas_call(
        _mod_kernel,
        grid=(6 * D_MODEL // tn,),
        in_specs=[pl.BlockSpec((16, D_MODEL), lambda j: (0, 0)),
                  pl.BlockSpec((D_MODEL, tn), lambda j: (0, j)),
                  pl.BlockSpec((1, tn), lambda j: (0, j))],
        out_specs=pl.BlockSpec((16, tn), lambda j: (0, j)),
        out_shape=jax.ShapeDtypeStruct((16, 6 * D_MODEL), F32),
        compiler_params=pltpu.CompilerParams(dimension_semantics=("arbitrary",),
                                             vmem_limit_bytes=VMEM_LIMIT),
        name="mod",
    )(cvec, w_mod, b_mod)


def _wprep_kernel(wint_ref, wout_ref, wff1_ref, wff2_ref, pin_ref, pout_ref, pff1_ref, pff2_ref):
    dt_rows = jnp.concatenate([wint_ref[DT_OFF:DT_OFF + N_DH, :],
                               jnp.zeros((LANES - N_DH, wint_ref.shape[1]), F32)], axis=0)
    pin_ref[:, P_DT:P_DT + LANES] = dt_rows.T.astype(BF16)
    def move(dst, src):
        pin_ref[:, dst:dst + LANES] = wint_ref[src:src + LANES, :].T.astype(BF16)

    for dst, src, width in ((P_XBC, XBC_OFF, XBC_DIM), (P_Z, Z_OFF, SSD_WIDTH)):
        for j in range(0, width, LANES):
            move(dst + j, src + j)
    for j in range(CONV_WIDTH // LANES):
        for k in range(3):
            move(P_CONV + (3 * j + k) * LANES, k * CONV_WIDTH + j * LANES)
    pout_ref[...] = wout_ref[0].astype(BF16)
    pff1_ref[...] = wff1_ref[0].astype(BF16)
    pff2_ref[...] = wff2_ref[0].astype(BF16)


def _wprep_call(w_in_t, w_out, w_ff1, w_ff2):
    steps = 8
    r1, r4 = D_MODEL // steps, D_FF // steps
    return pl.pallas_call(
        _wprep_kernel,
        grid=(steps,),
        in_specs=[pl.BlockSpec((w_in_t.shape[0], r1), lambda i: (0, i)),
                  pl.BlockSpec((1, r1, D_MODEL), lambda i: (0, i, 0)),
                  pl.BlockSpec((1, r1, D_FF), lambda i: (0, i, 0)),
                  pl.BlockSpec((1, r4, D_MODEL), lambda i: (0, i, 0))],
        out_specs=[pl.BlockSpec((r1, IN_PAD), lambda i: (i, 0)),
                   pl.BlockSpec((r1, D_MODEL), lambda i: (i, 0)),
                   pl.BlockSpec((r1, D_FF), lambda i: (i, 0)),
                   pl.BlockSpec((r4, D_MODEL), lambda i: (i, 0))],
        out_shape=[jax.ShapeDtypeStruct((D_MODEL, IN_PAD), BF16),
                   jax.ShapeDtypeStruct((D_MODEL, D_MODEL), BF16),
                   jax.ShapeDtypeStruct((D_MODEL, D_FF), BF16),
                   jax.ShapeDtypeStruct((D_FF, D_MODEL), BF16)],
        compiler_params=pltpu.CompilerParams(dimension_semantics=("arbitrary",),
                                             vmem_limit_bytes=VMEM_LIMIT),
        name="wprep",
    )(w_in_t, w_out, w_ff1, w_ff2)


def _ctx_kernel(x_ref, mod_ref, lng_ref, lnb_ref, w_ref, scw_ref, scb_ref, dtb_ref, alog_ref,
                exw_ref, u_ref, h0_ref):
    m = mod_ref[0]
    sc = 1.0 + m[1:2]
    u = _ln_hat(x_ref[0]) * (lng_ref[...] * sc) + (lnb_ref[...] * sc + m[0:1])
    proj = jnp.dot(u.astype(BF16), w_ref[...], preferred_element_type=F32)
    mprev, mnext = _edge_masks(CTX_LEN, CTX_LEN)
    slabs = []
    for j in range((SSD_WIDTH + SSD_GN) // LANES):
        sl = slice(j * LANES, (j + 1) * LANES)
        pj = proj[:, P_XBC + j * LANES:P_XBC + (j + 1) * LANES]
        slabs.append(_silu(_conv3(pj, scw_ref[:, sl], mprev, mnext) + scb_ref[:, sl]))
    xs = jnp.concatenate(slabs[:4], axis=1)
    dt, adt = _dt_rows(proj[:, P_DT:P_DT + LANES], dtb_ref[...], alog_ref[...])
    sc2 = _scan_mm(adt, u_ref)
    excl = jnp.where(_fwd_rows(), sc2[:, :CTX_LEN], sc2[:, CTX_LEN:])
    w = jnp.exp(excl) * dt
    zero = jnp.zeros((N_DH, CTX_LEN), F32)
    table = jnp.concatenate([zero] * (COL_W // 16) + _split(w, 2) + [zero], axis=0)
    cols = table.T.astype(BF16)
    wx = jnp.dot(cols, exw_ref[...], preferred_element_type=F32)
    for d in range(N_DIRS):
        xw = (xs * wx[:, d * SSD_WIDTH:(d + 1) * SSD_WIDTH]).astype(BF16)
        for g in range(SSD_GROUPS):
            bt = slabs[4 + g].T.astype(BF16)
            h0_ref[0, d, g] = jnp.dot(bt, xw[:, g * GROUP_COLS:(g + 1) * GROUP_COLS],
                                      preferred_element_type=F32)


def _ctx_call(ctx, mod_ctx, lng, lnb, w_in, scw, scb, dtb, alog, exw, u_ctx):
    return pl.pallas_call(
        _ctx_kernel,
        grid=(BATCH,),
        in_specs=[pl.BlockSpec((1, CTX_LEN, D_MODEL), lambda b: (b, 0, 0)),
                  _const_spec((1, 6, D_MODEL)), _const_spec((1, D_MODEL)), _const_spec((1, D_MODEL)),
                  _const_spec((D_MODEL, CTX_PAD)), _const_spec((3, XBC_DIM)), _const_spec((1, XBC_DIM)),
                  _const_spec((N_DH, 1)), _const_spec((N_DH, 1)), _const_spec((LANES, 2 * SSD_WIDTH)),
                  _const_spec((CTX_LEN, 2 * CTX_LEN))],
        out_specs=pl.BlockSpec((1, N_DIRS, SSD_GROUPS, SSD_STATE, GROUP_COLS),
                               lambda b: (b, 0, 0, 0, 0)),
        out_shape=jax.ShapeDtypeStruct((BATCH, N_DIRS, SSD_GROUPS, SSD_STATE, GROUP_COLS), F32),
        compiler_params=pltpu.CompilerParams(dimension_semantics=("arbitrary",),
                                             vmem_limit_bytes=VMEM_LIMIT),
        name="ctx",
    )(ctx, mod_ctx, lng, lnb, w_in, scw, scb, dtb, alog, exw, u_ctx)


def _proj_kernel(x_ref, mod_ref, lng_ref, lnb_ref, w_ref, cw_ref, scw_ref, scb_ref, dtb_ref,
                 alog_ref, exe_ref, exw_ref, u_ref, h0_ref,
                 ycv_ref, zg_ref, xs_ref, bt_ref, cm_ref, rows_ref, cols_ref, colf_ref, sb_ref, st_ref):
    @pl.when(pl.program_id(1) == 0)
    def _():
        st_ref[...] = h0_ref[0, 0]

    m = mod_ref[0]
    sc = 1.0 + m[1:2]
    u = _ln_hat(x_ref[0]) * (lng_ref[...] * sc) + (lnb_ref[...] * sc + m[0:1])
    ub = u.astype(BF16)
    mprev, mnext = _edge_masks(TM, GRID_W)
    is_fwd = _fwd_rows()
    zero = jnp.zeros((N_DH, SSD_CHUNK), F32)
    chunks = [slice(c * SSD_CHUNK, (c + 1) * SSD_CHUNK) for c in range(NCH)]

    def conv_slabs(p, first):
        for i in range(2):
            gb, gc, gh = (p[:, (3 * i + k) * LANES:(3 * i + k + 1) * LANES] for k in range(3))
            sl = slice((first + i) * LANES, (first + i + 1) * LANES)
            ycv_ref[0, :, sl] = (gb * _conv3(gc * gh, cw_ref[:, sl], mprev, mnext)).astype(BF16)

    pa = jnp.dot(ub, w_ref[:, :P_Z], preferred_element_type=F32)
    dts = [_dt_rows(pa[tok, P_DT:P_DT + LANES], dtb_ref[...], alog_ref[...]) for tok in chunks]
    pz = jnp.dot(ub, w_ref[:, P_Z:P_CONV], preferred_element_type=F32)

    colss = []
    for c, (dt, adt) in enumerate(dts):
        sc3 = _scan_mm(adt, u_ref)
        cs = jnp.where(is_fwd, sc3[:, :SSD_CHUNK], sc3[:, SSD_CHUNK:2 * SSD_CHUNK])
        tot = sc3[:, 2 * SSD_CHUNK:]
        e1 = jnp.exp(cs)
        w = jnp.exp(tot - cs) * dt
        rows_ref[0, c] = jnp.concatenate([cs, dt], axis=0)
        table = jnp.concatenate([zero] * 3 + _split(e1, 2) + _split(w, 2) + [cs], axis=0).T
        colf_ref[0, c] = table
        colss.append(table.astype(BF16))
        cols_ref[0, c] = colss[c]

    pc1 = jnp.dot(ub, w_ref[:, P_CONV:P_CONV + 6 * LANES], preferred_element_type=F32)

    slabs = []
    for j in range(XBC_DIM // LANES):
        sl = slice(j * LANES, (j + 1) * LANES)
        pj = pa[:, P_XBC + j * LANES:P_XBC + (j + 1) * LANES]
        slabs.append(_silu(_conv3(pj, scw_ref[:, sl], mprev, mnext) + scb_ref[:, sl]))
    xs = jnp.concatenate(slabs[:4], axis=1)
    xs_ref[0] = xs.astype(BF16)
    cm_ref[0] = jnp.concatenate(slabs[6:8], axis=1).astype(BF16)
    zg_ref[0] = _silu(pz).astype(BF16)
    bts = []
    for c, tok in enumerate(chunks):
        bt = [slabs[4 + g][tok].T.astype(BF16) for g in range(SSD_GROUPS)]
        for g in range(SSD_GROUPS):
            bt_ref[0, c, g] = bt[g]
        bts.append(bt)
    wxbs = [jnp.dot(cols, exw_ref[:, SSD_WIDTH:], preferred_element_type=F32) for cols in colss]
    decs = [jnp.dot(cols[0:16], exe_ref[:, SSD_WIDTH:], preferred_element_type=F32)[0:1]
            for cols in colss]

    pc2 = jnp.dot(ub, w_ref[:, P_CONV + 6 * LANES:], preferred_element_type=F32)
    conv_slabs(pc1, 0)

    state = [st_ref[g] for g in range(SSD_GROUPS)]
    for c in reversed(range(NCH)):
        xw = (xs[chunks[c]] * wxbs[c]).astype(BF16)
        for g in range(SSD_GROUPS):
            gs = slice(g * GROUP_COLS, (g + 1) * GROUP_COLS)
            sb_ref[0, c, g] = state[g].astype(BF16)
            local = jnp.dot(bts[c][g], xw[:, gs], preferred_element_type=F32)
            state[g] = state[g] * decs[c][:, gs] + local
    for g in range(SSD_GROUPS):
        st_ref[g] = state[g]
    conv_slabs(pc2, 2)


def _proj_call(x, mod_lat, lng, lnb, w_in, cw, scw, scb, dtb, alog, exe, exw, u_chunk, h0):
    rev = lambda b, t: (b, NT - 1 - t, 0)
    rev4 = lambda b, t: (b, NT - 1 - t, 0, 0)
    rev5 = lambda b, t: (b, NT - 1 - t, 0, 0, 0)
    tok = lambda width: pl.BlockSpec((1, TM, width), rev)
    out_shape = [
        jax.ShapeDtypeStruct((BATCH, SEQ, CONV_WIDTH), BF16),
        jax.ShapeDtypeStruct((BATCH, SEQ, SSD_WIDTH), BF16),
        jax.ShapeDtypeStruct((BATCH, SEQ, SSD_WIDTH), BF16),
        jax.ShapeDtypeStruct((BATCH, NCHUNK, SSD_GROUPS, SSD_STATE, SSD_CHUNK), BF16),
        jax.ShapeDtypeStruct((BATCH, SEQ, SSD_GN), BF16),
        jax.ShapeDtypeStruct((BATCH, NCHUNK, 2 * N_DH, SSD_CHUNK), F32),
        jax.ShapeDtypeStruct((BATCH, NCHUNK, SSD_CHUNK, LANES), BF16),
        jax.ShapeDtypeStruct((BATCH, NCHUNK, SSD_CHUNK, LANES), F32),
        jax.ShapeDtypeStruct((BATCH, NCHUNK, SSD_GROUPS, SSD_STATE, GROUP_COLS), BF16),
    ]
    out_specs = [
        tok(CONV_WIDTH), tok(SSD_WIDTH), tok(SSD_WIDTH),
        pl.BlockSpec((1, NCH, SSD_GROUPS, SSD_STATE, SSD_CHUNK), rev5),
        tok(SSD_GN),
        pl.BlockSpec((1, NCH, 2 * N_DH, SSD_CHUNK), rev4),
        pl.BlockSpec((1, NCH, SSD_CHUNK, LANES), rev4),
        pl.BlockSpec((1, NCH, SSD_CHUNK, LANES), rev4),
        pl.BlockSpec((1, NCH, SSD_GROUPS, SSD_STATE, GROUP_COLS), rev5),
    ]
    in_specs = [
        pl.BlockSpec((1, TM, D_MODEL), rev),
        pl.BlockSpec((1, 6, D_MODEL), lambda b, t: (b, 0, 0)),
        _const_spec((1, D_MODEL)), _const_spec((1, D_MODEL)),
        _const_spec((D_MODEL, IN_PAD)),
        _const_spec((3, CONV_WIDTH)), _const_spec((3, XBC_DIM)), _const_spec((1, XBC_DIM)),
        _const_spec((N_DH, 1)), _const_spec((N_DH, 1)),
        _const_spec((LANES, 2 * SSD_WIDTH)), _const_spec((LANES, 2 * SSD_WIDTH)),
        _const_spec((SSD_CHUNK, 3 * SSD_CHUNK)),
        pl.BlockSpec((1, 1, SSD_GROUPS, SSD_STATE, GROUP_COLS), lambda b, t: (b, 1, 0, 0, 0)),
    ]
    return pl.pallas_call(
        _proj_kernel,
        grid=(BATCH, NT),
        in_specs=in_specs,
        out_specs=out_specs,
        out_shape=out_shape,
        scratch_shapes=[pltpu.VMEM((SSD_GROUPS, SSD_STATE, GROUP_COLS), F32)],
        compiler_params=pltpu.CompilerParams(dimension_semantics=("arbitrary", "arbitrary"),
                                             vmem_limit_bytes=VMEM_LIMIT),
        name="proj",
    )(x, mod_lat, lng, lnb, w_in, cw, scw, scb, dtb, alog, exe, exw, u_chunk, h0)


def _out_kernel(x_ref, mod_ref, lng_ref, lnb_ref, ycv_ref, zg_ref, xs_ref, bt_ref, cm_ref, rows_ref,
                cols_ref, colf_ref, sb_ref, h0_ref, exw_ref, dx_ref, nw_ref, wout_ref,
                ln1g_ref, ln1b_ref, wff1_ref, wff2_ref, ln2g_ref, ln2b_ref,
                out_ref, st_ref, yn_ref, h1_ref, u2_ref, g2_ref, r2_ref, hid_ref):
    s = pl.program_id(0)
    wr = s % 2
    rd = (s + 1) % 2
    li = lax.broadcasted_iota(jnp.int32, (SSD_CHUNK, SSD_CHUNK), 0)
    si = lax.broadcasted_iota(jnp.int32, (SSD_CHUNK, SSD_CHUNK), 1)
    low = li >= si
    diag = li == si
    lo_half = si < SSD_HEADDIM

    def ssd_chunk(c, state, fill):
        tok = slice(c * SSD_CHUNK, (c + 1) * SSD_CHUNK)
        rows = rows_ref[0, c]
        cols = cols_ref[0, c]
        xs = xs_ref[0, tok, :]
        cm = cm_ref[0, tok, :]
        colf = colf_ref[0, c]
        bc = [jnp.broadcast_to(colf[:, COL_CSF + j:COL_CSF + j + 1], (SSD_CHUNK, LANES))
              for j in range(N_DH)]
        e1x = [jnp.concatenate(
            [jnp.exp(jnp.where(lo_half, bc[d * SSD_HEADS + 2 * k], bc[d * SSD_HEADS + 2 * k + 1]))
             for k in range(SSD_HEADS // 2)], axis=1) for d in range(N_DIRS)]
        wxf = jnp.dot(cols, exw_ref[:, :SSD_WIDTH], preferred_element_type=F32)
        dec = e1x[0][SSD_CHUNK - 1:SSD_CHUNK, :]
        gmat = [jnp.dot(cm[:, g * SSD_STATE:(g + 1) * SSD_STATE], bt_ref[0, c, g],
                        preferred_element_type=F32) for g in range(SSD_GROUPS)]

        ys = []
        for k in range(SSD_HEADS // 2):
            ms = []
            for h in (2 * k, 2 * k + 1):
                g = h // (SSD_HEADS // SSD_GROUPS)
                hb = SSD_HEADS + h
                arg = jnp.where(low,
                                bc[h] - rows[h:h + 1, :],
                                bc[hb] - rows[hb:hb + 1, :])
                dtf = rows[N_DH + h:N_DH + h + 1, :]
                dtb = rows[N_DH + hb:N_DH + hb + 1, :]
                fac = jnp.where(low, dtf, dtb) + jnp.where(diag, dtb, 0.0)
                ms.append((gmat[g] * jnp.exp(arg) * fac).astype(BF16))
            xp = xs[:, k * LANES:(k + 1) * LANES]
            rhs = jnp.concatenate([jnp.where(lo_half, xp, jnp.zeros_like(xp)),
                                   jnp.where(lo_half, jnp.zeros_like(xp), xp)], axis=0)
            ys.append(jnp.dot(jnp.concatenate(ms, axis=1), rhs, preferred_element_type=F32))
            fill[k]()
        y = jnp.concatenate(ys, axis=1)

        yf, yb = [], []
        for g in range(SSD_GROUPS):
            cg = cm[:, g * SSD_STATE:(g + 1) * SSD_STATE]
            yf.append(jnp.dot(cg, state[g].astype(BF16), preferred_element_type=F32))
            yb.append(jnp.dot(cg, sb_ref[0, c, g], preferred_element_type=F32))
        y = (y + jnp.concatenate(yf, axis=1) * e1x[0]
             + jnp.concatenate(yb, axis=1) * e1x[1]
             + xs.astype(F32) * dx_ref[...])

        yg = y * zg_ref[0, tok, :].astype(F32)
        ms_ = jnp.mean(yg * yg, axis=-1, keepdims=True)
        yn_ref[tok, :] = (yg * lax.rsqrt(ms_ + RMS_EPS) * nw_ref[...]).astype(BF16)

        xw = (xs.astype(F32) * wxf).astype(BF16)
        new_state = []
        for g in range(SSD_GROUPS):
            gs = slice(g * GROUP_COLS, (g + 1) * GROUP_COLS)
            local = jnp.dot(bt_ref[0, c, g], xw[:, gs], preferred_element_type=F32)
            new_state.append(state[g] * dec[:, gs] + local)
        return new_state

    def mlp_up(j, k):
        blk = slice(j * FF_BLK + k * FF_SUB, j * FF_BLK + (k + 1) * FF_SUB)
        hid = jnp.maximum(jnp.dot(u2_ref[...], wff1_ref[:, blk], preferred_element_type=F32), 0.0)
        hid_ref[:, blk] = (hid * hid).astype(BF16)

    def mlp_down(j, acc):
        blk = slice(j * FF_BLK, (j + 1) * FF_BLK)
        part = jnp.dot(hid_ref[:, blk], wff2_ref[blk, :], preferred_element_type=F32)
        return part if acc is None else acc + part

    def ln2_rows(i):
        r = slice(i * SSD_CHUNK, (i + 1) * SSD_CHUNK)
        out_ref[0, r, :] = _ln_hat(r2_ref[r, :]) * ln2g_ref[...] + ln2b_ref[...]

    def ln_in_rows(i):
        r = slice(i * SSD_CHUNK, (i + 1) * SSD_CHUNK)
        h1_ref[wr, r, :] = _ln_hat(x_ref[0, r, :]) * lng_ref[...] + lnb_ref[...]

    def step(mixer, mlp, norm):
        nothing = lambda: None
        state, acc = None, None
        if mixer:
            @pl.when(jnp.minimum(s, N_TILES - 1) % NT == 0)
            def _():
                st_ref[...] = h0_ref[0, 0]
            state = [st_ref[g] for g in range(SSD_GROUPS)]
        for c in range(NCH):
            ups = [(lambda k=k: mlp_up(c, k)) if mlp else nothing for k in range(FF_BLK // FF_SUB)]
            if mixer:
                state = ssd_chunk(c, state, ups)
            else:
                for up in ups:
                    up()
            if mixer:
                ln_in_rows(c)
            if norm:
                ln2_rows(c)
        if mixer:
            for g in range(SSD_GROUPS):
                st_ref[g] = state[g]
            mix = (jnp.dot(ycv_ref[0], wout_ref[:CONV_WIDTH, :], preferred_element_type=F32)
                   + jnp.dot(yn_ref[...], wout_ref[CONV_WIDTH:, :], preferred_element_type=F32))
            m = mod_ref[0]
            g2_ref[wr] = m[5:6]
        for j in range(NCH):
            if mlp:
                acc = mlp_down(j, acc)
            if mixer:
                r = slice(j * SSD_CHUNK, (j + 1) * SSD_CHUNK)
                h1 = _ln_hat(ALPHA * h1_ref[wr, r, :] + m[2:3] * mix[r]) * ln1g_ref[...] + ln1b_ref[...]
                h1_ref[wr, r, :] = h1
                u2_ref[r, :] = (h1 * (1.0 + m[4:5]) + m[3:4]).astype(BF16)
        if mlp:
            r2_ref[...] = ALPHA * h1_ref[rd] + g2_ref[rd] * acc

    @pl.when(s == 0)
    def _():
        r2_ref[...] = jnp.zeros((TM, D_MODEL), F32)
        step(True, False, False)

    @pl.when(jnp.logical_and(s >= 1, s <= N_TILES))
    def _():
        step(True, True, True)

    @pl.when(s == N_TILES + 1)
    def _():
        step(False, False, True)


def _out_call(x, mod_lat, lng, lnb, ycv, zg, xs, bt, cm, rows, cols, colf, sb, h0, exw, dx, nw,
              wout, ln1g, ln1b, wff1, wff2, ln2g, ln2b):
    cur = lambda s: jnp.minimum(s, N_TILES - 1)
    fwd = lambda s: (cur(s) // NT, cur(s) % NT, 0)
    fwd4 = lambda s: (cur(s) // NT, cur(s) % NT, 0, 0)
    fwd5 = lambda s: (cur(s) // NT, cur(s) % NT, 0, 0, 0)
    done = lambda s: (jnp.maximum(s - 2, 0) // NT, jnp.maximum(s - 2, 0) % NT, 0)
    tok = lambda width: pl.BlockSpec((1, TM, width), fwd)
    in_specs = [
        tok(D_MODEL),
        pl.BlockSpec((1, 6, D_MODEL), lambda s: (cur(s) // NT, 0, 0)),
        _const_spec((1, D_MODEL)), _const_spec((1, D_MODEL)),
        tok(CONV_WIDTH), tok(SSD_WIDTH), tok(SSD_WIDTH),
        pl.BlockSpec((1, NCH, SSD_GROUPS, SSD_STATE, SSD_CHUNK), fwd5),
        tok(SSD_GN),
        pl.BlockSpec((1, NCH, 2 * N_DH, SSD_CHUNK), fwd4),
        pl.BlockSpec((1, NCH, SSD_CHUNK, LANES), fwd4),
        pl.BlockSpec((1, NCH, SSD_CHUNK, LANES), fwd4),
        pl.BlockSpec((1, NCH, SSD_GROUPS, SSD_STATE, GROUP_COLS), fwd5),
        pl.BlockSpec((1, 1, SSD_GROUPS, SSD_STATE, GROUP_COLS), lambda s: (cur(s) // NT, 0, 0, 0, 0)),
        _const_spec((LANES, 2 * SSD_WIDTH)),
        _const_spec((1, SSD_WIDTH)), _const_spec((1, SSD_WIDTH)),
        _const_spec((D_MODEL, D_MODEL)),
        _const_spec((1, D_MODEL)), _const_spec((1, D_MODEL)),
        _const_spec((D_MODEL, D_FF)), _const_spec((D_FF, D_MODEL)),
        _const_spec((1, D_MODEL)), _const_spec((1, D_MODEL)),
    ]
    return pl.pallas_call(
        _out_kernel,
        grid=(N_TILES + 2,),
        in_specs=in_specs,
        out_specs=pl.BlockSpec((1, TM, D_MODEL), done),
        out_shape=jax.ShapeDtypeStruct((BATCH, SEQ, D_MODEL), F32),
        scratch_shapes=[pltpu.VMEM((SSD_GROUPS, SSD_STATE, GROUP_COLS), F32),
                        pltpu.VMEM((TM, SSD_WIDTH), BF16),
                        pltpu.VMEM((2, TM, D_MODEL), F32),
                        pltpu.VMEM((TM, D_MODEL), BF16),
                        pltpu.VMEM((2, 1, D_MODEL), F32),
                        pltpu.VMEM((TM, D_MODEL), F32),
                        pltpu.VMEM((TM, D_FF), BF16)],
        compiler_params=pltpu.CompilerParams(dimension_semantics=("arbitrary",),
                                             vmem_limit_bytes=VMEM_LIMIT),
        name="out",
    )(x, mod_lat, lng, lnb, ycv, zg, xs, bt, cm, rows, cols, colf, sb, h0, exw, dx, nw,
      wout, ln1g, ln1b, wff1, wff2, ln2g, ln2b)


def kernel(x, c, ctx, c_ctx, ln_in_g, ln_in_b, w_mod, b_mod, w_in, conv_w, ssd_conv_w, ssd_conv_b,
           dt_bias, a_log, ssd_d, ssd_norm_w, w_out, ln1_g, ln1_b, w_ff1, w_ff2, ln2_g, ln2_b):
    row = lambda v: v.reshape(1, -1).astype(F32)
    cvec = jnp.concatenate([c, c_ctx[None, :], jnp.zeros((16 - BATCH - 1, D_MODEL), F32)], axis=0)
    mod = _mod_call(cvec, w_mod[0], row(b_mod[0]))
    mod_lat = mod[:BATCH].reshape(BATCH, 6, D_MODEL)
    mod_ctx = mod[BATCH:BATCH + 1].reshape(1, 6, D_MODEL)

    w_in_p, w_out_p, w_ff1_p, w_ff2_p = _wprep_call(w_in[0].T, w_out, w_ff1, w_ff2)
    lng, lnb = row(ln_in_g), row(ln_in_b)
    scw, scb = ssd_conv_w[0], row(ssd_conv_b[0])
    dtb = dt_bias[0].reshape(N_DH, 1)
    alog = a_log[0].reshape(N_DH, 1)
    exe = jnp.asarray(_EXE, BF16)
    exw = jnp.asarray(_EXW, BF16)

    h0 = _ctx_call(ctx, mod_ctx, lng, lnb, w_in_p, scw, scb, dtb, alog, exw, jnp.asarray(_U_CTX, BF16))
    ycv, zg, xs, bt, cm, rows, cols, colf, sb = _proj_call(
        x, mod_lat, lng, lnb, w_in_p, conv_w[0], scw, scb, dtb, alog, exe, exw,
        jnp.asarray(_U_CHUNK, BF16), h0)
    dx = jnp.repeat(ssd_d[0], SSD_HEADDIM).reshape(1, SSD_WIDTH)
    return _out_call(x, mod_lat, lng, lnb, ycv, zg, xs, bt, cm, rows, cols, colf, sb, h0, exw,
                     dx, row(ssd_norm_w[0]), w_out_p, row(ln1_g[0]), row(ln1_b[0]),
                     w_ff1_p, w_ff2_p, row(ln2_g[0]), row(ln2_b[0]))
```

```python
import jax
import jax.numpy as jnp
import numpy as np
from jax import lax
from jax.experimental import pallas as pl
from jax.experimental.pallas import tpu as pltpu

F32 = jnp.float32
BF16 = jnp.bfloat16

D_MODEL = 1024
BATCH = 8
SEQ = 2048
CTX_LEN = 256
GRID_W = 64
CONV_WIDTH = 512
SSD_WIDTH = 512
SSD_HEADDIM = 64
SSD_HEADS = 8
SSD_GROUPS = 2
SSD_STATE = 128
SSD_CHUNK = 128
N_DIRS = 2
D_FF = 4 * D_MODEL
LN_EPS = 1e-5
RMS_EPS = 1e-5
SSD_GN = SSD_GROUPS * SSD_STATE
XBC_DIM = SSD_WIDTH + 2 * SSD_GN
Z_OFF = 3 * CONV_WIDTH
XBC_OFF = Z_OFF + SSD_WIDTH
DT_OFF = XBC_OFF + XBC_DIM
N_DH = N_DIRS * SSD_HEADS
LANES = 128
P_DT = 0
P_XBC = LANES
P_Z = P_XBC + XBC_DIM
P_CONV = P_Z + SSD_WIDTH
IN_PAD = P_CONV + 3 * CONV_WIDTH
CTX_PAD = P_XBC + SSD_WIDTH + SSD_GN
GROUP_COLS = (SSD_HEADS // SSD_GROUPS) * SSD_HEADDIM
ALPHA = 2.0 ** 0.25

TM = 512
NCH = TM // SSD_CHUNK
NT = SEQ // TM
NCHUNK = SEQ // SSD_CHUNK
N_TILES = BATCH * NT
FF_BLK = 1024
FF_SUB = 256
VMEM_LIMIT = 58 * 1024 * 1024

COL_E1 = 48
COL_W = 80
COL_CSF = 112


def _expansion(col0, pieces, width):
    m = np.zeros((LANES, N_DH * width), np.float32)
    for t in range(pieces):
        for j in range(N_DH):
            m[col0 + 16 * t + j, j * width:(j + 1) * width] = 1.0
    return m


_EXE = _expansion(COL_E1, 2, SSD_HEADDIM)
_EXW = _expansion(COL_W, 2, SSD_HEADDIM)


def _ln_hat(x):
    mu = jnp.mean(x, axis=-1, keepdims=True)
    xc = x - mu
    var = jnp.mean(xc * xc, axis=-1, keepdims=True)
    return xc * lax.rsqrt(var + LN_EPS)


def _silu(x):
    return x / (1.0 + jnp.exp(-x))


def _softplus(x):
    return jnp.maximum(x, 0.0) + jnp.log1p(jnp.exp(-jnp.abs(x)))


def _edge_masks(rows, period):
    pos = lax.broadcasted_iota(jnp.int32, (rows, LANES), 0) % period
    return (pos != 0).astype(F32), (pos != period - 1).astype(F32)


def _conv3(t, w, mprev, mnext):
    rows = t.shape[0]
    prev = pltpu.roll(t, 1, 0) * mprev
    nxt = pltpu.roll(t, rows - 1, 0) * mnext
    return prev * w[0:1, :] + t * w[1:2, :] + nxt * w[2:3, :]


def _split(v, pieces):
    out = []
    for _ in range(pieces - 1):
        p = v.astype(BF16).astype(F32)
        out.append(p)
        v = v - p
    out.append(v.astype(BF16).astype(F32))
    return out


def _tri(length, op):
    i = np.arange(length)
    return op(i[:, None], i[None, :]).astype(np.float32)


_U_CHUNK = np.concatenate([_tri(SSD_CHUNK, np.less_equal), _tri(SSD_CHUNK, np.greater_equal),
                           np.ones((SSD_CHUNK, SSD_CHUNK), np.float32)], axis=1)
_U_CTX = np.concatenate([_tri(CTX_LEN, np.greater), _tri(CTX_LEN, np.less)], axis=1)


def _scan_mm(v, u_ref):
    pieces = jnp.concatenate(_split(v, 3), axis=0).astype(BF16)
    o = jnp.dot(pieces, u_ref[...], preferred_element_type=F32)
    return o[0:N_DH] + o[N_DH:2 * N_DH] + o[2 * N_DH:3 * N_DH]


def _dt_rows(raw, dtb, a_log):
    r = raw.T[0:N_DH, :] + dtb
    dt = _softplus(r)
    return dt, dt * (-jnp.exp(a_log))


def _fwd_rows():
    return lax.broadcasted_iota(jnp.int32, (N_DH, 1), 0) < SSD_HEADS


def _const_spec(shape):
    nd = len(shape)
    return pl.BlockSpec(shape, lambda *_: (0,) * nd, pipeline_mode=pl.Buffered(1))


def _mod_kernel(c_ref, w_ref, b_ref, o_ref):
    s = _silu(c_ref[...]).astype(BF16)
    o_ref[...] = jnp.dot(s, w_ref[...].astype(BF16), preferred_element_type=F32) + b_ref[...]


def _mod_call(cvec, w_mod, b_mod):
    tn = 1536
    return pl.pallas_call(
        _mod_kernel,
        grid=(6 * D_MODEL // tn,),
        in_specs=[pl.BlockSpec((16, D_MODEL), lambda j: (0, 0)),
                  pl.BlockSpec((D_MODEL, tn), lambda j: (0, j)),
                  pl.BlockSpec((1, tn), lambda j: (0, j))],
        out_specs=pl.BlockSpec((16, tn), lambda j: (0, j)),
        out_shape=jax.ShapeDtypeStruct((16, 6 * D_MODEL), F32),
        compiler_params=pltpu.CompilerParams(dimension_semantics=("arbitrary",),
                                             vmem_limit_bytes=VMEM_LIMIT),
        name="mod",
    )(cvec, w_mod, b_mod)


def _wprep_kernel(wint_ref, wout_ref, wff1_ref, wff2_ref, pin_ref, pout_ref, pff1_ref, pff2_ref):
    dt_rows = jnp.concatenate([wint_ref[DT_OFF:DT_OFF + N_DH, :],
                               jnp.zeros((LANES - N_DH, wint_ref.shape[1]), F32)], axis=0)
    pin_ref[:, P_DT:P_DT + LANES] = dt_rows.T.astype(BF16)
    def move(dst, src):
        pin_ref[:, dst:dst + LANES] = wint_ref[src:src + LANES, :].T.astype(BF16)

    for dst, src, width in ((P_XBC, XBC_OFF, XBC_DIM), (P_Z, Z_OFF, SSD_WIDTH)):
        for j in range(0, width, LANES):
            move(dst + j, src + j)
    for j in range(CONV_WIDTH // LANES):
        for k in range(3):
            move(P_CONV + (3 * j + k) * LANES, k * CONV_WIDTH + j * LANES)
    pout_ref[...] = wout_ref[0].astype(BF16)
    pff1_ref[...] = wff1_ref[0].astype(BF16)
    pff2_ref[...] = wff2_ref[0].astype(BF16)


def _wprep_call(w_in_t, w_out, w_ff1, w_ff2):
    steps = 8
    r1, r4 = D_MODEL // steps, D_FF // steps
    return pl.pallas_call(
        _wprep_kernel,
        grid=(steps,),
        in_specs=[pl.BlockSpec((w_in_t.shape[0], r1), lambda i: (0, i)),
                  pl.BlockSpec((1, r1, D_MODEL), lambda i: (0, i, 0)),
                  pl.BlockSpec((1, r1, D_FF), lambda i: (0, i, 0)),
                  pl.BlockSpec((1, r4, D_MODEL), lambda i: (0, i, 0))],
        out_specs=[pl.BlockSpec((r1, IN_PAD), lambda i: (i, 0)),
                   pl.BlockSpec((r1, D_MODEL), lambda i: (i, 0)),
                   pl.BlockSpec((r1, D_FF), lambda i: (i, 0)),
                   pl.BlockSpec((r4, D_MODEL), lambda i: (i, 0))],
        out_shape=[jax.ShapeDtypeStruct((D_MODEL, IN_PAD), BF16),
                   jax.ShapeDtypeStruct((D_MODEL, D_MODEL), BF16),
                   jax.ShapeDtypeStruct((D_MODEL, D_FF), BF16),
                   jax.ShapeDtypeStruct((D_FF, D_MODEL), BF16)],
        compiler_params=pltpu.CompilerParams(dimension_semantics=("arbitrary",),
                                             vmem_limit_bytes=VMEM_LIMIT),
        name="wprep",
    )(w_in_t, w_out, w_ff1, w_ff2)


def _ctx_kernel(x_ref, mod_ref, lng_ref, lnb_ref, w_ref, scw_ref, scb_ref, dtb_ref, alog_ref,
                exw_ref, u_ref, h0_ref):
    m = mod_ref[0]
    sc = 1.0 + m[1:2]
    u = _ln_hat(x_ref[0]) * (lng_ref[...] * sc) + (lnb_ref[...] * sc + m[0:1])
    proj = jnp.dot(u.astype(BF16), w_ref[...], preferred_element_type=F32)
    mprev, mnext = _edge_masks(CTX_LEN, CTX_LEN)
    slabs = []
    for j in range((SSD_WIDTH + SSD_GN) // LANES):
        sl = slice(j * LANES, (j + 1) * LANES)
        pj = proj[:, P_XBC + j * LANES:P_XBC + (j + 1) * LANES]
        slabs.append(_silu(_conv3(pj, scw_ref[:, sl], mprev, mnext) + scb_ref[:, sl]))
    xs = jnp.concatenate(slabs[:4], axis=1)
    dt, adt = _dt_rows(proj[:, P_DT:P_DT + LANES], dtb_ref[...], alog_ref[...])
    sc2 = _scan_mm(adt, u_ref)
    excl = jnp.where(_fwd_rows(), sc2[:, :CTX_LEN], sc2[:, CTX_LEN:])
    w = jnp.exp(excl) * dt
    zero = jnp.zeros((N_DH, CTX_LEN), F32)
    table = jnp.concatenate([zero] * (COL_W // 16) + _split(w, 2) + [zero], axis=0)
    cols = table.T.astype(BF16)
    wx = jnp.dot(cols, exw_ref[...], preferred_element_type=F32)
    for d in range(N_DIRS):
        xw = (xs * wx[:, d * SSD_WIDTH:(d + 1) * SSD_WIDTH]).astype(BF16)
        for g in range(SSD_GROUPS):
            bt = slabs[4 + g].T.astype(BF16)
            h0_ref[0, d, g] = jnp.dot(bt, xw[:, g * GROUP_COLS:(g + 1) * GROUP_COLS],
                                      preferred_element_type=F32)


def _ctx_call(ctx, mod_ctx, lng, lnb, w_in, scw, scb, dtb, alog, exw, u_ctx):
    return pl.pallas_call(
        _ctx_kernel,
        grid=(BATCH,),
        in_specs=[pl.BlockSpec((1, CTX_LEN, D_MODEL), lambda b: (b, 0, 0)),
                  _const_spec((1, 6, D_MODEL)), _const_spec((1, D_MODEL)), _const_spec((1, D_MODEL)),
                  _const_spec((D_MODEL, CTX_PAD)), _const_spec((3, XBC_DIM)), _const_spec((1, XBC_DIM)),
                  _const_spec((N_DH, 1)), _const_spec((N_DH, 1)), _const_spec((LANES, 2 * SSD_WIDTH)),
                  _const_spec((CTX_LEN, 2 * CTX_LEN))],
        out_specs=pl.BlockSpec((1, N_DIRS, SSD_GROUPS, SSD_STATE, GROUP_COLS),
                               lambda b: (b, 0, 0, 0, 0)),
        out_shape=jax.ShapeDtypeStruct((BATCH, N_DIRS, SSD_GROUPS, SSD_STATE, GROUP_COLS), F32),
        compiler_params=pltpu.CompilerParams(dimension_semantics=("arbitrary",),
                                             vmem_limit_bytes=VMEM_LIMIT),
        name="ctx",
    )(ctx, mod_ctx, lng, lnb, w_in, scw, scb, dtb, alog, exw, u_ctx)


def _proj_kernel(x_ref, mod_ref, lng_ref, lnb_ref, w_ref, cw_ref, scw_ref, scb_ref, dtb_ref,
                 alog_ref, exe_ref, exw_ref, u_ref, h0_ref,
                 hln_ref, ycv_ref, zg_ref, xs_ref, bt_ref, cm_ref, rows_ref, cols_ref, colf_ref, sb_ref,
                 st_ref):
    @pl.when(pl.program_id(1) == 0)
    def _():
        st_ref[...] = h0_ref[0, 0]

    m = mod_ref[0]
    sc = 1.0 + m[1:2]
    scale, shift = lng_ref[...] * sc, lnb_ref[...] * sc + m[0:1]
    ubs, pas = [], []
    for r in range(2):
        rows = slice(r * (TM // 2), (r + 1) * (TM // 2))
        xhat = _ln_hat(x_ref[0, rows, :])
        hln_ref[0, rows, :] = xhat * lng_ref[...] + lnb_ref[...]
        ubs.append((xhat * scale + shift).astype(BF16))
        pas.append(jnp.dot(ubs[r], w_ref[:, :P_Z], preferred_element_type=F32))
    ub = jnp.concatenate(ubs, axis=0)
    pa = jnp.concatenate(pas, axis=0)
    mprev, mnext = _edge_masks(TM, GRID_W)
    is_fwd = _fwd_rows()
    zero = jnp.zeros((N_DH, SSD_CHUNK), F32)
    chunks = [slice(c * SSD_CHUNK, (c + 1) * SSD_CHUNK) for c in range(NCH)]

    def conv_slabs(p, first):
        for i in range(2):
            gb, gc, gh = (p[:, (3 * i + k) * LANES:(3 * i + k + 1) * LANES] for k in range(3))
            sl = slice((first + i) * LANES, (first + i + 1) * LANES)
            ycv_ref[0, :, sl] = (gb * _conv3(gc * gh, cw_ref[:, sl], mprev, mnext)).astype(BF16)

    dts = [_dt_rows(pa[tok, P_DT:P_DT + LANES], dtb_ref[...], alog_ref[...]) for tok in chunks]
    pz = jnp.dot(ub, w_ref[:, P_Z:P_CONV], preferred_element_type=F32)

    colss = []
    for c, (dt, adt) in enumerate(dts):
        sc3 = _scan_mm(adt, u_ref)
        cs = jnp.where(is_fwd, sc3[:, :SSD_CHUNK], sc3[:, SSD_CHUNK:2 * SSD_CHUNK])
        tot = sc3[:, 2 * SSD_CHUNK:]
        e1 = jnp.exp(cs)
        w = jnp.exp(tot - cs) * dt
        rows_ref[0, c] = jnp.concatenate([cs, dt], axis=0)
        table = jnp.concatenate([zero] * 3 + _split(e1, 2) + _split(w, 2) + [cs], axis=0).T
        colf_ref[0, c] = table
        colss.append(table.astype(BF16))
        cols_ref[0, c] = colss[c]

    pc1 = jnp.dot(ub, w_ref[:, P_CONV:P_CONV + 6 * LANES], preferred_element_type=F32)

    slabs = []
    for j in range(XBC_DIM // LANES):
        sl = slice(j * LANES, (j + 1) * LANES)
        pj = pa[:, P_XBC + j * LANES:P_XBC + (j + 1) * LANES]
        slabs.append(_silu(_conv3(pj, scw_ref[:, sl], mprev, mnext) + scb_ref[:, sl]))
    xs = jnp.concatenate(slabs[:4], axis=1)
    xs_ref[0] = xs.astype(BF16)
    cm_ref[0] = jnp.concatenate(slabs[6:8], axis=1).astype(BF16)
    zg_ref[0] = _silu(pz).astype(BF16)
    bts = []
    for c, tok in enumerate(chunks):
        bt = [slabs[4 + g][tok].T.astype(BF16) for g in range(SSD_GROUPS)]
        for g in range(SSD_GROUPS):
            bt_ref[0, c, g] = bt[g]
        bts.append(bt)
    wxbs = [jnp.dot(cols, exw_ref[:, SSD_WIDTH:], preferred_element_type=F32) for cols in colss]
    decs = [jnp.dot(cols[0:16], exe_ref[:, SSD_WIDTH:], preferred_element_type=F32)[0:1]
            for cols in colss]

    pc2 = jnp.dot(ub, w_ref[:, P_CONV + 6 * LANES:], preferred_element_type=F32)
    conv_slabs(pc1, 0)

    state = [st_ref[g] for g in range(SSD_GROUPS)]
    for c in reversed(range(NCH)):
        xw = (xs[chunks[c]] * wxbs[c]).astype(BF16)
        for g in range(SSD_GROUPS):
            gs = slice(g * GROUP_COLS, (g + 1) * GROUP_COLS)
            sb_ref[0, c, g] = state[g].astype(BF16)
            local = jnp.dot(bts[c][g], xw[:, gs], preferred_element_type=F32)
            state[g] = state[g] * decs[c][:, gs] + local
    for g in range(SSD_GROUPS):
        st_ref[g] = state[g]
    conv_slabs(pc2, 2)


def _proj_call(x, mod_lat, lng, lnb, w_in, cw, scw, scb, dtb, alog, exe, exw, u_chunk, h0):
    rev = lambda b, t: (b, NT - 1 - t, 0)
    rev4 = lambda b, t: (b, NT - 1 - t, 0, 0)
    rev5 = lambda b, t: (b, NT - 1 - t, 0, 0, 0)
    tok = lambda width: pl.BlockSpec((1, TM, width), rev)
    out_shape = [
        jax.ShapeDtypeStruct((BATCH, SEQ, D_MODEL), F32),
        jax.ShapeDtypeStruct((BATCH, SEQ, CONV_WIDTH), BF16),
        jax.ShapeDtypeStruct((BATCH, SEQ, SSD_WIDTH), BF16),
        jax.ShapeDtypeStruct((BATCH, SEQ, SSD_WIDTH), BF16),
        jax.ShapeDtypeStruct((BATCH, NCHUNK, SSD_GROUPS, SSD_STATE, SSD_CHUNK), BF16),
        jax.ShapeDtypeStruct((BATCH, SEQ, SSD_GN), BF16),
        jax.ShapeDtypeStruct((BATCH, NCHUNK, 2 * N_DH, SSD_CHUNK), F32),
        jax.ShapeDtypeStruct((BATCH, NCHUNK, SSD_CHUNK, LANES), BF16),
        jax.ShapeDtypeStruct((BATCH, NCHUNK, SSD_CHUNK, LANES), F32),
        jax.ShapeDtypeStruct((BATCH, NCHUNK, SSD_GROUPS, SSD_STATE, GROUP_COLS), BF16),
    ]
    out_specs = [
        tok(D_MODEL), tok(CONV_WIDTH), tok(SSD_WIDTH), tok(SSD_WIDTH),
        pl.BlockSpec((1, NCH, SSD_GROUPS, SSD_STATE, SSD_CHUNK), rev5),
        tok(SSD_GN),
        pl.BlockSpec((1, NCH, 2 * N_DH, SSD_CHUNK), rev4),
        pl.BlockSpec((1, NCH, SSD_CHUNK, LANES), rev4),
        pl.BlockSpec((1, NCH, SSD_CHUNK, LANES), rev4),
        pl.BlockSpec((1, NCH, SSD_GROUPS, SSD_STATE, GROUP_COLS), rev5),
    ]
    in_specs = [
        pl.BlockSpec((1, TM, D_MODEL), rev),
        pl.BlockSpec((1, 6, D_MODEL), lambda b, t: (b, 0, 0)),
        _const_spec((1, D_MODEL)), _const_spec((1, D_MODEL)),
        _const_spec((D_MODEL, IN_PAD)),
        _const_spec((3, CONV_WIDTH)), _const_spec((3, XBC_DIM)), _const_spec((1, XBC_DIM)),
        _const_spec((N_DH, 1)), _const_spec((N_DH, 1)),
        _const_spec((LANES, 2 * SSD_WIDTH)), _const_spec((LANES, 2 * SSD_WIDTH)),
        _const_spec((SSD_CHUNK, 3 * SSD_CHUNK)),
        pl.BlockSpec((1, 1, SSD_GROUPS, SSD_STATE, GROUP_COLS), lambda b, t: (b, 1, 0, 0, 0)),
    ]
    return pl.pallas_call(
        _proj_kernel,
        grid=(BATCH, NT),
        in_specs=in_specs,
        out_specs=out_specs,
        out_shape=out_shape,
        scratch_shapes=[pltpu.VMEM((SSD_GROUPS, SSD_STATE, GROUP_COLS), F32)],
        compiler_params=pltpu.CompilerParams(dimension_semantics=("arbitrary", "arbitrary"),
                                             vmem_limit_bytes=VMEM_LIMIT),
        name="proj",
    )(x, mod_lat, lng, lnb, w_in, cw, scw, scb, dtb, alog, exe, exw, u_chunk, h0)


def _out_kernel(hln_ref, mod_ref, ycv_ref, zg_ref, xs_ref, bt_ref, cm_ref, rows_ref,
                cols_ref, colf_ref, sb_ref, h0_ref, exw_ref, dx_ref, nw_ref, wout_ref,
                ln1g_ref, ln1b_ref, wff1_ref, wff2_ref, ln2g_ref, ln2b_ref,
                out_ref, st_ref, yn_ref, h1_ref, u2_ref, g2_ref, r2_ref, hid_ref):
    s = pl.program_id(0)
    wr = s % 2
    rd = (s + 1) % 2
    li = lax.broadcasted_iota(jnp.int32, (SSD_CHUNK, SSD_CHUNK), 0)
    si = lax.broadcasted_iota(jnp.int32, (SSD_CHUNK, SSD_CHUNK), 1)
    low = li >= si
    diag = li == si
    lo_half = si < SSD_HEADDIM

    def ssd_chunk(c, state, fill):
        tok = slice(c * SSD_CHUNK, (c + 1) * SSD_CHUNK)
        rows = rows_ref[0, c]
        cols = cols_ref[0, c]
        xs = xs_ref[0, tok, :]
        cm = cm_ref[0, tok, :]
        colf = colf_ref[0, c]
        bc = [jnp.broadcast_to(colf[:, COL_CSF + j:COL_CSF + j + 1], (SSD_CHUNK, LANES))
              for j in range(N_DH)]
        e1x = [jnp.concatenate(
            [jnp.exp(jnp.where(lo_half, bc[d * SSD_HEADS + 2 * k], bc[d * SSD_HEADS + 2 * k + 1]))
             for k in range(SSD_HEADS // 2)], axis=1) for d in range(N_DIRS)]
        wxf = jnp.dot(cols, exw_ref[:, :SSD_WIDTH], preferred_element_type=F32)
        dec = e1x[0][SSD_CHUNK - 1:SSD_CHUNK, :]
        gmat = [jnp.dot(cm[:, g * SSD_STATE:(g + 1) * SSD_STATE], bt_ref[0, c, g],
                        preferred_element_type=F32) for g in range(SSD_GROUPS)]

        ys = []
        for k in range(SSD_HEADS // 2):
            ms = []
            for h in (2 * k, 2 * k + 1):
                g = h // (SSD_HEADS // SSD_GROUPS)
                hb = SSD_HEADS + h
                arg = jnp.where(low,
                                bc[h] - rows[h:h + 1, :],
                                bc[hb] - rows[hb:hb + 1, :])
                dtf = rows[N_DH + h:N_DH + h + 1, :]
                dtb = rows[N_DH + hb:N_DH + hb + 1, :]
                fac = jnp.where(low, dtf, dtb) + jnp.where(diag, dtb, 0.0)
                ms.append((gmat[g] * jnp.exp(arg) * fac).astype(BF16))
            xp = xs[:, k * LANES:(k + 1) * LANES]
            rhs = jnp.concatenate([jnp.where(lo_half, xp, jnp.zeros_like(xp)),
                                   jnp.where(lo_half, jnp.zeros_like(xp), xp)], axis=0)
            ys.append(jnp.dot(jnp.concatenate(ms, axis=1), rhs, preferred_element_type=F32))
            fill[k]()
        y = jnp.concatenate(ys, axis=1)

        yf, yb = [], []
        for g in range(SSD_GROUPS):
            cg = cm[:, g * SSD_STATE:(g + 1) * SSD_STATE]
            yf.append(jnp.dot(cg, state[g].astype(BF16), preferred_element_type=F32))
            yb.append(jnp.dot(cg, sb_ref[0, c, g], preferred_element_type=F32))
        y = (y + jnp.concatenate(yf, axis=1) * e1x[0]
             + jnp.concatenate(yb, axis=1) * e1x[1]
             + xs.astype(F32) * dx_ref[...])

        yg = y * zg_ref[0, tok, :].astype(F32)
        ms_ = jnp.mean(yg * yg, axis=-1, keepdims=True)
        yn_ref[tok, :] = (yg * lax.rsqrt(ms_ + RMS_EPS) * nw_ref[...]).astype(BF16)

        xw = (xs.astype(F32) * wxf).astype(BF16)
        new_state = []
        for g in range(SSD_GROUPS):
            gs = slice(g * GROUP_COLS, (g + 1) * GROUP_COLS)
            local = jnp.dot(bt_ref[0, c, g], xw[:, gs], preferred_element_type=F32)
            new_state.append(state[g] * dec[:, gs] + local)
        return new_state

    def mlp_up(j, k):
        blk = slice(j * FF_BLK + k * FF_SUB, j * FF_BLK + (k + 1) * FF_SUB)
        hid = jnp.maximum(jnp.dot(u2_ref[...], wff1_ref[:, blk], preferred_element_type=F32), 0.0)
        hid_ref[:, blk] = (hid * hid).astype(BF16)

    def mlp_down(j, acc):
        blk = slice(j * FF_BLK, (j + 1) * FF_BLK)
        part = jnp.dot(hid_ref[:, blk], wff2_ref[blk, :], preferred_element_type=F32)
        return part if acc is None else acc + part

    def ln2_rows(i):
        r = slice(i * SSD_CHUNK, (i + 1) * SSD_CHUNK)
        out_ref[0, r, :] = _ln_hat(r2_ref[r, :]) * ln2g_ref[...] + ln2b_ref[...]

    def step(mixer, mlp, norm):
        nothing = lambda: None
        state, acc = None, None
        if mixer:
            @pl.when(jnp.minimum(s, N_TILES - 1) % NT == 0)
            def _():
                st_ref[...] = h0_ref[0, 0]
            state = [st_ref[g] for g in range(SSD_GROUPS)]
        for c in range(NCH):
            ups = [(lambda k=k: mlp_up(c, k)) if mlp else nothing for k in range(FF_BLK // FF_SUB)]
            if mixer:
                state = ssd_chunk(c, state, ups)
            else:
                for up in ups:
                    up()
            if norm:
                ln2_rows(c)
        if mixer:
            for g in range(SSD_GROUPS):
                st_ref[g] = state[g]
            mix = (jnp.dot(ycv_ref[0], wout_ref[:CONV_WIDTH, :], preferred_element_type=F32)
                   + jnp.dot(yn_ref[...], wout_ref[CONV_WIDTH:, :], preferred_element_type=F32))
            m = mod_ref[0]
            g2_ref[wr] = m[5:6]
        for j in range(NCH):
            if mlp:
                acc = mlp_down(j, acc)
            if mixer:
                r = slice(j * SSD_CHUNK, (j + 1) * SSD_CHUNK)
                h1 = _ln_hat(ALPHA * hln_ref[0, r, :] + m[2:3] * mix[r]) * ln1g_ref[...] + ln1b_ref[...]
                h1_ref[wr, r, :] = h1
                u2_ref[r, :] = (h1 * (1.0 + m[4:5]) + m[3:4]).astype(BF16)
        if mlp:
            r2_ref[...] = ALPHA * h1_ref[rd] + g2_ref[rd] * acc

    @pl.when(s == 0)
    def _():
        r2_ref[...] = jnp.zeros((TM, D_MODEL), F32)
        step(True, False, False)

    @pl.when(jnp.logical_and(s >= 1, s <= N_TILES))
    def _():
        step(True, True, True)

    @pl.when(s == N_TILES + 1)
    def _():
        step(False, False, True)


def _out_call(hln, mod_lat, ycv, zg, xs, bt, cm, rows, cols, colf, sb, h0, exw, dx, nw,
              wout, ln1g, ln1b, wff1, wff2, ln2g, ln2b):
    cur = lambda s: jnp.minimum(s, N_TILES - 1)
    fwd = lambda s: (cur(s) // NT, cur(s) % NT, 0)
    fwd4 = lambda s: (cur(s) // NT, cur(s) % NT, 0, 0)
    fwd5 = lambda s: (cur(s) // NT, cur(s) % NT, 0, 0, 0)
    done = lambda s: (jnp.maximum(s - 2, 0) // NT, jnp.maximum(s - 2, 0) % NT, 0)
    tok = lambda width: pl.BlockSpec((1, TM, width), fwd)
    in_specs = [
        tok(D_MODEL),
        pl.BlockSpec((1, 6, D_MODEL), lambda s: (cur(s) // NT, 0, 0)),
        tok(CONV_WIDTH), tok(SSD_WIDTH), tok(SSD_WIDTH),
        pl.BlockSpec((1, NCH, SSD_GROUPS, SSD_STATE, SSD_CHUNK), fwd5),
        tok(SSD_GN),
        pl.BlockSpec((1, NCH, 2 * N_DH, SSD_CHUNK), fwd4),
        pl.BlockSpec((1, NCH, SSD_CHUNK, LANES), fwd4),
        pl.BlockSpec((1, NCH, SSD_CHUNK, LANES), fwd4),
        pl.BlockSpec((1, NCH, SSD_GROUPS, SSD_STATE, GROUP_COLS), fwd5),
        pl.BlockSpec((1, 1, SSD_GROUPS, SSD_STATE, GROUP_COLS), lambda s: (cur(s) // NT, 0, 0, 0, 0)),
        _const_spec((LANES, 2 * SSD_WIDTH)),
        _const_spec((1, SSD_WIDTH)), _const_spec((1, SSD_WIDTH)),
        _const_spec((D_MODEL, D_MODEL)),
        _const_spec((1, D_MODEL)), _const_spec((1, D_MODEL)),
        _const_spec((D_MODEL, D_FF)), _const_spec((D_FF, D_MODEL)),
        _const_spec((1, D_MODEL)), _const_spec((1, D_MODEL)),
    ]
    return pl.pallas_call(
        _out_kernel,
        grid=(N_TILES + 2,),
        in_specs=in_specs,
        out_specs=pl.BlockSpec((1, TM, D_MODEL), done),
        out_shape=jax.ShapeDtypeStruct((BATCH, SEQ, D_MODEL), F32),
        scratch_shapes=[pltpu.VMEM((SSD_GROUPS, SSD_STATE, GROUP_COLS), F32),
                        pltpu.VMEM((TM, SSD_WIDTH), BF16),
                        pltpu.VMEM((2, TM, D_MODEL), F32),
                        pltpu.VMEM((TM, D_MODEL), BF16),
                        pltpu.VMEM((2, 1, D_MODEL), F32),
                        pltpu.VMEM((TM, D_MODEL), F32),
                        pltpu.VMEM((TM, D_FF), BF16)],
        compiler_params=pltpu.CompilerParams(dimension_semantics=("arbitrary",),
                                             vmem_limit_bytes=VMEM_LIMIT),
        name="out",
    )(hln, mod_lat, ycv, zg, xs, bt, cm, rows, cols, colf, sb, h0, exw, dx, nw,
      wout, ln1g, ln1b, wff1, wff2, ln2g, ln2b)


def kernel(x, c, ctx, c_ctx, ln_in_g, ln_in_b, w_mod, b_mod, w_in, conv_w, ssd_conv_w, ssd_conv_b,
           dt_bias, a_log, ssd_d, ssd_norm_w, w_out, ln1_g, ln1_b, w_ff1, w_ff2, ln2_g, ln2_b):
    row = lambda v: v.reshape(1, -1).astype(F32)
    cvec = jnp.concatenate([c, c_ctx[None, :], jnp.zeros((16 - BATCH - 1, D_MODEL), F32)], axis=0)
    mod = _mod_call(cvec, w_mod[0], row(b_mod[0]))
    mod_lat = mod[:BATCH].reshape(BATCH, 6, D_MODEL)
    mod_ctx = mod[BATCH:BATCH + 1].reshape(1, 6, D_MODEL)

    w_in_p, w_out_p, w_ff1_p, w_ff2_p = _wprep_call(w_in[0].T, w_out, w_ff1, w_ff2)
    lng, lnb = row(ln_in_g), row(ln_in_b)
    scw, scb = ssd_conv_w[0], row(ssd_conv_b[0])
    dtb = dt_bias[0].reshape(N_DH, 1)
    alog = a_log[0].reshape(N_DH, 1)
    exe = jnp.asarray(_EXE, BF16)
    exw = jnp.asarray(_EXW, BF16)

    h0 = _ctx_call(ctx, mod_ctx, lng, lnb, w_in_p, scw, scb, dtb, alog, exw, jnp.asarray(_U_CTX, BF16))
    hln, ycv, zg, xs, bt, cm, rows, cols, colf, sb = _proj_call(
        x, mod_lat, lng, lnb, w_in_p, conv_w[0], scw, scb, dtb, alog, exe, exw,
        jnp.asarray(_U_CHUNK, BF16), h0)
    dx = jnp.repeat(ssd_d[0], SSD_HEADDIM).reshape(1, SSD_WIDTH)
    return _out_call(hln, mod_lat, ycv, zg, xs, bt, cm, rows, cols, colf, sb, h0, exw,
                     dx, row(ssd_norm_w[0]), w_out_p, row(ln1_g[0]), row(ln1_b[0]),
                     w_ff1_p, w_ff2_p, row(ln2_g[0]), row(ln2_b[0]))
```

```python
import jax
import jax.numpy as jnp
import numpy as np
from jax import lax
from jax.experimental import pallas as pl
from jax.experimental.pallas import tpu as pltpu

F32 = jnp.float32
BF16 = jnp.bfloat16

D_MODEL = 1024
BATCH = 8
SEQ = 2048
CTX_LEN = 256
GRID_W = 64
CONV_WIDTH = 512
SSD_WIDTH = 512
SSD_HEADDIM = 64
SSD_HEADS = 8
SSD_GROUPS = 2
SSD_STATE = 128
SSD_CHUNK = 128
N_DIRS = 2
D_FF = 4 * D_MODEL
LN_EPS = 1e-5
RMS_EPS = 1e-5
SSD_GN = SSD_GROUPS * SSD_STATE
XBC_DIM = SSD_WIDTH + 2 * SSD_GN
Z_OFF = 3 * CONV_WIDTH
XBC_OFF = Z_OFF + SSD_WIDTH
DT_OFF = XBC_OFF + XBC_DIM
N_DH = N_DIRS * SSD_HEADS
LANES = 128
P_DT = 0
P_XBC = LANES
P_Z = P_XBC + XBC_DIM
P_CONV = P_Z + SSD_WIDTH
IN_PAD = P_CONV + 3 * CONV_WIDTH
CTX_PAD = P_XBC + SSD_WIDTH + SSD_GN
GROUP_COLS = (SSD_HEADS // SSD_GROUPS) * SSD_HEADDIM
ALPHA = 2.0 ** 0.25

TM = 512
CTX_ROWS = 2
NCH = TM // SSD_CHUNK
NT = SEQ // TM
NCHUNK = SEQ // SSD_CHUNK
N_TILES = BATCH * NT
FF_BLK = 1024
FF_SUB = 256
VMEM_LIMIT = 58 * 1024 * 1024

COL_E1 = 48
COL_W = 80
COL_CSF = 112


def _expansion(col0, pieces, width):
    m = np.zeros((LANES, N_DH * width), np.float32)
    for t in range(pieces):
        for j in range(N_DH):
            m[col0 + 16 * t + j, j * width:(j + 1) * width] = 1.0
    return m


_EXE = _expansion(COL_E1, 2, SSD_HEADDIM)
_EXW = _expansion(COL_W, 2, SSD_HEADDIM)


def _ln_hat(x):
    mu = jnp.mean(x, axis=-1, keepdims=True)
    xc = x - mu
    var = jnp.mean(xc * xc, axis=-1, keepdims=True)
    return xc * lax.rsqrt(var + LN_EPS)


def _silu(x):
    return x / (1.0 + jnp.exp(-x))


def _softplus(x):
    return jnp.maximum(x, 0.0) + jnp.log1p(jnp.exp(-jnp.abs(x)))


def _edge_masks(rows, period):
    pos = lax.broadcasted_iota(jnp.int32, (rows, LANES), 0) % period
    return (pos != 0).astype(F32), (pos != period - 1).astype(F32)


def _conv3(t, w, mprev, mnext):
    rows = t.shape[0]
    prev = pltpu.roll(t, 1, 0) * mprev
    nxt = pltpu.roll(t, rows - 1, 0) * mnext
    return prev * w[0:1, :] + t * w[1:2, :] + nxt * w[2:3, :]


def _split(v, pieces):
    out = []
    for _ in range(pieces - 1):
        p = v.astype(BF16).astype(F32)
        out.append(p)
        v = v - p
    out.append(v.astype(BF16).astype(F32))
    return out


def _tri(length, op):
    i = np.arange(length)
    return op(i[:, None], i[None, :]).astype(np.float32)


_U_CHUNK = np.concatenate([_tri(SSD_CHUNK, np.less_equal), _tri(SSD_CHUNK, np.greater_equal),
                           np.ones((SSD_CHUNK, SSD_CHUNK), np.float32)], axis=1)
_U_CTX = np.concatenate([_tri(CTX_LEN, np.greater), _tri(CTX_LEN, np.less)], axis=1)


def _scan_mm(v, u_ref):
    pieces = jnp.concatenate(_split(v, 3), axis=0).astype(BF16)
    o = jnp.dot(pieces, u_ref[...], preferred_element_type=F32)
    return o[0:N_DH] + o[N_DH:2 * N_DH] + o[2 * N_DH:3 * N_DH]


def _dt_rows(raw, dtb, a_log):
    r = raw.T[0:N_DH, :] + dtb
    dt = _softplus(r)
    return dt, dt * (-jnp.exp(a_log))


def _fwd_rows():
    return lax.broadcasted_iota(jnp.int32, (N_DH, 1), 0) < SSD_HEADS


def _const_spec(shape):
    nd = len(shape)
    return pl.BlockSpec(shape, lambda *_: (0,) * nd, pipeline_mode=pl.Buffered(1))


def _prep_kernel(c_ref, wmod_ref, bmod_ref, wint_ref, wout_ref, wff1_ref, wff2_ref,
                 mod_ref, pin_ref, pout_ref, pff1_ref, pff2_ref):
    cs = _silu(c_ref[...]).astype(BF16)
    mod_ref[...] = jnp.dot(cs, wmod_ref[0].astype(BF16), preferred_element_type=F32) + bmod_ref[...]

    dt_rows = jnp.concatenate([wint_ref[DT_OFF:DT_OFF + N_DH, :],
                               jnp.zeros((LANES - N_DH, wint_ref.shape[1]), F32)], axis=0)
    pin_ref[:, P_DT:P_DT + LANES] = dt_rows.T.astype(BF16)
    def move(dst, src):
        pin_ref[:, dst:dst + LANES] = wint_ref[src:src + LANES, :].T.astype(BF16)

    for dst, src, width in ((P_XBC, XBC_OFF, XBC_DIM), (P_Z, Z_OFF, SSD_WIDTH)):
        for j in range(0, width, LANES):
            move(dst + j, src + j)
    for j in range(CONV_WIDTH // LANES):
        for k in range(3):
            move(P_CONV + (3 * j + k) * LANES, k * CONV_WIDTH + j * LANES)
    pout_ref[...] = wout_ref[0].astype(BF16)
    pff1_ref[...] = wff1_ref[0].astype(BF16)
    pff2_ref[...] = wff2_ref[0].astype(BF16)


def _prep_call(cvec, w_mod, b_mod, w_in_t, w_out, w_ff1, w_ff2):
    steps = 8
    r1, r4, rm = D_MODEL // steps, D_FF // steps, 6 * D_MODEL // steps
    return pl.pallas_call(
        _prep_kernel,
        grid=(steps,),
        in_specs=[pl.BlockSpec((16, D_MODEL), lambda i: (0, 0)),
                  pl.BlockSpec((1, D_MODEL, rm), lambda i: (0, 0, i)),
                  pl.BlockSpec((1, rm), lambda i: (0, i)),
                  pl.BlockSpec((w_in_t.shape[0], r1), lambda i: (0, i)),
                  pl.BlockSpec((1, r1, D_MODEL), lambda i: (0, i, 0)),
                  pl.BlockSpec((1, r1, D_FF), lambda i: (0, i, 0)),
                  pl.BlockSpec((1, r4, D_MODEL), lambda i: (0, i, 0))],
        out_specs=[pl.BlockSpec((16, rm), lambda i: (0, i)),
                   pl.BlockSpec((r1, IN_PAD), lambda i: (i, 0)),
                   pl.BlockSpec((r1, D_MODEL), lambda i: (i, 0)),
                   pl.BlockSpec((r1, D_FF), lambda i: (i, 0)),
                   pl.BlockSpec((r4, D_MODEL), lambda i: (i, 0))],
        out_shape=[jax.ShapeDtypeStruct((16, 6 * D_MODEL), F32),
                   jax.ShapeDtypeStruct((D_MODEL, IN_PAD), BF16),
                   jax.ShapeDtypeStruct((D_MODEL, D_MODEL), BF16),
                   jax.ShapeDtypeStruct((D_MODEL, D_FF), BF16),
                   jax.ShapeDtypeStruct((D_FF, D_MODEL), BF16)],
        compiler_params=pltpu.CompilerParams(dimension_semantics=("arbitrary",),
                                             vmem_limit_bytes=VMEM_LIMIT),
        name="prep",
    )(cvec, w_mod, b_mod, w_in_t, w_out, w_ff1, w_ff2)


def _ctx_kernel(x_ref, mod_ref, lng_ref, lnb_ref, w_ref, scw_ref, scb_ref, dtb_ref, alog_ref,
                exw_ref, u_ref, h0_ref):
    m = mod_ref[0]
    sc = 1.0 + m[1:2]
    x = x_ref[...].reshape(CTX_ROWS * CTX_LEN, D_MODEL)
    u = _ln_hat(x) * (lng_ref[...] * sc) + (lnb_ref[...] * sc + m[0:1])
    proj = jnp.dot(u.astype(BF16), w_ref[...], preferred_element_type=F32)
    mprev, mnext = _edge_masks(CTX_ROWS * CTX_LEN, CTX_LEN)
    slabs = []
    for j in range((SSD_WIDTH + SSD_GN) // LANES):
        sl = slice(j * LANES, (j + 1) * LANES)
        pj = proj[:, P_XBC + j * LANES:P_XBC + (j + 1) * LANES]
        slabs.append(_silu(_conv3(pj, scw_ref[:, sl], mprev, mnext) + scb_ref[:, sl]))
    zero = jnp.zeros((N_DH, CTX_LEN), F32)
    for i in range(CTX_ROWS):
        tok = slice(i * CTX_LEN, (i + 1) * CTX_LEN)
        xs = jnp.concatenate([sl_[tok] for sl_ in slabs[:4]], axis=1)
        dt, adt = _dt_rows(proj[tok, P_DT:P_DT + LANES], dtb_ref[...], alog_ref[...])
        sc2 = _scan_mm(adt, u_ref)
        excl = jnp.where(_fwd_rows(), sc2[:, :CTX_LEN], sc2[:, CTX_LEN:])
        w = jnp.exp(excl) * dt
        table = jnp.concatenate([zero] * (COL_W // 16) + _split(w, 2) + [zero], axis=0)
        cols = table.T.astype(BF16)
        wx = jnp.dot(cols, exw_ref[...], preferred_element_type=F32)
        for d in range(N_DIRS):
            xw = (xs * wx[:, d * SSD_WIDTH:(d + 1) * SSD_WIDTH]).astype(BF16)
            for g in range(SSD_GROUPS):
                bt = slabs[4 + g][tok].T.astype(BF16)
                h0_ref[i, d, g] = jnp.dot(bt, xw[:, g * GROUP_COLS:(g + 1) * GROUP_COLS],
                                          preferred_element_type=F32)


def _ctx_call(ctx, mod_ctx, lng, lnb, w_in, scw, scb, dtb, alog, exw, u_ctx):
    return pl.pallas_call(
        _ctx_kernel,
        grid=(BATCH // CTX_ROWS,),
        in_specs=[pl.BlockSpec((CTX_ROWS, CTX_LEN, D_MODEL), lambda b: (b, 0, 0)),
                  _const_spec((1, 6, D_MODEL)), _const_spec((1, D_MODEL)), _const_spec((1, D_MODEL)),
                  _const_spec((D_MODEL, CTX_PAD)), _const_spec((3, XBC_DIM)), _const_spec((1, XBC_DIM)),
                  _const_spec((N_DH, 1)), _const_spec((N_DH, 1)), _const_spec((LANES, 2 * SSD_WIDTH)),
                  _const_spec((CTX_LEN, 2 * CTX_LEN))],
        out_specs=pl.BlockSpec((CTX_ROWS, N_DIRS, SSD_GROUPS, SSD_STATE, GROUP_COLS),
                               lambda b: (b, 0, 0, 0, 0)),
        out_shape=jax.ShapeDtypeStruct((BATCH, N_DIRS, SSD_GROUPS, SSD_STATE, GROUP_COLS), F32),
        compiler_params=pltpu.CompilerParams(dimension_semantics=("arbitrary",),
                                             vmem_limit_bytes=VMEM_LIMIT),
        name="ctx",
    )(ctx, mod_ctx, lng, lnb, w_in, scw, scb, dtb, alog, exw, u_ctx)


def _proj_kernel(x_ref, mod_ref, lng_ref, lnb_ref, w_ref, cw_ref, scw_ref, scb_ref, dtb_ref,
                 alog_ref, exe_ref, exw_ref, u_ref, h0_ref,
                 hln_ref, ycv_ref, zg_ref, xs_ref, bt_ref, cm_ref, rows_ref, cols_ref, colf_ref, sb_ref,
                 st_ref):
    @pl.when(pl.program_id(1) == 0)
    def _():
        st_ref[...] = h0_ref[0, 0]

    m = mod_ref[0]
    sc = 1.0 + m[1:2]
    scale, shift = lng_ref[...] * sc, lnb_ref[...] * sc + m[0:1]
    ubs, pas = [], []
    for r in range(2):
        rows = slice(r * (TM // 2), (r + 1) * (TM // 2))
        xhat = _ln_hat(x_ref[0, rows, :])
        hln_ref[0, rows, :] = xhat * lng_ref[...] + lnb_ref[...]
        ubs.append((xhat * scale + shift).astype(BF16))
        pas.append(jnp.dot(ubs[r], w_ref[:, :P_Z], preferred_element_type=F32))
    ub = jnp.concatenate(ubs, axis=0)
    pa = jnp.concatenate(pas, axis=0)
    mprev, mnext = _edge_masks(TM, GRID_W)
    is_fwd = _fwd_rows()
    zero = jnp.zeros((N_DH, SSD_CHUNK), F32)
    chunks = [slice(c * SSD_CHUNK, (c + 1) * SSD_CHUNK) for c in range(NCH)]

    def conv_slabs(p, first):
        for i in range(2):
            gb, gc, gh = (p[:, (3 * i + k) * LANES:(3 * i + k + 1) * LANES] for k in range(3))
            sl = slice((first + i) * LANES, (first + i + 1) * LANES)
            ycv_ref[0, :, sl] = (gb * _conv3(gc * gh, cw_ref[:, sl], mprev, mnext)).astype(BF16)

    dts = [_dt_rows(pa[tok, P_DT:P_DT + LANES], dtb_ref[...], alog_ref[...]) for tok in chunks]
    pz = jnp.dot(ub, w_ref[:, P_Z:P_CONV], preferred_element_type=F32)

    colss = []
    for c, (dt, adt) in enumerate(dts):
        sc3 = _scan_mm(adt, u_ref)
        cs = jnp.where(is_fwd, sc3[:, :SSD_CHUNK], sc3[:, SSD_CHUNK:2 * SSD_CHUNK])
        tot = sc3[:, 2 * SSD_CHUNK:]
        e1 = jnp.exp(cs)
        w = jnp.exp(tot - cs) * dt
        rows_ref[0, c] = jnp.concatenate([cs, dt], axis=0)
        table = jnp.concatenate([zero] * 3 + _split(e1, 2) + _split(w, 2) + [cs], axis=0).T
        colf_ref[0, c] = table
        colss.append(table.astype(BF16))
        cols_ref[0, c] = colss[c]

    pc1 = jnp.dot(ub, w_ref[:, P_CONV:P_CONV + 6 * LANES], preferred_element_type=F32)

    slabs = []
    for j in range(XBC_DIM // LANES):
        sl = slice(j * LANES, (j + 1) * LANES)
        pj = pa[:, P_XBC + j * LANES:P_XBC + (j + 1) * LANES]
        slabs.append(_silu(_conv3(pj, scw_ref[:, sl], mprev, mnext) + scb_ref[:, sl]))
    xs = jnp.concatenate(slabs[:4], axis=1)
    xs_ref[0] = xs.astype(BF16)
    cm_ref[0] = jnp.concatenate(slabs[6:8], axis=1).astype(BF16)
    zg_ref[0] = _silu(pz).astype(BF16)
    bts = []
    for c, tok in enumerate(chunks):
        bt = [slabs[4 + g][tok].T.astype(BF16) for g in range(SSD_GROUPS)]
        for g in range(SSD_GROUPS):
            bt_ref[0, c, g] = bt[g]
        bts.append(bt)
    wxbs = [jnp.dot(cols, exw_ref[:, SSD_WIDTH:], preferred_element_type=F32) for cols in colss]
    decs = [jnp.dot(cols[0:16], exe_ref[:, SSD_WIDTH:], preferred_element_type=F32)[0:1]
            for cols in colss]

    pc2 = jnp.dot(ub, w_ref[:, P_CONV + 6 * LANES:], preferred_element_type=F32)
    conv_slabs(pc1, 0)

    state = [st_ref[g] for g in range(SSD_GROUPS)]
    for c in reversed(range(NCH)):
        xw = (xs[chunks[c]] * wxbs[c]).astype(BF16)
        for g in range(SSD_GROUPS):
            gs = slice(g * GROUP_COLS, (g + 1) * GROUP_COLS)
            sb_ref[0, c, g] = state[g].astype(BF16)
            local = jnp.dot(bts[c][g], xw[:, gs], preferred_element_type=F32)
            state[g] = state[g] * decs[c][:, gs] + local
    for g in range(SSD_GROUPS):
        st_ref[g] = state[g]
    conv_slabs(pc2, 2)


def _proj_call(x, mod_lat, lng, lnb, w_in, cw, scw, scb, dtb, alog, exe, exw, u_chunk, h0):
    rev = lambda b, t: (b, NT - 1 - t, 0)
    rev4 = lambda b, t: (b, NT - 1 - t, 0, 0)
    rev5 = lambda b, t: (b, NT - 1 - t, 0, 0, 0)
    tok = lambda width: pl.BlockSpec((1, TM, width), rev)
    out_shape = [
        jax.ShapeDtypeStruct((BATCH, SEQ, D_MODEL), F32),
        jax.ShapeDtypeStruct((BATCH, SEQ, CONV_WIDTH), BF16),
        jax.ShapeDtypeStruct((BATCH, SEQ, SSD_WIDTH), BF16),
        jax.ShapeDtypeStruct((BATCH, SEQ, SSD_WIDTH), BF16),
        jax.ShapeDtypeStruct((BATCH, NCHUNK, SSD_GROUPS, SSD_STATE, SSD_CHUNK), BF16),
        jax.ShapeDtypeStruct((BATCH, SEQ, SSD_GN), BF16),
        jax.ShapeDtypeStruct((BATCH, NCHUNK, 2 * N_DH, SSD_CHUNK), F32),
        jax.ShapeDtypeStruct((BATCH, NCHUNK, SSD_CHUNK, LANES), BF16),
        jax.ShapeDtypeStruct((BATCH, NCHUNK, SSD_CHUNK, LANES), F32),
        jax.ShapeDtypeStruct((BATCH, NCHUNK, SSD_GROUPS, SSD_STATE, GROUP_COLS), BF16),
    ]
    out_specs = [
        tok(D_MODEL), tok(CONV_WIDTH), tok(SSD_WIDTH), tok(SSD_WIDTH),
        pl.BlockSpec((1, NCH, SSD_GROUPS, SSD_STATE, SSD_CHUNK), rev5),
        tok(SSD_GN),
        pl.BlockSpec((1, NCH, 2 * N_DH, SSD_CHUNK), rev4),
        pl.BlockSpec((1, NCH, SSD_CHUNK, LANES), rev4),
        pl.BlockSpec((1, NCH, SSD_CHUNK, LANES), rev4),
        pl.BlockSpec((1, NCH, SSD_GROUPS, SSD_STATE, GROUP_COLS), rev5),
    ]
    in_specs = [
        pl.BlockSpec((1, TM, D_MODEL), rev),
        pl.BlockSpec((1, 6, D_MODEL), lambda b, t: (b, 0, 0)),
        _const_spec((1, D_MODEL)), _const_spec((1, D_MODEL)),
        _const_spec((D_MODEL, IN_PAD)),
        _const_spec((3, CONV_WIDTH)), _const_spec((3, XBC_DIM)), _const_spec((1, XBC_DIM)),
        _const_spec((N_DH, 1)), _const_spec((N_DH, 1)),
        _const_spec((LANES, 2 * SSD_WIDTH)), _const_spec((LANES, 2 * SSD_WIDTH)),
        _const_spec((SSD_CHUNK, 3 * SSD_CHUNK)),
        pl.BlockSpec((1, 1, SSD_GROUPS, SSD_STATE, GROUP_COLS), lambda b, t: (b, 1, 0, 0, 0)),
    ]
    return pl.pallas_call(
        _proj_kernel,
        grid=(BATCH, NT),
        in_specs=in_specs,
        out_specs=out_specs,
        out_shape=out_shape,
        scratch_shapes=[pltpu.VMEM((SSD_GROUPS, SSD_STATE, GROUP_COLS), F32)],
        compiler_params=pltpu.CompilerParams(dimension_semantics=("arbitrary", "arbitrary"),
                                             vmem_limit_bytes=VMEM_LIMIT),
        name="proj",
    )(x, mod_lat, lng, lnb, w_in, cw, scw, scb, dtb, alog, exe, exw, u_chunk, h0)


def _out_kernel(hln_ref, mod_ref, ycv_ref, zg_ref, xs_ref, bt_ref, cm_ref, rows_ref,
                cols_ref, colf_ref, sb_ref, h0_ref, exw_ref, dx_ref, nw_ref, wout_ref,
                ln1g_ref, ln1b_ref, wff1_ref, wff2_ref, ln2g_ref, ln2b_ref,
                out_ref, st_ref, yn_ref, h1_ref, u2_ref, g2_ref, r2_ref, hid_ref):
    s = pl.program_id(0)
    wr = s % 2
    rd = (s + 1) % 2
    li = lax.broadcasted_iota(jnp.int32, (SSD_CHUNK, SSD_CHUNK), 0)
    si = lax.broadcasted_iota(jnp.int32, (SSD_CHUNK, SSD_CHUNK), 1)
    low = li >= si
    diag = li == si
    lo_half = si < SSD_HEADDIM

    def ssd_chunk(c, state, fill):
        tok = slice(c * SSD_CHUNK, (c + 1) * SSD_CHUNK)
        rows = rows_ref[0, c]
        cols = cols_ref[0, c]
        xs = xs_ref[0, tok, :]
        cm = cm_ref[0, tok, :]
        colf = colf_ref[0, c]
        bc = [jnp.broadcast_to(colf[:, COL_CSF + j:COL_CSF + j + 1], (SSD_CHUNK, LANES))
              for j in range(N_DH)]
        e1x = [jnp.concatenate(
            [jnp.exp(jnp.where(lo_half, bc[d * SSD_HEADS + 2 * k], bc[d * SSD_HEADS + 2 * k + 1]))
             for k in range(SSD_HEADS // 2)], axis=1) for d in range(N_DIRS)]
        wxf = jnp.dot(cols, exw_ref[:, :SSD_WIDTH], preferred_element_type=F32)
        dec = e1x[0][SSD_CHUNK - 1:SSD_CHUNK, :]
        gmat = [jnp.dot(cm[:, g * SSD_STATE:(g + 1) * SSD_STATE], bt_ref[0, c, g],
                        preferred_element_type=F32) for g in range(SSD_GROUPS)]

        ys = []
        for k in range(SSD_HEADS // 2):
            ms = []
            for h in (2 * k, 2 * k + 1):
                g = h // (SSD_HEADS // SSD_GROUPS)
                hb = SSD_HEADS + h
                arg = jnp.where(low,
                                bc[h] - rows[h:h + 1, :],
                                bc[hb] - rows[hb:hb + 1, :])
                dtf = rows[N_DH + h:N_DH + h + 1, :]
                dtb = rows[N_DH + hb:N_DH + hb + 1, :]
                fac = jnp.where(low, dtf, dtb) + jnp.where(diag, dtb, 0.0)
                ms.append((gmat[g] * jnp.exp(arg) * fac).astype(BF16))
            xp = xs[:, k * LANES:(k + 1) * LANES]
            rhs = jnp.concatenate([jnp.where(lo_half, xp, jnp.zeros_like(xp)),
                                   jnp.where(lo_half, jnp.zeros_like(xp), xp)], axis=0)
            ys.append(jnp.dot(jnp.concatenate(ms, axis=1), rhs, preferred_element_type=F32))
            fill[k]()
        y = jnp.concatenate(ys, axis=1)

        yf, yb = [], []
        for g in range(SSD_GROUPS):
            cg = cm[:, g * SSD_STATE:(g + 1) * SSD_STATE]
            yf.append(jnp.dot(cg, state[g].astype(BF16), preferred_element_type=F32))
            yb.append(jnp.dot(cg, sb_ref[0, c, g], preferred_element_type=F32))
        y = (y + jnp.concatenate(yf, axis=1) * e1x[0]
             + jnp.concatenate(yb, axis=1) * e1x[1]
             + xs.astype(F32) * dx_ref[...])

        yg = y * zg_ref[0, tok, :].astype(F32)
        ms_ = jnp.mean(yg * yg, axis=-1, keepdims=True)
        yn_ref[tok, :] = (yg * lax.rsqrt(ms_ + RMS_EPS) * nw_ref[...]).astype(BF16)

        xw = (xs.astype(F32) * wxf).astype(BF16)
        new_state = []
        for g in range(SSD_GROUPS):
            gs = slice(g * GROUP_COLS, (g + 1) * GROUP_COLS)
            local = jnp.dot(bt_ref[0, c, g], xw[:, gs], preferred_element_type=F32)
            new_state.append(state[g] * dec[:, gs] + local)
        return new_state

    def mlp_up(j, k):
        blk = slice(j * FF_BLK + k * FF_SUB, j * FF_BLK + (k + 1) * FF_SUB)
        hid = jnp.maximum(jnp.dot(u2_ref[...], wff1_ref[:, blk], preferred_element_type=F32), 0.0)
        hid_ref[:, blk] = (hid * hid).astype(BF16)

    def mlp_down(j, acc):
        blk = slice(j * FF_BLK, (j + 1) * FF_BLK)
        part = jnp.dot(hid_ref[:, blk], wff2_ref[blk, :], preferred_element_type=F32)
        return part if acc is None else acc + part

    def ln2_rows(i):
        r = slice(i * SSD_CHUNK, (i + 1) * SSD_CHUNK)
        out_ref[0, r, :] = _ln_hat(r2_ref[r, :]) * ln2g_ref[...] + ln2b_ref[...]

    def step(mixer, mlp, norm):
        nothing = lambda: None
        state, acc = None, None
        if mixer:
            @pl.when(jnp.minimum(s, N_TILES - 1) % NT == 0)
            def _():
                st_ref[...] = h0_ref[0, 0]
            state = [st_ref[g] for g in range(SSD_GROUPS)]
        for c in range(NCH):
            ups = [(lambda k=k: mlp_up(c, k)) if mlp else nothing for k in range(FF_BLK // FF_SUB)]
            if mixer:
                state = ssd_chunk(c, state, ups)
            else:
                for up in ups:
                    up()
            if norm:
                ln2_rows(c)
        if mixer:
            for g in range(SSD_GROUPS):
                st_ref[g] = state[g]
            mix = (jnp.dot(ycv_ref[0], wout_ref[:CONV_WIDTH, :], preferred_element_type=F32)
                   + jnp.dot(yn_ref[...], wout_ref[CONV_WIDTH:, :], preferred_element_type=F32))
            m = mod_ref[0]
            g2_ref[wr] = m[5:6]
        for j in range(NCH):
            if mlp:
                acc = mlp_down(j, acc)
            if mixer:
                r = slice(j * SSD_CHUNK, (j + 1) * SSD_CHUNK)
                h1 = _ln_hat(ALPHA * hln_ref[0, r, :] + m[2:3] * mix[r]) * ln1g_ref[...] + ln1b_ref[...]
                h1_ref[wr, r, :] = h1
                u2_ref[r, :] = (h1 * (1.0 + m[4:5]) + m[3:4]).astype(BF16)
        if mlp:
            r2_ref[...] = ALPHA * h1_ref[rd] + g2_ref[rd] * acc

    @pl.when(s == 0)
    def _():
        r2_ref[...] = jnp.zeros((TM, D_MODEL), F32)
        step(True, False, False)

    @pl.when(jnp.logical_and(s >= 1, s <= N_TILES))
    def _():
        step(True, True, True)

    @pl.when(s == N_TILES + 1)
    def _():
        step(False, False, True)


def _out_call(hln, mod_lat, ycv, zg, xs, bt, cm, rows, cols, colf, sb, h0, exw, dx, nw,
              wout, ln1g, ln1b, wff1, wff2, ln2g, ln2b):
    cur = lambda s: jnp.minimum(s, N_TILES - 1)
    fwd = lambda s: (cur(s) // NT, cur(s) % NT, 0)
    fwd4 = lambda s: (cur(s) // NT, cur(s) % NT, 0, 0)
    fwd5 = lambda s: (cur(s) // NT, cur(s) % NT, 0, 0, 0)
    done = lambda s: (jnp.maximum(s - 2, 0) // NT, jnp.maximum(s - 2, 0) % NT, 0)
    tok = lambda width: pl.BlockSpec((1, TM, width), fwd)
    in_specs = [
        tok(D_MODEL),
        pl.BlockSpec((1, 6, D_MODEL), lambda s: (cur(s) // NT, 0, 0)),
        tok(CONV_WIDTH), tok(SSD_WIDTH), tok(SSD_WIDTH),
        pl.BlockSpec((1, NCH, SSD_GROUPS, SSD_STATE, SSD_CHUNK), fwd5),
        tok(SSD_GN),
        pl.BlockSpec((1, NCH, 2 * N_DH, SSD_CHUNK), fwd4),
        pl.BlockSpec((1, NCH, SSD_CHUNK, LANES), fwd4),
        pl.BlockSpec((1, NCH, SSD_CHUNK, LANES), fwd4),
        pl.BlockSpec((1, NCH, SSD_GROUPS, SSD_STATE, GROUP_COLS), fwd5),
        pl.BlockSpec((1, 1, SSD_GROUPS, SSD_STATE, GROUP_COLS), lambda s: (cur(s) // NT, 0, 0, 0, 0)),
        _const_spec((LANES, 2 * SSD_WIDTH)),
        _const_spec((1, SSD_WIDTH)), _const_spec((1, SSD_WIDTH)),
        _const_spec((D_MODEL, D_MODEL)),
        _const_spec((1, D_MODEL)), _const_spec((1, D_MODEL)),
        _const_spec((D_MODEL, D_FF)), _const_spec((D_FF, D_MODEL)),
        _const_spec((1, D_MODEL)), _const_spec((1, D_MODEL)),
    ]
    return pl.pallas_call(
        _out_kernel,
        grid=(N_TILES + 2,),
        in_specs=in_specs,
        out_specs=pl.BlockSpec((1, TM, D_MODEL), done),
        out_shape=jax.ShapeDtypeStruct((BATCH, SEQ, D_MODEL), F32),
        scratch_shapes=[pltpu.VMEM((SSD_GROUPS, SSD_STATE, GROUP_COLS), F32),
                        pltpu.VMEM((TM, SSD_WIDTH), BF16),
                        pltpu.VMEM((2, TM, D_MODEL), F32),
                        pltpu.VMEM((TM, D_MODEL), BF16),
                        pltpu.VMEM((2, 1, D_MODEL), F32),
                        pltpu.VMEM((TM, D_MODEL), F32),
                        pltpu.VMEM((TM, D_FF), BF16)],
        compiler_params=pltpu.CompilerParams(dimension_semantics=("arbitrary",),
                                             vmem_limit_bytes=VMEM_LIMIT),
        name="out",
    )(hln, mod_lat, ycv, zg, xs, bt, cm, rows, cols, colf, sb, h0, exw, dx, nw,
      wout, ln1g, ln1b, wff1, wff2, ln2g, ln2b)


def kernel(x, c, ctx, c_ctx, ln_in_g, ln_in_b, w_mod, b_mod, w_in, conv_w, ssd_conv_w, ssd_conv_b,
           dt_bias, a_log, ssd_d, ssd_norm_w, w_out, ln1_g, ln1_b, w_ff1, w_ff2, ln2_g, ln2_b):
    row = lambda v: v.reshape(1, -1).astype(F32)
    cvec = jnp.concatenate([c, c_ctx[None, :], jnp.zeros((16 - BATCH - 1, D_MODEL), F32)], axis=0)
    mod, w_in_p, w_out_p, w_ff1_p, w_ff2_p = _prep_call(cvec, w_mod, b_mod, w_in[0].T, w_out, w_ff1, w_ff2)
    mod_lat = mod[:BATCH].reshape(BATCH, 6, D_MODEL)
    mod_ctx = mod[BATCH:BATCH + 1].reshape(1, 6, D_MODEL)

    lng, lnb = row(ln_in_g), row(ln_in_b)
    scw, scb = ssd_conv_w[0], row(ssd_conv_b[0])
    dtb = dt_bias[0].reshape(N_DH, 1)
    alog = a_log[0].reshape(N_DH, 1)
    exe = jnp.asarray(_EXE, BF16)
    exw = jnp.asarray(_EXW, BF16)

    h0 = _ctx_call(ctx, mod_ctx, lng, lnb, w_in_p, scw, scb, dtb, alog, exw, jnp.asarray(_U_CTX, BF16))
    hln, ycv, zg, xs, bt, cm, rows, cols, colf, sb = _proj_call(
        x, mod_lat, lng, lnb, w_in_p, conv_w[0], scw, scb, dtb, alog, exe, exw,
        jnp.asarray(_U_CHUNK, BF16), h0)
    dx = jnp.repeat(ssd_d[0], SSD_HEADDIM).reshape(1, SSD_WIDTH)
    return _out_call(hln, mod_lat, ycv, zg, xs, bt, cm, rows, cols, colf, sb, h0, exw,
                     dx, row(ssd_norm_w[0]), w_out_p, row(ln1_g[0]), row(ln1_b[0]),
                     w_ff1_p, w_ff2_p, row(ln2_g[0]), row(ln2_b[0]))
```

```python
import jax
import jax.numpy as jnp
import numpy as np
from jax import lax
from jax.experimental import pallas as pl
from jax.experimental.pallas import tpu as pltpu

F32 = jnp.float32
BF16 = jnp.bfloat16

D_MODEL = 1024
BATCH = 8
SEQ = 2048
CTX_LEN = 256
GRID_W = 64
CONV_WIDTH = 512
SSD_WIDTH = 512
SSD_HEADDIM = 64
SSD_HEADS = 8
SSD_GROUPS = 2
SSD_STATE = 128
SSD_CHUNK = 128
N_DIRS = 2
D_FF = 4 * D_MODEL
LN_EPS = 1e-5
RMS_EPS = 1e-5
SSD_GN = SSD_GROUPS * SSD_STATE
XBC_DIM = SSD_WIDTH + 2 * SSD_GN
Z_OFF = 3 * CONV_WIDTH
XBC_OFF = Z_OFF + SSD_WIDTH
DT_OFF = XBC_OFF + XBC_DIM
N_DH = N_DIRS * SSD_HEADS
LANES = 128
P_DT = 0
P_XBC = LANES
P_Z = P_XBC + XBC_DIM
P_CONV = P_Z + SSD_WIDTH
IN_PAD = P_CONV + 3 * CONV_WIDTH
CTX_PAD = P_XBC + SSD_WIDTH + SSD_GN
GROUP_COLS = (SSD_HEADS // SSD_GROUPS) * SSD_HEADDIM
ALPHA = 2.0 ** 0.25

TM = 512
CTX_ROWS = 2
NCH = TM // SSD_CHUNK
NT = SEQ // TM
NCHUNK = SEQ // SSD_CHUNK
N_TILES = BATCH * NT
TM_PROJ = 1024
NCH_PROJ = TM_PROJ // SSD_CHUNK
NT_PROJ = SEQ // TM_PROJ
FF_BLK = 1024
FF_SUB = 256
VMEM_LIMIT = 58 * 1024 * 1024

COL_E1 = 48
COL_W = 80
COL_CSF = 112


def _expansion(col0, pieces, width):
    m = np.zeros((LANES, N_DH * width), np.float32)
    for t in range(pieces):
        for j in range(N_DH):
            m[col0 + 16 * t + j, j * width:(j + 1) * width] = 1.0
    return m


_EXE = _expansion(COL_E1, 2, SSD_HEADDIM)
_EXW = _expansion(COL_W, 2, SSD_HEADDIM)


def _ln_hat(x):
    mu = jnp.mean(x, axis=-1, keepdims=True)
    xc = x - mu
    var = jnp.mean(xc * xc, axis=-1, keepdims=True)
    return xc * lax.rsqrt(var + LN_EPS)


def _silu(x):
    return x / (1.0 + jnp.exp(-x))


def _softplus(x):
    return jnp.maximum(x, 0.0) + jnp.log1p(jnp.exp(-jnp.abs(x)))


def _edge_masks(rows, period):
    pos = lax.broadcasted_iota(jnp.int32, (rows, LANES), 0) % period
    return (pos != 0).astype(F32), (pos != period - 1).astype(F32)


def _conv3(t, w, mprev, mnext):
    rows = t.shape[0]
    prev = pltpu.roll(t, 1, 0) * mprev
    nxt = pltpu.roll(t, rows - 1, 0) * mnext
    return prev * w[0:1, :] + t * w[1:2, :] + nxt * w[2:3, :]


def _split(v, pieces):
    out = []
    for _ in range(pieces - 1):
        p = v.astype(BF16).astype(F32)
        out.append(p)
        v = v - p
    out.append(v.astype(BF16).astype(F32))
    return out


def _tri(length, op):
    i = np.arange(length)
    return op(i[:, None], i[None, :]).astype(np.float32)


_U_CHUNK = np.concatenate([_tri(SSD_CHUNK, np.less_equal), _tri(SSD_CHUNK, np.greater_equal),
                           np.ones((SSD_CHUNK, SSD_CHUNK), np.float32)], axis=1)
_U_CTX = np.concatenate([_tri(CTX_LEN, np.greater), _tri(CTX_LEN, np.less)], axis=1)


def _scan_mm(v, u_ref):
    pieces = jnp.concatenate(_split(v, 3), axis=0).astype(BF16)
    o = jnp.dot(pieces, u_ref[...], preferred_element_type=F32)
    return o[0:N_DH] + o[N_DH:2 * N_DH] + o[2 * N_DH:3 * N_DH]


def _dt_rows(raw, dtb, a_log):
    r = raw.T[0:N_DH, :] + dtb
    dt = _softplus(r)
    return dt, dt * (-jnp.exp(a_log))


def _fwd_rows():
    return lax.broadcasted_iota(jnp.int32, (N_DH, 1), 0) < SSD_HEADS


def _const_spec(shape):
    nd = len(shape)
    return pl.BlockSpec(shape, lambda *_: (0,) * nd, pipeline_mode=pl.Buffered(1))


def _prep_kernel(c_ref, wmod_ref, bmod_ref, wint_ref, wout_ref, wff1_ref, wff2_ref,
                 mod_ref, pin_ref, pout_ref, pff1_ref, pff2_ref):
    cs = _silu(c_ref[...]).astype(BF16)
    mod_ref[...] = jnp.dot(cs, wmod_ref[0].astype(BF16), preferred_element_type=F32) + bmod_ref[...]

    dt_rows = jnp.concatenate([wint_ref[DT_OFF:DT_OFF + N_DH, :],
                               jnp.zeros((LANES - N_DH, wint_ref.shape[1]), F32)], axis=0)
    pin_ref[:, P_DT:P_DT + LANES] = dt_rows.T.astype(BF16)
    def move(dst, src):
        pin_ref[:, dst:dst + LANES] = wint_ref[src:src + LANES, :].T.astype(BF16)

    for dst, src, width in ((P_XBC, XBC_OFF, XBC_DIM), (P_Z, Z_OFF, SSD_WIDTH)):
        for j in range(0, width, LANES):
            move(dst + j, src + j)
    for j in range(CONV_WIDTH // LANES):
        for k in range(3):
            move(P_CONV + (3 * j + k) * LANES, k * CONV_WIDTH + j * LANES)
    pout_ref[...] = wout_ref[0].astype(BF16)
    pff1_ref[...] = wff1_ref[0].astype(BF16)
    pff2_ref[...] = wff2_ref[0].astype(BF16)


def _prep_call(cvec, w_mod, b_mod, w_in_t, w_out, w_ff1, w_ff2):
    steps = 8
    r1, r4, rm = D_MODEL // steps, D_FF // steps, 6 * D_MODEL // steps
    return pl.pallas_call(
        _prep_kernel,
        grid=(steps,),
        in_specs=[pl.BlockSpec((16, D_MODEL), lambda i: (0, 0)),
                  pl.BlockSpec((1, D_MODEL, rm), lambda i: (0, 0, i)),
                  pl.BlockSpec((1, rm), lambda i: (0, i)),
                  pl.BlockSpec((w_in_t.shape[0], r1), lambda i: (0, i)),
                  pl.BlockSpec((1, r1, D_MODEL), lambda i: (0, i, 0)),
                  pl.BlockSpec((1, r1, D_FF), lambda i: (0, i, 0)),
                  pl.BlockSpec((1, r4, D_MODEL), lambda i: (0, i, 0))],
        out_specs=[pl.BlockSpec((16, rm), lambda i: (0, i)),
                   pl.BlockSpec((r1, IN_PAD), lambda i: (i, 0)),
                   pl.BlockSpec((r1, D_MODEL), lambda i: (i, 0)),
                   pl.BlockSpec((r1, D_FF), lambda i: (i, 0)),
                   pl.BlockSpec((r4, D_MODEL), lambda i: (i, 0))],
        out_shape=[jax.ShapeDtypeStruct((16, 6 * D_MODEL), F32),
                   jax.ShapeDtypeStruct((D_MODEL, IN_PAD), BF16),
                   jax.ShapeDtypeStruct((D_MODEL, D_MODEL), BF16),
                   jax.ShapeDtypeStruct((D_MODEL, D_FF), BF16),
                   jax.ShapeDtypeStruct((D_FF, D_MODEL), BF16)],
        compiler_params=pltpu.CompilerParams(dimension_semantics=("arbitrary",),
                                             vmem_limit_bytes=VMEM_LIMIT),
        name="prep",
    )(cvec, w_mod, b_mod, w_in_t, w_out, w_ff1, w_ff2)


def _ctx_kernel(x_ref, mod_ref, lng_ref, lnb_ref, w_ref, scw_ref, scb_ref, dtb_ref, alog_ref,
                exw_ref, u_ref, h0_ref):
    m = mod_ref[0]
    sc = 1.0 + m[1:2]
    x = x_ref[...].reshape(CTX_ROWS * CTX_LEN, D_MODEL)
    u = _ln_hat(x) * (lng_ref[...] * sc) + (lnb_ref[...] * sc + m[0:1])
    proj = jnp.dot(u.astype(BF16), w_ref[...], preferred_element_type=F32)
    mprev, mnext = _edge_masks(CTX_ROWS * CTX_LEN, CTX_LEN)
    slabs = []
    for j in range((SSD_WIDTH + SSD_GN) // LANES):
        sl = slice(j * LANES, (j + 1) * LANES)
        pj = proj[:, P_XBC + j * LANES:P_XBC + (j + 1) * LANES]
        slabs.append(_silu(_conv3(pj, scw_ref[:, sl], mprev, mnext) + scb_ref[:, sl]))
    zero = jnp.zeros((N_DH, CTX_LEN), F32)
    for i in range(CTX_ROWS):
        tok = slice(i * CTX_LEN, (i + 1) * CTX_LEN)
        xs = jnp.concatenate([sl_[tok] for sl_ in slabs[:4]], axis=1)
        dt, adt = _dt_rows(proj[tok, P_DT:P_DT + LANES], dtb_ref[...], alog_ref[...])
        sc2 = _scan_mm(adt, u_ref)
        excl = jnp.where(_fwd_rows(), sc2[:, :CTX_LEN], sc2[:, CTX_LEN:])
        w = jnp.exp(excl) * dt
        table = jnp.concatenate([zero] * (COL_W // 16) + _split(w, 2) + [zero], axis=0)
        cols = table.T.astype(BF16)
        wx = jnp.dot(cols, exw_ref[...], preferred_element_type=F32)
        for d in range(N_DIRS):
            xw = (xs * wx[:, d * SSD_WIDTH:(d + 1) * SSD_WIDTH]).astype(BF16)
            for g in range(SSD_GROUPS):
                bt = slabs[4 + g][tok].T.astype(BF16)
                h0_ref[i, d, g] = jnp.dot(bt, xw[:, g * GROUP_COLS:(g + 1) * GROUP_COLS],
                                          preferred_element_type=F32)


def _ctx_call(ctx, mod_ctx, lng, lnb, w_in, scw, scb, dtb, alog, exw, u_ctx):
    return pl.pallas_call(
        _ctx_kernel,
        grid=(BATCH // CTX_ROWS,),
        in_specs=[pl.BlockSpec((CTX_ROWS, CTX_LEN, D_MODEL), lambda b: (b, 0, 0)),
                  _const_spec((1, 6, D_MODEL)), _const_spec((1, D_MODEL)), _const_spec((1, D_MODEL)),
                  _const_spec((D_MODEL, CTX_PAD)), _const_spec((3, XBC_DIM)), _const_spec((1, XBC_DIM)),
                  _const_spec((N_DH, 1)), _const_spec((N_DH, 1)), _const_spec((LANES, 2 * SSD_WIDTH)),
                  _const_spec((CTX_LEN, 2 * CTX_LEN))],
        out_specs=pl.BlockSpec((CTX_ROWS, N_DIRS, SSD_GROUPS, SSD_STATE, GROUP_COLS),
                               lambda b: (b, 0, 0, 0, 0)),
        out_shape=jax.ShapeDtypeStruct((BATCH, N_DIRS, SSD_GROUPS, SSD_STATE, GROUP_COLS), F32),
        compiler_params=pltpu.CompilerParams(dimension_semantics=("arbitrary",),
                                             vmem_limit_bytes=VMEM_LIMIT),
        name="ctx",
    )(ctx, mod_ctx, lng, lnb, w_in, scw, scb, dtb, alog, exw, u_ctx)


def _proj_kernel(x_ref, mod_ref, lng_ref, lnb_ref, w_ref, cw_ref, scw_ref, scb_ref, dtb_ref,
                 alog_ref, exe_ref, exw_ref, u_ref, h0_ref,
                 hln_ref, ycv_ref, zg_ref, xs_ref, bt_ref, cm_ref, rows_ref, cols_ref, colf_ref, sb_ref,
                 st_ref):
    @pl.when(pl.program_id(1) == 0)
    def _():
        st_ref[...] = h0_ref[0, 0]

    m = mod_ref[0]
    sc = 1.0 + m[1:2]
    scale, shift = lng_ref[...] * sc, lnb_ref[...] * sc + m[0:1]
    ubs, pas = [], []
    for r in range(2):
        rows = slice(r * (TM_PROJ // 2), (r + 1) * (TM_PROJ // 2))
        xhat = _ln_hat(x_ref[0, rows, :])
        hln_ref[0, rows, :] = xhat * lng_ref[...] + lnb_ref[...]
        ubs.append((xhat * scale + shift).astype(BF16))
        pas.append(jnp.dot(ubs[r], w_ref[:, :P_Z], preferred_element_type=F32))
    ub = jnp.concatenate(ubs, axis=0)
    pa = jnp.concatenate(pas, axis=0)
    mprev, mnext = _edge_masks(TM_PROJ, GRID_W)
    is_fwd = _fwd_rows()
    zero = jnp.zeros((N_DH, SSD_CHUNK), F32)
    chunks = [slice(c * SSD_CHUNK, (c + 1) * SSD_CHUNK) for c in range(NCH_PROJ)]

    def conv_slabs(p, first):
        for i in range(2):
            gb, gc, gh = (p[:, (3 * i + k) * LANES:(3 * i + k + 1) * LANES] for k in range(3))
            sl = slice((first + i) * LANES, (first + i + 1) * LANES)
            ycv_ref[0, :, sl] = (gb * _conv3(gc * gh, cw_ref[:, sl], mprev, mnext)).astype(BF16)

    dts = [_dt_rows(pa[tok, P_DT:P_DT + LANES], dtb_ref[...], alog_ref[...]) for tok in chunks]
    pz = jnp.dot(ub, w_ref[:, P_Z:P_CONV], preferred_element_type=F32)

    colss = []
    for c, (dt, adt) in enumerate(dts):
        sc3 = _scan_mm(adt, u_ref)
        cs = jnp.where(is_fwd, sc3[:, :SSD_CHUNK], sc3[:, SSD_CHUNK:2 * SSD_CHUNK])
        tot = sc3[:, 2 * SSD_CHUNK:]
        e1 = jnp.exp(cs)
        w = jnp.exp(tot - cs) * dt
        rows_ref[0, c] = jnp.concatenate([cs, dt], axis=0)
        table = jnp.concatenate([zero] * 3 + _split(e1, 2) + _split(w, 2) + [cs], axis=0).T
        colf_ref[0, c] = table
        colss.append(table.astype(BF16))
        cols_ref[0, c] = colss[c]

    pc1 = jnp.dot(ub, w_ref[:, P_CONV:P_CONV + 6 * LANES], preferred_element_type=F32)

    slabs = []
    for j in range(XBC_DIM // LANES):
        sl = slice(j * LANES, (j + 1) * LANES)
        pj = pa[:, P_XBC + j * LANES:P_XBC + (j + 1) * LANES]
        slabs.append(_silu(_conv3(pj, scw_ref[:, sl], mprev, mnext) + scb_ref[:, sl]))
    xs = jnp.concatenate(slabs[:4], axis=1)
    xs_ref[0] = xs.astype(BF16)
    cm_ref[0] = jnp.concatenate(slabs[6:8], axis=1).astype(BF16)
    zg_ref[0] = _silu(pz).astype(BF16)
    bts = []
    for c, tok in enumerate(chunks):
        bt = [slabs[4 + g][tok].T.astype(BF16) for g in range(SSD_GROUPS)]
        for g in range(SSD_GROUPS):
            bt_ref[0, c, g] = bt[g]
        bts.append(bt)
    wxbs = [jnp.dot(cols, exw_ref[:, SSD_WIDTH:], preferred_element_type=F32) for cols in colss]
    decs = [jnp.dot(cols[0:16], exe_ref[:, SSD_WIDTH:], preferred_element_type=F32)[0:1]
            for cols in colss]

    pc2 = jnp.dot(ub, w_ref[:, P_CONV + 6 * LANES:], preferred_element_type=F32)
    conv_slabs(pc1, 0)

    state = [st_ref[g] for g in range(SSD_GROUPS)]
    for c in reversed(range(NCH_PROJ)):
        xw = (xs[chunks[c]] * wxbs[c]).astype(BF16)
        for g in range(SSD_GROUPS):
            gs = slice(g * GROUP_COLS, (g + 1) * GROUP_COLS)
            sb_ref[0, c, g] = state[g].astype(BF16)
            local = jnp.dot(bts[c][g], xw[:, gs], preferred_element_type=F32)
            state[g] = state[g] * decs[c][:, gs] + local
    for g in range(SSD_GROUPS):
        st_ref[g] = state[g]
    conv_slabs(pc2, 2)


def _proj_call(x, mod_lat, lng, lnb, w_in, cw, scw, scb, dtb, alog, exe, exw, u_chunk, h0):
    rev = lambda b, t: (b, NT_PROJ - 1 - t, 0)
    rev4 = lambda b, t: (b, NT_PROJ - 1 - t, 0, 0)
    rev5 = lambda b, t: (b, NT_PROJ - 1 - t, 0, 0, 0)
    tok = lambda width: pl.BlockSpec((1, TM_PROJ, width), rev)
    out_shape = [
        jax.ShapeDtypeStruct((BATCH, SEQ, D_MODEL), F32),
        jax.ShapeDtypeStruct((BATCH, SEQ, CONV_WIDTH), BF16),
        jax.ShapeDtypeStruct((BATCH, SEQ, SSD_WIDTH), BF16),
        jax.ShapeDtypeStruct((BATCH, SEQ, SSD_WIDTH), BF16),
        jax.ShapeDtypeStruct((BATCH, NCHUNK, SSD_GROUPS, SSD_STATE, SSD_CHUNK), BF16),
        jax.ShapeDtypeStruct((BATCH, SEQ, SSD_GN), BF16),
        jax.ShapeDtypeStruct((BATCH, NCHUNK, 2 * N_DH, SSD_CHUNK), F32),
        jax.ShapeDtypeStruct((BATCH, NCHUNK, SSD_CHUNK, LANES), BF16),
        jax.ShapeDtypeStruct((BATCH, NCHUNK, SSD_CHUNK, LANES), F32),
        jax.ShapeDtypeStruct((BATCH, NCHUNK, SSD_GROUPS, SSD_STATE, GROUP_COLS), BF16),
    ]
    out_specs = [
        tok(D_MODEL), tok(CONV_WIDTH), tok(SSD_WIDTH), tok(SSD_WIDTH),
        pl.BlockSpec((1, NCH_PROJ, SSD_GROUPS, SSD_STATE, SSD_CHUNK), rev5),
        tok(SSD_GN),
        pl.BlockSpec((1, NCH_PROJ, 2 * N_DH, SSD_CHUNK), rev4),
        pl.BlockSpec((1, NCH_PROJ, SSD_CHUNK, LANES), rev4),
        pl.BlockSpec((1, NCH_PROJ, SSD_CHUNK, LANES), rev4),
        pl.BlockSpec((1, NCH_PROJ, SSD_GROUPS, SSD_STATE, GROUP_COLS), rev5),
    ]
    in_specs = [
        pl.BlockSpec((1, TM_PROJ, D_MODEL), rev),
        pl.BlockSpec((1, 6, D_MODEL), lambda b, t: (b, 0, 0)),
        _const_spec((1, D_MODEL)), _const_spec((1, D_MODEL)),
        _const_spec((D_MODEL, IN_PAD)),
        _const_spec((3, CONV_WIDTH)), _const_spec((3, XBC_DIM)), _const_spec((1, XBC_DIM)),
        _const_spec((N_DH, 1)), _const_spec((N_DH, 1)),
        _const_spec((LANES, 2 * SSD_WIDTH)), _const_spec((LANES, 2 * SSD_WIDTH)),
        _const_spec((SSD_CHUNK, 3 * SSD_CHUNK)),
        pl.BlockSpec((1, 1, SSD_GROUPS, SSD_STATE, GROUP_COLS), lambda b, t: (b, 1, 0, 0, 0)),
    ]
    return pl.pallas_call(
        _proj_kernel,
        grid=(BATCH, NT_PROJ),
        in_specs=in_specs,
        out_specs=out_specs,
        out_shape=out_shape,
        scratch_shapes=[pltpu.VMEM((SSD_GROUPS, SSD_STATE, GROUP_COLS), F32)],
        compiler_params=pltpu.CompilerParams(dimension_semantics=("arbitrary", "arbitrary"),
                                             vmem_limit_bytes=VMEM_LIMIT),
        name="proj",
    )(x, mod_lat, lng, lnb, w_in, cw, scw, scb, dtb, alog, exe, exw, u_chunk, h0)


def _out_kernel(hln_ref, mod_ref, ycv_ref, zg_ref, xs_ref, bt_ref, cm_ref, rows_ref,
                cols_ref, colf_ref, sb_ref, h0_ref, exw_ref, dx_ref, nw_ref, wout_ref,
                ln1g_ref, ln1b_ref, wff1_ref, wff2_ref, ln2g_ref, ln2b_ref,
                out_ref, st_ref, yn_ref, h1_ref, u2_ref, g2_ref, r2_ref, hid_ref):
    s = pl.program_id(0)
    wr = s % 2
    rd = (s + 1) % 2
    li = lax.broadcasted_iota(jnp.int32, (SSD_CHUNK, SSD_CHUNK), 0)
    si = lax.broadcasted_iota(jnp.int32, (SSD_CHUNK, SSD_CHUNK), 1)
    low = li >= si
    diag = li == si
    lo_half = si < SSD_HEADDIM

    def ssd_chunk(c, state, fill):
        tok = slice(c * SSD_CHUNK, (c + 1) * SSD_CHUNK)
        rows = rows_ref[0, c]
        cols = cols_ref[0, c]
        xs = xs_ref[0, tok, :]
        cm = cm_ref[0, tok, :]
        colf = colf_ref[0, c]
        bc = [jnp.broadcast_to(colf[:, COL_CSF + j:COL_CSF + j + 1], (SSD_CHUNK, LANES))
              for j in range(N_DH)]
        e1x = [jnp.concatenate(
            [jnp.exp(jnp.where(lo_half, bc[d * SSD_HEADS + 2 * k], bc[d * SSD_HEADS + 2 * k + 1]))
             for k in range(SSD_HEADS // 2)], axis=1) for d in range(N_DIRS)]
        wxf = jnp.dot(cols, exw_ref[:, :SSD_WIDTH], preferred_element_type=F32)
        dec = e1x[0][SSD_CHUNK - 1:SSD_CHUNK, :]
        gmat = [jnp.dot(cm[:, g * SSD_STATE:(g + 1) * SSD_STATE], bt_ref[0, c, g],
                        preferred_element_type=F32) for g in range(SSD_GROUPS)]

        ys = []
        for k in range(SSD_HEADS // 2):
            ms = []
            for h in (2 * k, 2 * k + 1):
                g = h // (SSD_HEADS // SSD_GROUPS)
                hb = SSD_HEADS + h
                arg = jnp.where(low,
                                bc[h] - rows[h:h + 1, :],
                                bc[hb] - rows[hb:hb + 1, :])
                dtf = rows[N_DH + h:N_DH + h + 1, :]
                dtb = rows[N_DH + hb:N_DH + hb + 1, :]
                fac = jnp.where(low, dtf, dtb) + jnp.where(diag, dtb, 0.0)
                ms.append((gmat[g] * jnp.exp(arg) * fac).astype(BF16))
            xp = xs[:, k * LANES:(k + 1) * LANES]
            rhs = jnp.concatenate([jnp.where(lo_half, xp, jnp.zeros_like(xp)),
                                   jnp.where(lo_half, jnp.zeros_like(xp), xp)], axis=0)
            ys.append(jnp.dot(jnp.concatenate(ms, axis=1), rhs, preferred_element_type=F32))
            fill[k]()
        y = jnp.concatenate(ys, axis=1)

        yf, yb = [], []
        for g in range(SSD_GROUPS):
            cg = cm[:, g * SSD_STATE:(g + 1) * SSD_STATE]
            yf.append(jnp.dot(cg, state[g].astype(BF16), preferred_element_type=F32))
            yb.append(jnp.dot(cg, sb_ref[0, c, g], preferred_element_type=F32))
        y = (y + jnp.concatenate(yf, axis=1) * e1x[0]
             + jnp.concatenate(yb, axis=1) * e1x[1]
             + xs.astype(F32) * dx_ref[...])

        yg = y * zg_ref[0, tok, :].astype(F32)
        ms_ = jnp.mean(yg * yg, axis=-1, keepdims=True)
        yn_ref[tok, :] = (yg * lax.rsqrt(ms_ + RMS_EPS) * nw_ref[...]).astype(BF16)

        xw = (xs.astype(F32) * wxf).astype(BF16)
        new_state = []
        for g in range(SSD_GROUPS):
            gs = slice(g * GROUP_COLS, (g + 1) * GROUP_COLS)
            local = jnp.dot(bt_ref[0, c, g], xw[:, gs], preferred_element_type=F32)
            new_state.append(state[g] * dec[:, gs] + local)
        return new_state

    def mlp_up(j, k):
        blk = slice(j * FF_BLK + k * FF_SUB, j * FF_BLK + (k + 1) * FF_SUB)
        hid = jnp.maximum(jnp.dot(u2_ref[...], wff1_ref[:, blk], preferred_element_type=F32), 0.0)
        hid_ref[:, blk] = (hid * hid).astype(BF16)

    def mlp_down(j, acc):
        blk = slice(j * FF_BLK, (j + 1) * FF_BLK)
        part = jnp.dot(hid_ref[:, blk], wff2_ref[blk, :], preferred_element_type=F32)
        return part if acc is None else acc + part

    def ln2_rows(i):
        r = slice(i * SSD_CHUNK, (i + 1) * SSD_CHUNK)
        out_ref[0, r, :] = _ln_hat(r2_ref[r, :]) * ln2g_ref[...] + ln2b_ref[...]

    def step(mixer, mlp, norm):
        nothing = lambda: None
        state, acc = None, None
        if mixer:
            @pl.when(jnp.minimum(s, N_TILES - 1) % NT == 0)
            def _():
                st_ref[...] = h0_ref[0, 0]
            state = [st_ref[g] for g in range(SSD_GROUPS)]
        for c in range(NCH):
            ups = [(lambda k=k: mlp_up(c, k)) if mlp else nothing for k in range(FF_BLK // FF_SUB)]
            if mixer:
                state = ssd_chunk(c, state, ups)
            else:
                for up in ups:
                    up()
            if norm:
                ln2_rows(c)
        if mixer:
            for g in range(SSD_GROUPS):
                st_ref[g] = state[g]
            mix = (jnp.dot(ycv_ref[0], wout_ref[:CONV_WIDTH, :], preferred_element_type=F32)
                   + jnp.dot(yn_ref[...], wout_ref[CONV_WIDTH:, :], preferred_element_type=F32))
            m = mod_ref[0]
            g2_ref[wr] = m[5:6]
        for j in range(NCH):
            if mlp:
                acc = mlp_down(j, acc)
            if mixer:
                r = slice(j * SSD_CHUNK, (j + 1) * SSD_CHUNK)
                h1 = _ln_hat(ALPHA * hln_ref[0, r, :] + m[2:3] * mix[r]) * ln1g_ref[...] + ln1b_ref[...]
                h1_ref[wr, r, :] = h1
                u2_ref[r, :] = (h1 * (1.0 + m[4:5]) + m[3:4]).astype(BF16)
        if mlp:
            r2_ref[...] = ALPHA * h1_ref[rd] + g2_ref[rd] * acc

    @pl.when(s == 0)
    def _():
        r2_ref[...] = jnp.zeros((TM, D_MODEL), F32)
        step(True, False, False)

    @pl.when(jnp.logical_and(s >= 1, s <= N_TILES))
    def _():
        step(True, True, True)

    @pl.when(s == N_TILES + 1)
    def _():
        step(False, False, True)


def _out_call(hln, mod_lat, ycv, zg, xs, bt, cm, rows, cols, colf, sb, h0, exw, dx, nw,
              wout, ln1g, ln1b, wff1, wff2, ln2g, ln2b):
    cur = lambda s: jnp.minimum(s, N_TILES - 1)
    fwd = lambda s: (cur(s) // NT, cur(s) % NT, 0)
    fwd4 = lambda s: (cur(s) // NT, cur(s) % NT, 0, 0)
    fwd5 = lambda s: (cur(s) // NT, cur(s) % NT, 0, 0, 0)
    done = lambda s: (jnp.maximum(s - 2, 0) // NT, jnp.maximum(s - 2, 0) % NT, 0)
    tok = lambda width: pl.BlockSpec((1, TM, width), fwd)
    in_specs = [
        tok(D_MODEL),
        pl.BlockSpec((1, 6, D_MODEL), lambda s: (cur(s) // NT, 0, 0)),
        tok(CONV_WIDTH), tok(SSD_WIDTH), tok(SSD_WIDTH),
        pl.BlockSpec((1, NCH, SSD_GROUPS, SSD_STATE, SSD_CHUNK), fwd5),
        tok(SSD_GN),
        pl.BlockSpec((1, NCH, 2 * N_DH, SSD_CHUNK), fwd4),
        pl.BlockSpec((1, NCH, SSD_CHUNK, LANES), fwd4),
        pl.BlockSpec((1, NCH, SSD_CHUNK, LANES), fwd4),
        pl.BlockSpec((1, NCH, SSD_GROUPS, SSD_STATE, GROUP_COLS), fwd5),
        pl.BlockSpec((1, 1, SSD_GROUPS, SSD_STATE, GROUP_COLS), lambda s: (cur(s) // NT, 0, 0, 0, 0)),
        _const_spec((LANES, 2 * SSD_WIDTH)),
        _const_spec((1, SSD_WIDTH)), _const_spec((1, SSD_WIDTH)),
        _const_spec((D_MODEL, D_MODEL)),
        _const_spec((1, D_MODEL)), _const_spec((1, D_MODEL)),
        _const_spec((D_MODEL, D_FF)), _const_spec((D_FF, D_MODEL)),
        _const_spec((1, D_MODEL)), _const_spec((1, D_MODEL)),
    ]
    return pl.pallas_call(
        _out_kernel,
        grid=(N_TILES + 2,),
        in_specs=in_specs,
        out_specs=pl.BlockSpec((1, TM, D_MODEL), done),
        out_shape=jax.ShapeDtypeStruct((BATCH, SEQ, D_MODEL), F32),
        scratch_shapes=[pltpu.VMEM((SSD_GROUPS, SSD_STATE, GROUP_COLS), F32),
                        pltpu.VMEM((TM, SSD_WIDTH), BF16),
                        pltpu.VMEM((2, TM, D_MODEL), F32),
                        pltpu.VMEM((TM, D_MODEL), BF16),
                        pltpu.VMEM((2, 1, D_MODEL), F32),
                        pltpu.VMEM((TM, D_MODEL), F32),
                        pltpu.VMEM((TM, D_FF), BF16)],
        compiler_params=pltpu.CompilerParams(dimension_semantics=("arbitrary",),
                                             vmem_limit_bytes=VMEM_LIMIT),
        name="out",
    )(hln, mod_lat, ycv, zg, xs, bt, cm, rows, cols, colf, sb, h0, exw, dx, nw,
      wout, ln1g, ln1b, wff1, wff2, ln2g, ln2b)


def kernel(x, c, ctx, c_ctx, ln_in_g, ln_in_b, w_mod, b_mod, w_in, conv_w, ssd_conv_w, ssd_conv_b,
           dt_bias, a_log, ssd_d, ssd_norm_w, w_out, ln1_g, ln1_b, w_ff1, w_ff2, ln2_g, ln2_b):
    row = lambda v: v.reshape(1, -1).astype(F32)
    cvec = jnp.concatenate([c, c_ctx[None, :], jnp.zeros((16 - BATCH - 1, D_MODEL), F32)], axis=0)
    mod, w_in_p, w_out_p, w_ff1_p, w_ff2_p = _prep_call(cvec, w_mod, b_mod, w_in[0].T, w_out, w_ff1, w_ff2)
    mod_lat = mod[:BATCH].reshape(BATCH, 6, D_MODEL)
    mod_ctx = mod[BATCH:BATCH + 1].reshape(1, 6, D_MODEL)

    lng, lnb = row(ln_in_g), row(ln_in_b)
    scw, scb = ssd_conv_w[0], row(ssd_conv_b[0])
    dtb = dt_bias[0].reshape(N_DH, 1)
    alog = a_log[0].reshape(N_DH, 1)
    exe = jnp.asarray(_EXE, BF16)
    exw = jnp.asarray(_EXW, BF16)

    h0 = _ctx_call(ctx, mod_ctx, lng, lnb, w_in_p, scw, scb, dtb, alog, exw, jnp.asarray(_U_CTX, BF16))
    hln, ycv, zg, xs, bt, cm, rows, cols, colf, sb = _proj_call(
        x, mod_lat, lng, lnb, w_in_p, conv_w[0], scw, scb, dtb, alog, exe, exw,
        jnp.asarray(_U_CHUNK, BF16), h0)
    dx = jnp.repeat(ssd_d[0], SSD_HEADDIM).reshape(1, SSD_WIDTH)
    return _out_call(hln, mod_lat, ycv, zg, xs, bt, cm, rows, cols, colf, sb, h0, exw,
                     dx, row(ssd_norm_w[0]), w_out_p, row(ln1_g[0]), row(ln1_b[0]),
                     w_ff1_p, w_ff2_p, row(ln2_g[0]), row(ln2_b[0]))
```

```python
import functools

import jax
import jax.numpy as jnp
import numpy as np
from jax import lax
from jax.experimental import pallas as pl
from jax.experimental.pallas import tpu as pltpu

F32 = jnp.float32
BF16 = jnp.bfloat16

D_MODEL = 1024
BATCH = 8
SEQ = 2048
CTX_LEN = 256
GRID_W = 64
CONV_WIDTH = 512
SSD_WIDTH = 512
SSD_HEADDIM = 64
SSD_HEADS = 8
SSD_GROUPS = 2
SSD_STATE = 128
SSD_CHUNK = 128
N_DIRS = 2
D_FF = 4 * D_MODEL
LN_EPS = 1e-5
RMS_EPS = 1e-5
SSD_GN = SSD_GROUPS * SSD_STATE
XBC_DIM = SSD_WIDTH + 2 * SSD_GN
Z_OFF = 3 * CONV_WIDTH
XBC_OFF = Z_OFF + SSD_WIDTH
DT_OFF = XBC_OFF + XBC_DIM
N_DH = N_DIRS * SSD_HEADS
LANES = 128
P_DT = 0
P_XBC = LANES
P_Z = P_XBC + XBC_DIM
P_CONV = P_Z + SSD_WIDTH
IN_PAD = P_CONV + 3 * CONV_WIDTH
CTX_PAD = P_XBC + SSD_WIDTH + SSD_GN
GROUP_COLS = (SSD_HEADS // SSD_GROUPS) * SSD_HEADDIM
ALPHA = 2.0 ** 0.25

TM = 512
CTX_ROWS = 2
NCH = TM // SSD_CHUNK
NT = SEQ // TM
NCHUNK = SEQ // SSD_CHUNK
N_TILES = BATCH * NT
TM_PROJ = 1024
NCH_PROJ = TM_PROJ // SSD_CHUNK
NT_PROJ = SEQ // TM_PROJ
FF_BLK = 1024
FF_SUB = 256
N_SLAB = D_FF // FF_BLK
FILL_SLOTS = 6
DOWN_KEPT = 2
VMEM_LIMIT = 58 * 1024 * 1024

COL_E1 = 48
COL_W = 80
COL_CSF = 112


def _expansion(col0, pieces, width):
    m = np.zeros((LANES, N_DH * width), np.float32)
    for t in range(pieces):
        for j in range(N_DH):
            m[col0 + 16 * t + j, j * width:(j + 1) * width] = 1.0
    return m


_EXE = _expansion(COL_E1, 2, SSD_HEADDIM)
_EXW = _expansion(COL_W, 2, SSD_HEADDIM)


def _ln_hat(x):
    mu = jnp.mean(x, axis=-1, keepdims=True)
    xc = x - mu
    var = jnp.mean(xc * xc, axis=-1, keepdims=True)
    return xc * lax.rsqrt(var + LN_EPS)


def _silu(x):
    return x / (1.0 + jnp.exp(-x))


def _softplus(x):
    return jnp.maximum(x, 0.0) + jnp.log1p(jnp.exp(-jnp.abs(x)))


def _edge_masks(rows, period):
    pos = lax.broadcasted_iota(jnp.int32, (rows, LANES), 0) % period
    return (pos != 0).astype(F32), (pos != period - 1).astype(F32)


def _conv3(t, w, mprev, mnext):
    rows = t.shape[0]
    prev = pltpu.roll(t, 1, 0) * mprev
    nxt = pltpu.roll(t, rows - 1, 0) * mnext
    return prev * w[0:1, :] + t * w[1:2, :] + nxt * w[2:3, :]


def _split(v, pieces):
    out = []
    for _ in range(pieces - 1):
        p = v.astype(BF16).astype(F32)
        out.append(p)
        v = v - p
    out.append(v.astype(BF16).astype(F32))
    return out


def _tri(length, op):
    i = np.arange(length)
    return op(i[:, None], i[None, :]).astype(np.float32)


_U_CHUNK = np.concatenate([_tri(SSD_CHUNK, np.less_equal), _tri(SSD_CHUNK, np.greater_equal),
                           np.ones((SSD_CHUNK, SSD_CHUNK), np.float32)], axis=1)
_U_CTX = np.concatenate([_tri(CTX_LEN, np.greater), _tri(CTX_LEN, np.less)], axis=1)


def _scan_mm(v, u_ref):
    pieces = jnp.concatenate(_split(v, 3), axis=0).astype(BF16)
    o = jnp.dot(pieces, u_ref[...], preferred_element_type=F32)
    return o[0:N_DH] + o[N_DH:2 * N_DH] + o[2 * N_DH:3 * N_DH]


def _dt_rows(raw, dtb, a_log):
    r = raw.T[0:N_DH, :] + dtb
    dt = _softplus(r)
    return dt, dt * (-jnp.exp(a_log))


def _fwd_rows():
    return lax.broadcasted_iota(jnp.int32, (N_DH, 1), 0) < SSD_HEADS


def _const_spec(shape):
    nd = len(shape)
    return pl.BlockSpec(shape, lambda *_: (0,) * nd, pipeline_mode=pl.Buffered(1))


def _prep_kernel(c_ref, wmod_ref, bmod_ref, wint_ref, wout_ref, wff1_ref, wff2_ref,
                 mod_ref, pin_ref, pout_ref, pff1_ref, pff2_ref):
    cs = _silu(c_ref[...]).astype(BF16)
    mod_ref[...] = jnp.dot(cs, wmod_ref[0].astype(BF16), preferred_element_type=F32) + bmod_ref[...]

    dt_rows = jnp.concatenate([wint_ref[DT_OFF:DT_OFF + N_DH, :],
                               jnp.zeros((LANES - N_DH, wint_ref.shape[1]), F32)], axis=0)
    pin_ref[:, P_DT:P_DT + LANES] = dt_rows.T.astype(BF16)
    def move(dst, src):
        pin_ref[:, dst:dst + LANES] = wint_ref[src:src + LANES, :].T.astype(BF16)

    for dst, src, width in ((P_XBC, XBC_OFF, XBC_DIM), (P_Z, Z_OFF, SSD_WIDTH)):
        for j in range(0, width, LANES):
            move(dst + j, src + j)
    for j in range(CONV_WIDTH // LANES):
        for k in range(3):
            move(P_CONV + (3 * j + k) * LANES, k * CONV_WIDTH + j * LANES)
    pout_ref[...] = wout_ref[0].astype(BF16)
    pff1_ref[...] = wff1_ref[0].astype(BF16)
    pff2_ref[...] = wff2_ref[0].astype(BF16)


def _prep_call(cvec, w_mod, b_mod, w_in_t, w_out, w_ff1, w_ff2):
    steps = 8
    r1, r4, rm = D_MODEL // steps, D_FF // steps, 6 * D_MODEL // steps
    return pl.pallas_call(
        _prep_kernel,
        grid=(steps,),
        in_specs=[pl.BlockSpec((16, D_MODEL), lambda i: (0, 0)),
                  pl.BlockSpec((1, D_MODEL, rm), lambda i: (0, 0, i)),
                  pl.BlockSpec((1, rm), lambda i: (0, i)),
                  pl.BlockSpec((w_in_t.shape[0], r1), lambda i: (0, i)),
                  pl.BlockSpec((1, r1, D_MODEL), lambda i: (0, i, 0)),
                  pl.BlockSpec((1, r1, D_FF), lambda i: (0, i, 0)),
                  pl.BlockSpec((1, r4, D_MODEL), lambda i: (0, i, 0))],
        out_specs=[pl.BlockSpec((16, rm), lambda i: (0, i)),
                   pl.BlockSpec((r1, IN_PAD), lambda i: (i, 0)),
                   pl.BlockSpec((r1, D_MODEL), lambda i: (i, 0)),
                   pl.BlockSpec((r1, D_FF), lambda i: (i, 0)),
                   pl.BlockSpec((r4, D_MODEL), lambda i: (i, 0))],
        out_shape=[jax.ShapeDtypeStruct((16, 6 * D_MODEL), F32),
                   jax.ShapeDtypeStruct((D_MODEL, IN_PAD), BF16),
                   jax.ShapeDtypeStruct((D_MODEL, D_MODEL), BF16),
                   jax.ShapeDtypeStruct((D_MODEL, D_FF), BF16),
                   jax.ShapeDtypeStruct((D_FF, D_MODEL), BF16)],
        compiler_params=pltpu.CompilerParams(dimension_semantics=("arbitrary",),
                                             vmem_limit_bytes=VMEM_LIMIT),
        name="prep",
    )(cvec, w_mod, b_mod, w_in_t, w_out, w_ff1, w_ff2)


def _ctx_kernel(x_ref, mod_ref, lng_ref, lnb_ref, w_ref, scw_ref, scb_ref, dtb_ref, alog_ref,
                exw_ref, u_ref, h0_ref):
    m = mod_ref[0]
    sc = 1.0 + m[1:2]
    x = x_ref[...].reshape(CTX_ROWS * CTX_LEN, D_MODEL)
    u = _ln_hat(x) * (lng_ref[...] * sc) + (lnb_ref[...] * sc + m[0:1])
    proj = jnp.dot(u.astype(BF16), w_ref[...], preferred_element_type=F32)
    mprev, mnext = _edge_masks(CTX_ROWS * CTX_LEN, CTX_LEN)
    slabs = []
    for j in range((SSD_WIDTH + SSD_GN) // LANES):
        sl = slice(j * LANES, (j + 1) * LANES)
        pj = proj[:, P_XBC + j * LANES:P_XBC + (j + 1) * LANES]
        slabs.append(_silu(_conv3(pj, scw_ref[:, sl], mprev, mnext) + scb_ref[:, sl]))
    zero = jnp.zeros((N_DH, CTX_LEN), F32)
    for i in range(CTX_ROWS):
        tok = slice(i * CTX_LEN, (i + 1) * CTX_LEN)
        xs = jnp.concatenate([sl_[tok] for sl_ in slabs[:4]], axis=1)
        dt, adt = _dt_rows(proj[tok, P_DT:P_DT + LANES], dtb_ref[...], alog_ref[...])
        sc2 = _scan_mm(adt, u_ref)
        excl = jnp.where(_fwd_rows(), sc2[:, :CTX_LEN], sc2[:, CTX_LEN:])
        w = jnp.exp(excl) * dt
        table = jnp.concatenate([zero] * (COL_W // 16) + _split(w, 2) + [zero], axis=0)
        cols = table.T.astype(BF16)
        wx = jnp.dot(cols, exw_ref[...], preferred_element_type=F32)
        for d in range(N_DIRS):
            xw = (xs * wx[:, d * SSD_WIDTH:(d + 1) * SSD_WIDTH]).astype(BF16)
            for g in range(SSD_GROUPS):
                bt = slabs[4 + g][tok].T.astype(BF16)
                h0_ref[i, d, g] = jnp.dot(bt, xw[:, g * GROUP_COLS:(g + 1) * GROUP_COLS],
                                          preferred_element_type=F32)


def _ctx_call(ctx, mod_ctx, lng, lnb, w_in, scw, scb, dtb, alog, exw, u_ctx):
    return pl.pallas_call(
        _ctx_kernel,
        grid=(BATCH // CTX_ROWS,),
        in_specs=[pl.BlockSpec((CTX_ROWS, CTX_LEN, D_MODEL), lambda b: (b, 0, 0)),
                  _const_spec((1, 6, D_MODEL)), _const_spec((1, D_MODEL)), _const_spec((1, D_MODEL)),
                  _const_spec((D_MODEL, CTX_PAD)), _const_spec((3, XBC_DIM)), _const_spec((1, XBC_DIM)),
                  _const_spec((N_DH, 1)), _const_spec((N_DH, 1)), _const_spec((LANES, 2 * SSD_WIDTH)),
                  _const_spec((CTX_LEN, 2 * CTX_LEN))],
        out_specs=pl.BlockSpec((CTX_ROWS, N_DIRS, SSD_GROUPS, SSD_STATE, GROUP_COLS),
                               lambda b: (b, 0, 0, 0, 0)),
        out_shape=jax.ShapeDtypeStruct((BATCH, N_DIRS, SSD_GROUPS, SSD_STATE, GROUP_COLS), F32),
        compiler_params=pltpu.CompilerParams(dimension_semantics=("arbitrary",),
                                             vmem_limit_bytes=VMEM_LIMIT),
        name="ctx",
    )(ctx, mod_ctx, lng, lnb, w_in, scw, scb, dtb, alog, exw, u_ctx)


def _proj_kernel(x_ref, mod_ref, lng_ref, lnb_ref, w_ref, cw_ref, scw_ref, scb_ref, dtb_ref,
                 alog_ref, exe_ref, exw_ref, u_ref, h0_ref,
                 hln_ref, ycv_ref, zg_ref, xs_ref, bt_ref, cm_ref, rows_ref, cols_ref, colf_ref, sb_ref,
                 st_ref):
    @pl.when(pl.program_id(1) == 0)
    def _():
        st_ref[...] = h0_ref[0, 0]

    m = mod_ref[0]
    sc = 1.0 + m[1:2]
    scale, shift = lng_ref[...] * sc, lnb_ref[...] * sc + m[0:1]
    ubs, pas = [], []
    for r in range(2):
        rows = slice(r * (TM_PROJ // 2), (r + 1) * (TM_PROJ // 2))
        xhat = _ln_hat(x_ref[0, rows, :])
        hln_ref[0, rows, :] = xhat * lng_ref[...] + lnb_ref[...]
        ubs.append((xhat * scale + shift).astype(BF16))
        pas.append(jnp.dot(ubs[r], w_ref[:, :P_Z], preferred_element_type=F32))
    ub = jnp.concatenate(ubs, axis=0)
    pa = jnp.concatenate(pas, axis=0)
    mprev, mnext = _edge_masks(TM_PROJ, GRID_W)
    is_fwd = _fwd_rows()
    zero = jnp.zeros((N_DH, SSD_CHUNK), F32)
    chunks = [slice(c * SSD_CHUNK, (c + 1) * SSD_CHUNK) for c in range(NCH_PROJ)]

    def conv_slabs(p, first):
        for i in range(2):
            gb, gc, gh = (p[:, (3 * i + k) * LANES:(3 * i + k + 1) * LANES] for k in range(3))
            sl = slice((first + i) * LANES, (first + i + 1) * LANES)
            ycv_ref[0, :, sl] = (gb * _conv3(gc * gh, cw_ref[:, sl], mprev, mnext)).astype(BF16)

    dts = [_dt_rows(pa[tok, P_DT:P_DT + LANES], dtb_ref[...], alog_ref[...]) for tok in chunks]
    pz = jnp.dot(ub, w_ref[:, P_Z:P_CONV], preferred_element_type=F32)

    colss = []
    for c, (dt, adt) in enumerate(dts):
        sc3 = _scan_mm(adt, u_ref)
        cs = jnp.where(is_fwd, sc3[:, :SSD_CHUNK], sc3[:, SSD_CHUNK:2 * SSD_CHUNK])
        tot = sc3[:, 2 * SSD_CHUNK:]
        e1 = jnp.exp(cs)
        w = jnp.exp(tot - cs) * dt
        rows_ref[0, c] = jnp.concatenate([cs, dt], axis=0)
        table = jnp.concatenate([zero] * 3 + _split(e1, 2) + _split(w, 2) + [cs], axis=0).T
        colf_ref[0, c] = table
        colss.append(table.astype(BF16))
        cols_ref[0, c] = colss[c]

    pc1 = jnp.dot(ub, w_ref[:, P_CONV:P_CONV + 6 * LANES], preferred_element_type=F32)

    slabs = []
    for j in range(XBC_DIM // LANES):
        sl = slice(j * LANES, (j + 1) * LANES)
        pj = pa[:, P_XBC + j * LANES:P_XBC + (j + 1) * LANES]
        slabs.append(_silu(_conv3(pj, scw_ref[:, sl], mprev, mnext) + scb_ref[:, sl]))
    xs = jnp.concatenate(slabs[:4], axis=1)
    xs_ref[0] = xs.astype(BF16)
    cm_ref[0] = jnp.concatenate(slabs[6:8], axis=1).astype(BF16)
    zg_ref[0] = _silu(pz).astype(BF16)
    bts = []
    for c, tok in enumerate(chunks):
        bt = [slabs[4 + g][tok].T.astype(BF16) for g in range(SSD_GROUPS)]
        for g in range(SSD_GROUPS):
            bt_ref[0, c, g] = bt[g]
        bts.append(bt)
    wxbs = [jnp.dot(cols, exw_ref[:, SSD_WIDTH:], preferred_element_type=F32) for cols in colss]
    decs = [jnp.dot(cols[0:16], exe_ref[:, SSD_WIDTH:], preferred_element_type=F32)[0:1]
            for cols in colss]

    pc2 = jnp.dot(ub, w_ref[:, P_CONV + 6 * LANES:], preferred_element_type=F32)
    conv_slabs(pc1, 0)

    state = [st_ref[g] for g in range(SSD_GROUPS)]
    for c in reversed(range(NCH_PROJ)):
        xw = (xs[chunks[c]] * wxbs[c]).astype(BF16)
        for g in range(SSD_GROUPS):
            gs = slice(g * GROUP_COLS, (g + 1) * GROUP_COLS)
            sb_ref[0, c, g] = state[g].astype(BF16)
            local = jnp.dot(bts[c][g], xw[:, gs], preferred_element_type=F32)
            state[g] = state[g] * decs[c][:, gs] + local
    for g in range(SSD_GROUPS):
        st_ref[g] = state[g]
    conv_slabs(pc2, 2)


def _proj_call(x, mod_lat, lng, lnb, w_in, cw, scw, scb, dtb, alog, exe, exw, u_chunk, h0):
    rev = lambda b, t: (b, NT_PROJ - 1 - t, 0)
    rev4 = lambda b, t: (b, NT_PROJ - 1 - t, 0, 0)
    rev5 = lambda b, t: (b, NT_PROJ - 1 - t, 0, 0, 0)
    tok = lambda width: pl.BlockSpec((1, TM_PROJ, width), rev)
    out_shape = [
        jax.ShapeDtypeStruct((BATCH, SEQ, D_MODEL), F32),
        jax.ShapeDtypeStruct((BATCH, SEQ, CONV_WIDTH), BF16),
        jax.ShapeDtypeStruct((BATCH, SEQ, SSD_WIDTH), BF16),
        jax.ShapeDtypeStruct((BATCH, SEQ, SSD_WIDTH), BF16),
        jax.ShapeDtypeStruct((BATCH, NCHUNK, SSD_GROUPS, SSD_STATE, SSD_CHUNK), BF16),
        jax.ShapeDtypeStruct((BATCH, SEQ, SSD_GN), BF16),
        jax.ShapeDtypeStruct((BATCH, NCHUNK, 2 * N_DH, SSD_CHUNK), F32),
        jax.ShapeDtypeStruct((BATCH, NCHUNK, SSD_CHUNK, LANES), BF16),
        jax.ShapeDtypeStruct((BATCH, NCHUNK, SSD_CHUNK, LANES), F32),
        jax.ShapeDtypeStruct((BATCH, NCHUNK, SSD_GROUPS, SSD_STATE, GROUP_COLS), BF16),
    ]
    out_specs = [
        tok(D_MODEL), tok(CONV_WIDTH), tok(SSD_WIDTH), tok(SSD_WIDTH),
        pl.BlockSpec((1, NCH_PROJ, SSD_GROUPS, SSD_STATE, SSD_CHUNK), rev5),
        tok(SSD_GN),
        pl.BlockSpec((1, NCH_PROJ, 2 * N_DH, SSD_CHUNK), rev4),
        pl.BlockSpec((1, NCH_PROJ, SSD_CHUNK, LANES), rev4),
        pl.BlockSpec((1, NCH_PROJ, SSD_CHUNK, LANES), rev4),
        pl.BlockSpec((1, NCH_PROJ, SSD_GROUPS, SSD_STATE, GROUP_COLS), rev5),
    ]
    in_specs = [
        pl.BlockSpec((1, TM_PROJ, D_MODEL), rev),
        pl.BlockSpec((1, 6, D_MODEL), lambda b, t: (b, 0, 0)),
        _const_spec((1, D_MODEL)), _const_spec((1, D_MODEL)),
        _const_spec((D_MODEL, IN_PAD)),
        _const_spec((3, CONV_WIDTH)), _const_spec((3, XBC_DIM)), _const_spec((1, XBC_DIM)),
        _const_spec((N_DH, 1)), _const_spec((N_DH, 1)),
        _const_spec((LANES, 2 * SSD_WIDTH)), _const_spec((LANES, 2 * SSD_WIDTH)),
        _const_spec((SSD_CHUNK, 3 * SSD_CHUNK)),
        pl.BlockSpec((1, 1, SSD_GROUPS, SSD_STATE, GROUP_COLS), lambda b, t: (b, 1, 0, 0, 0)),
    ]
    return pl.pallas_call(
        _proj_kernel,
        grid=(BATCH, NT_PROJ),
        in_specs=in_specs,
        out_specs=out_specs,
        out_shape=out_shape,
        scratch_shapes=[pltpu.VMEM((SSD_GROUPS, SSD_STATE, GROUP_COLS), F32)],
        compiler_params=pltpu.CompilerParams(dimension_semantics=("arbitrary", "arbitrary"),
                                             vmem_limit_bytes=VMEM_LIMIT),
        name="proj",
    )(x, mod_lat, lng, lnb, w_in, cw, scw, scb, dtb, alog, exe, exw, u_chunk, h0)


def _out_kernel(hln_ref, mod_ref, ycv_ref, zg_ref, xs_ref, bt_ref, cm_ref, rows_ref,
                cols_ref, colf_ref, sb_ref, h0_ref, exw_ref, dx_ref, nw_ref, wout_ref,
                ln1g_ref, ln1b_ref, wff1_ref, wff2_ref, ln2g_ref, ln2b_ref,
                out_ref, st_ref, yn_ref, h1_ref, u2_ref, g2_ref, r2_ref, hid_ref):
    s = pl.program_id(0)
    wr = s % 2
    rd = (s + 1) % 2
    li = lax.broadcasted_iota(jnp.int32, (SSD_CHUNK, SSD_CHUNK), 0)
    si = lax.broadcasted_iota(jnp.int32, (SSD_CHUNK, SSD_CHUNK), 1)
    low = li >= si
    diag = li == si
    lo_half = si < SSD_HEADDIM

    def ssd_chunk(c, state, fill):
        tok = slice(c * SSD_CHUNK, (c + 1) * SSD_CHUNK)
        rows = rows_ref[0, c]
        cols = cols_ref[0, c]
        xs = xs_ref[0, tok, :]
        cm = cm_ref[0, tok, :]
        colf = colf_ref[0, c]
        bc = [jnp.broadcast_to(colf[:, COL_CSF + j:COL_CSF + j + 1], (SSD_CHUNK, LANES))
              for j in range(N_DH)]
        e1x = [jnp.concatenate(
            [jnp.exp(jnp.where(lo_half, bc[d * SSD_HEADS + 2 * k], bc[d * SSD_HEADS + 2 * k + 1]))
             for k in range(SSD_HEADS // 2)], axis=1) for d in range(N_DIRS)]
        wxf = jnp.dot(cols, exw_ref[:, :SSD_WIDTH], preferred_element_type=F32)
        dec = e1x[0][SSD_CHUNK - 1:SSD_CHUNK, :]
        gmat = [jnp.dot(cm[:, g * SSD_STATE:(g + 1) * SSD_STATE], bt_ref[0, c, g],
                        preferred_element_type=F32) for g in range(SSD_GROUPS)]
        fill[0]()

        ys = []
        for k in range(SSD_HEADS // 2):
            ms = []
            for h in (2 * k, 2 * k + 1):
                g = h // (SSD_HEADS // SSD_GROUPS)
                hb = SSD_HEADS + h
                arg = jnp.where(low,
                                bc[h] - rows[h:h + 1, :],
                                bc[hb] - rows[hb:hb + 1, :])
                dtf = rows[N_DH + h:N_DH + h + 1, :]
                dtb = rows[N_DH + hb:N_DH + hb + 1, :]
                fac = jnp.where(low, dtf, dtb) + jnp.where(diag, dtb, 0.0)
                ms.append((gmat[g] * jnp.exp(arg) * fac).astype(BF16))
            xp = xs[:, k * LANES:(k + 1) * LANES]
            rhs = jnp.concatenate([jnp.where(lo_half, xp, jnp.zeros_like(xp)),
                                   jnp.where(lo_half, jnp.zeros_like(xp), xp)], axis=0)
            ys.append(jnp.dot(jnp.concatenate(ms, axis=1), rhs, preferred_element_type=F32))
            fill[k + 1]()
        y = jnp.concatenate(ys, axis=1)

        yf, yb = [], []
        for g in range(SSD_GROUPS):
            cg = cm[:, g * SSD_STATE:(g + 1) * SSD_STATE]
            yf.append(jnp.dot(cg, state[g].astype(BF16), preferred_element_type=F32))
            yb.append(jnp.dot(cg, sb_ref[0, c, g], preferred_element_type=F32))
        y = (y + jnp.concatenate(yf, axis=1) * e1x[0]
             + jnp.concatenate(yb, axis=1) * e1x[1]
             + xs.astype(F32) * dx_ref[...])

        yg = y * zg_ref[0, tok, :].astype(F32)
        ms_ = jnp.mean(yg * yg, axis=-1, keepdims=True)
        yn_ref[tok, :CONV_WIDTH] = ycv_ref[0, tok, :]
        yn_ref[tok, CONV_WIDTH:] = (yg * lax.rsqrt(ms_ + RMS_EPS) * nw_ref[...]).astype(BF16)
        fill[5]()

        xw = (xs.astype(F32) * wxf).astype(BF16)
        new_state = []
        for g in range(SSD_GROUPS):
            gs = slice(g * GROUP_COLS, (g + 1) * GROUP_COLS)
            local = jnp.dot(bt_ref[0, c, g], xw[:, gs], preferred_element_type=F32)
            new_state.append(state[g] * dec[:, gs] + local)
        return new_state

    def mlp_up(j, k):
        blk = slice(j * FF_BLK + k * FF_SUB, j * FF_BLK + (k + 1) * FF_SUB)
        hid = jnp.maximum(jnp.dot(u2_ref[...], wff1_ref[:, blk], preferred_element_type=F32), 0.0)
        hid_ref[:, blk] = (hid * hid).astype(BF16)

    def mlp_down(j, acc):
        blk = slice(j * FF_BLK, (j + 1) * FF_BLK)
        part = jnp.dot(hid_ref[:, blk], wff2_ref[blk, :], preferred_element_type=F32)
        return part if acc is None else acc + part

    def ln2_rows(i):
        r = slice(i * SSD_CHUNK, (i + 1) * SSD_CHUNK)
        out_ref[0, r, :] = _ln_hat(r2_ref[r, :]) * ln2g_ref[...] + ln2b_ref[...]

    def step(mixer, mlp, norm):
        state, acc = None, [None]
        pieces = []
        if mlp:
            def down(j):
                acc[0] = mlp_down(j, acc[0])
            for j in range(N_SLAB):
                pieces += [(1, functools.partial(mlp_up, j, k)) for k in range(FF_BLK // FF_SUB)]
                if j < N_SLAB - DOWN_KEPT:
                    pieces.append((FF_BLK // FF_SUB, functools.partial(down, j)))
        n_slots = NCH * FILL_SLOTS
        per_slot = sum(cost for cost, _ in pieces) / n_slots
        slots, issued = [], 0.0
        for i in range(n_slots):
            mine = []
            while pieces and issued < (i + 1) * per_slot:
                cost, fn = pieces.pop(0)
                issued += cost
                mine.append(fn)
            slots.append(lambda mine=mine: [fn() for fn in mine])
        if mixer:
            @pl.when(jnp.minimum(s, N_TILES - 1) % NT == 0)
            def _():
                st_ref[...] = h0_ref[0, 0]
            state = [st_ref[g] for g in range(SSD_GROUPS)]
        for c in range(NCH):
            fill = slots[c * FILL_SLOTS:(c + 1) * FILL_SLOTS]
            if mixer:
                state = ssd_chunk(c, state, fill)
            else:
                for f in fill:
                    f()
            if norm:
                ln2_rows(c)
        if mixer:
            for g in range(SSD_GROUPS):
                st_ref[g] = state[g]
            mix = jnp.dot(yn_ref[...], wout_ref[...], preferred_element_type=F32)
            m = mod_ref[0]
            g2_ref[wr] = m[5:6]
        for i in range(DOWN_KEPT):
            if mlp:
                down(N_SLAB - DOWN_KEPT + i)
            if mixer:
                for j in range(i * NCH // DOWN_KEPT, (i + 1) * NCH // DOWN_KEPT):
                    r = slice(j * SSD_CHUNK, (j + 1) * SSD_CHUNK)
                    h1 = _ln_hat(ALPHA * hln_ref[0, r, :] + m[2:3] * mix[r]) * ln1g_ref[...] + ln1b_ref[...]
                    h1_ref[wr, r, :] = h1
                    u2_ref[r, :] = (h1 * (1.0 + m[4:5]) + m[3:4]).astype(BF16)
        if mlp:
            r2_ref[...] = ALPHA * h1_ref[rd] + g2_ref[rd] * acc[0]

    @pl.when(s == 0)
    def _():
        r2_ref[...] = jnp.zeros((TM, D_MODEL), F32)
        step(True, False, False)

    @pl.when(jnp.logical_and(s >= 1, s <= N_TILES))
    def _():
        step(True, True, True)

    @pl.when(s == N_TILES + 1)
    def _():
        step(False, False, True)


def _out_call(hln, mod_lat, ycv, zg, xs, bt, cm, rows, cols, colf, sb, h0, exw, dx, nw,
              wout, ln1g, ln1b, wff1, wff2, ln2g, ln2b):
    cur = lambda s: jnp.minimum(s, N_TILES - 1)
    fwd = lambda s: (cur(s) // NT, cur(s) % NT, 0)
    fwd4 = lambda s: (cur(s) // NT, cur(s) % NT, 0, 0)
    fwd5 = lambda s: (cur(s) // NT, cur(s) % NT, 0, 0, 0)
    done = lambda s: (jnp.maximum(s - 2, 0) // NT, jnp.maximum(s - 2, 0) % NT, 0)
    tok = lambda width: pl.BlockSpec((1, TM, width), fwd)
    in_specs = [
        tok(D_MODEL),
        pl.BlockSpec((1, 6, D_MODEL), lambda s: (cur(s) // NT, 0, 0)),
        tok(CONV_WIDTH), tok(SSD_WIDTH), tok(SSD_WIDTH),
        pl.BlockSpec((1, NCH, SSD_GROUPS, SSD_STATE, SSD_CHUNK), fwd5),
        tok(SSD_GN),
        pl.BlockSpec((1, NCH, 2 * N_DH, SSD_CHUNK), fwd4),
        pl.BlockSpec((1, NCH, SSD_CHUNK, LANES), fwd4),
        pl.BlockSpec((1, NCH, SSD_CHUNK, LANES), fwd4),
        pl.BlockSpec((1, NCH, SSD_GROUPS, SSD_STATE, GROUP_COLS), fwd5),
        pl.BlockSpec((1, 1, SSD_GROUPS, SSD_STATE, GROUP_COLS), lambda s: (cur(s) // NT, 0, 0, 0, 0)),
        _const_spec((LANES, 2 * SSD_WIDTH)),
        _const_spec((1, SSD_WIDTH)), _const_spec((1, SSD_WIDTH)),
        _const_spec((D_MODEL, D_MODEL)),
        _const_spec((1, D_MODEL)), _const_spec((1, D_MODEL)),
        _const_spec((D_MODEL, D_FF)), _const_spec((D_FF, D_MODEL)),
        _const_spec((1, D_MODEL)), _const_spec((1, D_MODEL)),
    ]
    return pl.pallas_call(
        _out_kernel,
        grid=(N_TILES + 2,),
        in_specs=in_specs,
        out_specs=pl.BlockSpec((1, TM, D_MODEL), done),
        out_shape=jax.ShapeDtypeStruct((BATCH, SEQ, D_MODEL), F32),
        scratch_shapes=[pltpu.VMEM((SSD_GROUPS, SSD_STATE, GROUP_COLS), F32),
                        pltpu.VMEM((TM, D_MODEL), BF16),
                        pltpu.VMEM((2, TM, D_MODEL), F32),
                        pltpu.VMEM((TM, D_MODEL), BF16),
                        pltpu.VMEM((2, 1, D_MODEL), F32),
                        pltpu.VMEM((TM, D_MODEL), F32),
                        pltpu.VMEM((TM, D_FF), BF16)],
        compiler_params=pltpu.CompilerParams(dimension_semantics=("arbitrary",),
                                             vmem_limit_bytes=VMEM_LIMIT),
        name="out",
    )(hln, mod_lat, ycv, zg, xs, bt, cm, rows, cols, colf, sb, h0, exw, dx, nw,
      wout, ln1g, ln1b, wff1, wff2, ln2g, ln2b)


def kernel(x, c, ctx, c_ctx, ln_in_g, ln_in_b, w_mod, b_mod, w_in, conv_w, ssd_conv_w, ssd_conv_b,
           dt_bias, a_log, ssd_d, ssd_norm_w, w_out, ln1_g, ln1_b, w_ff1, w_ff2, ln2_g, ln2_b):
    row = lambda v: v.reshape(1, -1).astype(F32)
    cvec = jnp.concatenate([c, c_ctx[None, :], jnp.zeros((16 - BATCH - 1, D_MODEL), F32)], axis=0)
    mod, w_in_p, w_out_p, w_ff1_p, w_ff2_p = _prep_call(cvec, w_mod, b_mod, w_in[0].T, w_out, w_ff1, w_ff2)
    mod_lat = mod[:BATCH].reshape(BATCH, 6, D_MODEL)
    mod_ctx = mod[BATCH:BATCH + 1].reshape(1, 6, D_MODEL)

    lng, lnb = row(ln_in_g), row(ln_in_b)
    scw, scb = ssd_conv_w[0], row(ssd_conv_b[0])
    dtb = dt_bias[0].reshape(N_DH, 1)
    alog = a_log[0].reshape(N_DH, 1)
    exe = jnp.asarray(_EXE, BF16)
    exw = jnp.asarray(_EXW, BF16)

    h0 = _ctx_call(ctx, mod_ctx, lng, lnb, w_in_p, scw, scb, dtb, alog, exw, jnp.asarray(_U_CTX, BF16))
    hln, ycv, zg, xs, bt, cm, rows, cols, colf, sb = _proj_call(
        x, mod_lat, lng, lnb, w_in_p, conv_w[0], scw, scb, dtb, alog, exe, exw,
        jnp.asarray(_U_CHUNK, BF16), h0)
    dx = jnp.repeat(ssd_d[0], SSD_HEADDIM).reshape(1, SSD_WIDTH)
    return _out_call(hln, mod_lat, ycv, zg, xs, bt, cm, rows, cols, colf, sb, h0, exw,
                     dx, row(ssd_norm_w[0]), w_out_p, row(ln1_g[0]), row(ln1_b[0]),
                     w_ff1_p, w_ff2_p, row(ln2_g[0]), row(ln2_b[0]))
```

```python
import functools

import jax
import jax.numpy as jnp
import numpy as np
from jax import lax
from jax.experimental import pallas as pl
from jax.experimental.pallas import tpu as pltpu

F32 = jnp.float32
BF16 = jnp.bfloat16

D_MODEL = 1024
BATCH = 8
SEQ = 2048
CTX_LEN = 256
GRID_W = 64
CONV_WIDTH = 512
SSD_WIDTH = 512
SSD_HEADDIM = 64
SSD_HEADS = 8
SSD_GROUPS = 2
SSD_STATE = 128
SSD_CHUNK = 128
N_DIRS = 2
D_FF = 4 * D_MODEL
LN_EPS = 1e-5
RMS_EPS = 1e-5
SSD_GN = SSD_GROUPS * SSD_STATE
XBC_DIM = SSD_WIDTH + 2 * SSD_GN
Z_OFF = 3 * CONV_WIDTH
XBC_OFF = Z_OFF + SSD_WIDTH
DT_OFF = XBC_OFF + XBC_DIM
N_DH = N_DIRS * SSD_HEADS
LANES = 128
P_DT = 0
P_XBC = LANES
P_Z = P_XBC + XBC_DIM
P_CONV = P_Z + SSD_WIDTH
IN_PAD = P_CONV + 3 * CONV_WIDTH
CTX_PAD = P_XBC + SSD_WIDTH + SSD_GN
GROUP_COLS = (SSD_HEADS // SSD_GROUPS) * SSD_HEADDIM
ALPHA = 2.0 ** 0.25

TM = 512
CTX_ROWS = 2
NCH = TM // SSD_CHUNK
NT = SEQ // TM
NCHUNK = SEQ // SSD_CHUNK
N_TILES = BATCH * NT
TM_PROJ = 1024
NCH_PROJ = TM_PROJ // SSD_CHUNK
NT_PROJ = SEQ // TM_PROJ
FF_BLK = 1024
FF_SUB = 256
N_SLAB = D_FF // FF_BLK
FILL_SLOTS = 6
DOWN_KEPT = 2
VMEM_LIMIT = 58 * 1024 * 1024

COL_E1 = 48
COL_W = 80
COL_CSF = 112


def _expansion(col0, pieces, width):
    m = np.zeros((LANES, N_DH * width), np.float32)
    for t in range(pieces):
        for j in range(N_DH):
            m[col0 + 16 * t + j, j * width:(j + 1) * width] = 1.0
    return m


_EXE = _expansion(COL_E1, 2, SSD_HEADDIM)
_EXW = _expansion(COL_W, 2, SSD_HEADDIM)


def _ln_hat(x):
    mu = jnp.mean(x, axis=-1, keepdims=True)
    xc = x - mu
    var = jnp.mean(xc * xc, axis=-1, keepdims=True)
    return xc * lax.rsqrt(var + LN_EPS)


def _silu(x):
    return x / (1.0 + jnp.exp(-x))


def _softplus(x):
    return jnp.maximum(x, 0.0) + jnp.log1p(jnp.exp(-jnp.abs(x)))


def _edge_masks(rows, period):
    pos = lax.broadcasted_iota(jnp.int32, (rows, LANES), 0) % period
    return (pos != 0).astype(F32), (pos != period - 1).astype(F32)


def _conv3(t, w, mprev, mnext):
    rows = t.shape[0]
    prev = pltpu.roll(t, 1, 0) * mprev
    nxt = pltpu.roll(t, rows - 1, 0) * mnext
    return prev * w[0:1, :] + t * w[1:2, :] + nxt * w[2:3, :]


def _split(v, pieces):
    out = []
    for _ in range(pieces - 1):
        p = v.astype(BF16).astype(F32)
        out.append(p)
        v = v - p
    out.append(v.astype(BF16).astype(F32))
    return out


def _tri(length, op):
    i = np.arange(length)
    return op(i[:, None], i[None, :]).astype(np.float32)


_U_CHUNK = np.concatenate([_tri(SSD_CHUNK, np.less_equal), _tri(SSD_CHUNK, np.greater_equal),
                           np.ones((SSD_CHUNK, SSD_CHUNK), np.float32)], axis=1)
_U_CTX = np.concatenate([_tri(CTX_LEN, np.greater), _tri(CTX_LEN, np.less)], axis=1)


def _scan_mm(v, u_ref):
    pieces = jnp.concatenate(_split(v, 3), axis=0).astype(BF16)
    o = jnp.dot(pieces, u_ref[...], preferred_element_type=F32)
    return o[0:N_DH] + o[N_DH:2 * N_DH] + o[2 * N_DH:3 * N_DH]


def _dt_rows(raw, dtb, a_log):
    r = raw.T[0:N_DH, :] + dtb
    dt = _softplus(r)
    return dt, dt * (-jnp.exp(a_log))


def _fwd_rows():
    return lax.broadcasted_iota(jnp.int32, (N_DH, 1), 0) < SSD_HEADS


def _mod_vectors(mod_ref, row):
    r = mod_ref[pl.ds(row, 1), :]
    return [r[:, k * D_MODEL:(k + 1) * D_MODEL] for k in range(6)]


def _const_spec(shape):
    nd = len(shape)
    return pl.BlockSpec(shape, lambda *_: (0,) * nd, pipeline_mode=pl.Buffered(1))


def _prep_kernel(c_ref, wmod_ref, bmod_ref, wint_ref, wout_ref, wff1_ref, wff2_ref,
                 mod_ref, pin_ref, pout_ref, pff1_ref, pff2_ref):
    cs = _silu(c_ref[...]).astype(BF16)
    mod_ref[...] = jnp.dot(cs, wmod_ref[0].astype(BF16), preferred_element_type=F32) + bmod_ref[...]

    dt_rows = jnp.concatenate([wint_ref[DT_OFF:DT_OFF + N_DH, :],
                               jnp.zeros((LANES - N_DH, wint_ref.shape[1]), F32)], axis=0)
    pin_ref[:, P_DT:P_DT + LANES] = dt_rows.T.astype(BF16)
    def move(dst, src):
        pin_ref[:, dst:dst + LANES] = wint_ref[src:src + LANES, :].T.astype(BF16)

    for dst, src, width in ((P_XBC, XBC_OFF, XBC_DIM), (P_Z, Z_OFF, SSD_WIDTH)):
        for j in range(0, width, LANES):
            move(dst + j, src + j)
    for j in range(CONV_WIDTH // LANES):
        for k in range(3):
            move(P_CONV + (3 * j + k) * LANES, k * CONV_WIDTH + j * LANES)
    pout_ref[...] = wout_ref[0].astype(BF16)
    pff1_ref[...] = wff1_ref[0].astype(BF16)
    pff2_ref[...] = wff2_ref[0].astype(BF16)


def _prep_call(cvec, w_mod, b_mod, w_in_t, w_out, w_ff1, w_ff2):
    steps = 8
    r1, r4, rm = D_MODEL // steps, D_FF // steps, 6 * D_MODEL // steps
    return pl.pallas_call(
        _prep_kernel,
        grid=(steps,),
        in_specs=[pl.BlockSpec((16, D_MODEL), lambda i: (0, 0)),
                  pl.BlockSpec((1, D_MODEL, rm), lambda i: (0, 0, i)),
                  pl.BlockSpec((1, rm), lambda i: (0, i)),
                  pl.BlockSpec((w_in_t.shape[0], r1), lambda i: (0, i)),
                  pl.BlockSpec((1, r1, D_MODEL), lambda i: (0, i, 0)),
                  pl.BlockSpec((1, r1, D_FF), lambda i: (0, i, 0)),
                  pl.BlockSpec((1, r4, D_MODEL), lambda i: (0, i, 0))],
        out_specs=[pl.BlockSpec((16, rm), lambda i: (0, i)),
                   pl.BlockSpec((r1, IN_PAD), lambda i: (i, 0)),
                   pl.BlockSpec((r1, D_MODEL), lambda i: (i, 0)),
                   pl.BlockSpec((r1, D_FF), lambda i: (i, 0)),
                   pl.BlockSpec((r4, D_MODEL), lambda i: (i, 0))],
        out_shape=[jax.ShapeDtypeStruct((16, 6 * D_MODEL), F32),
                   jax.ShapeDtypeStruct((D_MODEL, IN_PAD), BF16),
                   jax.ShapeDtypeStruct((D_MODEL, D_MODEL), BF16),
                   jax.ShapeDtypeStruct((D_MODEL, D_FF), BF16),
                   jax.ShapeDtypeStruct((D_FF, D_MODEL), BF16)],
        compiler_params=pltpu.CompilerParams(dimension_semantics=("arbitrary",),
                                             vmem_limit_bytes=VMEM_LIMIT),
        name="prep",
    )(cvec, w_mod, b_mod, w_in_t, w_out, w_ff1, w_ff2)


def _ctx_kernel(x_ref, mod_ref, lng_ref, lnb_ref, w_ref, scw_ref, scb_ref, dtb_ref, alog_ref,
                exw_ref, u_ref, h0_ref):
    m = _mod_vectors(mod_ref, BATCH)
    sc = 1.0 + m[1]
    x = x_ref[...].reshape(CTX_ROWS * CTX_LEN, D_MODEL)
    u = _ln_hat(x) * (lng_ref[...] * sc) + (lnb_ref[...] * sc + m[0])
    proj = jnp.dot(u.astype(BF16), w_ref[...], preferred_element_type=F32)
    mprev, mnext = _edge_masks(CTX_ROWS * CTX_LEN, CTX_LEN)
    slabs = []
    for j in range((SSD_WIDTH + SSD_GN) // LANES):
        sl = slice(j * LANES, (j + 1) * LANES)
        pj = proj[:, P_XBC + j * LANES:P_XBC + (j + 1) * LANES]
        slabs.append(_silu(_conv3(pj, scw_ref[0, :, sl], mprev, mnext) + scb_ref[:, sl]))
    zero = jnp.zeros((N_DH, CTX_LEN), F32)
    for i in range(CTX_ROWS):
        tok = slice(i * CTX_LEN, (i + 1) * CTX_LEN)
        xs = jnp.concatenate([sl_[tok] for sl_ in slabs[:4]], axis=1)
        dt, adt = _dt_rows(proj[tok, P_DT:P_DT + LANES], dtb_ref[...], alog_ref[...])
        sc2 = _scan_mm(adt, u_ref)
        excl = jnp.where(_fwd_rows(), sc2[:, :CTX_LEN], sc2[:, CTX_LEN:])
        w = jnp.exp(excl) * dt
        table = jnp.concatenate([zero] * (COL_W // 16) + _split(w, 2) + [zero], axis=0)
        cols = table.T.astype(BF16)
        wx = jnp.dot(cols, exw_ref[...], preferred_element_type=F32)
        for d in range(N_DIRS):
            xw = (xs * wx[:, d * SSD_WIDTH:(d + 1) * SSD_WIDTH]).astype(BF16)
            for g in range(SSD_GROUPS):
                bt = slabs[4 + g][tok].T.astype(BF16)
                h0_ref[i, d, g] = jnp.dot(bt, xw[:, g * GROUP_COLS:(g + 1) * GROUP_COLS],
                                          preferred_element_type=F32)


def _ctx_call(ctx, mod, lng, lnb, w_in, scw, scb, dtb, alog, exw, u_ctx):
    return pl.pallas_call(
        _ctx_kernel,
        grid=(BATCH // CTX_ROWS,),
        in_specs=[pl.BlockSpec((CTX_ROWS, CTX_LEN, D_MODEL), lambda b: (b, 0, 0)),
                  _const_spec((16, 6 * D_MODEL)), _const_spec((1, D_MODEL)), _const_spec((1, D_MODEL)),
                  _const_spec((D_MODEL, CTX_PAD)), _const_spec((1, 3, XBC_DIM)), _const_spec((1, XBC_DIM)),
                  _const_spec((N_DH, 1)), _const_spec((N_DH, 1)), _const_spec((LANES, 2 * SSD_WIDTH)),
                  _const_spec((CTX_LEN, 2 * CTX_LEN))],
        out_specs=pl.BlockSpec((CTX_ROWS, N_DIRS, SSD_GROUPS, SSD_STATE, GROUP_COLS),
                               lambda b: (b, 0, 0, 0, 0)),
        out_shape=jax.ShapeDtypeStruct((BATCH, N_DIRS, SSD_GROUPS, SSD_STATE, GROUP_COLS), F32),
        compiler_params=pltpu.CompilerParams(dimension_semantics=("arbitrary",),
                                             vmem_limit_bytes=VMEM_LIMIT),
        name="ctx",
    )(ctx, mod, lng, lnb, w_in, scw, scb, dtb, alog, exw, u_ctx)


def _proj_kernel(x_ref, mod_ref, lng_ref, lnb_ref, w_ref, cw_ref, scw_ref, scb_ref, dtb_ref,
                 alog_ref, exe_ref, exw_ref, u_ref, h0_ref,
                 hln_ref, ycv_ref, zg_ref, xs_ref, bt_ref, cm_ref, rows_ref, cols_ref, colf_ref, sb_ref,
                 st_ref):
    @pl.when(pl.program_id(1) == 0)
    def _():
        st_ref[...] = h0_ref[0, 0]

    m = _mod_vectors(mod_ref, pl.program_id(0))
    sc = 1.0 + m[1]
    scale, shift = lng_ref[...] * sc, lnb_ref[...] * sc + m[0]
    ubs, pas = [], []
    for r in range(2):
        rows = slice(r * (TM_PROJ // 2), (r + 1) * (TM_PROJ // 2))
        xhat = _ln_hat(x_ref[0, rows, :])
        hln_ref[0, rows, :] = xhat * lng_ref[...] + lnb_ref[...]
        ubs.append((xhat * scale + shift).astype(BF16))
        pas.append(jnp.dot(ubs[r], w_ref[:, :P_Z], preferred_element_type=F32))
    ub = jnp.concatenate(ubs, axis=0)
    pa = jnp.concatenate(pas, axis=0)
    mprev, mnext = _edge_masks(TM_PROJ, GRID_W)
    is_fwd = _fwd_rows()
    zero = jnp.zeros((N_DH, SSD_CHUNK), F32)
    chunks = [slice(c * SSD_CHUNK, (c + 1) * SSD_CHUNK) for c in range(NCH_PROJ)]

    def conv_slabs(p, first):
        for i in range(2):
            gb, gc, gh = (p[:, (3 * i + k) * LANES:(3 * i + k + 1) * LANES] for k in range(3))
            sl = slice((first + i) * LANES, (first + i + 1) * LANES)
            ycv_ref[0, :, sl] = (gb * _conv3(gc * gh, cw_ref[0, :, sl], mprev, mnext)).astype(BF16)

    dts = [_dt_rows(pa[tok, P_DT:P_DT + LANES], dtb_ref[...], alog_ref[...]) for tok in chunks]
    pz = jnp.dot(ub, w_ref[:, P_Z:P_CONV], preferred_element_type=F32)

    colss = []
    for c, (dt, adt) in enumerate(dts):
        sc3 = _scan_mm(adt, u_ref)
        cs = jnp.where(is_fwd, sc3[:, :SSD_CHUNK], sc3[:, SSD_CHUNK:2 * SSD_CHUNK])
        tot = sc3[:, 2 * SSD_CHUNK:]
        e1 = jnp.exp(cs)
        w = jnp.exp(tot - cs) * dt
        rows_ref[0, c] = jnp.concatenate([cs, dt], axis=0)
        table = jnp.concatenate([zero] * 3 + _split(e1, 2) + _split(w, 2) + [cs], axis=0).T
        colf_ref[0, c] = table
        colss.append(table.astype(BF16))
        cols_ref[0, c] = colss[c]

    pc1 = jnp.dot(ub, w_ref[:, P_CONV:P_CONV + 6 * LANES], preferred_element_type=F32)

    slabs = []
    for j in range(XBC_DIM // LANES):
        sl = slice(j * LANES, (j + 1) * LANES)
        pj = pa[:, P_XBC + j * LANES:P_XBC + (j + 1) * LANES]
        slabs.append(_silu(_conv3(pj, scw_ref[0, :, sl], mprev, mnext) + scb_ref[:, sl]))
    xs = jnp.concatenate(slabs[:4], axis=1)
    xs_ref[0] = xs.astype(BF16)
    cm_ref[0] = jnp.concatenate(slabs[6:8], axis=1).astype(BF16)
    zg_ref[0] = _silu(pz).astype(BF16)
    bts = []
    for c, tok in enumerate(chunks):
        bt = [slabs[4 + g][tok].T.astype(BF16) for g in range(SSD_GROUPS)]
        for g in range(SSD_GROUPS):
            bt_ref[0, c, g] = bt[g]
        bts.append(bt)
    wxbs = [jnp.dot(cols, exw_ref[:, SSD_WIDTH:], preferred_element_type=F32) for cols in colss]
    decs = [jnp.dot(cols[0:16], exe_ref[:, SSD_WIDTH:], preferred_element_type=F32)[0:1]
            for cols in colss]

    pc2 = jnp.dot(ub, w_ref[:, P_CONV + 6 * LANES:], preferred_element_type=F32)
    conv_slabs(pc1, 0)

    state = [st_ref[g] for g in range(SSD_GROUPS)]
    for c in reversed(range(NCH_PROJ)):
        xw = (xs[chunks[c]] * wxbs[c]).astype(BF16)
        for g in range(SSD_GROUPS):
            gs = slice(g * GROUP_COLS, (g + 1) * GROUP_COLS)
            sb_ref[0, c, g] = state[g].astype(BF16)
            local = jnp.dot(bts[c][g], xw[:, gs], preferred_element_type=F32)
            state[g] = state[g] * decs[c][:, gs] + local
    for g in range(SSD_GROUPS):
        st_ref[g] = state[g]
    conv_slabs(pc2, 2)


def _proj_call(x, mod, lng, lnb, w_in, cw, scw, scb, dtb, alog, exe, exw, u_chunk, h0):
    rev = lambda b, t: (b, NT_PROJ - 1 - t, 0)
    rev4 = lambda b, t: (b, NT_PROJ - 1 - t, 0, 0)
    rev5 = lambda b, t: (b, NT_PROJ - 1 - t, 0, 0, 0)
    tok = lambda width: pl.BlockSpec((1, TM_PROJ, width), rev)
    out_shape = [
        jax.ShapeDtypeStruct((BATCH, SEQ, D_MODEL), F32),
        jax.ShapeDtypeStruct((BATCH, SEQ, CONV_WIDTH), BF16),
        jax.ShapeDtypeStruct((BATCH, SEQ, SSD_WIDTH), BF16),
        jax.ShapeDtypeStruct((BATCH, SEQ, SSD_WIDTH), BF16),
        jax.ShapeDtypeStruct((BATCH, NCHUNK, SSD_GROUPS, SSD_STATE, SSD_CHUNK), BF16),
        jax.ShapeDtypeStruct((BATCH, SEQ, SSD_GN), BF16),
        jax.ShapeDtypeStruct((BATCH, NCHUNK, 2 * N_DH, SSD_CHUNK), F32),
        jax.ShapeDtypeStruct((BATCH, NCHUNK, SSD_CHUNK, LANES), BF16),
        jax.ShapeDtypeStruct((BATCH, NCHUNK, SSD_CHUNK, LANES), F32),
        jax.ShapeDtypeStruct((BATCH, NCHUNK, SSD_GROUPS, SSD_STATE, GROUP_COLS), BF16),
    ]
    out_specs = [
        tok(D_MODEL), tok(CONV_WIDTH), tok(SSD_WIDTH), tok(SSD_WIDTH),
        pl.BlockSpec((1, NCH_PROJ, SSD_GROUPS, SSD_STATE, SSD_CHUNK), rev5),
        tok(SSD_GN),
        pl.BlockSpec((1, NCH_PROJ, 2 * N_DH, SSD_CHUNK), rev4),
        pl.BlockSpec((1, NCH_PROJ, SSD_CHUNK, LANES), rev4),
        pl.BlockSpec((1, NCH_PROJ, SSD_CHUNK, LANES), rev4),
        pl.BlockSpec((1, NCH_PROJ, SSD_GROUPS, SSD_STATE, GROUP_COLS), rev5),
    ]
    in_specs = [
        pl.BlockSpec((1, TM_PROJ, D_MODEL), rev),
        _const_spec((16, 6 * D_MODEL)),
        _const_spec((1, D_MODEL)), _const_spec((1, D_MODEL)),
        _const_spec((D_MODEL, IN_PAD)),
        _const_spec((1, 3, CONV_WIDTH)), _const_spec((1, 3, XBC_DIM)), _const_spec((1, XBC_DIM)),
        _const_spec((N_DH, 1)), _const_spec((N_DH, 1)),
        _const_spec((LANES, 2 * SSD_WIDTH)), _const_spec((LANES, 2 * SSD_WIDTH)),
        _const_spec((SSD_CHUNK, 3 * SSD_CHUNK)),
        pl.BlockSpec((1, 1, SSD_GROUPS, SSD_STATE, GROUP_COLS), lambda b, t: (b, 1, 0, 0, 0)),
    ]
    return pl.pallas_call(
        _proj_kernel,
        grid=(BATCH, NT_PROJ),
        in_specs=in_specs,
        out_specs=out_specs,
        out_shape=out_shape,
        scratch_shapes=[pltpu.VMEM((SSD_GROUPS, SSD_STATE, GROUP_COLS), F32)],
        compiler_params=pltpu.CompilerParams(dimension_semantics=("arbitrary", "arbitrary"),
                                             vmem_limit_bytes=VMEM_LIMIT),
        name="proj",
    )(x, mod, lng, lnb, w_in, cw, scw, scb, dtb, alog, exe, exw, u_chunk, h0)


def _out_kernel(hln_ref, mod_ref, ycv_ref, zg_ref, xs_ref, bt_ref, cm_ref, rows_ref,
                cols_ref, colf_ref, sb_ref, h0_ref, exw_ref, dx_ref, nw_ref, wout_ref,
                ln1g_ref, ln1b_ref, wff1_ref, wff2_ref, ln2g_ref, ln2b_ref,
                out_ref, st_ref, yn_ref, h1_ref, u2_ref, g2_ref, r2_ref, hid_ref):
    s = pl.program_id(0)
    wr = s % 2
    rd = (s + 1) % 2
    li = lax.broadcasted_iota(jnp.int32, (SSD_CHUNK, SSD_CHUNK), 0)
    si = lax.broadcasted_iota(jnp.int32, (SSD_CHUNK, SSD_CHUNK), 1)
    low = li >= si
    diag = li == si
    lo_half = si < SSD_HEADDIM

    def ssd_chunk(c, state, fill):
        tok = slice(c * SSD_CHUNK, (c + 1) * SSD_CHUNK)
        rows = rows_ref[0, c]
        cols = cols_ref[0, c]
        xs = xs_ref[0, tok, :]
        cm = cm_ref[0, tok, :]
        colf = colf_ref[0, c]
        bc = [jnp.broadcast_to(colf[:, COL_CSF + j:COL_CSF + j + 1], (SSD_CHUNK, LANES))
              for j in range(N_DH)]
        e1x = [jnp.concatenate(
            [jnp.exp(jnp.where(lo_half, bc[d * SSD_HEADS + 2 * k], bc[d * SSD_HEADS + 2 * k + 1]))
             for k in range(SSD_HEADS // 2)], axis=1) for d in range(N_DIRS)]
        wxf = jnp.dot(cols, exw_ref[:, :SSD_WIDTH], preferred_element_type=F32)
        dec = e1x[0][SSD_CHUNK - 1:SSD_CHUNK, :]
        gmat = [jnp.dot(cm[:, g * SSD_STATE:(g + 1) * SSD_STATE], bt_ref[0, c, g],
                        preferred_element_type=F32) for g in range(SSD_GROUPS)]
        fill[0]()

        ys = []
        for k in range(SSD_HEADS // 2):
            ms = []
            for h in (2 * k, 2 * k + 1):
                g = h // (SSD_HEADS // SSD_GROUPS)
                hb = SSD_HEADS + h
                arg = jnp.where(low,
                                bc[h] - rows[h:h + 1, :],
                                bc[hb] - rows[hb:hb + 1, :])
                dtf = rows[N_DH + h:N_DH + h + 1, :]
                dtb = rows[N_DH + hb:N_DH + hb + 1, :]
                fac = jnp.where(low, dtf, dtb) + jnp.where(diag, dtb, 0.0)
                ms.append((gmat[g] * jnp.exp(arg) * fac).astype(BF16))
            xp = xs[:, k * LANES:(k + 1) * LANES]
            rhs = jnp.concatenate([jnp.where(lo_half, xp, jnp.zeros_like(xp)),
                                   jnp.where(lo_half, jnp.zeros_like(xp), xp)], axis=0)
            ys.append(jnp.dot(jnp.concatenate(ms, axis=1), rhs, preferred_element_type=F32))
            fill[k + 1]()
        y = jnp.concatenate(ys, axis=1)

        yf, yb = [], []
        for g in range(SSD_GROUPS):
            cg = cm[:, g * SSD_STATE:(g + 1) * SSD_STATE]
            yf.append(jnp.dot(cg, state[g].astype(BF16), preferred_element_type=F32))
            yb.append(jnp.dot(cg, sb_ref[0, c, g], preferred_element_type=F32))
        y = (y + jnp.concatenate(yf, axis=1) * e1x[0]
             + jnp.concatenate(yb, axis=1) * e1x[1]
             + xs.astype(F32) * dx_ref[...])

        yg = y * zg_ref[0, tok, :].astype(F32)
        ms_ = jnp.mean(yg * yg, axis=-1, keepdims=True)
        yn_ref[tok, :CONV_WIDTH] = ycv_ref[0, tok, :]
        yn_ref[tok, CONV_WIDTH:] = (yg * lax.rsqrt(ms_ + RMS_EPS) * nw_ref[...]).astype(BF16)
        fill[5]()

        xw = (xs.astype(F32) * wxf).astype(BF16)
        new_state = []
        for g in range(SSD_GROUPS):
            gs = slice(g * GROUP_COLS, (g + 1) * GROUP_COLS)
            local = jnp.dot(bt_ref[0, c, g], xw[:, gs], preferred_element_type=F32)
            new_state.append(state[g] * dec[:, gs] + local)
        return new_state

    def mlp_up(j, k):
        blk = slice(j * FF_BLK + k * FF_SUB, j * FF_BLK + (k + 1) * FF_SUB)
        hid = jnp.maximum(jnp.dot(u2_ref[...], wff1_ref[:, blk], preferred_element_type=F32), 0.0)
        hid_ref[:, blk] = (hid * hid).astype(BF16)

    def mlp_down(j, acc):
        blk = slice(j * FF_BLK, (j + 1) * FF_BLK)
        part = jnp.dot(hid_ref[:, blk], wff2_ref[blk, :], preferred_element_type=F32)
        return part if acc is None else acc + part

    def ln2_rows(i):
        r = slice(i * SSD_CHUNK, (i + 1) * SSD_CHUNK)
        out_ref[0, r, :] = _ln_hat(r2_ref[r, :]) * ln2g_ref[...] + ln2b_ref[...]

    def step(mixer, mlp, norm):
        state, acc = None, [None]
        pieces = []
        if mlp:
            def down(j):
                acc[0] = mlp_down(j, acc[0])
            for j in range(N_SLAB):
                pieces += [(1, functools.partial(mlp_up, j, k)) for k in range(FF_BLK // FF_SUB)]
                if j < N_SLAB - DOWN_KEPT:
                    pieces.append((FF_BLK // FF_SUB, functools.partial(down, j)))
        n_slots = NCH * FILL_SLOTS
        per_slot = sum(cost for cost, _ in pieces) / n_slots
        slots, issued = [], 0.0
        for i in range(n_slots):
            mine = []
            while pieces and issued < (i + 1) * per_slot:
                cost, fn = pieces.pop(0)
                issued += cost
                mine.append(fn)
            slots.append(lambda mine=mine: [fn() for fn in mine])
        if mixer:
            @pl.when(jnp.minimum(s, N_TILES - 1) % NT == 0)
            def _():
                st_ref[...] = h0_ref[0, 0]
            state = [st_ref[g] for g in range(SSD_GROUPS)]
        for c in range(NCH):
            fill = slots[c * FILL_SLOTS:(c + 1) * FILL_SLOTS]
            if mixer:
                state = ssd_chunk(c, state, fill)
            else:
                for f in fill:
                    f()
            if norm:
                ln2_rows(c)
        if mixer:
            for g in range(SSD_GROUPS):
                st_ref[g] = state[g]
            mix = jnp.dot(yn_ref[...], wout_ref[...], preferred_element_type=F32)
            m = _mod_vectors(mod_ref, jnp.minimum(s, N_TILES - 1) // NT)
            g2_ref[wr] = m[5]
        for i in range(DOWN_KEPT):
            if mlp:
                down(N_SLAB - DOWN_KEPT + i)
            if mixer:
                for j in range(i * NCH // DOWN_KEPT, (i + 1) * NCH // DOWN_KEPT):
                    r = slice(j * SSD_CHUNK, (j + 1) * SSD_CHUNK)
                    h1 = _ln_hat(ALPHA * hln_ref[0, r, :] + m[2] * mix[r]) * ln1g_ref[...] + ln1b_ref[...]
                    h1_ref[wr, r, :] = h1
                    u2_ref[r, :] = (h1 * (1.0 + m[4]) + m[3]).astype(BF16)
        if mlp:
            r2_ref[...] = ALPHA * h1_ref[rd] + g2_ref[rd] * acc[0]

    @pl.when(s == 0)
    def _():
        r2_ref[...] = jnp.zeros((TM, D_MODEL), F32)
        step(True, False, False)

    @pl.when(jnp.logical_and(s >= 1, s <= N_TILES))
    def _():
        step(True, True, True)

    @pl.when(s == N_TILES + 1)
    def _():
        step(False, False, True)


def _out_call(hln, mod, ycv, zg, xs, bt, cm, rows, cols, colf, sb, h0, exw, dx, nw,
              wout, ln1g, ln1b, wff1, wff2, ln2g, ln2b):
    cur = lambda s: jnp.minimum(s, N_TILES - 1)
    fwd = lambda s: (cur(s) // NT, cur(s) % NT, 0)
    fwd4 = lambda s: (cur(s) // NT, cur(s) % NT, 0, 0)
    fwd5 = lambda s: (cur(s) // NT, cur(s) % NT, 0, 0, 0)
    done = lambda s: (jnp.maximum(s - 2, 0) // NT, jnp.maximum(s - 2, 0) % NT, 0)
    tok = lambda width: pl.BlockSpec((1, TM, width), fwd)
    in_specs = [
        tok(D_MODEL),
        _const_spec((16, 6 * D_MODEL)),
        tok(CONV_WIDTH), tok(SSD_WIDTH), tok(SSD_WIDTH),
        pl.BlockSpec((1, NCH, SSD_GROUPS, SSD_STATE, SSD_CHUNK), fwd5),
        tok(SSD_GN),
        pl.BlockSpec((1, NCH, 2 * N_DH, SSD_CHUNK), fwd4),
        pl.BlockSpec((1, NCH, SSD_CHUNK, LANES), fwd4),
        pl.BlockSpec((1, NCH, SSD_CHUNK, LANES), fwd4),
        pl.BlockSpec((1, NCH, SSD_GROUPS, SSD_STATE, GROUP_COLS), fwd5),
        pl.BlockSpec((1, 1, SSD_GROUPS, SSD_STATE, GROUP_COLS), lambda s: (cur(s) // NT, 0, 0, 0, 0)),
        _const_spec((LANES, 2 * SSD_WIDTH)),
        _const_spec((1, SSD_WIDTH)), _const_spec((1, SSD_WIDTH)),
        _const_spec((D_MODEL, D_MODEL)),
        _const_spec((1, D_MODEL)), _const_spec((1, D_MODEL)),
        _const_spec((D_MODEL, D_FF)), _const_spec((D_FF, D_MODEL)),
        _const_spec((1, D_MODEL)), _const_spec((1, D_MODEL)),
    ]
    return pl.pallas_call(
        _out_kernel,
        grid=(N_TILES + 2,),
        in_specs=in_specs,
        out_specs=pl.BlockSpec((1, TM, D_MODEL), done),
        out_shape=jax.ShapeDtypeStruct((BATCH, SEQ, D_MODEL), F32),
        scratch_shapes=[pltpu.VMEM((SSD_GROUPS, SSD_STATE, GROUP_COLS), F32),
                        pltpu.VMEM((TM, D_MODEL), BF16),
                        pltpu.VMEM((2, TM, D_MODEL), F32),
                        pltpu.VMEM((TM, D_MODEL), BF16),
                        pltpu.VMEM((2, 1, D_MODEL), F32),
                        pltpu.VMEM((TM, D_MODEL), F32),
                        pltpu.VMEM((TM, D_FF), BF16)],
        compiler_params=pltpu.CompilerParams(dimension_semantics=("arbitrary",),
                                             vmem_limit_bytes=VMEM_LIMIT),
        name="out",
    )(hln, mod, ycv, zg, xs, bt, cm, rows, cols, colf, sb, h0, exw, dx, nw,
      wout, ln1g, ln1b, wff1, wff2, ln2g, ln2b)


def kernel(x, c, ctx, c_ctx, ln_in_g, ln_in_b, w_mod, b_mod, w_in, conv_w, ssd_conv_w, ssd_conv_b,
           dt_bias, a_log, ssd_d, ssd_norm_w, w_out, ln1_g, ln1_b, w_ff1, w_ff2, ln2_g, ln2_b):
    row = lambda v: v.reshape(1, -1).astype(F32)
    cvec = jnp.concatenate([c, c_ctx[None, :], jnp.zeros((16 - BATCH - 1, D_MODEL), F32)], axis=0)
    mod, w_in_p, w_out_p, w_ff1_p, w_ff2_p = _prep_call(cvec, w_mod, b_mod, w_in[0].T, w_out, w_ff1, w_ff2)

    lng, lnb = row(ln_in_g), row(ln_in_b)
    scw, scb = ssd_conv_w, row(ssd_conv_b[0])
    dtb = dt_bias[0].reshape(N_DH, 1)
    alog = a_log[0].reshape(N_DH, 1)
    exe = jnp.asarray(_EXE, BF16)
    exw = jnp.asarray(_EXW, BF16)

    h0 = _ctx_call(ctx, mod, lng, lnb, w_in_p, scw, scb, dtb, alog, exw, jnp.asarray(_U_CTX, BF16))
    hln, ycv, zg, xs, bt, cm, rows, cols, colf, sb = _proj_call(
        x, mod, lng, lnb, w_in_p, conv_w, scw, scb, dtb, alog, exe, exw,
        jnp.asarray(_U_CHUNK, BF16), h0)
    dx = jnp.repeat(ssd_d[0], SSD_HEADDIM).reshape(1, SSD_WIDTH)
    return _out_call(hln, mod, ycv, zg, xs, bt, cm, rows, cols, colf, sb, h0, exw,
                     dx, row(ssd_norm_w[0]), w_out_p, row(ln1_g[0]), row(ln1_b[0]),
                     w_ff1_p, w_ff2_p, row(ln2_g[0]), row(ln2_b[0]))
```

```python
import functools

import jax
import jax.numpy as jnp
import numpy as np
from jax import lax
from jax.experimental import pallas as pl
from jax.experimental.pallas import tpu as pltpu

F32 = jnp.float32
BF16 = jnp.bfloat16

D_MODEL = 1024
BATCH = 8
SEQ = 2048
CTX_LEN = 256
GRID_W = 64
CONV_WIDTH = 512
SSD_WIDTH = 512
SSD_HEADDIM = 64
SSD_HEADS = 8
SSD_GROUPS = 2
SSD_STATE = 128
SSD_CHUNK = 128
N_DIRS = 2
D_FF = 4 * D_MODEL
LN_EPS = 1e-5
RMS_EPS = 1e-5
SSD_GN = SSD_GROUPS * SSD_STATE
XBC_DIM = SSD_WIDTH + 2 * SSD_GN
Z_OFF = 3 * CONV_WIDTH
XBC_OFF = Z_OFF + SSD_WIDTH
DT_OFF = XBC_OFF + XBC_DIM
N_DH = N_DIRS * SSD_HEADS
LANES = 128
P_DT = 0
P_XBC = LANES
P_Z = P_XBC + XBC_DIM
P_CONV = P_Z + SSD_WIDTH
IN_PAD = P_CONV + 3 * CONV_WIDTH
CTX_PAD = P_XBC + SSD_WIDTH + SSD_GN
GROUP_COLS = (SSD_HEADS // SSD_GROUPS) * SSD_HEADDIM
ALPHA = 2.0 ** 0.25

TM = 512
CTX_ROWS = 2
NCH = TM // SSD_CHUNK
NT = SEQ // TM
NCHUNK = SEQ // SSD_CHUNK
N_TILES = BATCH * NT
TM_PROJ = 1024
NCH_PROJ = TM_PROJ // SSD_CHUNK
NT_PROJ = SEQ // TM_PROJ
FF_BLK = 1024
FF_SUB = 256
N_SLAB = D_FF // FF_BLK
FILL_SLOTS = 6
DOWN_KEPT = 2
VMEM_LIMIT = 58 * 1024 * 1024

COL_E1 = 48
COL_W = 80
COL_CSF = 112


def _expansion(col0, pieces, width):
    m = np.zeros((LANES, N_DH * width), np.float32)
    for t in range(pieces):
        for j in range(N_DH):
            m[col0 + 16 * t + j, j * width:(j + 1) * width] = 1.0
    return m


_EXE = _expansion(COL_E1, 2, SSD_HEADDIM)
_EXW = _expansion(COL_W, 2, SSD_HEADDIM)


def _ln_hat(x):
    mu = jnp.mean(x, axis=-1, keepdims=True)
    xc = x - mu
    var = jnp.mean(xc * xc, axis=-1, keepdims=True)
    return xc * lax.rsqrt(var + LN_EPS)


def _silu(x):
    return x / (1.0 + jnp.exp(-x))


def _softplus(x):
    return jnp.maximum(x, 0.0) + jnp.log1p(jnp.exp(-jnp.abs(x)))


def _edge_masks(rows, period):
    pos = lax.broadcasted_iota(jnp.int32, (rows, LANES), 0) % period
    return (pos != 0).astype(F32), (pos != period - 1).astype(F32)


def _conv3(t, w, mprev, mnext):
    rows = t.shape[0]
    prev = pltpu.roll(t, 1, 0) * mprev
    nxt = pltpu.roll(t, rows - 1, 0) * mnext
    return prev * w[0:1, :] + t * w[1:2, :] + nxt * w[2:3, :]


def _split(v, pieces):
    out = []
    for _ in range(pieces - 1):
        p = v.astype(BF16).astype(F32)
        out.append(p)
        v = v - p
    out.append(v.astype(BF16).astype(F32))
    return out


def _tri(length, op):
    i = np.arange(length)
    return op(i[:, None], i[None, :]).astype(np.float32)


_U_CHUNK = np.concatenate([_tri(SSD_CHUNK, np.less_equal), _tri(SSD_CHUNK, np.greater_equal),
                           np.ones((SSD_CHUNK, SSD_CHUNK), np.float32)], axis=1)
_U_CTX = np.concatenate([_tri(CTX_LEN, np.greater), _tri(CTX_LEN, np.less)], axis=1)


def _scan_mm(v, u_ref):
    pieces = jnp.concatenate(_split(v, 3), axis=0).astype(BF16)
    o = jnp.dot(pieces, u_ref[...], preferred_element_type=F32)
    return o[0:N_DH] + o[N_DH:2 * N_DH] + o[2 * N_DH:3 * N_DH]


def _dt_rows(raw, dtb, a_log):
    r = raw.T[0:N_DH, :] + dtb
    dt = _softplus(r)
    return dt, dt * (-jnp.exp(a_log))


def _fwd_rows():
    return lax.broadcasted_iota(jnp.int32, (N_DH, 1), 0) < SSD_HEADS


def _mod_vectors(mod_ref, row):
    r = mod_ref[pl.ds(row, 1), :]
    return [r[:, k * D_MODEL:(k + 1) * D_MODEL] for k in range(6)]


def _const_spec(shape):
    nd = len(shape)
    return pl.BlockSpec(shape, lambda *_: (0,) * nd, pipeline_mode=pl.Buffered(1))


def _prep_kernel(c_ref, wmod_ref, bmod_ref, wint_ref, mod_ref, pin_ref):
    cs = _silu(c_ref[...]).astype(BF16)
    mod_ref[...] = jnp.dot(cs, wmod_ref[0].astype(BF16), preferred_element_type=F32) + bmod_ref[...]

    dt_rows = jnp.concatenate([wint_ref[DT_OFF:DT_OFF + N_DH, :],
                               jnp.zeros((LANES - N_DH, wint_ref.shape[1]), F32)], axis=0)
    pin_ref[:, P_DT:P_DT + LANES] = dt_rows.T.astype(BF16)
    def move(dst, src):
        pin_ref[:, dst:dst + LANES] = wint_ref[src:src + LANES, :].T.astype(BF16)

    for dst, src, width in ((P_XBC, XBC_OFF, XBC_DIM), (P_Z, Z_OFF, SSD_WIDTH)):
        for j in range(0, width, LANES):
            move(dst + j, src + j)
    for j in range(CONV_WIDTH // LANES):
        for k in range(3):
            move(P_CONV + (3 * j + k) * LANES, k * CONV_WIDTH + j * LANES)


def _prep_call(cvec, w_mod, b_mod, w_in_t):
    steps = 8
    r1, rm = D_MODEL // steps, 6 * D_MODEL // steps
    return pl.pallas_call(
        _prep_kernel,
        grid=(steps,),
        in_specs=[pl.BlockSpec((16, D_MODEL), lambda i: (0, 0)),
                  pl.BlockSpec((1, D_MODEL, rm), lambda i: (0, 0, i)),
                  pl.BlockSpec((1, rm), lambda i: (0, i)),
                  pl.BlockSpec((w_in_t.shape[0], r1), lambda i: (0, i))],
        out_specs=[pl.BlockSpec((16, rm), lambda i: (0, i)),
                   pl.BlockSpec((r1, IN_PAD), lambda i: (i, 0))],
        out_shape=[jax.ShapeDtypeStruct((16, 6 * D_MODEL), F32),
                   jax.ShapeDtypeStruct((D_MODEL, IN_PAD), BF16)],
        compiler_params=pltpu.CompilerParams(dimension_semantics=("arbitrary",),
                                             vmem_limit_bytes=VMEM_LIMIT),
        name="prep",
    )(cvec, w_mod, b_mod, w_in_t)


def _ctx_kernel(x_ref, mod_ref, lng_ref, lnb_ref, w_ref, scw_ref, scb_ref, dtb_ref, alog_ref,
                exw_ref, u_ref, h0_ref):
    m = _mod_vectors(mod_ref, BATCH)
    sc = 1.0 + m[1]
    x = x_ref[...].reshape(CTX_ROWS * CTX_LEN, D_MODEL)
    u = _ln_hat(x) * (lng_ref[...] * sc) + (lnb_ref[...] * sc + m[0])
    proj = jnp.dot(u.astype(BF16), w_ref[...], preferred_element_type=F32)
    mprev, mnext = _edge_masks(CTX_ROWS * CTX_LEN, CTX_LEN)
    slabs = []
    for j in range((SSD_WIDTH + SSD_GN) // LANES):
        sl = slice(j * LANES, (j + 1) * LANES)
        pj = proj[:, P_XBC + j * LANES:P_XBC + (j + 1) * LANES]
        slabs.append(_silu(_conv3(pj, scw_ref[0, :, sl], mprev, mnext) + scb_ref[:, sl]))
    zero = jnp.zeros((N_DH, CTX_LEN), F32)
    for i in range(CTX_ROWS):
        tok = slice(i * CTX_LEN, (i + 1) * CTX_LEN)
        xs = jnp.concatenate([sl_[tok] for sl_ in slabs[:4]], axis=1)
        dt, adt = _dt_rows(proj[tok, P_DT:P_DT + LANES], dtb_ref[...], alog_ref[...])
        sc2 = _scan_mm(adt, u_ref)
        excl = jnp.where(_fwd_rows(), sc2[:, :CTX_LEN], sc2[:, CTX_LEN:])
        w = jnp.exp(excl) * dt
        table = jnp.concatenate([zero] * (COL_W // 16) + _split(w, 2) + [zero], axis=0)
        cols = table.T.astype(BF16)
        wx = jnp.dot(cols, exw_ref[...], preferred_element_type=F32)
        for d in range(N_DIRS):
            xw = (xs * wx[:, d * SSD_WIDTH:(d + 1) * SSD_WIDTH]).astype(BF16)
            for g in range(SSD_GROUPS):
                bt = slabs[4 + g][tok].T.astype(BF16)
                h0_ref[i, d, g] = jnp.dot(bt, xw[:, g * GROUP_COLS:(g + 1) * GROUP_COLS],
                                          preferred_element_type=F32)


def _ctx_call(ctx, mod, lng, lnb, w_in, scw, scb, dtb, alog, exw, u_ctx):
    return pl.pallas_call(
        _ctx_kernel,
        grid=(BATCH // CTX_ROWS,),
        in_specs=[pl.BlockSpec((CTX_ROWS, CTX_LEN, D_MODEL), lambda b: (b, 0, 0)),
                  _const_spec((16, 6 * D_MODEL)), _const_spec((1, D_MODEL)), _const_spec((1, D_MODEL)),
                  _const_spec((D_MODEL, CTX_PAD)), _const_spec((1, 3, XBC_DIM)), _const_spec((1, XBC_DIM)),
                  _const_spec((N_DH, 1)), _const_spec((N_DH, 1)), _const_spec((LANES, 2 * SSD_WIDTH)),
                  _const_spec((CTX_LEN, 2 * CTX_LEN))],
        out_specs=pl.BlockSpec((CTX_ROWS, N_DIRS, SSD_GROUPS, SSD_STATE, GROUP_COLS),
                               lambda b: (b, 0, 0, 0, 0)),
        out_shape=jax.ShapeDtypeStruct((BATCH, N_DIRS, SSD_GROUPS, SSD_STATE, GROUP_COLS), F32),
        compiler_params=pltpu.CompilerParams(dimension_semantics=("arbitrary",),
                                             vmem_limit_bytes=VMEM_LIMIT),
        name="ctx",
    )(ctx, mod, lng, lnb, w_in, scw, scb, dtb, alog, exw, u_ctx)


def _proj_kernel(x_ref, mod_ref, lng_ref, lnb_ref, w_ref, cw_ref, scw_ref, scb_ref, dtb_ref,
                 alog_ref, exe_ref, exw_ref, u_ref, h0_ref, wout_ref, wff1_ref, wff2_ref,
                 hln_ref, ycv_ref, zg_ref, xs_ref, bt_ref, cm_ref, rows_ref, cols_ref, colf_ref, sb_ref,
                 pout_ref, pff1_ref, pff2_ref, st_ref):
    pout_ref[...] = wout_ref[0].astype(BF16)
    pff1_ref[...] = wff1_ref[0].astype(BF16)
    pff2_ref[...] = wff2_ref[0].astype(BF16)

    @pl.when(pl.program_id(1) == 0)
    def _():
        st_ref[...] = h0_ref[0, 0]

    m = _mod_vectors(mod_ref, pl.program_id(0))
    sc = 1.0 + m[1]
    scale, shift = lng_ref[...] * sc, lnb_ref[...] * sc + m[0]
    ubs, pas = [], []
    for r in range(2):
        rows = slice(r * (TM_PROJ // 2), (r + 1) * (TM_PROJ // 2))
        xhat = _ln_hat(x_ref[0, rows, :])
        hln_ref[0, rows, :] = xhat * lng_ref[...] + lnb_ref[...]
        ubs.append((xhat * scale + shift).astype(BF16))
        pas.append(jnp.dot(ubs[r], w_ref[:, :P_Z], preferred_element_type=F32))
    ub = jnp.concatenate(ubs, axis=0)
    pa = jnp.concatenate(pas, axis=0)
    mprev, mnext = _edge_masks(TM_PROJ, GRID_W)
    is_fwd = _fwd_rows()
    zero = jnp.zeros((N_DH, SSD_CHUNK), F32)
    chunks = [slice(c * SSD_CHUNK, (c + 1) * SSD_CHUNK) for c in range(NCH_PROJ)]

    def conv_slabs(p, first):
        for i in range(2):
            gb, gc, gh = (p[:, (3 * i + k) * LANES:(3 * i + k + 1) * LANES] for k in range(3))
            sl = slice((first + i) * LANES, (first + i + 1) * LANES)
            ycv_ref[0, :, sl] = (gb * _conv3(gc * gh, cw_ref[0, :, sl], mprev, mnext)).astype(BF16)

    dts = [_dt_rows(pa[tok, P_DT:P_DT + LANES], dtb_ref[...], alog_ref[...]) for tok in chunks]
    pz = jnp.dot(ub, w_ref[:, P_Z:P_CONV], preferred_element_type=F32)

    colss = []
    for c, (dt, adt) in enumerate(dts):
        sc3 = _scan_mm(adt, u_ref)
        cs = jnp.where(is_fwd, sc3[:, :SSD_CHUNK], sc3[:, SSD_CHUNK:2 * SSD_CHUNK])
        tot = sc3[:, 2 * SSD_CHUNK:]
        e1 = jnp.exp(cs)
        w = jnp.exp(tot - cs) * dt
        rows_ref[0, c] = jnp.concatenate([cs, dt], axis=0)
        table = jnp.concatenate([zero] * 3 + _split(e1, 2) + _split(w, 2) + [cs], axis=0).T
        colf_ref[0, c] = table
        colss.append(table.astype(BF16))
        cols_ref[0, c] = colss[c]

    pc1 = jnp.dot(ub, w_ref[:, P_CONV:P_CONV + 6 * LANES], preferred_element_type=F32)

    slabs = []
    for j in range(XBC_DIM // LANES):
        sl = slice(j * LANES, (j + 1) * LANES)
        pj = pa[:, P_XBC + j * LANES:P_XBC + (j + 1) * LANES]
        slabs.append(_silu(_conv3(pj, scw_ref[0, :, sl], mprev, mnext) + scb_ref[:, sl]))
    xs = jnp.concatenate(slabs[:4], axis=1)
    xs_ref[0] = xs.astype(BF16)
    cm_ref[0] = jnp.concatenate(slabs[6:8], axis=1).astype(BF16)
    zg_ref[0] = _silu(pz).astype(BF16)
    bts = []
    for c, tok in enumerate(chunks):
        bt = [slabs[4 + g][tok].T.astype(BF16) for g in range(SSD_GROUPS)]
        for g in range(SSD_GROUPS):
            bt_ref[0, c, g] = bt[g]
        bts.append(bt)
    wxbs = [jnp.dot(cols, exw_ref[:, SSD_WIDTH:], preferred_element_type=F32) for cols in colss]
    decs = [jnp.dot(cols[0:16], exe_ref[:, SSD_WIDTH:], preferred_element_type=F32)[0:1]
            for cols in colss]

    pc2 = jnp.dot(ub, w_ref[:, P_CONV + 6 * LANES:], preferred_element_type=F32)
    conv_slabs(pc1, 0)

    state = [st_ref[g] for g in range(SSD_GROUPS)]
    for c in reversed(range(NCH_PROJ)):
        xw = (xs[chunks[c]] * wxbs[c]).astype(BF16)
        for g in range(SSD_GROUPS):
            gs = slice(g * GROUP_COLS, (g + 1) * GROUP_COLS)
            sb_ref[0, c, g] = state[g].astype(BF16)
            local = jnp.dot(bts[c][g], xw[:, gs], preferred_element_type=F32)
            state[g] = state[g] * decs[c][:, gs] + local
    for g in range(SSD_GROUPS):
        st_ref[g] = state[g]
    conv_slabs(pc2, 2)


def _proj_call(x, mod, lng, lnb, w_in, cw, scw, scb, dtb, alog, exe, exw, u_chunk, h0, w_out, w_ff1, w_ff2):
    n_steps = BATCH * NT_PROJ
    r1, r4 = D_MODEL // n_steps, D_FF // n_steps
    wrow3 = lambda b, t: (0, b * NT_PROJ + t, 0)
    wrow2 = lambda b, t: (b * NT_PROJ + t, 0)
    rev = lambda b, t: (b, NT_PROJ - 1 - t, 0)
    rev4 = lambda b, t: (b, NT_PROJ - 1 - t, 0, 0)
    rev5 = lambda b, t: (b, NT_PROJ - 1 - t, 0, 0, 0)
    tok = lambda width: pl.BlockSpec((1, TM_PROJ, width), rev)
    out_shape = [
        jax.ShapeDtypeStruct((BATCH, SEQ, D_MODEL), F32),
        jax.ShapeDtypeStruct((BATCH, SEQ, CONV_WIDTH), BF16),
        jax.ShapeDtypeStruct((BATCH, SEQ, SSD_WIDTH), BF16),
        jax.ShapeDtypeStruct((BATCH, SEQ, SSD_WIDTH), BF16),
        jax.ShapeDtypeStruct((BATCH, NCHUNK, SSD_GROUPS, SSD_STATE, SSD_CHUNK), BF16),
        jax.ShapeDtypeStruct((BATCH, SEQ, SSD_GN), BF16),
        jax.ShapeDtypeStruct((BATCH, NCHUNK, 2 * N_DH, SSD_CHUNK), F32),
        jax.ShapeDtypeStruct((BATCH, NCHUNK, SSD_CHUNK, LANES), BF16),
        jax.ShapeDtypeStruct((BATCH, NCHUNK, SSD_CHUNK, LANES), F32),
        jax.ShapeDtypeStruct((BATCH, NCHUNK, SSD_GROUPS, SSD_STATE, GROUP_COLS), BF16),
        jax.ShapeDtypeStruct((D_MODEL, D_MODEL), BF16),
        jax.ShapeDtypeStruct((D_MODEL, D_FF), BF16),
        jax.ShapeDtypeStruct((D_FF, D_MODEL), BF16),
    ]
    out_specs = [
        tok(D_MODEL), tok(CONV_WIDTH), tok(SSD_WIDTH), tok(SSD_WIDTH),
        pl.BlockSpec((1, NCH_PROJ, SSD_GROUPS, SSD_STATE, SSD_CHUNK), rev5),
        tok(SSD_GN),
        pl.BlockSpec((1, NCH_PROJ, 2 * N_DH, SSD_CHUNK), rev4),
        pl.BlockSpec((1, NCH_PROJ, SSD_CHUNK, LANES), rev4),
        pl.BlockSpec((1, NCH_PROJ, SSD_CHUNK, LANES), rev4),
        pl.BlockSpec((1, NCH_PROJ, SSD_GROUPS, SSD_STATE, GROUP_COLS), rev5),
        pl.BlockSpec((r1, D_MODEL), wrow2), pl.BlockSpec((r1, D_FF), wrow2), pl.BlockSpec((r4, D_MODEL), wrow2),
    ]
    in_specs = [
        pl.BlockSpec((1, TM_PROJ, D_MODEL), rev),
        _const_spec((16, 6 * D_MODEL)),
        _const_spec((1, D_MODEL)), _const_spec((1, D_MODEL)),
        _const_spec((D_MODEL, IN_PAD)),
        _const_spec((1, 3, CONV_WIDTH)), _const_spec((1, 3, XBC_DIM)), _const_spec((1, XBC_DIM)),
        _const_spec((N_DH, 1)), _const_spec((N_DH, 1)),
        _const_spec((LANES, 2 * SSD_WIDTH)), _const_spec((LANES, 2 * SSD_WIDTH)),
        _const_spec((SSD_CHUNK, 3 * SSD_CHUNK)),
        pl.BlockSpec((1, 1, SSD_GROUPS, SSD_STATE, GROUP_COLS), lambda b, t: (b, 1, 0, 0, 0)),
        pl.BlockSpec((1, r1, D_MODEL), wrow3), pl.BlockSpec((1, r1, D_FF), wrow3),
        pl.BlockSpec((1, r4, D_MODEL), wrow3),
    ]
    return pl.pallas_call(
        _proj_kernel,
        grid=(BATCH, NT_PROJ),
        in_specs=in_specs,
        out_specs=out_specs,
        out_shape=out_shape,
        scratch_shapes=[pltpu.VMEM((SSD_GROUPS, SSD_STATE, GROUP_COLS), F32)],
        compiler_params=pltpu.CompilerParams(dimension_semantics=("arbitrary", "arbitrary"),
                                             vmem_limit_bytes=VMEM_LIMIT),
        name="proj",
    )(x, mod, lng, lnb, w_in, cw, scw, scb, dtb, alog, exe, exw, u_chunk, h0, w_out, w_ff1, w_ff2)


def _out_kernel(hln_ref, mod_ref, ycv_ref, zg_ref, xs_ref, bt_ref, cm_ref, rows_ref,
                cols_ref, colf_ref, sb_ref, h0_ref, exw_ref, dx_ref, nw_ref, wout_ref,
                ln1g_ref, ln1b_ref, wff1_ref, wff2_ref, ln2g_ref, ln2b_ref,
                out_ref, st_ref, yn_ref, h1_ref, u2_ref, g2_ref, r2_ref, hid_ref):
    s = pl.program_id(0)
    wr = s % 2
    rd = (s + 1) % 2
    li = lax.broadcasted_iota(jnp.int32, (SSD_CHUNK, SSD_CHUNK), 0)
    si = lax.broadcasted_iota(jnp.int32, (SSD_CHUNK, SSD_CHUNK), 1)
    low = li >= si
    diag = li == si
    lo_half = si < SSD_HEADDIM

    def ssd_chunk(c, state, fill):
        tok = slice(c * SSD_CHUNK, (c + 1) * SSD_CHUNK)
        rows = rows_ref[0, c]
        cols = cols_ref[0, c]
        xs = xs_ref[0, tok, :]
        cm = cm_ref[0, tok, :]
        colf = colf_ref[0, c]
        bc = [jnp.broadcast_to(colf[:, COL_CSF + j:COL_CSF + j + 1], (SSD_CHUNK, LANES))
              for j in range(N_DH)]
        e1x = [jnp.concatenate(
            [jnp.exp(jnp.where(lo_half, bc[d * SSD_HEADS + 2 * k], bc[d * SSD_HEADS + 2 * k + 1]))
             for k in range(SSD_HEADS // 2)], axis=1) for d in range(N_DIRS)]
        wxf = jnp.dot(cols, exw_ref[:, :SSD_WIDTH], preferred_element_type=F32)
        dec = e1x[0][SSD_CHUNK - 1:SSD_CHUNK, :]
        gmat = [jnp.dot(cm[:, g * SSD_STATE:(g + 1) * SSD_STATE], bt_ref[0, c, g],
                        preferred_element_type=F32) for g in range(SSD_GROUPS)]
        fill[0]()

        ys = []
        for k in range(SSD_HEADS // 2):
            ms = []
            for h in (2 * k, 2 * k + 1):
                g = h // (SSD_HEADS // SSD_GROUPS)
                hb = SSD_HEADS + h
                arg = jnp.where(low,
                                bc[h] - rows[h:h + 1, :],
                                bc[hb] - rows[hb:hb + 1, :])
                dtf = rows[N_DH + h:N_DH + h + 1, :]
                dtb = rows[N_DH + hb:N_DH + hb + 1, :]
                fac = jnp.where(low, dtf, dtb) + jnp.where(diag, dtb, 0.0)
                ms.append((gmat[g] * jnp.exp(arg) * fac).astype(BF16))
            xp = xs[:, k * LANES:(k + 1) * LANES]
            rhs = jnp.concatenate([jnp.where(lo_half, xp, jnp.zeros_like(xp)),
                                   jnp.where(lo_half, jnp.zeros_like(xp), xp)], axis=0)
            ys.append(jnp.dot(jnp.concatenate(ms, axis=1), rhs, preferred_element_type=F32))
            fill[k + 1]()
        y = jnp.concatenate(ys, axis=1)

        yf, yb = [], []
        for g in range(SSD_GROUPS):
            cg = cm[:, g * SSD_STATE:(g + 1) * SSD_STATE]
            yf.append(jnp.dot(cg, state[g].astype(BF16), preferred_element_type=F32))
            yb.append(jnp.dot(cg, sb_ref[0, c, g], preferred_element_type=F32))
        y = (y + jnp.concatenate(yf, axis=1) * e1x[0]
             + jnp.concatenate(yb, axis=1) * e1x[1]
             + xs.astype(F32) * dx_ref[...])

        yg = y * zg_ref[0, tok, :].astype(F32)
        ms_ = jnp.mean(yg * yg, axis=-1, keepdims=True)
        yn_ref[tok, :CONV_WIDTH] = ycv_ref[0, tok, :]
        yn_ref[tok, CONV_WIDTH:] = (yg * lax.rsqrt(ms_ + RMS_EPS) * nw_ref[...]).astype(BF16)
        fill[5]()

        xw = (xs.astype(F32) * wxf).astype(BF16)
        new_state = []
        for g in range(SSD_GROUPS):
            gs = slice(g * GROUP_COLS, (g + 1) * GROUP_COLS)
            local = jnp.dot(bt_ref[0, c, g], xw[:, gs], preferred_element_type=F32)
            new_state.append(state[g] * dec[:, gs] + local)
        return new_state

    def mlp_up(j, k):
        blk = slice(j * FF_BLK + k * FF_SUB, j * FF_BLK + (k + 1) * FF_SUB)
        hid = jnp.maximum(jnp.dot(u2_ref[...], wff1_ref[:, blk], preferred_element_type=F32), 0.0)
        hid_ref[:, blk] = (hid * hid).astype(BF16)

    def mlp_down(j, acc):
        blk = slice(j * FF_BLK, (j + 1) * FF_BLK)
        part = jnp.dot(hid_ref[:, blk], wff2_ref[blk, :], preferred_element_type=F32)
        return part if acc is None else acc + part

    def ln2_rows(i):
        r = slice(i * SSD_CHUNK, (i + 1) * SSD_CHUNK)
        out_ref[0, r, :] = _ln_hat(r2_ref[r, :]) * ln2g_ref[...] + ln2b_ref[...]

    def step(mixer, mlp, norm):
        state, acc = None, [None]
        pieces = []
        if mlp:
            def down(j):
                acc[0] = mlp_down(j, acc[0])
            for j in range(N_SLAB):
                pieces += [(1, functools.partial(mlp_up, j, k)) for k in range(FF_BLK // FF_SUB)]
                if j < N_SLAB - DOWN_KEPT:
                    pieces.append((FF_BLK // FF_SUB, functools.partial(down, j)))
        n_slots = NCH * FILL_SLOTS
        per_slot = sum(cost for cost, _ in pieces) / n_slots
        slots, issued = [], 0.0
        for i in range(n_slots):
            mine = []
            while pieces and issued < (i + 1) * per_slot:
                cost, fn = pieces.pop(0)
                issued += cost
                mine.append(fn)
            slots.append(lambda mine=mine: [fn() for fn in mine])
        if mixer:
            @pl.when(jnp.minimum(s, N_TILES - 1) % NT == 0)
            def _():
                st_ref[...] = h0_ref[0, 0]
            state = [st_ref[g] for g in range(SSD_GROUPS)]
        for c in range(NCH):
            fill = slots[c * FILL_SLOTS:(c + 1) * FILL_SLOTS]
            if mixer:
                state = ssd_chunk(c, state, fill)
            else:
                for f in fill:
                    f()
            if norm:
                ln2_rows(c)
        if mixer:
            for g in range(SSD_GROUPS):
                st_ref[g] = state[g]
            mix = jnp.dot(yn_ref[...], wout_ref[...], preferred_element_type=F32)
            m = _mod_vectors(mod_ref, jnp.minimum(s, N_TILES - 1) // NT)
            g2_ref[wr] = m[5]
        for i in range(DOWN_KEPT):
            if mlp:
                down(N_SLAB - DOWN_KEPT + i)
            if mixer:
                for j in range(i * NCH // DOWN_KEPT, (i + 1) * NCH // DOWN_KEPT):
                    r = slice(j * SSD_CHUNK, (j + 1) * SSD_CHUNK)
                    h1 = _ln_hat(ALPHA * hln_ref[0, r, :] + m[2] * mix[r]) * ln1g_ref[...] + ln1b_ref[...]
                    h1_ref[wr, r, :] = h1
                    u2_ref[r, :] = (h1 * (1.0 + m[4]) + m[3]).astype(BF16)
        if mlp:
            r2_ref[...] = ALPHA * h1_ref[rd] + g2_ref[rd] * acc[0]

    @pl.when(s == 0)
    def _():
        r2_ref[...] = jnp.zeros((TM, D_MODEL), F32)
        step(True, False, False)

    @pl.when(jnp.logical_and(s >= 1, s <= N_TILES))
    def _():
        step(True, True, True)

    @pl.when(s == N_TILES + 1)
    def _():
        step(False, False, True)


def _out_call(hln, mod, ycv, zg, xs, bt, cm, rows, cols, colf, sb, h0, exw, dx, nw,
              wout, ln1g, ln1b, wff1, wff2, ln2g, ln2b):
    cur = lambda s: jnp.minimum(s, N_TILES - 1)
    fwd = lambda s: (cur(s) // NT, cur(s) % NT, 0)
    fwd4 = lambda s: (cur(s) // NT, cur(s) % NT, 0, 0)
    fwd5 = lambda s: (cur(s) // NT, cur(s) % NT, 0, 0, 0)
    done = lambda s: (jnp.maximum(s - 2, 0) // NT, jnp.maximum(s - 2, 0) % NT, 0)
    tok = lambda width: pl.BlockSpec((1, TM, width), fwd)
    in_specs = [
        tok(D_MODEL),
        _const_spec((16, 6 * D_MODEL)),
        tok(CONV_WIDTH), tok(SSD_WIDTH), tok(SSD_WIDTH),
        pl.BlockSpec((1, NCH, SSD_GROUPS, SSD_STATE, SSD_CHUNK), fwd5),
        tok(SSD_GN),
        pl.BlockSpec((1, NCH, 2 * N_DH, SSD_CHUNK), fwd4),
        pl.BlockSpec((1, NCH, SSD_CHUNK, LANES), fwd4),
        pl.BlockSpec((1, NCH, SSD_CHUNK, LANES), fwd4),
        pl.BlockSpec((1, NCH, SSD_GROUPS, SSD_STATE, GROUP_COLS), fwd5),
        pl.BlockSpec((1, 1, SSD_GROUPS, SSD_STATE, GROUP_COLS), lambda s: (cur(s) // NT, 0, 0, 0, 0)),
        _const_spec((LANES, 2 * SSD_WIDTH)),
        _const_spec((1, SSD_WIDTH)), _const_spec((1, SSD_WIDTH)),
        _const_spec((D_MODEL, D_MODEL)),
        _const_spec((1, D_MODEL)), _const_spec((1, D_MODEL)),
        _const_spec((D_MODEL, D_FF)), _const_spec((D_FF, D_MODEL)),
        _const_spec((1, D_MODEL)), _const_spec((1, D_MODEL)),
    ]
    return pl.pallas_call(
        _out_kernel,
        grid=(N_TILES + 2,),
        in_specs=in_specs,
        out_specs=pl.BlockSpec((1, TM, D_MODEL), done),
        out_shape=jax.ShapeDtypeStruct((BATCH, SEQ, D_MODEL), F32),
        scratch_shapes=[pltpu.VMEM((SSD_GROUPS, SSD_STATE, GROUP_COLS), F32),
                        pltpu.VMEM((TM, D_MODEL), BF16),
                        pltpu.VMEM((2, TM, D_MODEL), F32),
                        pltpu.VMEM((TM, D_MODEL), BF16),
                        pltpu.VMEM((2, 1, D_MODEL), F32),
                        pltpu.VMEM((TM, D_MODEL), F32),
                        pltpu.VMEM((TM, D_FF), BF16)],
        compiler_params=pltpu.CompilerParams(dimension_semantics=("arbitrary",),
                                             vmem_limit_bytes=VMEM_LIMIT),
        name="out",
    )(hln, mod, ycv, zg, xs, bt, cm, rows, cols, colf, sb, h0, exw, dx, nw,
      wout, ln1g, ln1b, wff1, wff2, ln2g, ln2b)


def kernel(x, c, ctx, c_ctx, ln_in_g, ln_in_b, w_mod, b_mod, w_in, conv_w, ssd_conv_w, ssd_conv_b,
           dt_bias, a_log, ssd_d, ssd_norm_w, w_out, ln1_g, ln1_b, w_ff1, w_ff2, ln2_g, ln2_b):
    row = lambda v: v.reshape(1, -1).astype(F32)
    cvec = jnp.concatenate([c, c_ctx[None, :], jnp.zeros((16 - BATCH - 1, D_MODEL), F32)], axis=0)
    mod, w_in_p = _prep_call(cvec, w_mod, b_mod, w_in[0].T)

    lng, lnb = row(ln_in_g), row(ln_in_b)
    scw, scb = ssd_conv_w, row(ssd_conv_b[0])
    dtb = dt_bias[0].reshape(N_DH, 1)
    alog = a_log[0].reshape(N_DH, 1)
    exe = jnp.asarray(_EXE, BF16)
    exw = jnp.asarray(_EXW, BF16)

    h0 = _ctx_call(ctx, mod, lng, lnb, w_in_p, scw, scb, dtb, alog, exw, jnp.asarray(_U_CTX, BF16))
    hln, ycv, zg, xs, bt, cm, rows, cols, colf, sb, w_out_p, w_ff1_p, w_ff2_p = _proj_call(
        x, mod, lng, lnb, w_in_p, conv_w, scw, scb, dtb, alog, exe, exw,
        jnp.asarray(_U_CHUNK, BF16), h0, w_out, w_ff1, w_ff2)
    dx = jnp.repeat(ssd_d[0], SSD_HEADDIM).reshape(1, SSD_WIDTH)
    return _out_call(hln, mod, ycv, zg, xs, bt, cm, rows, cols, colf, sb, h0, exw,
                     dx, row(ssd_norm_w[0]), w_out_p, row(ln1_g[0]), row(ln1_b[0]),
                     w_ff1_p, w_ff2_p, row(ln2_g[0]), row(ln2_b[0]))
```

```python
import functools

import jax
import jax.numpy as jnp
import numpy as np
from jax import lax
from jax.experimental import pallas as pl
from jax.experimental.pallas import tpu as pltpu

F32 = jnp.float32
BF16 = jnp.bfloat16

D_MODEL = 1024
BATCH = 8
SEQ = 2048
CTX_LEN = 256
GRID_W = 64
CONV_WIDTH = 512
SSD_WIDTH = 512
SSD_HEADDIM = 64
SSD_HEADS = 8
SSD_GROUPS = 2
SSD_STATE = 128
SSD_CHUNK = 128
N_DIRS = 2
D_FF = 4 * D_MODEL
LN_EPS = 1e-5
RMS_EPS = 1e-5
SSD_GN = SSD_GROUPS * SSD_STATE
XBC_DIM = SSD_WIDTH + 2 * SSD_GN
Z_OFF = 3 * CONV_WIDTH
XBC_OFF = Z_OFF + SSD_WIDTH
DT_OFF = XBC_OFF + XBC_DIM
N_DH = N_DIRS * SSD_HEADS
LANES = 128
P_DT = 0
P_XBC = LANES
P_Z = P_XBC + XBC_DIM
P_CONV = P_Z + SSD_WIDTH
IN_PAD = P_CONV + 3 * CONV_WIDTH
CTX_PAD = P_XBC + SSD_WIDTH + SSD_GN
GROUP_COLS = (SSD_HEADS // SSD_GROUPS) * SSD_HEADDIM
ALPHA = 2.0 ** 0.25

TM = 512
CTX_ROWS = 2
NCH = TM // SSD_CHUNK
NT = SEQ // TM
NCHUNK = SEQ // SSD_CHUNK
N_TILES = BATCH * NT
TM_PROJ = 1024
NCH_PROJ = TM_PROJ // SSD_CHUNK
NT_PROJ = SEQ // TM_PROJ
FF_BLK = 1024
FF_SUB = 256
N_SLAB = D_FF // FF_BLK
FILL_SLOTS = 6
DOWN_KEPT = 2
DOWN_LAG = 1
VMEM_LIMIT = 58 * 1024 * 1024

COL_E1 = 48
COL_W = 80
COL_CSF = 112


def _expansion(col0, pieces, width):
    m = np.zeros((LANES, N_DH * width), np.float32)
    for t in range(pieces):
        for j in range(N_DH):
            m[col0 + 16 * t + j, j * width:(j + 1) * width] = 1.0
    return m


_EXE = _expansion(COL_E1, 2, SSD_HEADDIM)
_EXW = _expansion(COL_W, 2, SSD_HEADDIM)


def _ln_hat(x):
    mu = jnp.mean(x, axis=-1, keepdims=True)
    xc = x - mu
    var = jnp.mean(xc * xc, axis=-1, keepdims=True)
    return xc * lax.rsqrt(var + LN_EPS)


def _silu(x):
    return x / (1.0 + jnp.exp(-x))


def _softplus(x):
    return jnp.maximum(x, 0.0) + jnp.log1p(jnp.exp(-jnp.abs(x)))


def _edge_masks(rows, period):
    pos = lax.broadcasted_iota(jnp.int32, (rows, LANES), 0) % period
    return (pos != 0).astype(F32), (pos != period - 1).astype(F32)


def _conv3(t, w, mprev, mnext):
    rows = t.shape[0]
    prev = pltpu.roll(t, 1, 0) * mprev
    nxt = pltpu.roll(t, rows - 1, 0) * mnext
    return prev * w[0:1, :] + t * w[1:2, :] + nxt * w[2:3, :]


def _split(v, pieces):
    out = []
    for _ in range(pieces - 1):
        p = v.astype(BF16).astype(F32)
        out.append(p)
        v = v - p
    out.append(v.astype(BF16).astype(F32))
    return out


def _tri(length, op):
    i = np.arange(length)
    return op(i[:, None], i[None, :]).astype(np.float32)


_U_CHUNK = np.concatenate([_tri(SSD_CHUNK, np.less_equal), _tri(SSD_CHUNK, np.greater_equal),
                           np.ones((SSD_CHUNK, SSD_CHUNK), np.float32)], axis=1)
_U_CTX = np.concatenate([_tri(CTX_LEN, np.greater), _tri(CTX_LEN, np.less)], axis=1)


def _scan_mm(v, u_ref):
    pieces = jnp.concatenate(_split(v, 3), axis=0).astype(BF16)
    o = jnp.dot(pieces, u_ref[...], preferred_element_type=F32)
    return o[0:N_DH] + o[N_DH:2 * N_DH] + o[2 * N_DH:3 * N_DH]


def _dt_rows(raw, dtb, a_log):
    r = raw.T[0:N_DH, :] + dtb
    dt = _softplus(r)
    return dt, dt * (-jnp.exp(a_log))


def _fwd_rows():
    return lax.broadcasted_iota(jnp.int32, (N_DH, 1), 0) < SSD_HEADS


S_SCW = 0
S_SCB = 3
S_CW = 4
S_DT = 7
S_DXNW = 8


def _dt_params(small_ref):
    t = jnp.broadcast_to(small_ref[S_DT:S_DT + 1, 0:LANES], (N_DH, LANES))
    sub = lax.broadcasted_iota(jnp.int32, (N_DH, LANES), 0)
    lane = lax.broadcasted_iota(jnp.int32, (N_DH, LANES), 1)
    pick = lambda off: jnp.sum(jnp.where(lane == sub + off, t, 0.0), axis=1, keepdims=True)
    return pick(0), pick(N_DH)


def _mod_vectors(mod_ref, row):
    r = mod_ref[pl.ds(row, 1), :]
    return [r[:, k * D_MODEL:(k + 1) * D_MODEL] for k in range(6)]


def _const_spec(shape):
    nd = len(shape)
    return pl.BlockSpec(shape, lambda *_: (0,) * nd, pipeline_mode=pl.Buffered(1))


def _prep_kernel(c_ref, wmod_ref, bmod_ref, wint_ref, mod_ref, pin_ref):
    cs = _silu(c_ref[...]).astype(BF16)
    mod_ref[...] = jnp.dot(cs, wmod_ref[0].astype(BF16), preferred_element_type=F32) + bmod_ref[...]

    dt_rows = jnp.concatenate([wint_ref[DT_OFF:DT_OFF + N_DH, :],
                               jnp.zeros((LANES - N_DH, wint_ref.shape[1]), F32)], axis=0)
    pin_ref[:, P_DT:P_DT + LANES] = dt_rows.T.astype(BF16)
    def move(dst, src):
        pin_ref[:, dst:dst + LANES] = wint_ref[src:src + LANES, :].T.astype(BF16)

    for dst, src, width in ((P_XBC, XBC_OFF, XBC_DIM), (P_Z, Z_OFF, SSD_WIDTH)):
        for j in range(0, width, LANES):
            move(dst + j, src + j)
    for j in range(CONV_WIDTH // LANES):
        for k in range(3):
            move(P_CONV + (3 * j + k) * LANES, k * CONV_WIDTH + j * LANES)


def _prep_call(cvec, w_mod, b_mod, w_in_t):
    steps = 8
    r1, rm = D_MODEL // steps, 6 * D_MODEL // steps
    return pl.pallas_call(
        _prep_kernel,
        grid=(steps,),
        in_specs=[pl.BlockSpec((16, D_MODEL), lambda i: (0, 0)),
                  pl.BlockSpec((1, D_MODEL, rm), lambda i: (0, 0, i)),
                  pl.BlockSpec((1, rm), lambda i: (0, i)),
                  pl.BlockSpec((w_in_t.shape[0], r1), lambda i: (0, i))],
        out_specs=[pl.BlockSpec((16, rm), lambda i: (0, i)),
                   pl.BlockSpec((r1, IN_PAD), lambda i: (i, 0))],
        out_shape=[jax.ShapeDtypeStruct((16, 6 * D_MODEL), F32),
                   jax.ShapeDtypeStruct((D_MODEL, IN_PAD), BF16)],
        compiler_params=pltpu.CompilerParams(dimension_semantics=("arbitrary",),
                                             vmem_limit_bytes=VMEM_LIMIT),
        name="prep",
    )(cvec, w_mod, b_mod, w_in_t)


def _ctx_kernel(x_ref, mod_ref, lng_ref, lnb_ref, w_ref, small_ref, exw_ref, u_ref, h0_ref):
    dtb, alog = _dt_params(small_ref)
    m = _mod_vectors(mod_ref, BATCH)
    sc = 1.0 + m[1]
    x = x_ref[...].reshape(CTX_ROWS * CTX_LEN, D_MODEL)
    u = _ln_hat(x) * (lng_ref[...] * sc) + (lnb_ref[...] * sc + m[0])
    proj = jnp.dot(u.astype(BF16), w_ref[...], preferred_element_type=F32)
    mprev, mnext = _edge_masks(CTX_ROWS * CTX_LEN, CTX_LEN)
    slabs = []
    for j in range((SSD_WIDTH + SSD_GN) // LANES):
        sl = slice(j * LANES, (j + 1) * LANES)
        pj = proj[:, P_XBC + j * LANES:P_XBC + (j + 1) * LANES]
        slabs.append(_silu(_conv3(pj, small_ref[S_SCW:S_SCW + 3, sl], mprev, mnext)
                           + small_ref[S_SCB:S_SCB + 1, sl]))
    zero = jnp.zeros((N_DH, CTX_LEN), F32)
    for i in range(CTX_ROWS):
        tok = slice(i * CTX_LEN, (i + 1) * CTX_LEN)
        xs = jnp.concatenate([sl_[tok] for sl_ in slabs[:4]], axis=1)
        dt, adt = _dt_rows(proj[tok, P_DT:P_DT + LANES], dtb, alog)
        sc2 = _scan_mm(adt, u_ref)
        excl = jnp.where(_fwd_rows(), sc2[:, :CTX_LEN], sc2[:, CTX_LEN:])
        w = jnp.exp(excl) * dt
        table = jnp.concatenate([zero] * (COL_W // 16) + _split(w, 2) + [zero], axis=0)
        cols = table.T.astype(BF16)
        wx = jnp.dot(cols, exw_ref[...], preferred_element_type=F32)
        for d in range(N_DIRS):
            xw = (xs * wx[:, d * SSD_WIDTH:(d + 1) * SSD_WIDTH]).astype(BF16)
            for g in range(SSD_GROUPS):
                bt = slabs[4 + g][tok].T.astype(BF16)
                h0_ref[i, d, g] = jnp.dot(bt, xw[:, g * GROUP_COLS:(g + 1) * GROUP_COLS],
                                          preferred_element_type=F32)


def _ctx_call(ctx, mod, lng, lnb, w_in, small, exw, u_ctx):
    return pl.pallas_call(
        _ctx_kernel,
        grid=(BATCH // CTX_ROWS,),
        in_specs=[pl.BlockSpec((CTX_ROWS, CTX_LEN, D_MODEL), lambda b: (b, 0, 0)),
                  _const_spec((16, 6 * D_MODEL)), _const_spec((1, D_MODEL)), _const_spec((1, D_MODEL)),
                  _const_spec((D_MODEL, CTX_PAD)), _const_spec((16, D_MODEL)),
                  _const_spec((LANES, 2 * SSD_WIDTH)),
                  _const_spec((CTX_LEN, 2 * CTX_LEN))],
        out_specs=pl.BlockSpec((CTX_ROWS, N_DIRS, SSD_GROUPS, SSD_STATE, GROUP_COLS),
                               lambda b: (b, 0, 0, 0, 0)),
        out_shape=jax.ShapeDtypeStruct((BATCH, N_DIRS, SSD_GROUPS, SSD_STATE, GROUP_COLS), F32),
        compiler_params=pltpu.CompilerParams(dimension_semantics=("arbitrary",),
                                             vmem_limit_bytes=VMEM_LIMIT),
        name="ctx",
    )(ctx, mod, lng, lnb, w_in, small, exw, u_ctx)


def _proj_kernel(x_ref, mod_ref, lng_ref, lnb_ref, w_ref, small_ref,
                 exe_ref, exw_ref, u_ref, h0_ref, wout_ref, wff1_ref, wff2_ref,
                 hln_ref, ycv_ref, zg_ref, xs_ref, bt_ref, cm_ref, rows_ref, cols_ref, colf_ref, sb_ref,
                 pout_ref, pff1_ref, pff2_ref, st_ref):
    pout_ref[...] = wout_ref[0].astype(BF16)
    pff1_ref[...] = wff1_ref[0].astype(BF16)
    pff2_ref[...] = wff2_ref[0].astype(BF16)

    @pl.when(pl.program_id(1) == 0)
    def _():
        st_ref[...] = h0_ref[0, 0]

    m = _mod_vectors(mod_ref, pl.program_id(0))
    sc = 1.0 + m[1]
    scale, shift = lng_ref[...] * sc, lnb_ref[...] * sc + m[0]
    ubs, pas = [], []
    for r in range(2):
        rows = slice(r * (TM_PROJ // 2), (r + 1) * (TM_PROJ // 2))
        xhat = _ln_hat(x_ref[0, rows, :])
        hln_ref[0, rows, :] = xhat * lng_ref[...] + lnb_ref[...]
        ubs.append((xhat * scale + shift).astype(BF16))
        pas.append(jnp.dot(ubs[r], w_ref[:, :P_Z], preferred_element_type=F32))
    ub = jnp.concatenate(ubs, axis=0)
    pa = jnp.concatenate(pas, axis=0)
    mprev, mnext = _edge_masks(TM_PROJ, GRID_W)
    is_fwd = _fwd_rows()
    zero = jnp.zeros((N_DH, SSD_CHUNK), F32)
    chunks = [slice(c * SSD_CHUNK, (c + 1) * SSD_CHUNK) for c in range(NCH_PROJ)]

    def conv_slabs(p, first):
        for i in range(2):
            gb, gc, gh = (p[:, (3 * i + k) * LANES:(3 * i + k + 1) * LANES] for k in range(3))
            sl = slice((first + i) * LANES, (first + i + 1) * LANES)
            ycv_ref[0, :, sl] = (gb * _conv3(gc * gh, small_ref[S_CW:S_CW + 3, sl], mprev, mnext)).astype(BF16)

    dtb, alog = _dt_params(small_ref)
    dts = [_dt_rows(pa[tok, P_DT:P_DT + LANES], dtb, alog) for tok in chunks]
    pz = jnp.dot(ub, w_ref[:, P_Z:P_CONV], preferred_element_type=F32)

    colss = []
    for c, (dt, adt) in enumerate(dts):
        sc3 = _scan_mm(adt, u_ref)
        cs = jnp.where(is_fwd, sc3[:, :SSD_CHUNK], sc3[:, SSD_CHUNK:2 * SSD_CHUNK])
        tot = sc3[:, 2 * SSD_CHUNK:]
        e1 = jnp.exp(cs)
        w = jnp.exp(tot - cs) * dt
        rows_ref[0, c] = jnp.concatenate([cs, dt], axis=0)
        table = jnp.concatenate([zero] * 3 + _split(e1, 2) + _split(w, 2) + [cs], axis=0).T
        colf_ref[0, c] = table
        colss.append(table.astype(BF16))
        cols_ref[0, c] = colss[c]

    pc1 = jnp.dot(ub, w_ref[:, P_CONV:P_CONV + 6 * LANES], preferred_element_type=F32)

    slabs = []
    for j in range(XBC_DIM // LANES):
        sl = slice(j * LANES, (j + 1) * LANES)
        pj = pa[:, P_XBC + j * LANES:P_XBC + (j + 1) * LANES]
        slabs.append(_silu(_conv3(pj, small_ref[S_SCW:S_SCW + 3, sl], mprev, mnext)
                           + small_ref[S_SCB:S_SCB + 1, sl]))
    xs = jnp.concatenate(slabs[:4], axis=1)
    xs_ref[0] = xs.astype(BF16)
    cm_ref[0] = jnp.concatenate(slabs[6:8], axis=1).astype(BF16)
    zg_ref[0] = _silu(pz).astype(BF16)
    bts = []
    for c, tok in enumerate(chunks):
        bt = [slabs[4 + g][tok].T.astype(BF16) for g in range(SSD_GROUPS)]
        for g in range(SSD_GROUPS):
            bt_ref[0, c, g] = bt[g]
        bts.append(bt)
    wxbs = [jnp.dot(cols, exw_ref[:, SSD_WIDTH:], preferred_element_type=F32) for cols in colss]
    decs = [jnp.dot(cols[0:16], exe_ref[:, SSD_WIDTH:], preferred_element_type=F32)[0:1]
            for cols in colss]

    pc2 = jnp.dot(ub, w_ref[:, P_CONV + 6 * LANES:], preferred_element_type=F32)
    conv_slabs(pc1, 0)

    state = [st_ref[g] for g in range(SSD_GROUPS)]
    for c in reversed(range(NCH_PROJ)):
        xw = (xs[chunks[c]] * wxbs[c]).astype(BF16)
        for g in range(SSD_GROUPS):
            gs = slice(g * GROUP_COLS, (g + 1) * GROUP_COLS)
            sb_ref[0, c, g] = state[g].astype(BF16)
            local = jnp.dot(bts[c][g], xw[:, gs], preferred_element_type=F32)
            state[g] = state[g] * decs[c][:, gs] + local
    for g in range(SSD_GROUPS):
        st_ref[g] = state[g]
    conv_slabs(pc2, 2)


def _proj_call(x, mod, lng, lnb, w_in, small, exe, exw, u_chunk, h0, w_out, w_ff1, w_ff2):
    n_steps = BATCH * NT_PROJ
    r1, r4 = D_MODEL // n_steps, D_FF // n_steps
    wrow3 = lambda b, t: (0, b * NT_PROJ + t, 0)
    wrow2 = lambda b, t: (b * NT_PROJ + t, 0)
    rev = lambda b, t: (b, NT_PROJ - 1 - t, 0)
    rev4 = lambda b, t: (b, NT_PROJ - 1 - t, 0, 0)
    rev5 = lambda b, t: (b, NT_PROJ - 1 - t, 0, 0, 0)
    tok = lambda width: pl.BlockSpec((1, TM_PROJ, width), rev)
    out_shape = [
        jax.ShapeDtypeStruct((BATCH, SEQ, D_MODEL), F32),
        jax.ShapeDtypeStruct((BATCH, SEQ, CONV_WIDTH), BF16),
        jax.ShapeDtypeStruct((BATCH, SEQ, SSD_WIDTH), BF16),
        jax.ShapeDtypeStruct((BATCH, SEQ, SSD_WIDTH), BF16),
        jax.ShapeDtypeStruct((BATCH, NCHUNK, SSD_GROUPS, SSD_STATE, SSD_CHUNK), BF16),
        jax.ShapeDtypeStruct((BATCH, SEQ, SSD_GN), BF16),
        jax.ShapeDtypeStruct((BATCH, NCHUNK, 2 * N_DH, SSD_CHUNK), F32),
        jax.ShapeDtypeStruct((BATCH, NCHUNK, SSD_CHUNK, LANES), BF16),
        jax.ShapeDtypeStruct((BATCH, NCHUNK, SSD_CHUNK, LANES), F32),
        jax.ShapeDtypeStruct((BATCH, NCHUNK, SSD_GROUPS, SSD_STATE, GROUP_COLS), BF16),
        jax.ShapeDtypeStruct((D_MODEL, D_MODEL), BF16),
        jax.ShapeDtypeStruct((D_MODEL, D_FF), BF16),
        jax.ShapeDtypeStruct((D_FF, D_MODEL), BF16),
    ]
    out_specs = [
        tok(D_MODEL), tok(CONV_WIDTH), tok(SSD_WIDTH), tok(SSD_WIDTH),
        pl.BlockSpec((1, NCH_PROJ, SSD_GROUPS, SSD_STATE, SSD_CHUNK), rev5),
        tok(SSD_GN),
        pl.BlockSpec((1, NCH_PROJ, 2 * N_DH, SSD_CHUNK), rev4),
        pl.BlockSpec((1, NCH_PROJ, SSD_CHUNK, LANES), rev4),
        pl.BlockSpec((1, NCH_PROJ, SSD_CHUNK, LANES), rev4),
        pl.BlockSpec((1, NCH_PROJ, SSD_GROUPS, SSD_STATE, GROUP_COLS), rev5),
        pl.BlockSpec((r1, D_MODEL), wrow2), pl.BlockSpec((r1, D_FF), wrow2), pl.BlockSpec((r4, D_MODEL), wrow2),
    ]
    in_specs = [
        pl.BlockSpec((1, TM_PROJ, D_MODEL), rev),
        _const_spec((16, 6 * D_MODEL)),
        _const_spec((1, D_MODEL)), _const_spec((1, D_MODEL)),
        _const_spec((D_MODEL, IN_PAD)),
        _const_spec((16, D_MODEL)),
        _const_spec((LANES, 2 * SSD_WIDTH)), _const_spec((LANES, 2 * SSD_WIDTH)),
        _const_spec((SSD_CHUNK, 3 * SSD_CHUNK)),
        pl.BlockSpec((1, 1, SSD_GROUPS, SSD_STATE, GROUP_COLS), lambda b, t: (b, 1, 0, 0, 0)),
        pl.BlockSpec((1, r1, D_MODEL), wrow3), pl.BlockSpec((1, r1, D_FF), wrow3),
        pl.BlockSpec((1, r4, D_MODEL), wrow3),
    ]
    return pl.pallas_call(
        _proj_kernel,
        grid=(BATCH, NT_PROJ),
        in_specs=in_specs,
        out_specs=out_specs,
        out_shape=out_shape,
        scratch_shapes=[pltpu.VMEM((SSD_GROUPS, SSD_STATE, GROUP_COLS), F32)],
        compiler_params=pltpu.CompilerParams(dimension_semantics=("arbitrary", "arbitrary"),
                                             vmem_limit_bytes=VMEM_LIMIT),
        name="proj",
    )(x, mod, lng, lnb, w_in, small, exe, exw, u_chunk, h0, w_out, w_ff1, w_ff2)


def _out_kernel(hln_ref, mod_ref, ycv_ref, zg_ref, xs_ref, bt_ref, cm_ref, rows_ref,
                cols_ref, colf_ref, sb_ref, h0_ref, exw_ref, small_ref, wout_ref,
                ln1g_ref, ln1b_ref, wff1_ref, wff2_ref, ln2g_ref, ln2b_ref,
                out_ref, st_ref, yn_ref, h1_ref, u2_ref, g2_ref, r2_ref, hid_ref):
    s = pl.program_id(0)
    wr = s % 2
    rd = (s + 1) % 2
    li = lax.broadcasted_iota(jnp.int32, (SSD_CHUNK, SSD_CHUNK), 0)
    si = lax.broadcasted_iota(jnp.int32, (SSD_CHUNK, SSD_CHUNK), 1)
    low = li >= si
    diag = li == si
    lo_half = si < SSD_HEADDIM

    def ssd_chunk(c, state, fill):
        tok = slice(c * SSD_CHUNK, (c + 1) * SSD_CHUNK)
        rows = rows_ref[0, c]
        cols = cols_ref[0, c]
        xs = xs_ref[0, tok, :]
        cm = cm_ref[0, tok, :]
        colf = colf_ref[0, c]
        bc = [jnp.broadcast_to(colf[:, COL_CSF + j:COL_CSF + j + 1], (SSD_CHUNK, LANES))
              for j in range(N_DH)]
        e1x = [jnp.concatenate(
            [jnp.exp(jnp.where(lo_half, bc[d * SSD_HEADS + 2 * k], bc[d * SSD_HEADS + 2 * k + 1]))
             for k in range(SSD_HEADS // 2)], axis=1) for d in range(N_DIRS)]
        wxf = jnp.dot(cols, exw_ref[:, :SSD_WIDTH], preferred_element_type=F32)
        dec = e1x[0][SSD_CHUNK - 1:SSD_CHUNK, :]
        gmat = [jnp.dot(cm[:, g * SSD_STATE:(g + 1) * SSD_STATE], bt_ref[0, c, g],
                        preferred_element_type=F32) for g in range(SSD_GROUPS)]
        fill[0]()

        ys = []
        for k in range(SSD_HEADS // 2):
            ms = []
            for h in (2 * k, 2 * k + 1):
                g = h // (SSD_HEADS // SSD_GROUPS)
                hb = SSD_HEADS + h
                arg = jnp.where(low,
                                bc[h] - rows[h:h + 1, :],
                                bc[hb] - rows[hb:hb + 1, :])
                dtf = rows[N_DH + h:N_DH + h + 1, :]
                dtb = rows[N_DH + hb:N_DH + hb + 1, :]
                fac = jnp.where(low, dtf, dtb) + jnp.where(diag, dtb, 0.0)
                ms.append((gmat[g] * jnp.exp(arg) * fac).astype(BF16))
            xp = xs[:, k * LANES:(k + 1) * LANES]
            rhs = jnp.concatenate([jnp.where(lo_half, xp, jnp.zeros_like(xp)),
                                   jnp.where(lo_half, jnp.zeros_like(xp), xp)], axis=0)
            ys.append(jnp.dot(jnp.concatenate(ms, axis=1), rhs, preferred_element_type=F32))
            fill[k + 1]()
        y = jnp.concatenate(ys, axis=1)

        yf, yb = [], []
        for g in range(SSD_GROUPS):
            cg = cm[:, g * SSD_STATE:(g + 1) * SSD_STATE]
            yf.append(jnp.dot(cg, state[g].astype(BF16), preferred_element_type=F32))
            yb.append(jnp.dot(cg, sb_ref[0, c, g], preferred_element_type=F32))
        y = (y + jnp.concatenate(yf, axis=1) * e1x[0]
             + jnp.concatenate(yb, axis=1) * e1x[1]
             + xs.astype(F32) * small_ref[S_DXNW:S_DXNW + 1, :SSD_WIDTH])

        yg = y * zg_ref[0, tok, :].astype(F32)
        ms_ = jnp.mean(yg * yg, axis=-1, keepdims=True)
        yn_ref[tok, :CONV_WIDTH] = ycv_ref[0, tok, :]
        yn_ref[tok, CONV_WIDTH:] = (yg * lax.rsqrt(ms_ + RMS_EPS)
                                   * small_ref[S_DXNW:S_DXNW + 1, SSD_WIDTH:]).astype(BF16)
        fill[5]()

        xw = (xs.astype(F32) * wxf).astype(BF16)
        new_state = []
        for g in range(SSD_GROUPS):
            gs = slice(g * GROUP_COLS, (g + 1) * GROUP_COLS)
            local = jnp.dot(bt_ref[0, c, g], xw[:, gs], preferred_element_type=F32)
            new_state.append(state[g] * dec[:, gs] + local)
        return new_state

    def mlp_up(j, k):
        blk = slice(j * FF_BLK + k * FF_SUB, j * FF_BLK + (k + 1) * FF_SUB)
        hid = jnp.maximum(jnp.dot(u2_ref[...], wff1_ref[:, blk], preferred_element_type=F32), 0.0)
        hid_ref[:, blk] = (hid * hid).astype(BF16)

    def mlp_down(j, acc):
        blk = slice(j * FF_BLK, (j + 1) * FF_BLK)
        part = jnp.dot(hid_ref[:, blk], wff2_ref[blk, :], preferred_element_type=F32)
        return part if acc is None else acc + part

    def ln2_rows(i):
        r = slice(i * SSD_CHUNK, (i + 1) * SSD_CHUNK)
        out_ref[0, r, :] = _ln_hat(r2_ref[r, :]) * ln2g_ref[...] + ln2b_ref[...]

    def step(mixer, mlp, norm):
        state, acc = None, [None]
        pieces = []
        if mlp:
            def down(j):
                acc[0] = mlp_down(j, acc[0])
            for j in range(N_SLAB):
                pieces += [(1, functools.partial(mlp_up, j, k)) for k in range(FF_BLK // FF_SUB)]
            for j in range(N_SLAB - DOWN_KEPT):
                at = (j + 1) * (FF_BLK // FF_SUB) + j + DOWN_LAG
                pieces.insert(at, (FF_BLK // FF_SUB, functools.partial(down, j)))
        n_slots = NCH * FILL_SLOTS
        per_slot = sum(cost for cost, _ in pieces) / n_slots
        slots, issued = [], 0.0
        for i in range(n_slots):
            mine = []
            while pieces and issued < (i + 1) * per_slot:
                cost, fn = pieces.pop(0)
                issued += cost
                mine.append(fn)
            slots.append(lambda mine=mine: [fn() for fn in mine])
        if mixer:
            @pl.when(jnp.minimum(s, N_TILES - 1) % NT == 0)
            def _():
                st_ref[...] = h0_ref[0, 0]
            state = [st_ref[g] for g in range(SSD_GROUPS)]
        for c in range(NCH):
            fill = slots[c * FILL_SLOTS:(c + 1) * FILL_SLOTS]
            if mixer:
                state = ssd_chunk(c, state, fill)
            else:
                for f in fill:
                    f()
            if norm:
                ln2_rows(c)
        if mixer:
            for g in range(SSD_GROUPS):
                st_ref[g] = state[g]
            mix = jnp.dot(yn_ref[...], wout_ref[...], preferred_element_type=F32)
            m = _mod_vectors(mod_ref, jnp.minimum(s, N_TILES - 1) // NT)
            g2_ref[wr] = m[5]
        for i in range(DOWN_KEPT):
            if mlp:
                down(N_SLAB - DOWN_KEPT + i)
            if mixer:
                for j in range(i * NCH // DOWN_KEPT, (i + 1) * NCH // DOWN_KEPT):
                    r = slice(j * SSD_CHUNK, (j + 1) * SSD_CHUNK)
                    h1 = _ln_hat(ALPHA * hln_ref[0, r, :] + m[2] * mix[r]) * ln1g_ref[...] + ln1b_ref[...]
                    h1_ref[wr, r, :] = h1
                    u2_ref[r, :] = (h1 * (1.0 + m[4]) + m[3]).astype(BF16)
        if mlp:
            r2_ref[...] = ALPHA * h1_ref[rd] + g2_ref[rd] * acc[0]

    @pl.when(s == 0)
    def _():
        r2_ref[...] = jnp.zeros((TM, D_MODEL), F32)
        step(True, False, False)

    @pl.when(jnp.logical_and(s >= 1, s <= N_TILES))
    def _():
        step(True, True, True)

    @pl.when(s == N_TILES + 1)
    def _():
        step(False, False, True)


def _out_call(hln, mod, ycv, zg, xs, bt, cm, rows, cols, colf, sb, h0, exw, small,
              wout, ln1g, ln1b, wff1, wff2, ln2g, ln2b):
    cur = lambda s: jnp.minimum(s, N_TILES - 1)
    fwd = lambda s: (cur(s) // NT, cur(s) % NT, 0)
    fwd4 = lambda s: (cur(s) // NT, cur(s) % NT, 0, 0)
    fwd5 = lambda s: (cur(s) // NT, cur(s) % NT, 0, 0, 0)
    done = lambda s: (jnp.maximum(s - 2, 0) // NT, jnp.maximum(s - 2, 0) % NT, 0)
    tok = lambda width: pl.BlockSpec((1, TM, width), fwd)
    in_specs = [
        tok(D_MODEL),
        _const_spec((16, 6 * D_MODEL)),
        tok(CONV_WIDTH), tok(SSD_WIDTH), tok(SSD_WIDTH),
        pl.BlockSpec((1, NCH, SSD_GROUPS, SSD_STATE, SSD_CHUNK), fwd5),
        tok(SSD_GN),
        pl.BlockSpec((1, NCH, 2 * N_DH, SSD_CHUNK), fwd4),
        pl.BlockSpec((1, NCH, SSD_CHUNK, LANES), fwd4),
        pl.BlockSpec((1, NCH, SSD_CHUNK, LANES), fwd4),
        pl.BlockSpec((1, NCH, SSD_GROUPS, SSD_STATE, GROUP_COLS), fwd5),
        pl.BlockSpec((1, 1, SSD_GROUPS, SSD_STATE, GROUP_COLS), lambda s: (cur(s) // NT, 0, 0, 0, 0)),
        _const_spec((LANES, 2 * SSD_WIDTH)),
        _const_spec((16, D_MODEL)),
        _const_spec((D_MODEL, D_MODEL)),
        _const_spec((1, D_MODEL)), _const_spec((1, D_MODEL)),
        _const_spec((D_MODEL, D_FF)), _const_spec((D_FF, D_MODEL)),
        _const_spec((1, D_MODEL)), _const_spec((1, D_MODEL)),
    ]
    return pl.pallas_call(
        _out_kernel,
        grid=(N_TILES + 2,),
        in_specs=in_specs,
        out_specs=pl.BlockSpec((1, TM, D_MODEL), done),
        out_shape=jax.ShapeDtypeStruct((BATCH, SEQ, D_MODEL), F32),
        scratch_shapes=[pltpu.VMEM((SSD_GROUPS, SSD_STATE, GROUP_COLS), F32),
                        pltpu.VMEM((TM, D_MODEL), BF16),
                        pltpu.VMEM((2, TM, D_MODEL), F32),
                        pltpu.VMEM((TM, D_MODEL), BF16),
                        pltpu.VMEM((2, 1, D_MODEL), F32),
                        pltpu.VMEM((TM, D_MODEL), F32),
                        pltpu.VMEM((TM, D_FF), BF16)],
        compiler_params=pltpu.CompilerParams(dimension_semantics=("arbitrary",),
                                             vmem_limit_bytes=VMEM_LIMIT),
        name="out",
    )(hln, mod, ycv, zg, xs, bt, cm, rows, cols, colf, sb, h0, exw, small,
      wout, ln1g, ln1b, wff1, wff2, ln2g, ln2b)


def kernel(x, c, ctx, c_ctx, ln_in_g, ln_in_b, w_mod, b_mod, w_in, conv_w, ssd_conv_w, ssd_conv_b,
           dt_bias, a_log, ssd_d, ssd_norm_w, w_out, ln1_g, ln1_b, w_ff1, w_ff2, ln2_g, ln2_b):
    row = lambda v: v.reshape(1, -1).astype(F32)
    cvec = jnp.concatenate([c, c_ctx[None, :], jnp.zeros((16 - BATCH - 1, D_MODEL), F32)], axis=0)
    mod, w_in_p = _prep_call(cvec, w_mod, b_mod, w_in[0].T)

    lng, lnb = row(ln_in_g), row(ln_in_b)
    pad_to = lambda v: jnp.pad(v, ((0, 0), (0, D_MODEL - v.shape[1])))
    small = jnp.concatenate([
        ssd_conv_w[0], row(ssd_conv_b[0]), pad_to(conv_w[0]),
        pad_to(jnp.concatenate([dt_bias[0].reshape(1, N_DH), a_log[0].reshape(1, N_DH)], axis=1)),
        jnp.concatenate([jnp.repeat(ssd_d[0], SSD_HEADDIM).reshape(1, SSD_WIDTH), row(ssd_norm_w[0])], axis=1),
        jnp.zeros((16 - S_DXNW - 1, D_MODEL), F32)], axis=0)
    exe = jnp.asarray(_EXE, BF16)
    exw = jnp.asarray(_EXW, BF16)

    h0 = _ctx_call(ctx, mod, lng, lnb, w_in_p, small, exw, jnp.asarray(_U_CTX, BF16))
    hln, ycv, zg, xs, bt, cm, rows, cols, colf, sb, w_out_p, w_ff1_p, w_ff2_p = _proj_call(
        x, mod, lng, lnb, w_in_p, small, exe, exw,
        jnp.asarray(_U_CHUNK, BF16), h0, w_out, w_ff1, w_ff2)
    return _out_call(hln, mod, ycv, zg, xs, bt, cm, rows, cols, colf, sb, h0, exw,
                     small, w_out_p, row(ln1_g[0]), row(ln1_b[0]),
                     w_ff1_p, w_ff2_p, row(ln2_g[0]), row(ln2_b[0]))
```

```python
import functools

import jax
import jax.numpy as jnp
import numpy as np
from jax import lax
from jax.experimental import pallas as pl
from jax.experimental.pallas import tpu as pltpu

F32 = jnp.float32
BF16 = jnp.bfloat16

D_MODEL = 1024
BATCH = 8
SEQ = 2048
CTX_LEN = 256
GRID_W = 64
CONV_WIDTH = 512
SSD_WIDTH = 512
SSD_HEADDIM = 64
SSD_HEADS = 8
SSD_GROUPS = 2
SSD_STATE = 128
SSD_CHUNK = 128
N_DIRS = 2
D_FF = 4 * D_MODEL
LN_EPS = 1e-5
RMS_EPS = 1e-5
SSD_GN = SSD_GROUPS * SSD_STATE
XBC_DIM = SSD_WIDTH + 2 * SSD_GN
Z_OFF = 3 * CONV_WIDTH
XBC_OFF = Z_OFF + SSD_WIDTH
DT_OFF = XBC_OFF + XBC_DIM
N_DH = N_DIRS * SSD_HEADS
LANES = 128
P_DT = 0
P_XBC = LANES
P_Z = P_XBC + XBC_DIM
P_CONV = P_Z + SSD_WIDTH
IN_PAD = P_CONV + 3 * CONV_WIDTH
CTX_PAD = P_XBC + SSD_WIDTH + SSD_GN
GROUP_COLS = (SSD_HEADS // SSD_GROUPS) * SSD_HEADDIM
ALPHA = 2.0 ** 0.25

TM = 512
CTX_ROWS = 2
NCH = TM // SSD_CHUNK
NT = SEQ // TM
NCHUNK = SEQ // SSD_CHUNK
N_TILES = BATCH * NT
TM_PROJ = 1024
NCH_PROJ = TM_PROJ // SSD_CHUNK
NT_PROJ = SEQ // TM_PROJ
FF_BLK = 1024
FF_SUB = 256
N_SLAB = D_FF // FF_BLK
FILL_SLOTS = 6
DOWN_KEPT = 2
DOWN_LAG = 1
VMEM_LIMIT = 58 * 1024 * 1024

COL_E1 = 48
COL_W = 80
COL_CSF = 112


def _expansion(col0, pieces, width):
    m = np.zeros((LANES, N_DH * width), np.float32)
    for t in range(pieces):
        for j in range(N_DH):
            m[col0 + 16 * t + j, j * width:(j + 1) * width] = 1.0
    return m


_EXE = _expansion(COL_E1, 2, SSD_HEADDIM)
_EXW = _expansion(COL_W, 2, SSD_HEADDIM)


def _ln_hat(x):
    mu = jnp.mean(x, axis=-1, keepdims=True)
    xc = x - mu
    var = jnp.mean(xc * xc, axis=-1, keepdims=True)
    return xc * lax.rsqrt(var + LN_EPS)


def _silu(x):
    return x / (1.0 + jnp.exp(-x))


def _softplus(x):
    return jnp.maximum(x, 0.0) + jnp.log1p(jnp.exp(-jnp.abs(x)))


def _edge_masks(rows, period):
    pos = lax.broadcasted_iota(jnp.int32, (rows, LANES), 0) % period
    return (pos != 0).astype(F32), (pos != period - 1).astype(F32)


def _conv3(t, w, mprev, mnext):
    rows = t.shape[0]
    prev = pltpu.roll(t, 1, 0) * mprev
    nxt = pltpu.roll(t, rows - 1, 0) * mnext
    return prev * w[0:1, :] + t * w[1:2, :] + nxt * w[2:3, :]


def _split(v, pieces):
    out = []
    for _ in range(pieces - 1):
        p = v.astype(BF16).astype(F32)
        out.append(p)
        v = v - p
    out.append(v.astype(BF16).astype(F32))
    return out


def _tri(length, op):
    i = np.arange(length)
    return op(i[:, None], i[None, :]).astype(np.float32)


_U_CHUNK = np.concatenate([_tri(SSD_CHUNK, np.less_equal), _tri(SSD_CHUNK, np.greater_equal),
                           np.ones((SSD_CHUNK, SSD_CHUNK), np.float32)], axis=1)
_U_CTX = np.concatenate([_tri(CTX_LEN, np.greater), _tri(CTX_LEN, np.less)], axis=1)


def _scan_mm(v, u_ref):
    pieces = jnp.concatenate(_split(v, 3), axis=0).astype(BF16)
    o = jnp.dot(pieces, u_ref[...], preferred_element_type=F32)
    return o[0:N_DH] + o[N_DH:2 * N_DH] + o[2 * N_DH:3 * N_DH]


def _dt_rows(raw, dtb, a_log):
    r = raw.T[0:N_DH, :] + dtb
    dt = _softplus(r)
    return dt, dt * (-jnp.exp(a_log))


def _fwd_rows():
    return lax.broadcasted_iota(jnp.int32, (N_DH, 1), 0) < SSD_HEADS


S_SCW = 0
S_SCB = 3
S_CW = 4
S_DT = 7
S_DXNW = 8


def _dt_params(small_ref):
    t = jnp.broadcast_to(small_ref[S_DT:S_DT + 1, 0:LANES], (N_DH, LANES))
    sub = lax.broadcasted_iota(jnp.int32, (N_DH, LANES), 0)
    lane = lax.broadcasted_iota(jnp.int32, (N_DH, LANES), 1)
    pick = lambda off: jnp.sum(jnp.where(lane == sub + off, t, 0.0), axis=1, keepdims=True)
    return pick(0), pick(N_DH)


def _mod_vectors(mod_ref, row):
    r = mod_ref[pl.ds(row, 1), :]
    return [r[:, k * D_MODEL:(k + 1) * D_MODEL] for k in range(6)]


def _const_spec(shape):
    nd = len(shape)
    return pl.BlockSpec(shape, lambda *_: (0,) * nd, pipeline_mode=pl.Buffered(1))


def _prep_kernel(c_ref, wmod_ref, bmod_ref, wint_ref, mod_ref, pin_ref):
    cs = _silu(c_ref[...]).astype(BF16)
    mod_ref[...] = jnp.dot(cs, wmod_ref[0].astype(BF16), preferred_element_type=F32) + bmod_ref[...]

    dt_rows = jnp.concatenate([wint_ref[DT_OFF:DT_OFF + N_DH, :],
                               jnp.zeros((LANES - N_DH, wint_ref.shape[1]), F32)], axis=0)
    pin_ref[:, P_DT:P_DT + LANES] = dt_rows.T.astype(BF16)
    def move(dst, src):
        pin_ref[:, dst:dst + LANES] = wint_ref[src:src + LANES, :].T.astype(BF16)

    for dst, src, width in ((P_XBC, XBC_OFF, XBC_DIM), (P_Z, Z_OFF, SSD_WIDTH)):
        for j in range(0, width, LANES):
            move(dst + j, src + j)
    for j in range(CONV_WIDTH // LANES):
        for k in range(3):
            move(P_CONV + (3 * j + k) * LANES, k * CONV_WIDTH + j * LANES)


def _prep_call(cvec, w_mod, b_mod, w_in_t):
    steps = 8
    r1, rm = D_MODEL // steps, 6 * D_MODEL // steps
    return pl.pallas_call(
        _prep_kernel,
        grid=(steps,),
        in_specs=[pl.BlockSpec((16, D_MODEL), lambda i: (0, 0)),
                  pl.BlockSpec((1, D_MODEL, rm), lambda i: (0, 0, i)),
                  pl.BlockSpec((1, rm), lambda i: (0, i)),
                  pl.BlockSpec((w_in_t.shape[0], r1), lambda i: (0, i))],
        out_specs=[pl.BlockSpec((16, rm), lambda i: (0, i)),
                   pl.BlockSpec((r1, IN_PAD), lambda i: (i, 0))],
        out_shape=[jax.ShapeDtypeStruct((16, 6 * D_MODEL), F32),
                   jax.ShapeDtypeStruct((D_MODEL, IN_PAD), BF16)],
        compiler_params=pltpu.CompilerParams(dimension_semantics=("arbitrary",),
                                             vmem_limit_bytes=VMEM_LIMIT),
        name="prep",
    )(cvec, w_mod, b_mod, w_in_t)


def _ctx_kernel(x_ref, mod_ref, lng_ref, lnb_ref, w_ref, small_ref, exw_ref, u_ref, h0_ref):
    dtb, alog = _dt_params(small_ref)
    m = _mod_vectors(mod_ref, BATCH)
    sc = 1.0 + m[1]
    x = x_ref[...].reshape(CTX_ROWS * CTX_LEN, D_MODEL)
    u = _ln_hat(x) * (lng_ref[...] * sc) + (lnb_ref[...] * sc + m[0])
    proj = jnp.dot(u.astype(BF16), w_ref[...], preferred_element_type=F32)
    mprev, mnext = _edge_masks(CTX_ROWS * CTX_LEN, CTX_LEN)
    slabs = []
    for j in range((SSD_WIDTH + SSD_GN) // LANES):
        sl = slice(j * LANES, (j + 1) * LANES)
        pj = proj[:, P_XBC + j * LANES:P_XBC + (j + 1) * LANES]
        slabs.append(_silu(_conv3(pj, small_ref[S_SCW:S_SCW + 3, sl], mprev, mnext)
                           + small_ref[S_SCB:S_SCB + 1, sl]))
    zero = jnp.zeros((N_DH, CTX_LEN), F32)
    for i in range(CTX_ROWS):
        tok = slice(i * CTX_LEN, (i + 1) * CTX_LEN)
        xs = jnp.concatenate([sl_[tok] for sl_ in slabs[:4]], axis=1)
        dt, adt = _dt_rows(proj[tok, P_DT:P_DT + LANES], dtb, alog)
        sc2 = _scan_mm(adt, u_ref)
        excl = jnp.where(_fwd_rows(), sc2[:, :CTX_LEN], sc2[:, CTX_LEN:])
        w = jnp.exp(excl) * dt
        table = jnp.concatenate([zero] * (COL_W // 16) + _split(w, 2) + [zero], axis=0)
        cols = table.T.astype(BF16)
        wx = jnp.dot(cols, exw_ref[...], preferred_element_type=F32)
        for d in range(N_DIRS):
            xw = (xs * wx[:, d * SSD_WIDTH:(d + 1) * SSD_WIDTH]).astype(BF16)
            for g in range(SSD_GROUPS):
                bt = slabs[4 + g][tok].T.astype(BF16)
                h0_ref[i, d, g] = jnp.dot(bt, xw[:, g * GROUP_COLS:(g + 1) * GROUP_COLS],
                                          preferred_element_type=F32)


def _ctx_call(ctx, mod, lng, lnb, w_in, small, exw, u_ctx):
    return pl.pallas_call(
        _ctx_kernel,
        grid=(BATCH // CTX_ROWS,),
        in_specs=[pl.BlockSpec((CTX_ROWS, CTX_LEN, D_MODEL), lambda b: (b, 0, 0)),
                  _const_spec((16, 6 * D_MODEL)), _const_spec((1, D_MODEL)), _const_spec((1, D_MODEL)),
                  _const_spec((D_MODEL, CTX_PAD)), _const_spec((16, D_MODEL)),
                  _const_spec((LANES, 2 * SSD_WIDTH)),
                  _const_spec((CTX_LEN, 2 * CTX_LEN))],
        out_specs=pl.BlockSpec((CTX_ROWS, N_DIRS, SSD_GROUPS, SSD_STATE, GROUP_COLS),
                               lambda b: (b, 0, 0, 0, 0)),
        out_shape=jax.ShapeDtypeStruct((BATCH, N_DIRS, SSD_GROUPS, SSD_STATE, GROUP_COLS), F32),
        compiler_params=pltpu.CompilerParams(dimension_semantics=("arbitrary",),
                                             vmem_limit_bytes=VMEM_LIMIT),
        name="ctx",
    )(ctx, mod, lng, lnb, w_in, small, exw, u_ctx)


def _proj_kernel(x_ref, mod_ref, lng_ref, lnb_ref, w_ref, small_ref,
                 exe_ref, exw_ref, u_ref, h0_ref, wout_ref, wff1_ref, wff2_ref,
                 hln_ref, ycv_ref, zg_ref, xs_ref, bt_ref, cm_ref, rows_ref, cols_ref, colf_ref, sb_ref,
                 pout_ref, pff1_ref, pff2_ref, st_ref):
    pout_ref[...] = wout_ref[0].astype(BF16)
    pff1_ref[...] = wff1_ref[0].astype(BF16)
    pff2_ref[...] = wff2_ref[0].astype(BF16)

    @pl.when(pl.program_id(1) == 0)
    def _():
        st_ref[...] = h0_ref[0, 0]

    m = _mod_vectors(mod_ref, pl.program_id(0))
    sc = 1.0 + m[1]
    scale, shift = lng_ref[...] * sc, lnb_ref[...] * sc + m[0]
    ubs, pas = [], []
    for r in range(2):
        rows = slice(r * (TM_PROJ // 2), (r + 1) * (TM_PROJ // 2))
        xhat = _ln_hat(x_ref[0, rows, :])
        hln_ref[0, rows, :] = xhat * lng_ref[...] + lnb_ref[...]
        ubs.append((xhat * scale + shift).astype(BF16))
        pas.append(jnp.dot(ubs[r], w_ref[:, :P_Z], preferred_element_type=F32))
    ub = jnp.concatenate(ubs, axis=0)
    pa = jnp.concatenate(pas, axis=0)
    mprev, mnext = _edge_masks(TM_PROJ, GRID_W)
    is_fwd = _fwd_rows()
    zero = jnp.zeros((N_DH, SSD_CHUNK), F32)
    chunks = [slice(c * SSD_CHUNK, (c + 1) * SSD_CHUNK) for c in range(NCH_PROJ)]

    def conv_slabs(p, first):
        for i in range(2):
            gb, gc, gh = (p[:, (3 * i + k) * LANES:(3 * i + k + 1) * LANES] for k in range(3))
            sl = slice((first + i) * LANES, (first + i + 1) * LANES)
            ycv_ref[0, :, sl] = (gb * _conv3(gc * gh, small_ref[S_CW:S_CW + 3, sl], mprev, mnext)).astype(BF16)

    dtb, alog = _dt_params(small_ref)
    dts = [_dt_rows(pa[tok, P_DT:P_DT + LANES], dtb, alog) for tok in chunks]
    pz = jnp.dot(ub, w_ref[:, P_Z:P_CONV], preferred_element_type=F32)

    colss = []
    for c, (dt, adt) in enumerate(dts):
        sc3 = _scan_mm(adt, u_ref)
        cs = jnp.where(is_fwd, sc3[:, :SSD_CHUNK], sc3[:, SSD_CHUNK:2 * SSD_CHUNK])
        tot = sc3[:, 2 * SSD_CHUNK:]
        e1 = jnp.exp(cs)
        w = jnp.exp(tot - cs) * dt
        rows_ref[0, c] = jnp.concatenate([cs, dt], axis=0)
        table = jnp.concatenate([zero] * 3 + _split(e1, 2) + _split(w, 2) + [cs], axis=0).T
        colf_ref[0, c] = table
        colss.append(table.astype(BF16))
        cols_ref[0, c] = colss[c]

    pc1 = jnp.dot(ub, w_ref[:, P_CONV:P_CONV + 6 * LANES], preferred_element_type=F32)

    slabs = []
    for j in range(XBC_DIM // LANES):
        sl = slice(j * LANES, (j + 1) * LANES)
        pj = pa[:, P_XBC + j * LANES:P_XBC + (j + 1) * LANES]
        slabs.append(_silu(_conv3(pj, small_ref[S_SCW:S_SCW + 3, sl], mprev, mnext)
                           + small_ref[S_SCB:S_SCB + 1, sl]))
    xs = jnp.concatenate(slabs[:4], axis=1)
    xs_ref[0] = xs.astype(BF16)
    cm_ref[0] = jnp.concatenate(slabs[6:8], axis=1).astype(BF16)
    zg_ref[0] = _silu(pz).astype(BF16)
    bts = []
    for c, tok in enumerate(chunks):
        bt = [slabs[4 + g][tok].T.astype(BF16) for g in range(SSD_GROUPS)]
        for g in range(SSD_GROUPS):
            bt_ref[0, c, g] = bt[g]
        bts.append(bt)
    wxbs = [jnp.dot(cols, exw_ref[:, SSD_WIDTH:], preferred_element_type=F32) for cols in colss]
    decs = [jnp.dot(cols[0:16], exe_ref[:, SSD_WIDTH:], preferred_element_type=F32)[0:1]
            for cols in colss]

    pc2 = jnp.dot(ub, w_ref[:, P_CONV + 6 * LANES:], preferred_element_type=F32)
    conv_slabs(pc1, 0)

    state = [st_ref[g] for g in range(SSD_GROUPS)]
    for c in reversed(range(NCH_PROJ)):
        xw = (xs[chunks[c]] * wxbs[c]).astype(BF16)
        for g in range(SSD_GROUPS):
            gs = slice(g * GROUP_COLS, (g + 1) * GROUP_COLS)
            sb_ref[0, c, g] = state[g].astype(BF16)
            local = jnp.dot(bts[c][g], xw[:, gs], preferred_element_type=F32)
            state[g] = state[g] * decs[c][:, gs] + local
    for g in range(SSD_GROUPS):
        st_ref[g] = state[g]
    conv_slabs(pc2, 2)


def _proj_call(x, mod, lng, lnb, w_in, small, exe, exw, u_chunk, h0, w_out, w_ff1, w_ff2):
    n_steps = BATCH * NT_PROJ
    r1, r4 = D_MODEL // n_steps, D_FF // n_steps
    wrow3 = lambda b, t: (0, b * NT_PROJ + t, 0)
    wrow2 = lambda b, t: (b * NT_PROJ + t, 0)
    rev = lambda b, t: (b, NT_PROJ - 1 - t, 0)
    rev4 = lambda b, t: (b, NT_PROJ - 1 - t, 0, 0)
    rev5 = lambda b, t: (b, NT_PROJ - 1 - t, 0, 0, 0)
    tok = lambda width: pl.BlockSpec((1, TM_PROJ, width), rev)
    out_shape = [
        jax.ShapeDtypeStruct((BATCH, SEQ, D_MODEL), F32),
        jax.ShapeDtypeStruct((BATCH, SEQ, CONV_WIDTH), BF16),
        jax.ShapeDtypeStruct((BATCH, SEQ, SSD_WIDTH), BF16),
        jax.ShapeDtypeStruct((BATCH, SEQ, SSD_WIDTH), BF16),
        jax.ShapeDtypeStruct((BATCH, NCHUNK, SSD_GROUPS, SSD_STATE, SSD_CHUNK), BF16),
        jax.ShapeDtypeStruct((BATCH, SEQ, SSD_GN), BF16),
        jax.ShapeDtypeStruct((BATCH, NCHUNK, 2 * N_DH, SSD_CHUNK), F32),
        jax.ShapeDtypeStruct((BATCH, NCHUNK, SSD_CHUNK, LANES), BF16),
        jax.ShapeDtypeStruct((BATCH, NCHUNK, SSD_CHUNK, LANES), F32),
        jax.ShapeDtypeStruct((BATCH, NCHUNK, SSD_GROUPS, SSD_STATE, GROUP_COLS), BF16),
        jax.ShapeDtypeStruct((D_MODEL, D_MODEL), BF16),
        jax.ShapeDtypeStruct((D_MODEL, D_FF), BF16),
        jax.ShapeDtypeStruct((D_FF, D_MODEL), BF16),
    ]
    out_specs = [
        tok(D_MODEL), tok(CONV_WIDTH), tok(SSD_WIDTH), tok(SSD_WIDTH),
        pl.BlockSpec((1, NCH_PROJ, SSD_GROUPS, SSD_STATE, SSD_CHUNK), rev5),
        tok(SSD_GN),
        pl.BlockSpec((1, NCH_PROJ, 2 * N_DH, SSD_CHUNK), rev4),
        pl.BlockSpec((1, NCH_PROJ, SSD_CHUNK, LANES), rev4),
        pl.BlockSpec((1, NCH_PROJ, SSD_CHUNK, LANES), rev4),
        pl.BlockSpec((1, NCH_PROJ, SSD_GROUPS, SSD_STATE, GROUP_COLS), rev5),
        pl.BlockSpec((r1, D_MODEL), wrow2), pl.BlockSpec((r1, D_FF), wrow2), pl.BlockSpec((r4, D_MODEL), wrow2),
    ]
    in_specs = [
        pl.BlockSpec((1, TM_PROJ, D_MODEL), rev),
        _const_spec((16, 6 * D_MODEL)),
        _const_spec((1, D_MODEL)), _const_spec((1, D_MODEL)),
        _const_spec((D_MODEL, IN_PAD)),
        _const_spec((16, D_MODEL)),
        _const_spec((LANES, 2 * SSD_WIDTH)), _const_spec((LANES, 2 * SSD_WIDTH)),
        _const_spec((SSD_CHUNK, 3 * SSD_CHUNK)),
        pl.BlockSpec((1, 1, SSD_GROUPS, SSD_STATE, GROUP_COLS), lambda b, t: (b, 1, 0, 0, 0)),
        pl.BlockSpec((1, r1, D_MODEL), wrow3), pl.BlockSpec((1, r1, D_FF), wrow3),
        pl.BlockSpec((1, r4, D_MODEL), wrow3),
    ]
    return pl.pallas_call(
        _proj_kernel,
        grid=(BATCH, NT_PROJ),
        in_specs=in_specs,
        out_specs=out_specs,
        out_shape=out_shape,
        scratch_shapes=[pltpu.VMEM((SSD_GROUPS, SSD_STATE, GROUP_COLS), F32)],
        compiler_params=pltpu.CompilerParams(dimension_semantics=("arbitrary", "arbitrary"),
                                             vmem_limit_bytes=VMEM_LIMIT),
        name="proj",
    )(x, mod, lng, lnb, w_in, small, exe, exw, u_chunk, h0, w_out, w_ff1, w_ff2)


def _out_kernel(hln_ref, mod_ref, ycv_ref, zg_ref, xs_ref, bt_ref, cm_ref, rows_ref,
                cols_ref, colf_ref, sb_ref, h0_ref, exw_ref, small_ref, wout_ref,
                ln1g_ref, ln1b_ref, wff1_ref, wff2_ref, ln2g_ref, ln2b_ref,
                out_ref, st_ref, yn_ref, h1_ref, u2_ref, g2_ref, r2_ref, hid_ref):
    s = pl.program_id(0)
    wr = s % 2
    rd = (s + 1) % 2
    li = lax.broadcasted_iota(jnp.int32, (SSD_CHUNK, SSD_CHUNK), 0)
    si = lax.broadcasted_iota(jnp.int32, (SSD_CHUNK, SSD_CHUNK), 1)
    low = li >= si
    diag = li == si
    lo_half = si < SSD_HEADDIM

    def ssd_chunk(c, state, fill):
        tok = slice(c * SSD_CHUNK, (c + 1) * SSD_CHUNK)
        rows = rows_ref[0, c]
        cols = cols_ref[0, c]
        xs = xs_ref[0, tok, :]
        cm = cm_ref[0, tok, :]
        colf = colf_ref[0, c]
        bc = [jnp.broadcast_to(colf[:, COL_CSF + j:COL_CSF + j + 1], (SSD_CHUNK, LANES))
              for j in range(N_DH)]
        e1x = [jnp.concatenate(
            [jnp.exp(jnp.where(lo_half, bc[d * SSD_HEADS + 2 * k], bc[d * SSD_HEADS + 2 * k + 1]))
             for k in range(SSD_HEADS // 2)], axis=1) for d in range(N_DIRS)]
        wxf = jnp.dot(cols, exw_ref[:, :SSD_WIDTH], preferred_element_type=F32)
        dec = e1x[0][SSD_CHUNK - 1:SSD_CHUNK, :]
        gmat = [jnp.dot(cm[:, g * SSD_STATE:(g + 1) * SSD_STATE], bt_ref[0, c, g],
                        preferred_element_type=F32) for g in range(SSD_GROUPS)]
        fill[0]()

        ys = []
        for k in range(SSD_HEADS // 2):
            ms = []
            for h in (2 * k, 2 * k + 1):
                g = h // (SSD_HEADS // SSD_GROUPS)
                hb = SSD_HEADS + h
                arg = jnp.where(low,
                                bc[h] - rows[h:h + 1, :],
                                bc[hb] - rows[hb:hb + 1, :])
                dtf = rows[N_DH + h:N_DH + h + 1, :]
                dtb = rows[N_DH + hb:N_DH + hb + 1, :]
                fac = jnp.where(low, dtf, dtb) + jnp.where(diag, dtb, 0.0)
                ms.append((gmat[g] * jnp.exp(arg) * fac).astype(BF16))
            xp = xs[:, k * LANES:(k + 1) * LANES]
            rhs = jnp.concatenate([jnp.where(lo_half, xp, jnp.zeros_like(xp)),
                                   jnp.where(lo_half, jnp.zeros_like(xp), xp)], axis=0)
            ys.append(jnp.dot(jnp.concatenate(ms, axis=1), rhs, preferred_element_type=F32))
            fill[k + 1]()
        y = jnp.concatenate(ys, axis=1)

        yf, yb = [], []
        for g in range(SSD_GROUPS):
            cg = cm[:, g * SSD_STATE:(g + 1) * SSD_STATE]
            yf.append(jnp.dot(cg, state[g].astype(BF16), preferred_element_type=F32))
            yb.append(jnp.dot(cg, sb_ref[0, c, g], preferred_element_type=F32))
        y = (y + jnp.concatenate(yf, axis=1) * e1x[0]
             + jnp.concatenate(yb, axis=1) * e1x[1]
             + xs.astype(F32) * small_ref[S_DXNW:S_DXNW + 1, :SSD_WIDTH])

        yg = y * zg_ref[0, tok, :].astype(F32)
        ms_ = jnp.mean(yg * yg, axis=-1, keepdims=True)
        yn_ref[tok, :CONV_WIDTH] = ycv_ref[0, tok, :]
        yn_ref[tok, CONV_WIDTH:] = (yg * lax.rsqrt(ms_ + RMS_EPS)
                                   * small_ref[S_DXNW:S_DXNW + 1, SSD_WIDTH:]).astype(BF16)
        fill[5]()

        xw = (xs.astype(F32) * wxf).astype(BF16)
        new_state = []
        for g in range(SSD_GROUPS):
            gs = slice(g * GROUP_COLS, (g + 1) * GROUP_COLS)
            local = jnp.dot(bt_ref[0, c, g], xw[:, gs], preferred_element_type=F32)
            new_state.append(state[g] * dec[:, gs] + local)
        return new_state

    def mlp_up(j, k):
        blk = slice(j * FF_BLK + k * FF_SUB, j * FF_BLK + (k + 1) * FF_SUB)
        hid = jnp.maximum(jnp.dot(u2_ref[...], wff1_ref[:, blk], preferred_element_type=F32), 0.0)
        hid_ref[:, blk] = (hid * hid).astype(BF16)

    def mlp_down(j, acc):
        blk = slice(j * FF_BLK, (j + 1) * FF_BLK)
        part = jnp.dot(hid_ref[:, blk], wff2_ref[blk, :], preferred_element_type=F32)
        return part if acc is None else acc + part

    def ln2_rows(i):
        r = slice(i * SSD_CHUNK, (i + 1) * SSD_CHUNK)
        out_ref[0, r, :] = _ln_hat(r2_ref[r, :]) * ln2g_ref[...] + ln2b_ref[...]

    def step(mixer, mlp, norm):
        state, acc = None, [None]
        pieces = []
        if mlp:
            def down(j):
                acc[0] = mlp_down(j, acc[0])
            for j in range(N_SLAB):
                pieces += [(1, functools.partial(mlp_up, j, k)) for k in range(FF_BLK // FF_SUB)]
            for j in range(N_SLAB - DOWN_KEPT):
                at = (j + 1) * (FF_BLK // FF_SUB) + j + DOWN_LAG
                pieces.insert(at, (FF_BLK // FF_SUB, functools.partial(down, j)))
        n_slots = NCH * FILL_SLOTS
        per_slot = sum(cost for cost, _ in pieces) / n_slots
        slots, issued = [], 0.0
        for i in range(n_slots):
            mine = []
            while pieces and issued < (i + 1) * per_slot:
                cost, fn = pieces.pop(0)
                issued += cost
                mine.append(fn)
            slots.append(lambda mine=mine: [fn() for fn in mine])
        if mixer:
            @pl.when(jnp.minimum(s, N_TILES - 1) % NT == 0)
            def _():
                st_ref[...] = h0_ref[0, 0]
            state = [st_ref[g] for g in range(SSD_GROUPS)]
        for c in range(NCH):
            fill = slots[c * FILL_SLOTS:(c + 1) * FILL_SLOTS]
            if mixer:
                state = ssd_chunk(c, state, fill)
            else:
                for f in fill:
                    f()
            if norm:
                ln2_rows(c)
        if mixer:
            for g in range(SSD_GROUPS):
                st_ref[g] = state[g]
            mix = jnp.dot(yn_ref[...], wout_ref[...], preferred_element_type=F32)
            m = _mod_vectors(mod_ref, jnp.minimum(s, N_TILES - 1) // NT)
            g2_ref[wr] = m[5]
        for i in range(DOWN_KEPT):
            if mlp:
                down(N_SLAB - DOWN_KEPT + i)
            if mixer:
                for j in range(i * NCH // DOWN_KEPT, (i + 1) * NCH // DOWN_KEPT):
                    r = slice(j * SSD_CHUNK, (j + 1) * SSD_CHUNK)
                    h1 = _ln_hat(ALPHA * hln_ref[0, r, :] + m[2] * mix[r]) * ln1g_ref[...] + ln1b_ref[...]
                    h1_ref[wr, r, :] = h1
                    u2_ref[r, :] = (h1 * (1.0 + m[4]) + m[3]).astype(BF16)
        if mlp:
            r2_ref[...] = ALPHA * h1_ref[rd] + g2_ref[rd] * acc[0]

    @pl.when(s == 0)
    def _():
        r2_ref[...] = jnp.zeros((TM, D_MODEL), F32)
        step(True, False, False)

    @pl.when(jnp.logical_and(s >= 1, s < N_TILES))
    def _():
        step(True, True, True)

    @pl.when(s == N_TILES)
    def _():
        step(False, True, True)

    @pl.when(s == N_TILES + 1)
    def _():
        step(False, False, True)


def _out_call(hln, mod, ycv, zg, xs, bt, cm, rows, cols, colf, sb, h0, exw, small,
              wout, ln1g, ln1b, wff1, wff2, ln2g, ln2b):
    cur = lambda s: jnp.minimum(s, N_TILES - 1)
    fwd = lambda s: (cur(s) // NT, cur(s) % NT, 0)
    fwd4 = lambda s: (cur(s) // NT, cur(s) % NT, 0, 0)
    fwd5 = lambda s: (cur(s) // NT, cur(s) % NT, 0, 0, 0)
    done = lambda s: (jnp.maximum(s - 2, 0) // NT, jnp.maximum(s - 2, 0) % NT, 0)
    tok = lambda width: pl.BlockSpec((1, TM, width), fwd)
    in_specs = [
        tok(D_MODEL),
        _const_spec((16, 6 * D_MODEL)),
        tok(CONV_WIDTH), tok(SSD_WIDTH), tok(SSD_WIDTH),
        pl.BlockSpec((1, NCH, SSD_GROUPS, SSD_STATE, SSD_CHUNK), fwd5),
        tok(SSD_GN),
        pl.BlockSpec((1, NCH, 2 * N_DH, SSD_CHUNK), fwd4),
        pl.BlockSpec((1, NCH, SSD_CHUNK, LANES), fwd4),
        pl.BlockSpec((1, NCH, SSD_CHUNK, LANES), fwd4),
        pl.BlockSpec((1, NCH, SSD_GROUPS, SSD_STATE, GROUP_COLS), fwd5),
        pl.BlockSpec((1, 1, SSD_GROUPS, SSD_STATE, GROUP_COLS), lambda s: (cur(s) // NT, 0, 0, 0, 0)),
        _const_spec((LANES, 2 * SSD_WIDTH)),
        _const_spec((16, D_MODEL)),
        _const_spec((D_MODEL, D_MODEL)),
        _const_spec((1, D_MODEL)), _const_spec((1, D_MODEL)),
        _const_spec((D_MODEL, D_FF)), _const_spec((D_FF, D_MODEL)),
        _const_spec((1, D_MODEL)), _const_spec((1, D_MODEL)),
    ]
    return pl.pallas_call(
        _out_kernel,
        grid=(N_TILES + 2,),
        in_specs=in_specs,
        out_specs=pl.BlockSpec((1, TM, D_MODEL), done),
        out_shape=jax.ShapeDtypeStruct((BATCH, SEQ, D_MODEL), F32),
        scratch_shapes=[pltpu.VMEM((SSD_GROUPS, SSD_STATE, GROUP_COLS), F32),
                        pltpu.VMEM((TM, D_MODEL), BF16),
                        pltpu.VMEM((2, TM, D_MODEL), F32),
                        pltpu.VMEM((TM, D_MODEL), BF16),
                        pltpu.VMEM((2, 1, D_MODEL), F32),
                        pltpu.VMEM((TM, D_MODEL), F32),
                        pltpu.VMEM((TM, D_FF), BF16)],
        compiler_params=pltpu.CompilerParams(dimension_semantics=("arbitrary",),
                                             vmem_limit_bytes=VMEM_LIMIT),
        name="out",
    )(hln, mod, ycv, zg, xs, bt, cm, rows, cols, colf, sb, h0, exw, small,
      wout, ln1g, ln1b, wff1, wff2, ln2g, ln2b)


def kernel(x, c, ctx, c_ctx, ln_in_g, ln_in_b, w_mod, b_mod, w_in, conv_w, ssd_conv_w, ssd_conv_b,
           dt_bias, a_log, ssd_d, ssd_norm_w, w_out, ln1_g, ln1_b, w_ff1, w_ff2, ln2_g, ln2_b):
    row = lambda v: v.reshape(1, -1).astype(F32)
    cvec = jnp.concatenate([c, c_ctx[None, :], jnp.zeros((16 - BATCH - 1, D_MODEL), F32)], axis=0)
    mod, w_in_p = _prep_call(cvec, w_mod, b_mod, w_in[0].T)

    lng, lnb = row(ln_in_g), row(ln_in_b)
    pad_to = lambda v: jnp.pad(v, ((0, 0), (0, D_MODEL - v.shape[1])))
    small = jnp.concatenate([
        ssd_conv_w[0], row(ssd_conv_b[0]), pad_to(conv_w[0]),
        pad_to(jnp.concatenate([dt_bias[0].reshape(1, N_DH), a_log[0].reshape(1, N_DH)], axis=1)),
        jnp.concatenate([jnp.repeat(ssd_d[0], SSD_HEADDIM).reshape(1, SSD_WIDTH), row(ssd_norm_w[0])], axis=1),
        jnp.zeros((16 - S_DXNW - 1, D_MODEL), F32)], axis=0)
    exe = jnp.asarray(_EXE, BF16)
    exw = jnp.asarray(_EXW, BF16)

    h0 = _ctx_call(ctx, mod, lng, lnb, w_in_p, small, exw, jnp.asarray(_U_CTX, BF16))
    hln, ycv, zg, xs, bt, cm, rows, cols, colf, sb, w_out_p, w_ff1_p, w_ff2_p = _proj_call(
        x, mod, lng, lnb, w_in_p, small, exe, exw,
        jnp.asarray(_U_CHUNK, BF16), h0, w_out, w_ff1, w_ff2)
    return _out_call(hln, mod, ycv, zg, xs, bt, cm, rows, cols, colf, sb, h0, exw,
                     small, w_out_p, row(ln1_g[0]), row(ln1_b[0]),
                     w_ff1_p, w_ff2_p, row(ln2_g[0]), row(ln2_b[0]))
```

```python
import functools

import jax
import jax.numpy as jnp
import numpy as np
from jax import lax
from jax.experimental import pallas as pl
from jax.experimental.pallas import tpu as pltpu

F32 = jnp.float32
BF16 = jnp.bfloat16

D_MODEL = 1024
BATCH = 8
SEQ = 2048
CTX_LEN = 256
GRID_W = 64
CONV_WIDTH = 512
SSD_WIDTH = 512
SSD_HEADDIM = 64
SSD_HEADS = 8
SSD_GROUPS = 2
SSD_STATE = 128
SSD_CHUNK = 128
N_DIRS = 2
D_FF = 4 * D_MODEL
LN_EPS = 1e-5
RMS_EPS = 1e-5
SSD_GN = SSD_GROUPS * SSD_STATE
XBC_DIM = SSD_WIDTH + 2 * SSD_GN
Z_OFF = 3 * CONV_WIDTH
XBC_OFF = Z_OFF + SSD_WIDTH
DT_OFF = XBC_OFF + XBC_DIM
N_DH = N_DIRS * SSD_HEADS
LANES = 128
P_DT = 0
P_XBC = LANES
P_Z = P_XBC + XBC_DIM
P_CONV = P_Z + SSD_WIDTH
IN_PAD = P_CONV + 3 * CONV_WIDTH
CTX_PAD = P_XBC + SSD_WIDTH + SSD_GN
GROUP_COLS = (SSD_HEADS // SSD_GROUPS) * SSD_HEADDIM
ALPHA = 2.0 ** 0.25

TM = 512
CTX_ROWS = 4
MOD_ROWS = 16
SMALL_ROWS = 16
PREP_STEPS = 4
NCH = TM // SSD_CHUNK
NT = SEQ // TM
NCHUNK = SEQ // SSD_CHUNK
N_TILES = BATCH * NT
TM_PROJ = 1024
NCH_PROJ = TM_PROJ // SSD_CHUNK
NT_PROJ = SEQ // TM_PROJ
FF_BLK = 1024
FF_SUB = 256
N_SLAB = D_FF // FF_BLK
FILL_SLOTS = 6
DOWN_KEPT = 2
DOWN_LAG = 1
VMEM_LIMIT = 58 * 1024 * 1024

COL_E1 = 48
COL_W = 80
COL_CSF = 112


def _expansion(col0, pieces, width):
    m = np.zeros((LANES, N_DH * width), np.float32)
    for t in range(pieces):
        for j in range(N_DH):
            m[col0 + 16 * t + j, j * width:(j + 1) * width] = 1.0
    return m


_EXE = _expansion(COL_E1, 2, SSD_HEADDIM)
_EXW = _expansion(COL_W, 2, SSD_HEADDIM)


def _ln_hat(x):
    mu = jnp.mean(x, axis=-1, keepdims=True)
    xc = x - mu
    var = jnp.mean(xc * xc, axis=-1, keepdims=True)
    return xc * lax.rsqrt(var + LN_EPS)


def _silu(x):
    return x / (1.0 + jnp.exp(-x))


def _softplus(x):
    return jnp.maximum(x, 0.0) + jnp.log1p(jnp.exp(-jnp.abs(x)))


def _edge_masks(rows, period):
    pos = lax.broadcasted_iota(jnp.int32, (rows, LANES), 0) % period
    return (pos != 0).astype(F32), (pos != period - 1).astype(F32)


def _conv3(t, w, mprev, mnext):
    rows = t.shape[0]
    prev = pltpu.roll(t, 1, 0) * mprev
    nxt = pltpu.roll(t, rows - 1, 0) * mnext
    return prev * w[0:1, :] + t * w[1:2, :] + nxt * w[2:3, :]


def _split(v, pieces):
    out = []
    for _ in range(pieces - 1):
        p = v.astype(BF16).astype(F32)
        out.append(p)
        v = v - p
    out.append(v.astype(BF16).astype(F32))
    return out


def _tri(length, op):
    i = np.arange(length)
    return op(i[:, None], i[None, :]).astype(np.float32)


_U_CHUNK = np.concatenate([_tri(SSD_CHUNK, np.less_equal), _tri(SSD_CHUNK, np.greater_equal),
                           np.ones((SSD_CHUNK, SSD_CHUNK), np.float32)], axis=1)
_U_CTX = np.concatenate([_tri(CTX_LEN, np.greater), _tri(CTX_LEN, np.less)], axis=1)


def _scan_mm(v, u_ref):
    pieces = jnp.concatenate(_split(v, 3), axis=0).astype(BF16)
    o = jnp.dot(pieces, u_ref[...], preferred_element_type=F32)
    return o[0:N_DH] + o[N_DH:2 * N_DH] + o[2 * N_DH:3 * N_DH]


def _dt_rows(raw, dtb, a_log):
    r = raw.T[0:N_DH, :] + dtb
    dt = _softplus(r)
    return dt, dt * (-jnp.exp(a_log))


def _fwd_rows():
    return lax.broadcasted_iota(jnp.int32, (N_DH, 1), 0) < SSD_HEADS


S_SCW = 0
S_SCB = 3
S_CW = 4
S_DT = 7
S_DXNW = 8


def _dt_params(small_ref):
    t = jnp.broadcast_to(small_ref[S_DT:S_DT + 1, 0:LANES], (N_DH, LANES))
    sub = lax.broadcasted_iota(jnp.int32, (N_DH, LANES), 0)
    lane = lax.broadcasted_iota(jnp.int32, (N_DH, LANES), 1)
    pick = lambda off: jnp.sum(jnp.where(lane == sub + off, t, 0.0), axis=1, keepdims=True)
    return pick(0), pick(N_DH)


def _mod_vectors(mod_ref, row):
    r = mod_ref[pl.ds(row, 1), :]
    return [r[:, k * D_MODEL:(k + 1) * D_MODEL] for k in range(6)]


def _const_spec(shape):
    nd = len(shape)
    return pl.BlockSpec(shape, lambda *_: (0,) * nd, pipeline_mode=pl.Buffered(1))


def _prep_kernel(c_ref, wmod_ref, bmod_ref, wint_ref, mod_ref, pin_ref):
    cs = _silu(c_ref[...]).astype(BF16)
    mod_ref[...] = jnp.dot(cs, wmod_ref[0].astype(BF16), preferred_element_type=F32) + bmod_ref[...]

    dt_rows = jnp.concatenate([wint_ref[DT_OFF:DT_OFF + N_DH, :],
                               jnp.zeros((LANES - N_DH, wint_ref.shape[1]), F32)], axis=0)
    pin_ref[:, P_DT:P_DT + LANES] = dt_rows.T.astype(BF16)
    def move(dst, src):
        pin_ref[:, dst:dst + LANES] = wint_ref[src:src + LANES, :].T.astype(BF16)

    for dst, src, width in ((P_XBC, XBC_OFF, XBC_DIM), (P_Z, Z_OFF, SSD_WIDTH)):
        for j in range(0, width, LANES):
            move(dst + j, src + j)
    for j in range(CONV_WIDTH // LANES):
        for k in range(3):
            move(P_CONV + (3 * j + k) * LANES, k * CONV_WIDTH + j * LANES)


def _prep_call(cvec, w_mod, b_mod, w_in_t):
    steps = PREP_STEPS
    r1, rm = D_MODEL // steps, 6 * D_MODEL // steps
    return pl.pallas_call(
        _prep_kernel,
        grid=(steps,),
        in_specs=[pl.BlockSpec((MOD_ROWS, D_MODEL), lambda i: (0, 0)),
                  pl.BlockSpec((1, D_MODEL, rm), lambda i: (0, 0, i)),
                  pl.BlockSpec((1, rm), lambda i: (0, i)),
                  pl.BlockSpec((w_in_t.shape[0], r1), lambda i: (0, i))],
        out_specs=[pl.BlockSpec((MOD_ROWS, rm), lambda i: (0, i)),
                   pl.BlockSpec((r1, IN_PAD), lambda i: (i, 0))],
        out_shape=[jax.ShapeDtypeStruct((MOD_ROWS, 6 * D_MODEL), F32),
                   jax.ShapeDtypeStruct((D_MODEL, IN_PAD), BF16)],
        compiler_params=pltpu.CompilerParams(dimension_semantics=("arbitrary",),
                                             vmem_limit_bytes=VMEM_LIMIT),
        name="prep",
    )(cvec, w_mod, b_mod, w_in_t)


def _ctx_kernel(x_ref, mod_ref, lng_ref, lnb_ref, w_ref, small_ref, exw_ref, u_ref, h0_ref):
    dtb, alog = _dt_params(small_ref)
    m = _mod_vectors(mod_ref, BATCH)
    sc = 1.0 + m[1]
    x = x_ref[...].reshape(CTX_ROWS * CTX_LEN, D_MODEL)
    u = _ln_hat(x) * (lng_ref[...] * sc) + (lnb_ref[...] * sc + m[0])
    proj = jnp.dot(u.astype(BF16), w_ref[...], preferred_element_type=F32)
    mprev, mnext = _edge_masks(CTX_ROWS * CTX_LEN, CTX_LEN)
    slabs = []
    for j in range((SSD_WIDTH + SSD_GN) // LANES):
        sl = slice(j * LANES, (j + 1) * LANES)
        pj = proj[:, P_XBC + j * LANES:P_XBC + (j + 1) * LANES]
        slabs.append(_silu(_conv3(pj, small_ref[S_SCW:S_SCW + 3, sl], mprev, mnext)
                           + small_ref[S_SCB:S_SCB + 1, sl]))
    zero = jnp.zeros((N_DH, CTX_LEN), F32)
    for i in range(CTX_ROWS):
        tok = slice(i * CTX_LEN, (i + 1) * CTX_LEN)
        xs = jnp.concatenate([sl_[tok] for sl_ in slabs[:4]], axis=1)
        dt, adt = _dt_rows(proj[tok, P_DT:P_DT + LANES], dtb, alog)
        sc2 = _scan_mm(adt, u_ref)
        excl = jnp.where(_fwd_rows(), sc2[:, :CTX_LEN], sc2[:, CTX_LEN:])
        w = jnp.exp(excl) * dt
        table = jnp.concatenate([zero] * (COL_W // 16) + _split(w, 2) + [zero], axis=0)
        cols = table.T.astype(BF16)
        wx = jnp.dot(cols, exw_ref[...], preferred_element_type=F32)
        for d in range(N_DIRS):
            xw = (xs * wx[:, d * SSD_WIDTH:(d + 1) * SSD_WIDTH]).astype(BF16)
            for g in range(SSD_GROUPS):
                bt = slabs[4 + g][tok].T.astype(BF16)
                h0_ref[i, d, g] = jnp.dot(bt, xw[:, g * GROUP_COLS:(g + 1) * GROUP_COLS],
                                          preferred_element_type=F32)


def _ctx_call(ctx, mod, lng, lnb, w_in, small, exw, u_ctx):
    return pl.pallas_call(
        _ctx_kernel,
        grid=(BATCH // CTX_ROWS,),
        in_specs=[pl.BlockSpec((CTX_ROWS, CTX_LEN, D_MODEL), lambda b: (b, 0, 0)),
                  _const_spec((MOD_ROWS, 6 * D_MODEL)), _const_spec((1, D_MODEL)), _const_spec((1, D_MODEL)),
                  _const_spec((D_MODEL, CTX_PAD)), _const_spec((SMALL_ROWS, D_MODEL)),
                  _const_spec((LANES, 2 * SSD_WIDTH)),
                  _const_spec((CTX_LEN, 2 * CTX_LEN))],
        out_specs=pl.BlockSpec((CTX_ROWS, N_DIRS, SSD_GROUPS, SSD_STATE, GROUP_COLS),
                               lambda b: (b, 0, 0, 0, 0)),
        out_shape=jax.ShapeDtypeStruct((BATCH, N_DIRS, SSD_GROUPS, SSD_STATE, GROUP_COLS), F32),
        compiler_params=pltpu.CompilerParams(dimension_semantics=("arbitrary",),
                                             vmem_limit_bytes=VMEM_LIMIT),
        name="ctx",
    )(ctx, mod, lng, lnb, w_in, small, exw, u_ctx)


def _proj_kernel(x_ref, mod_ref, lng_ref, lnb_ref, w_ref, small_ref,
                 exe_ref, exw_ref, u_ref, h0_ref, wout_ref, wff1_ref, wff2_ref,
                 hln_ref, ycv_ref, zg_ref, xs_ref, bt_ref, cm_ref, rows_ref, cols_ref, colf_ref, sb_ref,
                 pout_ref, pff1_ref, pff2_ref, st_ref):
    pout_ref[...] = wout_ref[0].astype(BF16)
    pff1_ref[...] = wff1_ref[0].astype(BF16)
    pff2_ref[...] = wff2_ref[0].astype(BF16)

    @pl.when(pl.program_id(1) == 0)
    def _():
        st_ref[...] = h0_ref[0, 0]

    m = _mod_vectors(mod_ref, pl.program_id(0))
    sc = 1.0 + m[1]
    scale, shift = lng_ref[...] * sc, lnb_ref[...] * sc + m[0]
    ubs, pas = [], []
    for r in range(2):
        rows = slice(r * (TM_PROJ // 2), (r + 1) * (TM_PROJ // 2))
        xhat = _ln_hat(x_ref[0, rows, :])
        hln_ref[0, rows, :] = xhat * lng_ref[...] + lnb_ref[...]
        ubs.append((xhat * scale + shift).astype(BF16))
        pas.append(jnp.dot(ubs[r], w_ref[:, :P_Z], preferred_element_type=F32))
    ub = jnp.concatenate(ubs, axis=0)
    pa = jnp.concatenate(pas, axis=0)
    mprev, mnext = _edge_masks(TM_PROJ, GRID_W)
    is_fwd = _fwd_rows()
    zero = jnp.zeros((N_DH, SSD_CHUNK), F32)
    chunks = [slice(c * SSD_CHUNK, (c + 1) * SSD_CHUNK) for c in range(NCH_PROJ)]

    def conv_slabs(p, first):
        for i in range(2):
            gb, gc, gh = (p[:, (3 * i + k) * LANES:(3 * i + k + 1) * LANES] for k in range(3))
            sl = slice((first + i) * LANES, (first + i + 1) * LANES)
            ycv_ref[0, :, sl] = (gb * _conv3(gc * gh, small_ref[S_CW:S_CW + 3, sl], mprev, mnext)).astype(BF16)

    dtb, alog = _dt_params(small_ref)
    dts = [_dt_rows(pa[tok, P_DT:P_DT + LANES], dtb, alog) for tok in chunks]
    pz = jnp.dot(ub, w_ref[:, P_Z:P_CONV], preferred_element_type=F32)

    colss = []
    for c, (dt, adt) in enumerate(dts):
        sc3 = _scan_mm(adt, u_ref)
        cs = jnp.where(is_fwd, sc3[:, :SSD_CHUNK], sc3[:, SSD_CHUNK:2 * SSD_CHUNK])
        tot = sc3[:, 2 * SSD_CHUNK:]
        e1 = jnp.exp(cs)
        w = jnp.exp(tot - cs) * dt
        rows_ref[0, c] = jnp.concatenate([cs, dt], axis=0)
        table = jnp.concatenate([zero] * 3 + _split(e1, 2) + _split(w, 2) + [cs], axis=0).T
        colf_ref[0, c] = table
        colss.append(table.astype(BF16))
        cols_ref[0, c] = colss[c]

    pc1 = jnp.dot(ub, w_ref[:, P_CONV:P_CONV + 6 * LANES], preferred_element_type=F32)

    slabs = []
    for j in range(XBC_DIM // LANES):
        sl = slice(j * LANES, (j + 1) * LANES)
        pj = pa[:, P_XBC + j * LANES:P_XBC + (j + 1) * LANES]
        slabs.append(_silu(_conv3(pj, small_ref[S_SCW:S_SCW + 3, sl], mprev, mnext)
                           + small_ref[S_SCB:S_SCB + 1, sl]))
    xs = jnp.concatenate(slabs[:4], axis=1)
    xs_ref[0] = xs.astype(BF16)
    cm_ref[0] = jnp.concatenate(slabs[6:8], axis=1).astype(BF16)
    zg_ref[0] = _silu(pz).astype(BF16)
    bts = []
    for c, tok in enumerate(chunks):
        bt = [slabs[4 + g][tok].T.astype(BF16) for g in range(SSD_GROUPS)]
        for g in range(SSD_GROUPS):
            bt_ref[0, c, g] = bt[g]
        bts.append(bt)
    wxbs = [jnp.dot(cols, exw_ref[:, SSD_WIDTH:], preferred_element_type=F32) for cols in colss]
    decs = [jnp.dot(cols[0:16], exe_ref[:, SSD_WIDTH:], preferred_element_type=F32)[0:1]
            for cols in colss]

    pc2 = jnp.dot(ub, w_ref[:, P_CONV + 6 * LANES:], preferred_element_type=F32)
    conv_slabs(pc1, 0)

    state = [st_ref[g] for g in range(SSD_GROUPS)]
    for c in reversed(range(NCH_PROJ)):
        xw = (xs[chunks[c]] * wxbs[c]).astype(BF16)
        for g in range(SSD_GROUPS):
            gs = slice(g * GROUP_COLS, (g + 1) * GROUP_COLS)
            sb_ref[0, c, g] = state[g].astype(BF16)
            local = jnp.dot(bts[c][g], xw[:, gs], preferred_element_type=F32)
            state[g] = state[g] * decs[c][:, gs] + local
    for g in range(SSD_GROUPS):
        st_ref[g] = state[g]
    conv_slabs(pc2, 2)


def _proj_call(x, mod, lng, lnb, w_in, small, exe, exw, u_chunk, h0, w_out, w_ff1, w_ff2):
    n_steps = BATCH * NT_PROJ
    r1, r4 = D_MODEL // n_steps, D_FF // n_steps
    wrow3 = lambda b, t: (0, b * NT_PROJ + t, 0)
    wrow2 = lambda b, t: (b * NT_PROJ + t, 0)
    rev = lambda b, t: (b, NT_PROJ - 1 - t, 0)
    rev4 = lambda b, t: (b, NT_PROJ - 1 - t, 0, 0)
    rev5 = lambda b, t: (b, NT_PROJ - 1 - t, 0, 0, 0)
    tok = lambda width: pl.BlockSpec((1, TM_PROJ, width), rev)
    out_shape = [
        jax.ShapeDtypeStruct((BATCH, SEQ, D_MODEL), F32),
        jax.ShapeDtypeStruct((BATCH, SEQ, CONV_WIDTH), BF16),
        jax.ShapeDtypeStruct((BATCH, SEQ, SSD_WIDTH), BF16),
        jax.ShapeDtypeStruct((BATCH, SEQ, SSD_WIDTH), BF16),
        jax.ShapeDtypeStruct((BATCH, NCHUNK, SSD_GROUPS, SSD_STATE, SSD_CHUNK), BF16),
        jax.ShapeDtypeStruct((BATCH, SEQ, SSD_GN), BF16),
        jax.ShapeDtypeStruct((BATCH, NCHUNK, 2 * N_DH, SSD_CHUNK), F32),
        jax.ShapeDtypeStruct((BATCH, NCHUNK, SSD_CHUNK, LANES), BF16),
        jax.ShapeDtypeStruct((BATCH, NCHUNK, SSD_CHUNK, LANES), F32),
        jax.ShapeDtypeStruct((BATCH, NCHUNK, SSD_GROUPS, SSD_STATE, GROUP_COLS), BF16),
        jax.ShapeDtypeStruct((D_MODEL, D_MODEL), BF16),
        jax.ShapeDtypeStruct((D_MODEL, D_FF), BF16),
        jax.ShapeDtypeStruct((D_FF, D_MODEL), BF16),
    ]
    out_specs = [
        tok(D_MODEL), tok(CONV_WIDTH), tok(SSD_WIDTH), tok(SSD_WIDTH),
        pl.BlockSpec((1, NCH_PROJ, SSD_GROUPS, SSD_STATE, SSD_CHUNK), rev5),
        tok(SSD_GN),
        pl.BlockSpec((1, NCH_PROJ, 2 * N_DH, SSD_CHUNK), rev4),
        pl.BlockSpec((1, NCH_PROJ, SSD_CHUNK, LANES), rev4),
        pl.BlockSpec((1, NCH_PROJ, SSD_CHUNK, LANES), rev4),
        pl.BlockSpec((1, NCH_PROJ, SSD_GROUPS, SSD_STATE, GROUP_COLS), rev5),
        pl.BlockSpec((r1, D_MODEL), wrow2), pl.BlockSpec((r1, D_FF), wrow2), pl.BlockSpec((r4, D_MODEL), wrow2),
    ]
    in_specs = [
        pl.BlockSpec((1, TM_PROJ, D_MODEL), rev),
        _const_spec((MOD_ROWS, 6 * D_MODEL)),
        _const_spec((1, D_MODEL)), _const_spec((1, D_MODEL)),
        _const_spec((D_MODEL, IN_PAD)),
        _const_spec((SMALL_ROWS, D_MODEL)),
        _const_spec((LANES, 2 * SSD_WIDTH)), _const_spec((LANES, 2 * SSD_WIDTH)),
        _const_spec((SSD_CHUNK, 3 * SSD_CHUNK)),
        pl.BlockSpec((1, 1, SSD_GROUPS, SSD_STATE, GROUP_COLS), lambda b, t: (b, 1, 0, 0, 0)),
        pl.BlockSpec((1, r1, D_MODEL), wrow3), pl.BlockSpec((1, r1, D_FF), wrow3),
        pl.BlockSpec((1, r4, D_MODEL), wrow3),
    ]
    return pl.pallas_call(
        _proj_kernel,
        grid=(BATCH, NT_PROJ),
        in_specs=in_specs,
        out_specs=out_specs,
        out_shape=out_shape,
        scratch_shapes=[pltpu.VMEM((SSD_GROUPS, SSD_STATE, GROUP_COLS), F32)],
        compiler_params=pltpu.CompilerParams(dimension_semantics=("arbitrary", "arbitrary"),
                                             vmem_limit_bytes=VMEM_LIMIT),
        name="proj",
    )(x, mod, lng, lnb, w_in, small, exe, exw, u_chunk, h0, w_out, w_ff1, w_ff2)


def _out_kernel(hln_ref, mod_ref, ycv_ref, zg_ref, xs_ref, bt_ref, cm_ref, rows_ref,
                cols_ref, colf_ref, sb_ref, h0_ref, exw_ref, small_ref, wout_ref,
                ln1g_ref, ln1b_ref, wff1_ref, wff2_ref, ln2g_ref, ln2b_ref,
                out_ref, st_ref, yn_ref, h1_ref, u2_ref, g2_ref, r2_ref, hid_ref):
    s = pl.program_id(0)
    wr = s % 2
    rd = (s + 1) % 2
    li = lax.broadcasted_iota(jnp.int32, (SSD_CHUNK, SSD_CHUNK), 0)
    si = lax.broadcasted_iota(jnp.int32, (SSD_CHUNK, SSD_CHUNK), 1)
    low = li >= si
    diag = li == si
    lo_half = si < SSD_HEADDIM

    def ssd_chunk(c, state, fill):
        tok = slice(c * SSD_CHUNK, (c + 1) * SSD_CHUNK)
        rows = rows_ref[0, c]
        cols = cols_ref[0, c]
        xs = xs_ref[0, tok, :]
        cm = cm_ref[0, tok, :]
        colf = colf_ref[0, c]
        bc = [jnp.broadcast_to(colf[:, COL_CSF + j:COL_CSF + j + 1], (SSD_CHUNK, LANES))
              for j in range(N_DH)]
        e1x = [jnp.concatenate(
            [jnp.exp(jnp.where(lo_half, bc[d * SSD_HEADS + 2 * k], bc[d * SSD_HEADS + 2 * k + 1]))
             for k in range(SSD_HEADS // 2)], axis=1) for d in range(N_DIRS)]
        wxf = jnp.dot(cols, exw_ref[:, :SSD_WIDTH], preferred_element_type=F32)
        dec = e1x[0][SSD_CHUNK - 1:SSD_CHUNK, :]
        gmat = [jnp.dot(cm[:, g * SSD_STATE:(g + 1) * SSD_STATE], bt_ref[0, c, g],
                        preferred_element_type=F32) for g in range(SSD_GROUPS)]
        fill[0]()

        ys = []
        for k in range(SSD_HEADS // 2):
            ms = []
            for h in (2 * k, 2 * k + 1):
                g = h // (SSD_HEADS // SSD_GROUPS)
                hb = SSD_HEADS + h
                arg = jnp.where(low,
                                bc[h] - rows[h:h + 1, :],
                                bc[hb] - rows[hb:hb + 1, :])
                dtf = rows[N_DH + h:N_DH + h + 1, :]
                dtb = rows[N_DH + hb:N_DH + hb + 1, :]
                fac = jnp.where(low, dtf, dtb) + jnp.where(diag, dtb, 0.0)
                ms.append((gmat[g] * jnp.exp(arg) * fac).astype(BF16))
            xp = xs[:, k * LANES:(k + 1) * LANES]
            rhs = jnp.concatenate([jnp.where(lo_half, xp, jnp.zeros_like(xp)),
                                   jnp.where(lo_half, jnp.zeros_like(xp), xp)], axis=0)
            ys.append(jnp.dot(jnp.concatenate(ms, axis=1), rhs, preferred_element_type=F32))
            fill[k + 1]()
        y = jnp.concatenate(ys, axis=1)

        yf, yb = [], []
        for g in range(SSD_GROUPS):
            cg = cm[:, g * SSD_STATE:(g + 1) * SSD_STATE]
            yf.append(jnp.dot(cg, state[g].astype(BF16), preferred_element_type=F32))
            yb.append(jnp.dot(cg, sb_ref[0, c, g], preferred_element_type=F32))
        y = (y + jnp.concatenate(yf, axis=1) * e1x[0]
             + jnp.concatenate(yb, axis=1) * e1x[1]
             + xs.astype(F32) * small_ref[S_DXNW:S_DXNW + 1, :SSD_WIDTH])

        yg = y * zg_ref[0, tok, :].astype(F32)
        ms_ = jnp.mean(yg * yg, axis=-1, keepdims=True)
        yn_ref[tok, :CONV_WIDTH] = ycv_ref[0, tok, :]
        yn_ref[tok, CONV_WIDTH:] = (yg * lax.rsqrt(ms_ + RMS_EPS)
                                   * small_ref[S_DXNW:S_DXNW + 1, SSD_WIDTH:]).astype(BF16)
        fill[5]()

        xw = (xs.astype(F32) * wxf).astype(BF16)
        new_state = []
        for g in range(SSD_GROUPS):
            gs = slice(g * GROUP_COLS, (g + 1) * GROUP_COLS)
            local = jnp.dot(bt_ref[0, c, g], xw[:, gs], preferred_element_type=F32)
            new_state.append(state[g] * dec[:, gs] + local)
        return new_state

    def mlp_up(j, k):
        blk = slice(j * FF_BLK + k * FF_SUB, j * FF_BLK + (k + 1) * FF_SUB)
        hid = jnp.maximum(jnp.dot(u2_ref[...], wff1_ref[:, blk], preferred_element_type=F32), 0.0)
        hid_ref[:, blk] = (hid * hid).astype(BF16)

    def mlp_down(j, acc):
        blk = slice(j * FF_BLK, (j + 1) * FF_BLK)
        part = jnp.dot(hid_ref[:, blk], wff2_ref[blk, :], preferred_element_type=F32)
        return part if acc is None else acc + part

    def ln2_rows(i):
        r = slice(i * SSD_CHUNK, (i + 1) * SSD_CHUNK)
        out_ref[0, r, :] = _ln_hat(r2_ref[r, :]) * ln2g_ref[...] + ln2b_ref[...]

    def step(mixer, mlp, norm):
        state, acc = None, [None]
        pieces = []
        if mlp:
            def down(j):
                acc[0] = mlp_down(j, acc[0])
            for j in range(N_SLAB):
                pieces += [(1, functools.partial(mlp_up, j, k)) for k in range(FF_BLK // FF_SUB)]
            for j in range(N_SLAB - DOWN_KEPT):
                at = (j + 1) * (FF_BLK // FF_SUB) + j + DOWN_LAG
                pieces.insert(at, (FF_BLK // FF_SUB, functools.partial(down, j)))
        n_slots = NCH * FILL_SLOTS
        per_slot = sum(cost for cost, _ in pieces) / n_slots
        slots, issued = [], 0.0
        for i in range(n_slots):
            mine = []
            while pieces and issued < (i + 1) * per_slot:
                cost, fn = pieces.pop(0)
                issued += cost
                mine.append(fn)
            slots.append(lambda mine=mine: [fn() for fn in mine])
        if mixer:
            @pl.when(jnp.minimum(s, N_TILES - 1) % NT == 0)
            def _():
                st_ref[...] = h0_ref[0, 0]
            state = [st_ref[g] for g in range(SSD_GROUPS)]
        for c in range(NCH):
            fill = slots[c * FILL_SLOTS:(c + 1) * FILL_SLOTS]
            if mixer:
                state = ssd_chunk(c, state, fill)
            else:
                for f in fill:
                    f()
            if norm:
                ln2_rows(c)
        if mixer:
            for g in range(SSD_GROUPS):
                st_ref[g] = state[g]
            mix = jnp.dot(yn_ref[...], wout_ref[...], preferred_element_type=F32)
            m = _mod_vectors(mod_ref, jnp.minimum(s, N_TILES - 1) // NT)
            g2_ref[wr] = m[5]
        for i in range(DOWN_KEPT):
            if mlp:
                down(N_SLAB - DOWN_KEPT + i)
            if mixer:
                for j in range(i * NCH // DOWN_KEPT, (i + 1) * NCH // DOWN_KEPT):
                    r = slice(j * SSD_CHUNK, (j + 1) * SSD_CHUNK)
                    h1 = _ln_hat(ALPHA * hln_ref[0, r, :] + m[2] * mix[r]) * ln1g_ref[...] + ln1b_ref[...]
                    h1_ref[wr, r, :] = h1
                    u2_ref[r, :] = (h1 * (1.0 + m[4]) + m[3]).astype(BF16)
        if mlp:
            r2_ref[...] = ALPHA * h1_ref[rd] + g2_ref[rd] * acc[0]

    @pl.when(s == 0)
    def _():
        r2_ref[...] = jnp.zeros((TM, D_MODEL), F32)
        step(True, False, False)

    @pl.when(jnp.logical_and(s >= 1, s < N_TILES))
    def _():
        step(True, True, True)

    @pl.when(s == N_TILES)
    def _():
        step(False, True, True)

    @pl.when(s == N_TILES + 1)
    def _():
        step(False, False, True)


def _out_call(hln, mod, ycv, zg, xs, bt, cm, rows, cols, colf, sb, h0, exw, small,
              wout, ln1g, ln1b, wff1, wff2, ln2g, ln2b):
    cur = lambda s: jnp.minimum(s, N_TILES - 1)
    fwd = lambda s: (cur(s) // NT, cur(s) % NT, 0)
    fwd4 = lambda s: (cur(s) // NT, cur(s) % NT, 0, 0)
    fwd5 = lambda s: (cur(s) // NT, cur(s) % NT, 0, 0, 0)
    done = lambda s: (jnp.maximum(s - 2, 0) // NT, jnp.maximum(s - 2, 0) % NT, 0)
    tok = lambda width: pl.BlockSpec((1, TM, width), fwd)
    in_specs = [
        tok(D_MODEL),
        _const_spec((MOD_ROWS, 6 * D_MODEL)),
        tok(CONV_WIDTH), tok(SSD_WIDTH), tok(SSD_WIDTH),
        pl.BlockSpec((1, NCH, SSD_GROUPS, SSD_STATE, SSD_CHUNK), fwd5),
        tok(SSD_GN),
        pl.BlockSpec((1, NCH, 2 * N_DH, SSD_CHUNK), fwd4),
        pl.BlockSpec((1, NCH, SSD_CHUNK, LANES), fwd4),
        pl.BlockSpec((1, NCH, SSD_CHUNK, LANES), fwd4),
        pl.BlockSpec((1, NCH, SSD_GROUPS, SSD_STATE, GROUP_COLS), fwd5),
        pl.BlockSpec((1, 1, SSD_GROUPS, SSD_STATE, GROUP_COLS), lambda s: (cur(s) // NT, 0, 0, 0, 0)),
        _const_spec((LANES, 2 * SSD_WIDTH)),
        _const_spec((SMALL_ROWS, D_MODEL)),
        _const_spec((D_MODEL, D_MODEL)),
        _const_spec((1, D_MODEL)), _const_spec((1, D_MODEL)),
        _const_spec((D_MODEL, D_FF)), _const_spec((D_FF, D_MODEL)),
        _const_spec((1, D_MODEL)), _const_spec((1, D_MODEL)),
    ]
    return pl.pallas_call(
        _out_kernel,
        grid=(N_TILES + 2,),
        in_specs=in_specs,
        out_specs=pl.BlockSpec((1, TM, D_MODEL), done),
        out_shape=jax.ShapeDtypeStruct((BATCH, SEQ, D_MODEL), F32),
        scratch_shapes=[pltpu.VMEM((SSD_GROUPS, SSD_STATE, GROUP_COLS), F32),
                        pltpu.VMEM((TM, D_MODEL), BF16),
                        pltpu.VMEM((2, TM, D_MODEL), F32),
                        pltpu.VMEM((TM, D_MODEL), BF16),
                        pltpu.VMEM((2, 1, D_MODEL), F32),
                        pltpu.VMEM((TM, D_MODEL), F32),
                        pltpu.VMEM((TM, D_FF), BF16)],
        compiler_params=pltpu.CompilerParams(dimension_semantics=("arbitrary",),
                                             vmem_limit_bytes=VMEM_LIMIT),
        name="out",
    )(hln, mod, ycv, zg, xs, bt, cm, rows, cols, colf, sb, h0, exw, small,
      wout, ln1g, ln1b, wff1, wff2, ln2g, ln2b)


def kernel(x, c, ctx, c_ctx, ln_in_g, ln_in_b, w_mod, b_mod, w_in, conv_w, ssd_conv_w, ssd_conv_b,
           dt_bias, a_log, ssd_d, ssd_norm_w, w_out, ln1_g, ln1_b, w_ff1, w_ff2, ln2_g, ln2_b):
    row = lambda v: v.reshape(1, -1).astype(F32)
    cvec = jnp.concatenate([c, c_ctx[None, :], jnp.zeros((MOD_ROWS - BATCH - 1, D_MODEL), F32)], axis=0)
    mod, w_in_p = _prep_call(cvec, w_mod, b_mod, w_in[0].T)

    lng, lnb = row(ln_in_g), row(ln_in_b)
    pad_to = lambda v: jnp.pad(v, ((0, 0), (0, D_MODEL - v.shape[1])))
    small = jnp.concatenate([
        ssd_conv_w[0], row(ssd_conv_b[0]), pad_to(conv_w[0]),
        pad_to(jnp.concatenate([dt_bias[0].reshape(1, N_DH), a_log[0].reshape(1, N_DH)], axis=1)),
        jnp.concatenate([jnp.repeat(ssd_d[0], SSD_HEADDIM).reshape(1, SSD_WIDTH), row(ssd_norm_w[0])], axis=1),
        jnp.zeros((SMALL_ROWS - S_DXNW - 1, D_MODEL), F32)], axis=0)
    exe = jnp.asarray(_EXE, BF16)
    exw = jnp.asarray(_EXW, BF16)

    h0 = _ctx_call(ctx, mod, lng, lnb, w_in_p, small, exw, jnp.asarray(_U_CTX, BF16))
    hln, ycv, zg, xs, bt, cm, rows, cols, colf, sb, w_out_p, w_ff1_p, w_ff2_p = _proj_call(
        x, mod, lng, lnb, w_in_p, small, exe, exw,
        jnp.asarray(_U_CHUNK, BF16), h0, w_out, w_ff1, w_ff2)
    return _out_call(hln, mod, ycv, zg, xs, bt, cm, rows, cols, colf, sb, h0, exw,
                     small, w_out_p, row(ln1_g[0]), row(ln1_b[0]),
                     w_ff1_p, w_ff2_p, row(ln2_g[0]), row(ln2_b[0]))
```

```python
import functools

import jax
import jax.numpy as jnp
import numpy as np
from jax import lax
from jax.experimental import pallas as pl
from jax.experimental.pallas import tpu as pltpu

F32 = jnp.float32
BF16 = jnp.bfloat16

D_MODEL = 1024
BATCH = 8
SEQ = 2048
CTX_LEN = 256
GRID_W = 64
CONV_WIDTH = 512
SSD_WIDTH = 512
SSD_HEADDIM = 64
SSD_HEADS = 8
SSD_GROUPS = 2
SSD_STATE = 128
SSD_CHUNK = 128
N_DIRS = 2
D_FF = 4 * D_MODEL
LN_EPS = 1e-5
RMS_EPS = 1e-5
SSD_GN = SSD_GROUPS * SSD_STATE
XBC_DIM = SSD_WIDTH + 2 * SSD_GN
Z_OFF = 3 * CONV_WIDTH
XBC_OFF = Z_OFF + SSD_WIDTH
DT_OFF = XBC_OFF + XBC_DIM
N_DH = N_DIRS * SSD_HEADS
LANES = 128
P_DT = 0
P_XBC = LANES
P_Z = P_XBC + XBC_DIM
P_CONV = P_Z + SSD_WIDTH
IN_PAD = P_CONV + 3 * CONV_WIDTH
CTX_PAD = P_XBC + SSD_WIDTH + SSD_GN
GROUP_COLS = (SSD_HEADS // SSD_GROUPS) * SSD_HEADDIM
ALPHA = 2.0 ** 0.25

TM = 512
CTX_ROWS = 2
MOD_ROWS = 16
SMALL_ROWS = 16
PREP_STEPS = 8
NCH = TM // SSD_CHUNK
NT = SEQ // TM
NCHUNK = SEQ // SSD_CHUNK
N_TILES = BATCH * NT
TM_PROJ = 1024
NCH_PROJ = TM_PROJ // SSD_CHUNK
NT_PROJ = SEQ // TM_PROJ
FF_BLK = 1024
FF_SUB = 256
N_SLAB = D_FF // FF_BLK
FILL_SLOTS = 6
DOWN_KEPT = 2
DOWN_LAG = 1
VMEM_LIMIT = 58 * 1024 * 1024

COL_E1 = 48
COL_W = 80
COL_CSF = 112


def _expansion(col0, pieces, width):
    m = np.zeros((LANES, N_DH * width), np.float32)
    for t in range(pieces):
        for j in range(N_DH):
            m[col0 + 16 * t + j, j * width:(j + 1) * width] = 1.0
    return m


_EXE = _expansion(COL_E1, 2, SSD_HEADDIM)
_EXW = _expansion(COL_W, 2, SSD_HEADDIM)


def _ln_hat(x):
    mu = jnp.mean(x, axis=-1, keepdims=True)
    xc = x - mu
    var = jnp.mean(xc * xc, axis=-1, keepdims=True)
    return xc * lax.rsqrt(var + LN_EPS)


def _silu(x):
    return x / (1.0 + jnp.exp(-x))


def _softplus(x):
    return jnp.maximum(x, 0.0) + jnp.log1p(jnp.exp(-jnp.abs(x)))


def _edge_masks(rows, period):
    pos = lax.broadcasted_iota(jnp.int32, (rows, LANES), 0) % period
    return (pos != 0).astype(F32), (pos != period - 1).astype(F32)


def _conv3(t, w, mprev, mnext):
    rows = t.shape[0]
    prev = pltpu.roll(t, 1, 0) * mprev
    nxt = pltpu.roll(t, rows - 1, 0) * mnext
    return prev * w[0:1, :] + t * w[1:2, :] + nxt * w[2:3, :]


def _split(v, pieces):
    out = []
    for _ in range(pieces - 1):
        p = v.astype(BF16).astype(F32)
        out.append(p)
        v = v - p
    out.append(v.astype(BF16).astype(F32))
    return out


def _tri(length, op):
    i = np.arange(length)
    return op(i[:, None], i[None, :]).astype(np.float32)


_U_CHUNK = np.concatenate([_tri(SSD_CHUNK, np.less_equal), _tri(SSD_CHUNK, np.greater_equal),
                           np.ones((SSD_CHUNK, SSD_CHUNK), np.float32)], axis=1)
_U_CTX = np.concatenate([_tri(CTX_LEN, np.greater), _tri(CTX_LEN, np.less)], axis=1)


def _scan_mm(v, u_ref):
    pieces = jnp.concatenate(_split(v, 3), axis=0).astype(BF16)
    o = jnp.dot(pieces, u_ref[...], preferred_element_type=F32)
    return o[0:N_DH] + o[N_DH:2 * N_DH] + o[2 * N_DH:3 * N_DH]


def _dt_rows(raw, dtb, a_log):
    r = raw.T[0:N_DH, :] + dtb
    dt = _softplus(r)
    return dt, dt * (-jnp.exp(a_log))


def _fwd_rows():
    return lax.broadcasted_iota(jnp.int32, (N_DH, 1), 0) < SSD_HEADS


S_SCW = 0
S_SCB = 3
S_CW = 4
S_DT = 7
S_DXNW = 8


def _dt_params(small_ref):
    t = jnp.broadcast_to(small_ref[S_DT:S_DT + 1, 0:LANES], (N_DH, LANES))
    sub = lax.broadcasted_iota(jnp.int32, (N_DH, LANES), 0)
    lane = lax.broadcasted_iota(jnp.int32, (N_DH, LANES), 1)
    pick = lambda off: jnp.sum(jnp.where(lane == sub + off, t, 0.0), axis=1, keepdims=True)
    return pick(0), pick(N_DH)


def _mod_vectors(mod_ref, row):
    r = mod_ref[pl.ds(row, 1), :]
    return [r[:, k * D_MODEL:(k + 1) * D_MODEL] for k in range(6)]


def _const_spec(shape):
    nd = len(shape)
    return pl.BlockSpec(shape, lambda *_: (0,) * nd, pipeline_mode=pl.Buffered(1))


def _prep_kernel(c_ref, wmod_ref, bmod_ref, wint_ref, mod_ref, pin_ref):
    cs = _silu(c_ref[...]).astype(BF16)
    mod_ref[...] = jnp.dot(cs, wmod_ref[0].astype(BF16), preferred_element_type=F32) + bmod_ref[...]

    dt_rows = jnp.concatenate([wint_ref[DT_OFF:DT_OFF + N_DH, :],
                               jnp.zeros((LANES - N_DH, wint_ref.shape[1]), F32)], axis=0)
    pin_ref[:, P_DT:P_DT + LANES] = dt_rows.T.astype(BF16)
    def move(dst, src):
        pin_ref[:, dst:dst + LANES] = wint_ref[src:src + LANES, :].T.astype(BF16)

    for dst, src, width in ((P_XBC, XBC_OFF, XBC_DIM), (P_Z, Z_OFF, SSD_WIDTH)):
        for j in range(0, width, LANES):
            move(dst + j, src + j)
    for j in range(CONV_WIDTH // LANES):
        for k in range(3):
            move(P_CONV + (3 * j + k) * LANES, k * CONV_WIDTH + j * LANES)


def _prep_call(cvec, w_mod, b_mod, w_in_t):
    steps = PREP_STEPS
    r1, rm = D_MODEL // steps, 6 * D_MODEL // steps
    return pl.pallas_call(
        _prep_kernel,
        grid=(steps,),
        in_specs=[pl.BlockSpec((MOD_ROWS, D_MODEL), lambda i: (0, 0)),
                  pl.BlockSpec((1, D_MODEL, rm), lambda i: (0, 0, i)),
                  pl.BlockSpec((1, rm), lambda i: (0, i)),
                  pl.BlockSpec((w_in_t.shape[0], r1), lambda i: (0, i))],
        out_specs=[pl.BlockSpec((MOD_ROWS, rm), lambda i: (0, i)),
                   pl.BlockSpec((r1, IN_PAD), lambda i: (i, 0))],
        out_shape=[jax.ShapeDtypeStruct((MOD_ROWS, 6 * D_MODEL), F32),
                   jax.ShapeDtypeStruct((D_MODEL, IN_PAD), BF16)],
        compiler_params=pltpu.CompilerParams(dimension_semantics=("arbitrary",),
                                             vmem_limit_bytes=VMEM_LIMIT),
        name="prep",
    )(cvec, w_mod, b_mod, w_in_t)


def _ctx_kernel(x_ref, mod_ref, lng_ref, lnb_ref, w_ref, small_ref, exw_ref, u_ref, h0_ref):
    dtb, alog = _dt_params(small_ref)
    m = _mod_vectors(mod_ref, BATCH)
    sc = 1.0 + m[1]
    x = x_ref[...].reshape(CTX_ROWS * CTX_LEN, D_MODEL)
    u = _ln_hat(x) * (lng_ref[...] * sc) + (lnb_ref[...] * sc + m[0])
    proj = jnp.dot(u.astype(BF16), w_ref[...], preferred_element_type=F32)
    mprev, mnext = _edge_masks(CTX_ROWS * CTX_LEN, CTX_LEN)
    slabs = []
    for j in range((SSD_WIDTH + SSD_GN) // LANES):
        sl = slice(j * LANES, (j + 1) * LANES)
        pj = proj[:, P_XBC + j * LANES:P_XBC + (j + 1) * LANES]
        slabs.append(_silu(_conv3(pj, small_ref[S_SCW:S_SCW + 3, sl], mprev, mnext)
                           + small_ref[S_SCB:S_SCB + 1, sl]))
    zero = jnp.zeros((N_DH, CTX_LEN), F32)
    for i in range(CTX_ROWS):
        tok = slice(i * CTX_LEN, (i + 1) * CTX_LEN)
        xs = jnp.concatenate([sl_[tok] for sl_ in slabs[:4]], axis=1)
        dt, adt = _dt_rows(proj[tok, P_DT:P_DT + LANES], dtb, alog)
        sc2 = _scan_mm(adt, u_ref)
        excl = jnp.where(_fwd_rows(), sc2[:, :CTX_LEN], sc2[:, CTX_LEN:])
        w = jnp.exp(excl) * dt
        table = jnp.concatenate([zero] * (COL_W // 16) + _split(w, 2) + [zero], axis=0)
        cols = table.T.astype(BF16)
        wx = jnp.dot(cols, exw_ref[...], preferred_element_type=F32)
        for d in range(N_DIRS):
            xw = (xs * wx[:, d * SSD_WIDTH:(d + 1) * SSD_WIDTH]).astype(BF16)
            for g in range(SSD_GROUPS):
                bt = slabs[4 + g][tok].T.astype(BF16)
                h0_ref[i, d, g] = jnp.dot(bt, xw[:, g * GROUP_COLS:(g + 1) * GROUP_COLS],
                                          preferred_element_type=F32)


def _ctx_call(ctx, mod, lng, lnb, w_in, small, exw, u_ctx):
    return pl.pallas_call(
        _ctx_kernel,
        grid=(BATCH // CTX_ROWS,),
        in_specs=[pl.BlockSpec((CTX_ROWS, CTX_LEN, D_MODEL), lambda b: (b, 0, 0)),
                  _const_spec((MOD_ROWS, 6 * D_MODEL)), _const_spec((1, D_MODEL)), _const_spec((1, D_MODEL)),
                  _const_spec((D_MODEL, CTX_PAD)), _const_spec((SMALL_ROWS, D_MODEL)),
                  _const_spec((LANES, 2 * SSD_WIDTH)),
                  _const_spec((CTX_LEN, 2 * CTX_LEN))],
        out_specs=pl.BlockSpec((CTX_ROWS, N_DIRS, SSD_GROUPS, SSD_STATE, GROUP_COLS),
                               lambda b: (b, 0, 0, 0, 0)),
        out_shape=jax.ShapeDtypeStruct((BATCH, N_DIRS, SSD_GROUPS, SSD_STATE, GROUP_COLS), F32),
        compiler_params=pltpu.CompilerParams(dimension_semantics=("arbitrary",),
                                             vmem_limit_bytes=VMEM_LIMIT),
        name="ctx",
    )(ctx, mod, lng, lnb, w_in, small, exw, u_ctx)


def _proj_kernel(x_ref, mod_ref, lng_ref, lnb_ref, w_ref, small_ref,
                 exe_ref, exw_ref, u_ref, h0_ref, wout_ref, wff1_ref, wff2_ref,
                 hln_ref, ycv_ref, zg_ref, xs_ref, bt_ref, cm_ref, rows_ref, cols_ref, colf_ref, sb_ref,
                 pout_ref, pff1_ref, pff2_ref, st_ref):
    pout_ref[...] = wout_ref[0].astype(BF16)
    pff1_ref[...] = wff1_ref[0].astype(BF16)
    pff2_ref[...] = wff2_ref[0].astype(BF16)

    @pl.when(pl.program_id(1) == 0)
    def _():
        st_ref[...] = h0_ref[0, 0]

    m = _mod_vectors(mod_ref, pl.program_id(0))
    sc = 1.0 + m[1]
    scale, shift = lng_ref[...] * sc, lnb_ref[...] * sc + m[0]
    ubs, pas = [], []
    for r in range(2):
        rows = slice(r * (TM_PROJ // 2), (r + 1) * (TM_PROJ // 2))
        xhat = _ln_hat(x_ref[0, rows, :])
        hln_ref[0, rows, :] = xhat * lng_ref[...] + lnb_ref[...]
        ubs.append((xhat * scale + shift).astype(BF16))
        pas.append(jnp.dot(ubs[r], w_ref[:, :P_Z], preferred_element_type=F32))
    ub = jnp.concatenate(ubs, axis=0)
    pa = jnp.concatenate(pas, axis=0)
    mprev, mnext = _edge_masks(TM_PROJ, GRID_W)
    half = TM_PROJ // 2

    def xbc_half(r):
        out = []
        for j in range(XBC_DIM // LANES):
            sl = slice(j * LANES, (j + 1) * LANES)
            pj = pas[r][:, P_XBC + j * LANES:P_XBC + (j + 1) * LANES]
            out.append(_silu(_conv3(pj, small_ref[S_SCW:S_SCW + 3, sl], mprev[:half], mnext[:half])
                             + small_ref[S_SCB:S_SCB + 1, sl]))
        return out
    is_fwd = _fwd_rows()
    zero = jnp.zeros((N_DH, SSD_CHUNK), F32)
    chunks = [slice(c * SSD_CHUNK, (c + 1) * SSD_CHUNK) for c in range(NCH_PROJ)]

    def conv_slabs(p, first):
        for i in range(2):
            gb, gc, gh = (p[:, (3 * i + k) * LANES:(3 * i + k + 1) * LANES] for k in range(3))
            sl = slice((first + i) * LANES, (first + i + 1) * LANES)
            ycv_ref[0, :, sl] = (gb * _conv3(gc * gh, small_ref[S_CW:S_CW + 3, sl], mprev, mnext)).astype(BF16)

    dtb, alog = _dt_params(small_ref)
    slabs_lo = xbc_half(0)
    dts = [_dt_rows(pa[tok, P_DT:P_DT + LANES], dtb, alog) for tok in chunks]
    pz = jnp.dot(ub, w_ref[:, P_Z:P_CONV], preferred_element_type=F32)
    slabs_hi = xbc_half(1)
    slabs = [jnp.concatenate([lo, hi], axis=0) for lo, hi in zip(slabs_lo, slabs_hi)]

    colss = []
    for c, (dt, adt) in enumerate(dts):
        sc3 = _scan_mm(adt, u_ref)
        cs = jnp.where(is_fwd, sc3[:, :SSD_CHUNK], sc3[:, SSD_CHUNK:2 * SSD_CHUNK])
        tot = sc3[:, 2 * SSD_CHUNK:]
        e1 = jnp.exp(cs)
        w = jnp.exp(tot - cs) * dt
        dsum = dt[:SSD_HEADS] + dt[SSD_HEADS:]
        rows_ref[0, c] = jnp.concatenate([cs - jnp.log(dt), dsum, jnp.zeros_like(dsum)], axis=0)
        table = jnp.concatenate([zero] * 3 + _split(e1, 2) + _split(w, 2) + [cs], axis=0).T
        colf_ref[0, c] = table
        colss.append(table.astype(BF16))
        cols_ref[0, c] = colss[c]

    pc1 = jnp.dot(ub, w_ref[:, P_CONV:P_CONV + 6 * LANES], preferred_element_type=F32)

    xs = jnp.concatenate(slabs[:4], axis=1)
    xs_ref[0] = xs.astype(BF16)
    cm_ref[0] = jnp.concatenate(slabs[6:8], axis=1).astype(BF16)
    zg_ref[0] = _silu(pz).astype(BF16)
    bts = []
    for c, tok in enumerate(chunks):
        bt = [slabs[4 + g][tok].T.astype(BF16) for g in range(SSD_GROUPS)]
        for g in range(SSD_GROUPS):
            bt_ref[0, c, g] = bt[g]
        bts.append(bt)
    wxbs = [jnp.dot(cols, exw_ref[:, SSD_WIDTH:], preferred_element_type=F32) for cols in colss]
    decs = [jnp.dot(cols[0:16], exe_ref[:, SSD_WIDTH:], preferred_element_type=F32)[0:1]
            for cols in colss]

    pc2 = jnp.dot(ub, w_ref[:, P_CONV + 6 * LANES:], preferred_element_type=F32)
    conv_slabs(pc1, 0)

    state = [st_ref[g] for g in range(SSD_GROUPS)]
    for c in reversed(range(NCH_PROJ)):
        xw = (xs[chunks[c]] * wxbs[c]).astype(BF16)
        for g in range(SSD_GROUPS):
            gs = slice(g * GROUP_COLS, (g + 1) * GROUP_COLS)
            sb_ref[0, c, g] = state[g].astype(BF16)
            local = jnp.dot(bts[c][g], xw[:, gs], preferred_element_type=F32)
            state[g] = state[g] * decs[c][:, gs] + local
    for g in range(SSD_GROUPS):
        st_ref[g] = state[g]
    conv_slabs(pc2, 2)


def _proj_call(x, mod, lng, lnb, w_in, small, exe, exw, u_chunk, h0, w_out, w_ff1, w_ff2):
    n_steps = BATCH * NT_PROJ
    r1, r4 = D_MODEL // n_steps, D_FF // n_steps
    wrow3 = lambda b, t: (0, b * NT_PROJ + t, 0)
    wrow2 = lambda b, t: (b * NT_PROJ + t, 0)
    rev = lambda b, t: (b, NT_PROJ - 1 - t, 0)
    rev4 = lambda b, t: (b, NT_PROJ - 1 - t, 0, 0)
    rev5 = lambda b, t: (b, NT_PROJ - 1 - t, 0, 0, 0)
    tok = lambda width: pl.BlockSpec((1, TM_PROJ, width), rev)
    out_shape = [
        jax.ShapeDtypeStruct((BATCH, SEQ, D_MODEL), F32),
        jax.ShapeDtypeStruct((BATCH, SEQ, CONV_WIDTH), BF16),
        jax.ShapeDtypeStruct((BATCH, SEQ, SSD_WIDTH), BF16),
        jax.ShapeDtypeStruct((BATCH, SEQ, SSD_WIDTH), BF16),
        jax.ShapeDtypeStruct((BATCH, NCHUNK, SSD_GROUPS, SSD_STATE, SSD_CHUNK), BF16),
        jax.ShapeDtypeStruct((BATCH, SEQ, SSD_GN), BF16),
        jax.ShapeDtypeStruct((BATCH, NCHUNK, 2 * N_DH, SSD_CHUNK), F32),
        jax.ShapeDtypeStruct((BATCH, NCHUNK, SSD_CHUNK, LANES), BF16),
        jax.ShapeDtypeStruct((BATCH, NCHUNK, SSD_CHUNK, LANES), F32),
        jax.ShapeDtypeStruct((BATCH, NCHUNK, SSD_GROUPS, SSD_STATE, GROUP_COLS), BF16),
        jax.ShapeDtypeStruct((D_MODEL, D_MODEL), BF16),
        jax.ShapeDtypeStruct((D_MODEL, D_FF), BF16),
        jax.ShapeDtypeStruct((D_FF, D_MODEL), BF16),
    ]
    out_specs = [
        tok(D_MODEL), tok(CONV_WIDTH), tok(SSD_WIDTH), tok(SSD_WIDTH),
        pl.BlockSpec((1, NCH_PROJ, SSD_GROUPS, SSD_STATE, SSD_CHUNK), rev5),
        tok(SSD_GN),
        pl.BlockSpec((1, NCH_PROJ, 2 * N_DH, SSD_CHUNK), rev4),
        pl.BlockSpec((1, NCH_PROJ, SSD_CHUNK, LANES), rev4),
        pl.BlockSpec((1, NCH_PROJ, SSD_CHUNK, LANES), rev4),
        pl.BlockSpec((1, NCH_PROJ, SSD_GROUPS, SSD_STATE, GROUP_COLS), rev5),
        pl.BlockSpec((r1, D_MODEL), wrow2), pl.BlockSpec((r1, D_FF), wrow2), pl.BlockSpec((r4, D_MODEL), wrow2),
    ]
    in_specs = [
        pl.BlockSpec((1, TM_PROJ, D_MODEL), rev),
        _const_spec((MOD_ROWS, 6 * D_MODEL)),
        _const_spec((1, D_MODEL)), _const_spec((1, D_MODEL)),
        _const_spec((D_MODEL, IN_PAD)),
        _const_spec((SMALL_ROWS, D_MODEL)),
        _const_spec((LANES, 2 * SSD_WIDTH)), _const_spec((LANES, 2 * SSD_WIDTH)),
        _const_spec((SSD_CHUNK, 3 * SSD_CHUNK)),
        pl.BlockSpec((1, 1, SSD_GROUPS, SSD_STATE, GROUP_COLS), lambda b, t: (b, 1, 0, 0, 0)),
        pl.BlockSpec((1, r1, D_MODEL), wrow3), pl.BlockSpec((1, r1, D_FF), wrow3),
        pl.BlockSpec((1, r4, D_MODEL), wrow3),
    ]
    return pl.pallas_call(
        _proj_kernel,
        grid=(BATCH, NT_PROJ),
        in_specs=in_specs,
        out_specs=out_specs,
        out_shape=out_shape,
        scratch_shapes=[pltpu.VMEM((SSD_GROUPS, SSD_STATE, GROUP_COLS), F32)],
        compiler_params=pltpu.CompilerParams(dimension_semantics=("arbitrary", "arbitrary"),
                                             vmem_limit_bytes=VMEM_LIMIT),
        name="proj",
    )(x, mod, lng, lnb, w_in, small, exe, exw, u_chunk, h0, w_out, w_ff1, w_ff2)


def _out_kernel(hln_ref, mod_ref, ycv_ref, zg_ref, xs_ref, bt_ref, cm_ref, rows_ref,
                cols_ref, colf_ref, sb_ref, h0_ref, exw_ref, small_ref, wout_ref,
                ln1g_ref, ln1b_ref, wff1_ref, wff2_ref, ln2g_ref, ln2b_ref,
                out_ref, st_ref, yn_ref, h1_ref, u2_ref, g2_ref, r2_ref, hid_ref):
    s = pl.program_id(0)
    wr = s % 2
    rd = (s + 1) % 2
    li = lax.broadcasted_iota(jnp.int32, (SSD_CHUNK, SSD_CHUNK), 0)
    si = lax.broadcasted_iota(jnp.int32, (SSD_CHUNK, SSD_CHUNK), 1)
    low = li >= si
    diag = li == si
    lo_half = si < SSD_HEADDIM

    def ssd_chunk(c, state, fill):
        tok = slice(c * SSD_CHUNK, (c + 1) * SSD_CHUNK)
        rows = rows_ref[0, c]
        cols = cols_ref[0, c]
        xs = xs_ref[0, tok, :]
        cm = cm_ref[0, tok, :]
        colf = colf_ref[0, c]
        bc = [jnp.broadcast_to(colf[:, COL_CSF + j:COL_CSF + j + 1], (SSD_CHUNK, LANES))
              for j in range(N_DH)]
        e1x = [jnp.concatenate(
            [jnp.exp(jnp.where(lo_half, bc[d * SSD_HEADS + 2 * k], bc[d * SSD_HEADS + 2 * k + 1]))
             for k in range(SSD_HEADS // 2)], axis=1) for d in range(N_DIRS)]
        wxf = jnp.dot(cols, exw_ref[:, :SSD_WIDTH], preferred_element_type=F32)
        dec = e1x[0][SSD_CHUNK - 1:SSD_CHUNK, :]
        gmat = [jnp.dot(cm[:, g * SSD_STATE:(g + 1) * SSD_STATE], bt_ref[0, c, g],
                        preferred_element_type=F32) for g in range(SSD_GROUPS)]
        fill[0]()

        ys = []
        for k in range(SSD_HEADS // 2):
            ms = []
            for h in (2 * k, 2 * k + 1):
                g = h // (SSD_HEADS // SSD_GROUPS)
                hb = SSD_HEADS + h
                arg = jnp.where(low,
                                bc[h] - rows[h:h + 1, :],
                                bc[hb] - rows[hb:hb + 1, :])
                decay_dt = jnp.where(diag, rows[N_DH + h:N_DH + h + 1, :], jnp.exp(arg))
                ms.append((gmat[g] * decay_dt).astype(BF16))
            xp = xs[:, k * LANES:(k + 1) * LANES]
            rhs = jnp.concatenate([jnp.where(lo_half, xp, jnp.zeros_like(xp)),
                                   jnp.where(lo_half, jnp.zeros_like(xp), xp)], axis=0)
            ys.append(jnp.dot(jnp.concatenate(ms, axis=1), rhs, preferred_element_type=F32))
            fill[k + 1]()
        y = jnp.concatenate(ys, axis=1)

        yf, yb = [], []
        for g in range(SSD_GROUPS):
            cg = cm[:, g * SSD_STATE:(g + 1) * SSD_STATE]
            yf.append(jnp.dot(cg, state[g].astype(BF16), preferred_element_type=F32))
            yb.append(jnp.dot(cg, sb_ref[0, c, g], preferred_element_type=F32))
        y = (y + jnp.concatenate(yf, axis=1) * e1x[0]
             + jnp.concatenate(yb, axis=1) * e1x[1]
             + xs.astype(F32) * small_ref[S_DXNW:S_DXNW + 1, :SSD_WIDTH])

        yg = y * zg_ref[0, tok, :].astype(F32)
        ms_ = jnp.mean(yg * yg, axis=-1, keepdims=True)
        yn_ref[tok, :CONV_WIDTH] = ycv_ref[0, tok, :]
        yn_ref[tok, CONV_WIDTH:] = (yg * lax.rsqrt(ms_ + RMS_EPS)
                                   * small_ref[S_DXNW:S_DXNW + 1, SSD_WIDTH:]).astype(BF16)
        fill[5]()

        xw = (xs.astype(F32) * wxf).astype(BF16)
        new_state = []
        for g in range(SSD_GROUPS):
            gs = slice(g * GROUP_COLS, (g + 1) * GROUP_COLS)
            local = jnp.dot(bt_ref[0, c, g], xw[:, gs], preferred_element_type=F32)
            new_state.append(state[g] * dec[:, gs] + local)
        return new_state

    def mlp_up(j, k):
        blk = slice(j * FF_BLK + k * FF_SUB, j * FF_BLK + (k + 1) * FF_SUB)
        hid = jnp.maximum(jnp.dot(u2_ref[...], wff1_ref[:, blk], preferred_element_type=F32), 0.0)
        hid_ref[:, blk] = (hid * hid).astype(BF16)

    def mlp_down(j, acc):
        blk = slice(j * FF_BLK, (j + 1) * FF_BLK)
        part = jnp.dot(hid_ref[:, blk], wff2_ref[blk, :], preferred_element_type=F32)
        return part if acc is None else acc + part

    def ln2_rows(i):
        r = slice(i * SSD_CHUNK, (i + 1) * SSD_CHUNK)
        out_ref[0, r, :] = _ln_hat(r2_ref[r, :]) * ln2g_ref[...] + ln2b_ref[...]

    def step(mixer, mlp, norm):
        state, acc = None, [None]
        pieces = []
        if mlp:
            def down(j):
                acc[0] = mlp_down(j, acc[0])
            for j in range(N_SLAB):
                pieces += [(1, functools.partial(mlp_up, j, k)) for k in range(FF_BLK // FF_SUB)]
            for j in range(N_SLAB - DOWN_KEPT):
                at = (j + 1) * (FF_BLK // FF_SUB) + j + DOWN_LAG
                pieces.insert(at, (FF_BLK // FF_SUB, functools.partial(down, j)))
        n_slots = NCH * FILL_SLOTS
        per_slot = sum(cost for cost, _ in pieces) / n_slots
        slots, issued = [], 0.0
        for i in range(n_slots):
            mine = []
            while pieces and issued < (i + 1) * per_slot:
                cost, fn = pieces.pop(0)
                issued += cost
                mine.append(fn)
            slots.append(lambda mine=mine: [fn() for fn in mine])
        if mixer:
            @pl.when(jnp.minimum(s, N_TILES - 1) % NT == 0)
            def _():
                st_ref[...] = h0_ref[0, 0]
            state = [st_ref[g] for g in range(SSD_GROUPS)]
        for c in range(NCH):
            fill = slots[c * FILL_SLOTS:(c + 1) * FILL_SLOTS]
            if mixer:
                state = ssd_chunk(c, state, fill)
            else:
                for f in fill:
                    f()
            if norm:
                ln2_rows(c)
        if mixer:
            for g in range(SSD_GROUPS):
                st_ref[g] = state[g]
            mix = jnp.dot(yn_ref[...], wout_ref[...], preferred_element_type=F32)
            m = _mod_vectors(mod_ref, jnp.minimum(s, N_TILES - 1) // NT)
            g2_ref[wr] = m[5]
        for i in range(DOWN_KEPT):
            if mlp:
                down(N_SLAB - DOWN_KEPT + i)
            if mixer:
                for j in range(i * NCH // DOWN_KEPT, (i + 1) * NCH // DOWN_KEPT):
                    r = slice(j * SSD_CHUNK, (j + 1) * SSD_CHUNK)
                    h1 = _ln_hat(ALPHA * hln_ref[0, r, :] + m[2] * mix[r]) * ln1g_ref[...] + ln1b_ref[...]
                    h1_ref[wr, r, :] = h1
                    u2_ref[r, :] = (h1 * (1.0 + m[4]) + m[3]).astype(BF16)
        if mlp:
            r2_ref[...] = ALPHA * h1_ref[rd] + g2_ref[rd] * acc[0]

    @pl.when(s == 0)
    def _():
        r2_ref[...] = jnp.zeros((TM, D_MODEL), F32)
        step(True, False, False)

    @pl.when(jnp.logical_and(s >= 1, s < N_TILES))
    def _():
        step(True, True, True)

    @pl.when(s == N_TILES)
    def _():
        step(False, True, True)

    @pl.when(s == N_TILES + 1)
    def _():
        step(False, False, True)


def _out_call(hln, mod, ycv, zg, xs, bt, cm, rows, cols, colf, sb, h0, exw, small,
              wout, ln1g, ln1b, wff1, wff2, ln2g, ln2b):
    cur = lambda s: jnp.minimum(s, N_TILES - 1)
    fwd = lambda s: (cur(s) // NT, cur(s) % NT, 0)
    fwd4 = lambda s: (cur(s) // NT, cur(s) % NT, 0, 0)
    fwd5 = lambda s: (cur(s) // NT, cur(s) % NT, 0, 0, 0)
    done = lambda s: (jnp.maximum(s - 2, 0) // NT, jnp.maximum(s - 2, 0) % NT, 0)
    tok = lambda width: pl.BlockSpec((1, TM, width), fwd)
    in_specs = [
        tok(D_MODEL),
        _const_spec((MOD_ROWS, 6 * D_MODEL)),
        tok(CONV_WIDTH), tok(SSD_WIDTH), tok(SSD_WIDTH),
        pl.BlockSpec((1, NCH, SSD_GROUPS, SSD_STATE, SSD_CHUNK), fwd5),
        tok(SSD_GN),
        pl.BlockSpec((1, NCH, 2 * N_DH, SSD_CHUNK), fwd4),
        pl.BlockSpec((1, NCH, SSD_CHUNK, LANES), fwd4),
        pl.BlockSpec((1, NCH, SSD_CHUNK, LANES), fwd4),
        pl.BlockSpec((1, NCH, SSD_GROUPS, SSD_STATE, GROUP_COLS), fwd5),
        pl.BlockSpec((1, 1, SSD_GROUPS, SSD_STATE, GROUP_COLS), lambda s: (cur(s) // NT, 0, 0, 0, 0)),
        _const_spec((LANES, 2 * SSD_WIDTH)),
        _const_spec((SMALL_ROWS, D_MODEL)),
        _const_spec((D_MODEL, D_MODEL)),
        _const_spec((1, D_MODEL)), _const_spec((1, D_MODEL)),
        _const_spec((D_MODEL, D_FF)), _const_spec((D_FF, D_MODEL)),
        _const_spec((1, D_MODEL)), _const_spec((1, D_MODEL)),
    ]
    return pl.pallas_call(
        _out_kernel,
        grid=(N_TILES + 2,),
        in_specs=in_specs,
        out_specs=pl.BlockSpec((1, TM, D_MODEL), done),
        out_shape=jax.ShapeDtypeStruct((BATCH, SEQ, D_MODEL), F32),
        scratch_shapes=[pltpu.VMEM((SSD_GROUPS, SSD_STATE, GROUP_COLS), F32),
                        pltpu.VMEM((TM, D_MODEL), BF16),
                        pltpu.VMEM((2, TM, D_MODEL), F32),
                        pltpu.VMEM((TM, D_MODEL), BF16),
                        pltpu.VMEM((2, 1, D_MODEL), F32),
                        pltpu.VMEM((TM, D_MODEL), F32),
                        pltpu.VMEM((TM, D_FF), BF16)],
        compiler_params=pltpu.CompilerParams(dimension_semantics=("arbitrary",),
                                             vmem_limit_bytes=VMEM_LIMIT),
        name="out",
    )(hln, mod, ycv, zg, xs, bt, cm, rows, cols, colf, sb, h0, exw, small,
      wout, ln1g, ln1b, wff1, wff2, ln2g, ln2b)


def kernel(x, c, ctx, c_ctx, ln_in_g, ln_in_b, w_mod, b_mod, w_in, conv_w, ssd_conv_w, ssd_conv_b,
           dt_bias, a_log, ssd_d, ssd_norm_w, w_out, ln1_g, ln1_b, w_ff1, w_ff2, ln2_g, ln2_b):
    row = lambda v: v.reshape(1, -1).astype(F32)
    cvec = jnp.concatenate([c, c_ctx[None, :], jnp.zeros((MOD_ROWS - BATCH - 1, D_MODEL), F32)], axis=0)
    mod, w_in_p = _prep_call(cvec, w_mod, b_mod, w_in[0].T)

    lng, lnb = row(ln_in_g), row(ln_in_b)
    pad_to = lambda v: jnp.pad(v, ((0, 0), (0, D_MODEL - v.shape[1])))
    small = jnp.concatenate([
        ssd_conv_w[0], row(ssd_conv_b[0]), pad_to(conv_w[0]),
        pad_to(jnp.concatenate([dt_bias[0].reshape(1, N_DH), a_log[0].reshape(1, N_DH)], axis=1)),
        jnp.concatenate([jnp.repeat(ssd_d[0], SSD_HEADDIM).reshape(1, SSD_WIDTH), row(ssd_norm_w[0])], axis=1),
        jnp.zeros((SMALL_ROWS - S_DXNW - 1, D_MODEL), F32)], axis=0)
    exe = jnp.asarray(_EXE, BF16)
    exw = jnp.asarray(_EXW, BF16)

    h0 = _ctx_call(ctx, mod, lng, lnb, w_in_p, small, exw, jnp.asarray(_U_CTX, BF16))
    hln, ycv, zg, xs, bt, cm, rows, cols, colf, sb, w_out_p, w_ff1_p, w_ff2_p = _proj_call(
        x, mod, lng, lnb, w_in_p, small, exe, exw,
        jnp.asarray(_U_CHUNK, BF16), h0, w_out, w_ff1, w_ff2)
    return _out_call(hln, mod, ycv, zg, xs, bt, cm, rows, cols, colf, sb, h0, exw,
                     small, w_out_p, row(ln1_g[0]), row(ln1_b[0]),
                     w_ff1_p, w_ff2_p, row(ln2_g[0]), row(ln2_b[0]))
```

```python
import functools

import jax
import jax.numpy as jnp
import numpy as np
from jax import lax
from jax.experimental import pallas as pl
from jax.experimental.pallas import tpu as pltpu

F32 = jnp.float32
BF16 = jnp.bfloat16

D_MODEL = 1024
BATCH = 8
SEQ = 2048
CTX_LEN = 256
GRID_W = 64
CONV_WIDTH = 512
SSD_WIDTH = 512
SSD_HEADDIM = 64
SSD_HEADS = 8
SSD_GROUPS = 2
SSD_STATE = 128
SSD_CHUNK = 128
N_DIRS = 2
D_FF = 4 * D_MODEL
LN_EPS = 1e-5
RMS_EPS = 1e-5
SSD_GN = SSD_GROUPS * SSD_STATE
XBC_DIM = SSD_WIDTH + 2 * SSD_GN
Z_OFF = 3 * CONV_WIDTH
XBC_OFF = Z_OFF + SSD_WIDTH
DT_OFF = XBC_OFF + XBC_DIM
N_DH = N_DIRS * SSD_HEADS
LANES = 128
P_DT = 0
P_XBC = LANES
P_Z = P_XBC + XBC_DIM
P_CONV = P_Z + SSD_WIDTH
IN_PAD = P_CONV + 3 * CONV_WIDTH
CTX_PAD = P_XBC + SSD_WIDTH + SSD_GN
GROUP_COLS = (SSD_HEADS // SSD_GROUPS) * SSD_HEADDIM
ALPHA = 2.0 ** 0.25

TM = 512
CTX_ROWS = 2
MOD_ROWS = 16
SMALL_ROWS = 16
G2_ROWS = 8
PREP_STEPS = 8
NCH = TM // SSD_CHUNK
NT = SEQ // TM
NCHUNK = SEQ // SSD_CHUNK
N_TILES = BATCH * NT
TM_PROJ = 1024
NCH_PROJ = TM_PROJ // SSD_CHUNK
NT_PROJ = SEQ // TM_PROJ
FF_BLK = 1024
FF_SUB = 256
N_SLAB = D_FF // FF_BLK
FILL_SLOTS = 6
DOWN_KEPT = 2
DOWN_LAG = 1
VMEM_LIMIT = 58 * 1024 * 1024

COL_E1 = 48
COL_W = 80
COL_CSF = 112


def _expansion(col0, pieces, width):
    m = np.zeros((LANES, N_DH * width), np.float32)
    for t in range(pieces):
        for j in range(N_DH):
            m[col0 + 16 * t + j, j * width:(j + 1) * width] = 1.0
    return m


_EXE = _expansion(COL_E1, 2, SSD_HEADDIM)
_EXW = _expansion(COL_W, 2, SSD_HEADDIM)


def _ln_hat(x):
    mu = jnp.mean(x, axis=-1, keepdims=True)
    xc = x - mu
    var = jnp.mean(xc * xc, axis=-1, keepdims=True)
    return xc * lax.rsqrt(var + LN_EPS)


def _silu(x):
    return x / (1.0 + jnp.exp(-x))


def _softplus(x):
    return jnp.maximum(x, 0.0) + jnp.log1p(jnp.exp(-jnp.abs(x)))


def _edge_masks(rows, period):
    pos = lax.broadcasted_iota(jnp.int32, (rows, LANES), 0) % period
    return (pos != 0).astype(F32), (pos != period - 1).astype(F32)


def _conv3(t, w, mprev, mnext):
    rows = t.shape[0]
    prev = pltpu.roll(t, 1, 0) * mprev
    nxt = pltpu.roll(t, rows - 1, 0) * mnext
    return prev * w[0:1, :] + t * w[1:2, :] + nxt * w[2:3, :]


def _split(v, pieces):
    out = []
    for _ in range(pieces - 1):
        p = v.astype(BF16).astype(F32)
        out.append(p)
        v = v - p
    out.append(v.astype(BF16).astype(F32))
    return out


def _tri(length, op):
    i = np.arange(length)
    return op(i[:, None], i[None, :]).astype(np.float32)


_U_CHUNK = np.concatenate([_tri(SSD_CHUNK, np.less_equal), _tri(SSD_CHUNK, np.greater_equal),
                           np.ones((SSD_CHUNK, SSD_CHUNK), np.float32)], axis=1)
_U_CTX = np.concatenate([_tri(CTX_LEN, np.greater), _tri(CTX_LEN, np.less)], axis=1)


def _scan_mm(v, u_ref):
    pieces = jnp.concatenate(_split(v, 3), axis=0).astype(BF16)
    o = jnp.dot(pieces, u_ref[...], preferred_element_type=F32)
    return o[0:N_DH] + o[N_DH:2 * N_DH] + o[2 * N_DH:3 * N_DH]


def _dt_rows(raw, dtb, a_log):
    r = raw.T[0:N_DH, :] + dtb
    dt = _softplus(r)
    return dt, dt * (-jnp.exp(a_log))


def _fwd_rows():
    return lax.broadcasted_iota(jnp.int32, (N_DH, 1), 0) < SSD_HEADS


S_SCW = 0
S_SCB = 3
S_CW = 4
S_DT = 7
S_DXNW = 8
S_LN = 9


def _dt_params(small_ref):
    t = jnp.broadcast_to(small_ref[S_DT:S_DT + 1, 0:LANES], (N_DH, LANES))
    sub = lax.broadcasted_iota(jnp.int32, (N_DH, LANES), 0)
    lane = lax.broadcasted_iota(jnp.int32, (N_DH, LANES), 1)
    pick = lambda off: jnp.sum(jnp.where(lane == sub + off, t, 0.0), axis=1, keepdims=True)
    return pick(0), pick(N_DH)


def _ln_params(small_ref, which):
    r = S_LN + 2 * which
    return small_ref[r:r + 1, :], small_ref[r + 1:r + 2, :]


def _mod_vectors(mod_ref, row):
    r = mod_ref[pl.ds(row, 1), :]
    return [r[:, k * D_MODEL:(k + 1) * D_MODEL] for k in range(6)]


def _const_spec(shape):
    nd = len(shape)
    return pl.BlockSpec(shape, lambda *_: (0,) * nd, pipeline_mode=pl.Buffered(1))


def _prep_kernel(c_ref, wmod_ref, bmod_ref, wint_ref, mod_ref, pin_ref):
    cs = _silu(c_ref[...]).astype(BF16)
    mod_ref[...] = jnp.dot(cs, wmod_ref[0].astype(BF16), preferred_element_type=F32) + bmod_ref[...]

    dt_rows = jnp.concatenate([wint_ref[DT_OFF:DT_OFF + N_DH, :],
                               jnp.zeros((LANES - N_DH, wint_ref.shape[1]), F32)], axis=0)
    pin_ref[:, P_DT:P_DT + LANES] = dt_rows.T.astype(BF16)
    def move(dst, src):
        pin_ref[:, dst:dst + LANES] = wint_ref[src:src + LANES, :].T.astype(BF16)

    for dst, src, width in ((P_XBC, XBC_OFF, XBC_DIM), (P_Z, Z_OFF, SSD_WIDTH)):
        for j in range(0, width, LANES):
            move(dst + j, src + j)
    for j in range(CONV_WIDTH // LANES):
        for k in range(3):
            move(P_CONV + (3 * j + k) * LANES, k * CONV_WIDTH + j * LANES)


def _prep_call(cvec, w_mod, b_mod, w_in_t):
    steps = PREP_STEPS
    r1, rm = D_MODEL // steps, 6 * D_MODEL // steps
    return pl.pallas_call(
        _prep_kernel,
        grid=(steps,),
        in_specs=[pl.BlockSpec((MOD_ROWS, D_MODEL), lambda i: (0, 0)),
                  pl.BlockSpec((1, D_MODEL, rm), lambda i: (0, 0, i)),
                  pl.BlockSpec((1, rm), lambda i: (0, i)),
                  pl.BlockSpec((w_in_t.shape[0], r1), lambda i: (0, i))],
        out_specs=[pl.BlockSpec((MOD_ROWS, rm), lambda i: (0, i)),
                   pl.BlockSpec((r1, IN_PAD), lambda i: (i, 0))],
        out_shape=[jax.ShapeDtypeStruct((MOD_ROWS, 6 * D_MODEL), F32),
                   jax.ShapeDtypeStruct((D_MODEL, IN_PAD), BF16)],
        compiler_params=pltpu.CompilerParams(dimension_semantics=("arbitrary",),
                                             vmem_limit_bytes=VMEM_LIMIT),
        name="prep",
    )(cvec, w_mod, b_mod, w_in_t)


def _ctx_kernel(x_ref, mod_ref, w_ref, small_ref, exw_ref, u_ref, h0_ref):
    dtb, alog = _dt_params(small_ref)
    lng, lnb = _ln_params(small_ref, 0)
    m = _mod_vectors(mod_ref, BATCH)
    sc = 1.0 + m[1]
    x = x_ref[...].reshape(CTX_ROWS * CTX_LEN, D_MODEL)
    u = _ln_hat(x) * (lng * sc) + (lnb * sc + m[0])
    proj = jnp.dot(u.astype(BF16), w_ref[...], preferred_element_type=F32)
    mprev, mnext = _edge_masks(CTX_ROWS * CTX_LEN, CTX_LEN)
    slabs = []
    for j in range((SSD_WIDTH + SSD_GN) // LANES):
        sl = slice(j * LANES, (j + 1) * LANES)
        pj = proj[:, P_XBC + j * LANES:P_XBC + (j + 1) * LANES]
        slabs.append(_silu(_conv3(pj, small_ref[S_SCW:S_SCW + 3, sl], mprev, mnext)
                           + small_ref[S_SCB:S_SCB + 1, sl]))
    zero = jnp.zeros((N_DH, CTX_LEN), F32)
    for i in range(CTX_ROWS):
        tok = slice(i * CTX_LEN, (i + 1) * CTX_LEN)
        xs = jnp.concatenate([sl_[tok] for sl_ in slabs[:4]], axis=1)
        dt, adt = _dt_rows(proj[tok, P_DT:P_DT + LANES], dtb, alog)
        sc2 = _scan_mm(adt, u_ref)
        excl = jnp.where(_fwd_rows(), sc2[:, :CTX_LEN], sc2[:, CTX_LEN:])
        w = jnp.exp(excl) * dt
        table = jnp.concatenate([zero] * (COL_W // 16) + _split(w, 2) + [zero], axis=0)
        cols = table.T.astype(BF16)
        wx = jnp.dot(cols, exw_ref[...], preferred_element_type=F32)
        for d in range(N_DIRS):
            xw = (xs * wx[:, d * SSD_WIDTH:(d + 1) * SSD_WIDTH]).astype(BF16)
            for g in range(SSD_GROUPS):
                bt = slabs[4 + g][tok].T.astype(BF16)
                h0_ref[i, d, g] = jnp.dot(bt, xw[:, g * GROUP_COLS:(g + 1) * GROUP_COLS],
                                          preferred_element_type=F32)


def _ctx_call(ctx, mod, w_in, small, exw, u_ctx):
    return pl.pallas_call(
        _ctx_kernel,
        grid=(BATCH // CTX_ROWS,),
        in_specs=[pl.BlockSpec((CTX_ROWS, CTX_LEN, D_MODEL), lambda b: (b, 0, 0)),
                  _const_spec((MOD_ROWS, 6 * D_MODEL)),
                  _const_spec((D_MODEL, CTX_PAD)), _const_spec((SMALL_ROWS, D_MODEL)),
                  _const_spec((LANES, 2 * SSD_WIDTH)),
                  _const_spec((CTX_LEN, 2 * CTX_LEN))],
        out_specs=pl.BlockSpec((CTX_ROWS, N_DIRS, SSD_GROUPS, SSD_STATE, GROUP_COLS),
                               lambda b: (b, 0, 0, 0, 0)),
        out_shape=jax.ShapeDtypeStruct((BATCH, N_DIRS, SSD_GROUPS, SSD_STATE, GROUP_COLS), F32),
        compiler_params=pltpu.CompilerParams(dimension_semantics=("arbitrary",),
                                             vmem_limit_bytes=VMEM_LIMIT),
        name="ctx",
    )(ctx, mod, w_in, small, exw, u_ctx)


def _proj_kernel(x_ref, mod_ref, w_ref, small_ref,
                 exe_ref, exw_ref, u_ref, h0_ref, wout_ref, wff1_ref, wff2_ref,
                 hln_ref, ycv_ref, zg_ref, xs_ref, bt_ref, cm_ref, rows_ref, cols_ref, colf_ref, sb_ref,
                 pout_ref, pff1_ref, pff2_ref, st_ref):
    pout_ref[...] = wout_ref[0].astype(BF16)
    pff1_ref[...] = wff1_ref[0].astype(BF16)
    pff2_ref[...] = wff2_ref[0].astype(BF16)

    @pl.when(pl.program_id(1) == 0)
    def _():
        st_ref[...] = h0_ref[0, 0]

    m = _mod_vectors(mod_ref, pl.program_id(0))
    sc = 1.0 + m[1]
    lng, lnb = _ln_params(small_ref, 0)
    scale, shift = lng * sc, lnb * sc + m[0]
    ubs, pas = [], []
    for r in range(2):
        rows = slice(r * (TM_PROJ // 2), (r + 1) * (TM_PROJ // 2))
        xhat = _ln_hat(x_ref[0, rows, :])
        hln_ref[0, rows, :] = xhat * lng + lnb
        ubs.append((xhat * scale + shift).astype(BF16))
        pas.append(jnp.dot(ubs[r], w_ref[:, :P_Z], preferred_element_type=F32))
    ub = jnp.concatenate(ubs, axis=0)
    pa = jnp.concatenate(pas, axis=0)
    mprev, mnext = _edge_masks(TM_PROJ, GRID_W)
    half = TM_PROJ // 2

    def xbc_half(r):
        out = []
        for j in range(XBC_DIM // LANES):
            sl = slice(j * LANES, (j + 1) * LANES)
            pj = pas[r][:, P_XBC + j * LANES:P_XBC + (j + 1) * LANES]
            out.append(_silu(_conv3(pj, small_ref[S_SCW:S_SCW + 3, sl], mprev[:half], mnext[:half])
                             + small_ref[S_SCB:S_SCB + 1, sl]))
        return out
    is_fwd = _fwd_rows()
    zero = jnp.zeros((N_DH, SSD_CHUNK), F32)
    chunks = [slice(c * SSD_CHUNK, (c + 1) * SSD_CHUNK) for c in range(NCH_PROJ)]

    def conv_slabs(p, first):
        for i in range(2):
            gb, gc, gh = (p[:, (3 * i + k) * LANES:(3 * i + k + 1) * LANES] for k in range(3))
            sl = slice((first + i) * LANES, (first + i + 1) * LANES)
            ycv_ref[0, :, sl] = (gb * _conv3(gc * gh, small_ref[S_CW:S_CW + 3, sl], mprev, mnext)).astype(BF16)

    dtb, alog = _dt_params(small_ref)
    slabs_lo = xbc_half(0)
    dts = [_dt_rows(pa[tok, P_DT:P_DT + LANES], dtb, alog) for tok in chunks]
    pz = jnp.dot(ub, w_ref[:, P_Z:P_CONV], preferred_element_type=F32)
    slabs_hi = xbc_half(1)
    slabs = [jnp.concatenate([lo, hi], axis=0) for lo, hi in zip(slabs_lo, slabs_hi)]

    colss = []
    for c, (dt, adt) in enumerate(dts):
        sc3 = _scan_mm(adt, u_ref)
        cs = jnp.where(is_fwd, sc3[:, :SSD_CHUNK], sc3[:, SSD_CHUNK:2 * SSD_CHUNK])
        tot = sc3[:, 2 * SSD_CHUNK:]
        e1 = jnp.exp(cs)
        w = jnp.exp(tot - cs) * dt
        dsum = dt[:SSD_HEADS] + dt[SSD_HEADS:]
        rows_ref[0, c] = jnp.concatenate([cs - jnp.log(dt), dsum, jnp.zeros_like(dsum)], axis=0)
        table = jnp.concatenate([zero] * 3 + _split(e1, 2) + _split(w, 2) + [cs], axis=0).T
        colf_ref[0, c] = table
        colss.append(table.astype(BF16))
        cols_ref[0, c] = colss[c]

    pc1 = jnp.dot(ub, w_ref[:, P_CONV:P_CONV + 6 * LANES], preferred_element_type=F32)

    xs = jnp.concatenate(slabs[:4], axis=1)
    xs_ref[0] = xs.astype(BF16)
    cm_ref[0] = jnp.concatenate(slabs[6:8], axis=1).astype(BF16)
    zg_ref[0] = _silu(pz).astype(BF16)
    bts = []
    for c, tok in enumerate(chunks):
        bt = [slabs[4 + g][tok].T.astype(BF16) for g in range(SSD_GROUPS)]
        for g in range(SSD_GROUPS):
            bt_ref[0, c, g] = bt[g]
        bts.append(bt)
    wxbs = [jnp.dot(cols, exw_ref[:, SSD_WIDTH:], preferred_element_type=F32) for cols in colss]
    decs = [jnp.dot(cols[0:16], exe_ref[:, SSD_WIDTH:], preferred_element_type=F32)[0:1]
            for cols in colss]

    pc2 = jnp.dot(ub, w_ref[:, P_CONV + 6 * LANES:], preferred_element_type=F32)
    conv_slabs(pc1, 0)

    state = [st_ref[g] for g in range(SSD_GROUPS)]
    for c in reversed(range(NCH_PROJ)):
        xw = (xs[chunks[c]] * wxbs[c]).astype(BF16)
        for g in range(SSD_GROUPS):
            gs = slice(g * GROUP_COLS, (g + 1) * GROUP_COLS)
            sb_ref[0, c, g] = state[g].astype(BF16)
            local = jnp.dot(bts[c][g], xw[:, gs], preferred_element_type=F32)
            state[g] = state[g] * decs[c][:, gs] + local
    for g in range(SSD_GROUPS):
        st_ref[g] = state[g]
    conv_slabs(pc2, 2)


def _proj_call(x, mod, w_in, small, exe, exw, u_chunk, h0, w_out, w_ff1, w_ff2):
    n_steps = BATCH * NT_PROJ
    r1, r4 = D_MODEL // n_steps, D_FF // n_steps
    wrow3 = lambda b, t: (0, b * NT_PROJ + t, 0)
    wrow2 = lambda b, t: (b * NT_PROJ + t, 0)
    rev = lambda b, t: (b, NT_PROJ - 1 - t, 0)
    rev4 = lambda b, t: (b, NT_PROJ - 1 - t, 0, 0)
    rev5 = lambda b, t: (b, NT_PROJ - 1 - t, 0, 0, 0)
    tok = lambda width: pl.BlockSpec((1, TM_PROJ, width), rev)
    out_shape = [
        jax.ShapeDtypeStruct((BATCH, SEQ, D_MODEL), F32),
        jax.ShapeDtypeStruct((BATCH, SEQ, CONV_WIDTH), BF16),
        jax.ShapeDtypeStruct((BATCH, SEQ, SSD_WIDTH), BF16),
        jax.ShapeDtypeStruct((BATCH, SEQ, SSD_WIDTH), BF16),
        jax.ShapeDtypeStruct((BATCH, NCHUNK, SSD_GROUPS, SSD_STATE, SSD_CHUNK), BF16),
        jax.ShapeDtypeStruct((BATCH, SEQ, SSD_GN), BF16),
        jax.ShapeDtypeStruct((BATCH, NCHUNK, 2 * N_DH, SSD_CHUNK), F32),
        jax.ShapeDtypeStruct((BATCH, NCHUNK, SSD_CHUNK, LANES), BF16),
        jax.ShapeDtypeStruct((BATCH, NCHUNK, SSD_CHUNK, LANES), F32),
        jax.ShapeDtypeStruct((BATCH, NCHUNK, SSD_GROUPS, SSD_STATE, GROUP_COLS), BF16),
        jax.ShapeDtypeStruct((D_MODEL, D_MODEL), BF16),
        jax.ShapeDtypeStruct((D_MODEL, D_FF), BF16),
        jax.ShapeDtypeStruct((D_FF, D_MODEL), BF16),
    ]
    out_specs = [
        tok(D_MODEL), tok(CONV_WIDTH), tok(SSD_WIDTH), tok(SSD_WIDTH),
        pl.BlockSpec((1, NCH_PROJ, SSD_GROUPS, SSD_STATE, SSD_CHUNK), rev5),
        tok(SSD_GN),
        pl.BlockSpec((1, NCH_PROJ, 2 * N_DH, SSD_CHUNK), rev4),
        pl.BlockSpec((1, NCH_PROJ, SSD_CHUNK, LANES), rev4),
        pl.BlockSpec((1, NCH_PROJ, SSD_CHUNK, LANES), rev4),
        pl.BlockSpec((1, NCH_PROJ, SSD_GROUPS, SSD_STATE, GROUP_COLS), rev5),
        pl.BlockSpec((r1, D_MODEL), wrow2), pl.BlockSpec((r1, D_FF), wrow2), pl.BlockSpec((r4, D_MODEL), wrow2),
    ]
    in_specs = [
        pl.BlockSpec((1, TM_PROJ, D_MODEL), rev),
        _const_spec((MOD_ROWS, 6 * D_MODEL)),
        _const_spec((D_MODEL, IN_PAD)),
        _const_spec((SMALL_ROWS, D_MODEL)),
        _const_spec((LANES, 2 * SSD_WIDTH)), _const_spec((LANES, 2 * SSD_WIDTH)),
        _const_spec((SSD_CHUNK, 3 * SSD_CHUNK)),
        pl.BlockSpec((1, 1, SSD_GROUPS, SSD_STATE, GROUP_COLS), lambda b, t: (b, 1, 0, 0, 0)),
        pl.BlockSpec((1, r1, D_MODEL), wrow3), pl.BlockSpec((1, r1, D_FF), wrow3),
        pl.BlockSpec((1, r4, D_MODEL), wrow3),
    ]
    return pl.pallas_call(
        _proj_kernel,
        grid=(BATCH, NT_PROJ),
        in_specs=in_specs,
        out_specs=out_specs,
        out_shape=out_shape,
        scratch_shapes=[pltpu.VMEM((SSD_GROUPS, SSD_STATE, GROUP_COLS), F32)],
        compiler_params=pltpu.CompilerParams(dimension_semantics=("arbitrary", "arbitrary"),
                                             vmem_limit_bytes=VMEM_LIMIT),
        name="proj",
    )(x, mod, w_in, small, exe, exw, u_chunk, h0, w_out, w_ff1, w_ff2)


def _out_kernel(hln_ref, mod_ref, ycv_ref, zg_ref, xs_ref, bt_ref, cm_ref, rows_ref,
                cols_ref, colf_ref, sb_ref, h0_ref, exw_ref, small_ref, wout_ref,
                wff1_ref, wff2_ref,
                out_ref, st_ref, yn_ref, h1_ref, u2_ref, g2_ref, r2_ref, hid_ref):
    s = pl.program_id(0)
    wr = s % 2
    rd = (s + 1) % 2
    li = lax.broadcasted_iota(jnp.int32, (SSD_CHUNK, SSD_CHUNK), 0)
    si = lax.broadcasted_iota(jnp.int32, (SSD_CHUNK, SSD_CHUNK), 1)
    low = li >= si
    diag = li == si
    lo_half = si < SSD_HEADDIM

    def ssd_chunk(c, state, fill):
        tok = slice(c * SSD_CHUNK, (c + 1) * SSD_CHUNK)
        rows = rows_ref[0, c]
        cols = cols_ref[0, c]
        xs = xs_ref[0, tok, :]
        cm = cm_ref[0, tok, :]
        colf = colf_ref[0, c]
        bc = [jnp.broadcast_to(colf[:, COL_CSF + j:COL_CSF + j + 1], (SSD_CHUNK, LANES))
              for j in range(N_DH)]
        e1x = [jnp.concatenate(
            [jnp.exp(jnp.where(lo_half, bc[d * SSD_HEADS + 2 * k], bc[d * SSD_HEADS + 2 * k + 1]))
             for k in range(SSD_HEADS // 2)], axis=1) for d in range(N_DIRS)]
        wxf = jnp.dot(cols, exw_ref[:, :SSD_WIDTH], preferred_element_type=F32)
        dec = e1x[0][SSD_CHUNK - 1:SSD_CHUNK, :]
        gmat = [jnp.dot(cm[:, g * SSD_STATE:(g + 1) * SSD_STATE], bt_ref[0, c, g],
                        preferred_element_type=F32) for g in range(SSD_GROUPS)]
        fill[0]()

        ys = []
        for k in range(SSD_HEADS // 2):
            ms = []
            for h in (2 * k, 2 * k + 1):
                g = h // (SSD_HEADS // SSD_GROUPS)
                hb = SSD_HEADS + h
                arg = jnp.where(low,
                                bc[h] - rows[h:h + 1, :],
                                bc[hb] - rows[hb:hb + 1, :])
                decay_dt = jnp.where(diag, rows[N_DH + h:N_DH + h + 1, :], jnp.exp(arg))
                ms.append((gmat[g] * decay_dt).astype(BF16))
            xp = xs[:, k * LANES:(k + 1) * LANES]
            rhs = jnp.concatenate([jnp.where(lo_half, xp, jnp.zeros_like(xp)),
                                   jnp.where(lo_half, jnp.zeros_like(xp), xp)], axis=0)
            ys.append(jnp.dot(jnp.concatenate(ms, axis=1), rhs, preferred_element_type=F32))
            fill[k + 1]()
        y = jnp.concatenate(ys, axis=1)

        yf, yb = [], []
        for g in range(SSD_GROUPS):
            cg = cm[:, g * SSD_STATE:(g + 1) * SSD_STATE]
            yf.append(jnp.dot(cg, state[g].astype(BF16), preferred_element_type=F32))
            yb.append(jnp.dot(cg, sb_ref[0, c, g], preferred_element_type=F32))
        y = (y + jnp.concatenate(yf, axis=1) * e1x[0]
             + jnp.concatenate(yb, axis=1) * e1x[1]
             + xs.astype(F32) * small_ref[S_DXNW:S_DXNW + 1, :SSD_WIDTH])

        yg = y * zg_ref[0, tok, :].astype(F32)
        ms_ = jnp.mean(yg * yg, axis=-1, keepdims=True)
        yn_ref[tok, :CONV_WIDTH] = ycv_ref[0, tok, :]
        yn_ref[tok, CONV_WIDTH:] = (yg * lax.rsqrt(ms_ + RMS_EPS)
                                   * small_ref[S_DXNW:S_DXNW + 1, SSD_WIDTH:]).astype(BF16)
        fill[5]()

        xw = (xs.astype(F32) * wxf).astype(BF16)
        new_state = []
        for g in range(SSD_GROUPS):
            gs = slice(g * GROUP_COLS, (g + 1) * GROUP_COLS)
            local = jnp.dot(bt_ref[0, c, g], xw[:, gs], preferred_element_type=F32)
            new_state.append(state[g] * dec[:, gs] + local)
        return new_state

    def mlp_up(j, k):
        blk = slice(j * FF_BLK + k * FF_SUB, j * FF_BLK + (k + 1) * FF_SUB)
        hid = jnp.maximum(jnp.dot(u2_ref[...], wff1_ref[:, blk], preferred_element_type=F32), 0.0)
        hid_ref[:, blk] = (hid * hid).astype(BF16)

    def mlp_down(j, acc):
        blk = slice(j * FF_BLK, (j + 1) * FF_BLK)
        part = jnp.dot(hid_ref[:, blk], wff2_ref[blk, :], preferred_element_type=F32)
        return part if acc is None else acc + part

    def ln2_rows(i):
        r = slice(i * SSD_CHUNK, (i + 1) * SSD_CHUNK)
        gain, bias = _ln_params(small_ref, 2)
        out_ref[0, r, :] = _ln_hat(r2_ref[r, :]) * gain + bias

    def step(mixer, mlp, norm):
        state, acc = None, [None]
        pieces = []
        if mlp:
            def down(j):
                acc[0] = mlp_down(j, acc[0])
            for j in range(N_SLAB):
                pieces += [(1, functools.partial(mlp_up, j, k)) for k in range(FF_BLK // FF_SUB)]
            for j in range(N_SLAB - DOWN_KEPT):
                at = (j + 1) * (FF_BLK // FF_SUB) + j + DOWN_LAG
                pieces.insert(at, (FF_BLK // FF_SUB, functools.partial(down, j)))
        n_slots = NCH * FILL_SLOTS
        per_slot = sum(cost for cost, _ in pieces) / n_slots
        slots, issued = [], 0.0
        for i in range(n_slots):
            mine = []
            while pieces and issued < (i + 1) * per_slot:
                cost, fn = pieces.pop(0)
                issued += cost
                mine.append(fn)
            slots.append(lambda mine=mine: [fn() for fn in mine])
        if mixer:
            @pl.when(jnp.minimum(s, N_TILES - 1) % NT == 0)
            def _():
                st_ref[...] = h0_ref[0, 0]
            state = [st_ref[g] for g in range(SSD_GROUPS)]
        for c in range(NCH):
            fill = slots[c * FILL_SLOTS:(c + 1) * FILL_SLOTS]
            if mixer:
                state = ssd_chunk(c, state, fill)
            else:
                for f in fill:
                    f()
            if norm:
                ln2_rows(c)
        if mixer:
            for g in range(SSD_GROUPS):
                st_ref[g] = state[g]
            mix = jnp.dot(yn_ref[...], wout_ref[...], preferred_element_type=F32)
            m = _mod_vectors(mod_ref, jnp.minimum(s, N_TILES - 1) // NT)
            g2_ref[wr, 0:1, :] = m[5]
        for i in range(DOWN_KEPT):
            if mlp:
                down(N_SLAB - DOWN_KEPT + i)
            if mixer:
                for j in range(i * NCH // DOWN_KEPT, (i + 1) * NCH // DOWN_KEPT):
                    r = slice(j * SSD_CHUNK, (j + 1) * SSD_CHUNK)
                    gain, bias = _ln_params(small_ref, 1)
                    h1 = _ln_hat(ALPHA * hln_ref[0, r, :] + m[2] * mix[r]) * gain + bias
                    h1_ref[wr, r, :] = h1
                    u2_ref[r, :] = (h1 * (1.0 + m[4]) + m[3]).astype(BF16)
        if mlp:
            r2_ref[...] = ALPHA * h1_ref[rd] + g2_ref[rd, 0:1, :] * acc[0]

    @pl.when(s == 0)
    def _():
        r2_ref[...] = jnp.zeros((TM, D_MODEL), F32)
        step(True, False, False)

    @pl.when(jnp.logical_and(s >= 1, s < N_TILES))
    def _():
        step(True, True, True)

    @pl.when(s == N_TILES)
    def _():
        step(False, True, True)

    @pl.when(s == N_TILES + 1)
    def _():
        step(False, False, True)


def _out_call(hln, mod, ycv, zg, xs, bt, cm, rows, cols, colf, sb, h0, exw, small,
              wout, wff1, wff2):
    cur = lambda s: jnp.minimum(s, N_TILES - 1)
    fwd = lambda s: (cur(s) // NT, cur(s) % NT, 0)
    fwd4 = lambda s: (cur(s) // NT, cur(s) % NT, 0, 0)
    fwd5 = lambda s: (cur(s) // NT, cur(s) % NT, 0, 0, 0)
    done = lambda s: (jnp.maximum(s - 2, 0) // NT, jnp.maximum(s - 2, 0) % NT, 0)
    tok = lambda width: pl.BlockSpec((1, TM, width), fwd)
    in_specs = [
        tok(D_MODEL),
        _const_spec((MOD_ROWS, 6 * D_MODEL)),
        tok(CONV_WIDTH), tok(SSD_WIDTH), tok(SSD_WIDTH),
        pl.BlockSpec((1, NCH, SSD_GROUPS, SSD_STATE, SSD_CHUNK), fwd5),
        tok(SSD_GN),
        pl.BlockSpec((1, NCH, 2 * N_DH, SSD_CHUNK), fwd4),
        pl.BlockSpec((1, NCH, SSD_CHUNK, LANES), fwd4),
        pl.BlockSpec((1, NCH, SSD_CHUNK, LANES), fwd4),
        pl.BlockSpec((1, NCH, SSD_GROUPS, SSD_STATE, GROUP_COLS), fwd5),
        pl.BlockSpec((1, 1, SSD_GROUPS, SSD_STATE, GROUP_COLS), lambda s: (cur(s) // NT, 0, 0, 0, 0)),
        _const_spec((LANES, 2 * SSD_WIDTH)),
        _const_spec((SMALL_ROWS, D_MODEL)),
        _const_spec((D_MODEL, D_MODEL)),
        _const_spec((D_MODEL, D_FF)), _const_spec((D_FF, D_MODEL)),
    ]
    return pl.pallas_call(
        _out_kernel,
        grid=(N_TILES + 2,),
        in_specs=in_specs,
        out_specs=pl.BlockSpec((1, TM, D_MODEL), done),
        out_shape=jax.ShapeDtypeStruct((BATCH, SEQ, D_MODEL), F32),
        scratch_shapes=[pltpu.VMEM((SSD_GROUPS, SSD_STATE, GROUP_COLS), F32),
                        pltpu.VMEM((TM, D_MODEL), BF16),
                        pltpu.VMEM((2, TM, D_MODEL), F32),
                        pltpu.VMEM((TM, D_MODEL), BF16),
                        pltpu.VMEM((2, G2_ROWS, D_MODEL), F32),
                        pltpu.VMEM((TM, D_MODEL), F32),
                        pltpu.VMEM((TM, D_FF), BF16)],
        compiler_params=pltpu.CompilerParams(dimension_semantics=("arbitrary",),
                                             vmem_limit_bytes=VMEM_LIMIT),
        name="out",
    )(hln, mod, ycv, zg, xs, bt, cm, rows, cols, colf, sb, h0, exw, small,
      wout, wff1, wff2)


def kernel(x, c, ctx, c_ctx, ln_in_g, ln_in_b, w_mod, b_mod, w_in, conv_w, ssd_conv_w, ssd_conv_b,
           dt_bias, a_log, ssd_d, ssd_norm_w, w_out, ln1_g, ln1_b, w_ff1, w_ff2, ln2_g, ln2_b):
    row = lambda v: v.reshape(1, -1).astype(F32)
    cvec = jnp.concatenate([c, c_ctx[None, :], jnp.zeros((MOD_ROWS - BATCH - 1, D_MODEL), F32)], axis=0)
    mod, w_in_p = _prep_call(cvec, w_mod, b_mod, w_in[0].T)

    pad_to = lambda v: jnp.pad(v, ((0, 0), (0, D_MODEL - v.shape[1])))
    small = jnp.concatenate([
        ssd_conv_w[0], row(ssd_conv_b[0]), pad_to(conv_w[0]),
        pad_to(jnp.concatenate([dt_bias[0].reshape(1, N_DH), a_log[0].reshape(1, N_DH)], axis=1)),
        jnp.concatenate([jnp.repeat(ssd_d[0], SSD_HEADDIM).reshape(1, SSD_WIDTH), row(ssd_norm_w[0])], axis=1),
        row(ln_in_g), row(ln_in_b), row(ln1_g[0]), row(ln1_b[0]), row(ln2_g[0]), row(ln2_b[0]),
        jnp.zeros((SMALL_ROWS - S_LN - 6, D_MODEL), F32)], axis=0)
    exe = jnp.asarray(_EXE, BF16)
    exw = jnp.asarray(_EXW, BF16)

    h0 = _ctx_call(ctx, mod, w_in_p, small, exw, jnp.asarray(_U_CTX, BF16))
    hln, ycv, zg, xs, bt, cm, rows, cols, colf, sb, w_out_p, w_ff1_p, w_ff2_p = _proj_call(
        x, mod, w_in_p, small, exe, exw,
        jnp.asarray(_U_CHUNK, BF16), h0, w_out, w_ff1, w_ff2)
    return _out_call(hln, mod, ycv, zg, xs, bt, cm, rows, cols, colf, sb, h0, exw,
                     small, w_out_p, w_ff1_p, w_ff2_p)
```

```python
import functools

import jax
import jax.numpy as jnp
import numpy as np
from jax import lax
from jax.experimental import pallas as pl
from jax.experimental.pallas import tpu as pltpu

F32 = jnp.float32
BF16 = jnp.bfloat16

D_MODEL = 1024
BATCH = 8
SEQ = 2048
CTX_LEN = 256
GRID_W = 64
CONV_WIDTH = 512
SSD_WIDTH = 512
SSD_HEADDIM = 64
SSD_HEADS = 8
SSD_GROUPS = 2
SSD_STATE = 128
SSD_CHUNK = 128
N_DIRS = 2
D_FF = 4 * D_MODEL
LN_EPS = 1e-5
RMS_EPS = 1e-5
SSD_GN = SSD_GROUPS * SSD_STATE
XBC_DIM = SSD_WIDTH + 2 * SSD_GN
Z_OFF = 3 * CONV_WIDTH
XBC_OFF = Z_OFF + SSD_WIDTH
DT_OFF = XBC_OFF + XBC_DIM
N_DH = N_DIRS * SSD_HEADS
LANES = 128
P_DT = 0
P_XBC = LANES
P_Z = P_XBC + XBC_DIM
P_CONV = P_Z + SSD_WIDTH
IN_PAD = P_CONV + 3 * CONV_WIDTH
CTX_PAD = P_XBC + SSD_WIDTH + SSD_GN
GROUP_COLS = (SSD_HEADS // SSD_GROUPS) * SSD_HEADDIM
ALPHA = 2.0 ** 0.25

TM = 512
CTX_ROWS = 2
MOD_ROWS = 16
SMALL_ROWS = 24
G2_ROWS = 8
PREP_STEPS = 8
NCH = TM // SSD_CHUNK
NT = SEQ // TM
NCHUNK = SEQ // SSD_CHUNK
N_TILES = BATCH * NT
TM_PROJ = 1024
NCH_PROJ = TM_PROJ // SSD_CHUNK
NT_PROJ = SEQ // TM_PROJ
FF_BLK = 1024
FF_SUB = 256
N_SLAB = D_FF // FF_BLK
FILL_SLOTS = 6
DOWN_KEPT = 2
DOWN_LAG = 1
VMEM_LIMIT = 58 * 1024 * 1024

COL_E1 = 48
COL_W = 80
COL_CSF = 112


def _expansion(col0, pieces, width):
    m = np.zeros((LANES, N_DH * width), np.float32)
    for t in range(pieces):
        for j in range(N_DH):
            m[col0 + 16 * t + j, j * width:(j + 1) * width] = 1.0
    return m


_EXE = _expansion(COL_E1, 2, SSD_HEADDIM)
_EXW = _expansion(COL_W, 2, SSD_HEADDIM)


def _ln_hat(x):
    mu = jnp.mean(x, axis=-1, keepdims=True)
    xc = x - mu
    var = jnp.mean(xc * xc, axis=-1, keepdims=True)
    return xc * lax.rsqrt(var + LN_EPS)


def _silu(x):
    return x / (1.0 + jnp.exp(-x))


def _softplus(x):
    return jnp.maximum(x, 0.0) + jnp.log1p(jnp.exp(-jnp.abs(x)))


def _edge_masks(rows, period):
    pos = lax.broadcasted_iota(jnp.int32, (rows, LANES), 0) % period
    return (pos != 0).astype(F32), (pos != period - 1).astype(F32)


def _conv3(t, w, mprev, mnext):
    rows = t.shape[0]
    prev = pltpu.roll(t, 1, 0) * mprev
    nxt = pltpu.roll(t, rows - 1, 0) * mnext
    return prev * w[0:1, :] + t * w[1:2, :] + nxt * w[2:3, :]


def _split(v, pieces):
    out = []
    for _ in range(pieces - 1):
        p = v.astype(BF16).astype(F32)
        out.append(p)
        v = v - p
    out.append(v.astype(BF16).astype(F32))
    return out


def _tri(length, op):
    i = np.arange(length)
    return op(i[:, None], i[None, :]).astype(np.float32)


_U_CHUNK = np.concatenate([_tri(SSD_CHUNK, np.less_equal), _tri(SSD_CHUNK, np.greater_equal),
                           np.ones((SSD_CHUNK, SSD_CHUNK), np.float32)], axis=1)
_U_CTX = np.concatenate([_tri(CTX_LEN, np.greater), _tri(CTX_LEN, np.less)], axis=1)


def _scan_mm(v, u_ref):
    pieces = jnp.concatenate(_split(v, 3), axis=0).astype(BF16)
    o = jnp.dot(pieces, u_ref[...], preferred_element_type=F32)
    return o[0:N_DH] + o[N_DH:2 * N_DH] + o[2 * N_DH:3 * N_DH]


def _dt_rows(raw, dtb, a_log):
    r = raw.T[0:N_DH, :] + dtb
    dt = _softplus(r)
    return dt, dt * (-jnp.exp(a_log))


def _fwd_rows():
    return lax.broadcasted_iota(jnp.int32, (N_DH, 1), 0) < SSD_HEADS


S_SCW = 0
S_SCB = 3
S_CW = 4
S_DXNW = 8
S_LN = 9
S_DTB = 16
S_ALOG = 18


def _dt_params(small_ref):
    sub = lax.broadcasted_iota(jnp.int32, (N_DH, LANES), 0)
    lane = lax.broadcasted_iota(jnp.int32, (N_DH, LANES), 1)
    head = jnp.where(sub < SSD_HEADS, sub, sub - SSD_HEADS)

    def pick(r):
        t = jnp.where(sub < SSD_HEADS, small_ref[r:r + 1, 0:LANES], small_ref[r + 1:r + 2, 0:LANES])
        return jnp.sum(jnp.where(lane == head, t, 0.0), axis=1, keepdims=True)
    return pick(S_DTB), pick(S_ALOG)


def _ln_params(small_ref, which):
    r = S_LN + 2 * which
    return small_ref[r:r + 1, :], small_ref[r + 1:r + 2, :]


def _mod_vectors(mod_ref, row):
    r = mod_ref[pl.ds(row, 1), :]
    return [r[:, k * D_MODEL:(k + 1) * D_MODEL] for k in range(6)]


def _const_spec(shape):
    nd = len(shape)
    return pl.BlockSpec(shape, lambda *_: (0,) * nd, pipeline_mode=pl.Buffered(1))


def _prep_kernel(c_ref, cctx_ref, wmod_ref, bmod_ref, wint_ref,
                 scw_ref, scb_ref, cw_ref, dtb_ref, alog_ref, d_ref, nw_ref,
                 lng_ref, lnb_ref, l1g_ref, l1b_ref, l2g_ref, l2b_ref,
                 mod_ref, pin_ref, small_ref):
    @pl.when(pl.program_id(0) == 0)
    def _():
        small_ref[...] = jnp.zeros((SMALL_ROWS, D_MODEL), F32)
        for k in range(3):
            small_ref[S_SCW + k:S_SCW + k + 1, :XBC_DIM] = scw_ref[:, k * XBC_DIM:(k + 1) * XBC_DIM]
            small_ref[S_CW + k:S_CW + k + 1, :CONV_WIDTH] = cw_ref[:, k * CONV_WIDTH:(k + 1) * CONV_WIDTH]
        small_ref[S_SCB:S_SCB + 1, :XBC_DIM] = scb_ref[...]
        small_ref[S_DTB:S_DTB + N_DIRS, :SSD_HEADS] = dtb_ref[0]
        small_ref[S_ALOG:S_ALOG + N_DIRS, :SSD_HEADS] = alog_ref[0]
        d = d_ref[...]
        first = lax.broadcasted_iota(jnp.int32, (1, LANES), 1) < SSD_HEADDIM
        for k in range(SSD_WIDTH // LANES):
            pair = [jnp.broadcast_to(d[:, 2 * k + i:2 * k + i + 1], (1, LANES)) for i in range(2)]
            small_ref[S_DXNW:S_DXNW + 1, k * LANES:(k + 1) * LANES] = jnp.where(first, pair[0], pair[1])
        small_ref[S_DXNW:S_DXNW + 1, SSD_WIDTH:] = nw_ref[...]
        for k, ref in enumerate((lng_ref, lnb_ref, l1g_ref, l1b_ref, l2g_ref, l2b_ref)):
            small_ref[S_LN + k:S_LN + k + 1, :] = ref[...]

    ctx_row = jnp.where(lax.broadcasted_iota(jnp.int32, (MOD_ROWS - BATCH, D_MODEL), 0) == 0, cctx_ref[...], 0.0)
    cs = _silu(jnp.concatenate([c_ref[...], ctx_row], axis=0)).astype(BF16)
    mod_ref[...] = jnp.dot(cs, wmod_ref[0].astype(BF16), preferred_element_type=F32) + bmod_ref[...]

    dt_rows = jnp.concatenate([wint_ref[DT_OFF:DT_OFF + N_DH, :],
                               jnp.zeros((LANES - N_DH, wint_ref.shape[1]), F32)], axis=0)
    pin_ref[:, P_DT:P_DT + LANES] = dt_rows.T.astype(BF16)
    def move(dst, src):
        pin_ref[:, dst:dst + LANES] = wint_ref[src:src + LANES, :].T.astype(BF16)

    for dst, src, width in ((P_XBC, XBC_OFF, XBC_DIM), (P_Z, Z_OFF, SSD_WIDTH)):
        for j in range(0, width, LANES):
            move(dst + j, src + j)
    for j in range(CONV_WIDTH // LANES):
        for k in range(3):
            move(P_CONV + (3 * j + k) * LANES, k * CONV_WIDTH + j * LANES)


def _prep_call(c, c_ctx, w_mod, b_mod, w_in_t, small_params):
    steps = PREP_STEPS
    r1, rm = D_MODEL // steps, 6 * D_MODEL // steps
    whole = lambda a: pl.BlockSpec(a.shape, lambda i: (0,) * a.ndim)
    return pl.pallas_call(
        _prep_kernel,
        grid=(steps,),
        in_specs=[whole(c), whole(c_ctx),
                  pl.BlockSpec((1, D_MODEL, rm), lambda i: (0, 0, i)),
                  pl.BlockSpec((1, rm), lambda i: (0, i)),
                  pl.BlockSpec((w_in_t.shape[0], r1), lambda i: (0, i))] + [whole(a) for a in small_params],
        out_specs=[pl.BlockSpec((MOD_ROWS, rm), lambda i: (0, i)),
                   pl.BlockSpec((r1, IN_PAD), lambda i: (i, 0)),
                   pl.BlockSpec((SMALL_ROWS, D_MODEL), lambda i: (0, 0))],
        out_shape=[jax.ShapeDtypeStruct((MOD_ROWS, 6 * D_MODEL), F32),
                   jax.ShapeDtypeStruct((D_MODEL, IN_PAD), BF16),
                   jax.ShapeDtypeStruct((SMALL_ROWS, D_MODEL), F32)],
        compiler_params=pltpu.CompilerParams(dimension_semantics=("arbitrary",),
                                             vmem_limit_bytes=VMEM_LIMIT),
        name="prep",
    )(c, c_ctx, w_mod, b_mod, w_in_t, *small_params)


def _ctx_kernel(x_ref, mod_ref, w_ref, small_ref, exw_ref, u_ref, h0_ref):
    dtb, alog = _dt_params(small_ref)
    lng, lnb = _ln_params(small_ref, 0)
    m = _mod_vectors(mod_ref, BATCH)
    sc = 1.0 + m[1]
    x = x_ref[...].reshape(CTX_ROWS * CTX_LEN, D_MODEL)
    u = _ln_hat(x) * (lng * sc) + (lnb * sc + m[0])
    proj = jnp.dot(u.astype(BF16), w_ref[...], preferred_element_type=F32)
    mprev, mnext = _edge_masks(CTX_ROWS * CTX_LEN, CTX_LEN)
    slabs = []
    for j in range((SSD_WIDTH + SSD_GN) // LANES):
        sl = slice(j * LANES, (j + 1) * LANES)
        pj = proj[:, P_XBC + j * LANES:P_XBC + (j + 1) * LANES]
        slabs.append(_silu(_conv3(pj, small_ref[S_SCW:S_SCW + 3, sl], mprev, mnext)
                           + small_ref[S_SCB:S_SCB + 1, sl]))
    zero = jnp.zeros((N_DH, CTX_LEN), F32)
    for i in range(CTX_ROWS):
        tok = slice(i * CTX_LEN, (i + 1) * CTX_LEN)
        xs = jnp.concatenate([sl_[tok] for sl_ in slabs[:4]], axis=1)
        dt, adt = _dt_rows(proj[tok, P_DT:P_DT + LANES], dtb, alog)
        sc2 = _scan_mm(adt, u_ref)
        excl = jnp.where(_fwd_rows(), sc2[:, :CTX_LEN], sc2[:, CTX_LEN:])
        w = jnp.exp(excl) * dt
        table = jnp.concatenate([zero] * (COL_W // 16) + _split(w, 2) + [zero], axis=0)
        cols = table.T.astype(BF16)
        wx = jnp.dot(cols, exw_ref[...], preferred_element_type=F32)
        for d in range(N_DIRS):
            xw = (xs * wx[:, d * SSD_WIDTH:(d + 1) * SSD_WIDTH]).astype(BF16)
            for g in range(SSD_GROUPS):
                bt = slabs[4 + g][tok].T.astype(BF16)
                h0_ref[i, d, g] = jnp.dot(bt, xw[:, g * GROUP_COLS:(g + 1) * GROUP_COLS],
                                          preferred_element_type=F32)


def _ctx_call(ctx, mod, w_in, small, exw, u_ctx):
    return pl.pallas_call(
        _ctx_kernel,
        grid=(BATCH // CTX_ROWS,),
        in_specs=[pl.BlockSpec((CTX_ROWS, CTX_LEN, D_MODEL), lambda b: (b, 0, 0)),
                  _const_spec((MOD_ROWS, 6 * D_MODEL)),
                  _const_spec((D_MODEL, CTX_PAD)), _const_spec((SMALL_ROWS, D_MODEL)),
                  _const_spec((LANES, 2 * SSD_WIDTH)),
                  _const_spec((CTX_LEN, 2 * CTX_LEN))],
        out_specs=pl.BlockSpec((CTX_ROWS, N_DIRS, SSD_GROUPS, SSD_STATE, GROUP_COLS),
                               lambda b: (b, 0, 0, 0, 0)),
        out_shape=jax.ShapeDtypeStruct((BATCH, N_DIRS, SSD_GROUPS, SSD_STATE, GROUP_COLS), F32),
        compiler_params=pltpu.CompilerParams(dimension_semantics=("arbitrary",),
                                             vmem_limit_bytes=VMEM_LIMIT),
        name="ctx",
    )(ctx, mod, w_in, small, exw, u_ctx)


def _proj_kernel(x_ref, mod_ref, w_ref, small_ref,
                 exe_ref, exw_ref, u_ref, h0_ref, wout_ref, wff1_ref, wff2_ref,
                 hln_ref, ycv_ref, zg_ref, xs_ref, bt_ref, cm_ref, rows_ref, cols_ref, colf_ref, sb_ref,
                 pout_ref, pff1_ref, pff2_ref, st_ref):
    pout_ref[...] = wout_ref[0].astype(BF16)
    pff1_ref[...] = wff1_ref[0].astype(BF16)
    pff2_ref[...] = wff2_ref[0].astype(BF16)

    @pl.when(pl.program_id(1) == 0)
    def _():
        st_ref[...] = h0_ref[0, 0]

    m = _mod_vectors(mod_ref, pl.program_id(0))
    sc = 1.0 + m[1]
    lng, lnb = _ln_params(small_ref, 0)
    scale, shift = lng * sc, lnb * sc + m[0]
    ubs, pas = [], []
    for r in range(2):
        rows = slice(r * (TM_PROJ // 2), (r + 1) * (TM_PROJ // 2))
        xhat = _ln_hat(x_ref[0, rows, :])
        hln_ref[0, rows, :] = xhat * lng + lnb
        ubs.append((xhat * scale + shift).astype(BF16))
        pas.append(jnp.dot(ubs[r], w_ref[:, :P_Z], preferred_element_type=F32))
    ub = jnp.concatenate(ubs, axis=0)
    pa = jnp.concatenate(pas, axis=0)
    mprev, mnext = _edge_masks(TM_PROJ, GRID_W)
    half = TM_PROJ // 2

    def xbc_half(r):
        out = []
        for j in range(XBC_DIM // LANES):
            sl = slice(j * LANES, (j + 1) * LANES)
            pj = pas[r][:, P_XBC + j * LANES:P_XBC + (j + 1) * LANES]
            out.append(_silu(_conv3(pj, small_ref[S_SCW:S_SCW + 3, sl], mprev[:half], mnext[:half])
                             + small_ref[S_SCB:S_SCB + 1, sl]))
        return out
    is_fwd = _fwd_rows()
    zero = jnp.zeros((N_DH, SSD_CHUNK), F32)
    chunks = [slice(c * SSD_CHUNK, (c + 1) * SSD_CHUNK) for c in range(NCH_PROJ)]

    def conv_slabs(p, first):
        for i in range(2):
            gb, gc, gh = (p[:, (3 * i + k) * LANES:(3 * i + k + 1) * LANES] for k in range(3))
            sl = slice((first + i) * LANES, (first + i + 1) * LANES)
            ycv_ref[0, :, sl] = (gb * _conv3(gc * gh, small_ref[S_CW:S_CW + 3, sl], mprev, mnext)).astype(BF16)

    dtb, alog = _dt_params(small_ref)
    slabs_lo = xbc_half(0)
    dts = [_dt_rows(pa[tok, P_DT:P_DT + LANES], dtb, alog) for tok in chunks]
    pz = jnp.dot(ub, w_ref[:, P_Z:P_CONV], preferred_element_type=F32)
    slabs_hi = xbc_half(1)
    slabs = [jnp.concatenate([lo, hi], axis=0) for lo, hi in zip(slabs_lo, slabs_hi)]

    colss = []
    for c, (dt, adt) in enumerate(dts):
        sc3 = _scan_mm(adt, u_ref)
        cs = jnp.where(is_fwd, sc3[:, :SSD_CHUNK], sc3[:, SSD_CHUNK:2 * SSD_CHUNK])
        tot = sc3[:, 2 * SSD_CHUNK:]
        e1 = jnp.exp(cs)
        w = jnp.exp(tot - cs) * dt
        dsum = dt[:SSD_HEADS] + dt[SSD_HEADS:]
        rows_ref[0, c] = jnp.concatenate([cs - jnp.log(dt), dsum, jnp.zeros_like(dsum)], axis=0)
        table = jnp.concatenate([zero] * 3 + _split(e1, 2) + _split(w, 2) + [cs], axis=0).T
        colf_ref[0, c] = table
        colss.append(table.astype(BF16))
        cols_ref[0, c] = colss[c]

    pc1 = jnp.dot(ub, w_ref[:, P_CONV:P_CONV + 6 * LANES], preferred_element_type=F32)

    xs = jnp.concatenate(slabs[:4], axis=1)
    xs_ref[0] = xs.astype(BF16)
    cm_ref[0] = jnp.concatenate(slabs[6:8], axis=1).astype(BF16)
    zg_ref[0] = _silu(pz).astype(BF16)
    bts = []
    for c, tok in enumerate(chunks):
        bt = [slabs[4 + g][tok].T.astype(BF16) for g in range(SSD_GROUPS)]
        for g in range(SSD_GROUPS):
            bt_ref[0, c, g] = bt[g]
        bts.append(bt)
    wxbs = [jnp.dot(cols, exw_ref[:, SSD_WIDTH:], preferred_element_type=F32) for cols in colss]
    decs = [jnp.dot(cols[0:16], exe_ref[:, SSD_WIDTH:], preferred_element_type=F32)[0:1]
            for cols in colss]

    pc2 = jnp.dot(ub, w_ref[:, P_CONV + 6 * LANES:], preferred_element_type=F32)
    conv_slabs(pc1, 0)

    state = [st_ref[g] for g in range(SSD_GROUPS)]
    for c in reversed(range(NCH_PROJ)):
        xw = (xs[chunks[c]] * wxbs[c]).astype(BF16)
        for g in range(SSD_GROUPS):
            gs = slice(g * GROUP_COLS, (g + 1) * GROUP_COLS)
            sb_ref[0, c, g] = state[g].astype(BF16)
            local = jnp.dot(bts[c][g], xw[:, gs], preferred_element_type=F32)
            state[g] = state[g] * decs[c][:, gs] + local
    for g in range(SSD_GROUPS):
        st_ref[g] = state[g]
    conv_slabs(pc2, 2)


def _proj_call(x, mod, w_in, small, exe, exw, u_chunk, h0, w_out, w_ff1, w_ff2):
    n_steps = BATCH * NT_PROJ
    r1, r4 = D_MODEL // n_steps, D_FF // n_steps
    wrow3 = lambda b, t: (0, b * NT_PROJ + t, 0)
    wrow2 = lambda b, t: (b * NT_PROJ + t, 0)
    rev = lambda b, t: (b, NT_PROJ - 1 - t, 0)
    rev4 = lambda b, t: (b, NT_PROJ - 1 - t, 0, 0)
    rev5 = lambda b, t: (b, NT_PROJ - 1 - t, 0, 0, 0)
    tok = lambda width: pl.BlockSpec((1, TM_PROJ, width), rev)
    out_shape = [
        jax.ShapeDtypeStruct((BATCH, SEQ, D_MODEL), F32),
        jax.ShapeDtypeStruct((BATCH, SEQ, CONV_WIDTH), BF16),
        jax.ShapeDtypeStruct((BATCH, SEQ, SSD_WIDTH), BF16),
        jax.ShapeDtypeStruct((BATCH, SEQ, SSD_WIDTH), BF16),
        jax.ShapeDtypeStruct((BATCH, NCHUNK, SSD_GROUPS, SSD_STATE, SSD_CHUNK), BF16),
        jax.ShapeDtypeStruct((BATCH, SEQ, SSD_GN), BF16),
        jax.ShapeDtypeStruct((BATCH, NCHUNK, 2 * N_DH, SSD_CHUNK), F32),
        jax.ShapeDtypeStruct((BATCH, NCHUNK, SSD_CHUNK, LANES), BF16),
        jax.ShapeDtypeStruct((BATCH, NCHUNK, SSD_CHUNK, LANES), F32),
        jax.ShapeDtypeStruct((BATCH, NCHUNK, SSD_GROUPS, SSD_STATE, GROUP_COLS), BF16),
        jax.ShapeDtypeStruct((D_MODEL, D_MODEL), BF16),
        jax.ShapeDtypeStruct((D_MODEL, D_FF), BF16),
        jax.ShapeDtypeStruct((D_FF, D_MODEL), BF16),
    ]
    out_specs = [
        tok(D_MODEL), tok(CONV_WIDTH), tok(SSD_WIDTH), tok(SSD_WIDTH),
        pl.BlockSpec((1, NCH_PROJ, SSD_GROUPS, SSD_STATE, SSD_CHUNK), rev5),
        tok(SSD_GN),
        pl.BlockSpec((1, NCH_PROJ, 2 * N_DH, SSD_CHUNK), rev4),
        pl.BlockSpec((1, NCH_PROJ, SSD_CHUNK, LANES), rev4),
        pl.BlockSpec((1, NCH_PROJ, SSD_CHUNK, LANES), rev4),
        pl.BlockSpec((1, NCH_PROJ, SSD_GROUPS, SSD_STATE, GROUP_COLS), rev5),
        pl.BlockSpec((r1, D_MODEL), wrow2), pl.BlockSpec((r1, D_FF), wrow2), pl.BlockSpec((r4, D_MODEL), wrow2),
    ]
    in_specs = [
        pl.BlockSpec((1, TM_PROJ, D_MODEL), rev),
        _const_spec((MOD_ROWS, 6 * D_MODEL)),
        _const_spec((D_MODEL, IN_PAD)),
        _const_spec((SMALL_ROWS, D_MODEL)),
        _const_spec((LANES, 2 * SSD_WIDTH)), _const_spec((LANES, 2 * SSD_WIDTH)),
        _const_spec((SSD_CHUNK, 3 * SSD_CHUNK)),
        pl.BlockSpec((1, 1, SSD_GROUPS, SSD_STATE, GROUP_COLS), lambda b, t: (b, 1, 0, 0, 0)),
        pl.BlockSpec((1, r1, D_MODEL), wrow3), pl.BlockSpec((1, r1, D_FF), wrow3),
        pl.BlockSpec((1, r4, D_MODEL), wrow3),
    ]
    return pl.pallas_call(
        _proj_kernel,
        grid=(BATCH, NT_PROJ),
        in_specs=in_specs,
        out_specs=out_specs,
        out_shape=out_shape,
        scratch_shapes=[pltpu.VMEM((SSD_GROUPS, SSD_STATE, GROUP_COLS), F32)],
        compiler_params=pltpu.CompilerParams(dimension_semantics=("arbitrary", "arbitrary"),
                                             vmem_limit_bytes=VMEM_LIMIT),
        name="proj",
    )(x, mod, w_in, small, exe, exw, u_chunk, h0, w_out, w_ff1, w_ff2)


def _out_kernel(hln_ref, mod_ref, ycv_ref, zg_ref, xs_ref, bt_ref, cm_ref, rows_ref,
                cols_ref, colf_ref, sb_ref, h0_ref, exw_ref, small_ref, wout_ref,
                wff1_ref, wff2_ref,
                out_ref, st_ref, yn_ref, h1_ref, u2_ref, g2_ref, r2_ref, hid_ref):
    s = pl.program_id(0)
    wr = s % 2
    rd = (s + 1) % 2
    li = lax.broadcasted_iota(jnp.int32, (SSD_CHUNK, SSD_CHUNK), 0)
    si = lax.broadcasted_iota(jnp.int32, (SSD_CHUNK, SSD_CHUNK), 1)
    low = li >= si
    diag = li == si
    lo_half = si < SSD_HEADDIM

    def ssd_chunk(c, state, fill):
        tok = slice(c * SSD_CHUNK, (c + 1) * SSD_CHUNK)
        rows = rows_ref[0, c]
        cols = cols_ref[0, c]
        xs = xs_ref[0, tok, :]
        cm = cm_ref[0, tok, :]
        colf = colf_ref[0, c]
        bc = [jnp.broadcast_to(colf[:, COL_CSF + j:COL_CSF + j + 1], (SSD_CHUNK, LANES))
              for j in range(N_DH)]
        e1x = [jnp.concatenate(
            [jnp.exp(jnp.where(lo_half, bc[d * SSD_HEADS + 2 * k], bc[d * SSD_HEADS + 2 * k + 1]))
             for k in range(SSD_HEADS // 2)], axis=1) for d in range(N_DIRS)]
        wxf = jnp.dot(cols, exw_ref[:, :SSD_WIDTH], preferred_element_type=F32)
        dec = e1x[0][SSD_CHUNK - 1:SSD_CHUNK, :]
        gmat = [jnp.dot(cm[:, g * SSD_STATE:(g + 1) * SSD_STATE], bt_ref[0, c, g],
                        preferred_element_type=F32) for g in range(SSD_GROUPS)]
        fill[0]()

        ys = []
        for k in range(SSD_HEADS // 2):
            ms = []
            for h in (2 * k, 2 * k + 1):
                g = h // (SSD_HEADS // SSD_GROUPS)
                hb = SSD_HEADS + h
                arg = jnp.where(low,
                                bc[h] - rows[h:h + 1, :],
                                bc[hb] - rows[hb:hb + 1, :])
                decay_dt = jnp.where(diag, rows[N_DH + h:N_DH + h + 1, :], jnp.exp(arg))
                ms.append((gmat[g] * decay_dt).astype(BF16))
            xp = xs[:, k * LANES:(k + 1) * LANES]
            rhs = jnp.concatenate([jnp.where(lo_half, xp, jnp.zeros_like(xp)),
                                   jnp.where(lo_half, jnp.zeros_like(xp), xp)], axis=0)
            ys.append(jnp.dot(jnp.concatenate(ms, axis=1), rhs, preferred_element_type=F32))
            fill[k + 1]()
        y = jnp.concatenate(ys, axis=1)

        yf, yb = [], []
        for g in range(SSD_GROUPS):
            cg = cm[:, g * SSD_STATE:(g + 1) * SSD_STATE]
            yf.append(jnp.dot(cg, state[g].astype(BF16), preferred_element_type=F32))
            yb.append(jnp.dot(cg, sb_ref[0, c, g], preferred_element_type=F32))
        y = (y + jnp.concatenate(yf, axis=1) * e1x[0]
             + jnp.concatenate(yb, axis=1) * e1x[1]
             + xs.astype(F32) * small_ref[S_DXNW:S_DXNW + 1, :SSD_WIDTH])

        yg = y * zg_ref[0, tok, :].astype(F32)
        ms_ = jnp.mean(yg * yg, axis=-1, keepdims=True)
        yn_ref[tok, :CONV_WIDTH] = ycv_ref[0, tok, :]
        yn_ref[tok, CONV_WIDTH:] = (yg * lax.rsqrt(ms_ + RMS_EPS)
                                   * small_ref[S_DXNW:S_DXNW + 1, SSD_WIDTH:]).astype(BF16)
        fill[5]()

        xw = (xs.astype(F32) * wxf).astype(BF16)
        new_state = []
        for g in range(SSD_GROUPS):
            gs = slice(g * GROUP_COLS, (g + 1) * GROUP_COLS)
            local = jnp.dot(bt_ref[0, c, g], xw[:, gs], preferred_element_type=F32)
            new_state.append(state[g] * dec[:, gs] + local)
        return new_state

    def mlp_up(j, k):
        blk = slice(j * FF_BLK + k * FF_SUB, j * FF_BLK + (k + 1) * FF_SUB)
        hid = jnp.maximum(jnp.dot(u2_ref[...], wff1_ref[:, blk], preferred_element_type=F32), 0.0)
        hid_ref[:, blk] = (hid * hid).astype(BF16)

    def mlp_down(j, acc):
        blk = slice(j * FF_BLK, (j + 1) * FF_BLK)
        part = jnp.dot(hid_ref[:, blk], wff2_ref[blk, :], preferred_element_type=F32)
        return part if acc is None else acc + part

    def ln2_rows(i):
        r = slice(i * SSD_CHUNK, (i + 1) * SSD_CHUNK)
        gain, bias = _ln_params(small_ref, 2)
        out_ref[0, r, :] = _ln_hat(r2_ref[r, :]) * gain + bias

    def step(mixer, mlp, norm):
        state, acc = None, [None]
        pieces = []
        if mlp:
            def down(j):
                acc[0] = mlp_down(j, acc[0])
            for j in range(N_SLAB):
                pieces += [(1, functools.partial(mlp_up, j, k)) for k in range(FF_BLK // FF_SUB)]
            for j in range(N_SLAB - DOWN_KEPT):
                at = (j + 1) * (FF_BLK // FF_SUB) + j + DOWN_LAG
                pieces.insert(at, (FF_BLK // FF_SUB, functools.partial(down, j)))
        n_slots = NCH * FILL_SLOTS
        per_slot = sum(cost for cost, _ in pieces) / n_slots
        slots, issued = [], 0.0
        for i in range(n_slots):
            mine = []
            while pieces and issued < (i + 1) * per_slot:
                cost, fn = pieces.pop(0)
                issued += cost
                mine.append(fn)
            slots.append(lambda mine=mine: [fn() for fn in mine])
        if mixer:
            @pl.when(jnp.minimum(s, N_TILES - 1) % NT == 0)
            def _():
                st_ref[...] = h0_ref[0, 0]
            state = [st_ref[g] for g in range(SSD_GROUPS)]
        for c in range(NCH):
            fill = slots[c * FILL_SLOTS:(c + 1) * FILL_SLOTS]
            if mixer:
                state = ssd_chunk(c, state, fill)
            else:
                for f in fill:
                    f()
            if norm:
                ln2_rows(c)
        if mixer:
            for g in range(SSD_GROUPS):
                st_ref[g] = state[g]
            mix = jnp.dot(yn_ref[...], wout_ref[...], preferred_element_type=F32)
            m = _mod_vectors(mod_ref, jnp.minimum(s, N_TILES - 1) // NT)
            g2_ref[wr, 0:1, :] = m[5]
        for i in range(DOWN_KEPT):
            if mlp:
                down(N_SLAB - DOWN_KEPT + i)
            if mixer:
                for j in range(i * NCH // DOWN_KEPT, (i + 1) * NCH // DOWN_KEPT):
                    r = slice(j * SSD_CHUNK, (j + 1) * SSD_CHUNK)
                    gain, bias = _ln_params(small_ref, 1)
                    h1 = _ln_hat(ALPHA * hln_ref[0, r, :] + m[2] * mix[r]) * gain + bias
                    h1_ref[wr, r, :] = h1
                    u2_ref[r, :] = (h1 * (1.0 + m[4]) + m[3]).astype(BF16)
        if mlp:
            r2_ref[...] = ALPHA * h1_ref[rd] + g2_ref[rd, 0:1, :] * acc[0]

    @pl.when(s == 0)
    def _():
        r2_ref[...] = jnp.zeros((TM, D_MODEL), F32)
        step(True, False, False)

    @pl.when(jnp.logical_and(s >= 1, s < N_TILES))
    def _():
        step(True, True, True)

    @pl.when(s == N_TILES)
    def _():
        step(False, True, True)

    @pl.when(s == N_TILES + 1)
    def _():
        step(False, False, True)


def _out_call(hln, mod, ycv, zg, xs, bt, cm, rows, cols, colf, sb, h0, exw, small,
              wout, wff1, wff2):
    cur = lambda s: jnp.minimum(s, N_TILES - 1)
    fwd = lambda s: (cur(s) // NT, cur(s) % NT, 0)
    fwd4 = lambda s: (cur(s) // NT, cur(s) % NT, 0, 0)
    fwd5 = lambda s: (cur(s) // NT, cur(s) % NT, 0, 0, 0)
    done = lambda s: (jnp.maximum(s - 2, 0) // NT, jnp.maximum(s - 2, 0) % NT, 0)
    tok = lambda width: pl.BlockSpec((1, TM, width), fwd)
    in_specs = [
        tok(D_MODEL),
        _const_spec((MOD_ROWS, 6 * D_MODEL)),
        tok(CONV_WIDTH), tok(SSD_WIDTH), tok(SSD_WIDTH),
        pl.BlockSpec((1, NCH, SSD_GROUPS, SSD_STATE, SSD_CHUNK), fwd5),
        tok(SSD_GN),
        pl.BlockSpec((1, NCH, 2 * N_DH, SSD_CHUNK), fwd4),
        pl.BlockSpec((1, NCH, SSD_CHUNK, LANES), fwd4),
        pl.BlockSpec((1, NCH, SSD_CHUNK, LANES), fwd4),
        pl.BlockSpec((1, NCH, SSD_GROUPS, SSD_STATE, GROUP_COLS), fwd5),
        pl.BlockSpec((1, 1, SSD_GROUPS, SSD_STATE, GROUP_COLS), lambda s: (cur(s) // NT, 0, 0, 0, 0)),
        _const_spec((LANES, 2 * SSD_WIDTH)),
        _const_spec((SMALL_ROWS, D_MODEL)),
        _const_spec((D_MODEL, D_MODEL)),
        _const_spec((D_MODEL, D_FF)), _const_spec((D_FF, D_MODEL)),
    ]
    return pl.pallas_call(
        _out_kernel,
        grid=(N_TILES + 2,),
        in_specs=in_specs,
        out_specs=pl.BlockSpec((1, TM, D_MODEL), done),
        out_shape=jax.ShapeDtypeStruct((BATCH, SEQ, D_MODEL), F32),
        scratch_shapes=[pltpu.VMEM((SSD_GROUPS, SSD_STATE, GROUP_COLS), F32),
                        pltpu.VMEM((TM, D_MODEL), BF16),
                        pltpu.VMEM((2, TM, D_MODEL), F32),
                        pltpu.VMEM((TM, D_MODEL), BF16),
                        pltpu.VMEM((2, G2_ROWS, D_MODEL), F32),
                        pltpu.VMEM((TM, D_MODEL), F32),
                        pltpu.VMEM((TM, D_FF), BF16)],
        compiler_params=pltpu.CompilerParams(dimension_semantics=("arbitrary",),
                                             vmem_limit_bytes=VMEM_LIMIT),
        name="out",
    )(hln, mod, ycv, zg, xs, bt, cm, rows, cols, colf, sb, h0, exw, small,
      wout, wff1, wff2)


def kernel(x, c, ctx, c_ctx, ln_in_g, ln_in_b, w_mod, b_mod, w_in, conv_w, ssd_conv_w, ssd_conv_b,
           dt_bias, a_log, ssd_d, ssd_norm_w, w_out, ln1_g, ln1_b, w_ff1, w_ff2, ln2_g, ln2_b):
    flat = lambda v: v.reshape(1, -1)
    mod, w_in_p, small = _prep_call(
        c, flat(c_ctx), w_mod, b_mod, w_in[0].T,
        (flat(ssd_conv_w), ssd_conv_b, flat(conv_w), dt_bias, a_log, ssd_d, ssd_norm_w,
         flat(ln_in_g), flat(ln_in_b), ln1_g, ln1_b, ln2_g, ln2_b))
    exe = jnp.asarray(_EXE, BF16)
    exw = jnp.asarray(_EXW, BF16)

    h0 = _ctx_call(ctx, mod, w_in_p, small, exw, jnp.asarray(_U_CTX, BF16))
    hln, ycv, zg, xs, bt, cm, rows, cols, colf, sb, w_out_p, w_ff1_p, w_ff2_p = _proj_call(
        x, mod, w_in_p, small, exe, exw,
        jnp.asarray(_U_CHUNK, BF16), h0, w_out, w_ff1, w_ff2)
    return _out_call(hln, mod, ycv, zg, xs, bt, cm, rows, cols, colf, sb, h0, exw,
                     small, w_out_p, w_ff1_p, w_ff2_p)
```

```python
import functools

import jax
import jax.numpy as jnp
import numpy as np
from jax import lax
from jax.experimental import pallas as pl
from jax.experimental.pallas import tpu as pltpu

F32 = jnp.float32
BF16 = jnp.bfloat16

D_MODEL = 1024
BATCH = 8
SEQ = 2048
CTX_LEN = 256
GRID_W = 64
CONV_WIDTH = 512
SSD_WIDTH = 512
SSD_HEADDIM = 64
SSD_HEADS = 8
SSD_GROUPS = 2
SSD_STATE = 128
SSD_CHUNK = 128
N_DIRS = 2
D_FF = 4 * D_MODEL
LN_EPS = 1e-5
RMS_EPS = 1e-5
SSD_GN = SSD_GROUPS * SSD_STATE
XBC_DIM = SSD_WIDTH + 2 * SSD_GN
Z_OFF = 3 * CONV_WIDTH
XBC_OFF = Z_OFF + SSD_WIDTH
DT_OFF = XBC_OFF + XBC_DIM
N_DH = N_DIRS * SSD_HEADS
LANES = 128
P_DT = 0
P_XBC = LANES
P_Z = P_XBC + XBC_DIM
P_CONV = P_Z + SSD_WIDTH
IN_PAD = P_CONV + 3 * CONV_WIDTH
CTX_PAD = P_XBC + SSD_WIDTH + SSD_GN
GROUP_COLS = (SSD_HEADS // SSD_GROUPS) * SSD_HEADDIM
ALPHA = 2.0 ** 0.25

TM = 512
CTX_ROWS = 2
MOD_ROWS = 16
SMALL_ROWS = 24
G2_ROWS = 8
PREP_STEPS = 8
NCH = TM // SSD_CHUNK
NT = SEQ // TM
NCHUNK = SEQ // SSD_CHUNK
N_TILES = BATCH * NT
TM_PROJ = 1024
NCH_PROJ = TM_PROJ // SSD_CHUNK
NT_PROJ = SEQ // TM_PROJ
FF_BLK = 1024
FF_SUB = 256
N_SLAB = D_FF // FF_BLK
FILL_SLOTS = 6
DOWN_KEPT = 2
DOWN_LAG = 1
VMEM_LIMIT = 58 * 1024 * 1024

COL_E1 = 48
COL_W = 80
COL_CSF = 112


def _expansion(col0, pieces, width):
    m = np.zeros((LANES, N_DH * width), np.float32)
    for t in range(pieces):
        for j in range(N_DH):
            m[col0 + 16 * t + j, j * width:(j + 1) * width] = 1.0
    return m


_EXE = _expansion(COL_E1, 2, SSD_HEADDIM)
_EXW = _expansion(COL_W, 2, SSD_HEADDIM)


def _ln_hat(x):
    mu = jnp.mean(x, axis=-1, keepdims=True)
    xc = x - mu
    var = jnp.mean(xc * xc, axis=-1, keepdims=True)
    return xc * lax.rsqrt(var + LN_EPS)


def _silu(x):
    return x / (1.0 + jnp.exp(-x))


def _softplus(x):
    return jnp.maximum(x, 0.0) + jnp.log1p(jnp.exp(-jnp.abs(x)))


def _edge_masks(rows, period):
    pos = lax.broadcasted_iota(jnp.int32, (rows, LANES), 0) % period
    return (pos != 0).astype(F32), (pos != period - 1).astype(F32)


def _conv3(t, w, mprev, mnext):
    rows = t.shape[0]
    prev = pltpu.roll(t, 1, 0) * mprev
    nxt = pltpu.roll(t, rows - 1, 0) * mnext
    return prev * w[0:1, :] + t * w[1:2, :] + nxt * w[2:3, :]


def _split(v, pieces):
    out = []
    for _ in range(pieces - 1):
        p = v.astype(BF16).astype(F32)
        out.append(p)
        v = v - p
    out.append(v.astype(BF16).astype(F32))
    return out


def _tri(length, op):
    i = np.arange(length)
    return op(i[:, None], i[None, :]).astype(np.float32)


_U_CHUNK = np.concatenate([_tri(SSD_CHUNK, np.less_equal), _tri(SSD_CHUNK, np.greater_equal),
                           np.ones((SSD_CHUNK, SSD_CHUNK), np.float32)], axis=1)
_U_CTX = np.concatenate([_tri(CTX_LEN, np.greater), _tri(CTX_LEN, np.less)], axis=1)


def _scan_mm(v, u_ref):
    pieces = jnp.concatenate(_split(v, 3), axis=0).astype(BF16)
    o = jnp.dot(pieces, u_ref[...], preferred_element_type=F32)
    return o[0:N_DH] + o[N_DH:2 * N_DH] + o[2 * N_DH:3 * N_DH]


def _dt_rows(raw, dtb, a_log):
    r = raw.T[0:N_DH, :] + dtb
    dt = _softplus(r)
    return dt, dt * (-jnp.exp(a_log))


def _fwd_rows():
    return lax.broadcasted_iota(jnp.int32, (N_DH, 1), 0) < SSD_HEADS


S_SCW = 0
S_SCB = 3
S_CW = 4
S_DXNW = 8
S_LN = 9
S_DTB = 16
S_ALOG = 18


def _dt_params(small_ref):
    sub = lax.broadcasted_iota(jnp.int32, (N_DH, LANES), 0)
    lane = lax.broadcasted_iota(jnp.int32, (N_DH, LANES), 1)
    head = jnp.where(sub < SSD_HEADS, sub, sub - SSD_HEADS)

    def pick(r):
        t = jnp.where(sub < SSD_HEADS, small_ref[r:r + 1, 0:LANES], small_ref[r + 1:r + 2, 0:LANES])
        return jnp.sum(jnp.where(lane == head, t, 0.0), axis=1, keepdims=True)
    return pick(S_DTB), pick(S_ALOG)


def _ln_params(small_ref, which):
    r = S_LN + 2 * which
    return small_ref[r:r + 1, :], small_ref[r + 1:r + 2, :]


def _mod_vectors(mod_ref, row):
    r = mod_ref[pl.ds(row, 1), :]
    return [r[:, k * D_MODEL:(k + 1) * D_MODEL] for k in range(6)]


def _const_spec(shape):
    nd = len(shape)
    return pl.BlockSpec(shape, lambda *_: (0,) * nd, pipeline_mode=pl.Buffered(1))


def _prep_kernel(c_ref, cctx_ref, wmod_ref, bmod_ref, wint_ref,
                 scw_ref, scb_ref, cw_ref, dtb_ref, alog_ref, d_ref, nw_ref,
                 lng_ref, lnb_ref, l1g_ref, l1b_ref, l2g_ref, l2b_ref,
                 mod_ref, pin_ref, small_ref):
    @pl.when(pl.program_id(0) == 0)
    def _():
        small_ref[...] = jnp.zeros((SMALL_ROWS, D_MODEL), F32)
        for k in range(3):
            small_ref[S_SCW + k:S_SCW + k + 1, :XBC_DIM] = scw_ref[:, k * XBC_DIM:(k + 1) * XBC_DIM]
            small_ref[S_CW + k:S_CW + k + 1, :CONV_WIDTH] = cw_ref[:, k * CONV_WIDTH:(k + 1) * CONV_WIDTH]
        small_ref[S_SCB:S_SCB + 1, :XBC_DIM] = scb_ref[...]
        small_ref[S_DTB:S_DTB + N_DIRS, :SSD_HEADS] = dtb_ref[0]
        small_ref[S_ALOG:S_ALOG + N_DIRS, :SSD_HEADS] = alog_ref[0]
        d = d_ref[...]
        first = lax.broadcasted_iota(jnp.int32, (1, LANES), 1) < SSD_HEADDIM
        for k in range(SSD_WIDTH // LANES):
            pair = [jnp.broadcast_to(d[:, 2 * k + i:2 * k + i + 1], (1, LANES)) for i in range(2)]
            small_ref[S_DXNW:S_DXNW + 1, k * LANES:(k + 1) * LANES] = jnp.where(first, pair[0], pair[1])
        small_ref[S_DXNW:S_DXNW + 1, SSD_WIDTH:] = nw_ref[...]
        for k, ref in enumerate((lng_ref, lnb_ref, l1g_ref, l1b_ref, l2g_ref, l2b_ref)):
            small_ref[S_LN + k:S_LN + k + 1, :] = ref[...]

    ctx_row = jnp.where(lax.broadcasted_iota(jnp.int32, (MOD_ROWS - BATCH, D_MODEL), 0) == 0, cctx_ref[...], 0.0)
    cs = _silu(jnp.concatenate([c_ref[...], ctx_row], axis=0)).astype(BF16)
    mod_ref[...] = jnp.dot(cs, wmod_ref[0].astype(BF16), preferred_element_type=F32) + bmod_ref[...]

    dt_rows = jnp.concatenate([wint_ref[DT_OFF:DT_OFF + N_DH, :],
                               jnp.zeros((LANES - N_DH, wint_ref.shape[1]), F32)], axis=0)
    pin_ref[:, P_DT:P_DT + LANES] = dt_rows.T.astype(BF16)
    def move(dst, src):
        pin_ref[:, dst:dst + LANES] = wint_ref[src:src + LANES, :].T.astype(BF16)

    for dst, src, width in ((P_XBC, XBC_OFF, XBC_DIM), (P_Z, Z_OFF, SSD_WIDTH)):
        for j in range(0, width, LANES):
            move(dst + j, src + j)
    for j in range(CONV_WIDTH // LANES):
        for k in range(3):
            move(P_CONV + (3 * j + k) * LANES, k * CONV_WIDTH + j * LANES)


def _prep_call(c, c_ctx, w_mod, b_mod, w_in_t, small_params):
    steps = PREP_STEPS
    r1, rm = D_MODEL // steps, 6 * D_MODEL // steps
    whole = lambda a: pl.BlockSpec(a.shape, lambda i: (0,) * a.ndim)
    return pl.pallas_call(
        _prep_kernel,
        grid=(steps,),
        in_specs=[whole(c), whole(c_ctx),
                  pl.BlockSpec((1, D_MODEL, rm), lambda i: (0, 0, i)),
                  pl.BlockSpec((1, rm), lambda i: (0, i)),
                  pl.BlockSpec((w_in_t.shape[0], r1), lambda i: (0, i))] + [whole(a) for a in small_params],
        out_specs=[pl.BlockSpec((MOD_ROWS, rm), lambda i: (0, i)),
                   pl.BlockSpec((r1, IN_PAD), lambda i: (i, 0)),
                   pl.BlockSpec((SMALL_ROWS, D_MODEL), lambda i: (0, 0))],
        out_shape=[jax.ShapeDtypeStruct((MOD_ROWS, 6 * D_MODEL), F32),
                   jax.ShapeDtypeStruct((D_MODEL, IN_PAD), BF16),
                   jax.ShapeDtypeStruct((SMALL_ROWS, D_MODEL), F32)],
        compiler_params=pltpu.CompilerParams(dimension_semantics=("arbitrary",),
                                             vmem_limit_bytes=VMEM_LIMIT),
        name="prep",
    )(c, c_ctx, w_mod, b_mod, w_in_t, *small_params)


def _ctx_kernel(x_ref, mod_ref, w_ref, small_ref, exw_ref, u_ref, h0_ref):
    dtb, alog = _dt_params(small_ref)
    lng, lnb = _ln_params(small_ref, 0)
    m = _mod_vectors(mod_ref, BATCH)
    sc = 1.0 + m[1]
    x = x_ref[...].reshape(CTX_ROWS * CTX_LEN, D_MODEL)
    u = _ln_hat(x) * (lng * sc) + (lnb * sc + m[0])
    proj = jnp.dot(u.astype(BF16), w_ref[...], preferred_element_type=F32)
    mprev, mnext = _edge_masks(CTX_ROWS * CTX_LEN, CTX_LEN)
    slabs = []
    for j in range((SSD_WIDTH + SSD_GN) // LANES):
        sl = slice(j * LANES, (j + 1) * LANES)
        pj = proj[:, P_XBC + j * LANES:P_XBC + (j + 1) * LANES]
        slabs.append(_silu(_conv3(pj, small_ref[S_SCW:S_SCW + 3, sl], mprev, mnext)
                           + small_ref[S_SCB:S_SCB + 1, sl]))
    zero = jnp.zeros((N_DH, CTX_LEN), F32)
    for i in range(CTX_ROWS):
        tok = slice(i * CTX_LEN, (i + 1) * CTX_LEN)
        xs = jnp.concatenate([sl_[tok] for sl_ in slabs[:4]], axis=1)
        dt, adt = _dt_rows(proj[tok, P_DT:P_DT + LANES], dtb, alog)
        sc2 = _scan_mm(adt, u_ref)
        excl = jnp.where(_fwd_rows(), sc2[:, :CTX_LEN], sc2[:, CTX_LEN:])
        w = jnp.exp(excl) * dt
        table = jnp.concatenate([zero] * (COL_W // 16) + _split(w, 2) + [zero], axis=0)
        cols = table.T.astype(BF16)
        wx = jnp.dot(cols, exw_ref[...], preferred_element_type=F32)
        for d in range(N_DIRS):
            xw = (xs * wx[:, d * SSD_WIDTH:(d + 1) * SSD_WIDTH]).astype(BF16)
            for g in range(SSD_GROUPS):
                bt = slabs[4 + g][tok].T.astype(BF16)
                h0_ref[i, d, g] = jnp.dot(bt, xw[:, g * GROUP_COLS:(g + 1) * GROUP_COLS],
                                          preferred_element_type=F32)


def _ctx_call(ctx, mod, w_in, small, exw, u_ctx):
    return pl.pallas_call(
        _ctx_kernel,
        grid=(BATCH // CTX_ROWS,),
        in_specs=[pl.BlockSpec((CTX_ROWS, CTX_LEN, D_MODEL), lambda b: (b, 0, 0)),
                  _const_spec((MOD_ROWS, 6 * D_MODEL)),
                  _const_spec((D_MODEL, CTX_PAD)), _const_spec((SMALL_ROWS, D_MODEL)),
                  _const_spec((LANES, 2 * SSD_WIDTH)),
                  _const_spec((CTX_LEN, 2 * CTX_LEN))],
        out_specs=pl.BlockSpec((CTX_ROWS, N_DIRS, SSD_GROUPS, SSD_STATE, GROUP_COLS),
                               lambda b: (b, 0, 0, 0, 0)),
        out_shape=jax.ShapeDtypeStruct((BATCH, N_DIRS, SSD_GROUPS, SSD_STATE, GROUP_COLS), F32),
        compiler_params=pltpu.CompilerParams(dimension_semantics=("arbitrary",),
                                             vmem_limit_bytes=VMEM_LIMIT),
        name="ctx",
    )(ctx, mod, w_in, small, exw, u_ctx)


def _proj_kernel(x_ref, mod_ref, w_ref, small_ref,
                 exe_ref, exw_ref, u_ref, h0_ref, wout_ref, wff1_ref, wff2_ref,
                 hln_ref, ycv_ref, zg_ref, xs_ref, bt_ref, cm_ref, rows_ref, cols_ref, colf_ref, sb_ref,
                 pout_ref, pff1_ref, pff2_ref, st_ref):
    pout_ref[...] = wout_ref[0].astype(BF16)
    pff1_ref[...] = wff1_ref[0].astype(BF16)
    pff2_ref[...] = wff2_ref[0].astype(BF16)

    @pl.when(pl.program_id(1) == 0)
    def _():
        st_ref[...] = h0_ref[0, 0]

    m = _mod_vectors(mod_ref, pl.program_id(0))
    sc = 1.0 + m[1]
    lng, lnb = _ln_params(small_ref, 0)
    scale, shift = lng * sc, lnb * sc + m[0]
    ubs, pas = [], []
    for r in range(4):
        rows = slice(r * (TM_PROJ // 4), (r + 1) * (TM_PROJ // 4))
        xhat = _ln_hat(x_ref[0, rows, :])
        hln_ref[0, rows, :] = xhat * lng + lnb
        ubs.append((xhat * scale + shift).astype(BF16))
        pas.append(jnp.dot(ubs[r], w_ref[:, :P_Z], preferred_element_type=F32))
    ub = jnp.concatenate(ubs, axis=0)
    pas = [jnp.concatenate(pas[:2], axis=0), jnp.concatenate(pas[2:], axis=0)]
    pa = jnp.concatenate(pas, axis=0)
    mprev, mnext = _edge_masks(TM_PROJ, GRID_W)
    half = TM_PROJ // 2

    def xbc_half(r):
        out = []
        for j in range(XBC_DIM // LANES):
            sl = slice(j * LANES, (j + 1) * LANES)
            pj = pas[r][:, P_XBC + j * LANES:P_XBC + (j + 1) * LANES]
            out.append(_silu(_conv3(pj, small_ref[S_SCW:S_SCW + 3, sl], mprev[:half], mnext[:half])
                             + small_ref[S_SCB:S_SCB + 1, sl]))
        return out
    is_fwd = _fwd_rows()
    zero = jnp.zeros((N_DH, SSD_CHUNK), F32)
    chunks = [slice(c * SSD_CHUNK, (c + 1) * SSD_CHUNK) for c in range(NCH_PROJ)]

    def conv_slabs(p, first):
        for i in range(2):
            gb, gc, gh = (p[:, (3 * i + k) * LANES:(3 * i + k + 1) * LANES] for k in range(3))
            sl = slice((first + i) * LANES, (first + i + 1) * LANES)
            ycv_ref[0, :, sl] = (gb * _conv3(gc * gh, small_ref[S_CW:S_CW + 3, sl], mprev, mnext)).astype(BF16)

    dtb, alog = _dt_params(small_ref)
    slabs_lo = xbc_half(0)
    dts = [_dt_rows(pa[tok, P_DT:P_DT + LANES], dtb, alog) for tok in chunks]
    pz = jnp.dot(ub, w_ref[:, P_Z:P_CONV], preferred_element_type=F32)
    slabs_hi = xbc_half(1)
    slabs = [jnp.concatenate([lo, hi], axis=0) for lo, hi in zip(slabs_lo, slabs_hi)]

    colss = []
    for c, (dt, adt) in enumerate(dts):
        sc3 = _scan_mm(adt, u_ref)
        cs = jnp.where(is_fwd, sc3[:, :SSD_CHUNK], sc3[:, SSD_CHUNK:2 * SSD_CHUNK])
        tot = sc3[:, 2 * SSD_CHUNK:]
        e1 = jnp.exp(cs)
        w = jnp.exp(tot - cs) * dt
        dsum = dt[:SSD_HEADS] + dt[SSD_HEADS:]
        rows_ref[0, c] = jnp.concatenate([cs - jnp.log(dt), dsum, jnp.zeros_like(dsum)], axis=0)
        table = jnp.concatenate([zero] * 3 + _split(e1, 2) + _split(w, 2) + [cs], axis=0).T
        colf_ref[0, c] = table
        colss.append(table.astype(BF16))
        cols_ref[0, c] = colss[c]

    pc1 = jnp.dot(ub, w_ref[:, P_CONV:P_CONV + 6 * LANES], preferred_element_type=F32)

    xs = jnp.concatenate(slabs[:4], axis=1)
    xs_ref[0] = xs.astype(BF16)
    cm_ref[0] = jnp.concatenate(slabs[6:8], axis=1).astype(BF16)
    zg_ref[0] = _silu(pz).astype(BF16)
    bts = []
    for c, tok in enumerate(chunks):
        bt = [slabs[4 + g][tok].T.astype(BF16) for g in range(SSD_GROUPS)]
        for g in range(SSD_GROUPS):
            bt_ref[0, c, g] = bt[g]
        bts.append(bt)
    wxbs = [jnp.dot(cols, exw_ref[:, SSD_WIDTH:], preferred_element_type=F32) for cols in colss]
    decs = [jnp.dot(cols[0:16], exe_ref[:, SSD_WIDTH:], preferred_element_type=F32)[0:1]
            for cols in colss]

    pc2 = jnp.dot(ub, w_ref[:, P_CONV + 6 * LANES:], preferred_element_type=F32)
    conv_slabs(pc1, 0)

    state = [st_ref[g] for g in range(SSD_GROUPS)]
    for c in reversed(range(NCH_PROJ)):
        xw = (xs[chunks[c]] * wxbs[c]).astype(BF16)
        for g in range(SSD_GROUPS):
            gs = slice(g * GROUP_COLS, (g + 1) * GROUP_COLS)
            sb_ref[0, c, g] = state[g].astype(BF16)
            local = jnp.dot(bts[c][g], xw[:, gs], preferred_element_type=F32)
            state[g] = state[g] * decs[c][:, gs] + local
    for g in range(SSD_GROUPS):
        st_ref[g] = state[g]
    conv_slabs(pc2, 2)


def _proj_call(x, mod, w_in, small, exe, exw, u_chunk, h0, w_out, w_ff1, w_ff2):
    n_steps = BATCH * NT_PROJ
    r1, r4 = D_MODEL // n_steps, D_FF // n_steps
    wrow3 = lambda b, t: (0, b * NT_PROJ + t, 0)
    wrow2 = lambda b, t: (b * NT_PROJ + t, 0)
    rev = lambda b, t: (b, NT_PROJ - 1 - t, 0)
    rev4 = lambda b, t: (b, NT_PROJ - 1 - t, 0, 0)
    rev5 = lambda b, t: (b, NT_PROJ - 1 - t, 0, 0, 0)
    tok = lambda width: pl.BlockSpec((1, TM_PROJ, width), rev)
    out_shape = [
        jax.ShapeDtypeStruct((BATCH, SEQ, D_MODEL), F32),
        jax.ShapeDtypeStruct((BATCH, SEQ, CONV_WIDTH), BF16),
        jax.ShapeDtypeStruct((BATCH, SEQ, SSD_WIDTH), BF16),
        jax.ShapeDtypeStruct((BATCH, SEQ, SSD_WIDTH), BF16),
        jax.ShapeDtypeStruct((BATCH, NCHUNK, SSD_GROUPS, SSD_STATE, SSD_CHUNK), BF16),
        jax.ShapeDtypeStruct((BATCH, SEQ, SSD_GN), BF16),
        jax.ShapeDtypeStruct((BATCH, NCHUNK, 2 * N_DH, SSD_CHUNK), F32),
        jax.ShapeDtypeStruct((BATCH, NCHUNK, SSD_CHUNK, LANES), BF16),
        jax.ShapeDtypeStruct((BATCH, NCHUNK, SSD_CHUNK, LANES), F32),
        jax.ShapeDtypeStruct((BATCH, NCHUNK, SSD_GROUPS, SSD_STATE, GROUP_COLS), BF16),
        jax.ShapeDtypeStruct((D_MODEL, D_MODEL), BF16),
        jax.ShapeDtypeStruct((D_MODEL, D_FF), BF16),
        jax.ShapeDtypeStruct((D_FF, D_MODEL), BF16),
    ]
    out_specs = [
        tok(D_MODEL), tok(CONV_WIDTH), tok(SSD_WIDTH), tok(SSD_WIDTH),
        pl.BlockSpec((1, NCH_PROJ, SSD_GROUPS, SSD_STATE, SSD_CHUNK), rev5),
        tok(SSD_GN),
        pl.BlockSpec((1, NCH_PROJ, 2 * N_DH, SSD_CHUNK), rev4),
        pl.BlockSpec((1, NCH_PROJ, SSD_CHUNK, LANES), rev4),
        pl.BlockSpec((1, NCH_PROJ, SSD_CHUNK, LANES), rev4),
        pl.BlockSpec((1, NCH_PROJ, SSD_GROUPS, SSD_STATE, GROUP_COLS), rev5),
        pl.BlockSpec((r1, D_MODEL), wrow2), pl.BlockSpec((r1, D_FF), wrow2), pl.BlockSpec((r4, D_MODEL), wrow2),
    ]
    in_specs = [
        pl.BlockSpec((1, TM_PROJ, D_MODEL), rev),
        _const_spec((MOD_ROWS, 6 * D_MODEL)),
        _const_spec((D_MODEL, IN_PAD)),
        _const_spec((SMALL_ROWS, D_MODEL)),
        _const_spec((LANES, 2 * SSD_WIDTH)), _const_spec((LANES, 2 * SSD_WIDTH)),
        _const_spec((SSD_CHUNK, 3 * SSD_CHUNK)),
        pl.BlockSpec((1, 1, SSD_GROUPS, SSD_STATE, GROUP_COLS), lambda b, t: (b, 1, 0, 0, 0)),
        pl.BlockSpec((1, r1, D_MODEL), wrow3), pl.BlockSpec((1, r1, D_FF), wrow3),
        pl.BlockSpec((1, r4, D_MODEL), wrow3),
    ]
    return pl.pallas_call(
        _proj_kernel,
        grid=(BATCH, NT_PROJ),
        in_specs=in_specs,
        out_specs=out_specs,
        out_shape=out_shape,
        scratch_shapes=[pltpu.VMEM((SSD_GROUPS, SSD_STATE, GROUP_COLS), F32)],
        compiler_params=pltpu.CompilerParams(dimension_semantics=("arbitrary", "arbitrary"),
                                             vmem_limit_bytes=VMEM_LIMIT),
        name="proj",
    )(x, mod, w_in, small, exe, exw, u_chunk, h0, w_out, w_ff1, w_ff2)


def _out_kernel(hln_ref, mod_ref, ycv_ref, zg_ref, xs_ref, bt_ref, cm_ref, rows_ref,
                cols_ref, colf_ref, sb_ref, h0_ref, exw_ref, small_ref, wout_ref,
                wff1_ref, wff2_ref,
                out_ref, st_ref, yn_ref, h1_ref, u2_ref, g2_ref, r2_ref, hid_ref):
    s = pl.program_id(0)
    wr = s % 2
    rd = (s + 1) % 2
    li = lax.broadcasted_iota(jnp.int32, (SSD_CHUNK, SSD_CHUNK), 0)
    si = lax.broadcasted_iota(jnp.int32, (SSD_CHUNK, SSD_CHUNK), 1)
    low = li >= si
    diag = li == si
    lo_half = si < SSD_HEADDIM

    def ssd_chunk(c, state, fill):
        tok = slice(c * SSD_CHUNK, (c + 1) * SSD_CHUNK)
        rows = rows_ref[0, c]
        cols = cols_ref[0, c]
        xs = xs_ref[0, tok, :]
        cm = cm_ref[0, tok, :]
        colf = colf_ref[0, c]
        bc = [jnp.broadcast_to(colf[:, COL_CSF + j:COL_CSF + j + 1], (SSD_CHUNK, LANES))
              for j in range(N_DH)]
        e1x = [jnp.concatenate(
            [jnp.exp(jnp.where(lo_half, bc[d * SSD_HEADS + 2 * k], bc[d * SSD_HEADS + 2 * k + 1]))
             for k in range(SSD_HEADS // 2)], axis=1) for d in range(N_DIRS)]
        wxf = jnp.dot(cols, exw_ref[:, :SSD_WIDTH], preferred_element_type=F32)
        dec = e1x[0][SSD_CHUNK - 1:SSD_CHUNK, :]
        gmat = [jnp.dot(cm[:, g * SSD_STATE:(g + 1) * SSD_STATE], bt_ref[0, c, g],
                        preferred_element_type=F32) for g in range(SSD_GROUPS)]
        fill[0]()

        ys = []
        for k in range(SSD_HEADS // 2):
            ms = []
            for h in (2 * k, 2 * k + 1):
                g = h // (SSD_HEADS // SSD_GROUPS)
                hb = SSD_HEADS + h
                arg = jnp.where(low,
                                bc[h] - rows[h:h + 1, :],
                                bc[hb] - rows[hb:hb + 1, :])
                decay_dt = jnp.where(diag, rows[N_DH + h:N_DH + h + 1, :], jnp.exp(arg))
                ms.append((gmat[g] * decay_dt).astype(BF16))
            xp = xs[:, k * LANES:(k + 1) * LANES]
            rhs = jnp.concatenate([jnp.where(lo_half, xp, jnp.zeros_like(xp)),
                                   jnp.where(lo_half, jnp.zeros_like(xp), xp)], axis=0)
            ys.append(jnp.dot(jnp.concatenate(ms, axis=1), rhs, preferred_element_type=F32))
            fill[k + 1]()
        y = jnp.concatenate(ys, axis=1)

        yf, yb = [], []
        for g in range(SSD_GROUPS):
            cg = cm[:, g * SSD_STATE:(g + 1) * SSD_STATE]
            yf.append(jnp.dot(cg, state[g].astype(BF16), preferred_element_type=F32))
            yb.append(jnp.dot(cg, sb_ref[0, c, g], preferred_element_type=F32))
        y = (y + jnp.concatenate(yf, axis=1) * e1x[0]
             + jnp.concatenate(yb, axis=1) * e1x[1]
             + xs.astype(F32) * small_ref[S_DXNW:S_DXNW + 1, :SSD_WIDTH])

        yg = y * zg_ref[0, tok, :].astype(F32)
        ms_ = jnp.mean(yg * yg, axis=-1, keepdims=True)
        yn_ref[tok, :CONV_WIDTH] = ycv_ref[0, tok, :]
        yn_ref[tok, CONV_WIDTH:] = (yg * lax.rsqrt(ms_ + RMS_EPS)
                                   * small_ref[S_DXNW:S_DXNW + 1, SSD_WIDTH:]).astype(BF16)
        fill[5]()

        xw = (xs.astype(F32) * wxf).astype(BF16)
        new_state = []
        for g in range(SSD_GROUPS):
            gs = slice(g * GROUP_COLS, (g + 1) * GROUP_COLS)
            local = jnp.dot(bt_ref[0, c, g], xw[:, gs], preferred_element_type=F32)
            new_state.append(state[g] * dec[:, gs] + local)
        return new_state

    def mlp_up(j, k):
        blk = slice(j * FF_BLK + k * FF_SUB, j * FF_BLK + (k + 1) * FF_SUB)
        hid = jnp.maximum(jnp.dot(u2_ref[...], wff1_ref[:, blk], preferred_element_type=F32), 0.0)
        hid_ref[:, blk] = (hid * hid).astype(BF16)

    def mlp_down(j, acc):
        blk = slice(j * FF_BLK, (j + 1) * FF_BLK)
        part = jnp.dot(hid_ref[:, blk], wff2_ref[blk, :], preferred_element_type=F32)
        return part if acc is None else acc + part

    def ln2_rows(i):
        r = slice(i * SSD_CHUNK, (i + 1) * SSD_CHUNK)
        gain, bias = _ln_params(small_ref, 2)
        out_ref[0, r, :] = _ln_hat(r2_ref[r, :]) * gain + bias

    def step(mixer, mlp, norm):
        state, acc = None, [None]
        pieces = []
        if mlp:
            def down(j):
                acc[0] = mlp_down(j, acc[0])
            for j in range(N_SLAB):
                pieces += [(1, functools.partial(mlp_up, j, k)) for k in range(FF_BLK // FF_SUB)]
            for j in range(N_SLAB - DOWN_KEPT):
                at = (j + 1) * (FF_BLK // FF_SUB) + j + DOWN_LAG
                pieces.insert(at, (FF_BLK // FF_SUB, functools.partial(down, j)))
        n_slots = NCH * FILL_SLOTS
        per_slot = sum(cost for cost, _ in pieces) / n_slots
        slots, issued = [], 0.0
        for i in range(n_slots):
            mine = []
            while pieces and issued < (i + 1) * per_slot:
                cost, fn = pieces.pop(0)
                issued += cost
                mine.append(fn)
            slots.append(lambda mine=mine: [fn() for fn in mine])
        if mixer:
            @pl.when(jnp.minimum(s, N_TILES - 1) % NT == 0)
            def _():
                st_ref[...] = h0_ref[0, 0]
            state = [st_ref[g] for g in range(SSD_GROUPS)]
        for c in range(NCH):
            fill = slots[c * FILL_SLOTS:(c + 1) * FILL_SLOTS]
            if mixer:
                state = ssd_chunk(c, state, fill)
            else:
                for f in fill:
                    f()
            if norm:
                ln2_rows(c)
        if mixer:
            for g in range(SSD_GROUPS):
                st_ref[g] = state[g]
            mix = jnp.dot(yn_ref[...], wout_ref[...], preferred_element_type=F32)
            m = _mod_vectors(mod_ref, jnp.minimum(s, N_TILES - 1) // NT)
            g2_ref[wr, 0:1, :] = m[5]
        for i in range(DOWN_KEPT):
            if mlp:
                down(N_SLAB - DOWN_KEPT + i)
            if mixer:
                for j in range(i * NCH // DOWN_KEPT, (i + 1) * NCH // DOWN_KEPT):
                    r = slice(j * SSD_CHUNK, (j + 1) * SSD_CHUNK)
                    gain, bias = _ln_params(small_ref, 1)
                    h1 = _ln_hat(ALPHA * hln_ref[0, r, :] + m[2] * mix[r]) * gain + bias
                    h1_ref[wr, r, :] = h1
                    u2_ref[r, :] = (h1 * (1.0 + m[4]) + m[3]).astype(BF16)
        if mlp:
            r2_ref[...] = ALPHA * h1_ref[rd] + g2_ref[rd, 0:1, :] * acc[0]

    @pl.when(s == 0)
    def _():
        r2_ref[...] = jnp.zeros((TM, D_MODEL), F32)
        step(True, False, False)

    @pl.when(jnp.logical_and(s >= 1, s < N_TILES))
    def _():
        step(True, True, True)

    @pl.when(s == N_TILES)
    def _():
        step(False, True, True)

    @pl.when(s == N_TILES + 1)
    def _():
        step(False, False, True)


def _out_call(hln, mod, ycv, zg, xs, bt, cm, rows, cols, colf, sb, h0, exw, small,
              wout, wff1, wff2):
    cur = lambda s: jnp.minimum(s, N_TILES - 1)
    fwd = lambda s: (cur(s) // NT, cur(s) % NT, 0)
    fwd4 = lambda s: (cur(s) // NT, cur(s) % NT, 0, 0)
    fwd5 = lambda s: (cur(s) // NT, cur(s) % NT, 0, 0, 0)
    done = lambda s: (jnp.maximum(s - 2, 0) // NT, jnp.maximum(s - 2, 0) % NT, 0)
    tok = lambda width: pl.BlockSpec((1, TM, width), fwd)
    in_specs = [
        tok(D_MODEL),
        _const_spec((MOD_ROWS, 6 * D_MODEL)),
        tok(CONV_WIDTH), tok(SSD_WIDTH), tok(SSD_WIDTH),
        pl.BlockSpec((1, NCH, SSD_GROUPS, SSD_STATE, SSD_CHUNK), fwd5),
        tok(SSD_GN),
        pl.BlockSpec((1, NCH, 2 * N_DH, SSD_CHUNK), fwd4),
        pl.BlockSpec((1, NCH, SSD_CHUNK, LANES), fwd4),
        pl.BlockSpec((1, NCH, SSD_CHUNK, LANES), fwd4),
        pl.BlockSpec((1, NCH, SSD_GROUPS, SSD_STATE, GROUP_COLS), fwd5),
        pl.BlockSpec((1, 1, SSD_GROUPS, SSD_STATE, GROUP_COLS), lambda s: (cur(s) // NT, 0, 0, 0, 0)),
        _const_spec((LANES, 2 * SSD_WIDTH)),
        _const_spec((SMALL_ROWS, D_MODEL)),
        _const_spec((D_MODEL, D_MODEL)),
        _const_spec((D_MODEL, D_FF)), _const_spec((D_FF, D_MODEL)),
    ]
    return pl.pallas_call(
        _out_kernel,
        grid=(N_TILES + 2,),
        in_specs=in_specs,
        out_specs=pl.BlockSpec((1, TM, D_MODEL), done),
        out_shape=jax.ShapeDtypeStruct((BATCH, SEQ, D_MODEL), F32),
        scratch_shapes=[pltpu.VMEM((SSD_GROUPS, SSD_STATE, GROUP_COLS), F32),
                        pltpu.VMEM((TM, D_MODEL), BF16),
                        pltpu.VMEM((2, TM, D_MODEL), F32),
                        pltpu.VMEM((TM, D_MODEL), BF16),
                        pltpu.VMEM((2, G2_ROWS, D_MODEL), F32),
                        pltpu.VMEM((TM, D_MODEL), F32),
                        pltpu.VMEM((TM, D_FF), BF16)],
        compiler_params=pltpu.CompilerParams(dimension_semantics=("arbitrary",),
                                             vmem_limit_bytes=VMEM_LIMIT),
        name="out",
    )(hln, mod, ycv, zg, xs, bt, cm, rows, cols, colf, sb, h0, exw, small,
      wout, wff1, wff2)


def kernel(x, c, ctx, c_ctx, ln_in_g, ln_in_b, w_mod, b_mod, w_in, conv_w, ssd_conv_w, ssd_conv_b,
           dt_bias, a_log, ssd_d, ssd_norm_w, w_out, ln1_g, ln1_b, w_ff1, w_ff2, ln2_g, ln2_b):
    flat = lambda v: v.reshape(1, -1)
    mod, w_in_p, small = _prep_call(
        c, flat(c_ctx), w_mod, b_mod, w_in[0].T,
        (flat(ssd_conv_w), ssd_conv_b, flat(conv_w), dt_bias, a_log, ssd_d, ssd_norm_w,
         flat(ln_in_g), flat(ln_in_b), ln1_g, ln1_b, ln2_g, ln2_b))
    exe = jnp.asarray(_EXE, BF16)
    exw = jnp.asarray(_EXW, BF16)

    h0 = _ctx_call(ctx, mod, w_in_p, small, exw, jnp.asarray(_U_CTX, BF16))
    hln, ycv, zg, xs, bt, cm, rows, cols, colf, sb, w_out_p, w_ff1_p, w_ff2_p = _proj_call(
        x, mod, w_in_p, small, exe, exw,
        jnp.asarray(_U_CHUNK, BF16), h0, w_out, w_ff1, w_ff2)
    return _out_call(hln, mod, ycv, zg, xs, bt, cm, rows, cols, colf, sb, h0, exw,
                     small, w_out_p, w_ff1_p, w_ff2_p)
```

```python
import functools

import jax
import jax.numpy as jnp
import numpy as np
from jax import lax
from jax.experimental import pallas as pl
from jax.experimental.pallas import tpu as pltpu

F32 = jnp.float32
BF16 = jnp.bfloat16

D_MODEL = 1024
BATCH = 8
SEQ = 2048
CTX_LEN = 256
GRID_W = 64
CONV_WIDTH = 512
SSD_WIDTH = 512
SSD_HEADDIM = 64
SSD_HEADS = 8
SSD_GROUPS = 2
SSD_STATE = 128
SSD_CHUNK = 128
N_DIRS = 2
D_FF = 4 * D_MODEL
LN_EPS = 1e-5
RMS_EPS = 1e-5
SSD_GN = SSD_GROUPS * SSD_STATE
XBC_DIM = SSD_WIDTH + 2 * SSD_GN
Z_OFF = 3 * CONV_WIDTH
XBC_OFF = Z_OFF + SSD_WIDTH
DT_OFF = XBC_OFF + XBC_DIM
N_DH = N_DIRS * SSD_HEADS
LANES = 128
P_DT = 0
P_XBC = LANES
P_Z = P_XBC + XBC_DIM
P_CONV = P_Z + SSD_WIDTH
IN_PAD = P_CONV + 3 * CONV_WIDTH
CTX_PAD = P_XBC + SSD_WIDTH + SSD_GN
GROUP_COLS = (SSD_HEADS // SSD_GROUPS) * SSD_HEADDIM
ALPHA = 2.0 ** 0.25

TM = 512
CTX_ROWS = 2
MOD_ROWS = 16
SMALL_ROWS = 24
G2_ROWS = 8
PREP_STEPS = 8
NCH = TM // SSD_CHUNK
NT = SEQ // TM
NCHUNK = SEQ // SSD_CHUNK
N_TILES = BATCH * NT
TM_PROJ = 1024
NCH_PROJ = TM_PROJ // SSD_CHUNK
NT_PROJ = SEQ // TM_PROJ
FF_BLK = 1024
FF_SUB = 256
N_SLAB = D_FF // FF_BLK
FILL_SLOTS = 6
DOWN_KEPT = 2
DOWN_LAG = 1
VMEM_LIMIT = 58 * 1024 * 1024

COL_E1 = 48
COL_W = 80
COL_CSF = 112


def _expansion(col0, pieces, width):
    m = np.zeros((LANES, N_DH * width), np.float32)
    for t in range(pieces):
        for j in range(N_DH):
            m[col0 + 16 * t + j, j * width:(j + 1) * width] = 1.0
    return m


_EXE = _expansion(COL_E1, 2, SSD_HEADDIM)
_EXW = _expansion(COL_W, 2, SSD_HEADDIM)


def _ln_hat(x):
    mu = jnp.mean(x, axis=-1, keepdims=True)
    xc = x - mu
    var = jnp.mean(xc * xc, axis=-1, keepdims=True)
    return xc * lax.rsqrt(var + LN_EPS)


def _silu(x):
    return x / (1.0 + jnp.exp(-x))


def _softplus(x):
    return jnp.maximum(x, 0.0) + jnp.log1p(jnp.exp(-jnp.abs(x)))


def _edge_masks(rows, period):
    pos = lax.broadcasted_iota(jnp.int32, (rows, LANES), 0) % period
    return (pos != 0).astype(F32), (pos != period - 1).astype(F32)


def _conv3(t, w, mprev, mnext):
    rows = t.shape[0]
    prev = pltpu.roll(t, 1, 0) * mprev
    nxt = pltpu.roll(t, rows - 1, 0) * mnext
    return prev * w[0:1, :] + t * w[1:2, :] + nxt * w[2:3, :]


def _split(v, pieces):
    out = []
    for _ in range(pieces - 1):
        p = v.astype(BF16).astype(F32)
        out.append(p)
        v = v - p
    out.append(v.astype(BF16).astype(F32))
    return out


def _tri(length, op):
    i = np.arange(length)
    return op(i[:, None], i[None, :]).astype(np.float32)


_U_CHUNK = np.concatenate([_tri(SSD_CHUNK, np.less_equal), _tri(SSD_CHUNK, np.greater_equal),
                           np.ones((SSD_CHUNK, SSD_CHUNK), np.float32)], axis=1)
_U_CTX = np.concatenate([_tri(CTX_LEN, np.greater), _tri(CTX_LEN, np.less)], axis=1)


def _scan_mm(v, u_ref):
    pieces = jnp.concatenate(_split(v, 3), axis=0).astype(BF16)
    o = jnp.dot(pieces, u_ref[...], preferred_element_type=F32)
    return o[0:N_DH] + o[N_DH:2 * N_DH] + o[2 * N_DH:3 * N_DH]


def _dt_rows(raw, dtb, a_log):
    r = raw.T[0:N_DH, :] + dtb
    dt = _softplus(r)
    return dt, dt * (-jnp.exp(a_log))


def _fwd_rows():
    return lax.broadcasted_iota(jnp.int32, (N_DH, 1), 0) < SSD_HEADS


S_SCW = 0
S_SCB = 3
S_CW = 4
S_DXNW = 8
S_LN = 9
S_DTB = 16
S_ALOG = 18


def _dt_params(small_ref):
    sub = lax.broadcasted_iota(jnp.int32, (N_DH, LANES), 0)
    lane = lax.broadcasted_iota(jnp.int32, (N_DH, LANES), 1)
    head = jnp.where(sub < SSD_HEADS, sub, sub - SSD_HEADS)

    def pick(r):
        t = jnp.where(sub < SSD_HEADS, small_ref[r:r + 1, 0:LANES], small_ref[r + 1:r + 2, 0:LANES])
        return jnp.sum(jnp.where(lane == head, t, 0.0), axis=1, keepdims=True)
    return pick(S_DTB), pick(S_ALOG)


def _ln_params(small_ref, which):
    r = S_LN + 2 * which
    return small_ref[r:r + 1, :], small_ref[r + 1:r + 2, :]


def _mod_vectors(mod_ref, row):
    r = mod_ref[pl.ds(row, 1), :]
    return [r[:, k * D_MODEL:(k + 1) * D_MODEL] for k in range(6)]


def _const_spec(shape):
    nd = len(shape)
    return pl.BlockSpec(shape, lambda *_: (0,) * nd, pipeline_mode=pl.Buffered(1))


def _prep_kernel(c_ref, cctx_ref, wmod_ref, bmod_ref, wint_ref,
                 scw_ref, scb_ref, cw_ref, dtb_ref, alog_ref, d_ref, nw_ref,
                 lng_ref, lnb_ref, l1g_ref, l1b_ref, l2g_ref, l2b_ref,
                 mod_ref, pin_ref, small_ref):
    @pl.when(pl.program_id(0) == 0)
    def _():
        small_ref[...] = jnp.zeros((SMALL_ROWS, D_MODEL), F32)
        for k in range(3):
            small_ref[S_SCW + k:S_SCW + k + 1, :XBC_DIM] = scw_ref[:, k * XBC_DIM:(k + 1) * XBC_DIM]
            small_ref[S_CW + k:S_CW + k + 1, :CONV_WIDTH] = cw_ref[:, k * CONV_WIDTH:(k + 1) * CONV_WIDTH]
        small_ref[S_SCB:S_SCB + 1, :XBC_DIM] = scb_ref[...]
        small_ref[S_DTB:S_DTB + N_DIRS, :SSD_HEADS] = dtb_ref[0]
        small_ref[S_ALOG:S_ALOG + N_DIRS, :SSD_HEADS] = alog_ref[0]
        d = d_ref[...]
        first = lax.broadcasted_iota(jnp.int32, (1, LANES), 1) < SSD_HEADDIM
        for k in range(SSD_WIDTH // LANES):
            pair = [jnp.broadcast_to(d[:, 2 * k + i:2 * k + i + 1], (1, LANES)) for i in range(2)]
            small_ref[S_DXNW:S_DXNW + 1, k * LANES:(k + 1) * LANES] = jnp.where(first, pair[0], pair[1])
        small_ref[S_DXNW:S_DXNW + 1, SSD_WIDTH:] = nw_ref[...]
        for k, ref in enumerate((lng_ref, lnb_ref, l1g_ref, l1b_ref, l2g_ref, l2b_ref)):
            small_ref[S_LN + k:S_LN + k + 1, :] = ref[...]

    ctx_row = jnp.where(lax.broadcasted_iota(jnp.int32, (MOD_ROWS - BATCH, D_MODEL), 0) == 0, cctx_ref[...], 0.0)
    cs = _silu(jnp.concatenate([c_ref[...], ctx_row], axis=0)).astype(BF16)
    mod_ref[...] = jnp.dot(cs, wmod_ref[0].astype(BF16), preferred_element_type=F32) + bmod_ref[...]

    dt_rows = jnp.concatenate([wint_ref[DT_OFF:DT_OFF + N_DH, :],
                               jnp.zeros((LANES - N_DH, wint_ref.shape[1]), F32)], axis=0)
    pin_ref[:, P_DT:P_DT + LANES] = dt_rows.T.astype(BF16)
    def move(dst, src):
        pin_ref[:, dst:dst + LANES] = wint_ref[src:src + LANES, :].T.astype(BF16)

    for dst, src, width in ((P_XBC, XBC_OFF, XBC_DIM), (P_Z, Z_OFF, SSD_WIDTH)):
        for j in range(0, width, LANES):
            move(dst + j, src + j)
    for j in range(CONV_WIDTH // LANES):
        for k in range(3):
            move(P_CONV + (3 * j + k) * LANES, k * CONV_WIDTH + j * LANES)


def _prep_call(c, c_ctx, w_mod, b_mod, w_in_t, small_params):
    steps = PREP_STEPS
    r1, rm = D_MODEL // steps, 6 * D_MODEL // steps
    whole = lambda a: pl.BlockSpec(a.shape, lambda i: (0,) * a.ndim)
    return pl.pallas_call(
        _prep_kernel,
        grid=(steps,),
        in_specs=[whole(c), whole(c_ctx),
                  pl.BlockSpec((1, D_MODEL, rm), lambda i: (0, 0, i)),
                  pl.BlockSpec((1, rm), lambda i: (0, i)),
                  pl.BlockSpec((w_in_t.shape[0], r1), lambda i: (0, i))] + [whole(a) for a in small_params],
        out_specs=[pl.BlockSpec((MOD_ROWS, rm), lambda i: (0, i)),
                   pl.BlockSpec((r1, IN_PAD), lambda i: (i, 0)),
                   pl.BlockSpec((SMALL_ROWS, D_MODEL), lambda i: (0, 0))],
        out_shape=[jax.ShapeDtypeStruct((MOD_ROWS, 6 * D_MODEL), F32),
                   jax.ShapeDtypeStruct((D_MODEL, IN_PAD), BF16),
                   jax.ShapeDtypeStruct((SMALL_ROWS, D_MODEL), F32)],
        compiler_params=pltpu.CompilerParams(dimension_semantics=("arbitrary",),
                                             vmem_limit_bytes=VMEM_LIMIT),
        name="prep",
    )(c, c_ctx, w_mod, b_mod, w_in_t, *small_params)


def _ctx_kernel(x_ref, mod_ref, w_ref, small_ref, exw_ref, u_ref, h0_ref):
    dtb, alog = _dt_params(small_ref)
    lng, lnb = _ln_params(small_ref, 0)
    m = _mod_vectors(mod_ref, BATCH)
    sc = 1.0 + m[1]
    x = x_ref[...].reshape(CTX_ROWS * CTX_LEN, D_MODEL)
    u = _ln_hat(x) * (lng * sc) + (lnb * sc + m[0])
    proj = jnp.dot(u.astype(BF16), w_ref[...], preferred_element_type=F32)
    mprev, mnext = _edge_masks(CTX_ROWS * CTX_LEN, CTX_LEN)
    slabs = []
    for j in range((SSD_WIDTH + SSD_GN) // LANES):
        sl = slice(j * LANES, (j + 1) * LANES)
        pj = proj[:, P_XBC + j * LANES:P_XBC + (j + 1) * LANES]
        slabs.append(_silu(_conv3(pj, small_ref[S_SCW:S_SCW + 3, sl], mprev, mnext)
                           + small_ref[S_SCB:S_SCB + 1, sl]))
    zero = jnp.zeros((N_DH, CTX_LEN), F32)
    for i in range(CTX_ROWS):
        tok = slice(i * CTX_LEN, (i + 1) * CTX_LEN)
        xs = jnp.concatenate([sl_[tok] for sl_ in slabs[:4]], axis=1)
        dt, adt = _dt_rows(proj[tok, P_DT:P_DT + LANES], dtb, alog)
        sc2 = _scan_mm(adt, u_ref)
        excl = jnp.where(_fwd_rows(), sc2[:, :CTX_LEN], sc2[:, CTX_LEN:])
        w = jnp.exp(excl) * dt
        table = jnp.concatenate([zero] * (COL_W // 16) + _split(w, 2) + [zero], axis=0)
        cols = table.T.astype(BF16)
        wx = jnp.dot(cols, exw_ref[...], preferred_element_type=F32)
        for d in range(N_DIRS):
            xw = (xs * wx[:, d * SSD_WIDTH:(d + 1) * SSD_WIDTH]).astype(BF16)
            for g in range(SSD_GROUPS):
                bt = slabs[4 + g][tok].T.astype(BF16)
                h0_ref[i, d, g] = jnp.dot(bt, xw[:, g * GROUP_COLS:(g + 1) * GROUP_COLS],
                                          preferred_element_type=F32)


def _ctx_call(ctx, mod, w_in, small, exw, u_ctx):
    return pl.pallas_call(
        _ctx_kernel,
        grid=(BATCH // CTX_ROWS,),
        in_specs=[pl.BlockSpec((CTX_ROWS, CTX_LEN, D_MODEL), lambda b: (b, 0, 0)),
                  _const_spec((MOD_ROWS, 6 * D_MODEL)),
                  _const_spec((D_MODEL, CTX_PAD)), _const_spec((SMALL_ROWS, D_MODEL)),
                  _const_spec((LANES, 2 * SSD_WIDTH)),
                  _const_spec((CTX_LEN, 2 * CTX_LEN))],
        out_specs=pl.BlockSpec((CTX_ROWS, N_DIRS, SSD_GROUPS, SSD_STATE, GROUP_COLS),
                               lambda b: (b, 0, 0, 0, 0)),
        out_shape=jax.ShapeDtypeStruct((BATCH, N_DIRS, SSD_GROUPS, SSD_STATE, GROUP_COLS), F32),
        compiler_params=pltpu.CompilerParams(dimension_semantics=("arbitrary",),
                                             vmem_limit_bytes=VMEM_LIMIT),
        name="ctx",
    )(ctx, mod, w_in, small, exw, u_ctx)


def _proj_kernel(x_ref, mod_ref, w_ref, small_ref,
                 exe_ref, exw_ref, u_ref, h0_ref, wout_ref, wff1_ref, wff2_ref,
                 hln_ref, ycv_ref, zg_ref, xs_ref, bt_ref, cm_ref, rows_ref, cols_ref, colf_ref, sb_ref,
                 pout_ref, pff1_ref, pff2_ref, st_ref):
    pout_ref[...] = wout_ref[0].astype(BF16)
    pff1_ref[...] = wff1_ref[0].astype(BF16)
    pff2_ref[...] = wff2_ref[0].astype(BF16)

    @pl.when(pl.program_id(1) == 0)
    def _():
        st_ref[...] = h0_ref[0, 0]

    m = _mod_vectors(mod_ref, pl.program_id(0))
    sc = 1.0 + m[1]
    lng, lnb = _ln_params(small_ref, 0)
    scale, shift = lng * sc, lnb * sc + m[0]
    ubs, pas = [], []
    for r in range(2):
        rows = slice(r * (TM_PROJ // 2), (r + 1) * (TM_PROJ // 2))
        xhat = _ln_hat(x_ref[0, rows, :])
        hln_ref[0, rows, :] = xhat * lng + lnb
        ubs.append((xhat * scale + shift).astype(BF16))
        pas.append(jnp.dot(ubs[r], w_ref[:, :P_Z], preferred_element_type=F32))
    ub = jnp.concatenate(ubs, axis=0)
    pa = jnp.concatenate(pas, axis=0)
    mprev, mnext = _edge_masks(TM_PROJ, GRID_W)
    half = TM_PROJ // 2

    def xbc_half(r):
        out = []
        for j in range(XBC_DIM // LANES):
            sl = slice(j * LANES, (j + 1) * LANES)
            pj = pas[r][:, P_XBC + j * LANES:P_XBC + (j + 1) * LANES]
            out.append(_silu(_conv3(pj, small_ref[S_SCW:S_SCW + 3, sl], mprev[:half], mnext[:half])
                             + small_ref[S_SCB:S_SCB + 1, sl]))
        return out
    is_fwd = _fwd_rows()
    zero = jnp.zeros((N_DH, SSD_CHUNK), F32)
    chunks = [slice(c * SSD_CHUNK, (c + 1) * SSD_CHUNK) for c in range(NCH_PROJ)]

    def conv_slabs(p, first):
        for i in range(2):
            gb, gc, gh = (p[:, (3 * i + k) * LANES:(3 * i + k + 1) * LANES] for k in range(3))
            sl = slice((first + i) * LANES, (first + i + 1) * LANES)
            ycv_ref[0, :, sl] = (gb * _conv3(gc * gh, small_ref[S_CW:S_CW + 3, sl], mprev, mnext)).astype(BF16)

    dtb, alog = _dt_params(small_ref)
    slabs_lo = xbc_half(0)
    dts = [_dt_rows(pa[tok, P_DT:P_DT + LANES], dtb, alog) for tok in chunks]
    pz = jnp.dot(ub, w_ref[:, P_Z:P_CONV], preferred_element_type=F32)
    slabs_hi = xbc_half(1)
    slabs = [jnp.concatenate([lo, hi], axis=0) for lo, hi in zip(slabs_lo, slabs_hi)]

    colss = []
    for c, (dt, adt) in enumerate(dts):
        sc3 = _scan_mm(adt, u_ref)
        cs = jnp.where(is_fwd, sc3[:, :SSD_CHUNK], sc3[:, SSD_CHUNK:2 * SSD_CHUNK])
        tot = sc3[:, 2 * SSD_CHUNK:]
        e1 = jnp.exp(cs)
        w = jnp.exp(tot - cs) * dt
        dsum = dt[:SSD_HEADS] + dt[SSD_HEADS:]
        rows_ref[0, c] = jnp.concatenate([cs - jnp.log(dt), dsum, jnp.zeros_like(dsum)], axis=0)
        table = jnp.concatenate([zero] * 3 + _split(e1, 2) + _split(w, 2) + [cs], axis=0).T
        colf_ref[0, c] = table
        colss.append(table.astype(BF16))
        cols_ref[0, c] = colss[c]

    pc1 = jnp.dot(ub, w_ref[:, P_CONV:P_CONV + 6 * LANES], preferred_element_type=F32)

    xs = jnp.concatenate(slabs[:4], axis=1)
    xs_ref[0] = xs.astype(BF16)
    cm_ref[0] = jnp.concatenate(slabs[6:8], axis=1).astype(BF16)
    zg_ref[0] = _silu(pz).astype(BF16)
    bts = []
    for c, tok in enumerate(chunks):
        bt = [slabs[4 + g][tok].T.astype(BF16) for g in range(SSD_GROUPS)]
        for g in range(SSD_GROUPS):
            bt_ref[0, c, g] = bt[g]
        bts.append(bt)
    wxbs = [jnp.dot(cols, exw_ref[:, SSD_WIDTH:], preferred_element_type=F32) for cols in colss]
    decs = [jnp.dot(cols[0:16], exe_ref[:, SSD_WIDTH:], preferred_element_type=F32)[0:1]
            for cols in colss]

    pc2 = jnp.dot(ub, w_ref[:, P_CONV + 6 * LANES:], preferred_element_type=F32)
    conv_slabs(pc1, 0)

    state = [st_ref[g] for g in range(SSD_GROUPS)]
    for c in reversed(range(NCH_PROJ)):
        xw = (xs[chunks[c]] * wxbs[c]).astype(BF16)
        for g in range(SSD_GROUPS):
            gs = slice(g * GROUP_COLS, (g + 1) * GROUP_COLS)
            sb_ref[0, c, g] = state[g].astype(BF16)
            local = jnp.dot(bts[c][g], xw[:, gs], preferred_element_type=F32)
            state[g] = state[g] * decs[c][:, gs] + local
    for g in range(SSD_GROUPS):
        st_ref[g] = state[g]
    conv_slabs(pc2, 2)


def _proj_call(x, mod, w_in, small, exe, exw, u_chunk, h0, w_out, w_ff1, w_ff2):
    n_steps = BATCH * NT_PROJ
    r1, r4 = D_MODEL // n_steps, D_FF // n_steps
    wrow3 = lambda b, t: (0, b * NT_PROJ + t, 0)
    wrow2 = lambda b, t: (b * NT_PROJ + t, 0)
    rev = lambda b, t: (b, NT_PROJ - 1 - t, 0)
    rev4 = lambda b, t: (b, NT_PROJ - 1 - t, 0, 0)
    rev5 = lambda b, t: (b, NT_PROJ - 1 - t, 0, 0, 0)
    tok = lambda width: pl.BlockSpec((1, TM_PROJ, width), rev)
    out_shape = [
        jax.ShapeDtypeStruct((BATCH, SEQ, D_MODEL), F32),
        jax.ShapeDtypeStruct((BATCH, SEQ, CONV_WIDTH), BF16),
        jax.ShapeDtypeStruct((BATCH, SEQ, SSD_WIDTH), BF16),
        jax.ShapeDtypeStruct((BATCH, SEQ, SSD_WIDTH), BF16),
        jax.ShapeDtypeStruct((BATCH, NCHUNK, SSD_GROUPS, SSD_STATE, SSD_CHUNK), BF16),
        jax.ShapeDtypeStruct((BATCH, SEQ, SSD_GN), BF16),
        jax.ShapeDtypeStruct((BATCH, NCHUNK, 2 * N_DH, SSD_CHUNK), F32),
        jax.ShapeDtypeStruct((BATCH, NCHUNK, SSD_CHUNK, LANES), BF16),
        jax.ShapeDtypeStruct((BATCH, NCHUNK, SSD_CHUNK, LANES), F32),
        jax.ShapeDtypeStruct((BATCH, NCHUNK, SSD_GROUPS, SSD_STATE, GROUP_COLS), BF16),
        jax.ShapeDtypeStruct((D_MODEL, D_MODEL), BF16),
        jax.ShapeDtypeStruct((D_MODEL, D_FF), BF16),
        jax.ShapeDtypeStruct((D_FF, D_MODEL), BF16),
    ]
    out_specs = [
        tok(D_MODEL), tok(CONV_WIDTH), tok(SSD_WIDTH), tok(SSD_WIDTH),
        pl.BlockSpec((1, NCH_PROJ, SSD_GROUPS, SSD_STATE, SSD_CHUNK), rev5),
        tok(SSD_GN),
        pl.BlockSpec((1, NCH_PROJ, 2 * N_DH, SSD_CHUNK), rev4),
        pl.BlockSpec((1, NCH_PROJ, SSD_CHUNK, LANES), rev4),
        pl.BlockSpec((1, NCH_PROJ, SSD_CHUNK, LANES), rev4),
        pl.BlockSpec((1, NCH_PROJ, SSD_GROUPS, SSD_STATE, GROUP_COLS), rev5),
        pl.BlockSpec((r1, D_MODEL), wrow2), pl.BlockSpec((r1, D_FF), wrow2), pl.BlockSpec((r4, D_MODEL), wrow2),
    ]
    in_specs = [
        pl.BlockSpec((1, TM_PROJ, D_MODEL), rev),
        _const_spec((MOD_ROWS, 6 * D_MODEL)),
        _const_spec((D_MODEL, IN_PAD)),
        _const_spec((SMALL_ROWS, D_MODEL)),
        _const_spec((LANES, 2 * SSD_WIDTH)), _const_spec((LANES, 2 * SSD_WIDTH)),
        _const_spec((SSD_CHUNK, 3 * SSD_CHUNK)),
        pl.BlockSpec((1, 1, SSD_GROUPS, SSD_STATE, GROUP_COLS), lambda b, t: (b, 1, 0, 0, 0)),
        pl.BlockSpec((1, r1, D_MODEL), wrow3), pl.BlockSpec((1, r1, D_FF), wrow3),
        pl.BlockSpec((1, r4, D_MODEL), wrow3),
    ]
    return pl.pallas_call(
        _proj_kernel,
        grid=(BATCH, NT_PROJ),
        in_specs=in_specs,
        out_specs=out_specs,
        out_shape=out_shape,
        scratch_shapes=[pltpu.VMEM((SSD_GROUPS, SSD_STATE, GROUP_COLS), F32)],
        compiler_params=pltpu.CompilerParams(dimension_semantics=("arbitrary", "arbitrary"),
                                             vmem_limit_bytes=VMEM_LIMIT),
        name="proj",
    )(x, mod, w_in, small, exe, exw, u_chunk, h0, w_out, w_ff1, w_ff2)


def _out_kernel(hln_ref, mod_ref, ycv_ref, zg_ref, xs_ref, bt_ref, cm_ref, rows_ref,
                cols_ref, colf_ref, sb_ref, h0_ref, exw_ref, small_ref, wout_ref,
                wff1_hbm, wff2_hbm,
                out_ref, st_ref, yn_ref, h1_ref, u2_ref, g2_ref, r2_ref, hid_ref, wff1_ref, wff2_ref, wsem):
    s = pl.program_id(0)
    wr = s % 2
    rd = (s + 1) % 2
    li = lax.broadcasted_iota(jnp.int32, (SSD_CHUNK, SSD_CHUNK), 0)
    si = lax.broadcasted_iota(jnp.int32, (SSD_CHUNK, SSD_CHUNK), 1)
    low = li >= si
    diag = li == si
    lo_half = si < SSD_HEADDIM

    def ssd_chunk(c, state, fill):
        tok = slice(c * SSD_CHUNK, (c + 1) * SSD_CHUNK)
        rows = rows_ref[0, c]
        cols = cols_ref[0, c]
        xs = xs_ref[0, tok, :]
        cm = cm_ref[0, tok, :]
        colf = colf_ref[0, c]
        bc = [jnp.broadcast_to(colf[:, COL_CSF + j:COL_CSF + j + 1], (SSD_CHUNK, LANES))
              for j in range(N_DH)]
        e1x = [jnp.concatenate(
            [jnp.exp(jnp.where(lo_half, bc[d * SSD_HEADS + 2 * k], bc[d * SSD_HEADS + 2 * k + 1]))
             for k in range(SSD_HEADS // 2)], axis=1) for d in range(N_DIRS)]
        wxf = jnp.dot(cols, exw_ref[:, :SSD_WIDTH], preferred_element_type=F32)
        dec = e1x[0][SSD_CHUNK - 1:SSD_CHUNK, :]
        gmat = [jnp.dot(cm[:, g * SSD_STATE:(g + 1) * SSD_STATE], bt_ref[0, c, g],
                        preferred_element_type=F32) for g in range(SSD_GROUPS)]
        fill[0]()

        ys = []
        for k in range(SSD_HEADS // 2):
            ms = []
            for h in (2 * k, 2 * k + 1):
                g = h // (SSD_HEADS // SSD_GROUPS)
                hb = SSD_HEADS + h
                arg = jnp.where(low,
                                bc[h] - rows[h:h + 1, :],
                                bc[hb] - rows[hb:hb + 1, :])
                decay_dt = jnp.where(diag, rows[N_DH + h:N_DH + h + 1, :], jnp.exp(arg))
                ms.append((gmat[g] * decay_dt).astype(BF16))
            xp = xs[:, k * LANES:(k + 1) * LANES]
            rhs = jnp.concatenate([jnp.where(lo_half, xp, jnp.zeros_like(xp)),
                                   jnp.where(lo_half, jnp.zeros_like(xp), xp)], axis=0)
            ys.append(jnp.dot(jnp.concatenate(ms, axis=1), rhs, preferred_element_type=F32))
            fill[k + 1]()
        y = jnp.concatenate(ys, axis=1)

        yf, yb = [], []
        for g in range(SSD_GROUPS):
            cg = cm[:, g * SSD_STATE:(g + 1) * SSD_STATE]
            yf.append(jnp.dot(cg, state[g].astype(BF16), preferred_element_type=F32))
            yb.append(jnp.dot(cg, sb_ref[0, c, g], preferred_element_type=F32))
        y = (y + jnp.concatenate(yf, axis=1) * e1x[0]
             + jnp.concatenate(yb, axis=1) * e1x[1]
             + xs.astype(F32) * small_ref[S_DXNW:S_DXNW + 1, :SSD_WIDTH])

        yg = y * zg_ref[0, tok, :].astype(F32)
        ms_ = jnp.mean(yg * yg, axis=-1, keepdims=True)
        yn_ref[tok, :CONV_WIDTH] = ycv_ref[0, tok, :]
        yn_ref[tok, CONV_WIDTH:] = (yg * lax.rsqrt(ms_ + RMS_EPS)
                                   * small_ref[S_DXNW:S_DXNW + 1, SSD_WIDTH:]).astype(BF16)
        fill[5]()

        xw = (xs.astype(F32) * wxf).astype(BF16)
        new_state = []
        for g in range(SSD_GROUPS):
            gs = slice(g * GROUP_COLS, (g + 1) * GROUP_COLS)
            local = jnp.dot(bt_ref[0, c, g], xw[:, gs], preferred_element_type=F32)
            new_state.append(state[g] * dec[:, gs] + local)
        return new_state

    def mlp_up(j, k):
        blk = slice(j * FF_BLK + k * FF_SUB, j * FF_BLK + (k + 1) * FF_SUB)
        hid = jnp.maximum(jnp.dot(u2_ref[...], wff1_ref[:, blk], preferred_element_type=F32), 0.0)
        hid_ref[:, blk] = (hid * hid).astype(BF16)

    def mlp_down(j, acc):
        blk = slice(j * FF_BLK, (j + 1) * FF_BLK)
        part = jnp.dot(hid_ref[:, blk], wff2_ref[blk, :], preferred_element_type=F32)
        return part if acc is None else acc + part

    def ln2_rows(i):
        r = slice(i * SSD_CHUNK, (i + 1) * SSD_CHUNK)
        gain, bias = _ln_params(small_ref, 2)
        out_ref[0, r, :] = _ln_hat(r2_ref[r, :]) * gain + bias

    def step(mixer, mlp, norm):
        state, acc = None, [None]
        pieces = []
        if mlp:
            def down(j):
                acc[0] = mlp_down(j, acc[0])
            for j in range(N_SLAB):
                pieces += [(1, functools.partial(mlp_up, j, k)) for k in range(FF_BLK // FF_SUB)]
            for j in range(N_SLAB - DOWN_KEPT):
                at = (j + 1) * (FF_BLK // FF_SUB) + j + DOWN_LAG
                pieces.insert(at, (FF_BLK // FF_SUB, functools.partial(down, j)))
        n_slots = NCH * FILL_SLOTS
        per_slot = sum(cost for cost, _ in pieces) / n_slots
        slots, issued = [], 0.0
        for i in range(n_slots):
            mine = []
            while pieces and issued < (i + 1) * per_slot:
                cost, fn = pieces.pop(0)
                issued += cost
                mine.append(fn)
            slots.append(lambda mine=mine: [fn() for fn in mine])
        if mixer:
            @pl.when(jnp.minimum(s, N_TILES - 1) % NT == 0)
            def _():
                st_ref[...] = h0_ref[0, 0]
            state = [st_ref[g] for g in range(SSD_GROUPS)]
        for c in range(NCH):
            fill = slots[c * FILL_SLOTS:(c + 1) * FILL_SLOTS]
            if mixer:
                state = ssd_chunk(c, state, fill)
            else:
                for f in fill:
                    f()
            if norm:
                ln2_rows(c)
        if mixer:
            for g in range(SSD_GROUPS):
                st_ref[g] = state[g]
            mix = jnp.dot(yn_ref[...], wout_ref[...], preferred_element_type=F32)
            m = _mod_vectors(mod_ref, jnp.minimum(s, N_TILES - 1) // NT)
            g2_ref[wr, 0:1, :] = m[5]
        for i in range(DOWN_KEPT):
            if mlp:
                down(N_SLAB - DOWN_KEPT + i)
            if mixer:
                for j in range(i * NCH // DOWN_KEPT, (i + 1) * NCH // DOWN_KEPT):
                    r = slice(j * SSD_CHUNK, (j + 1) * SSD_CHUNK)
                    gain, bias = _ln_params(small_ref, 1)
                    h1 = _ln_hat(ALPHA * hln_ref[0, r, :] + m[2] * mix[r]) * gain + bias
                    h1_ref[wr, r, :] = h1
                    u2_ref[r, :] = (h1 * (1.0 + m[4]) + m[3]).astype(BF16)
        if mlp:
            r2_ref[...] = ALPHA * h1_ref[rd] + g2_ref[rd, 0:1, :] * acc[0]

    mlp_weights = (pltpu.make_async_copy(wff1_hbm, wff1_ref, wsem.at[0]),
                   pltpu.make_async_copy(wff2_hbm, wff2_ref, wsem.at[1]))

    @pl.when(s == 0)
    def _():
        for copy in mlp_weights:
            copy.start()
        r2_ref[...] = jnp.zeros((TM, D_MODEL), F32)
        step(True, False, False)

    @pl.when(s == 1)
    def _():
        for copy in mlp_weights:
            copy.wait()

    @pl.when(jnp.logical_and(s >= 1, s < N_TILES))
    def _():
        step(True, True, True)

    @pl.when(s == N_TILES)
    def _():
        step(False, True, True)

    @pl.when(s == N_TILES + 1)
    def _():
        step(False, False, True)


def _out_call(hln, mod, ycv, zg, xs, bt, cm, rows, cols, colf, sb, h0, exw, small,
              wout, wff1, wff2):
    cur = lambda s: jnp.minimum(s, N_TILES - 1)
    fwd = lambda s: (cur(s) // NT, cur(s) % NT, 0)
    fwd4 = lambda s: (cur(s) // NT, cur(s) % NT, 0, 0)
    fwd5 = lambda s: (cur(s) // NT, cur(s) % NT, 0, 0, 0)
    done = lambda s: (jnp.maximum(s - 2, 0) // NT, jnp.maximum(s - 2, 0) % NT, 0)
    tok = lambda width: pl.BlockSpec((1, TM, width), fwd)
    in_specs = [
        tok(D_MODEL),
        _const_spec((MOD_ROWS, 6 * D_MODEL)),
        tok(CONV_WIDTH), tok(SSD_WIDTH), tok(SSD_WIDTH),
        pl.BlockSpec((1, NCH, SSD_GROUPS, SSD_STATE, SSD_CHUNK), fwd5),
        tok(SSD_GN),
        pl.BlockSpec((1, NCH, 2 * N_DH, SSD_CHUNK), fwd4),
        pl.BlockSpec((1, NCH, SSD_CHUNK, LANES), fwd4),
        pl.BlockSpec((1, NCH, SSD_CHUNK, LANES), fwd4),
        pl.BlockSpec((1, NCH, SSD_GROUPS, SSD_STATE, GROUP_COLS), fwd5),
        pl.BlockSpec((1, 1, SSD_GROUPS, SSD_STATE, GROUP_COLS), lambda s: (cur(s) // NT, 0, 0, 0, 0)),
        _const_spec((LANES, 2 * SSD_WIDTH)),
        _const_spec((SMALL_ROWS, D_MODEL)),
        _const_spec((D_MODEL, D_MODEL)),
        pl.BlockSpec(memory_space=pl.ANY), pl.BlockSpec(memory_space=pl.ANY),
    ]
    return pl.pallas_call(
        _out_kernel,
        grid=(N_TILES + 2,),
        in_specs=in_specs,
        out_specs=pl.BlockSpec((1, TM, D_MODEL), done),
        out_shape=jax.ShapeDtypeStruct((BATCH, SEQ, D_MODEL), F32),
        scratch_shapes=[pltpu.VMEM((SSD_GROUPS, SSD_STATE, GROUP_COLS), F32),
                        pltpu.VMEM((TM, D_MODEL), BF16),
                        pltpu.VMEM((2, TM, D_MODEL), F32),
                        pltpu.VMEM((TM, D_MODEL), BF16),
                        pltpu.VMEM((2, G2_ROWS, D_MODEL), F32),
                        pltpu.VMEM((TM, D_MODEL), F32),
                        pltpu.VMEM((TM, D_FF), BF16),
                        pltpu.VMEM((D_MODEL, D_FF), BF16),
                        pltpu.VMEM((D_FF, D_MODEL), BF16),
                        pltpu.SemaphoreType.DMA((2,))],
        compiler_params=pltpu.CompilerParams(dimension_semantics=("arbitrary",),
                                             vmem_limit_bytes=VMEM_LIMIT),
        name="out",
    )(hln, mod, ycv, zg, xs, bt, cm, rows, cols, colf, sb, h0, exw, small,
      wout, wff1, wff2)


def kernel(x, c, ctx, c_ctx, ln_in_g, ln_in_b, w_mod, b_mod, w_in, conv_w, ssd_conv_w, ssd_conv_b,
           dt_bias, a_log, ssd_d, ssd_norm_w, w_out, ln1_g, ln1_b, w_ff1, w_ff2, ln2_g, ln2_b):
    flat = lambda v: v.reshape(1, -1)
    mod, w_in_p, small = _prep_call(
        c, flat(c_ctx), w_mod, b_mod, w_in[0].T,
        (flat(ssd_conv_w), ssd_conv_b, flat(conv_w), dt_bias, a_log, ssd_d, ssd_norm_w,
         flat(ln_in_g), flat(ln_in_b), ln1_g, ln1_b, ln2_g, ln2_b))
    exe = jnp.asarray(_EXE, BF16)
    exw = jnp.asarray(_EXW, BF16)

    h0 = _ctx_call(ctx, mod, w_in_p, small, exw, jnp.asarray(_U_CTX, BF16))
    hln, ycv, zg, xs, bt, cm, rows, cols, colf, sb, w_out_p, w_ff1_p, w_ff2_p = _proj_call(
        x, mod, w_in_p, small, exe, exw,
        jnp.asarray(_U_CHUNK, BF16), h0, w_out, w_ff1, w_ff2)
    return _out_call(hln, mod, ycv, zg, xs, bt, cm, rows, cols, colf, sb, h0, exw,
                     small, w_out_p, w_ff1_p, w_ff2_p)
```

```python
import functools

import jax
import jax.numpy as jnp
import numpy as np
from jax import lax
from jax.experimental import pallas as pl
from jax.experimental.pallas import tpu as pltpu

F32 = jnp.float32
BF16 = jnp.bfloat16

D_MODEL = 1024
BATCH = 8
SEQ = 2048
CTX_LEN = 256
GRID_W = 64
CONV_WIDTH = 512
SSD_WIDTH = 512
SSD_HEADDIM = 64
SSD_HEADS = 8
SSD_GROUPS = 2
SSD_STATE = 128
SSD_CHUNK = 128
N_DIRS = 2
D_FF = 4 * D_MODEL
LN_EPS = 1e-5
RMS_EPS = 1e-5
SSD_GN = SSD_GROUPS * SSD_STATE
XBC_DIM = SSD_WIDTH + 2 * SSD_GN
Z_OFF = 3 * CONV_WIDTH
XBC_OFF = Z_OFF + SSD_WIDTH
DT_OFF = XBC_OFF + XBC_DIM
N_DH = N_DIRS * SSD_HEADS
LANES = 128
P_DT = 0
P_XBC = LANES
P_Z = P_XBC + XBC_DIM
P_CONV = P_Z + SSD_WIDTH
IN_PAD = P_CONV + 3 * CONV_WIDTH
CTX_PAD = P_XBC + SSD_WIDTH + SSD_GN
GROUP_COLS = (SSD_HEADS // SSD_GROUPS) * SSD_HEADDIM
ALPHA = 2.0 ** 0.25

TM = 512
CTX_ROWS = 2
MOD_ROWS = 16
SMALL_ROWS = 24
G2_ROWS = 8
PREP_STEPS = 8
NCH = TM // SSD_CHUNK
NT = SEQ // TM
NCHUNK = SEQ // SSD_CHUNK
N_TILES = BATCH * NT
TM_PROJ = 1024
NCH_PROJ = TM_PROJ // SSD_CHUNK
NT_PROJ = SEQ // TM_PROJ
FF_BLK = 1024
FF_SUB = 256
N_SLAB = D_FF // FF_BLK
FILL_SLOTS = 6
DOWN_KEPT = 3
DOWN_LAG = 1
VMEM_LIMIT = 58 * 1024 * 1024

COL_E1 = 48
COL_W = 80
COL_CSF = 112


def _expansion(col0, pieces, width):
    m = np.zeros((LANES, N_DH * width), np.float32)
    for t in range(pieces):
        for j in range(N_DH):
            m[col0 + 16 * t + j, j * width:(j + 1) * width] = 1.0
    return m


_EXE = _expansion(COL_E1, 2, SSD_HEADDIM)
_EXW = _expansion(COL_W, 2, SSD_HEADDIM)


def _ln_hat(x):
    mu = jnp.mean(x, axis=-1, keepdims=True)
    xc = x - mu
    var = jnp.mean(xc * xc, axis=-1, keepdims=True)
    return xc * lax.rsqrt(var + LN_EPS)


def _silu(x):
    return x / (1.0 + jnp.exp(-x))


def _softplus(x):
    return jnp.maximum(x, 0.0) + jnp.log1p(jnp.exp(-jnp.abs(x)))


def _edge_masks(rows, period):
    pos = lax.broadcasted_iota(jnp.int32, (rows, LANES), 0) % period
    return (pos != 0).astype(F32), (pos != period - 1).astype(F32)


def _conv3(t, w, mprev, mnext):
    rows = t.shape[0]
    prev = pltpu.roll(t, 1, 0) * mprev
    nxt = pltpu.roll(t, rows - 1, 0) * mnext
    return prev * w[0:1, :] + t * w[1:2, :] + nxt * w[2:3, :]


def _split(v, pieces):
    out = []
    for _ in range(pieces - 1):
        p = v.astype(BF16).astype(F32)
        out.append(p)
        v = v - p
    out.append(v.astype(BF16).astype(F32))
    return out


def _tri(length, op):
    i = np.arange(length)
    return op(i[:, None], i[None, :]).astype(np.float32)


_U_CHUNK = np.concatenate([_tri(SSD_CHUNK, np.less_equal), _tri(SSD_CHUNK, np.greater_equal),
                           np.ones((SSD_CHUNK, SSD_CHUNK), np.float32)], axis=1)
_U_CTX = np.concatenate([_tri(CTX_LEN, np.greater), _tri(CTX_LEN, np.less)], axis=1)


def _scan_mm(v, u_ref):
    pieces = jnp.concatenate(_split(v, 3), axis=0).astype(BF16)
    o = jnp.dot(pieces, u_ref[...], preferred_element_type=F32)
    return o[0:N_DH] + o[N_DH:2 * N_DH] + o[2 * N_DH:3 * N_DH]


def _dt_rows(raw, dtb, a_log):
    r = raw.T[0:N_DH, :] + dtb
    dt = _softplus(r)
    return dt, dt * (-jnp.exp(a_log))


def _fwd_rows():
    return lax.broadcasted_iota(jnp.int32, (N_DH, 1), 0) < SSD_HEADS


S_SCW = 0
S_SCB = 3
S_CW = 4
S_DXNW = 8
S_LN = 9
S_DTB = 16
S_ALOG = 18


def _dt_params(small_ref):
    sub = lax.broadcasted_iota(jnp.int32, (N_DH, LANES), 0)
    lane = lax.broadcasted_iota(jnp.int32, (N_DH, LANES), 1)
    head = jnp.where(sub < SSD_HEADS, sub, sub - SSD_HEADS)

    def pick(r):
        t = jnp.where(sub < SSD_HEADS, small_ref[r:r + 1, 0:LANES], small_ref[r + 1:r + 2, 0:LANES])
        return jnp.sum(jnp.where(lane == head, t, 0.0), axis=1, keepdims=True)
    return pick(S_DTB), pick(S_ALOG)


def _ln_params(small_ref, which):
    r = S_LN + 2 * which
    return small_ref[r:r + 1, :], small_ref[r + 1:r + 2, :]


def _mod_vectors(mod_ref, row):
    r = mod_ref[pl.ds(row, 1), :]
    return [r[:, k * D_MODEL:(k + 1) * D_MODEL] for k in range(6)]


def _const_spec(shape):
    nd = len(shape)
    return pl.BlockSpec(shape, lambda *_: (0,) * nd, pipeline_mode=pl.Buffered(1))


def _prep_kernel(c_ref, cctx_ref, wmod_ref, bmod_ref, wint_ref,
                 scw_ref, scb_ref, cw_ref, dtb_ref, alog_ref, d_ref, nw_ref,
                 lng_ref, lnb_ref, l1g_ref, l1b_ref, l2g_ref, l2b_ref,
                 mod_ref, pin_ref, small_ref):
    @pl.when(pl.program_id(0) == 0)
    def _():
        small_ref[...] = jnp.zeros((SMALL_ROWS, D_MODEL), F32)
        for k in range(3):
            small_ref[S_SCW + k:S_SCW + k + 1, :XBC_DIM] = scw_ref[:, k * XBC_DIM:(k + 1) * XBC_DIM]
            small_ref[S_CW + k:S_CW + k + 1, :CONV_WIDTH] = cw_ref[:, k * CONV_WIDTH:(k + 1) * CONV_WIDTH]
        small_ref[S_SCB:S_SCB + 1, :XBC_DIM] = scb_ref[...]
        small_ref[S_DTB:S_DTB + N_DIRS, :SSD_HEADS] = dtb_ref[0]
        small_ref[S_ALOG:S_ALOG + N_DIRS, :SSD_HEADS] = alog_ref[0]
        d = d_ref[...]
        first = lax.broadcasted_iota(jnp.int32, (1, LANES), 1) < SSD_HEADDIM
        for k in range(SSD_WIDTH // LANES):
            pair = [jnp.broadcast_to(d[:, 2 * k + i:2 * k + i + 1], (1, LANES)) for i in range(2)]
            small_ref[S_DXNW:S_DXNW + 1, k * LANES:(k + 1) * LANES] = jnp.where(first, pair[0], pair[1])
        small_ref[S_DXNW:S_DXNW + 1, SSD_WIDTH:] = nw_ref[...]
        for k, ref in enumerate((lng_ref, lnb_ref, l1g_ref, l1b_ref, l2g_ref, l2b_ref)):
            small_ref[S_LN + k:S_LN + k + 1, :] = ref[...]

    ctx_row = jnp.where(lax.broadcasted_iota(jnp.int32, (MOD_ROWS - BATCH, D_MODEL), 0) == 0, cctx_ref[...], 0.0)
    cs = _silu(jnp.concatenate([c_ref[...], ctx_row], axis=0)).astype(BF16)
    mod_ref[...] = jnp.dot(cs, wmod_ref[0].astype(BF16), preferred_element_type=F32) + bmod_ref[...]

    dt_rows = jnp.concatenate([wint_ref[DT_OFF:DT_OFF + N_DH, :],
                               jnp.zeros((LANES - N_DH, wint_ref.shape[1]), F32)], axis=0)
    pin_ref[:, P_DT:P_DT + LANES] = dt_rows.T.astype(BF16)
    def move(dst, src):
        pin_ref[:, dst:dst + LANES] = wint_ref[src:src + LANES, :].T.astype(BF16)

    for dst, src, width in ((P_XBC, XBC_OFF, XBC_DIM), (P_Z, Z_OFF, SSD_WIDTH)):
        for j in range(0, width, LANES):
            move(dst + j, src + j)
    for j in range(CONV_WIDTH // LANES):
        for k in range(3):
            move(P_CONV + (3 * j + k) * LANES, k * CONV_WIDTH + j * LANES)


def _prep_call(c, c_ctx, w_mod, b_mod, w_in_t, small_params):
    steps = PREP_STEPS
    r1, rm = D_MODEL // steps, 6 * D_MODEL // steps
    whole = lambda a: pl.BlockSpec(a.shape, lambda i: (0,) * a.ndim)
    return pl.pallas_call(
        _prep_kernel,
        grid=(steps,),
        in_specs=[whole(c), whole(c_ctx),
                  pl.BlockSpec((1, D_MODEL, rm), lambda i: (0, 0, i)),
                  pl.BlockSpec((1, rm), lambda i: (0, i)),
                  pl.BlockSpec((w_in_t.shape[0], r1), lambda i: (0, i))] + [whole(a) for a in small_params],
        out_specs=[pl.BlockSpec((MOD_ROWS, rm), lambda i: (0, i)),
                   pl.BlockSpec((r1, IN_PAD), lambda i: (i, 0)),
                   pl.BlockSpec((SMALL_ROWS, D_MODEL), lambda i: (0, 0))],
        out_shape=[jax.ShapeDtypeStruct((MOD_ROWS, 6 * D_MODEL), F32),
                   jax.ShapeDtypeStruct((D_MODEL, IN_PAD), BF16),
                   jax.ShapeDtypeStruct((SMALL_ROWS, D_MODEL), F32)],
        compiler_params=pltpu.CompilerParams(dimension_semantics=("arbitrary",),
                                             vmem_limit_bytes=VMEM_LIMIT),
        name="prep",
    )(c, c_ctx, w_mod, b_mod, w_in_t, *small_params)


def _ctx_kernel(x_ref, mod_ref, w_ref, small_ref, exw_ref, u_ref, h0_ref):
    dtb, alog = _dt_params(small_ref)
    lng, lnb = _ln_params(small_ref, 0)
    m = _mod_vectors(mod_ref, BATCH)
    sc = 1.0 + m[1]
    x = x_ref[...].reshape(CTX_ROWS * CTX_LEN, D_MODEL)
    u = _ln_hat(x) * (lng * sc) + (lnb * sc + m[0])
    proj = jnp.dot(u.astype(BF16), w_ref[...], preferred_element_type=F32)
    mprev, mnext = _edge_masks(CTX_ROWS * CTX_LEN, CTX_LEN)
    slabs = []
    for j in range((SSD_WIDTH + SSD_GN) // LANES):
        sl = slice(j * LANES, (j + 1) * LANES)
        pj = proj[:, P_XBC + j * LANES:P_XBC + (j + 1) * LANES]
        slabs.append(_silu(_conv3(pj, small_ref[S_SCW:S_SCW + 3, sl], mprev, mnext)
                           + small_ref[S_SCB:S_SCB + 1, sl]))
    zero = jnp.zeros((N_DH, CTX_LEN), F32)
    for i in range(CTX_ROWS):
        tok = slice(i * CTX_LEN, (i + 1) * CTX_LEN)
        xs = jnp.concatenate([sl_[tok] for sl_ in slabs[:4]], axis=1)
        dt, adt = _dt_rows(proj[tok, P_DT:P_DT + LANES], dtb, alog)
        sc2 = _scan_mm(adt, u_ref)
        excl = jnp.where(_fwd_rows(), sc2[:, :CTX_LEN], sc2[:, CTX_LEN:])
        w = jnp.exp(excl) * dt
        table = jnp.concatenate([zero] * (COL_W // 16) + _split(w, 2) + [zero], axis=0)
        cols = table.T.astype(BF16)
        wx = jnp.dot(cols, exw_ref[...], preferred_element_type=F32)
        for d in range(N_DIRS):
            xw = (xs * wx[:, d * SSD_WIDTH:(d + 1) * SSD_WIDTH]).astype(BF16)
            for g in range(SSD_GROUPS):
                bt = slabs[4 + g][tok].T.astype(BF16)
                h0_ref[i, d, g] = jnp.dot(bt, xw[:, g * GROUP_COLS:(g + 1) * GROUP_COLS],
                                          preferred_element_type=F32)


def _ctx_call(ctx, mod, w_in, small, exw, u_ctx):
    return pl.pallas_call(
        _ctx_kernel,
        grid=(BATCH // CTX_ROWS,),
        in_specs=[pl.BlockSpec((CTX_ROWS, CTX_LEN, D_MODEL), lambda b: (b, 0, 0)),
                  _const_spec((MOD_ROWS, 6 * D_MODEL)),
                  _const_spec((D_MODEL, CTX_PAD)), _const_spec((SMALL_ROWS, D_MODEL)),
                  _const_spec((LANES, 2 * SSD_WIDTH)),
                  _const_spec((CTX_LEN, 2 * CTX_LEN))],
        out_specs=pl.BlockSpec((CTX_ROWS, N_DIRS, SSD_GROUPS, SSD_STATE, GROUP_COLS),
                               lambda b: (b, 0, 0, 0, 0)),
        out_shape=jax.ShapeDtypeStruct((BATCH, N_DIRS, SSD_GROUPS, SSD_STATE, GROUP_COLS), F32),
        compiler_params=pltpu.CompilerParams(dimension_semantics=("arbitrary",),
                                             vmem_limit_bytes=VMEM_LIMIT),
        name="ctx",
    )(ctx, mod, w_in, small, exw, u_ctx)


def _proj_kernel(x_ref, mod_ref, w_ref, small_ref,
                 exe_ref, exw_ref, u_ref, h0_ref, wout_ref, wff1_ref, wff2_ref,
                 hln_ref, ycv_ref, zg_ref, xs_ref, bt_ref, cm_ref, rows_ref, cols_ref, colf_ref, sb_ref,
                 pout_ref, pff1_ref, pff2_ref, st_ref):
    pout_ref[...] = wout_ref[0].astype(BF16)
    pff1_ref[...] = wff1_ref[0].astype(BF16)
    pff2_ref[...] = wff2_ref[0].astype(BF16)

    @pl.when(pl.program_id(1) == 0)
    def _():
        st_ref[...] = h0_ref[0, 0]

    m = _mod_vectors(mod_ref, pl.program_id(0))
    sc = 1.0 + m[1]
    lng, lnb = _ln_params(small_ref, 0)
    scale, shift = lng * sc, lnb * sc + m[0]
    ubs, pas = [], []
    for r in range(2):
        rows = slice(r * (TM_PROJ // 2), (r + 1) * (TM_PROJ // 2))
        xhat = _ln_hat(x_ref[0, rows, :])
        hln_ref[0, rows, :] = xhat * lng + lnb
        ubs.append((xhat * scale + shift).astype(BF16))
        pas.append(jnp.dot(ubs[r], w_ref[:, :P_Z], preferred_element_type=F32))
    ub = jnp.concatenate(ubs, axis=0)
    pa = jnp.concatenate(pas, axis=0)
    mprev, mnext = _edge_masks(TM_PROJ, GRID_W)
    half = TM_PROJ // 2

    def xbc_half(r):
        out = []
        for j in range(XBC_DIM // LANES):
            sl = slice(j * LANES, (j + 1) * LANES)
            pj = pas[r][:, P_XBC + j * LANES:P_XBC + (j + 1) * LANES]
            out.append(_silu(_conv3(pj, small_ref[S_SCW:S_SCW + 3, sl], mprev[:half], mnext[:half])
                             + small_ref[S_SCB:S_SCB + 1, sl]))
        return out
    is_fwd = _fwd_rows()
    zero = jnp.zeros((N_DH, SSD_CHUNK), F32)
    chunks = [slice(c * SSD_CHUNK, (c + 1) * SSD_CHUNK) for c in range(NCH_PROJ)]

    def conv_slabs(p, first):
        for i in range(2):
            gb, gc, gh = (p[:, (3 * i + k) * LANES:(3 * i + k + 1) * LANES] for k in range(3))
            sl = slice((first + i) * LANES, (first + i + 1) * LANES)
            ycv_ref[0, :, sl] = (gb * _conv3(gc * gh, small_ref[S_CW:S_CW + 3, sl], mprev, mnext)).astype(BF16)

    dtb, alog = _dt_params(small_ref)
    slabs_lo = xbc_half(0)
    dts = [_dt_rows(pa[tok, P_DT:P_DT + LANES], dtb, alog) for tok in chunks]
    pz = jnp.dot(ub, w_ref[:, P_Z:P_CONV], preferred_element_type=F32)
    slabs_hi = xbc_half(1)
    slabs = [jnp.concatenate([lo, hi], axis=0) for lo, hi in zip(slabs_lo, slabs_hi)]

    colss = []
    for c, (dt, adt) in enumerate(dts):
        sc3 = _scan_mm(adt, u_ref)
        cs = jnp.where(is_fwd, sc3[:, :SSD_CHUNK], sc3[:, SSD_CHUNK:2 * SSD_CHUNK])
        tot = sc3[:, 2 * SSD_CHUNK:]
        e1 = jnp.exp(cs)
        w = jnp.exp(tot - cs) * dt
        dsum = dt[:SSD_HEADS] + dt[SSD_HEADS:]
        rows_ref[0, c] = jnp.concatenate([cs - jnp.log(dt), dsum, jnp.zeros_like(dsum)], axis=0)
        table = jnp.concatenate([zero] * 3 + _split(e1, 2) + _split(w, 2) + [cs], axis=0).T
        colf_ref[0, c] = table
        colss.append(table.astype(BF16))
        cols_ref[0, c] = colss[c]

    pc1 = jnp.dot(ub, w_ref[:, P_CONV:P_CONV + 6 * LANES], preferred_element_type=F32)

    xs = jnp.concatenate(slabs[:4], axis=1)
    xs_ref[0] = xs.astype(BF16)
    cm_ref[0] = jnp.concatenate(slabs[6:8], axis=1).astype(BF16)
    zg_ref[0] = _silu(pz).astype(BF16)
    bts = []
    for c, tok in enumerate(chunks):
        bt = [slabs[4 + g][tok].T.astype(BF16) for g in range(SSD_GROUPS)]
        for g in range(SSD_GROUPS):
            bt_ref[0, c, g] = bt[g]
        bts.append(bt)
    wxbs = [jnp.dot(cols, exw_ref[:, SSD_WIDTH:], preferred_element_type=F32) for cols in colss]
    decs = [jnp.dot(cols[0:16], exe_ref[:, SSD_WIDTH:], preferred_element_type=F32)[0:1]
            for cols in colss]

    pc2 = jnp.dot(ub, w_ref[:, P_CONV + 6 * LANES:], preferred_element_type=F32)
    conv_slabs(pc1, 0)

    state = [st_ref[g] for g in range(SSD_GROUPS)]
    for c in reversed(range(NCH_PROJ)):
        xw = (xs[chunks[c]] * wxbs[c]).astype(BF16)
        for g in range(SSD_GROUPS):
            gs = slice(g * GROUP_COLS, (g + 1) * GROUP_COLS)
            sb_ref[0, c, g] = state[g].astype(BF16)
            local = jnp.dot(bts[c][g], xw[:, gs], preferred_element_type=F32)
            state[g] = state[g] * decs[c][:, gs] + local
    for g in range(SSD_GROUPS):
        st_ref[g] = state[g]
    conv_slabs(pc2, 2)


def _proj_call(x, mod, w_in, small, exe, exw, u_chunk, h0, w_out, w_ff1, w_ff2):
    n_steps = BATCH * NT_PROJ
    r1, r4 = D_MODEL // n_steps, D_FF // n_steps
    wrow3 = lambda b, t: (0, b * NT_PROJ + t, 0)
    wrow2 = lambda b, t: (b * NT_PROJ + t, 0)
    rev = lambda b, t: (b, NT_PROJ - 1 - t, 0)
    rev4 = lambda b, t: (b, NT_PROJ - 1 - t, 0, 0)
    rev5 = lambda b, t: (b, NT_PROJ - 1 - t, 0, 0, 0)
    tok = lambda width: pl.BlockSpec((1, TM_PROJ, width), rev)
    out_shape = [
        jax.ShapeDtypeStruct((BATCH, SEQ, D_MODEL), F32),
        jax.ShapeDtypeStruct((BATCH, SEQ, CONV_WIDTH), BF16),
        jax.ShapeDtypeStruct((BATCH, SEQ, SSD_WIDTH), BF16),
        jax.ShapeDtypeStruct((BATCH, SEQ, SSD_WIDTH), BF16),
        jax.ShapeDtypeStruct((BATCH, NCHUNK, SSD_GROUPS, SSD_STATE, SSD_CHUNK), BF16),
        jax.ShapeDtypeStruct((BATCH, SEQ, SSD_GN), BF16),
        jax.ShapeDtypeStruct((BATCH, NCHUNK, 2 * N_DH, SSD_CHUNK), F32),
        jax.ShapeDtypeStruct((BATCH, NCHUNK, SSD_CHUNK, LANES), BF16),
        jax.ShapeDtypeStruct((BATCH, NCHUNK, SSD_CHUNK, LANES), F32),
        jax.ShapeDtypeStruct((BATCH, NCHUNK, SSD_GROUPS, SSD_STATE, GROUP_COLS), BF16),
        jax.ShapeDtypeStruct((D_MODEL, D_MODEL), BF16),
        jax.ShapeDtypeStruct((D_MODEL, D_FF), BF16),
        jax.ShapeDtypeStruct((D_FF, D_MODEL), BF16),
    ]
    out_specs = [
        tok(D_MODEL), tok(CONV_WIDTH), tok(SSD_WIDTH), tok(SSD_WIDTH),
        pl.BlockSpec((1, NCH_PROJ, SSD_GROUPS, SSD_STATE, SSD_CHUNK), rev5),
        tok(SSD_GN),
        pl.BlockSpec((1, NCH_PROJ, 2 * N_DH, SSD_CHUNK), rev4),
        pl.BlockSpec((1, NCH_PROJ, SSD_CHUNK, LANES), rev4),
        pl.BlockSpec((1, NCH_PROJ, SSD_CHUNK, LANES), rev4),
        pl.BlockSpec((1, NCH_PROJ, SSD_GROUPS, SSD_STATE, GROUP_COLS), rev5),
        pl.BlockSpec((r1, D_MODEL), wrow2), pl.BlockSpec((r1, D_FF), wrow2), pl.BlockSpec((r4, D_MODEL), wrow2),
    ]
    in_specs = [
        pl.BlockSpec((1, TM_PROJ, D_MODEL), rev),
        _const_spec((MOD_ROWS, 6 * D_MODEL)),
        _const_spec((D_MODEL, IN_PAD)),
        _const_spec((SMALL_ROWS, D_MODEL)),
        _const_spec((LANES, 2 * SSD_WIDTH)), _const_spec((LANES, 2 * SSD_WIDTH)),
        _const_spec((SSD_CHUNK, 3 * SSD_CHUNK)),
        pl.BlockSpec((1, 1, SSD_GROUPS, SSD_STATE, GROUP_COLS), lambda b, t: (b, 1, 0, 0, 0)),
        pl.BlockSpec((1, r1, D_MODEL), wrow3), pl.BlockSpec((1, r1, D_FF), wrow3),
        pl.BlockSpec((1, r4, D_MODEL), wrow3),
    ]
    return pl.pallas_call(
        _proj_kernel,
        grid=(BATCH, NT_PROJ),
        in_specs=in_specs,
        out_specs=out_specs,
        out_shape=out_shape,
        scratch_shapes=[pltpu.VMEM((SSD_GROUPS, SSD_STATE, GROUP_COLS), F32)],
        compiler_params=pltpu.CompilerParams(dimension_semantics=("arbitrary", "arbitrary"),
                                             vmem_limit_bytes=VMEM_LIMIT),
        name="proj",
    )(x, mod, w_in, small, exe, exw, u_chunk, h0, w_out, w_ff1, w_ff2)


def _out_kernel(hln_ref, mod_ref, ycv_ref, zg_ref, xs_ref, bt_ref, cm_ref, rows_ref,
                cols_ref, colf_ref, sb_ref, h0_ref, exw_ref, small_ref, wout_ref,
                wff1_hbm, wff2_hbm,
                out_ref, st_ref, yn_ref, h1_ref, u2_ref, g2_ref, r2_ref, hid_ref, wff1_ref, wff2_ref, wsem):
    s = pl.program_id(0)
    wr = s % 2
    rd = (s + 1) % 2
    li = lax.broadcasted_iota(jnp.int32, (SSD_CHUNK, SSD_CHUNK), 0)
    si = lax.broadcasted_iota(jnp.int32, (SSD_CHUNK, SSD_CHUNK), 1)
    low = li >= si
    diag = li == si
    lo_half = si < SSD_HEADDIM

    def ssd_chunk(c, state, fill):
        tok = slice(c * SSD_CHUNK, (c + 1) * SSD_CHUNK)
        rows = rows_ref[0, c]
        cols = cols_ref[0, c]
        xs = xs_ref[0, tok, :]
        cm = cm_ref[0, tok, :]
        colf = colf_ref[0, c]
        bc = [jnp.broadcast_to(colf[:, COL_CSF + j:COL_CSF + j + 1], (SSD_CHUNK, LANES))
              for j in range(N_DH)]
        e1x = [jnp.concatenate(
            [jnp.exp(jnp.where(lo_half, bc[d * SSD_HEADS + 2 * k], bc[d * SSD_HEADS + 2 * k + 1]))
             for k in range(SSD_HEADS // 2)], axis=1) for d in range(N_DIRS)]
        wxf = jnp.dot(cols, exw_ref[:, :SSD_WIDTH], preferred_element_type=F32)
        dec = e1x[0][SSD_CHUNK - 1:SSD_CHUNK, :]
        gmat = [jnp.dot(cm[:, g * SSD_STATE:(g + 1) * SSD_STATE], bt_ref[0, c, g],
                        preferred_element_type=F32) for g in range(SSD_GROUPS)]
        fill[0]()

        ys = []
        for k in range(SSD_HEADS // 2):
            ms = []
            for h in (2 * k, 2 * k + 1):
                g = h // (SSD_HEADS // SSD_GROUPS)
                hb = SSD_HEADS + h
                arg = jnp.where(low,
                                bc[h] - rows[h:h + 1, :],
                                bc[hb] - rows[hb:hb + 1, :])
                decay_dt = jnp.where(diag, rows[N_DH + h:N_DH + h + 1, :], jnp.exp(arg))
                ms.append((gmat[g] * decay_dt).astype(BF16))
            xp = xs[:, k * LANES:(k + 1) * LANES]
            rhs = jnp.concatenate([jnp.where(lo_half, xp, jnp.zeros_like(xp)),
                                   jnp.where(lo_half, jnp.zeros_like(xp), xp)], axis=0)
            ys.append(jnp.dot(jnp.concatenate(ms, axis=1), rhs, preferred_element_type=F32))
            fill[k + 1]()
        y = jnp.concatenate(ys, axis=1)

        yf, yb = [], []
        for g in range(SSD_GROUPS):
            cg = cm[:, g * SSD_STATE:(g + 1) * SSD_STATE]
            yf.append(jnp.dot(cg, state[g].astype(BF16), preferred_element_type=F32))
            yb.append(jnp.dot(cg, sb_ref[0, c, g], preferred_element_type=F32))
        y = (y + jnp.concatenate(yf, axis=1) * e1x[0]
             + jnp.concatenate(yb, axis=1) * e1x[1]
             + xs.astype(F32) * small_ref[S_DXNW:S_DXNW + 1, :SSD_WIDTH])

        yg = y * zg_ref[0, tok, :].astype(F32)
        ms_ = jnp.mean(yg * yg, axis=-1, keepdims=True)
        yn_ref[tok, :CONV_WIDTH] = ycv_ref[0, tok, :]
        yn_ref[tok, CONV_WIDTH:] = (yg * lax.rsqrt(ms_ + RMS_EPS)
                                   * small_ref[S_DXNW:S_DXNW + 1, SSD_WIDTH:]).astype(BF16)
        fill[5]()

        xw = (xs.astype(F32) * wxf).astype(BF16)
        new_state = []
        for g in range(SSD_GROUPS):
            gs = slice(g * GROUP_COLS, (g + 1) * GROUP_COLS)
            local = jnp.dot(bt_ref[0, c, g], xw[:, gs], preferred_element_type=F32)
            new_state.append(state[g] * dec[:, gs] + local)
        return new_state

    def mlp_up(j, k):
        blk = slice(j * FF_BLK + k * FF_SUB, j * FF_BLK + (k + 1) * FF_SUB)
        hid = jnp.maximum(jnp.dot(u2_ref[...], wff1_ref[:, blk], preferred_element_type=F32), 0.0)
        hid_ref[:, blk] = (hid * hid).astype(BF16)

    def mlp_down(j, acc):
        blk = slice(j * FF_BLK, (j + 1) * FF_BLK)
        part = jnp.dot(hid_ref[:, blk], wff2_ref[blk, :], preferred_element_type=F32)
        return part if acc is None else acc + part

    def ln2_rows(i):
        r = slice(i * SSD_CHUNK, (i + 1) * SSD_CHUNK)
        gain, bias = _ln_params(small_ref, 2)
        out_ref[0, r, :] = _ln_hat(r2_ref[r, :]) * gain + bias

    def step(mixer, mlp, norm):
        state, acc = None, [None]
        pieces = []
        if mlp:
            def down(j):
                acc[0] = mlp_down(j, acc[0])
            for j in range(N_SLAB):
                pieces += [(1, functools.partial(mlp_up, j, k)) for k in range(FF_BLK // FF_SUB)]
            for j in range(N_SLAB - DOWN_KEPT):
                at = (j + 1) * (FF_BLK // FF_SUB) + j + DOWN_LAG
                pieces.insert(at, (FF_BLK // FF_SUB, functools.partial(down, j)))
        n_slots = NCH * FILL_SLOTS
        per_slot = sum(cost for cost, _ in pieces) / n_slots
        slots, issued = [], 0.0
        for i in range(n_slots):
            mine = []
            while pieces and issued < (i + 1) * per_slot:
                cost, fn = pieces.pop(0)
                issued += cost
                mine.append(fn)
            slots.append(lambda mine=mine: [fn() for fn in mine])
        if mixer:
            @pl.when(jnp.minimum(s, N_TILES - 1) % NT == 0)
            def _():
                st_ref[...] = h0_ref[0, 0]
            state = [st_ref[g] for g in range(SSD_GROUPS)]
        for c in range(NCH):
            fill = slots[c * FILL_SLOTS:(c + 1) * FILL_SLOTS]
            if mixer:
                state = ssd_chunk(c, state, fill)
            else:
                for f in fill:
                    f()
            if norm:
                ln2_rows(c)
        if mixer:
            for g in range(SSD_GROUPS):
                st_ref[g] = state[g]
            mix = jnp.dot(yn_ref[...], wout_ref[...], preferred_element_type=F32)
            m = _mod_vectors(mod_ref, jnp.minimum(s, N_TILES - 1) // NT)
            g2_ref[wr, 0:1, :] = m[5]
        for i in range(DOWN_KEPT):
            if mlp:
                down(N_SLAB - DOWN_KEPT + i)
            if mixer:
                for j in range(i * NCH // DOWN_KEPT, (i + 1) * NCH // DOWN_KEPT):
                    r = slice(j * SSD_CHUNK, (j + 1) * SSD_CHUNK)
                    gain, bias = _ln_params(small_ref, 1)
                    h1 = _ln_hat(ALPHA * hln_ref[0, r, :] + m[2] * mix[r]) * gain + bias
                    h1_ref[wr, r, :] = h1
                    u2_ref[r, :] = (h1 * (1.0 + m[4]) + m[3]).astype(BF16)
        if mlp:
            r2_ref[...] = ALPHA * h1_ref[rd] + g2_ref[rd, 0:1, :] * acc[0]

    mlp_weights = (pltpu.make_async_copy(wff1_hbm, wff1_ref, wsem.at[0]),
                   pltpu.make_async_copy(wff2_hbm, wff2_ref, wsem.at[1]))

    @pl.when(s == 0)
    def _():
        for copy in mlp_weights:
            copy.start()
        r2_ref[...] = jnp.zeros((TM, D_MODEL), F32)
        step(True, False, False)

    @pl.when(s == 1)
    def _():
        for copy in mlp_weights:
            copy.wait()

    @pl.when(jnp.logical_and(s >= 1, s < N_TILES))
    def _():
        step(True, True, True)

    @pl.when(s == N_TILES)
    def _():
        step(False, True, True)

    @pl.when(s == N_TILES + 1)
    def _():
        step(False, False, True)


def _out_call(hln, mod, ycv, zg, xs, bt, cm, rows, cols, colf, sb, h0, exw, small,
              wout, wff1, wff2):
    cur = lambda s: jnp.minimum(s, N_TILES - 1)
    fwd = lambda s: (cur(s) // NT, cur(s) % NT, 0)
    fwd4 = lambda s: (cur(s) // NT, cur(s) % NT, 0, 0)
    fwd5 = lambda s: (cur(s) // NT, cur(s) % NT, 0, 0, 0)
    done = lambda s: (jnp.maximum(s - 2, 0) // NT, jnp.maximum(s - 2, 0) % NT, 0)
    tok = lambda width: pl.BlockSpec((1, TM, width), fwd)
    in_specs = [
        tok(D_MODEL),
        _const_spec((MOD_ROWS, 6 * D_MODEL)),
        tok(CONV_WIDTH), tok(SSD_WIDTH), tok(SSD_WIDTH),
        pl.BlockSpec((1, NCH, SSD_GROUPS, SSD_STATE, SSD_CHUNK), fwd5),
        tok(SSD_GN),
        pl.BlockSpec((1, NCH, 2 * N_DH, SSD_CHUNK), fwd4),
        pl.BlockSpec((1, NCH, SSD_CHUNK, LANES), fwd4),
        pl.BlockSpec((1, NCH, SSD_CHUNK, LANES), fwd4),
        pl.BlockSpec((1, NCH, SSD_GROUPS, SSD_STATE, GROUP_COLS), fwd5),
        pl.BlockSpec((1, 1, SSD_GROUPS, SSD_STATE, GROUP_COLS), lambda s: (cur(s) // NT, 0, 0, 0, 0)),
        _const_spec((LANES, 2 * SSD_WIDTH)),
        _const_spec((SMALL_ROWS, D_MODEL)),
        _const_spec((D_MODEL, D_MODEL)),
        pl.BlockSpec(memory_space=pl.ANY), pl.BlockSpec(memory_space=pl.ANY),
    ]
    return pl.pallas_call(
        _out_kernel,
        grid=(N_TILES + 2,),
        in_specs=in_specs,
        out_specs=pl.BlockSpec((1, TM, D_MODEL), done),
        out_shape=jax.ShapeDtypeStruct((BATCH, SEQ, D_MODEL), F32),
        scratch_shapes=[pltpu.VMEM((SSD_GROUPS, SSD_STATE, GROUP_COLS), F32),
                        pltpu.VMEM((TM, D_MODEL), BF16),
                        pltpu.VMEM((2, TM, D_MODEL), F32),
                        pltpu.VMEM((TM, D_MODEL), BF16),
                        pltpu.VMEM((2, G2_ROWS, D_MODEL), F32),
                        pltpu.VMEM((TM, D_MODEL), F32),
                        pltpu.VMEM((TM, D_FF), BF16),
                        pltpu.VMEM((D_MODEL, D_FF), BF16),
                        pltpu.VMEM((D_FF, D_MODEL), BF16),
                        pltpu.SemaphoreType.DMA((2,))],
        compiler_params=pltpu.CompilerParams(dimension_semantics=("arbitrary",),
                                             vmem_limit_bytes=VMEM_LIMIT),
        name="out",
    )(hln, mod, ycv, zg, xs, bt, cm, rows, cols, colf, sb, h0, exw, small,
      wout, wff1, wff2)


def kernel(x, c, ctx, c_ctx, ln_in_g, ln_in_b, w_mod, b_mod, w_in, conv_w, ssd_conv_w, ssd_conv_b,
           dt_bias, a_log, ssd_d, ssd_norm_w, w_out, ln1_g, ln1_b, w_ff1, w_ff2, ln2_g, ln2_b):
    flat = lambda v: v.reshape(1, -1)
    mod, w_in_p, small = _prep_call(
        c, flat(c_ctx), w_mod, b_mod, w_in[0].T,
        (flat(ssd_conv_w), ssd_conv_b, flat(conv_w), dt_bias, a_log, ssd_d, ssd_norm_w,
         flat(ln_in_g), flat(ln_in_b), ln1_g, ln1_b, ln2_g, ln2_b))
    exe = jnp.asarray(_EXE, BF16)
    exw = jnp.asarray(_EXW, BF16)

    h0 = _ctx_call(ctx, mod, w_in_p, small, exw, jnp.asarray(_U_CTX, BF16))
    hln, ycv, zg, xs, bt, cm, rows, cols, colf, sb, w_out_p, w_ff1_p, w_ff2_p = _proj_call(
        x, mod, w_in_p, small, exe, exw,
        jnp.asarray(_U_CHUNK, BF16), h0, w_out, w_ff1, w_ff2)
    return _out_call(hln, mod, ycv, zg, xs, bt, cm, rows, cols, colf, sb, h0, exw,
                     small, w_out_p, w_ff1_p, w_ff2_p)
```

```python
import functools

import jax
import jax.numpy as jnp
import numpy as np
from jax import lax
from jax.experimental import pallas as pl
from jax.experimental.pallas import tpu as pltpu

F32 = jnp.float32
BF16 = jnp.bfloat16

D_MODEL = 1024
BATCH = 8
SEQ = 2048
CTX_LEN = 256
GRID_W = 64
CONV_WIDTH = 512
SSD_WIDTH = 512
SSD_HEADDIM = 64
SSD_HEADS = 8
SSD_GROUPS = 2
SSD_STATE = 128
SSD_CHUNK = 128
N_DIRS = 2
D_FF = 4 * D_MODEL
LN_EPS = 1e-5
RMS_EPS = 1e-5
SSD_GN = SSD_GROUPS * SSD_STATE
XBC_DIM = SSD_WIDTH + 2 * SSD_GN
Z_OFF = 3 * CONV_WIDTH
XBC_OFF = Z_OFF + SSD_WIDTH
DT_OFF = XBC_OFF + XBC_DIM
N_DH = N_DIRS * SSD_HEADS
LANES = 128
P_DT = 0
P_XBC = LANES
P_Z = P_XBC + XBC_DIM
P_CONV = P_Z + SSD_WIDTH
IN_PAD = P_CONV + 3 * CONV_WIDTH
CTX_PAD = P_XBC + SSD_WIDTH + SSD_GN
GROUP_COLS = (SSD_HEADS // SSD_GROUPS) * SSD_HEADDIM
ALPHA = 2.0 ** 0.25

TM = 512
CTX_ROWS = 2
MOD_ROWS = 16
SMALL_ROWS = 24
G2_ROWS = 8
PREP_BUFS = 3
PREP_STEPS = 8
NCH = TM // SSD_CHUNK
NT = SEQ // TM
NCHUNK = SEQ // SSD_CHUNK
N_TILES = BATCH * NT
TM_PROJ = 1024
NCH_PROJ = TM_PROJ // SSD_CHUNK
NT_PROJ = SEQ // TM_PROJ
FF_BLK = 1024
FF_SUB = 256
N_SLAB = D_FF // FF_BLK
FILL_SLOTS = 6
DOWN_KEPT = 2
DOWN_LAG = 1
VMEM_LIMIT = 58 * 1024 * 1024

COL_E1 = 48
COL_W = 80
COL_CSF = 112


def _expansion(col0, pieces, width):
    m = np.zeros((LANES, N_DH * width), np.float32)
    for t in range(pieces):
        for j in range(N_DH):
            m[col0 + 16 * t + j, j * width:(j + 1) * width] = 1.0
    return m


_EXE = _expansion(COL_E1, 2, SSD_HEADDIM)
_EXW = _expansion(COL_W, 2, SSD_HEADDIM)


def _ln_hat(x):
    mu = jnp.mean(x, axis=-1, keepdims=True)
    xc = x - mu
    var = jnp.mean(xc * xc, axis=-1, keepdims=True)
    return xc * lax.rsqrt(var + LN_EPS)


def _silu(x):
    return x / (1.0 + jnp.exp(-x))


def _softplus(x):
    return jnp.maximum(x, 0.0) + jnp.log1p(jnp.exp(-jnp.abs(x)))


def _edge_masks(rows, period):
    pos = lax.broadcasted_iota(jnp.int32, (rows, LANES), 0) % period
    return (pos != 0).astype(F32), (pos != period - 1).astype(F32)


def _conv3(t, w, mprev, mnext):
    rows = t.shape[0]
    prev = pltpu.roll(t, 1, 0) * mprev
    nxt = pltpu.roll(t, rows - 1, 0) * mnext
    return prev * w[0:1, :] + t * w[1:2, :] + nxt * w[2:3, :]


def _split(v, pieces):
    out = []
    for _ in range(pieces - 1):
        p = v.astype(BF16).astype(F32)
        out.append(p)
        v = v - p
    out.append(v.astype(BF16).astype(F32))
    return out


def _tri(length, op):
    i = np.arange(length)
    return op(i[:, None], i[None, :]).astype(np.float32)


_U_CHUNK = np.concatenate([_tri(SSD_CHUNK, np.less_equal), _tri(SSD_CHUNK, np.greater_equal),
                           np.ones((SSD_CHUNK, SSD_CHUNK), np.float32)], axis=1)
_U_CTX = np.concatenate([_tri(CTX_LEN, np.greater), _tri(CTX_LEN, np.less)], axis=1)


def _scan_mm(v, u_ref):
    pieces = jnp.concatenate(_split(v, 3), axis=0).astype(BF16)
    o = jnp.dot(pieces, u_ref[...], preferred_element_type=F32)
    return o[0:N_DH] + o[N_DH:2 * N_DH] + o[2 * N_DH:3 * N_DH]


def _dt_rows(raw, dtb, a_log):
    r = raw.T[0:N_DH, :] + dtb
    dt = _softplus(r)
    return dt, dt * (-jnp.exp(a_log))


def _fwd_rows():
    return lax.broadcasted_iota(jnp.int32, (N_DH, 1), 0) < SSD_HEADS


S_SCW = 0
S_SCB = 3
S_CW = 4
S_DXNW = 8
S_LN = 9
S_DTB = 16
S_ALOG = 18


def _dt_params(small_ref):
    sub = lax.broadcasted_iota(jnp.int32, (N_DH, LANES), 0)
    lane = lax.broadcasted_iota(jnp.int32, (N_DH, LANES), 1)
    head = jnp.where(sub < SSD_HEADS, sub, sub - SSD_HEADS)

    def pick(r):
        t = jnp.where(sub < SSD_HEADS, small_ref[r:r + 1, 0:LANES], small_ref[r + 1:r + 2, 0:LANES])
        return jnp.sum(jnp.where(lane == head, t, 0.0), axis=1, keepdims=True)
    return pick(S_DTB), pick(S_ALOG)


def _ln_params(small_ref, which):
    r = S_LN + 2 * which
    return small_ref[r:r + 1, :], small_ref[r + 1:r + 2, :]


def _mod_vectors(mod_ref, row):
    r = mod_ref[pl.ds(row, 1), :]
    return [r[:, k * D_MODEL:(k + 1) * D_MODEL] for k in range(6)]


def _const_spec(shape):
    nd = len(shape)
    return pl.BlockSpec(shape, lambda *_: (0,) * nd, pipeline_mode=pl.Buffered(1))


def _prep_copies(step, wmod_hbm, wint_hbm, wmod_buf, wint_buf, sem):
    r1, rm = D_MODEL // PREP_STEPS, 6 * D_MODEL // PREP_STEPS
    slot = step % PREP_BUFS
    col = lambda width: pl.ds(step * width if isinstance(step, int) else pl.multiple_of(step * width, LANES), width)
    return (pltpu.make_async_copy(wmod_hbm.at[0, :, col(rm)], wmod_buf.at[slot], sem.at[0, slot]),
            pltpu.make_async_copy(wint_hbm.at[:, col(r1)], wint_buf.at[slot], sem.at[1, slot]))


def _prep_kernel(c_ref, cctx_ref, wmod_hbm, bmod_ref, wint_hbm,
                 scw_ref, scb_ref, cw_ref, dtb_ref, alog_ref, d_ref, nw_ref,
                 lng_ref, lnb_ref, l1g_ref, l1b_ref, l2g_ref, l2b_ref,
                 mod_ref, pin_ref, small_ref, wmod_buf, wint_buf, sem):
    i = pl.program_id(0)
    ring = (wmod_hbm, wint_hbm, wmod_buf, wint_buf, sem)

    @pl.when(i == 0)
    def _():
        for step in range(PREP_BUFS - 1):
            for copy in _prep_copies(step, *ring):
                copy.start()

    @pl.when(i + PREP_BUFS - 1 < PREP_STEPS)
    def _():
        for copy in _prep_copies(i + PREP_BUFS - 1, *ring):
            copy.start()

    @pl.when(i == 0)
    def _():
        small_ref[...] = jnp.zeros((SMALL_ROWS, D_MODEL), F32)
        for k in range(3):
            small_ref[S_SCW + k:S_SCW + k + 1, :XBC_DIM] = scw_ref[:, k * XBC_DIM:(k + 1) * XBC_DIM]
            small_ref[S_CW + k:S_CW + k + 1, :CONV_WIDTH] = cw_ref[:, k * CONV_WIDTH:(k + 1) * CONV_WIDTH]
        small_ref[S_SCB:S_SCB + 1, :XBC_DIM] = scb_ref[...]
        small_ref[S_DTB:S_DTB + N_DIRS, :SSD_HEADS] = dtb_ref[0]
        small_ref[S_ALOG:S_ALOG + N_DIRS, :SSD_HEADS] = alog_ref[0]
        d = d_ref[...]
        first = lax.broadcasted_iota(jnp.int32, (1, LANES), 1) < SSD_HEADDIM
        for k in range(SSD_WIDTH // LANES):
            pair = [jnp.broadcast_to(d[:, 2 * k + i:2 * k + i + 1], (1, LANES)) for i in range(2)]
            small_ref[S_DXNW:S_DXNW + 1, k * LANES:(k + 1) * LANES] = jnp.where(first, pair[0], pair[1])
        small_ref[S_DXNW:S_DXNW + 1, SSD_WIDTH:] = nw_ref[...]
        for k, ref in enumerate((lng_ref, lnb_ref, l1g_ref, l1b_ref, l2g_ref, l2b_ref)):
            small_ref[S_LN + k:S_LN + k + 1, :] = ref[...]

    ctx_row = jnp.where(lax.broadcasted_iota(jnp.int32, (MOD_ROWS - BATCH, D_MODEL), 0) == 0, cctx_ref[...], 0.0)
    cs = _silu(jnp.concatenate([c_ref[...], ctx_row], axis=0)).astype(BF16)

    for copy in _prep_copies(i, wmod_hbm, wint_hbm, wmod_buf, wint_buf, sem):
        copy.wait()
    slot = i % PREP_BUFS
    mod_ref[...] = jnp.dot(cs, wmod_buf[slot].astype(BF16), preferred_element_type=F32) + bmod_ref[...]

    dt_rows = jnp.concatenate([wint_buf[slot, DT_OFF:DT_OFF + N_DH, :],
                               jnp.zeros((LANES - N_DH, wint_buf.shape[2]), F32)], axis=0)
    pin_ref[:, P_DT:P_DT + LANES] = dt_rows.T.astype(BF16)
    def move(dst, src):
        pin_ref[:, dst:dst + LANES] = wint_buf[slot, src:src + LANES, :].T.astype(BF16)

    for dst, src, width in ((P_XBC, XBC_OFF, XBC_DIM), (P_Z, Z_OFF, SSD_WIDTH)):
        for j in range(0, width, LANES):
            move(dst + j, src + j)
    for j in range(CONV_WIDTH // LANES):
        for k in range(3):
            move(P_CONV + (3 * j + k) * LANES, k * CONV_WIDTH + j * LANES)


def _prep_call(c, c_ctx, w_mod, b_mod, w_in_t, small_params):
    steps = PREP_STEPS
    r1, rm = D_MODEL // steps, 6 * D_MODEL // steps
    whole = lambda a: pl.BlockSpec(a.shape, lambda i: (0,) * a.ndim)
    return pl.pallas_call(
        _prep_kernel,
        grid=(steps,),
        in_specs=[whole(c), whole(c_ctx),
                  pl.BlockSpec(memory_space=pl.ANY),
                  pl.BlockSpec((1, rm), lambda i: (0, i)),
                  pl.BlockSpec(memory_space=pl.ANY)] + [whole(a) for a in small_params],
        out_specs=[pl.BlockSpec((MOD_ROWS, rm), lambda i: (0, i)),
                   pl.BlockSpec((r1, IN_PAD), lambda i: (i, 0)),
                   pl.BlockSpec((SMALL_ROWS, D_MODEL), lambda i: (0, 0))],
        out_shape=[jax.ShapeDtypeStruct((MOD_ROWS, 6 * D_MODEL), F32),
                   jax.ShapeDtypeStruct((D_MODEL, IN_PAD), BF16),
                   jax.ShapeDtypeStruct((SMALL_ROWS, D_MODEL), F32)],
        scratch_shapes=[pltpu.VMEM((PREP_BUFS, D_MODEL, rm), F32),
                        pltpu.VMEM((PREP_BUFS, w_in_t.shape[0], r1), F32),
                        pltpu.SemaphoreType.DMA((2, PREP_BUFS))],
        compiler_params=pltpu.CompilerParams(dimension_semantics=("arbitrary",),
                                             vmem_limit_bytes=VMEM_LIMIT),
        name="prep",
    )(c, c_ctx, w_mod, b_mod, w_in_t, *small_params)


def _ctx_kernel(x_ref, mod_ref, w_ref, small_ref, exw_ref, u_ref, h0_ref):
    dtb, alog = _dt_params(small_ref)
    lng, lnb = _ln_params(small_ref, 0)
    m = _mod_vectors(mod_ref, BATCH)
    sc = 1.0 + m[1]
    x = x_ref[...].reshape(CTX_ROWS * CTX_LEN, D_MODEL)
    u = _ln_hat(x) * (lng * sc) + (lnb * sc + m[0])
    proj = jnp.dot(u.astype(BF16), w_ref[...], preferred_element_type=F32)
    mprev, mnext = _edge_masks(CTX_ROWS * CTX_LEN, CTX_LEN)
    slabs = []
    for j in range((SSD_WIDTH + SSD_GN) // LANES):
        sl = slice(j * LANES, (j + 1) * LANES)
        pj = proj[:, P_XBC + j * LANES:P_XBC + (j + 1) * LANES]
        slabs.append(_silu(_conv3(pj, small_ref[S_SCW:S_SCW + 3, sl], mprev, mnext)
                           + small_ref[S_SCB:S_SCB + 1, sl]))
    zero = jnp.zeros((N_DH, CTX_LEN), F32)
    for i in range(CTX_ROWS):
        tok = slice(i * CTX_LEN, (i + 1) * CTX_LEN)
        xs = jnp.concatenate([sl_[tok] for sl_ in slabs[:4]], axis=1)
        dt, adt = _dt_rows(proj[tok, P_DT:P_DT + LANES], dtb, alog)
        sc2 = _scan_mm(adt, u_ref)
        excl = jnp.where(_fwd_rows(), sc2[:, :CTX_LEN], sc2[:, CTX_LEN:])
        w = jnp.exp(excl) * dt
        table = jnp.concatenate([zero] * (COL_W // 16) + _split(w, 2) + [zero], axis=0)
        cols = table.T.astype(BF16)
        wx = jnp.dot(cols, exw_ref[...], preferred_element_type=F32)
        for d in range(N_DIRS):
            xw = (xs * wx[:, d * SSD_WIDTH:(d + 1) * SSD_WIDTH]).astype(BF16)
            for g in range(SSD_GROUPS):
                bt = slabs[4 + g][tok].T.astype(BF16)
                h0_ref[i, d, g] = jnp.dot(bt, xw[:, g * GROUP_COLS:(g + 1) * GROUP_COLS],
                                          preferred_element_type=F32)


def _ctx_call(ctx, mod, w_in, small, exw, u_ctx):
    return pl.pallas_call(
        _ctx_kernel,
        grid=(BATCH // CTX_ROWS,),
        in_specs=[pl.BlockSpec((CTX_ROWS, CTX_LEN, D_MODEL), lambda b: (b, 0, 0)),
                  _const_spec((MOD_ROWS, 6 * D_MODEL)),
                  _const_spec((D_MODEL, CTX_PAD)), _const_spec((SMALL_ROWS, D_MODEL)),
                  _const_spec((LANES, 2 * SSD_WIDTH)),
                  _const_spec((CTX_LEN, 2 * CTX_LEN))],
        out_specs=pl.BlockSpec((CTX_ROWS, N_DIRS, SSD_GROUPS, SSD_STATE, GROUP_COLS),
                               lambda b: (b, 0, 0, 0, 0)),
        out_shape=jax.ShapeDtypeStruct((BATCH, N_DIRS, SSD_GROUPS, SSD_STATE, GROUP_COLS), F32),
        compiler_params=pltpu.CompilerParams(dimension_semantics=("arbitrary",),
                                             vmem_limit_bytes=VMEM_LIMIT),
        name="ctx",
    )(ctx, mod, w_in, small, exw, u_ctx)


def _proj_kernel(x_ref, mod_ref, w_ref, small_ref,
                 exe_ref, exw_ref, u_ref, h0_ref, wout_ref, wff1_ref, wff2_ref,
                 hln_ref, ycv_ref, zg_ref, xs_ref, bt_ref, cm_ref, rows_ref, cols_ref, colf_ref, sb_ref,
                 pout_ref, pff1_ref, pff2_ref, st_ref):
    pout_ref[...] = wout_ref[0].astype(BF16)
    pff1_ref[...] = wff1_ref[0].astype(BF16)
    pff2_ref[...] = wff2_ref[0].astype(BF16)

    @pl.when(pl.program_id(1) == 0)
    def _():
        st_ref[...] = h0_ref[0, 0]

    m = _mod_vectors(mod_ref, pl.program_id(0))
    sc = 1.0 + m[1]
    lng, lnb = _ln_params(small_ref, 0)
    scale, shift = lng * sc, lnb * sc + m[0]
    ubs, pas = [], []
    for r in range(2):
        rows = slice(r * (TM_PROJ // 2), (r + 1) * (TM_PROJ // 2))
        xhat = _ln_hat(x_ref[0, rows, :])
        hln_ref[0, rows, :] = xhat * lng + lnb
        ubs.append((xhat * scale + shift).astype(BF16))
        pas.append(jnp.dot(ubs[r], w_ref[:, :P_Z], preferred_element_type=F32))
    ub = jnp.concatenate(ubs, axis=0)
    pa = jnp.concatenate(pas, axis=0)
    mprev, mnext = _edge_masks(TM_PROJ, GRID_W)
    half = TM_PROJ // 2

    def xbc_half(r):
        out = []
        for j in range(XBC_DIM // LANES):
            sl = slice(j * LANES, (j + 1) * LANES)
            pj = pas[r][:, P_XBC + j * LANES:P_XBC + (j + 1) * LANES]
            out.append(_silu(_conv3(pj, small_ref[S_SCW:S_SCW + 3, sl], mprev[:half], mnext[:half])
                             + small_ref[S_SCB:S_SCB + 1, sl]))
        return out
    is_fwd = _fwd_rows()
    zero = jnp.zeros((N_DH, SSD_CHUNK), F32)
    chunks = [slice(c * SSD_CHUNK, (c + 1) * SSD_CHUNK) for c in range(NCH_PROJ)]

    def conv_slabs(p, first):
        for i in range(2):
            gb, gc, gh = (p[:, (3 * i + k) * LANES:(3 * i + k + 1) * LANES] for k in range(3))
            sl = slice((first + i) * LANES, (first + i + 1) * LANES)
            ycv_ref[0, :, sl] = (gb * _conv3(gc * gh, small_ref[S_CW:S_CW + 3, sl], mprev, mnext)).astype(BF16)

    dtb, alog = _dt_params(small_ref)
    slabs_lo = xbc_half(0)
    dts = [_dt_rows(pa[tok, P_DT:P_DT + LANES], dtb, alog) for tok in chunks]
    pz = jnp.dot(ub, w_ref[:, P_Z:P_CONV], preferred_element_type=F32)
    slabs_hi = xbc_half(1)
    slabs = [jnp.concatenate([lo, hi], axis=0) for lo, hi in zip(slabs_lo, slabs_hi)]

    colss = []
    for c, (dt, adt) in enumerate(dts):
        sc3 = _scan_mm(adt, u_ref)
        cs = jnp.where(is_fwd, sc3[:, :SSD_CHUNK], sc3[:, SSD_CHUNK:2 * SSD_CHUNK])
        tot = sc3[:, 2 * SSD_CHUNK:]
        e1 = jnp.exp(cs)
        w = jnp.exp(tot - cs) * dt
        dsum = dt[:SSD_HEADS] + dt[SSD_HEADS:]
        rows_ref[0, c] = jnp.concatenate([cs - jnp.log(dt), dsum, jnp.zeros_like(dsum)], axis=0)
        table = jnp.concatenate([zero] * 3 + _split(e1, 2) + _split(w, 2) + [cs], axis=0).T
        colf_ref[0, c] = table
        colss.append(table.astype(BF16))
        cols_ref[0, c] = colss[c]

    pc1 = jnp.dot(ub, w_ref[:, P_CONV:P_CONV + 6 * LANES], preferred_element_type=F32)

    xs = jnp.concatenate(slabs[:4], axis=1)
    xs_ref[0] = xs.astype(BF16)
    cm_ref[0] = jnp.concatenate(slabs[6:8], axis=1).astype(BF16)
    zg_ref[0] = _silu(pz).astype(BF16)
    bts = []
    for c, tok in enumerate(chunks):
        bt = [slabs[4 + g][tok].T.astype(BF16) for g in range(SSD_GROUPS)]
        for g in range(SSD_GROUPS):
            bt_ref[0, c, g] = bt[g]
        bts.append(bt)
    wxbs = [jnp.dot(cols, exw_ref[:, SSD_WIDTH:], preferred_element_type=F32) for cols in colss]
    decs = [jnp.dot(cols[0:16], exe_ref[:, SSD_WIDTH:], preferred_element_type=F32)[0:1]
            for cols in colss]

    pc2 = jnp.dot(ub, w_ref[:, P_CONV + 6 * LANES:], preferred_element_type=F32)
    conv_slabs(pc1, 0)

    state = [st_ref[g] for g in range(SSD_GROUPS)]
    for c in reversed(range(NCH_PROJ)):
        xw = (xs[chunks[c]] * wxbs[c]).astype(BF16)
        for g in range(SSD_GROUPS):
            gs = slice(g * GROUP_COLS, (g + 1) * GROUP_COLS)
            sb_ref[0, c, g] = state[g].astype(BF16)
            local = jnp.dot(bts[c][g], xw[:, gs], preferred_element_type=F32)
            state[g] = state[g] * decs[c][:, gs] + local
    for g in range(SSD_GROUPS):
        st_ref[g] = state[g]
    conv_slabs(pc2, 2)


def _proj_call(x, mod, w_in, small, exe, exw, u_chunk, h0, w_out, w_ff1, w_ff2):
    n_steps = BATCH * NT_PROJ
    r1, r4 = D_MODEL // n_steps, D_FF // n_steps
    wrow3 = lambda b, t: (0, b * NT_PROJ + t, 0)
    wrow2 = lambda b, t: (b * NT_PROJ + t, 0)
    rev = lambda b, t: (b, NT_PROJ - 1 - t, 0)
    rev4 = lambda b, t: (b, NT_PROJ - 1 - t, 0, 0)
    rev5 = lambda b, t: (b, NT_PROJ - 1 - t, 0, 0, 0)
    tok = lambda width: pl.BlockSpec((1, TM_PROJ, width), rev)
    out_shape = [
        jax.ShapeDtypeStruct((BATCH, SEQ, D_MODEL), F32),
        jax.ShapeDtypeStruct((BATCH, SEQ, CONV_WIDTH), BF16),
        jax.ShapeDtypeStruct((BATCH, SEQ, SSD_WIDTH), BF16),
        jax.ShapeDtypeStruct((BATCH, SEQ, SSD_WIDTH), BF16),
        jax.ShapeDtypeStruct((BATCH, NCHUNK, SSD_GROUPS, SSD_STATE, SSD_CHUNK), BF16),
        jax.ShapeDtypeStruct((BATCH, SEQ, SSD_GN), BF16),
        jax.ShapeDtypeStruct((BATCH, NCHUNK, 2 * N_DH, SSD_CHUNK), F32),
        jax.ShapeDtypeStruct((BATCH, NCHUNK, SSD_CHUNK, LANES), BF16),
        jax.ShapeDtypeStruct((BATCH, NCHUNK, SSD_CHUNK, LANES), F32),
        jax.ShapeDtypeStruct((BATCH, NCHUNK, SSD_GROUPS, SSD_STATE, GROUP_COLS), BF16),
        jax.ShapeDtypeStruct((D_MODEL, D_MODEL), BF16),
        jax.ShapeDtypeStruct((D_MODEL, D_FF), BF16),
        jax.ShapeDtypeStruct((D_FF, D_MODEL), BF16),
    ]
    out_specs = [
        tok(D_MODEL), tok(CONV_WIDTH), tok(SSD_WIDTH), tok(SSD_WIDTH),
        pl.BlockSpec((1, NCH_PROJ, SSD_GROUPS, SSD_STATE, SSD_CHUNK), rev5),
        tok(SSD_GN),
        pl.BlockSpec((1, NCH_PROJ, 2 * N_DH, SSD_CHUNK), rev4),
        pl.BlockSpec((1, NCH_PROJ, SSD_CHUNK, LANES), rev4),
        pl.BlockSpec((1, NCH_PROJ, SSD_CHUNK, LANES), rev4),
        pl.BlockSpec((1, NCH_PROJ, SSD_GROUPS, SSD_STATE, GROUP_COLS), rev5),
        pl.BlockSpec((r1, D_MODEL), wrow2), pl.BlockSpec((r1, D_FF), wrow2), pl.BlockSpec((r4, D_MODEL), wrow2),
    ]
    in_specs = [
        pl.BlockSpec((1, TM_PROJ, D_MODEL), rev),
        _const_spec((MOD_ROWS, 6 * D_MODEL)),
        _const_spec((D_MODEL, IN_PAD)),
        _const_spec((SMALL_ROWS, D_MODEL)),
        _const_spec((LANES, 2 * SSD_WIDTH)), _const_spec((LANES, 2 * SSD_WIDTH)),
        _const_spec((SSD_CHUNK, 3 * SSD_CHUNK)),
        pl.BlockSpec((1, 1, SSD_GROUPS, SSD_STATE, GROUP_COLS), lambda b, t: (b, 1, 0, 0, 0)),
        pl.BlockSpec((1, r1, D_MODEL), wrow3), pl.BlockSpec((1, r1, D_FF), wrow3),
        pl.BlockSpec((1, r4, D_MODEL), wrow3),
    ]
    return pl.pallas_call(
        _proj_kernel,
        grid=(BATCH, NT_PROJ),
        in_specs=in_specs,
        out_specs=out_specs,
        out_shape=out_shape,
        scratch_shapes=[pltpu.VMEM((SSD_GROUPS, SSD_STATE, GROUP_COLS), F32)],
        compiler_params=pltpu.CompilerParams(dimension_semantics=("arbitrary", "arbitrary"),
                                             vmem_limit_bytes=VMEM_LIMIT),
        name="proj",
    )(x, mod, w_in, small, exe, exw, u_chunk, h0, w_out, w_ff1, w_ff2)


def _out_kernel(hln_ref, mod_ref, ycv_ref, zg_ref, xs_ref, bt_ref, cm_ref, rows_ref,
                cols_ref, colf_ref, sb_ref, h0_ref, exw_ref, small_ref, wout_ref,
                wff1_hbm, wff2_hbm,
                out_ref, st_ref, yn_ref, h1_ref, u2_ref, g2_ref, r2_ref, hid_ref, wff1_ref, wff2_ref, wsem):
    s = pl.program_id(0)
    wr = s % 2
    rd = (s + 1) % 2
    li = lax.broadcasted_iota(jnp.int32, (SSD_CHUNK, SSD_CHUNK), 0)
    si = lax.broadcasted_iota(jnp.int32, (SSD_CHUNK, SSD_CHUNK), 1)
    low = li >= si
    diag = li == si
    lo_half = si < SSD_HEADDIM

    def ssd_chunk(c, state, fill):
        tok = slice(c * SSD_CHUNK, (c + 1) * SSD_CHUNK)
        rows = rows_ref[0, c]
        cols = cols_ref[0, c]
        xs = xs_ref[0, tok, :]
        cm = cm_ref[0, tok, :]
        colf = colf_ref[0, c]
        bc = [jnp.broadcast_to(colf[:, COL_CSF + j:COL_CSF + j + 1], (SSD_CHUNK, LANES))
              for j in range(N_DH)]
        e1x = [jnp.concatenate(
            [jnp.exp(jnp.where(lo_half, bc[d * SSD_HEADS + 2 * k], bc[d * SSD_HEADS + 2 * k + 1]))
             for k in range(SSD_HEADS // 2)], axis=1) for d in range(N_DIRS)]
        wxf = jnp.dot(cols, exw_ref[:, :SSD_WIDTH], preferred_element_type=F32)
        dec = e1x[0][SSD_CHUNK - 1:SSD_CHUNK, :]
        gmat = [jnp.dot(cm[:, g * SSD_STATE:(g + 1) * SSD_STATE], bt_ref[0, c, g],
                        preferred_element_type=F32) for g in range(SSD_GROUPS)]
        fill[0]()

        ys = []
        for k in range(SSD_HEADS // 2):
            ms = []
            for h in (2 * k, 2 * k + 1):
                g = h // (SSD_HEADS // SSD_GROUPS)
                hb = SSD_HEADS + h
                arg = jnp.where(low,
                                bc[h] - rows[h:h + 1, :],
                                bc[hb] - rows[hb:hb + 1, :])
                decay_dt = jnp.where(diag, rows[N_DH + h:N_DH + h + 1, :], jnp.exp(arg))
                ms.append((gmat[g] * decay_dt).astype(BF16))
            xp = xs[:, k * LANES:(k + 1) * LANES]
            rhs = jnp.concatenate([jnp.where(lo_half, xp, jnp.zeros_like(xp)),
                                   jnp.where(lo_half, jnp.zeros_like(xp), xp)], axis=0)
            ys.append(jnp.dot(jnp.concatenate(ms, axis=1), rhs, preferred_element_type=F32))
            fill[k + 1]()
        y = jnp.concatenate(ys, axis=1)

        yf, yb = [], []
        for g in range(SSD_GROUPS):
            cg = cm[:, g * SSD_STATE:(g + 1) * SSD_STATE]
            yf.append(jnp.dot(cg, state[g].astype(BF16), preferred_element_type=F32))
            yb.append(jnp.dot(cg, sb_ref[0, c, g], preferred_element_type=F32))
        y = (y + jnp.concatenate(yf, axis=1) * e1x[0]
             + jnp.concatenate(yb, axis=1) * e1x[1]
             + xs.astype(F32) * small_ref[S_DXNW:S_DXNW + 1, :SSD_WIDTH])

        yg = y * zg_ref[0, tok, :].astype(F32)
        ms_ = jnp.mean(yg * yg, axis=-1, keepdims=True)
        yn_ref[tok, :CONV_WIDTH] = ycv_ref[0, tok, :]
        yn_ref[tok, CONV_WIDTH:] = (yg * lax.rsqrt(ms_ + RMS_EPS)
                                   * small_ref[S_DXNW:S_DXNW + 1, SSD_WIDTH:]).astype(BF16)
        fill[5]()

        xw = (xs.astype(F32) * wxf).astype(BF16)
        new_state = []
        for g in range(SSD_GROUPS):
            gs = slice(g * GROUP_COLS, (g + 1) * GROUP_COLS)
            local = jnp.dot(bt_ref[0, c, g], xw[:, gs], preferred_element_type=F32)
            new_state.append(state[g] * dec[:, gs] + local)
        return new_state

    def mlp_up(j, k):
        blk = slice(j * FF_BLK + k * FF_SUB, j * FF_BLK + (k + 1) * FF_SUB)
        hid = jnp.maximum(jnp.dot(u2_ref[...], wff1_ref[:, blk], preferred_element_type=F32), 0.0)
        hid_ref[:, blk] = (hid * hid).astype(BF16)

    def mlp_down(j, acc):
        blk = slice(j * FF_BLK, (j + 1) * FF_BLK)
        part = jnp.dot(hid_ref[:, blk], wff2_ref[blk, :], preferred_element_type=F32)
        return part if acc is None else acc + part

    def ln2_rows(i):
        r = slice(i * SSD_CHUNK, (i + 1) * SSD_CHUNK)
        gain, bias = _ln_params(small_ref, 2)
        out_ref[0, r, :] = _ln_hat(r2_ref[r, :]) * gain + bias

    def step(mixer, mlp, norm):
        state, acc = None, [None]
        pieces = []
        if mlp:
            def down(j):
                acc[0] = mlp_down(j, acc[0])
            for j in range(N_SLAB):
                pieces += [(1, functools.partial(mlp_up, j, k)) for k in range(FF_BLK // FF_SUB)]
            for j in range(N_SLAB - DOWN_KEPT):
                at = (j + 1) * (FF_BLK // FF_SUB) + j + DOWN_LAG
                pieces.insert(at, (FF_BLK // FF_SUB, functools.partial(down, j)))
        n_slots = NCH * FILL_SLOTS
        per_slot = sum(cost for cost, _ in pieces) / n_slots
        slots, issued = [], 0.0
        for i in range(n_slots):
            mine = []
            while pieces and issued < (i + 1) * per_slot:
                cost, fn = pieces.pop(0)
                issued += cost
                mine.append(fn)
            slots.append(lambda mine=mine: [fn() for fn in mine])
        if mixer:
            @pl.when(jnp.minimum(s, N_TILES - 1) % NT == 0)
            def _():
                st_ref[...] = h0_ref[0, 0]
            state = [st_ref[g] for g in range(SSD_GROUPS)]
        for c in range(NCH):
            fill = slots[c * FILL_SLOTS:(c + 1) * FILL_SLOTS]
            if mixer:
                state = ssd_chunk(c, state, fill)
            else:
                for f in fill:
                    f()
            if norm:
                ln2_rows(c)
        if mixer:
            for g in range(SSD_GROUPS):
                st_ref[g] = state[g]
            mix = jnp.dot(yn_ref[...], wout_ref[...], preferred_element_type=F32)
            m = _mod_vectors(mod_ref, jnp.minimum(s, N_TILES - 1) // NT)
            g2_ref[wr, 0:1, :] = m[5]
        for i in range(DOWN_KEPT):
            if mlp:
                down(N_SLAB - DOWN_KEPT + i)
            if mixer:
                for j in range(i * NCH // DOWN_KEPT, (i + 1) * NCH // DOWN_KEPT):
                    r = slice(j * SSD_CHUNK, (j + 1) * SSD_CHUNK)
                    gain, bias = _ln_params(small_ref, 1)
                    h1 = _ln_hat(ALPHA * hln_ref[0, r, :] + m[2] * mix[r]) * gain + bias
                    h1_ref[wr, r, :] = h1
                    u2_ref[r, :] = (h1 * (1.0 + m[4]) + m[3]).astype(BF16)
        if mlp:
            r2_ref[...] = ALPHA * h1_ref[rd] + g2_ref[rd, 0:1, :] * acc[0]

    mlp_weights = (pltpu.make_async_copy(wff1_hbm, wff1_ref, wsem.at[0]),
                   pltpu.make_async_copy(wff2_hbm, wff2_ref, wsem.at[1]))

    @pl.when(s == 0)
    def _():
        for copy in mlp_weights:
            copy.start()
        r2_ref[...] = jnp.zeros((TM, D_MODEL), F32)
        step(True, False, False)

    @pl.when(s == 1)
    def _():
        for copy in mlp_weights:
            copy.wait()

    @pl.when(jnp.logical_and(s >= 1, s < N_TILES))
    def _():
        step(True, True, True)

    @pl.when(s == N_TILES)
    def _():
        step(False, True, True)

    @pl.when(s == N_TILES + 1)
    def _():
        step(False, False, True)


def _out_call(hln, mod, ycv, zg, xs, bt, cm, rows, cols, colf, sb, h0, exw, small,
              wout, wff1, wff2):
    cur = lambda s: jnp.minimum(s, N_TILES - 1)
    fwd = lambda s: (cur(s) // NT, cur(s) % NT, 0)
    fwd4 = lambda s: (cur(s) // NT, cur(s) % NT, 0, 0)
    fwd5 = lambda s: (cur(s) // NT, cur(s) % NT, 0, 0, 0)
    done = lambda s: (jnp.maximum(s - 2, 0) // NT, jnp.maximum(s - 2, 0) % NT, 0)
    tok = lambda width: pl.BlockSpec((1, TM, width), fwd)
    in_specs = [
        tok(D_MODEL),
        _const_spec((MOD_ROWS, 6 * D_MODEL)),
        tok(CONV_WIDTH), tok(SSD_WIDTH), tok(SSD_WIDTH),
        pl.BlockSpec((1, NCH, SSD_GROUPS, SSD_STATE, SSD_CHUNK), fwd5),
        tok(SSD_GN),
        pl.BlockSpec((1, NCH, 2 * N_DH, SSD_CHUNK), fwd4),
        pl.BlockSpec((1, NCH, SSD_CHUNK, LANES), fwd4),
        pl.BlockSpec((1, NCH, SSD_CHUNK, LANES), fwd4),
        pl.BlockSpec((1, NCH, SSD_GROUPS, SSD_STATE, GROUP_COLS), fwd5),
        pl.BlockSpec((1, 1, SSD_GROUPS, SSD_STATE, GROUP_COLS), lambda s: (cur(s) // NT, 0, 0, 0, 0)),
        _const_spec((LANES, 2 * SSD_WIDTH)),
        _const_spec((SMALL_ROWS, D_MODEL)),
        _const_spec((D_MODEL, D_MODEL)),
        pl.BlockSpec(memory_space=pl.ANY), pl.BlockSpec(memory_space=pl.ANY),
    ]
    return pl.pallas_call(
        _out_kernel,
        grid=(N_TILES + 2,),
        in_specs=in_specs,
        out_specs=pl.BlockSpec((1, TM, D_MODEL), done),
        out_shape=jax.ShapeDtypeStruct((BATCH, SEQ, D_MODEL), F32),
        scratch_shapes=[pltpu.VMEM((SSD_GROUPS, SSD_STATE, GROUP_COLS), F32),
                        pltpu.VMEM((TM, D_MODEL), BF16),
                        pltpu.VMEM((2, TM, D_MODEL), F32),
                        pltpu.VMEM((TM, D_MODEL), BF16),
                        pltpu.VMEM((2, G2_ROWS, D_MODEL), F32),
                        pltpu.VMEM((TM, D_MODEL), F32),
                        pltpu.VMEM((TM, D_FF), BF16),
                        pltpu.VMEM((D_MODEL, D_FF), BF16),
                        pltpu.VMEM((D_FF, D_MODEL), BF16),
                        pltpu.SemaphoreType.DMA((2,))],
        compiler_params=pltpu.CompilerParams(dimension_semantics=("arbitrary",),
                                             vmem_limit_bytes=VMEM_LIMIT),
        name="out",
    )(hln, mod, ycv, zg, xs, bt, cm, rows, cols, colf, sb, h0, exw, small,
      wout, wff1, wff2)


def kernel(x, c, ctx, c_ctx, ln_in_g, ln_in_b, w_mod, b_mod, w_in, conv_w, ssd_conv_w, ssd_conv_b,
           dt_bias, a_log, ssd_d, ssd_norm_w, w_out, ln1_g, ln1_b, w_ff1, w_ff2, ln2_g, ln2_b):
    flat = lambda v: v.reshape(1, -1)
    mod, w_in_p, small = _prep_call(
        c, flat(c_ctx), w_mod, b_mod, w_in[0].T,
        (flat(ssd_conv_w), ssd_conv_b, flat(conv_w), dt_bias, a_log, ssd_d, ssd_norm_w,
         flat(ln_in_g), flat(ln_in_b), ln1_g, ln1_b, ln2_g, ln2_b))
    exe = jnp.asarray(_EXE, BF16)
    exw = jnp.asarray(_EXW, BF16)

    h0 = _ctx_call(ctx, mod, w_in_p, small, exw, jnp.asarray(_U_CTX, BF16))
    hln, ycv, zg, xs, bt, cm, rows, cols, colf, sb, w_out_p, w_ff1_p, w_ff2_p = _proj_call(
        x, mod, w_in_p, small, exe, exw,
        jnp.asarray(_U_CHUNK, BF16), h0, w_out, w_ff1, w_ff2)
    return _out_call(hln, mod, ycv, zg, xs, bt, cm, rows, cols, colf, sb, h0, exw,
                     small, w_out_p, w_ff1_p, w_ff2_p)
```

```python
import functools

import jax
import jax.numpy as jnp
import numpy as np
from jax import lax
from jax.experimental import pallas as pl
from jax.experimental.pallas import tpu as pltpu

F32 = jnp.float32
BF16 = jnp.bfloat16

D_MODEL = 1024
BATCH = 8
SEQ = 2048
CTX_LEN = 256
GRID_W = 64
CONV_WIDTH = 512
SSD_WIDTH = 512
SSD_HEADDIM = 64
SSD_HEADS = 8
SSD_GROUPS = 2
SSD_STATE = 128
SSD_CHUNK = 128
N_DIRS = 2
D_FF = 4 * D_MODEL
LN_EPS = 1e-5
RMS_EPS = 1e-5
SSD_GN = SSD_GROUPS * SSD_STATE
XBC_DIM = SSD_WIDTH + 2 * SSD_GN
Z_OFF = 3 * CONV_WIDTH
XBC_OFF = Z_OFF + SSD_WIDTH
DT_OFF = XBC_OFF + XBC_DIM
N_DH = N_DIRS * SSD_HEADS
LANES = 128
P_DT = 0
P_XBC = LANES
P_Z = P_XBC + XBC_DIM
P_CONV = P_Z + SSD_WIDTH
IN_PAD = P_CONV + 3 * CONV_WIDTH
CTX_PAD = P_XBC + SSD_WIDTH + SSD_GN
GROUP_COLS = (SSD_HEADS // SSD_GROUPS) * SSD_HEADDIM
ALPHA = 2.0 ** 0.25

TM = 512
CTX_ROWS = 2
MOD_ROWS = 16
SMALL_ROWS = 24
G2_ROWS = 8
PREP_BUFS = 4
PREP_STEPS = 8
NCH = TM // SSD_CHUNK
NT = SEQ // TM
NCHUNK = SEQ // SSD_CHUNK
N_TILES = BATCH * NT
TM_PROJ = 1024
NCH_PROJ = TM_PROJ // SSD_CHUNK
NT_PROJ = SEQ // TM_PROJ
FF_BLK = 1024
FF_SUB = 256
N_SLAB = D_FF // FF_BLK
FILL_SLOTS = 6
DOWN_KEPT = 2
DOWN_LAG = 1
VMEM_LIMIT = 58 * 1024 * 1024

COL_E1 = 48
COL_W = 80
COL_CSF = 112


def _expansion(col0, pieces, width):
    m = np.zeros((LANES, N_DH * width), np.float32)
    for t in range(pieces):
        for j in range(N_DH):
            m[col0 + 16 * t + j, j * width:(j + 1) * width] = 1.0
    return m


_EXE = _expansion(COL_E1, 2, SSD_HEADDIM)
_EXW = _expansion(COL_W, 2, SSD_HEADDIM)


def _ln_hat(x):
    mu = jnp.mean(x, axis=-1, keepdims=True)
    xc = x - mu
    var = jnp.mean(xc * xc, axis=-1, keepdims=True)
    return xc * lax.rsqrt(var + LN_EPS)


def _silu(x):
    return x / (1.0 + jnp.exp(-x))


def _softplus(x):
    return jnp.maximum(x, 0.0) + jnp.log1p(jnp.exp(-jnp.abs(x)))


def _edge_masks(rows, period):
    pos = lax.broadcasted_iota(jnp.int32, (rows, LANES), 0) % period
    return (pos != 0).astype(F32), (pos != period - 1).astype(F32)


def _conv3(t, w, mprev, mnext):
    rows = t.shape[0]
    prev = pltpu.roll(t, 1, 0) * mprev
    nxt = pltpu.roll(t, rows - 1, 0) * mnext
    return prev * w[0:1, :] + t * w[1:2, :] + nxt * w[2:3, :]


def _split(v, pieces):
    out = []
    for _ in range(pieces - 1):
        p = v.astype(BF16).astype(F32)
        out.append(p)
        v = v - p
    out.append(v.astype(BF16).astype(F32))
    return out


def _tri(length, op):
    i = np.arange(length)
    return op(i[:, None], i[None, :]).astype(np.float32)


_U_CHUNK = np.concatenate([_tri(SSD_CHUNK, np.less_equal), _tri(SSD_CHUNK, np.greater_equal),
                           np.ones((SSD_CHUNK, SSD_CHUNK), np.float32)], axis=1)
_U_CTX = np.concatenate([_tri(CTX_LEN, np.greater), _tri(CTX_LEN, np.less)], axis=1)


def _scan_mm(v, u_ref):
    pieces = jnp.concatenate(_split(v, 3), axis=0).astype(BF16)
    o = jnp.dot(pieces, u_ref[...], preferred_element_type=F32)
    return o[0:N_DH] + o[N_DH:2 * N_DH] + o[2 * N_DH:3 * N_DH]


def _dt_rows(raw, dtb, a_log):
    r = raw.T[0:N_DH, :] + dtb
    dt = _softplus(r)
    return dt, dt * (-jnp.exp(a_log))


def _fwd_rows():
    return lax.broadcasted_iota(jnp.int32, (N_DH, 1), 0) < SSD_HEADS


S_SCW = 0
S_SCB = 3
S_CW = 4
S_DXNW = 8
S_LN = 9
S_DTB = 16
S_ALOG = 18


def _dt_params(small_ref):
    sub = lax.broadcasted_iota(jnp.int32, (N_DH, LANES), 0)
    lane = lax.broadcasted_iota(jnp.int32, (N_DH, LANES), 1)
    head = jnp.where(sub < SSD_HEADS, sub, sub - SSD_HEADS)

    def pick(r):
        t = jnp.where(sub < SSD_HEADS, small_ref[r:r + 1, 0:LANES], small_ref[r + 1:r + 2, 0:LANES])
        return jnp.sum(jnp.where(lane == head, t, 0.0), axis=1, keepdims=True)
    return pick(S_DTB), pick(S_ALOG)


def _ln_params(small_ref, which):
    r = S_LN + 2 * which
    return small_ref[r:r + 1, :], small_ref[r + 1:r + 2, :]


def _mod_vectors(mod_ref, row):
    r = mod_ref[pl.ds(row, 1), :]
    return [r[:, k * D_MODEL:(k + 1) * D_MODEL] for k in range(6)]


def _const_spec(shape):
    nd = len(shape)
    return pl.BlockSpec(shape, lambda *_: (0,) * nd, pipeline_mode=pl.Buffered(1))


def _prep_copies(step, wmod_hbm, wint_hbm, wmod_buf, wint_buf, sem):
    r1, rm = D_MODEL // PREP_STEPS, 6 * D_MODEL // PREP_STEPS
    slot = step % PREP_BUFS
    col = lambda width: pl.ds(step * width if isinstance(step, int) else pl.multiple_of(step * width, LANES), width)
    return (pltpu.make_async_copy(wmod_hbm.at[0, :, col(rm)], wmod_buf.at[slot], sem.at[0, slot]),
            pltpu.make_async_copy(wint_hbm.at[:, col(r1)], wint_buf.at[slot], sem.at[1, slot]))


def _prep_kernel(c_ref, cctx_ref, wmod_hbm, bmod_ref, wint_hbm,
                 scw_ref, scb_ref, cw_ref, dtb_ref, alog_ref, d_ref, nw_ref,
                 lng_ref, lnb_ref, l1g_ref, l1b_ref, l2g_ref, l2b_ref,
                 mod_ref, pin_ref, small_ref, wmod_buf, wint_buf, sem):
    i = pl.program_id(0)
    ring = (wmod_hbm, wint_hbm, wmod_buf, wint_buf, sem)

    @pl.when(i == 0)
    def _():
        for step in range(PREP_BUFS - 1):
            for copy in _prep_copies(step, *ring):
                copy.start()

    @pl.when(i + PREP_BUFS - 1 < PREP_STEPS)
    def _():
        for copy in _prep_copies(i + PREP_BUFS - 1, *ring):
            copy.start()

    @pl.when(i == 0)
    def _():
        small_ref[...] = jnp.zeros((SMALL_ROWS, D_MODEL), F32)
        for k in range(3):
            small_ref[S_SCW + k:S_SCW + k + 1, :XBC_DIM] = scw_ref[:, k * XBC_DIM:(k + 1) * XBC_DIM]
            small_ref[S_CW + k:S_CW + k + 1, :CONV_WIDTH] = cw_ref[:, k * CONV_WIDTH:(k + 1) * CONV_WIDTH]
        small_ref[S_SCB:S_SCB + 1, :XBC_DIM] = scb_ref[...]
        small_ref[S_DTB:S_DTB + N_DIRS, :SSD_HEADS] = dtb_ref[0]
        small_ref[S_ALOG:S_ALOG + N_DIRS, :SSD_HEADS] = alog_ref[0]
        d = d_ref[...]
        first = lax.broadcasted_iota(jnp.int32, (1, LANES), 1) < SSD_HEADDIM
        for k in range(SSD_WIDTH // LANES):
            pair = [jnp.broadcast_to(d[:, 2 * k + i:2 * k + i + 1], (1, LANES)) for i in range(2)]
            small_ref[S_DXNW:S_DXNW + 1, k * LANES:(k + 1) * LANES] = jnp.where(first, pair[0], pair[1])
        small_ref[S_DXNW:S_DXNW + 1, SSD_WIDTH:] = nw_ref[...]
        for k, ref in enumerate((lng_ref, lnb_ref, l1g_ref, l1b_ref, l2g_ref, l2b_ref)):
            small_ref[S_LN + k:S_LN + k + 1, :] = ref[...]

    ctx_row = jnp.where(lax.broadcasted_iota(jnp.int32, (MOD_ROWS - BATCH, D_MODEL), 0) == 0, cctx_ref[...], 0.0)
    cs = _silu(jnp.concatenate([c_ref[...], ctx_row], axis=0)).astype(BF16)

    for copy in _prep_copies(i, wmod_hbm, wint_hbm, wmod_buf, wint_buf, sem):
        copy.wait()
    slot = i % PREP_BUFS
    mod_ref[...] = jnp.dot(cs, wmod_buf[slot].astype(BF16), preferred_element_type=F32) + bmod_ref[...]

    dt_rows = jnp.concatenate([wint_buf[slot, DT_OFF:DT_OFF + N_DH, :],
                               jnp.zeros((LANES - N_DH, wint_buf.shape[2]), F32)], axis=0)
    pin_ref[:, P_DT:P_DT + LANES] = dt_rows.T.astype(BF16)
    def move(dst, src):
        pin_ref[:, dst:dst + LANES] = wint_buf[slot, src:src + LANES, :].T.astype(BF16)

    for dst, src, width in ((P_XBC, XBC_OFF, XBC_DIM), (P_Z, Z_OFF, SSD_WIDTH)):
        for j in range(0, width, LANES):
            move(dst + j, src + j)
    for j in range(CONV_WIDTH // LANES):
        for k in range(3):
            move(P_CONV + (3 * j + k) * LANES, k * CONV_WIDTH + j * LANES)


def _prep_call(c, c_ctx, w_mod, b_mod, w_in_t, small_params):
    steps = PREP_STEPS
    r1, rm = D_MODEL // steps, 6 * D_MODEL // steps
    whole = lambda a: pl.BlockSpec(a.shape, lambda i: (0,) * a.ndim)
    return pl.pallas_call(
        _prep_kernel,
        grid=(steps,),
        in_specs=[whole(c), whole(c_ctx),
                  pl.BlockSpec(memory_space=pl.ANY),
                  pl.BlockSpec((1, rm), lambda i: (0, i)),
                  pl.BlockSpec(memory_space=pl.ANY)] + [whole(a) for a in small_params],
        out_specs=[pl.BlockSpec((MOD_ROWS, rm), lambda i: (0, i)),
                   pl.BlockSpec((r1, IN_PAD), lambda i: (i, 0)),
                   pl.BlockSpec((SMALL_ROWS, D_MODEL), lambda i: (0, 0))],
        out_shape=[jax.ShapeDtypeStruct((MOD_ROWS, 6 * D_MODEL), F32),
                   jax.ShapeDtypeStruct((D_MODEL, IN_PAD), BF16),
                   jax.ShapeDtypeStruct((SMALL_ROWS, D_MODEL), F32)],
        scratch_shapes=[pltpu.VMEM((PREP_BUFS, D_MODEL, rm), F32),
                        pltpu.VMEM((PREP_BUFS, w_in_t.shape[0], r1), F32),
                        pltpu.SemaphoreType.DMA((2, PREP_BUFS))],
        compiler_params=pltpu.CompilerParams(dimension_semantics=("arbitrary",),
                                             vmem_limit_bytes=VMEM_LIMIT),
        name="prep",
    )(c, c_ctx, w_mod, b_mod, w_in_t, *small_params)


def _ctx_kernel(x_ref, mod_ref, w_ref, small_ref, exw_ref, u_ref, h0_ref):
    dtb, alog = _dt_params(small_ref)
    lng, lnb = _ln_params(small_ref, 0)
    m = _mod_vectors(mod_ref, BATCH)
    sc = 1.0 + m[1]
    x = x_ref[...].reshape(CTX_ROWS * CTX_LEN, D_MODEL)
    u = _ln_hat(x) * (lng * sc) + (lnb * sc + m[0])
    proj = jnp.dot(u.astype(BF16), w_ref[...], preferred_element_type=F32)
    mprev, mnext = _edge_masks(CTX_ROWS * CTX_LEN, CTX_LEN)
    slabs = []
    for j in range((SSD_WIDTH + SSD_GN) // LANES):
        sl = slice(j * LANES, (j + 1) * LANES)
        pj = proj[:, P_XBC + j * LANES:P_XBC + (j + 1) * LANES]
        slabs.append(_silu(_conv3(pj, small_ref[S_SCW:S_SCW + 3, sl], mprev, mnext)
                           + small_ref[S_SCB:S_SCB + 1, sl]))
    zero = jnp.zeros((N_DH, CTX_LEN), F32)
    for i in range(CTX_ROWS):
        tok = slice(i * CTX_LEN, (i + 1) * CTX_LEN)
        xs = jnp.concatenate([sl_[tok] for sl_ in slabs[:4]], axis=1)
        dt, adt = _dt_rows(proj[tok, P_DT:P_DT + LANES], dtb, alog)
        sc2 = _scan_mm(adt, u_ref)
        excl = jnp.where(_fwd_rows(), sc2[:, :CTX_LEN], sc2[:, CTX_LEN:])
        w = jnp.exp(excl) * dt
        table = jnp.concatenate([zero] * (COL_W // 16) + _split(w, 2) + [zero], axis=0)
        cols = table.T.astype(BF16)
        wx = jnp.dot(cols, exw_ref[...], preferred_element_type=F32)
        for d in range(N_DIRS):
            xw = (xs * wx[:, d * SSD_WIDTH:(d + 1) * SSD_WIDTH]).astype(BF16)
            for g in range(SSD_GROUPS):
                bt = slabs[4 + g][tok].T.astype(BF16)
                h0_ref[i, d, g] = jnp.dot(bt, xw[:, g * GROUP_COLS:(g + 1) * GROUP_COLS],
                                          preferred_element_type=F32)


def _ctx_call(ctx, mod, w_in, small, exw, u_ctx):
    return pl.pallas_call(
        _ctx_kernel,
        grid=(BATCH // CTX_ROWS,),
        in_specs=[pl.BlockSpec((CTX_ROWS, CTX_LEN, D_MODEL), lambda b: (b, 0, 0)),
                  _const_spec((MOD_ROWS, 6 * D_MODEL)),
                  _const_spec((D_MODEL, CTX_PAD)), _const_spec((SMALL_ROWS, D_MODEL)),
                  _const_spec((LANES, 2 * SSD_WIDTH)),
                  _const_spec((CTX_LEN, 2 * CTX_LEN))],
        out_specs=pl.BlockSpec((CTX_ROWS, N_DIRS, SSD_GROUPS, SSD_STATE, GROUP_COLS),
                               lambda b: (b, 0, 0, 0, 0)),
        out_shape=jax.ShapeDtypeStruct((BATCH, N_DIRS, SSD_GROUPS, SSD_STATE, GROUP_COLS), F32),
        compiler_params=pltpu.CompilerParams(dimension_semantics=("arbitrary",),
                                             vmem_limit_bytes=VMEM_LIMIT),
        name="ctx",
    )(ctx, mod, w_in, small, exw, u_ctx)


def _proj_kernel(x_ref, mod_ref, w_ref, small_ref,
                 exe_ref, exw_ref, u_ref, h0_ref, wout_ref, wff1_ref, wff2_ref,
                 hln_ref, ycv_ref, zg_ref, xs_ref, bt_ref, cm_ref, rows_ref, cols_ref, colf_ref, sb_ref,
                 pout_ref, pff1_ref, pff2_ref, st_ref):
    pout_ref[...] = wout_ref[0].astype(BF16)
    pff1_ref[...] = wff1_ref[0].astype(BF16)
    pff2_ref[...] = wff2_ref[0].astype(BF16)

    @pl.when(pl.program_id(1) == 0)
    def _():
        st_ref[...] = h0_ref[0, 0]

    m = _mod_vectors(mod_ref, pl.program_id(0))
    sc = 1.0 + m[1]
    lng, lnb = _ln_params(small_ref, 0)
    scale, shift = lng * sc, lnb * sc + m[0]
    ubs, pas = [], []
    for r in range(2):
        rows = slice(r * (TM_PROJ // 2), (r + 1) * (TM_PROJ // 2))
        xhat = _ln_hat(x_ref[0, rows, :])
        hln_ref[0, rows, :] = xhat * lng + lnb
        ubs.append((xhat * scale + shift).astype(BF16))
        pas.append(jnp.dot(ubs[r], w_ref[:, :P_Z], preferred_element_type=F32))
    ub = jnp.concatenate(ubs, axis=0)
    pa = jnp.concatenate(pas, axis=0)
    mprev, mnext = _edge_masks(TM_PROJ, GRID_W)
    half = TM_PROJ // 2

    def xbc_half(r):
        out = []
        for j in range(XBC_DIM // LANES):
            sl = slice(j * LANES, (j + 1) * LANES)
            pj = pas[r][:, P_XBC + j * LANES:P_XBC + (j + 1) * LANES]
            out.append(_silu(_conv3(pj, small_ref[S_SCW:S_SCW + 3, sl], mprev[:half], mnext[:half])
                             + small_ref[S_SCB:S_SCB + 1, sl]))
        return out
    is_fwd = _fwd_rows()
    zero = jnp.zeros((N_DH, SSD_CHUNK), F32)
    chunks = [slice(c * SSD_CHUNK, (c + 1) * SSD_CHUNK) for c in range(NCH_PROJ)]

    def conv_slabs(p, first):
        for i in range(2):
            gb, gc, gh = (p[:, (3 * i + k) * LANES:(3 * i + k + 1) * LANES] for k in range(3))
            sl = slice((first + i) * LANES, (first + i + 1) * LANES)
            ycv_ref[0, :, sl] = (gb * _conv3(gc * gh, small_ref[S_CW:S_CW + 3, sl], mprev, mnext)).astype(BF16)

    dtb, alog = _dt_params(small_ref)
    slabs_lo = xbc_half(0)
    dts = [_dt_rows(pa[tok, P_DT:P_DT + LANES], dtb, alog) for tok in chunks]
    pz = jnp.dot(ub, w_ref[:, P_Z:P_CONV], preferred_element_type=F32)
    slabs_hi = xbc_half(1)
    slabs = [jnp.concatenate([lo, hi], axis=0) for lo, hi in zip(slabs_lo, slabs_hi)]

    colss = []
    for c, (dt, adt) in enumerate(dts):
        sc3 = _scan_mm(adt, u_ref)
        cs = jnp.where(is_fwd, sc3[:, :SSD_CHUNK], sc3[:, SSD_CHUNK:2 * SSD_CHUNK])
        tot = sc3[:, 2 * SSD_CHUNK:]
        e1 = jnp.exp(cs)
        w = jnp.exp(tot - cs) * dt
        dsum = dt[:SSD_HEADS] + dt[SSD_HEADS:]
        rows_ref[0, c] = jnp.concatenate([cs - jnp.log(dt), dsum, jnp.zeros_like(dsum)], axis=0)
        table = jnp.concatenate([zero] * 3 + _split(e1, 2) + _split(w, 2) + [cs], axis=0).T
        colf_ref[0, c] = table
        colss.append(table.astype(BF16))
        cols_ref[0, c] = colss[c]

    pc1 = jnp.dot(ub, w_ref[:, P_CONV:P_CONV + 6 * LANES], preferred_element_type=F32)

    xs = jnp.concatenate(slabs[:4], axis=1)
    xs_ref[0] = xs.astype(BF16)
    cm_ref[0] = jnp.concatenate(slabs[6:8], axis=1).astype(BF16)
    zg_ref[0] = _silu(pz).astype(BF16)
    bts = []
    for c, tok in enumerate(chunks):
        bt = [slabs[4 + g][tok].T.astype(BF16) for g in range(SSD_GROUPS)]
        for g in range(SSD_GROUPS):
            bt_ref[0, c, g] = bt[g]
        bts.append(bt)
    wxbs = [jnp.dot(cols, exw_ref[:, SSD_WIDTH:], preferred_element_type=F32) for cols in colss]
    decs = [jnp.dot(cols[0:16], exe_ref[:, SSD_WIDTH:], preferred_element_type=F32)[0:1]
            for cols in colss]

    pc2 = jnp.dot(ub, w_ref[:, P_CONV + 6 * LANES:], preferred_element_type=F32)
    conv_slabs(pc1, 0)

    state = [st_ref[g] for g in range(SSD_GROUPS)]
    for c in reversed(range(NCH_PROJ)):
        xw = (xs[chunks[c]] * wxbs[c]).astype(BF16)
        for g in range(SSD_GROUPS):
            gs = slice(g * GROUP_COLS, (g + 1) * GROUP_COLS)
            sb_ref[0, c, g] = state[g].astype(BF16)
            local = jnp.dot(bts[c][g], xw[:, gs], preferred_element_type=F32)
            state[g] = state[g] * decs[c][:, gs] + local
    for g in range(SSD_GROUPS):
        st_ref[g] = state[g]
    conv_slabs(pc2, 2)


def _proj_call(x, mod, w_in, small, exe, exw, u_chunk, h0, w_out, w_ff1, w_ff2):
    n_steps = BATCH * NT_PROJ
    r1, r4 = D_MODEL // n_steps, D_FF // n_steps
    wrow3 = lambda b, t: (0, b * NT_PROJ + t, 0)
    wrow2 = lambda b, t: (b * NT_PROJ + t, 0)
    rev = lambda b, t: (b, NT_PROJ - 1 - t, 0)
    rev4 = lambda b, t: (b, NT_PROJ - 1 - t, 0, 0)
    rev5 = lambda b, t: (b, NT_PROJ - 1 - t, 0, 0, 0)
    tok = lambda width: pl.BlockSpec((1, TM_PROJ, width), rev)
    out_shape = [
        jax.ShapeDtypeStruct((BATCH, SEQ, D_MODEL), F32),
        jax.ShapeDtypeStruct((BATCH, SEQ, CONV_WIDTH), BF16),
        jax.ShapeDtypeStruct((BATCH, SEQ, SSD_WIDTH), BF16),
        jax.ShapeDtypeStruct((BATCH, SEQ, SSD_WIDTH), BF16),
        jax.ShapeDtypeStruct((BATCH, NCHUNK, SSD_GROUPS, SSD_STATE, SSD_CHUNK), BF16),
        jax.ShapeDtypeStruct((BATCH, SEQ, SSD_GN), BF16),
        jax.ShapeDtypeStruct((BATCH, NCHUNK, 2 * N_DH, SSD_CHUNK), F32),
        jax.ShapeDtypeStruct((BATCH, NCHUNK, SSD_CHUNK, LANES), BF16),
        jax.ShapeDtypeStruct((BATCH, NCHUNK, SSD_CHUNK, LANES), F32),
        jax.ShapeDtypeStruct((BATCH, NCHUNK, SSD_GROUPS, SSD_STATE, GROUP_COLS), BF16),
        jax.ShapeDtypeStruct((D_MODEL, D_MODEL), BF16),
        jax.ShapeDtypeStruct((D_MODEL, D_FF), BF16),
        jax.ShapeDtypeStruct((D_FF, D_MODEL), BF16),
    ]
    out_specs = [
        tok(D_MODEL), tok(CONV_WIDTH), tok(SSD_WIDTH), tok(SSD_WIDTH),
        pl.BlockSpec((1, NCH_PROJ, SSD_GROUPS, SSD_STATE, SSD_CHUNK), rev5),
        tok(SSD_GN),
        pl.BlockSpec((1, NCH_PROJ, 2 * N_DH, SSD_CHUNK), rev4),
        pl.BlockSpec((1, NCH_PROJ, SSD_CHUNK, LANES), rev4),
        pl.BlockSpec((1, NCH_PROJ, SSD_CHUNK, LANES), rev4),
        pl.BlockSpec((1, NCH_PROJ, SSD_GROUPS, SSD_STATE, GROUP_COLS), rev5),
        pl.BlockSpec((r1, D_MODEL), wrow2), pl.BlockSpec((r1, D_FF), wrow2), pl.BlockSpec((r4, D_MODEL), wrow2),
    ]
    in_specs = [
        pl.BlockSpec((1, TM_PROJ, D_MODEL), rev),
        _const_spec((MOD_ROWS, 6 * D_MODEL)),
        _const_spec((D_MODEL, IN_PAD)),
        _const_spec((SMALL_ROWS, D_MODEL)),
        _const_spec((LANES, 2 * SSD_WIDTH)), _const_spec((LANES, 2 * SSD_WIDTH)),
        _const_spec((SSD_CHUNK, 3 * SSD_CHUNK)),
        pl.BlockSpec((1, 1, SSD_GROUPS, SSD_STATE, GROUP_COLS), lambda b, t: (b, 1, 0, 0, 0)),
        pl.BlockSpec((1, r1, D_MODEL), wrow3), pl.BlockSpec((1, r1, D_FF), wrow3),
        pl.BlockSpec((1, r4, D_MODEL), wrow3),
    ]
    return pl.pallas_call(
        _proj_kernel,
        grid=(BATCH, NT_PROJ),
        in_specs=in_specs,
        out_specs=out_specs,
        out_shape=out_shape,
        scratch_shapes=[pltpu.VMEM((SSD_GROUPS, SSD_STATE, GROUP_COLS), F32)],
        compiler_params=pltpu.CompilerParams(dimension_semantics=("arbitrary", "arbitrary"),
                                             vmem_limit_bytes=VMEM_LIMIT),
        name="proj",
    )(x, mod, w_in, small, exe, exw, u_chunk, h0, w_out, w_ff1, w_ff2)


def _out_kernel(hln_ref, mod_ref, ycv_ref, zg_ref, xs_ref, bt_ref, cm_ref, rows_ref,
                cols_ref, colf_ref, sb_ref, h0_ref, exw_ref, small_ref, wout_ref,
                wff1_hbm, wff2_hbm,
                out_ref, st_ref, yn_ref, h1_ref, u2_ref, g2_ref, r2_ref, hid_ref, wff1_ref, wff2_ref, wsem):
    s = pl.program_id(0)
    wr = s % 2
    rd = (s + 1) % 2
    li = lax.broadcasted_iota(jnp.int32, (SSD_CHUNK, SSD_CHUNK), 0)
    si = lax.broadcasted_iota(jnp.int32, (SSD_CHUNK, SSD_CHUNK), 1)
    low = li >= si
    diag = li == si
    lo_half = si < SSD_HEADDIM

    def ssd_chunk(c, state, fill):
        tok = slice(c * SSD_CHUNK, (c + 1) * SSD_CHUNK)
        rows = rows_ref[0, c]
        cols = cols_ref[0, c]
        xs = xs_ref[0, tok, :]
        cm = cm_ref[0, tok, :]
        colf = colf_ref[0, c]
        bc = [jnp.broadcast_to(colf[:, COL_CSF + j:COL_CSF + j + 1], (SSD_CHUNK, LANES))
              for j in range(N_DH)]
        e1x = [jnp.concatenate(
            [jnp.exp(jnp.where(lo_half, bc[d * SSD_HEADS + 2 * k], bc[d * SSD_HEADS + 2 * k + 1]))
             for k in range(SSD_HEADS // 2)], axis=1) for d in range(N_DIRS)]
        wxf = jnp.dot(cols, exw_ref[:, :SSD_WIDTH], preferred_element_type=F32)
        dec = e1x[0][SSD_CHUNK - 1:SSD_CHUNK, :]
        gmat = [jnp.dot(cm[:, g * SSD_STATE:(g + 1) * SSD_STATE], bt_ref[0, c, g],
                        preferred_element_type=F32) for g in range(SSD_GROUPS)]
        fill[0]()

        ys = []
        for k in range(SSD_HEADS // 2):
            ms = []
            for h in (2 * k, 2 * k + 1):
                g = h // (SSD_HEADS // SSD_GROUPS)
                hb = SSD_HEADS + h
                arg = jnp.where(low,
                                bc[h] - rows[h:h + 1, :],
                                bc[hb] - rows[hb:hb + 1, :])
                decay_dt = jnp.where(diag, rows[N_DH + h:N_DH + h + 1, :], jnp.exp(arg))
                ms.append((gmat[g] * decay_dt).astype(BF16))
            xp = xs[:, k * LANES:(k + 1) * LANES]
            rhs = jnp.concatenate([jnp.where(lo_half, xp, jnp.zeros_like(xp)),
                                   jnp.where(lo_half, jnp.zeros_like(xp), xp)], axis=0)
            ys.append(jnp.dot(jnp.concatenate(ms, axis=1), rhs, preferred_element_type=F32))
            fill[k + 1]()
        y = jnp.concatenate(ys, axis=1)

        yf, yb = [], []
        for g in range(SSD_GROUPS):
            cg = cm[:, g * SSD_STATE:(g + 1) * SSD_STATE]
            yf.append(jnp.dot(cg, state[g].astype(BF16), preferred_element_type=F32))
            yb.append(jnp.dot(cg, sb_ref[0, c, g], preferred_element_type=F32))
        y = (y + jnp.concatenate(yf, axis=1) * e1x[0]
             + jnp.concatenate(yb, axis=1) * e1x[1]
             + xs.astype(F32) * small_ref[S_DXNW:S_DXNW + 1, :SSD_WIDTH])

        yg = y * zg_ref[0, tok, :].astype(F32)
        ms_ = jnp.mean(yg * yg, axis=-1, keepdims=True)
        yn_ref[tok, :CONV_WIDTH] = ycv_ref[0, tok, :]
        yn_ref[tok, CONV_WIDTH:] = (yg * lax.rsqrt(ms_ + RMS_EPS)
                                   * small_ref[S_DXNW:S_DXNW + 1, SSD_WIDTH:]).astype(BF16)
        fill[5]()

        xw = (xs.astype(F32) * wxf).astype(BF16)
        new_state = []
        for g in range(SSD_GROUPS):
            gs = slice(g * GROUP_COLS, (g + 1) * GROUP_COLS)
            local = jnp.dot(bt_ref[0, c, g], xw[:, gs], preferred_element_type=F32)
            new_state.append(state[g] * dec[:, gs] + local)
        return new_state

    def mlp_up(j, k):
        blk = slice(j * FF_BLK + k * FF_SUB, j * FF_BLK + (k + 1) * FF_SUB)
        hid = jnp.maximum(jnp.dot(u2_ref[...], wff1_ref[:, blk], preferred_element_type=F32), 0.0)
        hid_ref[:, blk] = (hid * hid).astype(BF16)

    def mlp_down(j, acc):
        blk = slice(j * FF_BLK, (j + 1) * FF_BLK)
        part = jnp.dot(hid_ref[:, blk], wff2_ref[blk, :], preferred_element_type=F32)
        return part if acc is None else acc + part

    def ln2_rows(i):
        r = slice(i * SSD_CHUNK, (i + 1) * SSD_CHUNK)
        gain, bias = _ln_params(small_ref, 2)
        out_ref[0, r, :] = _ln_hat(r2_ref[r, :]) * gain + bias

    def step(mixer, mlp, norm):
        state, acc = None, [None]
        pieces = []
        if mlp:
            def down(j):
                acc[0] = mlp_down(j, acc[0])
            for j in range(N_SLAB):
                pieces += [(1, functools.partial(mlp_up, j, k)) for k in range(FF_BLK // FF_SUB)]
            for j in range(N_SLAB - DOWN_KEPT):
                at = (j + 1) * (FF_BLK // FF_SUB) + j + DOWN_LAG
                pieces.insert(at, (FF_BLK // FF_SUB, functools.partial(down, j)))
        n_slots = NCH * FILL_SLOTS
        per_slot = sum(cost for cost, _ in pieces) / n_slots
        slots, issued = [], 0.0
        for i in range(n_slots):
            mine = []
            while pieces and issued < (i + 1) * per_slot:
                cost, fn = pieces.pop(0)
                issued += cost
                mine.append(fn)
            slots.append(lambda mine=mine: [fn() for fn in mine])
        if mixer:
            @pl.when(jnp.minimum(s, N_TILES - 1) % NT == 0)
            def _():
                st_ref[...] = h0_ref[0, 0]
            state = [st_ref[g] for g in range(SSD_GROUPS)]
        for c in range(NCH):
            fill = slots[c * FILL_SLOTS:(c + 1) * FILL_SLOTS]
            if mixer:
                state = ssd_chunk(c, state, fill)
            else:
                for f in fill:
                    f()
            if norm:
                ln2_rows(c)
        if mixer:
            for g in range(SSD_GROUPS):
                st_ref[g] = state[g]
            mix = jnp.dot(yn_ref[...], wout_ref[...], preferred_element_type=F32)
            m = _mod_vectors(mod_ref, jnp.minimum(s, N_TILES - 1) // NT)
            g2_ref[wr, 0:1, :] = m[5]
        for i in range(DOWN_KEPT):
            if mlp:
                down(N_SLAB - DOWN_KEPT + i)
            if mixer:
                for j in range(i * NCH // DOWN_KEPT, (i + 1) * NCH // DOWN_KEPT):
                    r = slice(j * SSD_CHUNK, (j + 1) * SSD_CHUNK)
                    gain, bias = _ln_params(small_ref, 1)
                    h1 = _ln_hat(ALPHA * hln_ref[0, r, :] + m[2] * mix[r]) * gain + bias
                    h1_ref[wr, r, :] = h1
                    u2_ref[r, :] = (h1 * (1.0 + m[4]) + m[3]).astype(BF16)
        if mlp:
            r2_ref[...] = ALPHA * h1_ref[rd] + g2_ref[rd, 0:1, :] * acc[0]

    mlp_weights = (pltpu.make_async_copy(wff1_hbm, wff1_ref, wsem.at[0]),
                   pltpu.make_async_copy(wff2_hbm, wff2_ref, wsem.at[1]))

    @pl.when(s == 0)
    def _():
        for copy in mlp_weights:
            copy.start()
        r2_ref[...] = jnp.zeros((TM, D_MODEL), F32)
        step(True, False, False)

    @pl.when(s == 1)
    def _():
        for copy in mlp_weights:
            copy.wait()

    @pl.when(jnp.logical_and(s >= 1, s < N_TILES))
    def _():
        step(True, True, True)

    @pl.when(s == N_TILES)
    def _():
        step(False, True, True)

    @pl.when(s == N_TILES + 1)
    def _():
        step(False, False, True)


def _out_call(hln, mod, ycv, zg, xs, bt, cm, rows, cols, colf, sb, h0, exw, small,
              wout, wff1, wff2):
    cur = lambda s: jnp.minimum(s, N_TILES - 1)
    fwd = lambda s: (cur(s) // NT, cur(s) % NT, 0)
    fwd4 = lambda s: (cur(s) // NT, cur(s) % NT, 0, 0)
    fwd5 = lambda s: (cur(s) // NT, cur(s) % NT, 0, 0, 0)
    done = lambda s: (jnp.maximum(s - 2, 0) // NT, jnp.maximum(s - 2, 0) % NT, 0)
    tok = lambda width: pl.BlockSpec((1, TM, width), fwd)
    in_specs = [
        tok(D_MODEL),
        _const_spec((MOD_ROWS, 6 * D_MODEL)),
        tok(CONV_WIDTH), tok(SSD_WIDTH), tok(SSD_WIDTH),
        pl.BlockSpec((1, NCH, SSD_GROUPS, SSD_STATE, SSD_CHUNK), fwd5),
        tok(SSD_GN),
        pl.BlockSpec((1, NCH, 2 * N_DH, SSD_CHUNK), fwd4),
        pl.BlockSpec((1, NCH, SSD_CHUNK, LANES), fwd4),
        pl.BlockSpec((1, NCH, SSD_CHUNK, LANES), fwd4),
        pl.BlockSpec((1, NCH, SSD_GROUPS, SSD_STATE, GROUP_COLS), fwd5),
        pl.BlockSpec((1, 1, SSD_GROUPS, SSD_STATE, GROUP_COLS), lambda s: (cur(s) // NT, 0, 0, 0, 0)),
        _const_spec((LANES, 2 * SSD_WIDTH)),
        _const_spec((SMALL_ROWS, D_MODEL)),
        _const_spec((D_MODEL, D_MODEL)),
        pl.BlockSpec(memory_space=pl.ANY), pl.BlockSpec(memory_space=pl.ANY),
    ]
    return pl.pallas_call(
        _out_kernel,
        grid=(N_TILES + 2,),
        in_specs=in_specs,
        out_specs=pl.BlockSpec((1, TM, D_MODEL), done),
        out_shape=jax.ShapeDtypeStruct((BATCH, SEQ, D_MODEL), F32),
        scratch_shapes=[pltpu.VMEM((SSD_GROUPS, SSD_STATE, GROUP_COLS), F32),
                        pltpu.VMEM((TM, D_MODEL), BF16),
                        pltpu.VMEM((2, TM, D_MODEL), F32),
                        pltpu.VMEM((TM, D_MODEL), BF16),
                        pltpu.VMEM((2, G2_ROWS, D_MODEL), F32),
                        pltpu.VMEM((TM, D_MODEL), F32),
                        pltpu.VMEM((TM, D_FF), BF16),
                        pltpu.VMEM((D_MODEL, D_FF), BF16),
                        pltpu.VMEM((D_FF, D_MODEL), BF16),
                        pltpu.SemaphoreType.DMA((2,))],
        compiler_params=pltpu.CompilerParams(dimension_semantics=("arbitrary",),
                                             vmem_limit_bytes=VMEM_LIMIT),
        name="out",
    )(hln, mod, ycv, zg, xs, bt, cm, rows, cols, colf, sb, h0, exw, small,
      wout, wff1, wff2)


def kernel(x, c, ctx, c_ctx, ln_in_g, ln_in_b, w_mod, b_mod, w_in, conv_w, ssd_conv_w, ssd_conv_b,
           dt_bias, a_log, ssd_d, ssd_norm_w, w_out, ln1_g, ln1_b, w_ff1, w_ff2, ln2_g, ln2_b):
    flat = lambda v: v.reshape(1, -1)
    mod, w_in_p, small = _prep_call(
        c, flat(c_ctx), w_mod, b_mod, w_in[0].T,
        (flat(ssd_conv_w), ssd_conv_b, flat(conv_w), dt_bias, a_log, ssd_d, ssd_norm_w,
         flat(ln_in_g), flat(ln_in_b), ln1_g, ln1_b, ln2_g, ln2_b))
    exe = jnp.asarray(_EXE, BF16)
    exw = jnp.asarray(_EXW, BF16)

    h0 = _ctx_call(ctx, mod, w_in_p, small, exw, jnp.asarray(_U_CTX, BF16))
    hln, ycv, zg, xs, bt, cm, rows, cols, colf, sb, w_out_p, w_ff1_p, w_ff2_p = _proj_call(
        x, mod, w_in_p, small, exe, exw,
        jnp.asarray(_U_CHUNK, BF16), h0, w_out, w_ff1, w_ff2)
    return _out_call(hln, mod, ycv, zg, xs, bt, cm, rows, cols, colf, sb, h0, exw,
                     small, w_out_p, w_ff1_p, w_ff2_p)
```

```python
import functools

import jax
import jax.numpy as jnp
import numpy as np
from jax import lax
from jax.experimental import pallas as pl
from jax.experimental.pallas import tpu as pltpu

F32 = jnp.float32
BF16 = jnp.bfloat16

D_MODEL = 1024
BATCH = 8
SEQ = 2048
CTX_LEN = 256
GRID_W = 64
CONV_WIDTH = 512
SSD_WIDTH = 512
SSD_HEADDIM = 64
SSD_HEADS = 8
SSD_GROUPS = 2
SSD_STATE = 128
SSD_CHUNK = 128
N_DIRS = 2
D_FF = 4 * D_MODEL
LN_EPS = 1e-5
RMS_EPS = 1e-5
SSD_GN = SSD_GROUPS * SSD_STATE
XBC_DIM = SSD_WIDTH + 2 * SSD_GN
Z_OFF = 3 * CONV_WIDTH
XBC_OFF = Z_OFF + SSD_WIDTH
DT_OFF = XBC_OFF + XBC_DIM
N_DH = N_DIRS * SSD_HEADS
LANES = 128
P_DT = 0
P_XBC = LANES
P_Z = P_XBC + XBC_DIM
P_CONV = P_Z + SSD_WIDTH
IN_PAD = P_CONV + 3 * CONV_WIDTH
CTX_PAD = P_XBC + SSD_WIDTH + SSD_GN
GROUP_COLS = (SSD_HEADS // SSD_GROUPS) * SSD_HEADDIM
ALPHA = 2.0 ** 0.25

TM = 512
CTX_ROWS = 2
MOD_ROWS = 16
SMALL_ROWS = 24
G2_ROWS = 8
PREP_BUFS = 4
PREP_STEPS = 4
NCH = TM // SSD_CHUNK
NT = SEQ // TM
NCHUNK = SEQ // SSD_CHUNK
N_TILES = BATCH * NT
TM_PROJ = 1024
NCH_PROJ = TM_PROJ // SSD_CHUNK
NT_PROJ = SEQ // TM_PROJ
FF_BLK = 1024
FF_SUB = 256
N_SLAB = D_FF // FF_BLK
FILL_SLOTS = 6
DOWN_KEPT = 2
DOWN_LAG = 1
VMEM_LIMIT = 58 * 1024 * 1024

COL_E1 = 48
COL_W = 80
COL_CSF = 112


def _expansion(col0, pieces, width):
    m = np.zeros((LANES, N_DH * width), np.float32)
    for t in range(pieces):
        for j in range(N_DH):
            m[col0 + 16 * t + j, j * width:(j + 1) * width] = 1.0
    return m


_EXE = _expansion(COL_E1, 2, SSD_HEADDIM)
_EXW = _expansion(COL_W, 2, SSD_HEADDIM)


def _ln_hat(x):
    mu = jnp.mean(x, axis=-1, keepdims=True)
    xc = x - mu
    var = jnp.mean(xc * xc, axis=-1, keepdims=True)
    return xc * lax.rsqrt(var + LN_EPS)


def _silu(x):
    return x / (1.0 + jnp.exp(-x))


def _softplus(x):
    return jnp.maximum(x, 0.0) + jnp.log1p(jnp.exp(-jnp.abs(x)))


def _edge_masks(rows, period):
    pos = lax.broadcasted_iota(jnp.int32, (rows, LANES), 0) % period
    return (pos != 0).astype(F32), (pos != period - 1).astype(F32)


def _conv3(t, w, mprev, mnext):
    rows = t.shape[0]
    prev = pltpu.roll(t, 1, 0) * mprev
    nxt = pltpu.roll(t, rows - 1, 0) * mnext
    return prev * w[0:1, :] + t * w[1:2, :] + nxt * w[2:3, :]


def _split(v, pieces):
    out = []
    for _ in range(pieces - 1):
        p = v.astype(BF16).astype(F32)
        out.append(p)
        v = v - p
    out.append(v.astype(BF16).astype(F32))
    return out


def _tri(length, op):
    i = np.arange(length)
    return op(i[:, None], i[None, :]).astype(np.float32)


_U_CHUNK = np.concatenate([_tri(SSD_CHUNK, np.less_equal), _tri(SSD_CHUNK, np.greater_equal),
                           np.ones((SSD_CHUNK, SSD_CHUNK), np.float32)], axis=1)
_U_CTX = np.concatenate([_tri(CTX_LEN, np.greater), _tri(CTX_LEN, np.less)], axis=1)


def _scan_mm(v, u_ref):
    pieces = jnp.concatenate(_split(v, 3), axis=0).astype(BF16)
    o = jnp.dot(pieces, u_ref[...], preferred_element_type=F32)
    return o[0:N_DH] + o[N_DH:2 * N_DH] + o[2 * N_DH:3 * N_DH]


def _dt_rows(raw, dtb, a_log):
    r = raw.T[0:N_DH, :] + dtb
    dt = _softplus(r)
    return dt, dt * (-jnp.exp(a_log))


def _fwd_rows():
    return lax.broadcasted_iota(jnp.int32, (N_DH, 1), 0) < SSD_HEADS


S_SCW = 0
S_SCB = 3
S_CW = 4
S_DXNW = 8
S_LN = 9
S_DTB = 16
S_ALOG = 18


def _dt_params(small_ref):
    sub = lax.broadcasted_iota(jnp.int32, (N_DH, LANES), 0)
    lane = lax.broadcasted_iota(jnp.int32, (N_DH, LANES), 1)
    head = jnp.where(sub < SSD_HEADS, sub, sub - SSD_HEADS)

    def pick(r):
        t = jnp.where(sub < SSD_HEADS, small_ref[r:r + 1, 0:LANES], small_ref[r + 1:r + 2, 0:LANES])
        return jnp.sum(jnp.where(lane == head, t, 0.0), axis=1, keepdims=True)
    return pick(S_DTB), pick(S_ALOG)


def _ln_params(small_ref, which):
    r = S_LN + 2 * which
    return small_ref[r:r + 1, :], small_ref[r + 1:r + 2, :]


def _mod_vectors(mod_ref, row):
    r = mod_ref[pl.ds(row, 1), :]
    return [r[:, k * D_MODEL:(k + 1) * D_MODEL] for k in range(6)]


def _const_spec(shape):
    nd = len(shape)
    return pl.BlockSpec(shape, lambda *_: (0,) * nd, pipeline_mode=pl.Buffered(1))


def _prep_copies(step, wmod_hbm, wint_hbm, wmod_buf, wint_buf, sem):
    r1, rm = D_MODEL // PREP_STEPS, 6 * D_MODEL // PREP_STEPS
    slot = step % PREP_BUFS
    col = lambda width: pl.ds(step * width if isinstance(step, int) else pl.multiple_of(step * width, LANES), width)
    return (pltpu.make_async_copy(wmod_hbm.at[0, :, col(rm)], wmod_buf.at[slot], sem.at[0, slot]),
            pltpu.make_async_copy(wint_hbm.at[:, col(r1)], wint_buf.at[slot], sem.at[1, slot]))


def _prep_kernel(c_ref, cctx_ref, wmod_hbm, bmod_ref, wint_hbm,
                 scw_ref, scb_ref, cw_ref, dtb_ref, alog_ref, d_ref, nw_ref,
                 lng_ref, lnb_ref, l1g_ref, l1b_ref, l2g_ref, l2b_ref,
                 mod_ref, pin_ref, small_ref, wmod_buf, wint_buf, sem):
    i = pl.program_id(0)
    ring = (wmod_hbm, wint_hbm, wmod_buf, wint_buf, sem)

    @pl.when(i == 0)
    def _():
        for step in range(PREP_BUFS - 1):
            for copy in _prep_copies(step, *ring):
                copy.start()

    @pl.when(i + PREP_BUFS - 1 < PREP_STEPS)
    def _():
        for copy in _prep_copies(i + PREP_BUFS - 1, *ring):
            copy.start()

    @pl.when(i == 0)
    def _():
        small_ref[...] = jnp.zeros((SMALL_ROWS, D_MODEL), F32)
        for k in range(3):
            small_ref[S_SCW + k:S_SCW + k + 1, :XBC_DIM] = scw_ref[:, k * XBC_DIM:(k + 1) * XBC_DIM]
            small_ref[S_CW + k:S_CW + k + 1, :CONV_WIDTH] = cw_ref[:, k * CONV_WIDTH:(k + 1) * CONV_WIDTH]
        small_ref[S_SCB:S_SCB + 1, :XBC_DIM] = scb_ref[...]
        small_ref[S_DTB:S_DTB + N_DIRS, :SSD_HEADS] = dtb_ref[0]
        small_ref[S_ALOG:S_ALOG + N_DIRS, :SSD_HEADS] = alog_ref[0]
        d = d_ref[...]
        first = lax.broadcasted_iota(jnp.int32, (1, LANES), 1) < SSD_HEADDIM
        for k in range(SSD_WIDTH // LANES):
            pair = [jnp.broadcast_to(d[:, 2 * k + i:2 * k + i + 1], (1, LANES)) for i in range(2)]
            small_ref[S_DXNW:S_DXNW + 1, k * LANES:(k + 1) * LANES] = jnp.where(first, pair[0], pair[1])
        small_ref[S_DXNW:S_DXNW + 1, SSD_WIDTH:] = nw_ref[...]
        for k, ref in enumerate((lng_ref, lnb_ref, l1g_ref, l1b_ref, l2g_ref, l2b_ref)):
            small_ref[S_LN + k:S_LN + k + 1, :] = ref[...]

    ctx_row = jnp.where(lax.broadcasted_iota(jnp.int32, (MOD_ROWS - BATCH, D_MODEL), 0) == 0, cctx_ref[...], 0.0)
    cs = _silu(jnp.concatenate([c_ref[...], ctx_row], axis=0)).astype(BF16)

    for copy in _prep_copies(i, wmod_hbm, wint_hbm, wmod_buf, wint_buf, sem):
        copy.wait()
    slot = i % PREP_BUFS
    mod_ref[...] = jnp.dot(cs, wmod_buf[slot].astype(BF16), preferred_element_type=F32) + bmod_ref[...]

    dt_rows = jnp.concatenate([wint_buf[slot, DT_OFF:DT_OFF + N_DH, :],
                               jnp.zeros((LANES - N_DH, wint_buf.shape[2]), F32)], axis=0)
    pin_ref[:, P_DT:P_DT + LANES] = dt_rows.T.astype(BF16)
    def move(dst, src):
        pin_ref[:, dst:dst + LANES] = wint_buf[slot, src:src + LANES, :].T.astype(BF16)

    for dst, src, width in ((P_XBC, XBC_OFF, XBC_DIM), (P_Z, Z_OFF, SSD_WIDTH)):
        for j in range(0, width, LANES):
            move(dst + j, src + j)
    for j in range(CONV_WIDTH // LANES):
        for k in range(3):
            move(P_CONV + (3 * j + k) * LANES, k * CONV_WIDTH + j * LANES)


def _prep_call(c, c_ctx, w_mod, b_mod, w_in_t, small_params):
    steps = PREP_STEPS
    r1, rm = D_MODEL // steps, 6 * D_MODEL // steps
    whole = lambda a: pl.BlockSpec(a.shape, lambda i: (0,) * a.ndim)
    return pl.pallas_call(
        _prep_kernel,
        grid=(steps,),
        in_specs=[whole(c), whole(c_ctx),
                  pl.BlockSpec(memory_space=pl.ANY),
                  pl.BlockSpec((1, rm), lambda i: (0, i)),
                  pl.BlockSpec(memory_space=pl.ANY)] + [whole(a) for a in small_params],
        out_specs=[pl.BlockSpec((MOD_ROWS, rm), lambda i: (0, i)),
                   pl.BlockSpec((r1, IN_PAD), lambda i: (i, 0)),
                   pl.BlockSpec((SMALL_ROWS, D_MODEL), lambda i: (0, 0))],
        out_shape=[jax.ShapeDtypeStruct((MOD_ROWS, 6 * D_MODEL), F32),
                   jax.ShapeDtypeStruct((D_MODEL, IN_PAD), BF16),
                   jax.ShapeDtypeStruct((SMALL_ROWS, D_MODEL), F32)],
        scratch_shapes=[pltpu.VMEM((PREP_BUFS, D_MODEL, rm), F32),
                        pltpu.VMEM((PREP_BUFS, w_in_t.shape[0], r1), F32),
                        pltpu.SemaphoreType.DMA((2, PREP_BUFS))],
        compiler_params=pltpu.CompilerParams(dimension_semantics=("arbitrary",),
                                             vmem_limit_bytes=VMEM_LIMIT),
        name="prep",
    )(c, c_ctx, w_mod, b_mod, w_in_t, *small_params)


def _ctx_kernel(x_ref, mod_ref, w_ref, small_ref, exw_ref, u_ref, h0_ref):
    dtb, alog = _dt_params(small_ref)
    lng, lnb = _ln_params(small_ref, 0)
    m = _mod_vectors(mod_ref, BATCH)
    sc = 1.0 + m[1]
    x = x_ref[...].reshape(CTX_ROWS * CTX_LEN, D_MODEL)
    u = _ln_hat(x) * (lng * sc) + (lnb * sc + m[0])
    proj = jnp.dot(u.astype(BF16), w_ref[...], preferred_element_type=F32)
    mprev, mnext = _edge_masks(CTX_ROWS * CTX_LEN, CTX_LEN)
    slabs = []
    for j in range((SSD_WIDTH + SSD_GN) // LANES):
        sl = slice(j * LANES, (j + 1) * LANES)
        pj = proj[:, P_XBC + j * LANES:P_XBC + (j + 1) * LANES]
        slabs.append(_silu(_conv3(pj, small_ref[S_SCW:S_SCW + 3, sl], mprev, mnext)
                           + small_ref[S_SCB:S_SCB + 1, sl]))
    zero = jnp.zeros((N_DH, CTX_LEN), F32)
    for i in range(CTX_ROWS):
        tok = slice(i * CTX_LEN, (i + 1) * CTX_LEN)
        xs = jnp.concatenate([sl_[tok] for sl_ in slabs[:4]], axis=1)
        dt, adt = _dt_rows(proj[tok, P_DT:P_DT + LANES], dtb, alog)
        sc2 = _scan_mm(adt, u_ref)
        excl = jnp.where(_fwd_rows(), sc2[:, :CTX_LEN], sc2[:, CTX_LEN:])
        w = jnp.exp(excl) * dt
        table = jnp.concatenate([zero] * (COL_W // 16) + _split(w, 2) + [zero], axis=0)
        cols = table.T.astype(BF16)
        wx = jnp.dot(cols, exw_ref[...], preferred_element_type=F32)
        for d in range(N_DIRS):
            xw = (xs * wx[:, d * SSD_WIDTH:(d + 1) * SSD_WIDTH]).astype(BF16)
            for g in range(SSD_GROUPS):
                bt = slabs[4 + g][tok].T.astype(BF16)
                h0_ref[i, d, g] = jnp.dot(bt, xw[:, g * GROUP_COLS:(g + 1) * GROUP_COLS],
                                          preferred_element_type=F32)


def _ctx_call(ctx, mod, w_in, small, exw, u_ctx):
    return pl.pallas_call(
        _ctx_kernel,
        grid=(BATCH // CTX_ROWS,),
        in_specs=[pl.BlockSpec((CTX_ROWS, CTX_LEN, D_MODEL), lambda b: (b, 0, 0)),
                  _const_spec((MOD_ROWS, 6 * D_MODEL)),
                  _const_spec((D_MODEL, CTX_PAD)), _const_spec((SMALL_ROWS, D_MODEL)),
                  _const_spec((LANES, 2 * SSD_WIDTH)),
                  _const_spec((CTX_LEN, 2 * CTX_LEN))],
        out_specs=pl.BlockSpec((CTX_ROWS, N_DIRS, SSD_GROUPS, SSD_STATE, GROUP_COLS),
                               lambda b: (b, 0, 0, 0, 0)),
        out_shape=jax.ShapeDtypeStruct((BATCH, N_DIRS, SSD_GROUPS, SSD_STATE, GROUP_COLS), F32),
        compiler_params=pltpu.CompilerParams(dimension_semantics=("arbitrary",),
                                             vmem_limit_bytes=VMEM_LIMIT),
        name="ctx",
    )(ctx, mod, w_in, small, exw, u_ctx)


def _proj_kernel(x_ref, mod_ref, w_ref, small_ref,
                 exe_ref, exw_ref, u_ref, h0_ref, wout_ref, wff1_ref, wff2_ref,
                 hln_ref, ycv_ref, zg_ref, xs_ref, bt_ref, cm_ref, rows_ref, cols_ref, colf_ref, sb_ref,
                 pout_ref, pff1_ref, pff2_ref, st_ref):
    pout_ref[...] = wout_ref[0].astype(BF16)
    pff1_ref[...] = wff1_ref[0].astype(BF16)
    pff2_ref[...] = wff2_ref[0].astype(BF16)

    @pl.when(pl.program_id(1) == 0)
    def _():
        st_ref[...] = h0_ref[0, 0]

    m = _mod_vectors(mod_ref, pl.program_id(0))
    sc = 1.0 + m[1]
    lng, lnb = _ln_params(small_ref, 0)
    scale, shift = lng * sc, lnb * sc + m[0]
    ubs, pas = [], []
    for r in range(2):
        rows = slice(r * (TM_PROJ // 2), (r + 1) * (TM_PROJ // 2))
        xhat = _ln_hat(x_ref[0, rows, :])
        hln_ref[0, rows, :] = xhat * lng + lnb
        ubs.append((xhat * scale + shift).astype(BF16))
        pas.append(jnp.dot(ubs[r], w_ref[:, :P_Z], preferred_element_type=F32))
    ub = jnp.concatenate(ubs, axis=0)
    pa = jnp.concatenate(pas, axis=0)
    mprev, mnext = _edge_masks(TM_PROJ, GRID_W)
    half = TM_PROJ // 2

    def xbc_half(r):
        out = []
        for j in range(XBC_DIM // LANES):
            sl = slice(j * LANES, (j + 1) * LANES)
            pj = pas[r][:, P_XBC + j * LANES:P_XBC + (j + 1) * LANES]
            out.append(_silu(_conv3(pj, small_ref[S_SCW:S_SCW + 3, sl], mprev[:half], mnext[:half])
                             + small_ref[S_SCB:S_SCB + 1, sl]))
        return out
    is_fwd = _fwd_rows()
    zero = jnp.zeros((N_DH, SSD_CHUNK), F32)
    chunks = [slice(c * SSD_CHUNK, (c + 1) * SSD_CHUNK) for c in range(NCH_PROJ)]

    def conv_slabs(p, first):
        for i in range(2):
            gb, gc, gh = (p[:, (3 * i + k) * LANES:(3 * i + k + 1) * LANES] for k in range(3))
            sl = slice((first + i) * LANES, (first + i + 1) * LANES)
            ycv_ref[0, :, sl] = (gb * _conv3(gc * gh, small_ref[S_CW:S_CW + 3, sl], mprev, mnext)).astype(BF16)

    dtb, alog = _dt_params(small_ref)
    slabs_lo = xbc_half(0)
    dts = [_dt_rows(pa[tok, P_DT:P_DT + LANES], dtb, alog) for tok in chunks]
    pz = jnp.dot(ub, w_ref[:, P_Z:P_CONV], preferred_element_type=F32)
    slabs_hi = xbc_half(1)
    slabs = [jnp.concatenate([lo, hi], axis=0) for lo, hi in zip(slabs_lo, slabs_hi)]

    colss = []
    for c, (dt, adt) in enumerate(dts):
        sc3 = _scan_mm(adt, u_ref)
        cs = jnp.where(is_fwd, sc3[:, :SSD_CHUNK], sc3[:, SSD_CHUNK:2 * SSD_CHUNK])
        tot = sc3[:, 2 * SSD_CHUNK:]
        e1 = jnp.exp(cs)
        w = jnp.exp(tot - cs) * dt
        dsum = dt[:SSD_HEADS] + dt[SSD_HEADS:]
        rows_ref[0, c] = jnp.concatenate([cs - jnp.log(dt), dsum, jnp.zeros_like(dsum)], axis=0)
        table = jnp.concatenate([zero] * 3 + _split(e1, 2) + _split(w, 2) + [cs], axis=0).T
        colf_ref[0, c] = table
        colss.append(table.astype(BF16))
        cols_ref[0, c] = colss[c]

    pc1 = jnp.dot(ub, w_ref[:, P_CONV:P_CONV + 6 * LANES], preferred_element_type=F32)

    xs = jnp.concatenate(slabs[:4], axis=1)
    xs_ref[0] = xs.astype(BF16)
    cm_ref[0] = jnp.concatenate(slabs[6:8], axis=1).astype(BF16)
    zg_ref[0] = _silu(pz).astype(BF16)
    bts = []
    for c, tok in enumerate(chunks):
        bt = [slabs[4 + g][tok].T.astype(BF16) for g in range(SSD_GROUPS)]
        for g in range(SSD_GROUPS):
            bt_ref[0, c, g] = bt[g]
        bts.append(bt)
    wxbs = [jnp.dot(cols, exw_ref[:, SSD_WIDTH:], preferred_element_type=F32) for cols in colss]
    decs = [jnp.dot(cols[0:16], exe_ref[:, SSD_WIDTH:], preferred_element_type=F32)[0:1]
            for cols in colss]

    pc2 = jnp.dot(ub, w_ref[:, P_CONV + 6 * LANES:], preferred_element_type=F32)
    conv_slabs(pc1, 0)

    state = [st_ref[g] for g in range(SSD_GROUPS)]
    for c in reversed(range(NCH_PROJ)):
        xw = (xs[chunks[c]] * wxbs[c]).astype(BF16)
        for g in range(SSD_GROUPS):
            gs = slice(g * GROUP_COLS, (g + 1) * GROUP_COLS)
            sb_ref[0, c, g] = state[g].astype(BF16)
            local = jnp.dot(bts[c][g], xw[:, gs], preferred_element_type=F32)
            state[g] = state[g] * decs[c][:, gs] + local
    for g in range(SSD_GROUPS):
        st_ref[g] = state[g]
    conv_slabs(pc2, 2)


def _proj_call(x, mod, w_in, small, exe, exw, u_chunk, h0, w_out, w_ff1, w_ff2):
    n_steps = BATCH * NT_PROJ
    r1, r4 = D_MODEL // n_steps, D_FF // n_steps
    wrow3 = lambda b, t: (0, b * NT_PROJ + t, 0)
    wrow2 = lambda b, t: (b * NT_PROJ + t, 0)
    rev = lambda b, t: (b, NT_PROJ - 1 - t, 0)
    rev4 = lambda b, t: (b, NT_PROJ - 1 - t, 0, 0)
    rev5 = lambda b, t: (b, NT_PROJ - 1 - t, 0, 0, 0)
    tok = lambda width: pl.BlockSpec((1, TM_PROJ, width), rev)
    out_shape = [
        jax.ShapeDtypeStruct((BATCH, SEQ, D_MODEL), F32),
        jax.ShapeDtypeStruct((BATCH, SEQ, CONV_WIDTH), BF16),
        jax.ShapeDtypeStruct((BATCH, SEQ, SSD_WIDTH), BF16),
        jax.ShapeDtypeStruct((BATCH, SEQ, SSD_WIDTH), BF16),
        jax.ShapeDtypeStruct((BATCH, NCHUNK, SSD_GROUPS, SSD_STATE, SSD_CHUNK), BF16),
        jax.ShapeDtypeStruct((BATCH, SEQ, SSD_GN), BF16),
        jax.ShapeDtypeStruct((BATCH, NCHUNK, 2 * N_DH, SSD_CHUNK), F32),
        jax.ShapeDtypeStruct((BATCH, NCHUNK, SSD_CHUNK, LANES), BF16),
        jax.ShapeDtypeStruct((BATCH, NCHUNK, SSD_CHUNK, LANES), F32),
        jax.ShapeDtypeStruct((BATCH, NCHUNK, SSD_GROUPS, SSD_STATE, GROUP_COLS), BF16),
        jax.ShapeDtypeStruct((D_MODEL, D_MODEL), BF16),
        jax.ShapeDtypeStruct((D_MODEL, D_FF), BF16),
        jax.ShapeDtypeStruct((D_FF, D_MODEL), BF16),
    ]
    out_specs = [
        tok(D_MODEL), tok(CONV_WIDTH), tok(SSD_WIDTH), tok(SSD_WIDTH),
        pl.BlockSpec((1, NCH_PROJ, SSD_GROUPS, SSD_STATE, SSD_CHUNK), rev5),
        tok(SSD_GN),
        pl.BlockSpec((1, NCH_PROJ, 2 * N_DH, SSD_CHUNK), rev4),
        pl.BlockSpec((1, NCH_PROJ, SSD_CHUNK, LANES), rev4),
        pl.BlockSpec((1, NCH_PROJ, SSD_CHUNK, LANES), rev4),
        pl.BlockSpec((1, NCH_PROJ, SSD_GROUPS, SSD_STATE, GROUP_COLS), rev5),
        pl.BlockSpec((r1, D_MODEL), wrow2), pl.BlockSpec((r1, D_FF), wrow2), pl.BlockSpec((r4, D_MODEL), wrow2),
    ]
    in_specs = [
        pl.BlockSpec((1, TM_PROJ, D_MODEL), rev),
        _const_spec((MOD_ROWS, 6 * D_MODEL)),
        _const_spec((D_MODEL, IN_PAD)),
        _const_spec((SMALL_ROWS, D_MODEL)),
        _const_spec((LANES, 2 * SSD_WIDTH)), _const_spec((LANES, 2 * SSD_WIDTH)),
        _const_spec((SSD_CHUNK, 3 * SSD_CHUNK)),
        pl.BlockSpec((1, 1, SSD_GROUPS, SSD_STATE, GROUP_COLS), lambda b, t: (b, 1, 0, 0, 0)),
        pl.BlockSpec((1, r1, D_MODEL), wrow3), pl.BlockSpec((1, r1, D_FF), wrow3),
        pl.BlockSpec((1, r4, D_MODEL), wrow3),
    ]
    return pl.pallas_call(
        _proj_kernel,
        grid=(BATCH, NT_PROJ),
        in_specs=in_specs,
        out_specs=out_specs,
        out_shape=out_shape,
        scratch_shapes=[pltpu.VMEM((SSD_GROUPS, SSD_STATE, GROUP_COLS), F32)],
        compiler_params=pltpu.CompilerParams(dimension_semantics=("arbitrary", "arbitrary"),
                                             vmem_limit_bytes=VMEM_LIMIT),
        name="proj",
    )(x, mod, w_in, small, exe, exw, u_chunk, h0, w_out, w_ff1, w_ff2)


def _out_kernel(hln_ref, mod_ref, ycv_ref, zg_ref, xs_ref, bt_ref, cm_ref, rows_ref,
                cols_ref, colf_ref, sb_ref, h0_ref, exw_ref, small_ref, wout_ref,
                wff1_hbm, wff2_hbm,
                out_ref, st_ref, yn_ref, h1_ref, u2_ref, g2_ref, r2_ref, hid_ref, wff1_ref, wff2_ref, wsem):
    s = pl.program_id(0)
    wr = s % 2
    rd = (s + 1) % 2
    li = lax.broadcasted_iota(jnp.int32, (SSD_CHUNK, SSD_CHUNK), 0)
    si = lax.broadcasted_iota(jnp.int32, (SSD_CHUNK, SSD_CHUNK), 1)
    low = li >= si
    diag = li == si
    lo_half = si < SSD_HEADDIM

    def ssd_chunk(c, state, fill):
        tok = slice(c * SSD_CHUNK, (c + 1) * SSD_CHUNK)
        rows = rows_ref[0, c]
        cols = cols_ref[0, c]
        xs = xs_ref[0, tok, :]
        cm = cm_ref[0, tok, :]
        colf = colf_ref[0, c]
        bc = [jnp.broadcast_to(colf[:, COL_CSF + j:COL_CSF + j + 1], (SSD_CHUNK, LANES))
              for j in range(N_DH)]
        e1x = [jnp.concatenate(
            [jnp.exp(jnp.where(lo_half, bc[d * SSD_HEADS + 2 * k], bc[d * SSD_HEADS + 2 * k + 1]))
             for k in range(SSD_HEADS // 2)], axis=1) for d in range(N_DIRS)]
        wxf = jnp.dot(cols, exw_ref[:, :SSD_WIDTH], preferred_element_type=F32)
        dec = e1x[0][SSD_CHUNK - 1:SSD_CHUNK, :]
        gmat = [jnp.dot(cm[:, g * SSD_STATE:(g + 1) * SSD_STATE], bt_ref[0, c, g],
                        preferred_element_type=F32) for g in range(SSD_GROUPS)]
        fill[0]()

        ys = []
        for k in range(SSD_HEADS // 2):
            ms = []
            for h in (2 * k, 2 * k + 1):
                g = h // (SSD_HEADS // SSD_GROUPS)
                hb = SSD_HEADS + h
                arg = jnp.where(low,
                                bc[h] - rows[h:h + 1, :],
                                bc[hb] - rows[hb:hb + 1, :])
                decay_dt = jnp.where(diag, rows[N_DH + h:N_DH + h + 1, :], jnp.exp(arg))
                ms.append((gmat[g] * decay_dt).astype(BF16))
            xp = xs[:, k * LANES:(k + 1) * LANES]
            rhs = jnp.concatenate([jnp.where(lo_half, xp, jnp.zeros_like(xp)),
                                   jnp.where(lo_half, jnp.zeros_like(xp), xp)], axis=0)
            ys.append(jnp.dot(jnp.concatenate(ms, axis=1), rhs, preferred_element_type=F32))
            fill[k + 1]()
        y = jnp.concatenate(ys, axis=1)

        yf, yb = [], []
        for g in range(SSD_GROUPS):
            cg = cm[:, g * SSD_STATE:(g + 1) * SSD_STATE]
            yf.append(jnp.dot(cg, state[g].astype(BF16), preferred_element_type=F32))
            yb.append(jnp.dot(cg, sb_ref[0, c, g], preferred_element_type=F32))
        y = (y + jnp.concatenate(yf, axis=1) * e1x[0]
             + jnp.concatenate(yb, axis=1) * e1x[1]
             + xs.astype(F32) * small_ref[S_DXNW:S_DXNW + 1, :SSD_WIDTH])

        yg = y * zg_ref[0, tok, :].astype(F32)
        ms_ = jnp.mean(yg * yg, axis=-1, keepdims=True)
        yn_ref[tok, :CONV_WIDTH] = ycv_ref[0, tok, :]
        yn_ref[tok, CONV_WIDTH:] = (yg * lax.rsqrt(ms_ + RMS_EPS)
                                   * small_ref[S_DXNW:S_DXNW + 1, SSD_WIDTH:]).astype(BF16)
        fill[5]()

        xw = (xs.astype(F32) * wxf).astype(BF16)
        new_state = []
        for g in range(SSD_GROUPS):
            gs = slice(g * GROUP_COLS, (g + 1) * GROUP_COLS)
            local = jnp.dot(bt_ref[0, c, g], xw[:, gs], preferred_element_type=F32)
            new_state.append(state[g] * dec[:, gs] + local)
        return new_state

    def mlp_up(j, k):
        blk = slice(j * FF_BLK + k * FF_SUB, j * FF_BLK + (k + 1) * FF_SUB)
        hid = jnp.maximum(jnp.dot(u2_ref[...], wff1_ref[:, blk], preferred_element_type=F32), 0.0)
        hid_ref[:, blk] = (hid * hid).astype(BF16)

    def mlp_down(j, acc):
        blk = slice(j * FF_BLK, (j + 1) * FF_BLK)
        part = jnp.dot(hid_ref[:, blk], wff2_ref[blk, :], preferred_element_type=F32)
        return part if acc is None else acc + part

    def ln2_rows(i):
        r = slice(i * SSD_CHUNK, (i + 1) * SSD_CHUNK)
        gain, bias = _ln_params(small_ref, 2)
        out_ref[0, r, :] = _ln_hat(r2_ref[r, :]) * gain + bias

    def step(mixer, mlp, norm):
        state, acc = None, [None]
        pieces = []
        if mlp:
            def down(j):
                acc[0] = mlp_down(j, acc[0])
            for j in range(N_SLAB):
                pieces += [(1, functools.partial(mlp_up, j, k)) for k in range(FF_BLK // FF_SUB)]
            for j in range(N_SLAB - DOWN_KEPT):
                at = (j + 1) * (FF_BLK // FF_SUB) + j + DOWN_LAG
                pieces.insert(at, (FF_BLK // FF_SUB, functools.partial(down, j)))
        n_slots = NCH * FILL_SLOTS
        per_slot = sum(cost for cost, _ in pieces) / n_slots
        slots, issued = [], 0.0
        for i in range(n_slots):
            mine = []
            while pieces and issued < (i + 1) * per_slot:
                cost, fn = pieces.pop(0)
                issued += cost
                mine.append(fn)
            slots.append(lambda mine=mine: [fn() for fn in mine])
        if mixer:
            @pl.when(jnp.minimum(s, N_TILES - 1) % NT == 0)
            def _():
                st_ref[...] = h0_ref[0, 0]
            state = [st_ref[g] for g in range(SSD_GROUPS)]
        for c in range(NCH):
            fill = slots[c * FILL_SLOTS:(c + 1) * FILL_SLOTS]
            if mixer:
                state = ssd_chunk(c, state, fill)
            else:
                for f in fill:
                    f()
            if norm:
                ln2_rows(c)
        if mixer:
            for g in range(SSD_GROUPS):
                st_ref[g] = state[g]
            mix = jnp.dot(yn_ref[...], wout_ref[...], preferred_element_type=F32)
            m = _mod_vectors(mod_ref, jnp.minimum(s, N_TILES - 1) // NT)
            g2_ref[wr, 0:1, :] = m[5]
        for i in range(DOWN_KEPT):
            if mlp:
                down(N_SLAB - DOWN_KEPT + i)
            if mixer:
                for j in range(i * NCH // DOWN_KEPT, (i + 1) * NCH // DOWN_KEPT):
                    r = slice(j * SSD_CHUNK, (j + 1) * SSD_CHUNK)
                    gain, bias = _ln_params(small_ref, 1)
                    h1 = _ln_hat(ALPHA * hln_ref[0, r, :] + m[2] * mix[r]) * gain + bias
                    h1_ref[wr, r, :] = h1
                    u2_ref[r, :] = (h1 * (1.0 + m[4]) + m[3]).astype(BF16)
        if mlp:
            r2_ref[...] = ALPHA * h1_ref[rd] + g2_ref[rd, 0:1, :] * acc[0]

    mlp_weights = (pltpu.make_async_copy(wff1_hbm, wff1_ref, wsem.at[0]),
                   pltpu.make_async_copy(wff2_hbm, wff2_ref, wsem.at[1]))

    @pl.when(s == 0)
    def _():
        for copy in mlp_weights:
            copy.start()
        r2_ref[...] = jnp.zeros((TM, D_MODEL), F32)
        step(True, False, False)

    @pl.when(s == 1)
    def _():
        for copy in mlp_weights:
            copy.wait()

    @pl.when(jnp.logical_and(s >= 1, s < N_TILES))
    def _():
        step(True, True, True)

    @pl.when(s == N_TILES)
    def _():
        step(False, True, True)

    @pl.when(s == N_TILES + 1)
    def _():
        step(False, False, True)


def _out_call(hln, mod, ycv, zg, xs, bt, cm, rows, cols, colf, sb, h0, exw, small,
              wout, wff1, wff2):
    cur = lambda s: jnp.minimum(s, N_TILES - 1)
    fwd = lambda s: (cur(s) // NT, cur(s) % NT, 0)
    fwd4 = lambda s: (cur(s) // NT, cur(s) % NT, 0, 0)
    fwd5 = lambda s: (cur(s) // NT, cur(s) % NT, 0, 0, 0)
    done = lambda s: (jnp.maximum(s - 2, 0) // NT, jnp.maximum(s - 2, 0) % NT, 0)
    tok = lambda width: pl.BlockSpec((1, TM, width), fwd)
    in_specs = [
        tok(D_MODEL),
        _const_spec((MOD_ROWS, 6 * D_MODEL)),
        tok(CONV_WIDTH), tok(SSD_WIDTH), tok(SSD_WIDTH),
        pl.BlockSpec((1, NCH, SSD_GROUPS, SSD_STATE, SSD_CHUNK), fwd5),
        tok(SSD_GN),
        pl.BlockSpec((1, NCH, 2 * N_DH, SSD_CHUNK), fwd4),
        pl.BlockSpec((1, NCH, SSD_CHUNK, LANES), fwd4),
        pl.BlockSpec((1, NCH, SSD_CHUNK, LANES), fwd4),
        pl.BlockSpec((1, NCH, SSD_GROUPS, SSD_STATE, GROUP_COLS), fwd5),
        pl.BlockSpec((1, 1, SSD_GROUPS, SSD_STATE, GROUP_COLS), lambda s: (cur(s) // NT, 0, 0, 0, 0)),
        _const_spec((LANES, 2 * SSD_WIDTH)),
        _const_spec((SMALL_ROWS, D_MODEL)),
        _const_spec((D_MODEL, D_MODEL)),
        pl.BlockSpec(memory_space=pl.ANY), pl.BlockSpec(memory_space=pl.ANY),
    ]
    return pl.pallas_call(
        _out_kernel,
        grid=(N_TILES + 2,),
        in_specs=in_specs,
        out_specs=pl.BlockSpec((1, TM, D_MODEL), done),
        out_shape=jax.ShapeDtypeStruct((BATCH, SEQ, D_MODEL), F32),
        scratch_shapes=[pltpu.VMEM((SSD_GROUPS, SSD_STATE, GROUP_COLS), F32),
                        pltpu.VMEM((TM, D_MODEL), BF16),
                        pltpu.VMEM((2, TM, D_MODEL), F32),
                        pltpu.VMEM((TM, D_MODEL), BF16),
                        pltpu.VMEM((2, G2_ROWS, D_MODEL), F32),
                        pltpu.VMEM((TM, D_MODEL), F32),
                        pltpu.VMEM((TM, D_FF), BF16),
                        pltpu.VMEM((D_MODEL, D_FF), BF16),
                        pltpu.VMEM((D_FF, D_MODEL), BF16),
                        pltpu.SemaphoreType.DMA((2,))],
        compiler_params=pltpu.CompilerParams(dimension_semantics=("arbitrary",),
                                             vmem_limit_bytes=VMEM_LIMIT),
        name="out",
    )(hln, mod, ycv, zg, xs, bt, cm, rows, cols, colf, sb, h0, exw, small,
      wout, wff1, wff2)


def kernel(x, c, ctx, c_ctx, ln_in_g, ln_in_b, w_mod, b_mod, w_in, conv_w, ssd_conv_w, ssd_conv_b,
           dt_bias, a_log, ssd_d, ssd_norm_w, w_out, ln1_g, ln1_b, w_ff1, w_ff2, ln2_g, ln2_b):
    flat = lambda v: v.reshape(1, -1)
    mod, w_in_p, small = _prep_call(
        c, flat(c_ctx), w_mod, b_mod, w_in[0].T,
        (flat(ssd_conv_w), ssd_conv_b, flat(conv_w), dt_bias, a_log, ssd_d, ssd_norm_w,
         flat(ln_in_g), flat(ln_in_b), ln1_g, ln1_b, ln2_g, ln2_b))
    exe = jnp.asarray(_EXE, BF16)
    exw = jnp.asarray(_EXW, BF16)

    h0 = _ctx_call(ctx, mod, w_in_p, small, exw, jnp.asarray(_U_CTX, BF16))
    hln, ycv, zg, xs, bt, cm, rows, cols, colf, sb, w_out_p, w_ff1_p, w_ff2_p = _proj_call(
        x, mod, w_in_p, small, exe, exw,
        jnp.asarray(_U_CHUNK, BF16), h0, w_out, w_ff1, w_ff2)
    return _out_call(hln, mod, ycv, zg, xs, bt, cm, rows, cols, colf, sb, h0, exw,
                     small, w_out_p, w_ff1_p, w_ff2_p)
```
